```python
import jax, jax.numpy as jnp
from jax import lax
import numpy as np

D_MODEL = 1024
BATCH = 16
SEQ = 2048
DEPTH = 2

N_A_LAYERS = DEPTH // 2
N_B_LAYERS = DEPTH - N_A_LAYERS
D_FF = 2816
CHUNK = 128
GMLP_HALF = 2 * D_MODEL
GMLP_GROUPS = 16
GMLP_GROUP_DIM = GMLP_HALF // GMLP_GROUPS
N_HEADS = 8
QK_NOPE = 128
QK_ROPE = 64
V_DIM = 128
KV_RANK = 256
Q_RANK = 512
Q_BLOCK = 128
ROPE_THETA = 10000.0
RMS_EPS = 1e-6
LN_EPS = 1e-5
NEG_INF = -1e30

kernel_name = "yoco_gmlp_mla_macaron_sandwich"


def rms_norm(x, g):
    x32 = x.astype(jnp.float32)
    y = x32 * lax.rsqrt(jnp.mean(x32 * x32, axis=-1, keepdims=True) + RMS_EPS) * g.astype(jnp.float32)
    return y.astype(x.dtype)


def layer_norm(x, g, b):
    x32 = x.astype(jnp.float32)
    mu = jnp.mean(x32, axis=-1, keepdims=True)
    xc = x32 - mu
    y = xc * lax.rsqrt(jnp.mean(xc * xc, axis=-1, keepdims=True) + LN_EPS)
    return (y * g.astype(jnp.float32) + b.astype(jnp.float32)).astype(x.dtype)


def swiglu(n, w_gate, w_up, w_down):
    return (jax.nn.silu(n @ w_gate) * (n @ w_up)) @ w_down


def rope_tables(positions):
    inv_freq = ROPE_THETA ** (-jnp.arange(0, QK_ROPE, 2, dtype=jnp.float32) / QK_ROPE)
    ang = positions.astype(jnp.float32)[..., None] * inv_freq
    return jnp.cos(ang), jnp.sin(ang)


def apply_rope(x, cos, sin):
    cos = cos.astype(x.dtype)
    sin = sin.astype(x.dtype)
    x1, x2 = jnp.split(x, 2, axis=-1)
    return jnp.concatenate([x1 * cos - x2 * sin, x2 * cos + x1 * sin], axis=-1)


def gmlp_mixer(n, w_in, ln_g, ln_b, w_s, b_s, w_out):
    b, s, _ = n.shape
    z = jax.nn.gelu(n @ w_in)
    u, v = jnp.split(z, 2, axis=-1)
    v = layer_norm(v, ln_g, ln_b)
    v = v.reshape(b, s // CHUNK, CHUNK, GMLP_GROUPS, GMLP_GROUP_DIM)
    causal = jnp.tril(jnp.ones((CHUNK, CHUNK), dtype=w_s.dtype))
    w = w_s * causal
    sv = jnp.einsum('gtc,bncgd->bntgd', w, v) + jnp.transpose(b_s)[None, None, :, :, None]
    return (u * sv.reshape(b, s, GMLP_HALF)) @ w_out


def shared_kv(h, kv_norm_g, w_dkv, kv_a_norm_g, w_ukv, cos, sin):
    b, s, _ = h.shape
    n = rms_norm(h, kv_norm_g)
    ckv = n @ w_dkv
    c, k_rope = ckv[..., :KV_RANK], ckv[..., KV_RANK:]
    c = rms_norm(c, kv_a_norm_g)
    k_rope = apply_rope(k_rope, cos, sin)
    kv = (c @ w_ukv).reshape(b, s, N_HEADS, QK_NOPE + V_DIM)
    return kv[..., :QK_NOPE], k_rope, kv[..., QK_NOPE:]


def mla_mixer(n, k_nope, k_rope, v, w_dq, q_norm_g, w_uq, w_o, cos, sin):
    b, s, _ = n.shape
    q = (rms_norm(n @ w_dq, q_norm_g) @ w_uq).reshape(b, s, N_HEADS, QK_NOPE + QK_ROPE)
    q_nope = q[..., :QK_NOPE]
    q_rope = apply_rope(q[..., QK_NOPE:], cos[:, :, None, :], sin[:, :, None, :])
    scale = (QK_NOPE + QK_ROPE) ** -0.5
    outs = []
    for blk in range(s // Q_BLOCK):
        q0, q1 = blk * Q_BLOCK, (blk + 1) * Q_BLOCK
        sc = (jnp.einsum('bqhd,bkhd->bhqk', q_nope[:, q0:q1], k_nope[:, :q1])
              + jnp.einsum('bqhr,bkr->bhqk', q_rope[:, q0:q1], k_rope[:, :q1]))
        sc = sc.astype(jnp.float32) * scale
        q_idx = q0 + jnp.arange(Q_BLOCK)
        k_idx = jnp.arange(q1)
        sc = jnp.where(k_idx[None, :] <= q_idx[:, None], sc, NEG_INF)
        p = jax.nn.softmax(sc, axis=-1).astype(v.dtype)
        outs.append(jnp.einsum('bhqk,bkhd->bqhd', p, v[:, :q1]))
    o = jnp.concatenate(outs, axis=1).reshape(b, s, N_HEADS * V_DIM)
    return o @ w_o


def _fwd_setup_inputs(seed: int = 0) -> dict:
    key = jax.random.key(seed)
    ks = iter(jax.random.split(key, 40))

    def w(shape, fan_in):
        return jax.random.normal(next(ks), shape, jnp.float32) * (fan_in ** -0.5)

    def g(shape):
        return 1.0 + 0.02 * jax.random.normal(next(ks), shape, jnp.float32)

    x = jax.random.normal(next(ks), (BATCH, SEQ, D_MODEL), jnp.float32)
    offs = jax.random.randint(next(ks), (BATCH, 1), 0, 1024, dtype=jnp.int32)
    positions = (jnp.arange(SEQ, dtype=jnp.int32)[None, :] + offs).astype(jnp.int32)
    return {
        "x": x,
        "positions": positions,
        "ffn_pre_g": g((DEPTH, 2, D_MODEL)),
        "ffn_post_g": g((DEPTH, 2, D_MODEL)),
        "ffn_w_gate": w((DEPTH, 2, D_MODEL, D_FF), D_MODEL),
        "ffn_w_up": w((DEPTH, 2, D_MODEL, D_FF), D_MODEL),
        "ffn_w_down": w((DEPTH, 2, D_FF, D_MODEL), D_FF),
        "mix_pre_g": g((DEPTH, D_MODEL)),
        "mix_post_g": g((DEPTH, D_MODEL)),
        "gmlp_w_in": w((N_A_LAYERS, D_MODEL, 2 * GMLP_HALF), D_MODEL),
        "gmlp_ln_g": g((N_A_LAYERS, GMLP_HALF)),
        "gmlp_ln_b": 0.02 * jax.random.normal(next(ks), (N_A_LAYERS, GMLP_HALF), jnp.float32),
        "gmlp_w_s": w((N_A_LAYERS, GMLP_GROUPS, CHUNK, CHUNK), CHUNK),
        "gmlp_b_s": g((N_A_LAYERS, GMLP_GROUPS, CHUNK)),
        "gmlp_w_out": w((N_A_LAYERS, GMLP_HALF, D_MODEL), GMLP_HALF),
        "kv_norm_g": g((D_MODEL,)),
        "w_dkv": w((D_MODEL, KV_RANK + QK_ROPE), D_MODEL),
        "kv_a_norm_g": g((KV_RANK,)),
        "w_ukv": w((KV_RANK, N_HEADS * (QK_NOPE + V_DIM)), KV_RANK),
        "mla_w_dq": w((N_B_LAYERS, D_MODEL, Q_RANK), D_MODEL),
        "mla_q_norm_g": g((N_B_LAYERS, Q_RANK)),
        "mla_w_uq": w((N_B_LAYERS, Q_RANK, N_HEADS * (QK_NOPE + QK_ROPE)), Q_RANK),
        "mla_w_o": w((N_B_LAYERS, N_HEADS * V_DIM, D_MODEL), N_HEADS * V_DIM),
    }


def _fwd_reference(x, positions, ffn_pre_g, ffn_post_g, ffn_w_gate, ffn_w_up, ffn_w_down,
              mix_pre_g, mix_post_g, gmlp_w_in, gmlp_ln_g, gmlp_ln_b, gmlp_w_s, gmlp_b_s,
              gmlp_w_out, kv_norm_g, w_dkv, kv_a_norm_g, w_ukv, mla_w_dq, mla_q_norm_g,
              mla_w_uq, mla_w_o):
    cos, sin = rope_tables(positions)
    h = x
    k_nope = k_rope = v = None
    for layer in range(DEPTH):
        f = swiglu(rms_norm(h, ffn_pre_g[layer, 0]), ffn_w_gate[layer, 0], ffn_w_up[layer, 0], ffn_w_down[layer, 0])
        h = h + 0.5 * rms_norm(f, ffn_post_g[layer, 0])
        n = rms_norm(h, mix_pre_g[layer])
        if layer < N_A_LAYERS:
            m = gmlp_mixer(n, gmlp_w_in[layer], gmlp_ln_g[layer], gmlp_ln_b[layer],
                           gmlp_w_s[layer], gmlp_b_s[layer], gmlp_w_out[layer])
        else:
            j = layer - N_A_LAYERS
            m = mla_mixer(n, k_nope, k_rope, v, mla_w_dq[j], mla_q_norm_g[j], mla_w_uq[j], mla_w_o[j], cos, sin)
        h = h + rms_norm(m, mix_post_g[layer])
        f = swiglu(rms_norm(h, ffn_pre_g[layer, 1]), ffn_w_gate[layer, 1], ffn_w_up[layer, 1], ffn_w_down[layer, 1])
        h = h + 0.5 * rms_norm(f, ffn_post_g[layer, 1])
        if layer == N_A_LAYERS - 1:
            k_nope, k_rope, v = shared_kv(h, kv_norm_g, w_dkv, kv_a_norm_g, w_ukv, cos, sin)
    return h


import jax as _jax
import jax.numpy as _jnp

TWIN_FORMAT = 'train_step'
FWD_PARAMS = ['x', 'positions', 'ffn_pre_g', 'ffn_post_g', 'ffn_w_gate', 'ffn_w_up', 'ffn_w_down', 'mix_pre_g', 'mix_post_g', 'gmlp_w_in', 'gmlp_ln_g', 'gmlp_ln_b', 'gmlp_w_s', 'gmlp_b_s', 'gmlp_w_out', 'kv_norm_g', 'w_dkv', 'kv_a_norm_g', 'w_ukv', 'mla_w_dq', 'mla_q_norm_g', 'mla_w_uq', 'mla_w_o']
TWIN_WEIGHTS = ['ffn_pre_g', 'ffn_post_g', 'ffn_w_gate', 'ffn_w_up', 'ffn_w_down', 'mix_pre_g', 'mix_post_g', 'gmlp_w_in', 'gmlp_ln_g', 'gmlp_ln_b', 'gmlp_w_s', 'gmlp_b_s', 'gmlp_w_out', 'kv_norm_g', 'w_dkv', 'kv_a_norm_g', 'w_ukv', 'mla_w_dq', 'mla_q_norm_g', 'mla_w_uq', 'mla_w_o']
TWIN_DIFF_INPUT = 'x'
TWIN_INPUTS = ['x', 'positions', 'ffn_pre_g', 'ffn_post_g', 'ffn_w_gate', 'ffn_w_up', 'ffn_w_down', 'mix_pre_g', 'mix_post_g', 'gmlp_w_in', 'gmlp_ln_g', 'gmlp_ln_b', 'gmlp_w_s', 'gmlp_b_s', 'gmlp_w_out', 'kv_norm_g', 'w_dkv', 'kv_a_norm_g', 'w_ukv', 'mla_w_dq', 'mla_q_norm_g', 'mla_w_uq', 'mla_w_o', 'loss_target', 'm_ffn_pre_g', 'm_ffn_post_g', 'm_ffn_w_gate', 'm_ffn_w_up', 'm_ffn_w_down', 'm_mix_pre_g', 'm_mix_post_g', 'm_gmlp_w_in', 'm_gmlp_ln_g', 'm_gmlp_ln_b', 'm_gmlp_w_s', 'm_gmlp_b_s', 'm_gmlp_w_out', 'm_kv_norm_g', 'm_w_dkv', 'm_kv_a_norm_g', 'm_w_ukv', 'm_mla_w_dq', 'm_mla_q_norm_g', 'm_mla_w_uq', 'm_mla_w_o', 'v_ffn_pre_g', 'v_ffn_post_g', 'v_ffn_w_gate', 'v_ffn_w_up', 'v_ffn_w_down', 'v_mix_pre_g', 'v_mix_post_g', 'v_gmlp_w_in', 'v_gmlp_ln_g', 'v_gmlp_ln_b', 'v_gmlp_w_s', 'v_gmlp_b_s', 'v_gmlp_w_out', 'v_kv_norm_g', 'v_w_dkv', 'v_kv_a_norm_g', 'v_w_ukv', 'v_mla_w_dq', 'v_mla_q_norm_g', 'v_mla_w_uq', 'v_mla_w_o']
TWIN_OUTPUTS = ['loss', 'grad_x', 'grad_ffn_pre_g', 'grad_ffn_post_g', 'grad_ffn_w_gate', 'grad_ffn_w_up', 'grad_ffn_w_down', 'grad_mix_pre_g', 'grad_mix_post_g', 'grad_gmlp_w_in', 'grad_gmlp_ln_g', 'grad_gmlp_ln_b', 'grad_gmlp_w_s', 'grad_gmlp_b_s', 'grad_gmlp_w_out', 'grad_kv_norm_g', 'grad_w_dkv', 'grad_kv_a_norm_g', 'grad_w_ukv', 'grad_mla_w_dq', 'grad_mla_q_norm_g', 'grad_mla_w_uq', 'grad_mla_w_o', 'delta_ffn_pre_g', 'delta_ffn_post_g', 'delta_ffn_w_gate', 'delta_ffn_w_up', 'delta_ffn_w_down', 'delta_mix_pre_g', 'delta_mix_post_g', 'delta_gmlp_w_in', 'delta_gmlp_ln_g', 'delta_gmlp_ln_b', 'delta_gmlp_w_s', 'delta_gmlp_b_s', 'delta_gmlp_w_out', 'delta_kv_norm_g', 'delta_w_dkv', 'delta_kv_a_norm_g', 'delta_w_ukv', 'delta_mla_w_dq', 'delta_mla_q_norm_g', 'delta_mla_w_uq', 'delta_mla_w_o', 'new_m_ffn_pre_g', 'new_m_ffn_post_g', 'new_m_ffn_w_gate', 'new_m_ffn_w_up', 'new_m_ffn_w_down', 'new_m_mix_pre_g', 'new_m_mix_post_g', 'new_m_gmlp_w_in', 'new_m_gmlp_ln_g', 'new_m_gmlp_ln_b', 'new_m_gmlp_w_s', 'new_m_gmlp_b_s', 'new_m_gmlp_w_out', 'new_m_kv_norm_g', 'new_m_w_dkv', 'new_m_kv_a_norm_g', 'new_m_w_ukv', 'new_m_mla_w_dq', 'new_m_mla_q_norm_g', 'new_m_mla_w_uq', 'new_m_mla_w_o', 'new_v_ffn_pre_g', 'new_v_ffn_post_g', 'new_v_ffn_w_gate', 'new_v_ffn_w_up', 'new_v_ffn_w_down', 'new_v_mix_pre_g', 'new_v_mix_post_g', 'new_v_gmlp_w_in', 'new_v_gmlp_ln_g', 'new_v_gmlp_ln_b', 'new_v_gmlp_w_s', 'new_v_gmlp_b_s', 'new_v_gmlp_w_out', 'new_v_kv_norm_g', 'new_v_w_dkv', 'new_v_kv_a_norm_g', 'new_v_w_ukv', 'new_v_mla_w_dq', 'new_v_mla_q_norm_g', 'new_v_mla_w_uq', 'new_v_mla_w_o']
TWIN_LEAF_KINDS = {'loss': 'loss', 'grad_x': 'grad_x', 'grad_ffn_pre_g': 'grad_w', 'grad_ffn_post_g': 'grad_w', 'grad_ffn_w_gate': 'grad_w', 'grad_ffn_w_up': 'grad_w', 'grad_ffn_w_down': 'grad_w', 'grad_mix_pre_g': 'grad_w', 'grad_mix_post_g': 'grad_w', 'grad_gmlp_w_in': 'grad_w', 'grad_gmlp_ln_g': 'grad_w', 'grad_gmlp_ln_b': 'grad_w', 'grad_gmlp_w_s': 'grad_w', 'grad_gmlp_b_s': 'grad_w', 'grad_gmlp_w_out': 'grad_w', 'grad_kv_norm_g': 'grad_w', 'grad_w_dkv': 'grad_w', 'grad_kv_a_norm_g': 'grad_w', 'grad_w_ukv': 'grad_w', 'grad_mla_w_dq': 'grad_w', 'grad_mla_q_norm_g': 'grad_w', 'grad_mla_w_uq': 'grad_w', 'grad_mla_w_o': 'grad_w', 'delta_ffn_pre_g': 'delta_w', 'delta_ffn_post_g': 'delta_w', 'delta_ffn_w_gate': 'delta_w', 'delta_ffn_w_up': 'delta_w', 'delta_ffn_w_down': 'delta_w', 'delta_mix_pre_g': 'delta_w', 'delta_mix_post_g': 'delta_w', 'delta_gmlp_w_in': 'delta_w', 'delta_gmlp_ln_g': 'delta_w', 'delta_gmlp_ln_b': 'delta_w', 'delta_gmlp_w_s': 'delta_w', 'delta_gmlp_b_s': 'delta_w', 'delta_gmlp_w_out': 'delta_w', 'delta_kv_norm_g': 'delta_w', 'delta_w_dkv': 'delta_w', 'delta_kv_a_norm_g': 'delta_w', 'delta_w_ukv': 'delta_w', 'delta_mla_w_dq': 'delta_w', 'delta_mla_q_norm_g': 'delta_w', 'delta_mla_w_uq': 'delta_w', 'delta_mla_w_o': 'delta_w', 'new_m_ffn_pre_g': 'new_m', 'new_m_ffn_post_g': 'new_m', 'new_m_ffn_w_gate': 'new_m', 'new_m_ffn_w_up': 'new_m', 'new_m_ffn_w_down': 'new_m', 'new_m_mix_pre_g': 'new_m', 'new_m_mix_post_g': 'new_m', 'new_m_gmlp_w_in': 'new_m', 'new_m_gmlp_ln_g': 'new_m', 'new_m_gmlp_ln_b': 'new_m', 'new_m_gmlp_w_s': 'new_m', 'new_m_gmlp_b_s': 'new_m', 'new_m_gmlp_w_out': 'new_m', 'new_m_kv_norm_g': 'new_m', 'new_m_w_dkv': 'new_m', 'new_m_kv_a_norm_g': 'new_m', 'new_m_w_ukv': 'new_m', 'new_m_mla_w_dq': 'new_m', 'new_m_mla_q_norm_g': 'new_m', 'new_m_mla_w_uq': 'new_m', 'new_m_mla_w_o': 'new_m', 'new_v_ffn_pre_g': 'new_v', 'new_v_ffn_post_g': 'new_v', 'new_v_ffn_w_gate': 'new_v', 'new_v_ffn_w_up': 'new_v', 'new_v_ffn_w_down': 'new_v', 'new_v_mix_pre_g': 'new_v', 'new_v_mix_post_g': 'new_v', 'new_v_gmlp_w_in': 'new_v', 'new_v_gmlp_ln_g': 'new_v', 'new_v_gmlp_ln_b': 'new_v', 'new_v_gmlp_w_s': 'new_v', 'new_v_gmlp_b_s': 'new_v', 'new_v_gmlp_w_out': 'new_v', 'new_v_kv_norm_g': 'new_v', 'new_v_w_dkv': 'new_v', 'new_v_kv_a_norm_g': 'new_v', 'new_v_w_ukv': 'new_v', 'new_v_mla_w_dq': 'new_v', 'new_v_mla_q_norm_g': 'new_v', 'new_v_mla_w_uq': 'new_v', 'new_v_mla_w_o': 'new_v'}


def _forward(args):
    return _fwd_reference(*[args[k] for k in FWD_PARAMS])


def _output_shape():
    out = _jax.eval_shape(lambda: _forward(_fwd_setup_inputs(0)))
    return out.shape, out.dtype

N_MICROBATCH = 1
ADAM_LR = 0.001
ADAM_B1 = 0.9
ADAM_B2 = 0.999
ADAM_EPS = 1e-08
ADAM_WD = 0.01
ADAM_STEP = 10
PER_EXAMPLE_BATCH_AXIS = {'x': 0, 'positions': 0, 'loss_target': 0}
SHARED_INPUTS = []
_WEIGHT_DTYPES = {'ffn_pre_g': _jnp.float32, 'ffn_post_g': _jnp.float32, 'ffn_w_gate': _jnp.float32, 'ffn_w_up': _jnp.float32, 'ffn_w_down': _jnp.float32, 'mix_pre_g': _jnp.float32, 'mix_post_g': _jnp.float32, 'gmlp_w_in': _jnp.float32, 'gmlp_ln_g': _jnp.float32, 'gmlp_ln_b': _jnp.float32, 'gmlp_w_s': _jnp.float32, 'gmlp_b_s': _jnp.float32, 'gmlp_w_out': _jnp.float32, 'kv_norm_g': _jnp.float32, 'w_dkv': _jnp.float32, 'kv_a_norm_g': _jnp.float32, 'w_ukv': _jnp.float32, 'mla_w_dq': _jnp.float32, 'mla_q_norm_g': _jnp.float32, 'mla_w_uq': _jnp.float32, 'mla_w_o': _jnp.float32}
MOMENT_SCALE = {'ffn_pre_g': 3.650294e+00, 'ffn_post_g': 8.481458e+00, 'ffn_w_gate': 1.285868e+00, 'ffn_w_up': 1.672495e+00, 'ffn_w_down': 2.873710e+00, 'mix_pre_g': 8.697536e-01, 'mix_post_g': 4.118716e+01, 'gmlp_w_in': 5.247190e-01, 'gmlp_ln_g': 2.422437e-01, 'gmlp_ln_b': 3.645942e-01, 'gmlp_w_s': 2.411857e-01, 'gmlp_b_s': 6.633844e-01, 'gmlp_w_out': 1.525358e+01, 'kv_norm_g': 1.080038e+01, 'w_dkv': 1.979145e+01, 'kv_a_norm_g': 2.228222e+01, 'w_ukv': 8.072620e+00, 'mla_w_dq': 7.165967e-01, 'mla_q_norm_g': 7.808551e-01, 'mla_w_uq': 4.339716e-01, 'mla_w_o': 1.146771e+01}


def _to_microbatches(a, axis):
    t = _jnp.moveaxis(a, axis, 0)
    t = t.reshape((N_MICROBATCH, t.shape[0] // N_MICROBATCH) + t.shape[1:])
    return _jnp.moveaxis(t, 1, axis + 1)


def setup_inputs(seed: int = 0) -> dict:
    inp = _fwd_setup_inputs(seed)
    key = _jax.random.fold_in(_jax.random.key(seed), 7919)
    shape, _ = _output_shape()
    out = dict(inp)
    out["loss_target"] = _jax.random.normal(_jax.random.fold_in(key, 0), shape, _jnp.float32)
    for i, name in enumerate(TWIN_WEIGHTS):
        w = inp[name].astype(_jnp.float32)
        if MOMENT_SCALE is None:
            s = _jnp.sqrt(_jnp.mean(_jnp.square(w)) + 1e-30)
        else:
            s = MOMENT_SCALE[name]
        km, kv = _jax.random.split(_jax.random.fold_in(key, i + 1))
        out[name] = w
        out["m_" + name] = s * _jax.random.normal(km, w.shape, _jnp.float32)
        out["v_" + name] = (s * s) * _jax.random.uniform(kv, w.shape, _jnp.float32, 0.5, 1.5)
    if N_MICROBATCH > 1:
        for name, axis in PER_EXAMPLE_BATCH_AXIS.items():
            out[name] = _to_microbatches(out[name], axis)
    return {'x': out['x'], 'positions': out['positions'], 'ffn_pre_g': out['ffn_pre_g'], 'ffn_post_g': out['ffn_post_g'], 'ffn_w_gate': out['ffn_w_gate'], 'ffn_w_up': out['ffn_w_up'], 'ffn_w_down': out['ffn_w_down'], 'mix_pre_g': out['mix_pre_g'], 'mix_post_g': out['mix_post_g'], 'gmlp_w_in': out['gmlp_w_in'], 'gmlp_ln_g': out['gmlp_ln_g'], 'gmlp_ln_b': out['gmlp_ln_b'], 'gmlp_w_s': out['gmlp_w_s'], 'gmlp_b_s': out['gmlp_b_s'], 'gmlp_w_out': out['gmlp_w_out'], 'kv_norm_g': out['kv_norm_g'], 'w_dkv': out['w_dkv'], 'kv_a_norm_g': out['kv_a_norm_g'], 'w_ukv': out['w_ukv'], 'mla_w_dq': out['mla_w_dq'], 'mla_q_norm_g': out['mla_q_norm_g'], 'mla_w_uq': out['mla_w_uq'], 'mla_w_o': out['mla_w_o'], 'loss_target': out['loss_target'], 'm_ffn_pre_g': out['m_ffn_pre_g'], 'm_ffn_post_g': out['m_ffn_post_g'], 'm_ffn_w_gate': out['m_ffn_w_gate'], 'm_ffn_w_up': out['m_ffn_w_up'], 'm_ffn_w_down': out['m_ffn_w_down'], 'm_mix_pre_g': out['m_mix_pre_g'], 'm_mix_post_g': out['m_mix_post_g'], 'm_gmlp_w_in': out['m_gmlp_w_in'], 'm_gmlp_ln_g': out['m_gmlp_ln_g'], 'm_gmlp_ln_b': out['m_gmlp_ln_b'], 'm_gmlp_w_s': out['m_gmlp_w_s'], 'm_gmlp_b_s': out['m_gmlp_b_s'], 'm_gmlp_w_out': out['m_gmlp_w_out'], 'm_kv_norm_g': out['m_kv_norm_g'], 'm_w_dkv': out['m_w_dkv'], 'm_kv_a_norm_g': out['m_kv_a_norm_g'], 'm_w_ukv': out['m_w_ukv'], 'm_mla_w_dq': out['m_mla_w_dq'], 'm_mla_q_norm_g': out['m_mla_q_norm_g'], 'm_mla_w_uq': out['m_mla_w_uq'], 'm_mla_w_o': out['m_mla_w_o'], 'v_ffn_pre_g': out['v_ffn_pre_g'], 'v_ffn_post_g': out['v_ffn_post_g'], 'v_ffn_w_gate': out['v_ffn_w_gate'], 'v_ffn_w_up': out['v_ffn_w_up'], 'v_ffn_w_down': out['v_ffn_w_down'], 'v_mix_pre_g': out['v_mix_pre_g'], 'v_mix_post_g': out['v_mix_post_g'], 'v_gmlp_w_in': out['v_gmlp_w_in'], 'v_gmlp_ln_g': out['v_gmlp_ln_g'], 'v_gmlp_ln_b': out['v_gmlp_ln_b'], 'v_gmlp_w_s': out['v_gmlp_w_s'], 'v_gmlp_b_s': out['v_gmlp_b_s'], 'v_gmlp_w_out': out['v_gmlp_w_out'], 'v_kv_norm_g': out['v_kv_norm_g'], 'v_w_dkv': out['v_w_dkv'], 'v_kv_a_norm_g': out['v_kv_a_norm_g'], 'v_w_ukv': out['v_w_ukv'], 'v_mla_w_dq': out['v_mla_w_dq'], 'v_mla_q_norm_g': out['v_mla_q_norm_g'], 'v_mla_w_uq': out['v_mla_w_uq'], 'v_mla_w_o': out['v_mla_w_o']}


def _loss(weights, diff, rest, loss_target):
    with _jax.named_scope("forward"):
        args = {**rest, TWIN_DIFF_INPUT: diff, **{k: w.astype(_WEIGHT_DTYPES[k]) for k, w in weights.items()}}
        y = _forward(args)
    with _jax.named_scope("loss_head"):
        err = _jnp.square(y.astype(_jnp.float32) - loss_target)
        return 0.5 * _jnp.sum(_jnp.mean(err, axis=-1)) if err.ndim else 0.5 * err


def _adamw(w, g, m, v):
    m = ADAM_B1 * m + (1.0 - ADAM_B1) * g
    v = ADAM_B2 * v + (1.0 - ADAM_B2) * _jnp.square(g)
    m_hat = m / (1.0 - ADAM_B1 ** ADAM_STEP)
    v_hat = v / (1.0 - ADAM_B2 ** ADAM_STEP)
    delta = -ADAM_LR * (m_hat / (_jnp.sqrt(v_hat) + ADAM_EPS) + ADAM_WD * w)
    return delta, m, v


def reference(x, positions, ffn_pre_g, ffn_post_g, ffn_w_gate, ffn_w_up, ffn_w_down, mix_pre_g, mix_post_g, gmlp_w_in, gmlp_ln_g, gmlp_ln_b, gmlp_w_s, gmlp_b_s, gmlp_w_out, kv_norm_g, w_dkv, kv_a_norm_g, w_ukv, mla_w_dq, mla_q_norm_g, mla_w_uq, mla_w_o, loss_target, m_ffn_pre_g, m_ffn_post_g, m_ffn_w_gate, m_ffn_w_up, m_ffn_w_down, m_mix_pre_g, m_mix_post_g, m_gmlp_w_in, m_gmlp_ln_g, m_gmlp_ln_b, m_gmlp_w_s, m_gmlp_b_s, m_gmlp_w_out, m_kv_norm_g, m_w_dkv, m_kv_a_norm_g, m_w_ukv, m_mla_w_dq, m_mla_q_norm_g, m_mla_w_uq, m_mla_w_o, v_ffn_pre_g, v_ffn_post_g, v_ffn_w_gate, v_ffn_w_up, v_ffn_w_down, v_mix_pre_g, v_mix_post_g, v_gmlp_w_in, v_gmlp_ln_g, v_gmlp_ln_b, v_gmlp_w_s, v_gmlp_b_s, v_gmlp_w_out, v_kv_norm_g, v_w_dkv, v_kv_a_norm_g, v_w_ukv, v_mla_w_dq, v_mla_q_norm_g, v_mla_w_uq, v_mla_w_o):
    given = dict(x=x, positions=positions, ffn_pre_g=ffn_pre_g, ffn_post_g=ffn_post_g, ffn_w_gate=ffn_w_gate, ffn_w_up=ffn_w_up, ffn_w_down=ffn_w_down, mix_pre_g=mix_pre_g, mix_post_g=mix_post_g, gmlp_w_in=gmlp_w_in, gmlp_ln_g=gmlp_ln_g, gmlp_ln_b=gmlp_ln_b, gmlp_w_s=gmlp_w_s, gmlp_b_s=gmlp_b_s, gmlp_w_out=gmlp_w_out, kv_norm_g=kv_norm_g, w_dkv=w_dkv, kv_a_norm_g=kv_a_norm_g, w_ukv=w_ukv, mla_w_dq=mla_w_dq, mla_q_norm_g=mla_q_norm_g, mla_w_uq=mla_w_uq, mla_w_o=mla_w_o, loss_target=loss_target, m_ffn_pre_g=m_ffn_pre_g, m_ffn_post_g=m_ffn_post_g, m_ffn_w_gate=m_ffn_w_gate, m_ffn_w_up=m_ffn_w_up, m_ffn_w_down=m_ffn_w_down, m_mix_pre_g=m_mix_pre_g, m_mix_post_g=m_mix_post_g, m_gmlp_w_in=m_gmlp_w_in, m_gmlp_ln_g=m_gmlp_ln_g, m_gmlp_ln_b=m_gmlp_ln_b, m_gmlp_w_s=m_gmlp_w_s, m_gmlp_b_s=m_gmlp_b_s, m_gmlp_w_out=m_gmlp_w_out, m_kv_norm_g=m_kv_norm_g, m_w_dkv=m_w_dkv, m_kv_a_norm_g=m_kv_a_norm_g, m_w_ukv=m_w_ukv, m_mla_w_dq=m_mla_w_dq, m_mla_q_norm_g=m_mla_q_norm_g, m_mla_w_uq=m_mla_w_uq, m_mla_w_o=m_mla_w_o, v_ffn_pre_g=v_ffn_pre_g, v_ffn_post_g=v_ffn_post_g, v_ffn_w_gate=v_ffn_w_gate, v_ffn_w_up=v_ffn_w_up, v_ffn_w_down=v_ffn_w_down, v_mix_pre_g=v_mix_pre_g, v_mix_post_g=v_mix_post_g, v_gmlp_w_in=v_gmlp_w_in, v_gmlp_ln_g=v_gmlp_ln_g, v_gmlp_ln_b=v_gmlp_ln_b, v_gmlp_w_s=v_gmlp_w_s, v_gmlp_b_s=v_gmlp_b_s, v_gmlp_w_out=v_gmlp_w_out, v_kv_norm_g=v_kv_norm_g, v_w_dkv=v_w_dkv, v_kv_a_norm_g=v_kv_a_norm_g, v_w_ukv=v_w_ukv, v_mla_w_dq=v_mla_w_dq, v_mla_q_norm_g=v_mla_q_norm_g, v_mla_w_uq=v_mla_w_uq, v_mla_w_o=v_mla_w_o)
    weights = {n: given[n] for n in TWIN_WEIGHTS}
    shared = {n: given[n] for n in SHARED_INPUTS}
    per_example = {n: given[n] for n in ['x', 'positions']}
    grad_fn = _jax.value_and_grad(_loss, argnums=(0, 1))

    def one_microbatch(ex, loss_target):
        ex = dict(ex)
        diff = ex.pop(TWIN_DIFF_INPUT)
        return grad_fn(weights, diff, {**shared, **ex}, loss_target)

    if N_MICROBATCH == 1:
        loss, (grad_w, grad_x) = one_microbatch(per_example, given["loss_target"])
    else:
        def body(carry, xs):
            loss_sum, grad_sum = carry
            l_k, (gw_k, gx_k) = one_microbatch(xs[0], xs[1])
            with _jax.named_scope("update"):
                return (loss_sum + l_k, _jax.tree.map(_jnp.add, grad_sum, gw_k)), gx_k

        init = (_jnp.zeros((), _jnp.float32), _jax.tree.map(_jnp.zeros_like, weights))
        (loss, grad_w), grad_x = _jax.lax.scan(body, init, (per_example, given["loss_target"]))
    with _jax.named_scope("update"):
        delta_w, new_m, new_v = {}, {}, {}
        for n in TWIN_WEIGHTS:
            delta_w[n], new_m[n], new_v[n] = _adamw(weights[n], grad_w[n], given["m_" + n], given["v_" + n])
    return (loss, grad_x, *[grad_w[n] for n in TWIN_WEIGHTS], *[delta_w[n] for n in TWIN_WEIGHTS],
            *[new_m[n] for n in TWIN_WEIGHTS], *[new_v[n] for n in TWIN_WEIGHTS])
```

```python
import math

import jax
import jax.numpy as jnp
from jax import lax
from jax.experimental import pallas as pl
from jax.experimental.pallas import tpu as pltpu

F32, BF16 = jnp.float32, jnp.bfloat16

RMS_EPS, LN_EPS, NEG_INF = 1e-6, 1e-5, -1e30
N_HEADS, QK_NOPE, QK_ROPE, V_DIM, KV_RANK = 8, 128, 64, 128, 256
CHUNK, GROUPS = 128, 16
ROPE_THETA = 10000.0
ADAM_LR, ADAM_B1, ADAM_B2, ADAM_EPS, ADAM_WD, ADAM_STEP = 0.001, 0.9, 0.999, 1e-08, 0.01, 10
N_SHARDS = 4

VMEM_LIMIT_BYTES = 48 * 1024 * 1024
ROW_TILE = 512
PACK_WIDTH = 1024

_DN = {"nn": (((1,), (0,)), ((), ())), "nt": (((1,), (1,)), ((), ())), "tn": (((0,), (0,)), ((), ()))}
_MESH = pl.DeviceIdType.MESH
_ANY = pl.BlockSpec(memory_space=pl.ANY)


def _params(sem):
    return pltpu.CompilerParams(dimension_semantics=sem, vmem_limit_bytes=VMEM_LIMIT_BYTES)


def _mm(name, grid, ins, pairs, acc_shapes, outs, epilogue, extras=()):
    n_in, n_ex, n_out = len(ins), len(extras), len(outs)
    gk = grid[2]

    def body(*refs):
        in_refs, ex_refs = refs[:n_in], refs[n_in:n_in + n_ex]
        out_refs = refs[n_in + n_ex:n_in + n_ex + n_out]
        acc_refs = refs[n_in + n_ex + n_out:]
        parts = [None] * len(acc_shapes)
        for a, b, c, dims in pairs:
            p = lax.dot_general(in_refs[a][...], in_refs[b][...], _DN[dims], preferred_element_type=F32)
            parts[c] = p if parts[c] is None else parts[c] + p

        def finish(accs):
            vals = epilogue(accs, [r[...] for r in ex_refs])
            for r, v in zip(out_refs, vals):
                r[...] = v.astype(r.dtype)

        if gk == 1:
            finish(parts)
        else:
            k = pl.program_id(2)

            @pl.when(k == 0)
            def _():
                for r, p in zip(acc_refs, parts):
                    r[...] = p

            @pl.when(k > 0)
            def _():
                for r, p in zip(acc_refs, parts):
                    r[...] += p

            @pl.when(k == gk - 1)
            def _():
                finish([r[...] for r in acc_refs])

    return pl.pallas_call(
        body,
        out_shape=[jax.ShapeDtypeStruct(s, d) for s, d, _, _ in outs],
        grid=grid,
        in_specs=[pl.BlockSpec(bs, im) for _, bs, im in list(ins) + list(extras)],
        out_specs=[pl.BlockSpec(bs, im) for _, _, bs, im in outs],
        scratch_shapes=[pltpu.VMEM(s, F32) for s in acc_shapes] if gk > 1 else [],
        name=name,
        compiler_params=_params(("parallel", "parallel", "arbitrary")),
    )(*[a for a, _, _ in ins], *[a for a, _, _ in extras])


def _mm2d(name, pairs, outs, epilogue=None, row_extras=(), vec_extras=()):
    def mk(a, dims):
        return (a.shape[0], a.shape[1]) if dims[0] == "n" else (a.shape[1], a.shape[0])

    def nk(b, dims):
        return (b.shape[1], b.shape[0]) if dims[1] == "n" else (b.shape[0], b.shape[1])

    m = mk(pairs[0][0], pairs[0][2])[0]
    ks = [mk(a, d)[1] for a, _, d, _ in pairs]
    n_acc = 1 + max(p[3] for p in pairs)
    acc_n = [None] * n_acc
    for a, b, d, c in pairs:
        assert mk(a, d)[0] == m and nk(b, d)[1] == mk(a, d)[1]
        acc_n[c] = nk(b, d)[0]
    tm = min(m, ROW_TILE)
    if len(set(ks)) == 1 and ks[0] > 1024:
        tks, gk = [512] * len(pairs), ks[0] // 512
    else:
        tks, gk = ks, 1
    if len(set(acc_n)) == 1 and acc_n[0] > 1024:
        tns, gj = [1024] * n_acc, acc_n[0] // 1024
    else:
        tns, gj = acc_n, 1

    ins, plist = [], []
    for (a, b, d, c), tk in zip(pairs, tks):
        tn = tns[c]
        a_spec = ((tm, tk), lambda i, j, k: (i, k)) if d[0] == "n" else ((tk, tm), lambda i, j, k: (k, i))
        b_spec = ((tk, tn), lambda i, j, k: (k, j)) if d[1] == "n" else ((tn, tk), lambda i, j, k: (j, k))
        ins += [(a, *a_spec), (b, *b_spec)]
        plist.append((len(ins) - 2, len(ins) - 1, c, d))
    extras = [(r, (tm, r.shape[1]), lambda i, j, k: (i, 0)) for r in row_extras]
    extras += [(v, v.shape, lambda i, j, k: (0, 0)) for v in vec_extras]
    out_specs = []
    for n, dt in outs:
        bn = 1024 if (gj > 1) else n
        out_specs.append(((m, n), dt, (tm, bn), lambda i, j, k: (i, j)))
    if epilogue is None:
        epilogue = lambda accs, ex: accs
    return _mm(name, (m // tm, gj, gk), ins, plist, [(tm, tn) for tn in tns], out_specs, epilogue, extras)


def _rms(x, g):
    return x * lax.rsqrt(jnp.mean(x * x, axis=-1, keepdims=True) + RMS_EPS) * g


def _rms_bwd(x, g, dy):
    r = lax.rsqrt(jnp.mean(x * x, axis=-1, keepdims=True) + RMS_EPS)
    gy = dy * g
    dx = r * gy - x * (r * r * r) * jnp.mean(gy * x, axis=-1, keepdims=True)
    return dx, jnp.sum(dy * x * r, axis=0, keepdims=True)


def _sigmoid(x):
    return 1.0 / (1.0 + jnp.exp(-x))


_GELU_C = math.sqrt(2.0 / math.pi)


def _gelu(x):
    return x * (0.5 * (1.0 + jnp.tanh(_GELU_C * (x + 0.044715 * (x * x * x)))))


def _gelu_grad(x):
    t = jnp.tanh(_GELU_C * (x + 0.044715 * (x * x * x)))
    return 0.5 * (1.0 + t) + 0.5 * x * (1.0 - t * t) * (_GELU_C * (1.0 + 3.0 * 0.044715 * (x * x)))


def _rows(name, row_ins, vec_ins, fn, row_outs, acc_outs=()):
    t = row_ins[0].shape[0]
    tm = min(t, ROW_TILE)
    nr, nv, no = len(row_ins), len(vec_ins), len(row_outs)

    def body(*refs):
        outs, incs = fn([r[...] for r in refs[:nr]], [r[...] for r in refs[nr:nr + nv]])
        for r, v in zip(refs[nr + nv:nr + nv + no], outs):
            r[...] = v.astype(r.dtype)
        i = pl.program_id(0)
        for r, v in zip(refs[nr + nv + no:], incs):
            @pl.when(i == 0)
            def _():
                r[...] = v

            @pl.when(i > 0)
            def _():
                r[...] += v

    in_specs = [pl.BlockSpec((tm, a.shape[1]), lambda i: (i, 0)) for a in row_ins]
    in_specs += [pl.BlockSpec(v.shape, lambda i, nd=v.ndim: (0,) * nd) for v in vec_ins]
    out_shape = [jax.ShapeDtypeStruct((t, c), dt) for c, dt in row_outs]
    out_shape += [jax.ShapeDtypeStruct(s, F32) for s in acc_outs]
    out_specs = [pl.BlockSpec((tm, c), lambda i: (i, 0)) for c, _ in row_outs]
    out_specs += [pl.BlockSpec(s, lambda i, nd=len(s): (0,) * nd) for s in acc_outs]
    return pl.pallas_call(body, out_shape=out_shape, grid=(t // tm,), in_specs=in_specs, out_specs=out_specs,
                          name=name, compiler_params=_params(("arbitrary",)))(*row_ins, *vec_ins)


def _rms_fwd(x, g):
    return _rows("rms_fwd", [x], [g], lambda r, v: ([_rms(r[0], v[0])], []), [(x.shape[1], BF16)])[0]


def _norm_out_bwd(name, f, d_out, g, scale):
    def fn(r, v):
        dx, dg = _rms_bwd(r[0], v[0], r[1] * scale)
        return [dx], [dg]

    c = f.shape[1]
    return _rows(name, [f, d_out], [g], fn, [(c, BF16)], [(1, c)])


def _norm_in_bwd(name, h, d_res, branches):
    nb = len(branches)

    def fn(r, v):
        dh, dgs = r[1], []
        for b in range(nb):
            dx, dg = _rms_bwd(r[0], v[b], r[2 + b])
            dh = dh + dx
            dgs.append(dg)
        return [dh], dgs

    c = h.shape[1]
    return _rows(name, [h, d_res] + [dn for _, dn in branches], [g for g, _ in branches], fn, [(c, F32)],
                 [(1, c)] * nb)


def _loss_head(y, target):
    d = y.shape[1]

    def fn(r, v):
        e = r[0] - r[1]
        s = jnp.sum(jnp.sum(e * e, axis=1, keepdims=True), axis=0, keepdims=True) * (0.5 / d)
        return [e * (1.0 / d)], [jnp.broadcast_to(s, (1, 128))]

    dy, acc = _rows("loss_head", [y, target], [], fn, [(d, F32)], [(1, 128)])
    return acc[0, 0], dy


def _rope_bwd(dk, cos2, sin2):
    c = dk.shape[1]
    return _rows("rope_bwd", [dk, cos2, sin2], [], lambda r, v: ([r[0] * r[1], r[0] * r[2]], []),
                 [(c, BF16), (c, BF16)])


def _ffn_up(n, wg, wu, l, j):
    t, d = n.shape
    fs = wg.shape[-1]
    tm = min(t, ROW_TILE)
    w_spec = ((None, None, None, d, fs), lambda s, i, k: (s, l, j, 0, 0))

    def epi(accs, ex):
        g, u = accs
        return [g, u, g * _sigmoid(g) * u]

    o_spec = ((None, tm, fs), lambda s, i, k: (s, i, 0))
    outs = [((N_SHARDS, t, fs), F32, *o_spec), ((N_SHARDS, t, fs), F32, *o_spec), ((N_SHARDS, t, fs), BF16, *o_spec)]
    return _mm("ffn_up", (N_SHARDS, t // tm, 1),
               [(n, (tm, d), lambda s, i, k: (i, 0)), (wg, *w_spec), (wu, *w_spec)],
               [(0, 1, 0, "nn"), (0, 2, 1, "nn")], [(tm, fs)] * 2, outs, epi)


def _down(name, a_in, w_in, gk, h, post_g, next_gs, scale):
    t, d = h.shape
    tm = min(t, ROW_TILE)
    kn = next_gs.shape[0]

    def epi(accs, ex):
        f, hv, pg, ng = accs[0], ex[0], ex[1], ex[2]
        hn = hv + scale * _rms(f, pg)
        return [f, hn] + [_rms(hn, ng[q:q + 1]) for q in range(kn)]

    row = ((tm, d), lambda i, j, k: (i, 0))
    outs = [((t, d), F32, *row), ((t, d), F32, *row)] + [((t, d), BF16, *row)] * kn
    extras = [(h, *row), (post_g, (1, d), lambda i, j, k: (0, 0)), (next_gs, (kn, d), lambda i, j, k: (0, 0))]
    return _mm(name, (t // tm, 1, gk), [a_in, w_in], [(0, 1, 0, "nn")], [(tm, d)], outs, epi, extras)


def _ffn_down(a, wd, l, j, h, post_g, next_gs):
    t, d = h.shape
    fs = a.shape[-1]
    tm = min(t, ROW_TILE)
    return _down("ffn_down", (a, (None, tm, fs), lambda i, _, s: (s, i, 0)),
                 (wd, (None, None, None, fs, d), lambda i, _, s: (s, l, j, 0, 0)), N_SHARDS, h, post_g, next_gs, 0.5)


def _ffn_dact(df, wd, l, j, g, u):
    t, d = df.shape
    fs = g.shape[-1]
    tm = min(t, ROW_TILE)

    def epi(accs, ex):
        da, gv, uv = accs[0], ex[0], ex[1]
        sg = _sigmoid(gv)
        return [da * uv * (sg * (1.0 + gv * (1.0 - sg))), da * (gv * sg)]

    o_spec = ((None, tm, fs), lambda s, i, k: (s, i, 0))
    outs = [((N_SHARDS, t, fs), BF16, *o_spec)] * 2
    return _mm("ffn_dact", (N_SHARDS, t // tm, 1),
               [(df, (tm, d), lambda s, i, k: (i, 0)), (wd, (None, None, None, fs, d), lambda s, i, k: (s, l, j, 0, 0))],
               [(0, 1, 0, "nt")], [(tm, fs)], outs, epi, [(g, *o_spec), (u, *o_spec)])


def _ffn_dn(dg, du, wg, wu, l, j):
    _, t, fs = dg.shape
    d = wg.shape[-2]
    tm = min(t, ROW_TILE)
    a_spec = ((None, tm, fs), lambda i, _, s: (s, i, 0))
    w_spec = ((None, None, None, d, fs), lambda i, _, s: (s, l, j, 0, 0))
    outs = [((t, d), F32, (tm, d), lambda i, _, s: (i, 0))]
    return _mm("ffn_dn", (t // tm, 1, N_SHARDS), [(dg, *a_spec), (wg, *w_spec), (du, *a_spec), (wu, *w_spec)],
               [(0, 1, 0, "nt"), (2, 3, 0, "nt")], [(tm, d)], outs, lambda accs, ex: accs)[0]


def _ffn_dw_in(n, dg, du):
    _, t, fs = dg.shape
    d = n.shape[1]
    tk = min(t, ROW_TILE)
    b_spec = ((None, tk, fs), lambda s, _, k: (s, k, 0))
    o_spec = ((None, d, fs), lambda s, _, k: (s, 0, 0))
    outs = [((N_SHARDS, d, fs), BF16, *o_spec)] * 2
    return _mm("ffn_dw_in", (N_SHARDS, 1, t // tk), [(n, (tk, d), lambda s, _, k: (k, 0)), (dg, *b_spec), (du, *b_spec)],
               [(0, 1, 0, "tn"), (0, 2, 1, "tn")], [(d, fs)] * 2, outs, lambda accs, ex: accs)


def _ffn_dw_down(a, df):
    _, t, fs = a.shape
    d = df.shape[1]
    tk = min(t, ROW_TILE)
    outs = [((N_SHARDS, fs, d), BF16, (None, fs, d), lambda s, _, k: (s, 0, 0))]
    return _mm("ffn_dw_down", (N_SHARDS, 1, t // tk),
               [(a, (None, tk, fs), lambda s, _, k: (s, k, 0)), (df, (tk, d), lambda s, _, k: (k, 0))],
               [(0, 1, 0, "tn")], [(fs, d)], outs, lambda accs, ex: accs)[0]


def _causal_weight(w):
    row = lax.broadcasted_iota(jnp.int32, (CHUNK, CHUNK), 0)
    col = lax.broadcasted_iota(jnp.int32, (CHUNK, CHUNK), 1)
    return row >= col, jnp.where(row >= col, w, 0.0).astype(BF16)


def _layer_norm(v, g, b):
    xc = v - jnp.mean(v, axis=-1, keepdims=True)
    rstd = lax.rsqrt(jnp.mean(xc * xc, axis=-1, keepdims=True) + LN_EPS)
    xhat = xc * rstd
    return xhat, rstd, xhat * g + b


def _sgu_specs(t, half, tm):
    return [pl.BlockSpec((tm, half), lambda i: (i, 0)), pl.BlockSpec((tm, half), lambda i: (i, 1))]


def _sgu_fwd(zp, ln_g, ln_b, w_s, bsb):
    t, half = zp.shape[0], zp.shape[1] // 2
    tm = min(t, 2 * CHUNK)

    def body(u_ref, v_ref, g_ref, b_ref, w_ref, bs_ref, o_ref):
        u = _gelu(u_ref[...])
        _, _, vn = _layer_norm(_gelu(v_ref[...]), g_ref[...], b_ref[...])
        vb = vn.astype(BF16)
        for g in range(GROUPS):
            _, wm = _causal_weight(w_ref[g])
            cols = slice(g * CHUNK, (g + 1) * CHUNK)
            for c in range(tm // CHUNK):
                rows = slice(c * CHUNK, (c + 1) * CHUNK)
                sv = jnp.dot(wm, vb[rows, cols], preferred_element_type=F32) + bs_ref[g]
                o_ref[rows, cols] = (u[rows, cols] * sv).astype(BF16)

    whole = lambda a: pl.BlockSpec(a.shape, lambda i, nd=a.ndim: (0,) * nd)
    return pl.pallas_call(
        body, out_shape=jax.ShapeDtypeStruct((t, half), BF16), grid=(t // tm,),
        in_specs=_sgu_specs(t, half, tm) + [whole(ln_g), whole(ln_b), whole(w_s), whole(bsb)],
        out_specs=pl.BlockSpec((tm, half), lambda i: (i, 0)), name="sgu_fwd",
        compiler_params=_params(("arbitrary",)))(zp, zp, ln_g, ln_b, w_s, bsb)


def _sgu_bwd(zp, d_uv, ln_g, ln_b, w_s, bsb):
    t, half = zp.shape[0], zp.shape[1] // 2
    tm = min(t, 2 * CHUNK)

    def body(u_ref, v_ref, d_ref, g_ref, b_ref, w_ref, bs_ref, dz_ref, dlg_ref, dlb_ref, dws_ref, dbs_ref, dvn_ref):
        i = pl.program_id(0)

        @pl.when(i == 0)
        def _():
            dlg_ref[...] = jnp.zeros_like(dlg_ref)
            dlb_ref[...] = jnp.zeros_like(dlb_ref)
            dws_ref[...] = jnp.zeros_like(dws_ref)
            dbs_ref[...] = jnp.zeros_like(dbs_ref)

        up, vp = u_ref[...], v_ref[...]
        u, gup = _gelu(up), _gelu_grad(up)
        xhat, rstd, vn = _layer_norm(_gelu(vp), g_ref[...], b_ref[...])
        vb = vn.astype(BF16)
        d = d_ref[...]
        for g in range(GROUPS):
            mask, wm = _causal_weight(w_ref[g])
            cols = slice(g * CHUNK, (g + 1) * CHUNK)
            for c in range(tm // CHUNK):
                rows = slice(c * CHUNK, (c + 1) * CHUNK)
                blk = vb[rows, cols]
                sv = jnp.dot(wm, blk, preferred_element_type=F32) + bs_ref[g]
                dblk = d[rows, cols]
                dz_ref[rows, cols] = (dblk * sv * gup[rows, cols]).astype(BF16)
                dsv = dblk * u[rows, cols]
                dsvb = dsv.astype(BF16)
                dvn_ref[rows, cols] = lax.dot_general(wm, dsvb, _DN["tn"], preferred_element_type=F32)
                dw = lax.dot_general(dsvb, blk, _DN["nt"], preferred_element_type=F32)
                dws_ref[g] += jnp.where(mask, dw, 0.0)
                dbs_ref[g] += jnp.sum(dsv, axis=1, keepdims=True)
        dvn = dvn_ref[...]
        dlg_ref[...] += jnp.sum(dvn * xhat, axis=0, keepdims=True)
        dlb_ref[...] += jnp.sum(dvn, axis=0, keepdims=True)
        dxh = dvn * g_ref[...]
        dv = rstd * (dxh - jnp.mean(dxh, axis=-1, keepdims=True)
                     - xhat * jnp.mean(dxh * xhat, axis=-1, keepdims=True))
        dz_ref[:, half:] = (dv * _gelu_grad(vp)).astype(BF16)

    whole = lambda a: pl.BlockSpec(a.shape, lambda i, nd=a.ndim: (0,) * nd)
    wshape = lambda s: pl.BlockSpec(s, lambda i, nd=len(s): (0,) * nd)
    out_shape = [jax.ShapeDtypeStruct((t, 2 * half), BF16), jax.ShapeDtypeStruct((1, half), F32),
                 jax.ShapeDtypeStruct((1, half), F32), jax.ShapeDtypeStruct(w_s.shape, F32),
                 jax.ShapeDtypeStruct((GROUPS, CHUNK, 1), F32)]
    return pl.pallas_call(
        body, out_shape=out_shape, grid=(t // tm,),
        in_specs=_sgu_specs(t, half, tm) + [pl.BlockSpec((tm, half), lambda i: (i, 0)), whole(ln_g), whole(ln_b),
                                            whole(w_s), whole(bsb)],
        out_specs=[pl.BlockSpec((tm, 2 * half), lambda i: (i, 0)), wshape((1, half)), wshape((1, half)),
                   wshape(w_s.shape), wshape((GROUPS, CHUNK, 1))],
        scratch_shapes=[pltpu.VMEM((tm, half), F32)], name="sgu_bwd",
        compiler_params=_params(("arbitrary",)))(zp, zp, d_uv, ln_g, ln_b, w_s, bsb)


_SCALE = (QK_NOPE + QK_ROPE) ** -0.5


def _attn_scores(qn, qr, kn, kr, i, tq, n):
    s = lax.dot_general(qn, kn, _DN["nt"], preferred_element_type=F32)
    s = (s + lax.dot_general(qr, kr, _DN["nt"], preferred_element_type=F32)) * _SCALE
    row = i * tq + lax.broadcasted_iota(jnp.int32, (tq, n), 0)
    col = lax.broadcasted_iota(jnp.int32, (tq, n), 1)
    return jnp.where(col <= row, s, NEG_INF)


def _attn_specs(seq):
    head = lambda b, h: (b, h)
    return dict(
        qn=pl.BlockSpec((seq, QK_NOPE), head),
        qr=pl.BlockSpec((None, seq, QK_ROPE), lambda b, h: (h, b, 0)),
        kr=pl.BlockSpec((seq, QK_ROPE), lambda b, h: (b, 0)),
        lse=pl.BlockSpec((None, seq, 1), lambda b, h: (h, b, 0)),
    )


def _attn_fwd(qn, qr, kn, v, kr, seq):
    t = qn.shape[0]
    tq = min(seq, 2 * CHUNK)
    sp = _attn_specs(seq)

    def body(qn_ref, qr_ref, kn_ref, v_ref, kr_ref, o_ref, lse_ref):
        for i in range(seq // tq):
            rows, n = slice(i * tq, (i + 1) * tq), (i + 1) * tq
            s = _attn_scores(qn_ref[rows, :], qr_ref[rows, :], kn_ref[0:n, :], kr_ref[0:n, :], i, tq, n)
            m = jnp.max(s, axis=-1, keepdims=True)
            p = jnp.exp(s - m)
            l = jnp.sum(p, axis=-1, keepdims=True)
            o_ref[rows, :] = jnp.dot((p / l).astype(BF16), v_ref[0:n, :], preferred_element_type=F32).astype(BF16)
            lse_ref[rows, :] = m + jnp.log(l)

    return pl.pallas_call(
        body, out_shape=[jax.ShapeDtypeStruct((t, N_HEADS * V_DIM), BF16), jax.ShapeDtypeStruct((N_HEADS, t, 1), F32)],
        grid=(t // seq, N_HEADS), in_specs=[sp["qn"], sp["qr"], sp["qn"], sp["qn"], sp["kr"]],
        out_specs=[sp["qn"], sp["lse"]], name="attn_fwd",
        compiler_params=_params(("parallel", "arbitrary")))(qn, qr, kn, v, kr)


def _attn_bwd(qn, qr, kn, v, kr, do, lse, cos2, sin2, seq):
    t = qn.shape[0]
    tq = min(seq, 2 * CHUNK)
    sp = _attn_specs(seq)

    def body(qn_ref, qr_ref, kn_ref, v_ref, kr_ref, do_ref, lse_ref, cos_ref, sin_ref,
             dqn_ref, dkn_ref, dv_ref, dqc_ref, dqs_ref, dkr_ref, dk_acc, dv_acc, dkr_acc):
        dk_acc[...] = jnp.zeros_like(dk_acc)
        dv_acc[...] = jnp.zeros_like(dv_acc)
        dkr_acc[...] = jnp.zeros_like(dkr_acc)
        for i in range(seq // tq):
            rows, n = slice(i * tq, (i + 1) * tq), (i + 1) * tq
            q_n, q_r, d_o = qn_ref[rows, :], qr_ref[rows, :], do_ref[rows, :]
            k_n, k_r = kn_ref[0:n, :], kr_ref[0:n, :]
            s = _attn_scores(q_n, q_r, k_n, k_r, i, tq, n)
            p = jnp.exp(s - lse_ref[rows, :])
            dp = lax.dot_general(d_o, v_ref[0:n, :], _DN["nt"], preferred_element_type=F32)
            ds = (p * (dp - jnp.sum(p * dp, axis=-1, keepdims=True)) * _SCALE).astype(BF16)
            dqn_ref[rows, :] = jnp.dot(ds, k_n, preferred_element_type=F32).astype(BF16)
            dqr = jnp.dot(ds, k_r, preferred_element_type=F32)
            dqc_ref[rows, :] = (dqr * cos_ref[rows, :]).astype(BF16)
            dqs_ref[rows, :] = (dqr * sin_ref[rows, :]).astype(BF16)
            dk_acc[0:n, :] += lax.dot_general(ds, q_n, _DN["tn"], preferred_element_type=F32)
            dkr_acc[0:n, :] += lax.dot_general(ds, q_r, _DN["tn"], preferred_element_type=F32)
            dv_acc[0:n, :] += lax.dot_general(p.astype(BF16), d_o, _DN["tn"], preferred_element_type=F32)
        dkn_ref[...] = dk_acc[...].astype(BF16)
        dv_ref[...] = dv_acc[...].astype(BF16)
        h = pl.program_id(1)

        @pl.when(h == 0)
        def _():
            dkr_ref[...] = dkr_acc[...]

        @pl.when(h > 0)
        def _():
            dkr_ref[...] += dkr_acc[...]

    wide = jax.ShapeDtypeStruct((t, N_HEADS * V_DIM), BF16)
    rope = jax.ShapeDtypeStruct((N_HEADS, t, QK_ROPE), BF16)
    krf = pl.BlockSpec((seq, QK_ROPE), lambda b, h: (b, 0))
    return pl.pallas_call(
        body, out_shape=[wide, wide, wide, rope, rope, jax.ShapeDtypeStruct((t, QK_ROPE), F32)],
        grid=(t // seq, N_HEADS),
        in_specs=[sp["qn"], sp["qr"], sp["qn"], sp["qn"], sp["kr"], sp["qn"], sp["lse"], krf, krf],
        out_specs=[sp["qn"], sp["qn"], sp["qn"], sp["qr"], sp["qr"], krf],
        scratch_shapes=[pltpu.VMEM((seq, QK_NOPE), F32), pltpu.VMEM((seq, V_DIM), F32), pltpu.VMEM((seq, QK_ROPE), F32)],
        name="attn_bwd", compiler_params=_params(("parallel", "arbitrary")))(qn, qr, kn, v, kr, do, lse, cos2, sin2)


def _row_tile(rows, cols, row_mult=8):
    cap = max(row_mult, (1 << 18) // cols)
    best = rows
    for tr in range(row_mult, min(rows, cap) + 1, row_mult):
        if rows % tr == 0:
            best = tr
    return best if rows > cap else rows


def _adamw(w, g, m, v):
    shape = w.shape
    c = shape[-1]
    r = w.size // c
    tr = _row_tile(r, c)

    def body(w_ref, g_ref, m_ref, v_ref, d_ref, nm_ref, nv_ref):
        gv = g_ref[...]
        mv = ADAM_B1 * m_ref[...] + (1.0 - ADAM_B1) * gv
        vv = ADAM_B2 * v_ref[...] + (1.0 - ADAM_B2) * (gv * gv)
        m_hat = mv / (1.0 - ADAM_B1 ** ADAM_STEP)
        v_hat = vv / (1.0 - ADAM_B2 ** ADAM_STEP)
        d_ref[...] = -ADAM_LR * (m_hat / (jnp.sqrt(v_hat) + ADAM_EPS) + ADAM_WD * w_ref[...])
        nm_ref[...] = mv
        nv_ref[...] = vv

    spec = pl.BlockSpec((tr, c), lambda i: (i, 0))
    outs = pl.pallas_call(body, out_shape=[jax.ShapeDtypeStruct((r, c), F32)] * 3, grid=(r // tr,),
                          in_specs=[spec] * 4, out_specs=[spec] * 3, name="adamw",
                          compiler_params=_params(("parallel",)))(*[a.reshape(r, c) for a in (w, g, m, v)])
    return [o.reshape(shape) for o in outs]


def _place():
    x, y, c = lax.axis_index("x"), lax.axis_index("y"), lax.axis_index("c")
    return x, y, c, [(1 - x, y), (x, 1 - y), (1 - x, 1 - y)]


def _dma_sems(*counts):
    return [pltpu.SemaphoreType.DMA((n,)) for n in counts]


def _all_gather(bufs):
    n = len(bufs)

    def body(*refs):
        ins, outs = refs[:n], refs[n:2 * n]
        send, recv, fsend, frecv, lsem = refs[2 * n:]
        x, y, c, chips = _place()
        k = 2 * x + y
        local = [pltpu.make_async_copy(ins[b], outs[b].at[k], lsem.at[b]) for b in range(n)]
        for cp in local:
            cp.start()
        started = []
        for b in range(n):
            for j, (px, py) in enumerate(chips):
                cp = pltpu.make_async_remote_copy(ins[b].at[c], outs[b].at[k, c], send.at[3 * b + j], recv.at[3 * b + j],
                                                  device_id=(px, py, c), device_id_type=_MESH)
                cp.start()
                started.append(cp)
        for b in range(n):
            for j, (px, py) in enumerate(chips):
                landed = outs[b].at[2 * px + py, c]
                pltpu.make_async_remote_copy(landed, landed, send.at[3 * b + j], recv.at[3 * b + j],
                                             device_id=(px, py, c), device_id_type=_MESH).wait_recv()
                cp = pltpu.make_async_remote_copy(landed, landed, fsend.at[3 * b + j], frecv.at[3 * b + j],
                                                  device_id=(x, y, 1 - c), device_id_type=_MESH)
                cp.start()
                started.append(cp)
        for b in range(n):
            for j, (px, py) in enumerate(chips):
                got = outs[b].at[2 * px + py, 1 - c]
                pltpu.make_async_remote_copy(got, got, fsend.at[3 * b + j], frecv.at[3 * b + j],
                                             device_id=(x, y, 1 - c), device_id_type=_MESH).wait_recv()
        for cp in started:
            cp.wait_send()
        for cp in local:
            cp.wait()

    return pl.pallas_call(
        body, out_shape=[jax.ShapeDtypeStruct((N_SHARDS,) + b.shape, b.dtype) for b in bufs],
        in_specs=[_ANY] * n, out_specs=[_ANY] * n, scratch_shapes=_dma_sems(3 * n, 3 * n, 3 * n, 3 * n, n),
        name="all_gather_weights")(*bufs)


def _swap_halves(parts):
    n = len(parts)

    def body(*refs):
        ins, outs = refs[:n], refs[n:2 * n]
        send, recv = refs[2 * n:]
        x, y, c, _ = _place()
        cps = [pltpu.make_async_remote_copy(ins[b].at[:, pl.ds(1 - c, 1)], outs[b], send.at[b], recv.at[b],
                                            device_id=(x, y, 1 - c), device_id_type=_MESH) for b in range(n)]
        for cp in cps:
            cp.start()
        for cp in cps:
            cp.wait()

    return pl.pallas_call(
        body, out_shape=[jax.ShapeDtypeStruct((N_SHARDS, 1) + p.shape[2:], p.dtype) for p in parts],
        in_specs=[_ANY] * n, out_specs=[_ANY] * n, scratch_shapes=_dma_sems(n, n), name="grad_swap_halves")(*parts)


def _add_half(part, other, core):
    _, _, r, c = part.shape
    tr = _row_tile(r, c, 16)

    def body(core_ref, p_ref, o_ref, out_ref):
        out_ref[...] = (p_ref[...].astype(F32) + o_ref[...].astype(F32)).astype(out_ref.dtype)

    grid_spec = pltpu.PrefetchScalarGridSpec(
        num_scalar_prefetch=1, grid=(N_SHARDS, r // tr),
        in_specs=[pl.BlockSpec((None, None, tr, c), lambda k, i, cr: (k, cr[0], i, 0)),
                  pl.BlockSpec((None, None, tr, c), lambda k, i, cr: (k, 0, i, 0))],
        out_specs=pl.BlockSpec((None, tr, c), lambda k, i, cr: (k, i, 0)))
    return pl.pallas_call(body, out_shape=jax.ShapeDtypeStruct((N_SHARDS, r, c), part.dtype), grid_spec=grid_spec,
                          name="grad_add_half", compiler_params=_params(("parallel", "parallel")))(core, part, other)


def _scatter_chips(parts):
    n = len(parts)

    def body(*refs):
        ins, outs = refs[:n], refs[n:2 * n]
        send, recv, lsem = refs[2 * n:]
        x, y, c, chips = _place()
        k = 2 * x + y
        local = [pltpu.make_async_copy(ins[b].at[k], outs[b].at[k], lsem.at[b]) for b in range(n)]
        for cp in local:
            cp.start()
        started = []
        for b in range(n):
            for j, (px, py) in enumerate(chips):
                cp = pltpu.make_async_remote_copy(ins[b].at[2 * px + py], outs[b].at[k], send.at[3 * b + j],
                                                  recv.at[3 * b + j], device_id=(px, py, c), device_id_type=_MESH)
                cp.start()
                started.append(cp)
        for b in range(n):
            for j, (px, py) in enumerate(chips):
                got = outs[b].at[2 * px + py]
                pltpu.make_async_remote_copy(got, got, send.at[3 * b + j], recv.at[3 * b + j],
                                             device_id=(px, py, c), device_id_type=_MESH).wait_recv()
        for cp in started:
            cp.wait_send()
        for cp in local:
            cp.wait()

    return pl.pallas_call(
        body, out_shape=[jax.ShapeDtypeStruct(p.shape, p.dtype) for p in parts],
        in_specs=[_ANY] * n, out_specs=[_ANY] * n, scratch_shapes=_dma_sems(3 * n, 3 * n, n),
        name="grad_scatter_chips")(*parts)


def _sum_slots(parts):
    _, r, c = parts.shape
    tr = _row_tile(r, c, 16)

    def body(p_ref, out_ref):
        acc = p_ref[0].astype(F32) + p_ref[1].astype(F32)
        out_ref[...] = (acc + p_ref[2].astype(F32)) + p_ref[3].astype(F32)

    return pl.pallas_call(body, out_shape=jax.ShapeDtypeStruct((r, c), F32), grid=(r // tr,),
                          in_specs=[pl.BlockSpec((N_SHARDS, tr, c), lambda i: (0, i, 0))],
                          out_specs=pl.BlockSpec((tr, c), lambda i: (i, 0)), name="grad_sum_slots",
                          compiler_params=_params(("parallel",)))(parts)


def _join_halves(halves, out_shapes, where):
    n, no = len(halves), len(out_shapes)

    def body(*refs):
        ins, outs = refs[:n], refs[n:n + no]
        send, recv, lsem = refs[n + no:]
        x, y, c, _ = _place()
        local, remote = [], []
        for b in range(n):
            o, lead = where[b]
            dst = outs[o].at[(*lead, c)]
            local.append(pltpu.make_async_copy(ins[b], dst, lsem.at[b]))
            remote.append(pltpu.make_async_remote_copy(ins[b], dst, send.at[b], recv.at[b],
                                                       device_id=(x, y, 1 - c), device_id_type=_MESH))
        for cp in local + remote:
            cp.start()
        for b in range(n):
            o, lead = where[b]
            got = outs[o].at[(*lead, 1 - c)]
            pltpu.make_async_remote_copy(got, got, send.at[b], recv.at[b], device_id=(x, y, 1 - c),
                                         device_id_type=_MESH).wait_recv()
        for cp in remote:
            cp.wait_send()
        for cp in local:
            cp.wait()

    return pl.pallas_call(
        body, out_shape=[jax.ShapeDtypeStruct(s, F32) for s in out_shapes], in_specs=[_ANY] * n,
        out_specs=[_ANY] * no, scratch_shapes=_dma_sems(n, n, n), name="grad_join_halves")(*halves)


def _gather_rows(buf, start, rows):
    def body(in_ref, out_ref, send, recv, lsem):
        x, y, c, chips = _place()
        k = 2 * x + y
        src = in_ref.at[pl.ds(start, rows)]
        local = pltpu.make_async_copy(src, out_ref.at[k], lsem.at[0])
        local.start()
        cps = [pltpu.make_async_remote_copy(src, out_ref.at[k], send.at[j], recv.at[j], device_id=(px, py, c),
                                            device_id_type=_MESH) for j, (px, py) in enumerate(chips)]
        for cp in cps:
            cp.start()
        for j, (px, py) in enumerate(chips):
            got = out_ref.at[2 * px + py]
            pltpu.make_async_remote_copy(got, got, send.at[j], recv.at[j], device_id=(px, py, c),
                                         device_id_type=_MESH).wait_recv()
        for cp in cps:
            cp.wait_send()
        local.wait()

    return pl.pallas_call(body, out_shape=jax.ShapeDtypeStruct((N_SHARDS, rows, buf.shape[1]), F32),
                          in_specs=[_ANY], out_specs=_ANY, scratch_shapes=_dma_sems(3, 3, 1),
                          name="gather_replicated_grads")(buf)


def _round_up(n, m):
    return -(-n // m) * m


def _pack_flat(vecs, rows, width, dtype):
    flat = jnp.concatenate([v.reshape(-1).astype(dtype) for v in vecs])
    return jnp.pad(flat, (0, rows * width - flat.size)).reshape(rows, width)


def _split_flat(flat, shapes):
    out, off = [], 0
    for s in shapes:
        n = math.prod(s)
        out.append(flat[off:off + n].reshape(s))
        off += n
    return out


def _merge_shards(arr4, axis):
    a = jnp.moveaxis(arr4, 0, axis)
    s = list(a.shape)
    return a.reshape(s[:axis] + [s[axis] * s[axis + 1]] + s[axis + 2:])


def _split_shards(full, axis):
    s = list(full.shape)
    a = full.reshape(s[:axis] + [N_SHARDS, s[axis] // N_SHARDS] + s[axis + 1:])
    return jnp.moveaxis(a, axis, 0).reshape(N_SHARDS, -1)


def _rot_cols(w):
    half = w.shape[-1] // 2
    return jnp.concatenate([-w[..., half:], w[..., :half]], axis=-1)


def _unrot_cols(dw):
    half = dw.shape[-1] // 2
    return jnp.concatenate([dw[..., half:], -dw[..., :half]], axis=-1)


def kernel(x, positions, ffn_pre_g, ffn_post_g, ffn_w_gate, ffn_w_up, ffn_w_down, mix_pre_g, mix_post_g, gmlp_w_in, gmlp_ln_g, gmlp_ln_b, gmlp_w_s, gmlp_b_s, gmlp_w_out, kv_norm_g, w_dkv, kv_a_norm_g, w_ukv, mla_w_dq, mla_q_norm_g, mla_w_uq, mla_w_o, loss_target, m_ffn_pre_g, m_ffn_post_g, m_ffn_w_gate, m_ffn_w_up, m_ffn_w_down, m_mix_pre_g, m_mix_post_g, m_gmlp_w_in, m_gmlp_ln_g, m_gmlp_ln_b, m_gmlp_w_s, m_gmlp_b_s, m_gmlp_w_out, m_kv_norm_g, m_w_dkv, m_kv_a_norm_g, m_w_ukv, m_mla_w_dq, m_mla_q_norm_g, m_mla_w_uq, m_mla_w_o, v_ffn_pre_g, v_ffn_post_g, v_ffn_w_gate, v_ffn_w_up, v_ffn_w_down, v_mix_pre_g, v_mix_post_g, v_gmlp_w_in, v_gmlp_ln_g, v_gmlp_ln_b, v_gmlp_w_s, v_gmlp_b_s, v_gmlp_w_out, v_kv_norm_g, v_w_dkv, v_kv_a_norm_g, v_w_ukv, v_mla_w_dq, v_mla_q_norm_g, v_mla_w_uq, v_mla_w_o):
    names = ["ffn_pre_g", "ffn_post_g", "ffn_w_gate", "ffn_w_up", "ffn_w_down", "mix_pre_g", "mix_post_g", "gmlp_w_in",
             "gmlp_ln_g", "gmlp_ln_b", "gmlp_w_s", "gmlp_b_s", "gmlp_w_out", "kv_norm_g", "w_dkv", "kv_a_norm_g", "w_ukv",
             "mla_w_dq", "mla_q_norm_g", "mla_w_uq", "mla_w_o"]
    env = locals()
    w = {n: env[n] for n in names}
    mom = {n: env["m_" + n] for n in names}
    var = {n: env["v_" + n] for n in names}

    bsz, seq, d = x.shape
    t = bsz * seq
    core = lax.axis_index("c").astype(jnp.int32).reshape(1)

    mats = [("gmlp_w_in", 2), ("gmlp_w_out", 1), ("w_dkv", 0), ("w_ukv", 1), ("mla_w_dq", 1), ("mla_w_uq", 2),
            ("mla_w_o", 1)]
    vecs = [("ffn_pre_g", 2), ("ffn_post_g", 2), ("gmlp_ln_g", 1), ("gmlp_ln_b", 1)]
    replicated = ["mix_pre_g", "mix_post_g", "gmlp_w_s", "gmlp_b_s", "kv_norm_g", "kv_a_norm_g", "mla_q_norm_g"]
    n_mats = sum(w[n].size for n, _ in mats)
    n_vecs = sum(w[n].size for n, _ in vecs)
    mat_rows = _round_up(-(-n_mats // PACK_WIDTH), 32)
    vec_rows = _round_up(-(-n_vecs // 128), 16)
    mat_pack = _pack_flat([w[n] for n, _ in mats], mat_rows, PACK_WIDTH, BF16).reshape(2, mat_rows // 2, PACK_WIDTH)
    vec_pack = _pack_flat([w[n] for n, _ in vecs], vec_rows, 128, F32).reshape(2, vec_rows // 2, 128)
    wg_all, wu_all, wd_all, mat_all, vec_all = _all_gather(
        [w["ffn_w_gate"].astype(BF16), w["ffn_w_up"].astype(BF16), w["ffn_w_down"].astype(BF16), mat_pack, vec_pack])

    full = {}
    flat4 = mat_all.reshape(N_SHARDS, -1)
    off = 0
    for n, ax in mats:
        full[n] = _merge_shards(flat4[:, off:off + w[n].size].reshape((N_SHARDS,) + w[n].shape), ax)
        off += w[n].size
    flat4 = vec_all.reshape(N_SHARDS, -1)
    off = 0
    for n, ax in vecs:
        full[n] = _merge_shards(flat4[:, off:off + w[n].size].reshape((N_SHARDS,) + w[n].shape), ax)
        off += w[n].size

    w_in, w_out = full["gmlp_w_in"][0], full["gmlp_w_out"][0]
    ln_g, ln_b = full["gmlp_ln_g"], full["gmlp_ln_b"]
    w_c, w_kr = full["w_dkv"][:, :KV_RANK], full["w_dkv"][:, KV_RANK:]
    w_kr_rot = _rot_cols(w_kr)
    ukv = full["w_ukv"].reshape(KV_RANK, N_HEADS, 2, QK_NOPE)
    w_k, w_v = ukv[:, :, 0].reshape(KV_RANK, -1), ukv[:, :, 1].reshape(KV_RANK, -1)
    w_dq, w_o = full["mla_w_dq"][0], full["mla_w_o"][0]
    q_rank = w_dq.shape[1]
    uq = full["mla_w_uq"][0].reshape(q_rank, N_HEADS, QK_NOPE + QK_ROPE)
    w_qn = uq[:, :, :QK_NOPE].reshape(q_rank, -1)
    w_qr = uq[:, :, QK_NOPE:].reshape(q_rank, -1)
    w_qr_rot = _rot_cols(uq[:, :, QK_NOPE:]).reshape(q_rank, -1)
    pre_g, post_g = full["ffn_pre_g"], full["ffn_post_g"]
    w_s = w["gmlp_w_s"][0]
    bsb = jnp.broadcast_to(w["gmlp_b_s"][0][:, :, None], (GROUPS, CHUNK, CHUNK))
    row = lambda v: v.reshape(1, -1)

    inv_freq = ROPE_THETA ** (-jnp.arange(0, QK_ROPE, 2, dtype=F32) / QK_ROPE)
    ang = positions.astype(F32).reshape(t, 1) * inv_freq
    cos2 = jnp.concatenate([jnp.cos(ang)] * 2, axis=-1)
    sin2 = jnp.concatenate([jnp.sin(ang)] * 2, axis=-1)
    cos_h, sin_h = jnp.tile(cos2, (1, N_HEADS)), jnp.tile(sin2, (1, N_HEADS))

    def rope_epi(n_lin):
        def epi(accs, ex):
            return accs[:n_lin] + [accs[n_lin] * ex[0] + accs[n_lin + 1] * ex[1]]
        return epi

    h0 = x.reshape(t, d)
    saved = {}

    def ffn_fwd(l, j, h, n, next_gs):
        g, u, a = _ffn_up(n, wg_all, wu_all, l, j)
        f, h_new, *n_next = _ffn_down(a, wd_all, l, j, h, row(post_g[l, j]), next_gs)
        saved[("ffn", l, j)] = (h, n, g, u, a, f)
        return h_new, n_next

    n0 = _rms_fwd(h0, row(pre_g[0, 0]))
    h1, (n1,) = ffn_fwd(0, 0, h0, n0, row(w["mix_pre_g"][0]))
    zp = _mm2d("gmlp_in", [(n1, w_in, "nn", 0)], [(w_in.shape[1], F32)])[0]
    uv = _sgu_fwd(zp, ln_g, ln_b, w_s, bsb)
    half = uv.shape[1]
    tm = min(t, ROW_TILE)
    m0, h2, n2 = _down("gmlp_out", (uv, (tm, 512), lambda i, _, k: (i, k)), (w_out, (512, d), lambda i, _, k: (k, 0)),
                       half // 512, h1, row(w["mix_post_g"][0]), row(pre_g[0, 1]), 1.0)
    h3, (n3kv, n3) = ffn_fwd(0, 1, h2, n2, jnp.stack([w["kv_norm_g"], pre_g[1, 0]]))

    def kv_epi(accs, ex):
        c_raw = accs[0]
        return [c_raw, _rms(c_raw, ex[2]), accs[1] * ex[0] + accs[2] * ex[1]]

    c_raw, c_n, k_r = _mm2d("kv_down", [(n3kv, w_c, "nn", 0), (n3kv, w_kr, "nn", 1), (n3kv, w_kr_rot, "nn", 2)],
                            [(KV_RANK, F32), (KV_RANK, BF16), (QK_ROPE, BF16)], kv_epi, [cos2, sin2],
                            [row(w["kv_a_norm_g"])])
    k_n, v_h = _mm2d("kv_up", [(c_n, w_k, "nn", 0), (c_n, w_v, "nn", 1)], [(w_k.shape[1], BF16), (w_v.shape[1], BF16)])

    h4, (n4,) = ffn_fwd(1, 0, h3, n3, row(w["mix_pre_g"][1]))
    qd, qn = _mm2d("q_down", [(n4, w_dq, "nn", 0)], [(q_rank, F32), (q_rank, BF16)],
                   lambda accs, ex: [accs[0], _rms(accs[0], ex[0])], [], [row(w["mla_q_norm_g"][0])])
    q_n, q_r = _mm2d("q_up", [(qn, w_qn, "nn", 0), (qn, w_qr, "nn", 1), (qn, w_qr_rot, "nn", 2)],
                     [(w_qn.shape[1], BF16), (w_qr.shape[1], BF16)], rope_epi(1), [cos_h, sin_h])
    q_r = q_r.reshape(t, N_HEADS, QK_ROPE).transpose(1, 0, 2)
    o, lse = _attn_fwd(q_n, q_r, k_n, v_h, k_r, seq)
    m1, h5, n5 = _down("attn_out", (o, (tm, 512), lambda i, _, k: (i, k)), (w_o, (512, d), lambda i, _, k: (k, 0)),
                       o.shape[1] // 512, h4, row(w["mix_post_g"][1]), row(pre_g[1, 1]), 1.0)
    y, _ = ffn_fwd(1, 1, h5, n5, row(pre_g[1, 1]))

    loss_part, dy = _loss_head(y, loss_target.reshape(t, d))
    loss = lax.psum(loss_part, ("x", "y", "c"))

    grads = {}
    ffn_dw = {}
    d_pre, d_post = {}, {}

    def ffn_bwd(l, j, dh_out, extra=()):
        h, n, g, u, a, f = saved[("ffn", l, j)]
        df, d_post[(l, j)] = _norm_out_bwd("ffn_post_bwd", f, dh_out, row(post_g[l, j]), 0.5)
        dg, du = _ffn_dact(df, wd_all, l, j, g, u)
        dwd = _ffn_dw_down(a, df)
        dn = _ffn_dn(dg, du, wg_all, wu_all, l, j)
        dwg, dwu = _ffn_dw_in(n, dg, du)
        ffn_dw[(l, j)] = (dwg, dwu, dwd)
        dh, d_pre[(l, j)], *rest = _norm_in_bwd("ffn_pre_bwd", h, dh_out, [(row(pre_g[l, j]), dn)] + list(extra))
        return dh, rest

    dh5, _ = ffn_bwd(1, 1, dy)

    dm1, g_mix_post1 = _norm_out_bwd("mix_post_bwd", m1, dh5, row(w["mix_post_g"][1]), 1.0)
    do = _mm2d("attn_out_dx", [(dm1, w_o, "nt", 0)], [(w_o.shape[0], BF16)])[0]
    g_w_o = _mm2d("attn_out_dw", [(o, dm1, "tn", 0)], [(d, F32)])[0]
    dq_n, dk_n, dv_h, dq_c, dq_s, dk_r = _attn_bwd(q_n, q_r, k_n, v_h, k_r, do, lse, cos2, sin2, seq)
    dq_c = dq_c.transpose(1, 0, 2).reshape(t, -1)
    dq_s = dq_s.transpose(1, 0, 2).reshape(t, -1)
    dqn = _mm2d("q_up_dx", [(dq_n, w_qn, "nt", 0), (dq_c, w_qr, "nt", 0), (dq_s, w_qr_rot, "nt", 0)], [(q_rank, F32)])[0]
    g_qn, g_qr, g_qr_rot = _mm2d("q_up_dw", [(qn, dq_n, "tn", 0), (qn, dq_c, "tn", 1), (qn, dq_s, "tn", 2)],
                                 [(w_qn.shape[1], F32), (w_qr.shape[1], F32), (w_qr.shape[1], F32)])
    dqd, g_q_norm = _norm_out_bwd("q_norm_bwd", qd, dqn, row(w["mla_q_norm_g"][0]), 1.0)
    dn4 = _mm2d("q_down_dx", [(dqd, w_dq, "nt", 0)], [(d, F32)])[0]
    g_w_dq = _mm2d("q_down_dw", [(n4, dqd, "tn", 0)], [(q_rank, F32)])[0]
    dh4, g_mix_pre1 = _norm_in_bwd("mix_pre_bwd", h4, dh5, [(row(w["mix_pre_g"][1]), dn4)])

    dc_n = _mm2d("kv_up_dx", [(dk_n, w_k, "nt", 0), (dv_h, w_v, "nt", 0)], [(KV_RANK, F32)])[0]
    g_wk, g_wv = _mm2d("kv_up_dw", [(c_n, dk_n, "tn", 0), (c_n, dv_h, "tn", 1)], [(w_k.shape[1], F32), (w_v.shape[1], F32)])
    dc, g_kv_a = _norm_out_bwd("kv_a_norm_bwd", c_raw, dc_n, row(w["kv_a_norm_g"]), 1.0)
    dkr_c, dkr_s = _rope_bwd(dk_r, cos2, sin2)
    dn3kv = _mm2d("kv_down_dx", [(dc, w_c, "nt", 0), (dkr_c, w_kr, "nt", 0), (dkr_s, w_kr_rot, "nt", 0)], [(d, F32)])[0]
    g_wc, g_wkr, g_wkr_rot = _mm2d("kv_down_dw", [(n3kv, dc, "tn", 0), (n3kv, dkr_c, "tn", 1), (n3kv, dkr_s, "tn", 2)],
                                   [(KV_RANK, F32), (QK_ROPE, F32), (QK_ROPE, F32)])

    dh3, (g_kv_norm,) = ffn_bwd(1, 0, dh4, extra=[(row(w["kv_norm_g"]), dn3kv)])
    dh2, _ = ffn_bwd(0, 1, dh3)

    dm0, g_mix_post0 = _norm_out_bwd("mix_post_bwd", m0, dh2, row(w["mix_post_g"][0]), 1.0)
    d_uv = _mm2d("gmlp_out_dx", [(dm0, w_out, "nt", 0)], [(half, F32)])[0]
    g_w_out = _mm2d("gmlp_out_dw", [(uv, dm0, "tn", 0)], [(d, F32)])[0]
    dzp, g_ln_g, g_ln_b, g_w_s, g_b_s = _sgu_bwd(zp, d_uv, ln_g, ln_b, w_s, bsb)
    dn1 = _mm2d("gmlp_in_dx", [(dzp, w_in, "nt", 0)], [(d, F32)])[0]
    g_w_in = _mm2d("gmlp_in_dw", [(n1, dzp, "tn", 0)], [(w_in.shape[1], F32)])[0]
    dh1, g_mix_pre0 = _norm_in_bwd("mix_pre_bwd", h1, dh2, [(row(w["mix_pre_g"][0]), dn1)])
    dx, _ = ffn_bwd(0, 0, dh1)

    lj = [(l, j) for l in range(2) for j in range(2)]
    part = {
        "gmlp_w_in": g_w_in[None], "gmlp_w_out": g_w_out[None],
        "w_dkv": jnp.concatenate([g_wc, g_wkr + _unrot_cols(g_wkr_rot)], axis=1),
        "w_ukv": jnp.stack([g_wk.reshape(KV_RANK, N_HEADS, QK_NOPE), g_wv.reshape(KV_RANK, N_HEADS, V_DIM)],
                           axis=2).reshape(KV_RANK, -1),
        "mla_w_dq": g_w_dq[None],
        "mla_w_uq": jnp.concatenate(
            [g_qn.reshape(q_rank, N_HEADS, QK_NOPE),
             g_qr.reshape(q_rank, N_HEADS, QK_ROPE) + _unrot_cols(g_qr_rot.reshape(q_rank, N_HEADS, QK_ROPE))],
            axis=-1).reshape(1, q_rank, -1),
        "mla_w_o": g_w_o[None],
        "ffn_pre_g": jnp.concatenate([d_pre[k] for k in lj]).reshape(2, 2, d),
        "ffn_post_g": jnp.concatenate([d_post[k] for k in lj]).reshape(2, 2, d),
        "gmlp_ln_g": g_ln_g, "gmlp_ln_b": g_ln_b,
        "mix_pre_g": jnp.concatenate([g_mix_pre0, g_mix_pre1]), "mix_post_g": jnp.concatenate([g_mix_post0, g_mix_post1]),
        "gmlp_w_s": g_w_s[None], "gmlp_b_s": g_b_s.reshape(1, GROUPS, CHUNK),
        "kv_norm_g": g_kv_norm.reshape(-1), "kv_a_norm_g": g_kv_a.reshape(-1), "mla_q_norm_g": g_q_norm,
    }

    sharded = mats + vecs
    n_sh = n_mats + n_vecs
    n_rep = sum(w[n].size for n in replicated)
    sh_rows = _round_up(-(-n_sh // PACK_WIDTH), 8)
    rep_rows = _round_up(-(-(n_rep // N_SHARDS) // PACK_WIDTH), 8)
    rows = _round_up(sh_rows + rep_rows, 32)
    sh_flat = jnp.concatenate([_split_shards(part[n], ax) for n, ax in sharded], axis=1)
    rep_flat = jnp.concatenate([part[n].reshape(-1) for n in replicated]).reshape(N_SHARDS, -1)
    small = jnp.concatenate([
        jnp.pad(sh_flat, ((0, 0), (0, sh_rows * PACK_WIDTH - n_sh))),
        jnp.pad(rep_flat, ((0, 0), (0, (rows - sh_rows) * PACK_WIDTH - n_rep // N_SHARDS)))], axis=1)
    small = small.astype(BF16).reshape(N_SHARDS, 2, rows // 2, PACK_WIDTH)

    parts, where = [], []
    for l, j in lj:
        for q, g in enumerate(ffn_dw[(l, j)]):
            parts.append(g.reshape(N_SHARDS, 2, g.shape[1] // 2, g.shape[2]))
            where.append((q, (l, j)))
    parts.append(small)
    where.append((3, ()))
    others = _swap_halves(parts)
    chip_sums = [_add_half(p, o, core) for p, o in zip(parts, others)]
    slots = _scatter_chips(chip_sums)
    halves = [_sum_slots(s) for s in slots]
    out_shapes = [(2, 2) + parts[q].shape[1:] for q in range(3)] + [(2, rows // 2, PACK_WIDTH)]
    g_gate, g_up, g_down, g_small = _join_halves(halves, out_shapes, where)
    g_rep = _gather_rows(g_small.reshape(rows, PACK_WIDTH), sh_rows, rep_rows)

    grads = {"ffn_w_gate": g_gate.reshape(w["ffn_w_gate"].shape), "ffn_w_up": g_up.reshape(w["ffn_w_up"].shape),
             "ffn_w_down": g_down.reshape(w["ffn_w_down"].shape)}
    for (n, _), g in zip(sharded, _split_flat(g_small.reshape(-1), [w[n].shape for n, _ in sharded])):
        grads[n] = g
    rep_vec = g_rep.reshape(N_SHARDS, -1)[:, :n_rep // N_SHARDS].reshape(-1)
    for n, g in zip(replicated, _split_flat(rep_vec, [w[n].shape for n in replicated])):
        grads[n] = g

    delta, new_m, new_v = {}, {}, {}
    for n in names:
        delta[n], new_m[n], new_v[n] = _adamw(w[n], grads[n], mom[n], var[n])
    return (loss, dx.reshape(x.shape), *[grads[n] for n in names], *[delta[n] for n in names],
            *[new_m[n] for n in names], *[new_v[n] for n in names])
```

```python
import math

import jax
import jax.numpy as jnp
from jax import lax
from jax.experimental import pallas as pl
from jax.experimental.pallas import tpu as pltpu

F32, BF16 = jnp.float32, jnp.bfloat16

RMS_EPS, LN_EPS, NEG_INF = 1e-6, 1e-5, -1e30
N_HEADS, QK_NOPE, QK_ROPE, V_DIM, KV_RANK = 8, 128, 64, 128, 256
CHUNK, GROUPS = 128, 16
ROPE_THETA = 10000.0
ADAM_LR, ADAM_B1, ADAM_B2, ADAM_EPS, ADAM_WD, ADAM_STEP = 0.001, 0.9, 0.999, 1e-08, 0.01, 10
N_SHARDS = 4

VMEM_LIMIT_BYTES = 48 * 1024 * 1024
ROW_TILE = 512
PACK_WIDTH = 1024

_DN = {"nn": (((1,), (0,)), ((), ())), "nt": (((1,), (1,)), ((), ())), "tn": (((0,), (0,)), ((), ()))}
_MESH = pl.DeviceIdType.MESH
_ANY = pl.BlockSpec(memory_space=pl.ANY)


def _params(sem):
    return pltpu.CompilerParams(dimension_semantics=sem, vmem_limit_bytes=VMEM_LIMIT_BYTES)


def _mm(name, grid, ins, pairs, acc_shapes, outs, epilogue, extras=()):
    n_in, n_ex, n_out = len(ins), len(extras), len(outs)
    gk = grid[2]

    def body(*refs):
        in_refs, ex_refs = refs[:n_in], refs[n_in:n_in + n_ex]
        out_refs = refs[n_in + n_ex:n_in + n_ex + n_out]
        acc_refs = refs[n_in + n_ex + n_out:]
        parts = [None] * len(acc_shapes)
        for a, b, c, dims in pairs:
            p = lax.dot_general(in_refs[a][...], in_refs[b][...], _DN[dims], preferred_element_type=F32)
            parts[c] = p if parts[c] is None else parts[c] + p

        def finish(accs):
            vals = epilogue(accs, [r[...] for r in ex_refs])
            for r, v in zip(out_refs, vals):
                r[...] = v.astype(r.dtype)

        if gk == 1:
            finish(parts)
        else:
            k = pl.program_id(2)

            @pl.when(k == 0)
            def _():
                for r, p in zip(acc_refs, parts):
                    r[...] = p

            @pl.when(k > 0)
            def _():
                for r, p in zip(acc_refs, parts):
                    r[...] += p

            @pl.when(k == gk - 1)
            def _():
                finish([r[...] for r in acc_refs])

    return pl.pallas_call(
        body,
        out_shape=[jax.ShapeDtypeStruct(s, d) for s, d, _, _ in outs],
        grid=grid,
        in_specs=[pl.BlockSpec(bs, im) for _, bs, im in list(ins) + list(extras)],
        out_specs=[pl.BlockSpec(bs, im) for _, _, bs, im in outs],
        scratch_shapes=[pltpu.VMEM(s, F32) for s in acc_shapes] if gk > 1 else [],
        name=name,
        compiler_params=_params(("parallel", "parallel", "arbitrary")),
    )(*[a for a, _, _ in ins], *[a for a, _, _ in extras])


def _mm2d(name, pairs, outs, epilogue=None, row_extras=(), vec_extras=()):
    def mk(a, dims):
        return (a.shape[0], a.shape[1]) if dims[0] == "n" else (a.shape[1], a.shape[0])

    def nk(b, dims):
        return (b.shape[1], b.shape[0]) if dims[1] == "n" else (b.shape[0], b.shape[1])

    m = mk(pairs[0][0], pairs[0][2])[0]
    ks = [mk(a, d)[1] for a, _, d, _ in pairs]
    n_acc = 1 + max(p[3] for p in pairs)
    acc_n = [None] * n_acc
    for a, b, d, c in pairs:
        assert mk(a, d)[0] == m and nk(b, d)[1] == mk(a, d)[1]
        acc_n[c] = nk(b, d)[0]
    tm = min(m, ROW_TILE)
    if len(set(ks)) == 1 and ks[0] > 1024:
        tks, gk = [512] * len(pairs), ks[0] // 512
    else:
        tks, gk = ks, 1
    if len(set(acc_n)) == 1 and acc_n[0] > 1024:
        tns, gj = [1024] * n_acc, acc_n[0] // 1024
    else:
        tns, gj = acc_n, 1

    ins, plist = [], []
    for (a, b, d, c), tk in zip(pairs, tks):
        tn = tns[c]
        a_spec = ((tm, tk), lambda i, j, k: (i, k)) if d[0] == "n" else ((tk, tm), lambda i, j, k: (k, i))
        b_spec = ((tk, tn), lambda i, j, k: (k, j)) if d[1] == "n" else ((tn, tk), lambda i, j, k: (j, k))
        ins += [(a, *a_spec), (b, *b_spec)]
        plist.append((len(ins) - 2, len(ins) - 1, c, d))
    extras = [(r, (tm, r.shape[1]), lambda i, j, k: (i, 0)) for r in row_extras]
    extras += [(v, v.shape, lambda i, j, k: (0, 0)) for v in vec_extras]
    out_specs = []
    for n, dt in outs:
        bn = 1024 if (gj > 1) else n
        out_specs.append(((m, n), dt, (tm, bn), lambda i, j, k: (i, j)))
    if epilogue is None:
        epilogue = lambda accs, ex: accs
    return _mm(name, (m // tm, gj, gk), ins, plist, [(tm, tn) for tn in tns], out_specs, epilogue, extras)


def _rms(x, g):
    return x * lax.rsqrt(jnp.mean(x * x, axis=-1, keepdims=True) + RMS_EPS) * g


def _rms_bwd(x, g, dy):
    r = lax.rsqrt(jnp.mean(x * x, axis=-1, keepdims=True) + RMS_EPS)
    gy = dy * g
    dx = r * gy - x * (r * r * r) * jnp.mean(gy * x, axis=-1, keepdims=True)
    return dx, jnp.sum(dy * x * r, axis=0, keepdims=True)


def _sigmoid(x):
    return 1.0 / (1.0 + jnp.exp(-x))


_GELU_C = math.sqrt(2.0 / math.pi)


def _gelu(x):
    return x * (0.5 * (1.0 + jnp.tanh(_GELU_C * (x + 0.044715 * (x * x * x)))))


def _gelu_grad(x):
    t = jnp.tanh(_GELU_C * (x + 0.044715 * (x * x * x)))
    return 0.5 * (1.0 + t) + 0.5 * x * (1.0 - t * t) * (_GELU_C * (1.0 + 3.0 * 0.044715 * (x * x)))


def _rows(name, row_ins, vec_ins, fn, row_outs, acc_outs=()):
    t = row_ins[0].shape[0]
    tm = min(t, ROW_TILE)
    nr, nv, no = len(row_ins), len(vec_ins), len(row_outs)

    def body(*refs):
        outs, incs = fn([r[...] for r in refs[:nr]], [r[...] for r in refs[nr:nr + nv]])
        for r, v in zip(refs[nr + nv:nr + nv + no], outs):
            r[...] = v.astype(r.dtype)
        i = pl.program_id(0)
        for r, v in zip(refs[nr + nv + no:], incs):
            @pl.when(i == 0)
            def _():
                r[...] = v

            @pl.when(i > 0)
            def _():
                r[...] += v

    in_specs = [pl.BlockSpec((tm, a.shape[1]), lambda i: (i, 0)) for a in row_ins]
    in_specs += [pl.BlockSpec(v.shape, lambda i, nd=v.ndim: (0,) * nd) for v in vec_ins]
    out_shape = [jax.ShapeDtypeStruct((t, c), dt) for c, dt in row_outs]
    out_shape += [jax.ShapeDtypeStruct(s, F32) for s in acc_outs]
    out_specs = [pl.BlockSpec((tm, c), lambda i: (i, 0)) for c, _ in row_outs]
    out_specs += [pl.BlockSpec(s, lambda i, nd=len(s): (0,) * nd) for s in acc_outs]
    return pl.pallas_call(body, out_shape=out_shape, grid=(t // tm,), in_specs=in_specs, out_specs=out_specs,
                          name=name, compiler_params=_params(("arbitrary",)))(*row_ins, *vec_ins)


def _rms_fwd(x, g):
    return _rows("rms_fwd", [x], [g], lambda r, v: ([_rms(r[0], v[0])], []), [(x.shape[1], BF16)])[0]


def _norm_out_bwd(name, f, d_out, g, scale):
    def fn(r, v):
        dx, dg = _rms_bwd(r[0], v[0], r[1] * scale)
        return [dx], [dg]

    c = f.shape[1]
    return _rows(name, [f, d_out], [g], fn, [(c, BF16)], [(1, c)])


def _norm_in_bwd(name, h, d_res, branches):
    nb = len(branches)

    def fn(r, v):
        dh, dgs = r[1], []
        for b in range(nb):
            dx, dg = _rms_bwd(r[0], v[b], r[2 + b])
            dh = dh + dx
            dgs.append(dg)
        return [dh], dgs

    c = h.shape[1]
    return _rows(name, [h, d_res] + [dn for _, dn in branches], [g for g, _ in branches], fn, [(c, F32)],
                 [(1, c)] * nb)


def _loss_head(y, target):
    d = y.shape[1]

    def fn(r, v):
        e = r[0] - r[1]
        s = jnp.sum(jnp.sum(e * e, axis=1, keepdims=True), axis=0, keepdims=True) * (0.5 / d)
        return [e * (1.0 / d)], [jnp.broadcast_to(s, (1, 128))]

    dy, acc = _rows("loss_head", [y, target], [], fn, [(d, F32)], [(1, 128)])
    return acc[0, 0], dy


def _rope_bwd(dk, cos2, sin2):
    c = dk.shape[1]
    return _rows("rope_bwd", [dk, cos2, sin2], [], lambda r, v: ([r[0] * r[1], r[0] * r[2]], []),
                 [(c, BF16), (c, BF16)])


def _ffn_up(n, wg, wu, l, j):
    t, d = n.shape
    fs = wg.shape[-1]
    tm = min(t, ROW_TILE)
    w_spec = ((None, None, None, d, fs), lambda s, i, k: (s, l, j, 0, 0))

    def epi(accs, ex):
        g, u = accs
        return [g, u, g * _sigmoid(g) * u]

    o_spec = ((None, tm, fs), lambda s, i, k: (s, i, 0))
    outs = [((N_SHARDS, t, fs), F32, *o_spec), ((N_SHARDS, t, fs), F32, *o_spec), ((N_SHARDS, t, fs), BF16, *o_spec)]
    return _mm("ffn_up", (N_SHARDS, t // tm, 1),
               [(n, (tm, d), lambda s, i, k: (i, 0)), (wg, *w_spec), (wu, *w_spec)],
               [(0, 1, 0, "nn"), (0, 2, 1, "nn")], [(tm, fs)] * 2, outs, epi)


def _down(name, a_in, w_in, gk, h, post_g, next_gs, scale):
    t, d = h.shape
    tm = min(t, ROW_TILE)
    kn = next_gs.shape[0]

    def epi(accs, ex):
        f, hv, pg, ng = accs[0], ex[0], ex[1], ex[2]
        hn = hv + scale * _rms(f, pg)
        return [f, hn] + [_rms(hn, ng[q:q + 1]) for q in range(kn)]

    row = ((tm, d), lambda i, j, k: (i, 0))
    outs = [((t, d), F32, *row), ((t, d), F32, *row)] + [((t, d), BF16, *row)] * kn
    extras = [(h, *row), (post_g, (1, d), lambda i, j, k: (0, 0)), (next_gs, (kn, d), lambda i, j, k: (0, 0))]
    return _mm(name, (t // tm, 1, gk), [a_in, w_in], [(0, 1, 0, "nn")], [(tm, d)], outs, epi, extras)


def _ffn_down(a, wd, l, j, h, post_g, next_gs):
    t, d = h.shape
    fs = a.shape[-1]
    tm = min(t, ROW_TILE)
    return _down("ffn_down", (a, (None, tm, fs), lambda i, _, s: (s, i, 0)),
                 (wd, (None, None, None, fs, d), lambda i, _, s: (s, l, j, 0, 0)), N_SHARDS, h, post_g, next_gs, 0.5)


def _ffn_dact(df, wd, l, j, g, u):
    t, d = df.shape
    fs = g.shape[-1]
    tm = min(t, ROW_TILE)

    def epi(accs, ex):
        da, gv, uv = accs[0], ex[0], ex[1]
        sg = _sigmoid(gv)
        return [da * uv * (sg * (1.0 + gv * (1.0 - sg))), da * (gv * sg)]

    o_spec = ((None, tm, fs), lambda s, i, k: (s, i, 0))
    outs = [((N_SHARDS, t, fs), BF16, *o_spec)] * 2
    return _mm("ffn_dact", (N_SHARDS, t // tm, 1),
               [(df, (tm, d), lambda s, i, k: (i, 0)), (wd, (None, None, None, fs, d), lambda s, i, k: (s, l, j, 0, 0))],
               [(0, 1, 0, "nt")], [(tm, fs)], outs, epi, [(g, *o_spec), (u, *o_spec)])


def _ffn_dn(dg, du, wg, wu, l, j):
    _, t, fs = dg.shape
    d = wg.shape[-2]
    tm = min(t, ROW_TILE)
    a_spec = ((None, tm, fs), lambda i, _, s: (s, i, 0))
    w_spec = ((None, None, None, d, fs), lambda i, _, s: (s, l, j, 0, 0))
    outs = [((t, d), F32, (tm, d), lambda i, _, s: (i, 0))]
    return _mm("ffn_dn", (t // tm, 1, N_SHARDS), [(dg, *a_spec), (wg, *w_spec), (du, *a_spec), (wu, *w_spec)],
               [(0, 1, 0, "nt"), (2, 3, 0, "nt")], [(tm, d)], outs, lambda accs, ex: accs)[0]


def _ffn_dw_in(n, dg, du):
    _, t, fs = dg.shape
    d = n.shape[1]
    tk = min(t, ROW_TILE)
    b_spec = ((None, tk, fs), lambda s, _, k: (s, k, 0))
    o_spec = ((None, d, fs), lambda s, _, k: (s, 0, 0))
    outs = [((N_SHARDS, d, fs), BF16, *o_spec)] * 2
    return _mm("ffn_dw_in", (N_SHARDS, 1, t // tk), [(n, (tk, d), lambda s, _, k: (k, 0)), (dg, *b_spec), (du, *b_spec)],
               [(0, 1, 0, "tn"), (0, 2, 1, "tn")], [(d, fs)] * 2, outs, lambda accs, ex: accs)


def _ffn_dw_down(a, df):
    _, t, fs = a.shape
    d = df.shape[1]
    tk = min(t, ROW_TILE)
    outs = [((N_SHARDS, fs, d), BF16, (None, fs, d), lambda s, _, k: (s, 0, 0))]
    return _mm("ffn_dw_down", (N_SHARDS, 1, t // tk),
               [(a, (None, tk, fs), lambda s, _, k: (s, k, 0)), (df, (tk, d), lambda s, _, k: (k, 0))],
               [(0, 1, 0, "tn")], [(fs, d)], outs, lambda accs, ex: accs)[0]


def _causal_weight(w):
    row = lax.broadcasted_iota(jnp.int32, (CHUNK, CHUNK), 0)
    col = lax.broadcasted_iota(jnp.int32, (CHUNK, CHUNK), 1)
    return row >= col, jnp.where(row >= col, w, 0.0).astype(BF16)


def _layer_norm(v, g, b):
    xc = v - jnp.mean(v, axis=-1, keepdims=True)
    rstd = lax.rsqrt(jnp.mean(xc * xc, axis=-1, keepdims=True) + LN_EPS)
    xhat = xc * rstd
    return xhat, rstd, xhat * g + b


def _sgu_specs(t, half, tm):
    return [pl.BlockSpec((tm, half), lambda i: (i, 0)), pl.BlockSpec((tm, half), lambda i: (i, 1))]


def _sgu_fwd(zp, ln_g, ln_b, w_s, bsb):
    t, half = zp.shape[0], zp.shape[1] // 2
    tm = min(t, 2 * CHUNK)

    def body(u_ref, v_ref, g_ref, b_ref, w_ref, bs_ref, o_ref):
        u = _gelu(u_ref[...])
        _, _, vn = _layer_norm(_gelu(v_ref[...]), g_ref[...], b_ref[...])
        vb = vn.astype(BF16)
        for g in range(GROUPS):
            _, wm = _causal_weight(w_ref[g])
            cols = slice(g * CHUNK, (g + 1) * CHUNK)
            for c in range(tm // CHUNK):
                rows = slice(c * CHUNK, (c + 1) * CHUNK)
                sv = jnp.dot(wm, vb[rows, cols], preferred_element_type=F32) + bs_ref[g]
                o_ref[rows, cols] = (u[rows, cols] * sv).astype(BF16)

    whole = lambda a: pl.BlockSpec(a.shape, lambda i, nd=a.ndim: (0,) * nd)
    return pl.pallas_call(
        body, out_shape=jax.ShapeDtypeStruct((t, half), BF16), grid=(t // tm,),
        in_specs=_sgu_specs(t, half, tm) + [whole(ln_g), whole(ln_b), whole(w_s), whole(bsb)],
        out_specs=pl.BlockSpec((tm, half), lambda i: (i, 0)), name="sgu_fwd",
        compiler_params=_params(("arbitrary",)))(zp, zp, ln_g, ln_b, w_s, bsb)


def _sgu_bwd(zp, d_uv, ln_g, ln_b, w_s, bsb):
    t, half = zp.shape[0], zp.shape[1] // 2
    tm = min(t, 2 * CHUNK)

    def body(u_ref, v_ref, d_ref, g_ref, b_ref, w_ref, bs_ref, dz_ref, dlg_ref, dlb_ref, dws_ref, dbs_ref, dvn_ref):
        i = pl.program_id(0)

        @pl.when(i == 0)
        def _():
            dlg_ref[...] = jnp.zeros_like(dlg_ref)
            dlb_ref[...] = jnp.zeros_like(dlb_ref)
            dws_ref[...] = jnp.zeros_like(dws_ref)
            dbs_ref[...] = jnp.zeros_like(dbs_ref)

        up, vp = u_ref[...], v_ref[...]
        u, gup = _gelu(up), _gelu_grad(up)
        xhat, rstd, vn = _layer_norm(_gelu(vp), g_ref[...], b_ref[...])
        vb = vn.astype(BF16)
        d = d_ref[...]
        for g in range(GROUPS):
            mask, wm = _causal_weight(w_ref[g])
            cols = slice(g * CHUNK, (g + 1) * CHUNK)
            for c in range(tm // CHUNK):
                rows = slice(c * CHUNK, (c + 1) * CHUNK)
                blk = vb[rows, cols]
                sv = jnp.dot(wm, blk, preferred_element_type=F32) + bs_ref[g]
                dblk = d[rows, cols]
                dz_ref[rows, cols] = (dblk * sv * gup[rows, cols]).astype(BF16)
                dsv = dblk * u[rows, cols]
                dsvb = dsv.astype(BF16)
                dvn_ref[rows, cols] = lax.dot_general(wm, dsvb, _DN["tn"], preferred_element_type=F32)
                dw = lax.dot_general(dsvb, blk, _DN["nt"], preferred_element_type=F32)
                dws_ref[g] += jnp.where(mask, dw, 0.0)
                dbs_ref[g] += jnp.sum(dsv, axis=1, keepdims=True)
        dvn = dvn_ref[...]
        dlg_ref[...] += jnp.sum(dvn * xhat, axis=0, keepdims=True)
        dlb_ref[...] += jnp.sum(dvn, axis=0, keepdims=True)
        dxh = dvn * g_ref[...]
        dv = rstd * (dxh - jnp.mean(dxh, axis=-1, keepdims=True)
                     - xhat * jnp.mean(dxh * xhat, axis=-1, keepdims=True))
        dz_ref[:, half:] = (dv * _gelu_grad(vp)).astype(BF16)

    whole = lambda a: pl.BlockSpec(a.shape, lambda i, nd=a.ndim: (0,) * nd)
    wshape = lambda s: pl.BlockSpec(s, lambda i, nd=len(s): (0,) * nd)
    out_shape = [jax.ShapeDtypeStruct((t, 2 * half), BF16), jax.ShapeDtypeStruct((1, half), F32),
                 jax.ShapeDtypeStruct((1, half), F32), jax.ShapeDtypeStruct(w_s.shape, F32),
                 jax.ShapeDtypeStruct((GROUPS, CHUNK, 1), F32)]
    return pl.pallas_call(
        body, out_shape=out_shape, grid=(t // tm,),
        in_specs=_sgu_specs(t, half, tm) + [pl.BlockSpec((tm, half), lambda i: (i, 0)), whole(ln_g), whole(ln_b),
                                            whole(w_s), whole(bsb)],
        out_specs=[pl.BlockSpec((tm, 2 * half), lambda i: (i, 0)), wshape((1, half)), wshape((1, half)),
                   wshape(w_s.shape), wshape((GROUPS, CHUNK, 1))],
        scratch_shapes=[pltpu.VMEM((tm, half), F32)], name="sgu_bwd",
        compiler_params=_params(("arbitrary",)))(zp, zp, d_uv, ln_g, ln_b, w_s, bsb)


_SCALE = (QK_NOPE + QK_ROPE) ** -0.5


def _attn_scores(qn, qr, kn, kr, i, tq, n):
    s = lax.dot_general(qn, kn, _DN["nt"], preferred_element_type=F32)
    s = (s + lax.dot_general(qr, kr, _DN["nt"], preferred_element_type=F32)) * _SCALE
    row = i * tq + lax.broadcasted_iota(jnp.int32, (tq, n), 0)
    col = lax.broadcasted_iota(jnp.int32, (tq, n), 1)
    return jnp.where(col <= row, s, NEG_INF)


def _attn_specs(seq):
    head = lambda b, h: (b, h)
    return dict(
        qn=pl.BlockSpec((seq, QK_NOPE), head),
        qr=pl.BlockSpec((None, seq, QK_ROPE), lambda b, h: (h, b, 0)),
        kr=pl.BlockSpec((seq, QK_ROPE), lambda b, h: (b, 0)),
        lse=pl.BlockSpec((None, seq, 1), lambda b, h: (h, b, 0)),
    )


def _attn_fwd(qn, qr, kn, v, kr, seq):
    t = qn.shape[0]
    tq = min(seq, 2 * CHUNK)
    sp = _attn_specs(seq)

    def body(qn_ref, qr_ref, kn_ref, v_ref, kr_ref, o_ref, lse_ref):
        for i in range(seq // tq):
            rows, n = slice(i * tq, (i + 1) * tq), (i + 1) * tq
            s = _attn_scores(qn_ref[rows, :], qr_ref[rows, :], kn_ref[0:n, :], kr_ref[0:n, :], i, tq, n)
            m = jnp.max(s, axis=-1, keepdims=True)
            p = jnp.exp(s - m)
            l = jnp.sum(p, axis=-1, keepdims=True)
            o_ref[rows, :] = jnp.dot((p / l).astype(BF16), v_ref[0:n, :], preferred_element_type=F32).astype(BF16)
            lse_ref[rows, :] = m + jnp.log(l)

    return pl.pallas_call(
        body, out_shape=[jax.ShapeDtypeStruct((t, N_HEADS * V_DIM), BF16), jax.ShapeDtypeStruct((N_HEADS, t, 1), F32)],
        grid=(t // seq, N_HEADS), in_specs=[sp["qn"], sp["qr"], sp["qn"], sp["qn"], sp["kr"]],
        out_specs=[sp["qn"], sp["lse"]], name="attn_fwd",
        compiler_params=_params(("parallel", "arbitrary")))(qn, qr, kn, v, kr)


def _attn_bwd(qn, qr, kn, v, kr, do, lse, cos2, sin2, seq):
    t = qn.shape[0]
    tq = min(seq, 2 * CHUNK)
    sp = _attn_specs(seq)

    def body(qn_ref, qr_ref, kn_ref, v_ref, kr_ref, do_ref, lse_ref, cos_ref, sin_ref,
             dqn_ref, dkn_ref, dv_ref, dqc_ref, dqs_ref, dkr_ref, dk_acc, dv_acc, dkr_acc):
        dk_acc[...] = jnp.zeros_like(dk_acc)
        dv_acc[...] = jnp.zeros_like(dv_acc)
        dkr_acc[...] = jnp.zeros_like(dkr_acc)
        for i in range(seq // tq):
            rows, n = slice(i * tq, (i + 1) * tq), (i + 1) * tq
            q_n, q_r, d_o = qn_ref[rows, :], qr_ref[rows, :], do_ref[rows, :]
            k_n, k_r = kn_ref[0:n, :], kr_ref[0:n, :]
            s = _attn_scores(q_n, q_r, k_n, k_r, i, tq, n)
            p = jnp.exp(s - lse_ref[rows, :])
            dp = lax.dot_general(d_o, v_ref[0:n, :], _DN["nt"], preferred_element_type=F32)
            ds = (p * (dp - jnp.sum(p * dp, axis=-1, keepdims=True)) * _SCALE).astype(BF16)
            dqn_ref[rows, :] = jnp.dot(ds, k_n, preferred_element_type=F32).astype(BF16)
            dqr = jnp.dot(ds, k_r, preferred_element_type=F32)
            dqc_ref[rows, :] = (dqr * cos_ref[rows, :]).astype(BF16)
            dqs_ref[rows, :] = (dqr * sin_ref[rows, :]).astype(BF16)
            dk_acc[0:n, :] += lax.dot_general(ds, q_n, _DN["tn"], preferred_element_type=F32)
            dkr_acc[0:n, :] += lax.dot_general(ds, q_r, _DN["tn"], preferred_element_type=F32)
            dv_acc[0:n, :] += lax.dot_general(p.astype(BF16), d_o, _DN["tn"], preferred_element_type=F32)
        dkn_ref[...] = dk_acc[...].astype(BF16)
        dv_ref[...] = dv_acc[...].astype(BF16)
        h = pl.program_id(1)

        @pl.when(h == 0)
        def _():
            dkr_ref[...] = dkr_acc[...]

        @pl.when(h > 0)
        def _():
            dkr_ref[...] += dkr_acc[...]

    wide = jax.ShapeDtypeStruct((t, N_HEADS * V_DIM), BF16)
    rope = jax.ShapeDtypeStruct((N_HEADS, t, QK_ROPE), BF16)
    krf = pl.BlockSpec((seq, QK_ROPE), lambda b, h: (b, 0))
    return pl.pallas_call(
        body, out_shape=[wide, wide, wide, rope, rope, jax.ShapeDtypeStruct((t, QK_ROPE), F32)],
        grid=(t // seq, N_HEADS),
        in_specs=[sp["qn"], sp["qr"], sp["qn"], sp["qn"], sp["kr"], sp["qn"], sp["lse"], krf, krf],
        out_specs=[sp["qn"], sp["qn"], sp["qn"], sp["qr"], sp["qr"], krf],
        scratch_shapes=[pltpu.VMEM((seq, QK_NOPE), F32), pltpu.VMEM((seq, V_DIM), F32), pltpu.VMEM((seq, QK_ROPE), F32)],
        name="attn_bwd", compiler_params=_params(("parallel", "arbitrary")))(qn, qr, kn, v, kr, do, lse, cos2, sin2)


def _row_tile(rows, cols, row_mult=8):
    cap = max(row_mult, (1 << 18) // cols)
    best = rows
    for tr in range(row_mult, min(rows, cap) + 1, row_mult):
        if rows % tr == 0:
            best = tr
    return best if rows > cap else rows


def _adamw_math(w, g, m, v):
    mv = ADAM_B1 * m + (1.0 - ADAM_B1) * g
    vv = ADAM_B2 * v + (1.0 - ADAM_B2) * (g * g)
    m_hat = mv / (1.0 - ADAM_B1 ** ADAM_STEP)
    v_hat = vv / (1.0 - ADAM_B2 ** ADAM_STEP)
    return -ADAM_LR * (m_hat / (jnp.sqrt(v_hat) + ADAM_EPS) + ADAM_WD * w), mv, vv


def _adamw(w, g, m, v):
    shape = w.shape
    c = shape[-1]
    r = w.size // c
    tr = _row_tile(r, c)

    def body(w_ref, g_ref, m_ref, v_ref, d_ref, nm_ref, nv_ref):
        d_ref[...], nm_ref[...], nv_ref[...] = _adamw_math(w_ref[...], g_ref[...], m_ref[...], v_ref[...])

    spec = pl.BlockSpec((tr, c), lambda i: (i, 0))
    outs = pl.pallas_call(body, out_shape=[jax.ShapeDtypeStruct((r, c), F32)] * 3, grid=(r // tr,),
                          in_specs=[spec] * 4, out_specs=[spec] * 3, name="adamw",
                          compiler_params=_params(("parallel",)))(*[a.reshape(r, c) for a in (w, g, m, v)])
    return [o.reshape(shape) for o in outs]


def _adamw_halves(w, own, recv, m, v, core):
    nl, nj, rows, c = w.shape
    r = rows // 2
    tr = _row_tile(r, c)

    def body(core_ref, w_ref, own_ref, recv_ref, m_ref, v_ref, g_ref, d_ref, nm_ref, nv_ref):
        g = jnp.where(pl.program_id(2) == core_ref[0], own_ref[...], recv_ref[...])
        g_ref[...] = g
        d_ref[...], nm_ref[...], nv_ref[...] = _adamw_math(w_ref[...], g, m_ref[...], v_ref[...])

    full = pl.BlockSpec((None, None, None, tr, c), lambda a, b, h, i, cr: (a, b, h, i, 0))
    half = pl.BlockSpec((None, None, tr, c), lambda a, b, h, i, cr: (a, b, i, 0))
    grid_spec = pltpu.PrefetchScalarGridSpec(num_scalar_prefetch=1, grid=(nl, nj, 2, r // tr),
                                             in_specs=[full, half, half, full, full], out_specs=[full] * 4)
    split = lambda a: a.reshape(nl, nj, 2, r, c)
    outs = pl.pallas_call(body, out_shape=[jax.ShapeDtypeStruct((nl, nj, 2, r, c), F32)] * 4, grid_spec=grid_spec,
                          name="adamw_halves", compiler_params=_params(("parallel",) * 4))(
                              core, split(w), own, recv, split(m), split(v))
    return [o.reshape(w.shape) for o in outs]


def _place():
    x, y, c = lax.axis_index("x"), lax.axis_index("y"), lax.axis_index("c")
    return x, y, c, [(1 - x, y), (x, 1 - y), (1 - x, 1 - y)]


def _dma_sems(*counts):
    return [pltpu.SemaphoreType.DMA((n,)) for n in counts]


def _all_gather(bufs):
    n = len(bufs)

    def body(*refs):
        ins, outs = refs[:n], refs[n:2 * n]
        send, recv, fsend, frecv, osend, orecv = refs[2 * n:]
        x, y, c, _ = _place()
        xn, yn, sib = (1 - x, y, c), (x, 1 - y, c), (x, y, 1 - c)
        k, kx, ky, kd = 2 * x + y, 2 * (1 - x) + y, 2 * x + 1 - y, 2 * (1 - x) + 1 - y

        def copy(src, dst, sems, i, to):
            return pltpu.make_async_remote_copy(src, dst, sems[0].at[i], sems[1].at[i], device_id=to, device_id_type=_MESH)

        ici, d2d, own_s = (send, recv), (fsend, frecv), (osend, orecv)
        started = [copy(ins[b], outs[b].at[k], own_s, b, sib) for b in range(n)]
        for first in (True, False):
            for b in range(n):
                mine = outs[b].at[k, c]
                if first:
                    started += [copy(ins[b].at[c, 0], mine.at[0], ici, 6 * b, xn), copy(ins[b].at[c, 1], mine.at[1], ici, 6 * b + 1, yn)]
                else:
                    started += [copy(ins[b].at[c, 1], mine.at[1], ici, 6 * b + 2, xn), copy(ins[b].at[c, 0], mine.at[0], ici, 6 * b + 3, yn)]
        for cp in started:
            cp.start()
        passed = []
        for b in range(n):
            for i, (src_chip, q, to) in enumerate([(kx, 0, yn), (ky, 1, xn)]):
                piece = outs[b].at[src_chip, c, q]
                copy(piece, piece, ici, 6 * b + i, to).wait_recv()
                cp = copy(piece, piece, ici, 6 * b + 4 + i, to)
                cp.start()
                passed.append(cp)
        for b in range(n):
            for i, (src_chip, q) in enumerate([(kx, 1), (ky, 0)]):
                piece = outs[b].at[src_chip, c, q]
                copy(piece, piece, ici, 6 * b + 2 + i, xn).wait_recv()
                half = outs[b].at[src_chip, c]
                cp = copy(half, half, d2d, 3 * b + i, sib)
                cp.start()
                passed.append(cp)
        for b in range(n):
            for i, q in enumerate([0, 1]):
                piece = outs[b].at[kd, c, q]
                copy(piece, piece, ici, 6 * b + 4 + i, xn).wait_recv()
            half = outs[b].at[kd, c]
            cp = copy(half, half, d2d, 3 * b + 2, sib)
            cp.start()
            passed.append(cp)
        for b in range(n):
            for i, src_chip in enumerate([kx, ky, kd]):
                half = outs[b].at[src_chip, 1 - c]
                copy(half, half, d2d, 3 * b + i, sib).wait_recv()
        for cp in started[n:] + passed:
            cp.wait_send()
        for cp in started[:n]:
            cp.wait()

    return pl.pallas_call(
        body, out_shape=[jax.ShapeDtypeStruct((N_SHARDS,) + b.shape, b.dtype) for b in bufs],
        in_specs=[_ANY] * n, out_specs=[_ANY] * n, scratch_shapes=_dma_sems(6 * n, 6 * n, 3 * n, 3 * n, n, n),
        name="all_gather_weights")(*bufs)


def _swap_halves(parts):
    n = len(parts)

    def body(*refs):
        ins, outs = refs[:n], refs[n:2 * n]
        send, recv = refs[2 * n:]
        x, y, c, _ = _place()
        cps = [pltpu.make_async_remote_copy(ins[b].at[:, pl.ds(1 - c, 1)], outs[b], send.at[b], recv.at[b],
                                            device_id=(x, y, 1 - c), device_id_type=_MESH) for b in range(n)]
        for cp in cps:
            cp.start()
        for cp in cps:
            cp.wait()

    return pl.pallas_call(
        body, out_shape=[jax.ShapeDtypeStruct((N_SHARDS, 1) + p.shape[2:], p.dtype) for p in parts],
        in_specs=[_ANY] * n, out_specs=[_ANY] * n, scratch_shapes=_dma_sems(n, n), name="grad_swap_halves")(*parts)


def _add_half(part, other, core):
    _, _, r, c = part.shape
    tr = _row_tile(r, c, 16)

    def body(core_ref, p_ref, o_ref, out_ref):
        out_ref[...] = (p_ref[...].astype(F32) + o_ref[...].astype(F32)).astype(out_ref.dtype)

    grid_spec = pltpu.PrefetchScalarGridSpec(
        num_scalar_prefetch=1, grid=(N_SHARDS, r // tr),
        in_specs=[pl.BlockSpec((None, None, tr, c), lambda k, i, cr: (k, cr[0], i, 0)),
                  pl.BlockSpec((None, None, tr, c), lambda k, i, cr: (k, 0, i, 0))],
        out_specs=pl.BlockSpec((None, tr, c), lambda k, i, cr: (k, i, 0)))
    return pl.pallas_call(body, out_shape=jax.ShapeDtypeStruct((N_SHARDS, r, c), part.dtype), grid_spec=grid_spec,
                          name="grad_add_half", compiler_params=_params(("parallel", "parallel")))(core, part, other)


def _scatter_chips(parts):
    n = len(parts)

    def body(*refs):
        ins, outs = refs[:n], refs[n:2 * n]
        send, recv = refs[2 * n:]
        x, y, c, chips = _place()
        k = 2 * x + y
        started = []
        for b in range(n):
            for j, (px, py) in enumerate(chips):
                cp = pltpu.make_async_remote_copy(ins[b].at[2 * px + py], outs[b].at[k], send.at[3 * b + j],
                                                  recv.at[3 * b + j], device_id=(px, py, c), device_id_type=_MESH)
                cp.start()
                started.append(cp)
        for b in range(n):
            for j, (px, py) in enumerate(chips):
                got = outs[b].at[2 * px + py]
                pltpu.make_async_remote_copy(got, got, send.at[3 * b + j], recv.at[3 * b + j],
                                             device_id=(px, py, c), device_id_type=_MESH).wait_recv()
        for cp in started:
            cp.wait_send()

    return pl.pallas_call(
        body, out_shape=[jax.ShapeDtypeStruct(p.shape, p.dtype) for p in parts],
        in_specs=[_ANY] * n, out_specs=[_ANY] * n, scratch_shapes=_dma_sems(3 * n, 3 * n),
        name="grad_scatter_chips")(*parts)


def _sum_slots(slots, mine, chip):
    _, r, c = slots.shape
    tr = _row_tile(r, c, 16)

    def body(chip_ref, s0, s1, s2, s3, own_ref, out_ref):
        own = own_ref[...].astype(F32)
        v = [jnp.where(chip_ref[0] == s, own, ref[...].astype(F32)) for s, ref in enumerate((s0, s1, s2, s3))]
        out_ref[...] = ((v[0] + v[1]) + v[2]) + v[3]

    def slot_spec(s):
        return pl.BlockSpec((None, tr, c), lambda i, kr: (jnp.where(kr[0] == s, (s + 1) % N_SHARDS, s), i, 0))

    grid_spec = pltpu.PrefetchScalarGridSpec(
        num_scalar_prefetch=1, grid=(r // tr,),
        in_specs=[slot_spec(s) for s in range(N_SHARDS)] + [pl.BlockSpec((None, tr, c), lambda i, kr: (kr[0], i, 0))],
        out_specs=pl.BlockSpec((tr, c), lambda i, kr: (i, 0)))
    return pl.pallas_call(body, out_shape=jax.ShapeDtypeStruct((r, c), F32), grid_spec=grid_spec, name="grad_sum_slots",
                          compiler_params=_params(("parallel",)))(chip, slots, slots, slots, slots, mine)


def _join_halves(halves, out_shapes, where):
    n, no = len(halves), len(out_shapes)

    def body(*refs):
        ins, outs = refs[:n], refs[n:n + no]
        send, recv = refs[n + no:]
        x, y, c, _ = _place()
        cps = []
        for b in range(n):
            o, lead = where[b]
            dst = outs[o].at[lead] if lead else outs[o]
            cps.append(pltpu.make_async_remote_copy(ins[b], dst, send.at[b], recv.at[b],
                                                    device_id=(x, y, 1 - c), device_id_type=_MESH))
        for cp in cps:
            cp.start()
        for cp in cps:
            cp.wait()

    return pl.pallas_call(
        body, out_shape=[jax.ShapeDtypeStruct(s, F32) for s in out_shapes], in_specs=[_ANY] * n,
        out_specs=[_ANY] * no, scratch_shapes=_dma_sems(n, n), name="grad_join_halves")(*halves)


def _gather_rows(buf, start, rows):
    def body(in_ref, out_ref, send, recv, lsem):
        x, y, c, chips = _place()
        k = 2 * x + y
        src = in_ref.at[pl.ds(start, rows)]
        local = pltpu.make_async_copy(src, out_ref.at[k], lsem.at[0])
        local.start()
        cps = [pltpu.make_async_remote_copy(src, out_ref.at[k], send.at[j], recv.at[j], device_id=(px, py, c),
                                            device_id_type=_MESH) for j, (px, py) in enumerate(chips)]
        for cp in cps:
            cp.start()
        for j, (px, py) in enumerate(chips):
            got = out_ref.at[2 * px + py]
            pltpu.make_async_remote_copy(got, got, send.at[j], recv.at[j], device_id=(px, py, c),
                                         device_id_type=_MESH).wait_recv()
        for cp in cps:
            cp.wait_send()
        local.wait()

    return pl.pallas_call(body, out_shape=jax.ShapeDtypeStruct((N_SHARDS, rows, buf.shape[1]), F32),
                          in_specs=[_ANY], out_specs=_ANY, scratch_shapes=_dma_sems(3, 3, 1),
                          name="gather_replicated_grads")(buf)


def _round_up(n, m):
    return -(-n // m) * m


def _pack_flat(vecs, rows, width, dtype):
    flat = jnp.concatenate([v.reshape(-1).astype(dtype) for v in vecs])
    return jnp.pad(flat, (0, rows * width - flat.size)).reshape(rows, width)


def _split_flat(flat, shapes):
    out, off = [], 0
    for s in shapes:
        n = math.prod(s)
        out.append(flat[off:off + n].reshape(s))
        off += n
    return out


def _merge_shards(arr4, axis):
    a = jnp.moveaxis(arr4, 0, axis)
    s = list(a.shape)
    return a.reshape(s[:axis] + [s[axis] * s[axis + 1]] + s[axis + 2:])


def _split_shards(full, axis):
    s = list(full.shape)
    a = full.reshape(s[:axis] + [N_SHARDS, s[axis] // N_SHARDS] + s[axis + 1:])
    return jnp.moveaxis(a, axis, 0).reshape(N_SHARDS, -1)


def _rot_cols(w):
    half = w.shape[-1] // 2
    return jnp.concatenate([-w[..., half:], w[..., :half]], axis=-1)


def _unrot_cols(dw):
    half = dw.shape[-1] // 2
    return jnp.concatenate([dw[..., half:], -dw[..., :half]], axis=-1)


def kernel(x, positions, ffn_pre_g, ffn_post_g, ffn_w_gate, ffn_w_up, ffn_w_down, mix_pre_g, mix_post_g, gmlp_w_in, gmlp_ln_g, gmlp_ln_b, gmlp_w_s, gmlp_b_s, gmlp_w_out, kv_norm_g, w_dkv, kv_a_norm_g, w_ukv, mla_w_dq, mla_q_norm_g, mla_w_uq, mla_w_o, loss_target, m_ffn_pre_g, m_ffn_post_g, m_ffn_w_gate, m_ffn_w_up, m_ffn_w_down, m_mix_pre_g, m_mix_post_g, m_gmlp_w_in, m_gmlp_ln_g, m_gmlp_ln_b, m_gmlp_w_s, m_gmlp_b_s, m_gmlp_w_out, m_kv_norm_g, m_w_dkv, m_kv_a_norm_g, m_w_ukv, m_mla_w_dq, m_mla_q_norm_g, m_mla_w_uq, m_mla_w_o, v_ffn_pre_g, v_ffn_post_g, v_ffn_w_gate, v_ffn_w_up, v_ffn_w_down, v_mix_pre_g, v_mix_post_g, v_gmlp_w_in, v_gmlp_ln_g, v_gmlp_ln_b, v_gmlp_w_s, v_gmlp_b_s, v_gmlp_w_out, v_kv_norm_g, v_w_dkv, v_kv_a_norm_g, v_w_ukv, v_mla_w_dq, v_mla_q_norm_g, v_mla_w_uq, v_mla_w_o):
    names = ["ffn_pre_g", "ffn_post_g", "ffn_w_gate", "ffn_w_up", "ffn_w_down", "mix_pre_g", "mix_post_g", "gmlp_w_in",
             "gmlp_ln_g", "gmlp_ln_b", "gmlp_w_s", "gmlp_b_s", "gmlp_w_out", "kv_norm_g", "w_dkv", "kv_a_norm_g", "w_ukv",
             "mla_w_dq", "mla_q_norm_g", "mla_w_uq", "mla_w_o"]
    env = locals()
    w = {n: env[n] for n in names}
    mom = {n: env["m_" + n] for n in names}
    var = {n: env["v_" + n] for n in names}

    bsz, seq, d = x.shape
    t = bsz * seq
    core = lax.axis_index("c").astype(jnp.int32).reshape(1)

    mats = [("gmlp_w_in", 2), ("gmlp_w_out", 1), ("w_dkv", 0), ("w_ukv", 1), ("mla_w_dq", 1), ("mla_w_uq", 2),
            ("mla_w_o", 1)]
    vecs = [("ffn_pre_g", 2), ("ffn_post_g", 2), ("gmlp_ln_g", 1), ("gmlp_ln_b", 1)]
    replicated = ["mix_pre_g", "mix_post_g", "gmlp_w_s", "gmlp_b_s", "kv_norm_g", "kv_a_norm_g", "mla_q_norm_g"]
    n_mats = sum(w[n].size for n, _ in mats)
    n_vecs = sum(w[n].size for n, _ in vecs)
    mat_rows = _round_up(-(-n_mats // PACK_WIDTH), 64)
    vec_rows = _round_up(-(-n_vecs // 128), 32)
    mat_pack = _pack_flat([w[n] for n, _ in mats], mat_rows, PACK_WIDTH, BF16).reshape(2, 2, mat_rows // 4, PACK_WIDTH)
    vec_pack = _pack_flat([w[n] for n, _ in vecs], vec_rows, 128, F32).reshape(2, 2, vec_rows // 4, 128)
    wg_all, wu_all, wd_all, mat_all, vec_all = _all_gather(
        [w["ffn_w_gate"].astype(BF16), w["ffn_w_up"].astype(BF16), w["ffn_w_down"].astype(BF16), mat_pack, vec_pack])

    full = {}
    flat4 = mat_all.reshape(N_SHARDS, -1)
    off = 0
    for n, ax in mats:
        full[n] = _merge_shards(flat4[:, off:off + w[n].size].reshape((N_SHARDS,) + w[n].shape), ax)
        off += w[n].size
    flat4 = vec_all.reshape(N_SHARDS, -1)
    off = 0
    for n, ax in vecs:
        full[n] = _merge_shards(flat4[:, off:off + w[n].size].reshape((N_SHARDS,) + w[n].shape), ax)
        off += w[n].size

    w_in, w_out = full["gmlp_w_in"][0], full["gmlp_w_out"][0]
    ln_g, ln_b = full["gmlp_ln_g"], full["gmlp_ln_b"]
    w_c, w_kr = full["w_dkv"][:, :KV_RANK], full["w_dkv"][:, KV_RANK:]
    w_kr_rot = _rot_cols(w_kr)
    ukv = full["w_ukv"].reshape(KV_RANK, N_HEADS, 2, QK_NOPE)
    w_k, w_v = ukv[:, :, 0].reshape(KV_RANK, -1), ukv[:, :, 1].reshape(KV_RANK, -1)
    w_dq, w_o = full["mla_w_dq"][0], full["mla_w_o"][0]
    q_rank = w_dq.shape[1]
    uq = full["mla_w_uq"][0].reshape(q_rank, N_HEADS, QK_NOPE + QK_ROPE)
    w_qn = uq[:, :, :QK_NOPE].reshape(q_rank, -1)
    w_qr = uq[:, :, QK_NOPE:].reshape(q_rank, -1)
    w_qr_rot = _rot_cols(uq[:, :, QK_NOPE:]).reshape(q_rank, -1)
    pre_g, post_g = full["ffn_pre_g"], full["ffn_post_g"]
    w_s = w["gmlp_w_s"][0]
    bsb = jnp.broadcast_to(w["gmlp_b_s"][0][:, :, None], (GROUPS, CHUNK, CHUNK))
    row = lambda v: v.reshape(1, -1)

    inv_freq = ROPE_THETA ** (-jnp.arange(0, QK_ROPE, 2, dtype=F32) / QK_ROPE)
    ang = positions.astype(F32).reshape(t, 1) * inv_freq
    cos2 = jnp.concatenate([jnp.cos(ang)] * 2, axis=-1)
    sin2 = jnp.concatenate([jnp.sin(ang)] * 2, axis=-1)
    cos_h, sin_h = jnp.tile(cos2, (1, N_HEADS)), jnp.tile(sin2, (1, N_HEADS))

    def rope_epi(n_lin):
        def epi(accs, ex):
            return accs[:n_lin] + [accs[n_lin] * ex[0] + accs[n_lin + 1] * ex[1]]
        return epi

    h0 = x.reshape(t, d)
    saved = {}

    def ffn_fwd(l, j, h, n, next_gs):
        g, u, a = _ffn_up(n, wg_all, wu_all, l, j)
        f, h_new, *n_next = _ffn_down(a, wd_all, l, j, h, row(post_g[l, j]), next_gs)
        saved[("ffn", l, j)] = (h, n, g, u, a, f)
        return h_new, n_next

    n0 = _rms_fwd(h0, row(pre_g[0, 0]))
    h1, (n1,) = ffn_fwd(0, 0, h0, n0, row(w["mix_pre_g"][0]))
    zp = _mm2d("gmlp_in", [(n1, w_in, "nn", 0)], [(w_in.shape[1], F32)])[0]
    uv = _sgu_fwd(zp, ln_g, ln_b, w_s, bsb)
    half = uv.shape[1]
    tm = min(t, ROW_TILE)
    m0, h2, n2 = _down("gmlp_out", (uv, (tm, 512), lambda i, _, k: (i, k)), (w_out, (512, d), lambda i, _, k: (k, 0)),
                       half // 512, h1, row(w["mix_post_g"][0]), row(pre_g[0, 1]), 1.0)
    h3, (n3kv, n3) = ffn_fwd(0, 1, h2, n2, jnp.stack([w["kv_norm_g"], pre_g[1, 0]]))

    def kv_epi(accs, ex):
        c_raw = accs[0]
        return [c_raw, _rms(c_raw, ex[2]), accs[1] * ex[0] + accs[2] * ex[1]]

    c_raw, c_n, k_r = _mm2d("kv_down", [(n3kv, w_c, "nn", 0), (n3kv, w_kr, "nn", 1), (n3kv, w_kr_rot, "nn", 2)],
                            [(KV_RANK, F32), (KV_RANK, BF16), (QK_ROPE, BF16)], kv_epi, [cos2, sin2],
                            [row(w["kv_a_norm_g"])])
    k_n, v_h = _mm2d("kv_up", [(c_n, w_k, "nn", 0), (c_n, w_v, "nn", 1)], [(w_k.shape[1], BF16), (w_v.shape[1], BF16)])

    h4, (n4,) = ffn_fwd(1, 0, h3, n3, row(w["mix_pre_g"][1]))
    qd, qn = _mm2d("q_down", [(n4, w_dq, "nn", 0)], [(q_rank, F32), (q_rank, BF16)],
                   lambda accs, ex: [accs[0], _rms(accs[0], ex[0])], [], [row(w["mla_q_norm_g"][0])])
    q_n, q_r = _mm2d("q_up", [(qn, w_qn, "nn", 0), (qn, w_qr, "nn", 1), (qn, w_qr_rot, "nn", 2)],
                     [(w_qn.shape[1], BF16), (w_qr.shape[1], BF16)], rope_epi(1), [cos_h, sin_h])
    q_r = q_r.reshape(t, N_HEADS, QK_ROPE).transpose(1, 0, 2)
    o, lse = _attn_fwd(q_n, q_r, k_n, v_h, k_r, seq)
    m1, h5, n5 = _down("attn_out", (o, (tm, 512), lambda i, _, k: (i, k)), (w_o, (512, d), lambda i, _, k: (k, 0)),
                       o.shape[1] // 512, h4, row(w["mix_post_g"][1]), row(pre_g[1, 1]), 1.0)
    y, _ = ffn_fwd(1, 1, h5, n5, row(pre_g[1, 1]))

    loss_part, dy = _loss_head(y, loss_target.reshape(t, d))
    loss = lax.psum(loss_part, ("x", "y", "c"))

    grads = {}
    ffn_dw = {}
    d_pre, d_post = {}, {}

    def ffn_bwd(l, j, dh_out, extra=()):
        h, n, g, u, a, f = saved[("ffn", l, j)]
        df, d_post[(l, j)] = _norm_out_bwd("ffn_post_bwd", f, dh_out, row(post_g[l, j]), 0.5)
        dg, du = _ffn_dact(df, wd_all, l, j, g, u)
        dwd = _ffn_dw_down(a, df)
        dn = _ffn_dn(dg, du, wg_all, wu_all, l, j)
        dwg, dwu = _ffn_dw_in(n, dg, du)
        ffn_dw[(l, j)] = (dwg, dwu, dwd)
        dh, d_pre[(l, j)], *rest = _norm_in_bwd("ffn_pre_bwd", h, dh_out, [(row(pre_g[l, j]), dn)] + list(extra))
        return dh, rest

    dh5, _ = ffn_bwd(1, 1, dy)

    dm1, g_mix_post1 = _norm_out_bwd("mix_post_bwd", m1, dh5, row(w["mix_post_g"][1]), 1.0)
    do = _mm2d("attn_out_dx", [(dm1, w_o, "nt", 0)], [(w_o.shape[0], BF16)])[0]
    g_w_o = _mm2d("attn_out_dw", [(o, dm1, "tn", 0)], [(d, F32)])[0]
    dq_n, dk_n, dv_h, dq_c, dq_s, dk_r = _attn_bwd(q_n, q_r, k_n, v_h, k_r, do, lse, cos2, sin2, seq)
    dq_c = dq_c.transpose(1, 0, 2).reshape(t, -1)
    dq_s = dq_s.transpose(1, 0, 2).reshape(t, -1)
    dqn = _mm2d("q_up_dx", [(dq_n, w_qn, "nt", 0), (dq_c, w_qr, "nt", 0), (dq_s, w_qr_rot, "nt", 0)], [(q_rank, F32)])[0]
    g_qn, g_qr, g_qr_rot = _mm2d("q_up_dw", [(qn, dq_n, "tn", 0), (qn, dq_c, "tn", 1), (qn, dq_s, "tn", 2)],
                                 [(w_qn.shape[1], F32), (w_qr.shape[1], F32), (w_qr.shape[1], F32)])
    dqd, g_q_norm = _norm_out_bwd("q_norm_bwd", qd, dqn, row(w["mla_q_norm_g"][0]), 1.0)
    dn4 = _mm2d("q_down_dx", [(dqd, w_dq, "nt", 0)], [(d, F32)])[0]
    g_w_dq = _mm2d("q_down_dw", [(n4, dqd, "tn", 0)], [(q_rank, F32)])[0]
    dh4, g_mix_pre1 = _norm_in_bwd("mix_pre_bwd", h4, dh5, [(row(w["mix_pre_g"][1]), dn4)])

    dc_n = _mm2d("kv_up_dx", [(dk_n, w_k, "nt", 0), (dv_h, w_v, "nt", 0)], [(KV_RANK, F32)])[0]
    g_wk, g_wv = _mm2d("kv_up_dw", [(c_n, dk_n, "tn", 0), (c_n, dv_h, "tn", 1)], [(w_k.shape[1], F32), (w_v.shape[1], F32)])
    dc, g_kv_a = _norm_out_bwd("kv_a_norm_bwd", c_raw, dc_n, row(w["kv_a_norm_g"]), 1.0)
    dkr_c, dkr_s = _rope_bwd(dk_r, cos2, sin2)
    dn3kv = _mm2d("kv_down_dx", [(dc, w_c, "nt", 0), (dkr_c, w_kr, "nt", 0), (dkr_s, w_kr_rot, "nt", 0)], [(d, F32)])[0]
    g_wc, g_wkr, g_wkr_rot = _mm2d("kv_down_dw", [(n3kv, dc, "tn", 0), (n3kv, dkr_c, "tn", 1), (n3kv, dkr_s, "tn", 2)],
                                   [(KV_RANK, F32), (QK_ROPE, F32), (QK_ROPE, F32)])

    dh3, (g_kv_norm,) = ffn_bwd(1, 0, dh4, extra=[(row(w["kv_norm_g"]), dn3kv)])
    dh2, _ = ffn_bwd(0, 1, dh3)

    dm0, g_mix_post0 = _norm_out_bwd("mix_post_bwd", m0, dh2, row(w["mix_post_g"][0]), 1.0)
    d_uv = _mm2d("gmlp_out_dx", [(dm0, w_out, "nt", 0)], [(half, F32)])[0]
    g_w_out = _mm2d("gmlp_out_dw", [(uv, dm0, "tn", 0)], [(d, F32)])[0]
    dzp, g_ln_g, g_ln_b, g_w_s, g_b_s = _sgu_bwd(zp, d_uv, ln_g, ln_b, w_s, bsb)
    dn1 = _mm2d("gmlp_in_dx", [(dzp, w_in, "nt", 0)], [(d, F32)])[0]
    g_w_in = _mm2d("gmlp_in_dw", [(n1, dzp, "tn", 0)], [(w_in.shape[1], F32)])[0]
    dh1, g_mix_pre0 = _norm_in_bwd("mix_pre_bwd", h1, dh2, [(row(w["mix_pre_g"][0]), dn1)])
    dx, _ = ffn_bwd(0, 0, dh1)

    lj = [(l, j) for l in range(2) for j in range(2)]
    part = {
        "gmlp_w_in": g_w_in[None], "gmlp_w_out": g_w_out[None],
        "w_dkv": jnp.concatenate([g_wc, g_wkr + _unrot_cols(g_wkr_rot)], axis=1),
        "w_ukv": jnp.stack([g_wk.reshape(KV_RANK, N_HEADS, QK_NOPE), g_wv.reshape(KV_RANK, N_HEADS, V_DIM)],
                           axis=2).reshape(KV_RANK, -1),
        "mla_w_dq": g_w_dq[None],
        "mla_w_uq": jnp.concatenate(
            [g_qn.reshape(q_rank, N_HEADS, QK_NOPE),
             g_qr.reshape(q_rank, N_HEADS, QK_ROPE) + _unrot_cols(g_qr_rot.reshape(q_rank, N_HEADS, QK_ROPE))],
            axis=-1).reshape(1, q_rank, -1),
        "mla_w_o": g_w_o[None],
        "ffn_pre_g": jnp.concatenate([d_pre[k] for k in lj]).reshape(2, 2, d),
        "ffn_post_g": jnp.concatenate([d_post[k] for k in lj]).reshape(2, 2, d),
        "gmlp_ln_g": g_ln_g, "gmlp_ln_b": g_ln_b,
        "mix_pre_g": jnp.concatenate([g_mix_pre0, g_mix_pre1]), "mix_post_g": jnp.concatenate([g_mix_post0, g_mix_post1]),
        "gmlp_w_s": g_w_s[None], "gmlp_b_s": g_b_s.reshape(1, GROUPS, CHUNK),
        "kv_norm_g": g_kv_norm.reshape(-1), "kv_a_norm_g": g_kv_a.reshape(-1), "mla_q_norm_g": g_q_norm,
    }

    sharded = mats + vecs
    n_sh = n_mats + n_vecs
    n_rep = sum(w[n].size for n in replicated)
    sh_rows = _round_up(-(-n_sh // PACK_WIDTH), 8)
    rep_rows = _round_up(-(-(n_rep // N_SHARDS) // PACK_WIDTH), 8)
    rows = _round_up(sh_rows + rep_rows, 32)
    sh_flat = jnp.concatenate([_split_shards(part[n], ax) for n, ax in sharded], axis=1)
    rep_flat = jnp.concatenate([part[n].reshape(-1) for n in replicated]).reshape(N_SHARDS, -1)
    small = jnp.concatenate([
        jnp.pad(sh_flat, ((0, 0), (0, sh_rows * PACK_WIDTH - n_sh))),
        jnp.pad(rep_flat, ((0, 0), (0, (rows - sh_rows) * PACK_WIDTH - n_rep // N_SHARDS)))], axis=1)
    small = small.astype(BF16).reshape(N_SHARDS, 2, rows // 2, PACK_WIDTH)

    parts, where = [], []
    for l, j in lj:
        for q, g in enumerate(ffn_dw[(l, j)]):
            parts.append(g.reshape(N_SHARDS, 2, g.shape[1] // 2, g.shape[2]))
            where.append((q, (l, j)))
    parts.append(small)
    where.append((3, ()))
    chip = (2 * lax.axis_index("x") + lax.axis_index("y")).astype(jnp.int32).reshape(1)
    others = _swap_halves(parts)
    chip_sums = [_add_half(p, o, core) for p, o in zip(parts, others)]
    slots = _scatter_chips(chip_sums)
    halves = [_sum_slots(s, p, chip) for s, p in zip(slots, chip_sums)]
    out_shapes = [(2, 2) + parts[q].shape[2:] for q in range(3)] + [(rows // 2, PACK_WIDTH)]
    recv = _join_halves(halves, out_shapes, where)
    own_small = halves[-1]
    g_small = jnp.where(core[0] == 0, jnp.concatenate([own_small, recv[3]]), jnp.concatenate([recv[3], own_small]))
    g_rep = _gather_rows(g_small, sh_rows, rep_rows)

    grads, delta, new_m, new_v = {}, {}, {}, {}
    for q, n in enumerate(["ffn_w_gate", "ffn_w_up", "ffn_w_down"]):
        own = jnp.stack(halves[q:12:3]).reshape(recv[q].shape)
        grads[n], delta[n], new_m[n], new_v[n] = _adamw_halves(w[n], own, recv[q], mom[n], var[n], core)
    for (n, _), g in zip(sharded, _split_flat(g_small.reshape(-1), [w[n].shape for n, _ in sharded])):
        grads[n] = g
    rep_vec = g_rep.reshape(N_SHARDS, -1)[:, :n_rep // N_SHARDS].reshape(-1)
    for n, g in zip(replicated, _split_flat(rep_vec, [w[n].shape for n in replicated])):
        grads[n] = g

    for n in names:
        if n not in delta:
            delta[n], new_m[n], new_v[n] = _adamw(w[n], grads[n], mom[n], var[n])
    return (loss, dx.reshape(x.shape), *[grads[n] for n in names], *[delta[n] for n in names],
            *[new_m[n] for n in names], *[new_v[n] for n in names])
```

```python
import math

import jax
import jax.numpy as jnp
from jax import lax
from jax.experimental import pallas as pl
from jax.experimental.pallas import tpu as pltpu
from jax.experimental.pallas import tpu_sc as plsc

F32, BF16 = jnp.float32, jnp.bfloat16

RMS_EPS, LN_EPS, NEG_INF = 1e-6, 1e-5, -1e30
N_HEADS, QK_NOPE, QK_ROPE, V_DIM, KV_RANK = 8, 128, 64, 128, 256
CHUNK, GROUPS = 128, 16
ROPE_THETA = 10000.0
ADAM_LR, ADAM_B1, ADAM_B2, ADAM_EPS, ADAM_WD, ADAM_STEP = 0.001, 0.9, 0.999, 1e-08, 0.01, 10
N_SHARDS = 4

VMEM_LIMIT_BYTES = 48 * 1024 * 1024
ROW_TILE = 512
PACK_WIDTH = 1024

_DN = {"nn": (((1,), (0,)), ((), ())), "nt": (((1,), (1,)), ((), ())), "tn": (((0,), (0,)), ((), ()))}
_MESH = pl.DeviceIdType.MESH
_ANY = pl.BlockSpec(memory_space=pl.ANY)


def _params(sem):
    return pltpu.CompilerParams(dimension_semantics=sem, vmem_limit_bytes=VMEM_LIMIT_BYTES)


def _mm(name, grid, ins, pairs, acc_shapes, outs, epilogue, extras=()):
    n_in, n_ex, n_out = len(ins), len(extras), len(outs)
    gk = grid[2]

    def body(*refs):
        in_refs, ex_refs = refs[:n_in], refs[n_in:n_in + n_ex]
        out_refs = refs[n_in + n_ex:n_in + n_ex + n_out]
        acc_refs = refs[n_in + n_ex + n_out:]
        parts = [None] * len(acc_shapes)
        for a, b, c, dims in pairs:
            p = lax.dot_general(in_refs[a][...], in_refs[b][...], _DN[dims], preferred_element_type=F32)
            parts[c] = p if parts[c] is None else parts[c] + p

        def finish(accs):
            vals = epilogue(accs, [r[...] for r in ex_refs])
            for r, v in zip(out_refs, vals):
                r[...] = v.astype(r.dtype)

        if gk == 1:
            finish(parts)
        else:
            k = pl.program_id(2)

            @pl.when(k == 0)
            def _():
                for r, p in zip(acc_refs, parts):
                    r[...] = p

            @pl.when(k > 0)
            def _():
                for r, p in zip(acc_refs, parts):
                    r[...] += p

            @pl.when(k == gk - 1)
            def _():
                finish([r[...] for r in acc_refs])

    return pl.pallas_call(
        body,
        out_shape=[jax.ShapeDtypeStruct(s, d) for s, d, _, _ in outs],
        grid=grid,
        in_specs=[pl.BlockSpec(bs, im) for _, bs, im in list(ins) + list(extras)],
        out_specs=[pl.BlockSpec(bs, im) for _, _, bs, im in outs],
        scratch_shapes=[pltpu.VMEM(s, F32) for s in acc_shapes] if gk > 1 else [],
        name=name,
        compiler_params=_params(("parallel", "parallel", "arbitrary")),
    )(*[a for a, _, _ in ins], *[a for a, _, _ in extras])


def _mm2d(name, pairs, outs, epilogue=None, row_extras=(), vec_extras=()):
    def mk(a, dims):
        return (a.shape[0], a.shape[1]) if dims[0] == "n" else (a.shape[1], a.shape[0])

    def nk(b, dims):
        return (b.shape[1], b.shape[0]) if dims[1] == "n" else (b.shape[0], b.shape[1])

    m = mk(pairs[0][0], pairs[0][2])[0]
    ks = [mk(a, d)[1] for a, _, d, _ in pairs]
    n_acc = 1 + max(p[3] for p in pairs)
    acc_n = [None] * n_acc
    for a, b, d, c in pairs:
        assert mk(a, d)[0] == m and nk(b, d)[1] == mk(a, d)[1]
        acc_n[c] = nk(b, d)[0]
    tm = min(m, ROW_TILE)
    if len(set(ks)) == 1 and ks[0] > 1024:
        tks, gk = [512] * len(pairs), ks[0] // 512
    else:
        tks, gk = ks, 1
    if len(set(acc_n)) == 1 and acc_n[0] > 1024:
        tns, gj = [1024] * n_acc, acc_n[0] // 1024
    else:
        tns, gj = acc_n, 1

    ins, plist = [], []
    for (a, b, d, c), tk in zip(pairs, tks):
        tn = tns[c]
        a_spec = ((tm, tk), lambda i, j, k: (i, k)) if d[0] == "n" else ((tk, tm), lambda i, j, k: (k, i))
        b_spec = ((tk, tn), lambda i, j, k: (k, j)) if d[1] == "n" else ((tn, tk), lambda i, j, k: (j, k))
        ins += [(a, *a_spec), (b, *b_spec)]
        plist.append((len(ins) - 2, len(ins) - 1, c, d))
    extras = [(r, (tm, r.shape[1]), lambda i, j, k: (i, 0)) for r in row_extras]
    extras += [(v, v.shape, lambda i, j, k: (0, 0)) for v in vec_extras]
    out_specs = []
    for n, dt in outs:
        bn = 1024 if (gj > 1) else n
        out_specs.append(((m, n), dt, (tm, bn), lambda i, j, k: (i, j)))
    if epilogue is None:
        epilogue = lambda accs, ex: accs
    return _mm(name, (m // tm, gj, gk), ins, plist, [(tm, tn) for tn in tns], out_specs, epilogue, extras)


def _rms(x, g):
    return x * lax.rsqrt(jnp.mean(x * x, axis=-1, keepdims=True) + RMS_EPS) * g


def _rms_bwd(x, g, dy):
    r = lax.rsqrt(jnp.mean(x * x, axis=-1, keepdims=True) + RMS_EPS)
    gy = dy * g
    dx = r * gy - x * (r * r * r) * jnp.mean(gy * x, axis=-1, keepdims=True)
    return dx, jnp.sum(dy * x * r, axis=0, keepdims=True)


def _sigmoid(x):
    return 1.0 / (1.0 + jnp.exp(-x))


_GELU_C = math.sqrt(2.0 / math.pi)


def _gelu(x):
    return x * (0.5 * (1.0 + jnp.tanh(_GELU_C * (x + 0.044715 * (x * x * x)))))


def _gelu_grad(x):
    t = jnp.tanh(_GELU_C * (x + 0.044715 * (x * x * x)))
    return 0.5 * (1.0 + t) + 0.5 * x * (1.0 - t * t) * (_GELU_C * (1.0 + 3.0 * 0.044715 * (x * x)))


def _rows(name, row_ins, vec_ins, fn, row_outs, acc_outs=()):
    t = row_ins[0].shape[0]
    tm = min(t, ROW_TILE)
    nr, nv, no = len(row_ins), len(vec_ins), len(row_outs)

    def body(*refs):
        outs, incs = fn([r[...] for r in refs[:nr]], [r[...] for r in refs[nr:nr + nv]])
        for r, v in zip(refs[nr + nv:nr + nv + no], outs):
            r[...] = v.astype(r.dtype)
        i = pl.program_id(0)
        for r, v in zip(refs[nr + nv + no:], incs):
            @pl.when(i == 0)
            def _():
                r[...] = v

            @pl.when(i > 0)
            def _():
                r[...] += v

    in_specs = [pl.BlockSpec((tm, a.shape[1]), lambda i: (i, 0)) for a in row_ins]
    in_specs += [pl.BlockSpec(v.shape, lambda i, nd=v.ndim: (0,) * nd) for v in vec_ins]
    out_shape = [jax.ShapeDtypeStruct((t, c), dt) for c, dt in row_outs]
    out_shape += [jax.ShapeDtypeStruct(s, F32) for s in acc_outs]
    out_specs = [pl.BlockSpec((tm, c), lambda i: (i, 0)) for c, _ in row_outs]
    out_specs += [pl.BlockSpec(s, lambda i, nd=len(s): (0,) * nd) for s in acc_outs]
    return pl.pallas_call(body, out_shape=out_shape, grid=(t // tm,), in_specs=in_specs, out_specs=out_specs,
                          name=name, compiler_params=_params(("arbitrary",)))(*row_ins, *vec_ins)


def _rms_fwd(x, g):
    return _rows("rms_fwd", [x], [g], lambda r, v: ([_rms(r[0], v[0])], []), [(x.shape[1], BF16)])[0]


def _norm_out_bwd(name, f, d_out, g, scale):
    def fn(r, v):
        dx, dg = _rms_bwd(r[0], v[0], r[1] * scale)
        return [dx], [dg]

    c = f.shape[1]
    return _rows(name, [f, d_out], [g], fn, [(c, BF16)], [(1, c)])


def _norm_in_bwd(name, h, d_res, branches):
    nb = len(branches)

    def fn(r, v):
        dh, dgs = r[1], []
        for b in range(nb):
            dx, dg = _rms_bwd(r[0], v[b], r[2 + b])
            dh = dh + dx
            dgs.append(dg)
        return [dh], dgs

    c = h.shape[1]
    return _rows(name, [h, d_res] + [dn for _, dn in branches], [g for g, _ in branches], fn, [(c, F32)],
                 [(1, c)] * nb)


def _loss_head(y, target):
    d = y.shape[1]

    def fn(r, v):
        e = r[0] - r[1]
        s = jnp.sum(jnp.sum(e * e, axis=1, keepdims=True), axis=0, keepdims=True) * (0.5 / d)
        return [e * (1.0 / d)], [jnp.broadcast_to(s, (1, 128))]

    dy, acc = _rows("loss_head", [y, target], [], fn, [(d, F32)], [(1, 128)])
    return acc[0, 0], dy


def _rope_bwd(dk, cos2, sin2):
    c = dk.shape[1]
    return _rows("rope_bwd", [dk, cos2, sin2], [], lambda r, v: ([r[0] * r[1], r[0] * r[2]], []),
                 [(c, BF16), (c, BF16)])


def _ffn_up(n, wg, wu):
    t, d = n.shape
    fs = wg.shape[-1]
    tm = min(t, ROW_TILE)
    w_spec = ((None, d, fs), lambda s, i, k: (s, 0, 0))

    def epi(accs, ex):
        g, u = accs
        return [g, u, g * _sigmoid(g) * u]

    o_spec = ((None, tm, fs), lambda s, i, k: (s, i, 0))
    outs = [((N_SHARDS, t, fs), F32, *o_spec), ((N_SHARDS, t, fs), F32, *o_spec), ((N_SHARDS, t, fs), BF16, *o_spec)]
    return _mm("ffn_up", (N_SHARDS, t // tm, 1),
               [(n, (tm, d), lambda s, i, k: (i, 0)), (wg, *w_spec), (wu, *w_spec)],
               [(0, 1, 0, "nn"), (0, 2, 1, "nn")], [(tm, fs)] * 2, outs, epi)


def _down(name, a_in, w_in, gk, h, post_g, next_gs, scale):
    t, d = h.shape
    tm = min(t, ROW_TILE)
    kn = next_gs.shape[0]

    def epi(accs, ex):
        f, hv, pg, ng = accs[0], ex[0], ex[1], ex[2]
        hn = hv + scale * _rms(f, pg)
        return [f, hn] + [_rms(hn, ng[q:q + 1]) for q in range(kn)]

    row = ((tm, d), lambda i, j, k: (i, 0))
    outs = [((t, d), F32, *row), ((t, d), F32, *row)] + [((t, d), BF16, *row)] * kn
    extras = [(h, *row), (post_g, (1, d), lambda i, j, k: (0, 0)), (next_gs, (kn, d), lambda i, j, k: (0, 0))]
    return _mm(name, (t // tm, 1, gk), [a_in, w_in], [(0, 1, 0, "nn")], [(tm, d)], outs, epi, extras)


def _ffn_down(a, wd, h, post_g, next_gs):
    t, d = h.shape
    fs = a.shape[-1]
    tm = min(t, ROW_TILE)
    return _down("ffn_down", (a, (None, tm, fs), lambda i, _, s: (s, i, 0)),
                 (wd, (None, fs, d), lambda i, _, s: (s, 0, 0)), N_SHARDS, h, post_g, next_gs, 0.5)


def _ffn_dact(df, wd, g, u):
    t, d = df.shape
    fs = g.shape[-1]
    tm = min(t, ROW_TILE)

    def epi(accs, ex):
        da, gv, uv = accs[0], ex[0], ex[1]
        sg = _sigmoid(gv)
        return [da * uv * (sg * (1.0 + gv * (1.0 - sg))), da * (gv * sg)]

    o_spec = ((None, tm, fs), lambda s, i, k: (s, i, 0))
    outs = [((N_SHARDS, t, fs), BF16, *o_spec)] * 2
    return _mm("ffn_dact", (N_SHARDS, t // tm, 1),
               [(df, (tm, d), lambda s, i, k: (i, 0)), (wd, (None, fs, d), lambda s, i, k: (s, 0, 0))],
               [(0, 1, 0, "nt")], [(tm, fs)], outs, epi, [(g, *o_spec), (u, *o_spec)])


def _ffn_dn(dg, du, wg, wu):
    _, t, fs = dg.shape
    d = wg.shape[-2]
    tm = min(t, ROW_TILE)
    a_spec = ((None, tm, fs), lambda i, _, s: (s, i, 0))
    w_spec = ((None, d, fs), lambda i, _, s: (s, 0, 0))
    outs = [((t, d), F32, (tm, d), lambda i, _, s: (i, 0))]
    return _mm("ffn_dn", (t // tm, 1, N_SHARDS), [(dg, *a_spec), (wg, *w_spec), (du, *a_spec), (wu, *w_spec)],
               [(0, 1, 0, "nt"), (2, 3, 0, "nt")], [(tm, d)], outs, lambda accs, ex: accs)[0]


def _ffn_dw_in(n, dg, du):
    _, t, fs = dg.shape
    d = n.shape[1]
    tk = min(t, ROW_TILE)
    b_spec = ((None, tk, fs), lambda s, _, k: (s, k, 0))
    o_spec = ((None, d, fs), lambda s, _, k: (s, 0, 0))
    outs = [((N_SHARDS, d, fs), BF16, *o_spec)] * 2
    return _mm("ffn_dw_in", (N_SHARDS, 1, t // tk), [(n, (tk, d), lambda s, _, k: (k, 0)), (dg, *b_spec), (du, *b_spec)],
               [(0, 1, 0, "tn"), (0, 2, 1, "tn")], [(d, fs)] * 2, outs, lambda accs, ex: accs)


def _ffn_dw_down(a, df):
    _, t, fs = a.shape
    d = df.shape[1]
    tk = min(t, ROW_TILE)
    outs = [((N_SHARDS, fs, d), BF16, (None, fs, d), lambda s, _, k: (s, 0, 0))]
    return _mm("ffn_dw_down", (N_SHARDS, 1, t // tk),
               [(a, (None, tk, fs), lambda s, _, k: (s, k, 0)), (df, (tk, d), lambda s, _, k: (k, 0))],
               [(0, 1, 0, "tn")], [(fs, d)], outs, lambda accs, ex: accs)[0]


def _causal_weight(w):
    row = lax.broadcasted_iota(jnp.int32, (CHUNK, CHUNK), 0)
    col = lax.broadcasted_iota(jnp.int32, (CHUNK, CHUNK), 1)
    return row >= col, jnp.where(row >= col, w, 0.0).astype(BF16)


def _layer_norm(v, g, b):
    xc = v - jnp.mean(v, axis=-1, keepdims=True)
    rstd = lax.rsqrt(jnp.mean(xc * xc, axis=-1, keepdims=True) + LN_EPS)
    xhat = xc * rstd
    return xhat, rstd, xhat * g + b


def _sgu_specs(t, half, tm):
    return [pl.BlockSpec((tm, half), lambda i: (i, 0)), pl.BlockSpec((tm, half), lambda i: (i, 1))]


def _sgu_fwd(zp, ln_g, ln_b, w_s, bsb):
    t, half = zp.shape[0], zp.shape[1] // 2
    tm = min(t, 2 * CHUNK)

    def body(u_ref, v_ref, g_ref, b_ref, w_ref, bs_ref, o_ref):
        u = _gelu(u_ref[...])
        _, _, vn = _layer_norm(_gelu(v_ref[...]), g_ref[...], b_ref[...])
        vb = vn.astype(BF16)
        for g in range(GROUPS):
            _, wm = _causal_weight(w_ref[g])
            cols = slice(g * CHUNK, (g + 1) * CHUNK)
            for c in range(tm // CHUNK):
                rows = slice(c * CHUNK, (c + 1) * CHUNK)
                sv = jnp.dot(wm, vb[rows, cols], preferred_element_type=F32) + bs_ref[g]
                o_ref[rows, cols] = (u[rows, cols] * sv).astype(BF16)

    whole = lambda a: pl.BlockSpec(a.shape, lambda i, nd=a.ndim: (0,) * nd)
    return pl.pallas_call(
        body, out_shape=jax.ShapeDtypeStruct((t, half), BF16), grid=(t // tm,),
        in_specs=_sgu_specs(t, half, tm) + [whole(ln_g), whole(ln_b), whole(w_s), whole(bsb)],
        out_specs=pl.BlockSpec((tm, half), lambda i: (i, 0)), name="sgu_fwd",
        compiler_params=_params(("arbitrary",)))(zp, zp, ln_g, ln_b, w_s, bsb)


def _sgu_bwd(zp, d_uv, ln_g, ln_b, w_s, bsb):
    t, half = zp.shape[0], zp.shape[1] // 2
    tm = min(t, 2 * CHUNK)

    def body(u_ref, v_ref, d_ref, g_ref, b_ref, w_ref, bs_ref, dz_ref, dlg_ref, dlb_ref, dws_ref, dbs_ref, dvn_ref):
        i = pl.program_id(0)

        @pl.when(i == 0)
        def _():
            dlg_ref[...] = jnp.zeros_like(dlg_ref)
            dlb_ref[...] = jnp.zeros_like(dlb_ref)
            dws_ref[...] = jnp.zeros_like(dws_ref)
            dbs_ref[...] = jnp.zeros_like(dbs_ref)

        up, vp = u_ref[...], v_ref[...]
        u, gup = _gelu(up), _gelu_grad(up)
        xhat, rstd, vn = _layer_norm(_gelu(vp), g_ref[...], b_ref[...])
        vb = vn.astype(BF16)
        d = d_ref[...]
        for g in range(GROUPS):
            mask, wm = _causal_weight(w_ref[g])
            cols = slice(g * CHUNK, (g + 1) * CHUNK)
            for c in range(tm // CHUNK):
                rows = slice(c * CHUNK, (c + 1) * CHUNK)
                blk = vb[rows, cols]
                sv = jnp.dot(wm, blk, preferred_element_type=F32) + bs_ref[g]
                dblk = d[rows, cols]
                dz_ref[rows, cols] = (dblk * sv * gup[rows, cols]).astype(BF16)
                dsv = dblk * u[rows, cols]
                dsvb = dsv.astype(BF16)
                dvn_ref[rows, cols] = lax.dot_general(wm, dsvb, _DN["tn"], preferred_element_type=F32)
                dw = lax.dot_general(dsvb, blk, _DN["nt"], preferred_element_type=F32)
                dws_ref[g] += jnp.where(mask, dw, 0.0)
                dbs_ref[g] += jnp.sum(dsv, axis=1, keepdims=True)
        dvn = dvn_ref[...]
        dlg_ref[...] += jnp.sum(dvn * xhat, axis=0, keepdims=True)
        dlb_ref[...] += jnp.sum(dvn, axis=0, keepdims=True)
        dxh = dvn * g_ref[...]
        dv = rstd * (dxh - jnp.mean(dxh, axis=-1, keepdims=True)
                     - xhat * jnp.mean(dxh * xhat, axis=-1, keepdims=True))
        dz_ref[:, half:] = (dv * _gelu_grad(vp)).astype(BF16)

    whole = lambda a: pl.BlockSpec(a.shape, lambda i, nd=a.ndim: (0,) * nd)
    wshape = lambda s: pl.BlockSpec(s, lambda i, nd=len(s): (0,) * nd)
    out_shape = [jax.ShapeDtypeStruct((t, 2 * half), BF16), jax.ShapeDtypeStruct((1, half), F32),
                 jax.ShapeDtypeStruct((1, half), F32), jax.ShapeDtypeStruct(w_s.shape, F32),
                 jax.ShapeDtypeStruct((GROUPS, CHUNK, 1), F32)]
    return pl.pallas_call(
        body, out_shape=out_shape, grid=(t // tm,),
        in_specs=_sgu_specs(t, half, tm) + [pl.BlockSpec((tm, half), lambda i: (i, 0)), whole(ln_g), whole(ln_b),
                                            whole(w_s), whole(bsb)],
        out_specs=[pl.BlockSpec((tm, 2 * half), lambda i: (i, 0)), wshape((1, half)), wshape((1, half)),
                   wshape(w_s.shape), wshape((GROUPS, CHUNK, 1))],
        scratch_shapes=[pltpu.VMEM((tm, half), F32)], name="sgu_bwd",
        compiler_params=_params(("arbitrary",)))(zp, zp, d_uv, ln_g, ln_b, w_s, bsb)


_SCALE = (QK_NOPE + QK_ROPE) ** -0.5


def _attn_scores(qn, qr, kn, kr, i, tq, n):
    s = lax.dot_general(qn, kn, _DN["nt"], preferred_element_type=F32)
    s = (s + lax.dot_general(qr, kr, _DN["nt"], preferred_element_type=F32)) * _SCALE
    row = i * tq + lax.broadcasted_iota(jnp.int32, (tq, n), 0)
    col = lax.broadcasted_iota(jnp.int32, (tq, n), 1)
    return jnp.where(col <= row, s, NEG_INF)


def _attn_specs(seq):
    head = lambda b, h: (b, h)
    return dict(
        qn=pl.BlockSpec((seq, QK_NOPE), head),
        qr=pl.BlockSpec((None, seq, QK_ROPE), lambda b, h: (h, b, 0)),
        kr=pl.BlockSpec((seq, QK_ROPE), lambda b, h: (b, 0)),
        lse=pl.BlockSpec((None, seq, 1), lambda b, h: (h, b, 0)),
    )


def _attn_fwd(qn, qr, kn, v, kr, seq):
    t = qn.shape[0]
    tq = min(seq, 2 * CHUNK)
    sp = _attn_specs(seq)

    def body(qn_ref, qr_ref, kn_ref, v_ref, kr_ref, o_ref, lse_ref):
        for i in range(seq // tq):
            rows, n = slice(i * tq, (i + 1) * tq), (i + 1) * tq
            s = _attn_scores(qn_ref[rows, :], qr_ref[rows, :], kn_ref[0:n, :], kr_ref[0:n, :], i, tq, n)
            m = jnp.max(s, axis=-1, keepdims=True)
            p = jnp.exp(s - m)
            l = jnp.sum(p, axis=-1, keepdims=True)
            o_ref[rows, :] = jnp.dot((p / l).astype(BF16), v_ref[0:n, :], preferred_element_type=F32).astype(BF16)
            lse_ref[rows, :] = m + jnp.log(l)

    return pl.pallas_call(
        body, out_shape=[jax.ShapeDtypeStruct((t, N_HEADS * V_DIM), BF16), jax.ShapeDtypeStruct((N_HEADS, t, 1), F32)],
        grid=(t // seq, N_HEADS), in_specs=[sp["qn"], sp["qr"], sp["qn"], sp["qn"], sp["kr"]],
        out_specs=[sp["qn"], sp["lse"]], name="attn_fwd",
        compiler_params=_params(("parallel", "arbitrary")))(qn, qr, kn, v, kr)


def _attn_bwd(qn, qr, kn, v, kr, do, lse, cos2, sin2, seq):
    t = qn.shape[0]
    tq = min(seq, 2 * CHUNK)
    sp = _attn_specs(seq)

    def body(qn_ref, qr_ref, kn_ref, v_ref, kr_ref, do_ref, lse_ref, cos_ref, sin_ref,
             dqn_ref, dkn_ref, dv_ref, dqc_ref, dqs_ref, dkr_ref, dk_acc, dv_acc, dkr_acc):
        dk_acc[...] = jnp.zeros_like(dk_acc)
        dv_acc[...] = jnp.zeros_like(dv_acc)
        dkr_acc[...] = jnp.zeros_like(dkr_acc)
        for i in range(seq // tq):
            rows, n = slice(i * tq, (i + 1) * tq), (i + 1) * tq
            q_n, q_r, d_o = qn_ref[rows, :], qr_ref[rows, :], do_ref[rows, :]
            k_n, k_r = kn_ref[0:n, :], kr_ref[0:n, :]
            s = _attn_scores(q_n, q_r, k_n, k_r, i, tq, n)
            p = jnp.exp(s - lse_ref[rows, :])
            dp = lax.dot_general(d_o, v_ref[0:n, :], _DN["nt"], preferred_element_type=F32)
            ds = (p * (dp - jnp.sum(p * dp, axis=-1, keepdims=True)) * _SCALE).astype(BF16)
            dqn_ref[rows, :] = jnp.dot(ds, k_n, preferred_element_type=F32).astype(BF16)
            dqr = jnp.dot(ds, k_r, preferred_element_type=F32)
            dqc_ref[rows, :] = (dqr * cos_ref[rows, :]).astype(BF16)
            dqs_ref[rows, :] = (dqr * sin_ref[rows, :]).astype(BF16)
            dk_acc[0:n, :] += lax.dot_general(ds, q_n, _DN["tn"], preferred_element_type=F32)
            dkr_acc[0:n, :] += lax.dot_general(ds, q_r, _DN["tn"], preferred_element_type=F32)
            dv_acc[0:n, :] += lax.dot_general(p.astype(BF16), d_o, _DN["tn"], preferred_element_type=F32)
        dkn_ref[...] = dk_acc[...].astype(BF16)
        dv_ref[...] = dv_acc[...].astype(BF16)
        h = pl.program_id(1)

        @pl.when(h == 0)
        def _():
            dkr_ref[...] = dkr_acc[...]

        @pl.when(h > 0)
        def _():
            dkr_ref[...] += dkr_acc[...]

    wide = jax.ShapeDtypeStruct((t, N_HEADS * V_DIM), BF16)
    rope = jax.ShapeDtypeStruct((N_HEADS, t, QK_ROPE), BF16)
    krf = pl.BlockSpec((seq, QK_ROPE), lambda b, h: (b, 0))
    return pl.pallas_call(
        body, out_shape=[wide, wide, wide, rope, rope, jax.ShapeDtypeStruct((t, QK_ROPE), F32)],
        grid=(t // seq, N_HEADS),
        in_specs=[sp["qn"], sp["qr"], sp["qn"], sp["qn"], sp["kr"], sp["qn"], sp["lse"], krf, krf],
        out_specs=[sp["qn"], sp["qn"], sp["qn"], sp["qr"], sp["qr"], krf],
        scratch_shapes=[pltpu.VMEM((seq, QK_NOPE), F32), pltpu.VMEM((seq, V_DIM), F32), pltpu.VMEM((seq, QK_ROPE), F32)],
        name="attn_bwd", compiler_params=_params(("parallel", "arbitrary")))(qn, qr, kn, v, kr, do, lse, cos2, sin2)


def _row_tile(rows, cols, row_mult=8):
    cap = max(row_mult, (1 << 18) // cols)
    best = rows
    for tr in range(row_mult, min(rows, cap) + 1, row_mult):
        if rows % tr == 0:
            best = tr
    return best if rows > cap else rows


def _adamw_math(w, g, m, v):
    mv = ADAM_B1 * m + (1.0 - ADAM_B1) * g
    vv = ADAM_B2 * v + (1.0 - ADAM_B2) * (g * g)
    m_hat = mv / (1.0 - ADAM_B1 ** ADAM_STEP)
    v_hat = vv / (1.0 - ADAM_B2 ** ADAM_STEP)
    return -ADAM_LR * (m_hat / (jnp.sqrt(v_hat) + ADAM_EPS) + ADAM_WD * w), mv, vv


def _adamw(w, g, m, v):
    shape = w.shape
    c = shape[-1]
    r = w.size // c
    tr = _row_tile(r, c)

    def body(w_ref, g_ref, m_ref, v_ref, d_ref, nm_ref, nv_ref):
        d_ref[...], nm_ref[...], nv_ref[...] = _adamw_math(w_ref[...], g_ref[...], m_ref[...], v_ref[...])

    spec = pl.BlockSpec((tr, c), lambda i: (i, 0))
    outs = pl.pallas_call(body, out_shape=[jax.ShapeDtypeStruct((r, c), F32)] * 3, grid=(r // tr,),
                          in_specs=[spec] * 4, out_specs=[spec] * 3, name="adamw",
                          compiler_params=_params(("parallel",)))(*[a.reshape(r, c) for a in (w, g, m, v)])
    return [o.reshape(shape) for o in outs]


def _adamw_halves(w, own, recv, m, v, core):
    nl, nj, rows, c = w.shape
    r = rows // 2
    tr = _row_tile(r, c)

    def body(core_ref, w_ref, own_ref, recv_ref, m_ref, v_ref, g_ref, d_ref, nm_ref, nv_ref):
        g = jnp.where(pl.program_id(2) == core_ref[0], own_ref[...], recv_ref[...])
        g_ref[...] = g
        d_ref[...], nm_ref[...], nv_ref[...] = _adamw_math(w_ref[...], g, m_ref[...], v_ref[...])

    full = pl.BlockSpec((None, None, None, tr, c), lambda a, b, h, i, cr: (a, b, h, i, 0))
    half = pl.BlockSpec((None, None, tr, c), lambda a, b, h, i, cr: (a, b, i, 0))
    grid_spec = pltpu.PrefetchScalarGridSpec(num_scalar_prefetch=1, grid=(nl, nj, 2, r // tr),
                                             in_specs=[full, half, half, full, full], out_specs=[full] * 4)
    split = lambda a: a.reshape(nl, nj, 2, r, c)
    outs = pl.pallas_call(body, out_shape=[jax.ShapeDtypeStruct((nl, nj, 2, r, c), F32)] * 4, grid_spec=grid_spec,
                          name="adamw_halves", compiler_params=_params(("parallel",) * 4))(
                              core, split(w), own, recv, split(m), split(v))
    return [o.reshape(w.shape) for o in outs]


def _place():
    x, y, c = lax.axis_index("x"), lax.axis_index("y"), lax.axis_index("c")
    return x, y, c, [(1 - x, y), (x, 1 - y), (1 - x, 1 - y)]


def _dma_sems(*counts):
    return [pltpu.SemaphoreType.DMA((n,)) for n in counts]


def _all_gather(bufs, collective_id, name):
    n = len(bufs)

    def body(*refs):
        ins, outs = refs[:n], refs[n:2 * n]
        send, recv, fsend, frecv, osend, orecv = refs[2 * n:]
        x, y, c, _ = _place()
        xn, yn, sib = (1 - x, y, c), (x, 1 - y, c), (x, y, 1 - c)
        k, kx, ky, kd = 2 * x + y, 2 * (1 - x) + y, 2 * x + 1 - y, 2 * (1 - x) + 1 - y
        barrier = pltpu.get_barrier_semaphore()
        for peer in (xn, yn, sib):
            pl.semaphore_signal(barrier, inc=1, device_id=peer, device_id_type=_MESH)
        pl.semaphore_wait(barrier, 3)

        def copy(src, dst, sems, i, to):
            return pltpu.make_async_remote_copy(src, dst, sems[0].at[i], sems[1].at[i], device_id=to, device_id_type=_MESH)

        ici, d2d, own_s = (send, recv), (fsend, frecv), (osend, orecv)
        started = [copy(ins[b], outs[b].at[k], own_s, b, sib) for b in range(n)]
        for first in (True, False):
            for b in range(n):
                mine = outs[b].at[k, c]
                if first:
                    started += [copy(ins[b].at[c, 0], mine.at[0], ici, 6 * b, xn), copy(ins[b].at[c, 1], mine.at[1], ici, 6 * b + 1, yn)]
                else:
                    started += [copy(ins[b].at[c, 1], mine.at[1], ici, 6 * b + 2, xn), copy(ins[b].at[c, 0], mine.at[0], ici, 6 * b + 3, yn)]
        for cp in started:
            cp.start()
        passed = []
        for b in range(n):
            for i, (src_chip, q, to) in enumerate([(kx, 0, yn), (ky, 1, xn)]):
                piece = outs[b].at[src_chip, c, q]
                copy(piece, piece, ici, 6 * b + i, to).wait_recv()
                cp = copy(piece, piece, ici, 6 * b + 4 + i, to)
                cp.start()
                passed.append(cp)
        for b in range(n):
            for i, (src_chip, q) in enumerate([(kx, 1), (ky, 0)]):
                piece = outs[b].at[src_chip, c, q]
                copy(piece, piece, ici, 6 * b + 2 + i, xn).wait_recv()
                half = outs[b].at[src_chip, c]
                cp = copy(half, half, d2d, 3 * b + i, sib)
                cp.start()
                passed.append(cp)
        for b in range(n):
            for i, q in enumerate([0, 1]):
                piece = outs[b].at[kd, c, q]
                copy(piece, piece, ici, 6 * b + 4 + i, xn).wait_recv()
            half = outs[b].at[kd, c]
            cp = copy(half, half, d2d, 3 * b + 2, sib)
            cp.start()
            passed.append(cp)
        for b in range(n):
            for i, src_chip in enumerate([kx, ky, kd]):
                half = outs[b].at[src_chip, 1 - c]
                copy(half, half, d2d, 3 * b + i, sib).wait_recv()
        for cp in started[n:] + passed:
            cp.wait_send()
        for cp in started[:n]:
            cp.wait()

    return pl.kernel(
        body, out_type=[jax.ShapeDtypeStruct((N_SHARDS,) + b.shape, b.dtype) for b in bufs],
        mesh=plsc.ScalarSubcoreMesh(axis_name="sequencer", num_cores=1),
        scratch_types=_dma_sems(6 * n, 6 * n, 3 * n, 3 * n, n, n),
        compiler_params=pltpu.CompilerParams(collective_id=collective_id), name=name)(*bufs)


def _swap_halves(parts):
    n = len(parts)

    def body(*refs):
        ins, outs = refs[:n], refs[n:2 * n]
        send, recv = refs[2 * n:]
        x, y, c, _ = _place()
        cps = [pltpu.make_async_remote_copy(ins[b].at[:, pl.ds(1 - c, 1)], outs[b], send.at[b], recv.at[b],
                                            device_id=(x, y, 1 - c), device_id_type=_MESH) for b in range(n)]
        for cp in cps:
            cp.start()
        for cp in cps:
            cp.wait()

    return pl.pallas_call(
        body, out_shape=[jax.ShapeDtypeStruct((N_SHARDS, 1) + p.shape[2:], p.dtype) for p in parts],
        in_specs=[_ANY] * n, out_specs=[_ANY] * n, scratch_shapes=_dma_sems(n, n), name="grad_swap_halves")(*parts)


def _add_half(part, other, core):
    _, _, r, c = part.shape
    tr = _row_tile(r, c, 16)

    def body(core_ref, p_ref, o_ref, out_ref):
        out_ref[...] = (p_ref[...].astype(F32) + o_ref[...].astype(F32)).astype(out_ref.dtype)

    grid_spec = pltpu.PrefetchScalarGridSpec(
        num_scalar_prefetch=1, grid=(N_SHARDS, r // tr),
        in_specs=[pl.BlockSpec((None, None, tr, c), lambda k, i, cr: (k, cr[0], i, 0)),
                  pl.BlockSpec((None, None, tr, c), lambda k, i, cr: (k, 0, i, 0))],
        out_specs=pl.BlockSpec((None, tr, c), lambda k, i, cr: (k, i, 0)))
    return pl.pallas_call(body, out_shape=jax.ShapeDtypeStruct((N_SHARDS, r, c), part.dtype), grid_spec=grid_spec,
                          name="grad_add_half", compiler_params=_params(("parallel", "parallel")))(core, part, other)


def _scatter_chips(parts):
    n = len(parts)

    def body(*refs):
        ins, outs = refs[:n], refs[n:2 * n]
        send, recv = refs[2 * n:]
        x, y, c, chips = _place()
        k = 2 * x + y
        started = []
        for b in range(n):
            for j, (px, py) in enumerate(chips):
                cp = pltpu.make_async_remote_copy(ins[b].at[2 * px + py], outs[b].at[k], send.at[3 * b + j],
                                                  recv.at[3 * b + j], device_id=(px, py, c), device_id_type=_MESH)
                cp.start()
                started.append(cp)
        for b in range(n):
            for j, (px, py) in enumerate(chips):
                got = outs[b].at[2 * px + py]
                pltpu.make_async_remote_copy(got, got, send.at[3 * b + j], recv.at[3 * b + j],
                                             device_id=(px, py, c), device_id_type=_MESH).wait_recv()
        for cp in started:
            cp.wait_send()

    return pl.pallas_call(
        body, out_shape=[jax.ShapeDtypeStruct(p.shape, p.dtype) for p in parts],
        in_specs=[_ANY] * n, out_specs=[_ANY] * n, scratch_shapes=_dma_sems(3 * n, 3 * n),
        name="grad_scatter_chips")(*parts)


def _sum_slots(slots, mine, chip):
    _, r, c = slots.shape
    tr = _row_tile(r, c, 16)

    def body(chip_ref, s0, s1, s2, s3, own_ref, out_ref):
        own = own_ref[...].astype(F32)
        v = [jnp.where(chip_ref[0] == s, own, ref[...].astype(F32)) for s, ref in enumerate((s0, s1, s2, s3))]
        out_ref[...] = ((v[0] + v[1]) + v[2]) + v[3]

    def slot_spec(s):
        return pl.BlockSpec((None, tr, c), lambda i, kr: (jnp.where(kr[0] == s, (s + 1) % N_SHARDS, s), i, 0))

    grid_spec = pltpu.PrefetchScalarGridSpec(
        num_scalar_prefetch=1, grid=(r // tr,),
        in_specs=[slot_spec(s) for s in range(N_SHARDS)] + [pl.BlockSpec((None, tr, c), lambda i, kr: (kr[0], i, 0))],
        out_specs=pl.BlockSpec((tr, c), lambda i, kr: (i, 0)))
    return pl.pallas_call(body, out_shape=jax.ShapeDtypeStruct((r, c), F32), grid_spec=grid_spec, name="grad_sum_slots",
                          compiler_params=_params(("parallel",)))(chip, slots, slots, slots, slots, mine)


def _join_halves(halves, out_shapes, where):
    n, no = len(halves), len(out_shapes)

    def body(*refs):
        ins, outs = refs[:n], refs[n:n + no]
        send, recv = refs[n + no:]
        x, y, c, _ = _place()
        cps = []
        for b in range(n):
            o, lead = where[b]
            dst = outs[o].at[lead] if lead else outs[o]
            cps.append(pltpu.make_async_remote_copy(ins[b], dst, send.at[b], recv.at[b],
                                                    device_id=(x, y, 1 - c), device_id_type=_MESH))
        for cp in cps:
            cp.start()
        for cp in cps:
            cp.wait()

    return pl.pallas_call(
        body, out_shape=[jax.ShapeDtypeStruct(s, F32) for s in out_shapes], in_specs=[_ANY] * n,
        out_specs=[_ANY] * no, scratch_shapes=_dma_sems(n, n), name="grad_join_halves")(*halves)


def _gather_rows(buf, start, rows):
    def body(in_ref, out_ref, send, recv, lsem):
        x, y, c, chips = _place()
        k = 2 * x + y
        src = in_ref.at[pl.ds(start, rows)]
        local = pltpu.make_async_copy(src, out_ref.at[k], lsem.at[0])
        local.start()
        cps = [pltpu.make_async_remote_copy(src, out_ref.at[k], send.at[j], recv.at[j], device_id=(px, py, c),
                                            device_id_type=_MESH) for j, (px, py) in enumerate(chips)]
        for cp in cps:
            cp.start()
        for j, (px, py) in enumerate(chips):
            got = out_ref.at[2 * px + py]
            pltpu.make_async_remote_copy(got, got, send.at[j], recv.at[j], device_id=(px, py, c),
                                         device_id_type=_MESH).wait_recv()
        for cp in cps:
            cp.wait_send()
        local.wait()

    return pl.pallas_call(body, out_shape=jax.ShapeDtypeStruct((N_SHARDS, rows, buf.shape[1]), F32),
                          in_specs=[_ANY], out_specs=_ANY, scratch_shapes=_dma_sems(3, 3, 1),
                          name="gather_replicated_grads")(buf)


def _round_up(n, m):
    return -(-n // m) * m


def _pack_flat(vecs, rows, width, dtype):
    flat = jnp.concatenate([v.reshape(-1).astype(dtype) for v in vecs])
    return jnp.pad(flat, (0, rows * width - flat.size)).reshape(rows, width)


def _split_flat(flat, shapes):
    out, off = [], 0
    for s in shapes:
        n = math.prod(s)
        out.append(flat[off:off + n].reshape(s))
        off += n
    return out


def _merge_shards(arr4, axis):
    a = jnp.moveaxis(arr4, 0, axis)
    s = list(a.shape)
    return a.reshape(s[:axis] + [s[axis] * s[axis + 1]] + s[axis + 2:])


def _split_shards(full, axis):
    s = list(full.shape)
    a = full.reshape(s[:axis] + [N_SHARDS, s[axis] // N_SHARDS] + s[axis + 1:])
    return jnp.moveaxis(a, axis, 0).reshape(N_SHARDS, -1)


def _rot_cols(w):
    half = w.shape[-1] // 2
    return jnp.concatenate([-w[..., half:], w[..., :half]], axis=-1)


def _unrot_cols(dw):
    half = dw.shape[-1] // 2
    return jnp.concatenate([dw[..., half:], -dw[..., :half]], axis=-1)


def kernel(x, positions, ffn_pre_g, ffn_post_g, ffn_w_gate, ffn_w_up, ffn_w_down, mix_pre_g, mix_post_g, gmlp_w_in, gmlp_ln_g, gmlp_ln_b, gmlp_w_s, gmlp_b_s, gmlp_w_out, kv_norm_g, w_dkv, kv_a_norm_g, w_ukv, mla_w_dq, mla_q_norm_g, mla_w_uq, mla_w_o, loss_target, m_ffn_pre_g, m_ffn_post_g, m_ffn_w_gate, m_ffn_w_up, m_ffn_w_down, m_mix_pre_g, m_mix_post_g, m_gmlp_w_in, m_gmlp_ln_g, m_gmlp_ln_b, m_gmlp_w_s, m_gmlp_b_s, m_gmlp_w_out, m_kv_norm_g, m_w_dkv, m_kv_a_norm_g, m_w_ukv, m_mla_w_dq, m_mla_q_norm_g, m_mla_w_uq, m_mla_w_o, v_ffn_pre_g, v_ffn_post_g, v_ffn_w_gate, v_ffn_w_up, v_ffn_w_down, v_mix_pre_g, v_mix_post_g, v_gmlp_w_in, v_gmlp_ln_g, v_gmlp_ln_b, v_gmlp_w_s, v_gmlp_b_s, v_gmlp_w_out, v_kv_norm_g, v_w_dkv, v_kv_a_norm_g, v_w_ukv, v_mla_w_dq, v_mla_q_norm_g, v_mla_w_uq, v_mla_w_o):
    names = ["ffn_pre_g", "ffn_post_g", "ffn_w_gate", "ffn_w_up", "ffn_w_down", "mix_pre_g", "mix_post_g", "gmlp_w_in",
             "gmlp_ln_g", "gmlp_ln_b", "gmlp_w_s", "gmlp_b_s", "gmlp_w_out", "kv_norm_g", "w_dkv", "kv_a_norm_g", "w_ukv",
             "mla_w_dq", "mla_q_norm_g", "mla_w_uq", "mla_w_o"]
    env = locals()
    w = {n: env[n] for n in names}
    mom = {n: env["m_" + n] for n in names}
    var = {n: env["v_" + n] for n in names}

    bsz, seq, d = x.shape
    t = bsz * seq
    core = lax.axis_index("c").astype(jnp.int32).reshape(1)

    mats = [("gmlp_w_in", 2), ("gmlp_w_out", 1), ("w_dkv", 0), ("w_ukv", 1), ("mla_w_dq", 1), ("mla_w_uq", 2),
            ("mla_w_o", 1)]
    vecs = [("ffn_pre_g", 2), ("ffn_post_g", 2), ("gmlp_ln_g", 1), ("gmlp_ln_b", 1)]
    replicated = ["mix_pre_g", "mix_post_g", "gmlp_w_s", "gmlp_b_s", "kv_norm_g", "kv_a_norm_g", "mla_q_norm_g"]
    n_mats = sum(w[n].size for n, _ in mats)
    n_vecs = sum(w[n].size for n, _ in vecs)
    mat_rows = _round_up(-(-n_mats // PACK_WIDTH), 64)
    vec_rows = _round_up(-(-n_vecs // 128), 32)
    mat_pack = _pack_flat([w[n] for n, _ in mats], mat_rows, PACK_WIDTH, BF16).reshape(2, 2, mat_rows // 4, PACK_WIDTH)
    vec_pack = _pack_flat([w[n] for n, _ in vecs], vec_rows, 128, F32).reshape(2, 2, vec_rows // 4, 128)
    ffn_names = ("ffn_w_gate", "ffn_w_up", "ffn_w_down")
    lj = [(l, j) for l in range(2) for j in range(2)]
    riders = {(0, 0): [vec_pack], (0, 1): [mat_pack], (1, 0): [], (1, 1): []}
    ffn_w = {}
    for q, (l, j) in enumerate(lj):
        shards = [w[n][l, j].astype(BF16) for n in ffn_names]
        got = _all_gather([s.reshape(2, 2, s.shape[0] // 4, s.shape[1]) for s in shards] + riders[(l, j)],
                          q + 1, f"gather_weights_{q}")
        ffn_w[(l, j)] = [g.reshape((N_SHARDS,) + s.shape) for g, s in zip(got, shards)]
        if (l, j) == (0, 0):
            vec_all = got[3]
        if (l, j) == (0, 1):
            mat_all = got[3]

    full = {}
    flat4 = mat_all.reshape(N_SHARDS, -1)
    off = 0
    for n, ax in mats:
        full[n] = _merge_shards(flat4[:, off:off + w[n].size].reshape((N_SHARDS,) + w[n].shape), ax)
        off += w[n].size
    flat4 = vec_all.reshape(N_SHARDS, -1)
    off = 0
    for n, ax in vecs:
        full[n] = _merge_shards(flat4[:, off:off + w[n].size].reshape((N_SHARDS,) + w[n].shape), ax)
        off += w[n].size

    w_in, w_out = full["gmlp_w_in"][0], full["gmlp_w_out"][0]
    ln_g, ln_b = full["gmlp_ln_g"], full["gmlp_ln_b"]
    w_c, w_kr = full["w_dkv"][:, :KV_RANK], full["w_dkv"][:, KV_RANK:]
    w_kr_rot = _rot_cols(w_kr)
    ukv = full["w_ukv"].reshape(KV_RANK, N_HEADS, 2, QK_NOPE)
    w_k, w_v = ukv[:, :, 0].reshape(KV_RANK, -1), ukv[:, :, 1].reshape(KV_RANK, -1)
    w_dq, w_o = full["mla_w_dq"][0], full["mla_w_o"][0]
    q_rank = w_dq.shape[1]
    uq = full["mla_w_uq"][0].reshape(q_rank, N_HEADS, QK_NOPE + QK_ROPE)
    w_qn = uq[:, :, :QK_NOPE].reshape(q_rank, -1)
    w_qr = uq[:, :, QK_NOPE:].reshape(q_rank, -1)
    w_qr_rot = _rot_cols(uq[:, :, QK_NOPE:]).reshape(q_rank, -1)
    pre_g, post_g = full["ffn_pre_g"], full["ffn_post_g"]
    w_s = w["gmlp_w_s"][0]
    bsb = jnp.broadcast_to(w["gmlp_b_s"][0][:, :, None], (GROUPS, CHUNK, CHUNK))
    row = lambda v: v.reshape(1, -1)

    inv_freq = ROPE_THETA ** (-jnp.arange(0, QK_ROPE, 2, dtype=F32) / QK_ROPE)
    ang = positions.astype(F32).reshape(t, 1) * inv_freq
    cos2 = jnp.concatenate([jnp.cos(ang)] * 2, axis=-1)
    sin2 = jnp.concatenate([jnp.sin(ang)] * 2, axis=-1)
    cos_h, sin_h = jnp.tile(cos2, (1, N_HEADS)), jnp.tile(sin2, (1, N_HEADS))

    def rope_epi(n_lin):
        def epi(accs, ex):
            return accs[:n_lin] + [accs[n_lin] * ex[0] + accs[n_lin + 1] * ex[1]]
        return epi

    h0 = x.reshape(t, d)
    saved = {}

    def ffn_fwd(l, j, h, n, next_gs):
        wg, wu, wd = ffn_w[(l, j)]
        g, u, a = _ffn_up(n, wg, wu)
        f, h_new, *n_next = _ffn_down(a, wd, h, row(post_g[l, j]), next_gs)
        saved[("ffn", l, j)] = (h, n, g, u, a, f)
        return h_new, n_next

    n0 = _rms_fwd(h0, row(pre_g[0, 0]))
    h1, (n1,) = ffn_fwd(0, 0, h0, n0, row(w["mix_pre_g"][0]))
    zp = _mm2d("gmlp_in", [(n1, w_in, "nn", 0)], [(w_in.shape[1], F32)])[0]
    uv = _sgu_fwd(zp, ln_g, ln_b, w_s, bsb)
    half = uv.shape[1]
    tm = min(t, ROW_TILE)
    m0, h2, n2 = _down("gmlp_out", (uv, (tm, 512), lambda i, _, k: (i, k)), (w_out, (512, d), lambda i, _, k: (k, 0)),
                       half // 512, h1, row(w["mix_post_g"][0]), row(pre_g[0, 1]), 1.0)
    h3, (n3kv, n3) = ffn_fwd(0, 1, h2, n2, jnp.stack([w["kv_norm_g"], pre_g[1, 0]]))

    def kv_epi(accs, ex):
        c_raw = accs[0]
        return [c_raw, _rms(c_raw, ex[2]), accs[1] * ex[0] + accs[2] * ex[1]]

    c_raw, c_n, k_r = _mm2d("kv_down", [(n3kv, w_c, "nn", 0), (n3kv, w_kr, "nn", 1), (n3kv, w_kr_rot, "nn", 2)],
                            [(KV_RANK, F32), (KV_RANK, BF16), (QK_ROPE, BF16)], kv_epi, [cos2, sin2],
                            [row(w["kv_a_norm_g"])])
    k_n, v_h = _mm2d("kv_up", [(c_n, w_k, "nn", 0), (c_n, w_v, "nn", 1)], [(w_k.shape[1], BF16), (w_v.shape[1], BF16)])

    h4, (n4,) = ffn_fwd(1, 0, h3, n3, row(w["mix_pre_g"][1]))
    qd, qn = _mm2d("q_down", [(n4, w_dq, "nn", 0)], [(q_rank, F32), (q_rank, BF16)],
                   lambda accs, ex: [accs[0], _rms(accs[0], ex[0])], [], [row(w["mla_q_norm_g"][0])])
    q_n, q_r = _mm2d("q_up", [(qn, w_qn, "nn", 0), (qn, w_qr, "nn", 1), (qn, w_qr_rot, "nn", 2)],
                     [(w_qn.shape[1], BF16), (w_qr.shape[1], BF16)], rope_epi(1), [cos_h, sin_h])
    q_r = q_r.reshape(t, N_HEADS, QK_ROPE).transpose(1, 0, 2)
    o, lse = _attn_fwd(q_n, q_r, k_n, v_h, k_r, seq)
    m1, h5, n5 = _down("attn_out", (o, (tm, 512), lambda i, _, k: (i, k)), (w_o, (512, d), lambda i, _, k: (k, 0)),
                       o.shape[1] // 512, h4, row(w["mix_post_g"][1]), row(pre_g[1, 1]), 1.0)
    y, _ = ffn_fwd(1, 1, h5, n5, row(pre_g[1, 1]))

    loss_part, dy = _loss_head(y, loss_target.reshape(t, d))
    loss = lax.psum(loss_part, ("x", "y", "c"))

    grads = {}
    ffn_dw = {}
    d_pre, d_post = {}, {}

    def ffn_bwd(l, j, dh_out, extra=()):
        h, n, g, u, a, f = saved[("ffn", l, j)]
        df, d_post[(l, j)] = _norm_out_bwd("ffn_post_bwd", f, dh_out, row(post_g[l, j]), 0.5)
        wg, wu, wd = ffn_w[(l, j)]
        dg, du = _ffn_dact(df, wd, g, u)
        dwd = _ffn_dw_down(a, df)
        dn = _ffn_dn(dg, du, wg, wu)
        dwg, dwu = _ffn_dw_in(n, dg, du)
        ffn_dw[(l, j)] = (dwg, dwu, dwd)
        dh, d_pre[(l, j)], *rest = _norm_in_bwd("ffn_pre_bwd", h, dh_out, [(row(pre_g[l, j]), dn)] + list(extra))
        return dh, rest

    dh5, _ = ffn_bwd(1, 1, dy)

    dm1, g_mix_post1 = _norm_out_bwd("mix_post_bwd", m1, dh5, row(w["mix_post_g"][1]), 1.0)
    do = _mm2d("attn_out_dx", [(dm1, w_o, "nt", 0)], [(w_o.shape[0], BF16)])[0]
    g_w_o = _mm2d("attn_out_dw", [(o, dm1, "tn", 0)], [(d, F32)])[0]
    dq_n, dk_n, dv_h, dq_c, dq_s, dk_r = _attn_bwd(q_n, q_r, k_n, v_h, k_r, do, lse, cos2, sin2, seq)
    dq_c = dq_c.transpose(1, 0, 2).reshape(t, -1)
    dq_s = dq_s.transpose(1, 0, 2).reshape(t, -1)
    dqn = _mm2d("q_up_dx", [(dq_n, w_qn, "nt", 0), (dq_c, w_qr, "nt", 0), (dq_s, w_qr_rot, "nt", 0)], [(q_rank, F32)])[0]
    g_qn, g_qr, g_qr_rot = _mm2d("q_up_dw", [(qn, dq_n, "tn", 0), (qn, dq_c, "tn", 1), (qn, dq_s, "tn", 2)],
                                 [(w_qn.shape[1], F32), (w_qr.shape[1], F32), (w_qr.shape[1], F32)])
    dqd, g_q_norm = _norm_out_bwd("q_norm_bwd", qd, dqn, row(w["mla_q_norm_g"][0]), 1.0)
    dn4 = _mm2d("q_down_dx", [(dqd, w_dq, "nt", 0)], [(d, F32)])[0]
    g_w_dq = _mm2d("q_down_dw", [(n4, dqd, "tn", 0)], [(q_rank, F32)])[0]
    dh4, g_mix_pre1 = _norm_in_bwd("mix_pre_bwd", h4, dh5, [(row(w["mix_pre_g"][1]), dn4)])

    dc_n = _mm2d("kv_up_dx", [(dk_n, w_k, "nt", 0), (dv_h, w_v, "nt", 0)], [(KV_RANK, F32)])[0]
    g_wk, g_wv = _mm2d("kv_up_dw", [(c_n, dk_n, "tn", 0), (c_n, dv_h, "tn", 1)], [(w_k.shape[1], F32), (w_v.shape[1], F32)])
    dc, g_kv_a = _norm_out_bwd("kv_a_norm_bwd", c_raw, dc_n, row(w["kv_a_norm_g"]), 1.0)
    dkr_c, dkr_s = _rope_bwd(dk_r, cos2, sin2)
    dn3kv = _mm2d("kv_down_dx", [(dc, w_c, "nt", 0), (dkr_c, w_kr, "nt", 0), (dkr_s, w_kr_rot, "nt", 0)], [(d, F32)])[0]
    g_wc, g_wkr, g_wkr_rot = _mm2d("kv_down_dw", [(n3kv, dc, "tn", 0), (n3kv, dkr_c, "tn", 1), (n3kv, dkr_s, "tn", 2)],
                                   [(KV_RANK, F32), (QK_ROPE, F32), (QK_ROPE, F32)])

    dh3, (g_kv_norm,) = ffn_bwd(1, 0, dh4, extra=[(row(w["kv_norm_g"]), dn3kv)])
    dh2, _ = ffn_bwd(0, 1, dh3)

    dm0, g_mix_post0 = _norm_out_bwd("mix_post_bwd", m0, dh2, row(w["mix_post_g"][0]), 1.0)
    d_uv = _mm2d("gmlp_out_dx", [(dm0, w_out, "nt", 0)], [(half, F32)])[0]
    g_w_out = _mm2d("gmlp_out_dw", [(uv, dm0, "tn", 0)], [(d, F32)])[0]
    dzp, g_ln_g, g_ln_b, g_w_s, g_b_s = _sgu_bwd(zp, d_uv, ln_g, ln_b, w_s, bsb)
    dn1 = _mm2d("gmlp_in_dx", [(dzp, w_in, "nt", 0)], [(d, F32)])[0]
    g_w_in = _mm2d("gmlp_in_dw", [(n1, dzp, "tn", 0)], [(w_in.shape[1], F32)])[0]
    dh1, g_mix_pre0 = _norm_in_bwd("mix_pre_bwd", h1, dh2, [(row(w["mix_pre_g"][0]), dn1)])
    dx, _ = ffn_bwd(0, 0, dh1)

    lj = [(l, j) for l in range(2) for j in range(2)]
    part = {
        "gmlp_w_in": g_w_in[None], "gmlp_w_out": g_w_out[None],
        "w_dkv": jnp.concatenate([g_wc, g_wkr + _unrot_cols(g_wkr_rot)], axis=1),
        "w_ukv": jnp.stack([g_wk.reshape(KV_RANK, N_HEADS, QK_NOPE), g_wv.reshape(KV_RANK, N_HEADS, V_DIM)],
                           axis=2).reshape(KV_RANK, -1),
        "mla_w_dq": g_w_dq[None],
        "mla_w_uq": jnp.concatenate(
            [g_qn.reshape(q_rank, N_HEADS, QK_NOPE),
             g_qr.reshape(q_rank, N_HEADS, QK_ROPE) + _unrot_cols(g_qr_rot.reshape(q_rank, N_HEADS, QK_ROPE))],
            axis=-1).reshape(1, q_rank, -1),
        "mla_w_o": g_w_o[None],
        "ffn_pre_g": jnp.concatenate([d_pre[k] for k in lj]).reshape(2, 2, d),
        "ffn_post_g": jnp.concatenate([d_post[k] for k in lj]).reshape(2, 2, d),
        "gmlp_ln_g": g_ln_g, "gmlp_ln_b": g_ln_b,
        "mix_pre_g": jnp.concatenate([g_mix_pre0, g_mix_pre1]), "mix_post_g": jnp.concatenate([g_mix_post0, g_mix_post1]),
        "gmlp_w_s": g_w_s[None], "gmlp_b_s": g_b_s.reshape(1, GROUPS, CHUNK),
        "kv_norm_g": g_kv_norm.reshape(-1), "kv_a_norm_g": g_kv_a.reshape(-1), "mla_q_norm_g": g_q_norm,
    }

    sharded = mats + vecs
    n_sh = n_mats + n_vecs
    n_rep = sum(w[n].size for n in replicated)
    sh_rows = _round_up(-(-n_sh // PACK_WIDTH), 8)
    rep_rows = _round_up(-(-(n_rep // N_SHARDS) // PACK_WIDTH), 8)
    rows = _round_up(sh_rows + rep_rows, 32)
    sh_flat = jnp.concatenate([_split_shards(part[n], ax) for n, ax in sharded], axis=1)
    rep_flat = jnp.concatenate([part[n].reshape(-1) for n in replicated]).reshape(N_SHARDS, -1)
    small = jnp.concatenate([
        jnp.pad(sh_flat, ((0, 0), (0, sh_rows * PACK_WIDTH - n_sh))),
        jnp.pad(rep_flat, ((0, 0), (0, (rows - sh_rows) * PACK_WIDTH - n_rep // N_SHARDS)))], axis=1)
    small = small.astype(BF16).reshape(N_SHARDS, 2, rows // 2, PACK_WIDTH)

    parts, where = [], []
    for l, j in lj:
        for q, g in enumerate(ffn_dw[(l, j)]):
            parts.append(g.reshape(N_SHARDS, 2, g.shape[1] // 2, g.shape[2]))
            where.append((q, (l, j)))
    parts.append(small)
    where.append((3, ()))
    chip = (2 * lax.axis_index("x") + lax.axis_index("y")).astype(jnp.int32).reshape(1)
    others = _swap_halves(parts)
    chip_sums = [_add_half(p, o, core) for p, o in zip(parts, others)]
    slots = _scatter_chips(chip_sums)
    halves = [_sum_slots(s, p, chip) for s, p in zip(slots, chip_sums)]
    out_shapes = [(2, 2) + parts[q].shape[2:] for q in range(3)] + [(rows // 2, PACK_WIDTH)]
    recv = _join_halves(halves, out_shapes, where)
    own_small = halves[-1]
    g_small = jnp.where(core[0] == 0, jnp.concatenate([own_small, recv[3]]), jnp.concatenate([recv[3], own_small]))
    g_rep = _gather_rows(g_small, sh_rows, rep_rows)

    grads, delta, new_m, new_v = {}, {}, {}, {}
    for q, n in enumerate(["ffn_w_gate", "ffn_w_up", "ffn_w_down"]):
        own = jnp.stack(halves[q:12:3]).reshape(recv[q].shape)
        grads[n], delta[n], new_m[n], new_v[n] = _adamw_halves(w[n], own, recv[q], mom[n], var[n], core)
    for (n, _), g in zip(sharded, _split_flat(g_small.reshape(-1), [w[n].shape for n, _ in sharded])):
        grads[n] = g
    rep_vec = g_rep.reshape(N_SHARDS, -1)[:, :n_rep // N_SHARDS].reshape(-1)
    for n, g in zip(replicated, _split_flat(rep_vec, [w[n].shape for n in replicated])):
        grads[n] = g

    for n in names:
        if n not in delta:
            delta[n], new_m[n], new_v[n] = _adamw(w[n], grads[n], mom[n], var[n])
    return (loss, dx.reshape(x.shape), *[grads[n] for n in names], *[delta[n] for n in names],
            *[new_m[n] for n in names], *[new_v[n] for n in names])
```

```python
import math

import jax
import jax.numpy as jnp
from jax import lax
from jax.experimental import pallas as pl
from jax.experimental.pallas import tpu as pltpu
from jax.experimental.pallas import tpu_sc as plsc

F32, BF16 = jnp.float32, jnp.bfloat16

RMS_EPS, LN_EPS, NEG_INF = 1e-6, 1e-5, -1e30
N_HEADS, QK_NOPE, QK_ROPE, V_DIM, KV_RANK = 8, 128, 64, 128, 256
CHUNK, GROUPS = 128, 16
ROPE_THETA = 10000.0
ADAM_LR, ADAM_B1, ADAM_B2, ADAM_EPS, ADAM_WD, ADAM_STEP = 0.001, 0.9, 0.999, 1e-08, 0.01, 10
N_SHARDS = 4

VMEM_LIMIT_BYTES = 48 * 1024 * 1024
ROW_TILE = 512
PACK_WIDTH = 1024

_DN = {"nn": (((1,), (0,)), ((), ())), "nt": (((1,), (1,)), ((), ())), "tn": (((0,), (0,)), ((), ()))}
_MESH = pl.DeviceIdType.MESH
_ANY = pl.BlockSpec(memory_space=pl.ANY)


def _params(sem):
    return pltpu.CompilerParams(dimension_semantics=sem, vmem_limit_bytes=VMEM_LIMIT_BYTES)


def _mm(name, grid, ins, pairs, acc_shapes, outs, epilogue, extras=()):
    n_in, n_ex, n_out = len(ins), len(extras), len(outs)
    gk = grid[2]

    def body(*refs):
        in_refs, ex_refs = refs[:n_in], refs[n_in:n_in + n_ex]
        out_refs = refs[n_in + n_ex:n_in + n_ex + n_out]
        acc_refs = refs[n_in + n_ex + n_out:]
        parts = [None] * len(acc_shapes)
        for a, b, c, dims in pairs:
            p = lax.dot_general(in_refs[a][...], in_refs[b][...], _DN[dims], preferred_element_type=F32)
            parts[c] = p if parts[c] is None else parts[c] + p

        def finish(accs):
            vals = epilogue(accs, [r[...] for r in ex_refs])
            for r, v in zip(out_refs, vals):
                r[...] = v.astype(r.dtype)

        if gk == 1:
            finish(parts)
        else:
            k = pl.program_id(2)

            @pl.when(k == 0)
            def _():
                for r, p in zip(acc_refs, parts):
                    r[...] = p

            @pl.when(k > 0)
            def _():
                for r, p in zip(acc_refs, parts):
                    r[...] += p

            @pl.when(k == gk - 1)
            def _():
                finish([r[...] for r in acc_refs])

    return pl.pallas_call(
        body,
        out_shape=[jax.ShapeDtypeStruct(s, d) for s, d, _, _ in outs],
        grid=grid,
        in_specs=[pl.BlockSpec(bs, im) for _, bs, im in list(ins) + list(extras)],
        out_specs=[pl.BlockSpec(bs, im) for _, _, bs, im in outs],
        scratch_shapes=[pltpu.VMEM(s, F32) for s in acc_shapes] if gk > 1 else [],
        name=name,
        compiler_params=_params(("parallel", "parallel", "arbitrary")),
    )(*[a for a, _, _ in ins], *[a for a, _, _ in extras])


def _mm2d(name, pairs, outs, epilogue=None, row_extras=(), vec_extras=()):
    def mk(a, dims):
        return (a.shape[0], a.shape[1]) if dims[0] == "n" else (a.shape[1], a.shape[0])

    def nk(b, dims):
        return (b.shape[1], b.shape[0]) if dims[1] == "n" else (b.shape[0], b.shape[1])

    m = mk(pairs[0][0], pairs[0][2])[0]
    ks = [mk(a, d)[1] for a, _, d, _ in pairs]
    n_acc = 1 + max(p[3] for p in pairs)
    acc_n = [None] * n_acc
    for a, b, d, c in pairs:
        assert mk(a, d)[0] == m and nk(b, d)[1] == mk(a, d)[1]
        acc_n[c] = nk(b, d)[0]
    tm = min(m, ROW_TILE)
    if len(set(ks)) == 1 and ks[0] > 1024:
        tks, gk = [512] * len(pairs), ks[0] // 512
    else:
        tks, gk = ks, 1
    if len(set(acc_n)) == 1 and acc_n[0] > 1024:
        tns, gj = [1024] * n_acc, acc_n[0] // 1024
    else:
        tns, gj = acc_n, 1

    ins, plist = [], []
    for (a, b, d, c), tk in zip(pairs, tks):
        tn = tns[c]
        a_spec = ((tm, tk), lambda i, j, k: (i, k)) if d[0] == "n" else ((tk, tm), lambda i, j, k: (k, i))
        b_spec = ((tk, tn), lambda i, j, k: (k, j)) if d[1] == "n" else ((tn, tk), lambda i, j, k: (j, k))
        ins += [(a, *a_spec), (b, *b_spec)]
        plist.append((len(ins) - 2, len(ins) - 1, c, d))
    extras = [(r, (tm, r.shape[1]), lambda i, j, k: (i, 0)) for r in row_extras]
    extras += [(v, v.shape, lambda i, j, k: (0, 0)) for v in vec_extras]
    out_specs = []
    for n, dt in outs:
        bn = 1024 if (gj > 1) else n
        out_specs.append(((m, n), dt, (tm, bn), lambda i, j, k: (i, j)))
    if epilogue is None:
        epilogue = lambda accs, ex: accs
    return _mm(name, (m // tm, gj, gk), ins, plist, [(tm, tn) for tn in tns], out_specs, epilogue, extras)


def _rms(x, g):
    return x * lax.rsqrt(jnp.mean(x * x, axis=-1, keepdims=True) + RMS_EPS) * g


def _rms_bwd(x, g, dy):
    r = lax.rsqrt(jnp.mean(x * x, axis=-1, keepdims=True) + RMS_EPS)
    gy = dy * g
    dx = r * gy - x * (r * r * r) * jnp.mean(gy * x, axis=-1, keepdims=True)
    return dx, jnp.sum(dy * x * r, axis=0, keepdims=True)


def _sigmoid(x):
    return 1.0 / (1.0 + jnp.exp(-x))


_GELU_C = math.sqrt(2.0 / math.pi)


def _gelu(x):
    return x * (0.5 * (1.0 + jnp.tanh(_GELU_C * (x + 0.044715 * (x * x * x)))))


def _gelu_grad(x):
    t = jnp.tanh(_GELU_C * (x + 0.044715 * (x * x * x)))
    return 0.5 * (1.0 + t) + 0.5 * x * (1.0 - t * t) * (_GELU_C * (1.0 + 3.0 * 0.044715 * (x * x)))


def _rows(name, row_ins, vec_ins, fn, row_outs, acc_outs=()):
    t = row_ins[0].shape[0]
    tm = min(t, ROW_TILE)
    nr, nv, no = len(row_ins), len(vec_ins), len(row_outs)

    def body(*refs):
        outs, incs = fn([r[...] for r in refs[:nr]], [r[...] for r in refs[nr:nr + nv]])
        for r, v in zip(refs[nr + nv:nr + nv + no], outs):
            r[...] = v.astype(r.dtype)
        i = pl.program_id(0)
        for r, v in zip(refs[nr + nv + no:], incs):
            @pl.when(i == 0)
            def _():
                r[...] = v

            @pl.when(i > 0)
            def _():
                r[...] += v

    in_specs = [pl.BlockSpec((tm, a.shape[1]), lambda i: (i, 0)) for a in row_ins]
    in_specs += [pl.BlockSpec(v.shape, lambda i, nd=v.ndim: (0,) * nd) for v in vec_ins]
    out_shape = [jax.ShapeDtypeStruct((t, c), dt) for c, dt in row_outs]
    out_shape += [jax.ShapeDtypeStruct(s, F32) for s in acc_outs]
    out_specs = [pl.BlockSpec((tm, c), lambda i: (i, 0)) for c, _ in row_outs]
    out_specs += [pl.BlockSpec(s, lambda i, nd=len(s): (0,) * nd) for s in acc_outs]
    return pl.pallas_call(body, out_shape=out_shape, grid=(t // tm,), in_specs=in_specs, out_specs=out_specs,
                          name=name, compiler_params=_params(("arbitrary",)))(*row_ins, *vec_ins)


def _rms_fwd(x, g):
    return _rows("rms_fwd", [x], [g], lambda r, v: ([_rms(r[0], v[0])], []), [(x.shape[1], BF16)])[0]


def _norm_out_bwd(name, f, d_out, g, scale):
    def fn(r, v):
        dx, dg = _rms_bwd(r[0], v[0], r[1] * scale)
        return [dx], [dg]

    c = f.shape[1]
    return _rows(name, [f, d_out], [g], fn, [(c, BF16)], [(1, c)])


def _norm_in_bwd(name, h, d_res, branches):
    nb = len(branches)

    def fn(r, v):
        dh, dgs = r[1], []
        for b in range(nb):
            dx, dg = _rms_bwd(r[0], v[b], r[2 + b])
            dh = dh + dx
            dgs.append(dg)
        return [dh], dgs

    c = h.shape[1]
    return _rows(name, [h, d_res] + [dn for _, dn in branches], [g for g, _ in branches], fn, [(c, F32)],
                 [(1, c)] * nb)


def _loss_head(y, target):
    d = y.shape[1]

    def fn(r, v):
        e = r[0] - r[1]
        s = jnp.sum(jnp.sum(e * e, axis=1, keepdims=True), axis=0, keepdims=True) * (0.5 / d)
        return [e * (1.0 / d)], [jnp.broadcast_to(s, (1, 128))]

    dy, acc = _rows("loss_head", [y, target], [], fn, [(d, F32)], [(1, 128)])
    return acc[0, 0], dy


def _rope_bwd(dk, cos2, sin2):
    c = dk.shape[1]
    return _rows("rope_bwd", [dk, cos2, sin2], [], lambda r, v: ([r[0] * r[1], r[0] * r[2]], []),
                 [(c, BF16), (c, BF16)])


def _ffn_up(n, wg, wu):
    t, d = n.shape
    fs = wg.shape[-1]
    tm = min(t, ROW_TILE)
    w_spec = ((None, d, fs), lambda s, i, k: (s, 0, 0))

    def epi(accs, ex):
        g, u = accs
        return [g, u, g * _sigmoid(g) * u]

    o_spec = ((None, tm, fs), lambda s, i, k: (s, i, 0))
    outs = [((N_SHARDS, t, fs), F32, *o_spec), ((N_SHARDS, t, fs), F32, *o_spec), ((N_SHARDS, t, fs), BF16, *o_spec)]
    return _mm("ffn_up", (N_SHARDS, t // tm, 1),
               [(n, (tm, d), lambda s, i, k: (i, 0)), (wg, *w_spec), (wu, *w_spec)],
               [(0, 1, 0, "nn"), (0, 2, 1, "nn")], [(tm, fs)] * 2, outs, epi)


def _down(name, a_in, w_in, gk, h, post_g, next_gs, scale):
    t, d = h.shape
    tm = min(t, ROW_TILE)
    kn = next_gs.shape[0]

    def epi(accs, ex):
        f, hv, pg, ng = accs[0], ex[0], ex[1], ex[2]
        hn = hv + scale * _rms(f, pg)
        return [f, hn] + [_rms(hn, ng[q:q + 1]) for q in range(kn)]

    row = ((tm, d), lambda i, j, k: (i, 0))
    outs = [((t, d), F32, *row), ((t, d), F32, *row)] + [((t, d), BF16, *row)] * kn
    extras = [(h, *row), (post_g, (1, d), lambda i, j, k: (0, 0)), (next_gs, (kn, d), lambda i, j, k: (0, 0))]
    return _mm(name, (t // tm, 1, gk), [a_in, w_in], [(0, 1, 0, "nn")], [(tm, d)], outs, epi, extras)


def _ffn_down(a, wd, h, post_g, next_gs):
    t, d = h.shape
    fs = a.shape[-1]
    tm = min(t, ROW_TILE)
    return _down("ffn_down", (a, (None, tm, fs), lambda i, _, s: (s, i, 0)),
                 (wd, (None, fs, d), lambda i, _, s: (s, 0, 0)), N_SHARDS, h, post_g, next_gs, 0.5)


def _ffn_dact(df, wd, g, u):
    t, d = df.shape
    fs = g.shape[-1]
    tm = min(t, ROW_TILE)

    def epi(accs, ex):
        da, gv, uv = accs[0], ex[0], ex[1]
        sg = _sigmoid(gv)
        return [da * uv * (sg * (1.0 + gv * (1.0 - sg))), da * (gv * sg)]

    o_spec = ((None, tm, fs), lambda s, i, k: (s, i, 0))
    outs = [((N_SHARDS, t, fs), BF16, *o_spec)] * 2
    return _mm("ffn_dact", (N_SHARDS, t // tm, 1),
               [(df, (tm, d), lambda s, i, k: (i, 0)), (wd, (None, fs, d), lambda s, i, k: (s, 0, 0))],
               [(0, 1, 0, "nt")], [(tm, fs)], outs, epi, [(g, *o_spec), (u, *o_spec)])


def _ffn_dn(dg, du, wg, wu):
    _, t, fs = dg.shape
    d = wg.shape[-2]
    tm = min(t, ROW_TILE)
    a_spec = ((None, tm, fs), lambda i, _, s: (s, i, 0))
    w_spec = ((None, d, fs), lambda i, _, s: (s, 0, 0))
    outs = [((t, d), F32, (tm, d), lambda i, _, s: (i, 0))]
    return _mm("ffn_dn", (t // tm, 1, N_SHARDS), [(dg, *a_spec), (wg, *w_spec), (du, *a_spec), (wu, *w_spec)],
               [(0, 1, 0, "nt"), (2, 3, 0, "nt")], [(tm, d)], outs, lambda accs, ex: accs)[0]


def _ffn_dw_in(n, dg, du):
    _, t, fs = dg.shape
    d = n.shape[1]
    tk = min(t, ROW_TILE)
    b_spec = ((None, tk, fs), lambda s, _, k: (s, k, 0))
    o_spec = ((None, d, fs), lambda s, _, k: (s, 0, 0))
    outs = [((N_SHARDS, d, fs), BF16, *o_spec)] * 2
    return _mm("ffn_dw_in", (N_SHARDS, 1, t // tk), [(n, (tk, d), lambda s, _, k: (k, 0)), (dg, *b_spec), (du, *b_spec)],
               [(0, 1, 0, "tn"), (0, 2, 1, "tn")], [(d, fs)] * 2, outs, lambda accs, ex: accs)


def _ffn_dw_down(a, df):
    _, t, fs = a.shape
    d = df.shape[1]
    tk = min(t, ROW_TILE)
    outs = [((N_SHARDS, fs, d), BF16, (None, fs, d), lambda s, _, k: (s, 0, 0))]
    return _mm("ffn_dw_down", (N_SHARDS, 1, t // tk),
               [(a, (None, tk, fs), lambda s, _, k: (s, k, 0)), (df, (tk, d), lambda s, _, k: (k, 0))],
               [(0, 1, 0, "tn")], [(fs, d)], outs, lambda accs, ex: accs)[0]


def _causal_weight(w):
    row = lax.broadcasted_iota(jnp.int32, (CHUNK, CHUNK), 0)
    col = lax.broadcasted_iota(jnp.int32, (CHUNK, CHUNK), 1)
    return row >= col, jnp.where(row >= col, w, 0.0).astype(BF16)


def _layer_norm(v, g, b):
    xc = v - jnp.mean(v, axis=-1, keepdims=True)
    rstd = lax.rsqrt(jnp.mean(xc * xc, axis=-1, keepdims=True) + LN_EPS)
    xhat = xc * rstd
    return xhat, rstd, xhat * g + b


def _sgu_specs(t, half, tm):
    return [pl.BlockSpec((tm, half), lambda i: (i, 0)), pl.BlockSpec((tm, half), lambda i: (i, 1))]


def _sgu_fwd(zp, ln_g, ln_b, w_s, bsb):
    t, half = zp.shape[0], zp.shape[1] // 2
    tm = min(t, 2 * CHUNK)

    def body(u_ref, v_ref, g_ref, b_ref, w_ref, bs_ref, o_ref):
        u = _gelu(u_ref[...])
        _, _, vn = _layer_norm(_gelu(v_ref[...]), g_ref[...], b_ref[...])
        vb = vn.astype(BF16)
        for g in range(GROUPS):
            _, wm = _causal_weight(w_ref[g])
            cols = slice(g * CHUNK, (g + 1) * CHUNK)
            for c in range(tm // CHUNK):
                rows = slice(c * CHUNK, (c + 1) * CHUNK)
                sv = jnp.dot(wm, vb[rows, cols], preferred_element_type=F32) + bs_ref[g]
                o_ref[rows, cols] = (u[rows, cols] * sv).astype(BF16)

    whole = lambda a: pl.BlockSpec(a.shape, lambda i, nd=a.ndim: (0,) * nd)
    return pl.pallas_call(
        body, out_shape=jax.ShapeDtypeStruct((t, half), BF16), grid=(t // tm,),
        in_specs=_sgu_specs(t, half, tm) + [whole(ln_g), whole(ln_b), whole(w_s), whole(bsb)],
        out_specs=pl.BlockSpec((tm, half), lambda i: (i, 0)), name="sgu_fwd",
        compiler_params=_params(("arbitrary",)))(zp, zp, ln_g, ln_b, w_s, bsb)


def _sgu_bwd(zp, d_uv, ln_g, ln_b, w_s, bsb):
    t, half = zp.shape[0], zp.shape[1] // 2
    tm = min(t, 2 * CHUNK)

    def body(u_ref, v_ref, d_ref, g_ref, b_ref, w_ref, bs_ref, dz_ref, dlg_ref, dlb_ref, dws_ref, dbs_ref, dvn_ref):
        i = pl.program_id(0)

        @pl.when(i == 0)
        def _():
            dlg_ref[...] = jnp.zeros_like(dlg_ref)
            dlb_ref[...] = jnp.zeros_like(dlb_ref)
            dws_ref[...] = jnp.zeros_like(dws_ref)
            dbs_ref[...] = jnp.zeros_like(dbs_ref)

        up, vp = u_ref[...], v_ref[...]
        u, gup = _gelu(up), _gelu_grad(up)
        xhat, rstd, vn = _layer_norm(_gelu(vp), g_ref[...], b_ref[...])
        vb = vn.astype(BF16)
        d = d_ref[...]
        for g in range(GROUPS):
            mask, wm = _causal_weight(w_ref[g])
            cols = slice(g * CHUNK, (g + 1) * CHUNK)
            for c in range(tm // CHUNK):
                rows = slice(c * CHUNK, (c + 1) * CHUNK)
                blk = vb[rows, cols]
                sv = jnp.dot(wm, blk, preferred_element_type=F32) + bs_ref[g]
                dblk = d[rows, cols]
                dz_ref[rows, cols] = (dblk * sv * gup[rows, cols]).astype(BF16)
                dsv = dblk * u[rows, cols]
                dsvb = dsv.astype(BF16)
                dvn_ref[rows, cols] = lax.dot_general(wm, dsvb, _DN["tn"], preferred_element_type=F32)
                dw = lax.dot_general(dsvb, blk, _DN["nt"], preferred_element_type=F32)
                dws_ref[g] += jnp.where(mask, dw, 0.0)
                dbs_ref[g] += jnp.sum(dsv, axis=1, keepdims=True)
        dvn = dvn_ref[...]
        dlg_ref[...] += jnp.sum(dvn * xhat, axis=0, keepdims=True)
        dlb_ref[...] += jnp.sum(dvn, axis=0, keepdims=True)
        dxh = dvn * g_ref[...]
        dv = rstd * (dxh - jnp.mean(dxh, axis=-1, keepdims=True)
                     - xhat * jnp.mean(dxh * xhat, axis=-1, keepdims=True))
        dz_ref[:, half:] = (dv * _gelu_grad(vp)).astype(BF16)

    whole = lambda a: pl.BlockSpec(a.shape, lambda i, nd=a.ndim: (0,) * nd)
    wshape = lambda s: pl.BlockSpec(s, lambda i, nd=len(s): (0,) * nd)
    out_shape = [jax.ShapeDtypeStruct((t, 2 * half), BF16), jax.ShapeDtypeStruct((1, half), F32),
                 jax.ShapeDtypeStruct((1, half), F32), jax.ShapeDtypeStruct(w_s.shape, F32),
                 jax.ShapeDtypeStruct((GROUPS, CHUNK, 1), F32)]
    return pl.pallas_call(
        body, out_shape=out_shape, grid=(t // tm,),
        in_specs=_sgu_specs(t, half, tm) + [pl.BlockSpec((tm, half), lambda i: (i, 0)), whole(ln_g), whole(ln_b),
                                            whole(w_s), whole(bsb)],
        out_specs=[pl.BlockSpec((tm, 2 * half), lambda i: (i, 0)), wshape((1, half)), wshape((1, half)),
                   wshape(w_s.shape), wshape((GROUPS, CHUNK, 1))],
        scratch_shapes=[pltpu.VMEM((tm, half), F32)], name="sgu_bwd",
        compiler_params=_params(("arbitrary",)))(zp, zp, d_uv, ln_g, ln_b, w_s, bsb)


_SCALE = (QK_NOPE + QK_ROPE) ** -0.5


def _attn_scores(qn, qr, kn, kr, i, tq, n):
    s = lax.dot_general(qn, kn, _DN["nt"], preferred_element_type=F32)
    s = (s + lax.dot_general(qr, kr, _DN["nt"], preferred_element_type=F32)) * _SCALE
    row = i * tq + lax.broadcasted_iota(jnp.int32, (tq, n), 0)
    col = lax.broadcasted_iota(jnp.int32, (tq, n), 1)
    return jnp.where(col <= row, s, NEG_INF)


def _attn_specs(seq):
    head = lambda b, h: (b, h)
    return dict(
        qn=pl.BlockSpec((seq, QK_NOPE), head),
        qr=pl.BlockSpec((None, seq, QK_ROPE), lambda b, h: (h, b, 0)),
        kr=pl.BlockSpec((seq, QK_ROPE), lambda b, h: (b, 0)),
        lse=pl.BlockSpec((None, seq, 1), lambda b, h: (h, b, 0)),
    )


def _attn_fwd(qn, qr, kn, v, kr, seq):
    t = qn.shape[0]
    tq = min(seq, 2 * CHUNK)
    sp = _attn_specs(seq)

    def body(qn_ref, qr_ref, kn_ref, v_ref, kr_ref, o_ref, lse_ref):
        for i in range(seq // tq):
            rows, n = slice(i * tq, (i + 1) * tq), (i + 1) * tq
            s = _attn_scores(qn_ref[rows, :], qr_ref[rows, :], kn_ref[0:n, :], kr_ref[0:n, :], i, tq, n)
            m = jnp.max(s, axis=-1, keepdims=True)
            p = jnp.exp(s - m)
            l = jnp.sum(p, axis=-1, keepdims=True)
            o_ref[rows, :] = jnp.dot((p / l).astype(BF16), v_ref[0:n, :], preferred_element_type=F32).astype(BF16)
            lse_ref[rows, :] = m + jnp.log(l)

    return pl.pallas_call(
        body, out_shape=[jax.ShapeDtypeStruct((t, N_HEADS * V_DIM), BF16), jax.ShapeDtypeStruct((N_HEADS, t, 1), F32)],
        grid=(t // seq, N_HEADS), in_specs=[sp["qn"], sp["qr"], sp["qn"], sp["qn"], sp["kr"]],
        out_specs=[sp["qn"], sp["lse"]], name="attn_fwd",
        compiler_params=_params(("parallel", "arbitrary")))(qn, qr, kn, v, kr)


def _attn_bwd(qn, qr, kn, v, kr, do, lse, cos2, sin2, seq):
    t = qn.shape[0]
    tq = min(seq, 2 * CHUNK)
    sp = _attn_specs(seq)

    def body(qn_ref, qr_ref, kn_ref, v_ref, kr_ref, do_ref, lse_ref, cos_ref, sin_ref,
             dqn_ref, dkn_ref, dv_ref, dqc_ref, dqs_ref, dkr_ref, dk_acc, dv_acc, dkr_acc):
        dk_acc[...] = jnp.zeros_like(dk_acc)
        dv_acc[...] = jnp.zeros_like(dv_acc)
        dkr_acc[...] = jnp.zeros_like(dkr_acc)
        for i in range(seq // tq):
            rows, n = slice(i * tq, (i + 1) * tq), (i + 1) * tq
            q_n, q_r, d_o = qn_ref[rows, :], qr_ref[rows, :], do_ref[rows, :]
            k_n, k_r = kn_ref[0:n, :], kr_ref[0:n, :]
            s = _attn_scores(q_n, q_r, k_n, k_r, i, tq, n)
            p = jnp.exp(s - lse_ref[rows, :])
            dp = lax.dot_general(d_o, v_ref[0:n, :], _DN["nt"], preferred_element_type=F32)
            ds = (p * (dp - jnp.sum(p * dp, axis=-1, keepdims=True)) * _SCALE).astype(BF16)
            dqn_ref[rows, :] = jnp.dot(ds, k_n, preferred_element_type=F32).astype(BF16)
            dqr = jnp.dot(ds, k_r, preferred_element_type=F32)
            dqc_ref[rows, :] = (dqr * cos_ref[rows, :]).astype(BF16)
            dqs_ref[rows, :] = (dqr * sin_ref[rows, :]).astype(BF16)
            dk_acc[0:n, :] += lax.dot_general(ds, q_n, _DN["tn"], preferred_element_type=F32)
            dkr_acc[0:n, :] += lax.dot_general(ds, q_r, _DN["tn"], preferred_element_type=F32)
            dv_acc[0:n, :] += lax.dot_general(p.astype(BF16), d_o, _DN["tn"], preferred_element_type=F32)
        dkn_ref[...] = dk_acc[...].astype(BF16)
        dv_ref[...] = dv_acc[...].astype(BF16)
        h = pl.program_id(1)

        @pl.when(h == 0)
        def _():
            dkr_ref[...] = dkr_acc[...]

        @pl.when(h > 0)
        def _():
            dkr_ref[...] += dkr_acc[...]

    wide = jax.ShapeDtypeStruct((t, N_HEADS * V_DIM), BF16)
    rope = jax.ShapeDtypeStruct((N_HEADS, t, QK_ROPE), BF16)
    krf = pl.BlockSpec((seq, QK_ROPE), lambda b, h: (b, 0))
    return pl.pallas_call(
        body, out_shape=[wide, wide, wide, rope, rope, jax.ShapeDtypeStruct((t, QK_ROPE), F32)],
        grid=(t // seq, N_HEADS),
        in_specs=[sp["qn"], sp["qr"], sp["qn"], sp["qn"], sp["kr"], sp["qn"], sp["lse"], krf, krf],
        out_specs=[sp["qn"], sp["qn"], sp["qn"], sp["qr"], sp["qr"], krf],
        scratch_shapes=[pltpu.VMEM((seq, QK_NOPE), F32), pltpu.VMEM((seq, V_DIM), F32), pltpu.VMEM((seq, QK_ROPE), F32)],
        name="attn_bwd", compiler_params=_params(("parallel", "arbitrary")))(qn, qr, kn, v, kr, do, lse, cos2, sin2)


def _row_tile(rows, cols, row_mult=8):
    cap = max(row_mult, (1 << 18) // cols)
    best = rows
    for tr in range(row_mult, min(rows, cap) + 1, row_mult):
        if rows % tr == 0:
            best = tr
    return best if rows > cap else rows


def _adamw_math(w, g, m, v):
    mv = ADAM_B1 * m + (1.0 - ADAM_B1) * g
    vv = ADAM_B2 * v + (1.0 - ADAM_B2) * (g * g)
    m_hat = mv / (1.0 - ADAM_B1 ** ADAM_STEP)
    v_hat = vv / (1.0 - ADAM_B2 ** ADAM_STEP)
    return -ADAM_LR * (m_hat / (jnp.sqrt(v_hat) + ADAM_EPS) + ADAM_WD * w), mv, vv


def _adamw(w, g, m, v):
    shape = w.shape
    c = shape[-1]
    r = w.size // c
    tr = _row_tile(r, c)

    def body(w_ref, g_ref, m_ref, v_ref, d_ref, nm_ref, nv_ref):
        d_ref[...], nm_ref[...], nv_ref[...] = _adamw_math(w_ref[...], g_ref[...], m_ref[...], v_ref[...])

    spec = pl.BlockSpec((tr, c), lambda i: (i, 0))
    outs = pl.pallas_call(body, out_shape=[jax.ShapeDtypeStruct((r, c), F32)] * 3, grid=(r // tr,),
                          in_specs=[spec] * 4, out_specs=[spec] * 3, name="adamw",
                          compiler_params=_params(("parallel",)))(*[a.reshape(r, c) for a in (w, g, m, v)])
    return [o.reshape(shape) for o in outs]


def _adamw_halves(w, own, recv, m, v, core):
    nl, nj, rows, c = w.shape
    r = rows // 2
    tr = _row_tile(r, c)

    def body(core_ref, w_ref, own_ref, recv_ref, m_ref, v_ref, g_ref, d_ref, nm_ref, nv_ref):
        g = jnp.where(pl.program_id(2) == core_ref[0], own_ref[...], recv_ref[...])
        g_ref[...] = g
        d_ref[...], nm_ref[...], nv_ref[...] = _adamw_math(w_ref[...], g, m_ref[...], v_ref[...])

    full = pl.BlockSpec((None, None, None, tr, c), lambda a, b, h, i, cr: (a, b, h, i, 0))
    half = pl.BlockSpec((None, None, tr, c), lambda a, b, h, i, cr: (a, b, i, 0))
    grid_spec = pltpu.PrefetchScalarGridSpec(num_scalar_prefetch=1, grid=(nl, nj, 2, r // tr),
                                             in_specs=[full, half, half, full, full], out_specs=[full] * 4)
    split = lambda a: a.reshape(nl, nj, 2, r, c)
    outs = pl.pallas_call(body, out_shape=[jax.ShapeDtypeStruct((nl, nj, 2, r, c), F32)] * 4, grid_spec=grid_spec,
                          name="adamw_halves", compiler_params=_params(("parallel",) * 4))(
                              core, split(w), own, recv, split(m), split(v))
    return [o.reshape(w.shape) for o in outs]


def _place():
    x, y, c = lax.axis_index("x"), lax.axis_index("y"), lax.axis_index("c")
    return x, y, c, [(1 - x, y), (x, 1 - y), (1 - x, 1 - y)]


def _dma_sems(*counts):
    return [pltpu.SemaphoreType.DMA((n,)) for n in counts]


def _all_gather(bufs, collective_id, name):
    n = len(bufs)

    def body(*refs):
        ins, outs = refs[:n], refs[n:2 * n]
        send, recv, fsend, frecv, osend, orecv = refs[2 * n:]
        x, y, c, _ = _place()
        xn, yn, sib = (1 - x, y, c), (x, 1 - y, c), (x, y, 1 - c)
        k, kx, ky, kd = 2 * x + y, 2 * (1 - x) + y, 2 * x + 1 - y, 2 * (1 - x) + 1 - y
        _handshake([xn, yn, sib])

        def copy(src, dst, sems, i, to):
            return pltpu.make_async_remote_copy(src, dst, sems[0].at[i], sems[1].at[i], device_id=to, device_id_type=_MESH)

        ici, d2d, own_s = (send, recv), (fsend, frecv), (osend, orecv)
        started = [copy(ins[b], outs[b].at[k], own_s, b, sib) for b in range(n)]
        for first in (True, False):
            for b in range(n):
                mine = outs[b].at[k, c]
                if first:
                    started += [copy(ins[b].at[c, 0], mine.at[0], ici, 6 * b, xn), copy(ins[b].at[c, 1], mine.at[1], ici, 6 * b + 1, yn)]
                else:
                    started += [copy(ins[b].at[c, 1], mine.at[1], ici, 6 * b + 2, xn), copy(ins[b].at[c, 0], mine.at[0], ici, 6 * b + 3, yn)]
        for cp in started:
            cp.start()
        passed = []
        for b in range(n):
            for i, (src_chip, q, to) in enumerate([(kx, 0, yn), (ky, 1, xn)]):
                piece = outs[b].at[src_chip, c, q]
                copy(piece, piece, ici, 6 * b + i, to).wait_recv()
                cp = copy(piece, piece, ici, 6 * b + 4 + i, to)
                cp.start()
                passed.append(cp)
        for b in range(n):
            for i, (src_chip, q) in enumerate([(kx, 1), (ky, 0)]):
                piece = outs[b].at[src_chip, c, q]
                copy(piece, piece, ici, 6 * b + 2 + i, xn).wait_recv()
                half = outs[b].at[src_chip, c]
                cp = copy(half, half, d2d, 3 * b + i, sib)
                cp.start()
                passed.append(cp)
        for b in range(n):
            for i, q in enumerate([0, 1]):
                piece = outs[b].at[kd, c, q]
                copy(piece, piece, ici, 6 * b + 4 + i, xn).wait_recv()
            half = outs[b].at[kd, c]
            cp = copy(half, half, d2d, 3 * b + 2, sib)
            cp.start()
            passed.append(cp)
        for b in range(n):
            for i, src_chip in enumerate([kx, ky, kd]):
                half = outs[b].at[src_chip, 1 - c]
                copy(half, half, d2d, 3 * b + i, sib).wait_recv()
        for cp in started[n:] + passed:
            cp.wait_send()
        for cp in started[:n]:
            cp.wait()

    return _sequencer(body, [jax.ShapeDtypeStruct((N_SHARDS,) + b.shape, b.dtype) for b in bufs],
                      _dma_sems(6 * n, 6 * n, 3 * n, 3 * n, n, n), collective_id, name, bufs)


def _sequencer(body, out_type, sems, collective_id, name, args):
    return pl.kernel(body, out_type=out_type, mesh=plsc.ScalarSubcoreMesh(axis_name="sequencer", num_cores=1),
                     scratch_types=sems, compiler_params=pltpu.CompilerParams(collective_id=collective_id),
                     name=name)(*args)


def _handshake(peers):
    barrier = pltpu.get_barrier_semaphore()
    for peer in peers:
        pl.semaphore_signal(barrier, inc=1, device_id=peer, device_id_type=_MESH)
    pl.semaphore_wait(barrier, len(peers))


def _swap_halves(parts, collective_id, name):
    n = len(parts)

    def body(*refs):
        ins, outs = refs[:n], refs[n:2 * n]
        send, recv = refs[2 * n:]
        x, y, c, _ = _place()
        _handshake([(x, y, 1 - c)])
        cps = [pltpu.make_async_remote_copy(ins[b].at[:, pl.ds(1 - c, 1)], outs[b], send.at[b], recv.at[b],
                                            device_id=(x, y, 1 - c), device_id_type=_MESH) for b in range(n)]
        for cp in cps:
            cp.start()
        for cp in cps:
            cp.wait()

    return _sequencer(body, [jax.ShapeDtypeStruct((N_SHARDS, 1) + p.shape[2:], p.dtype) for p in parts],
                      _dma_sems(n, n), collective_id, name, parts)


def _add_half(part, other, core):
    _, _, r, c = part.shape
    tr = _row_tile(r, c, 16)

    def body(core_ref, p_ref, o_ref, out_ref):
        out_ref[...] = (p_ref[...].astype(F32) + o_ref[...].astype(F32)).astype(out_ref.dtype)

    grid_spec = pltpu.PrefetchScalarGridSpec(
        num_scalar_prefetch=1, grid=(N_SHARDS, r // tr),
        in_specs=[pl.BlockSpec((None, None, tr, c), lambda k, i, cr: (k, cr[0], i, 0)),
                  pl.BlockSpec((None, None, tr, c), lambda k, i, cr: (k, 0, i, 0))],
        out_specs=pl.BlockSpec((None, tr, c), lambda k, i, cr: (k, i, 0)))
    return pl.pallas_call(body, out_shape=jax.ShapeDtypeStruct((N_SHARDS, r, c), part.dtype), grid_spec=grid_spec,
                          name="grad_add_half", compiler_params=_params(("parallel", "parallel")))(core, part, other)


def _scatter_chips(parts, collective_id, name):
    n = len(parts)

    def body(*refs):
        ins, outs = refs[:n], refs[n:2 * n]
        send, recv = refs[2 * n:]
        x, y, c, chips = _place()
        k = 2 * x + y
        _handshake([(px, py, c) for px, py in chips])
        started = []
        for b in range(n):
            for j, (px, py) in enumerate(chips):
                cp = pltpu.make_async_remote_copy(ins[b].at[2 * px + py], outs[b].at[k], send.at[3 * b + j],
                                                  recv.at[3 * b + j], device_id=(px, py, c), device_id_type=_MESH)
                cp.start()
                started.append(cp)
        for b in range(n):
            for j, (px, py) in enumerate(chips):
                got = outs[b].at[2 * px + py]
                pltpu.make_async_remote_copy(got, got, send.at[3 * b + j], recv.at[3 * b + j],
                                             device_id=(px, py, c), device_id_type=_MESH).wait_recv()
        for cp in started:
            cp.wait_send()

    return _sequencer(body, [jax.ShapeDtypeStruct(p.shape, p.dtype) for p in parts], _dma_sems(3 * n, 3 * n),
                      collective_id, name, parts)


def _sum_slots(slots, mine, chip):
    _, r, c = slots.shape
    tr = _row_tile(r, c, 16)

    def body(chip_ref, s0, s1, s2, s3, own_ref, out_ref):
        own = own_ref[...].astype(F32)
        v = [jnp.where(chip_ref[0] == s, own, ref[...].astype(F32)) for s, ref in enumerate((s0, s1, s2, s3))]
        out_ref[...] = ((v[0] + v[1]) + v[2]) + v[3]

    def slot_spec(s):
        return pl.BlockSpec((None, tr, c), lambda i, kr: (jnp.where(kr[0] == s, (s + 1) % N_SHARDS, s), i, 0))

    grid_spec = pltpu.PrefetchScalarGridSpec(
        num_scalar_prefetch=1, grid=(r // tr,),
        in_specs=[slot_spec(s) for s in range(N_SHARDS)] + [pl.BlockSpec((None, tr, c), lambda i, kr: (kr[0], i, 0))],
        out_specs=pl.BlockSpec((tr, c), lambda i, kr: (i, 0)))
    return pl.pallas_call(body, out_shape=jax.ShapeDtypeStruct((r, c), F32), grid_spec=grid_spec, name="grad_sum_slots",
                          compiler_params=_params(("parallel",)))(chip, slots, slots, slots, slots, mine)


def _join_halves(halves, collective_id, name):
    n = len(halves)

    def body(*refs):
        ins, outs = refs[:n], refs[n:2 * n]
        send, recv = refs[2 * n:]
        x, y, c, _ = _place()
        _handshake([(x, y, 1 - c)])
        cps = [pltpu.make_async_remote_copy(ins[b], outs[b], send.at[b], recv.at[b], device_id=(x, y, 1 - c),
                                            device_id_type=_MESH) for b in range(n)]
        for cp in cps:
            cp.start()
        for cp in cps:
            cp.wait()

    return _sequencer(body, [jax.ShapeDtypeStruct(h.shape, F32) for h in halves], _dma_sems(n, n), collective_id,
                      name, halves)


def _gather_rows(buf, start, rows):
    def body(in_ref, out_ref, send, recv, lsem):
        x, y, c, chips = _place()
        k = 2 * x + y
        src = in_ref.at[pl.ds(start, rows)]
        local = pltpu.make_async_remote_copy(src, out_ref.at[k], lsem.at[0], lsem.at[1], device_id=(x, y, 1 - c),
                                             device_id_type=_MESH)
        local.start()
        cps = [pltpu.make_async_remote_copy(src, out_ref.at[k], send.at[j], recv.at[j], device_id=(px, py, c),
                                            device_id_type=_MESH) for j, (px, py) in enumerate(chips)]
        for cp in cps:
            cp.start()
        for j, (px, py) in enumerate(chips):
            got = out_ref.at[2 * px + py]
            pltpu.make_async_remote_copy(got, got, send.at[j], recv.at[j], device_id=(px, py, c),
                                         device_id_type=_MESH).wait_recv()
        for cp in cps:
            cp.wait_send()
        local.wait()

    return pl.pallas_call(body, out_shape=jax.ShapeDtypeStruct((N_SHARDS, rows, buf.shape[1]), F32),
                          in_specs=[_ANY], out_specs=_ANY, scratch_shapes=_dma_sems(3, 3, 2),
                          name="gather_replicated_grads")(buf)


def _not_before(value, other):
    return lax.optimization_barrier((value, other))[0]


def _round_up(n, m):
    return -(-n // m) * m


def _pack_flat(vecs, rows, width, dtype):
    flat = jnp.concatenate([v.reshape(-1).astype(dtype) for v in vecs])
    return jnp.pad(flat, (0, rows * width - flat.size)).reshape(rows, width)


def _split_flat(flat, shapes):
    out, off = [], 0
    for s in shapes:
        n = math.prod(s)
        out.append(flat[off:off + n].reshape(s))
        off += n
    return out


def _merge_shards(arr4, axis):
    a = jnp.moveaxis(arr4, 0, axis)
    s = list(a.shape)
    return a.reshape(s[:axis] + [s[axis] * s[axis + 1]] + s[axis + 2:])


def _split_shards(full, axis):
    s = list(full.shape)
    a = full.reshape(s[:axis] + [N_SHARDS, s[axis] // N_SHARDS] + s[axis + 1:])
    return jnp.moveaxis(a, axis, 0).reshape(N_SHARDS, -1)


def _rot_cols(w):
    half = w.shape[-1] // 2
    return jnp.concatenate([-w[..., half:], w[..., :half]], axis=-1)


def _unrot_cols(dw):
    half = dw.shape[-1] // 2
    return jnp.concatenate([dw[..., half:], -dw[..., :half]], axis=-1)


def kernel(x, positions, ffn_pre_g, ffn_post_g, ffn_w_gate, ffn_w_up, ffn_w_down, mix_pre_g, mix_post_g, gmlp_w_in, gmlp_ln_g, gmlp_ln_b, gmlp_w_s, gmlp_b_s, gmlp_w_out, kv_norm_g, w_dkv, kv_a_norm_g, w_ukv, mla_w_dq, mla_q_norm_g, mla_w_uq, mla_w_o, loss_target, m_ffn_pre_g, m_ffn_post_g, m_ffn_w_gate, m_ffn_w_up, m_ffn_w_down, m_mix_pre_g, m_mix_post_g, m_gmlp_w_in, m_gmlp_ln_g, m_gmlp_ln_b, m_gmlp_w_s, m_gmlp_b_s, m_gmlp_w_out, m_kv_norm_g, m_w_dkv, m_kv_a_norm_g, m_w_ukv, m_mla_w_dq, m_mla_q_norm_g, m_mla_w_uq, m_mla_w_o, v_ffn_pre_g, v_ffn_post_g, v_ffn_w_gate, v_ffn_w_up, v_ffn_w_down, v_mix_pre_g, v_mix_post_g, v_gmlp_w_in, v_gmlp_ln_g, v_gmlp_ln_b, v_gmlp_w_s, v_gmlp_b_s, v_gmlp_w_out, v_kv_norm_g, v_w_dkv, v_kv_a_norm_g, v_w_ukv, v_mla_w_dq, v_mla_q_norm_g, v_mla_w_uq, v_mla_w_o):
    names = ["ffn_pre_g", "ffn_post_g", "ffn_w_gate", "ffn_w_up", "ffn_w_down", "mix_pre_g", "mix_post_g", "gmlp_w_in",
             "gmlp_ln_g", "gmlp_ln_b", "gmlp_w_s", "gmlp_b_s", "gmlp_w_out", "kv_norm_g", "w_dkv", "kv_a_norm_g", "w_ukv",
             "mla_w_dq", "mla_q_norm_g", "mla_w_uq", "mla_w_o"]
    env = locals()
    w = {n: env[n] for n in names}
    mom = {n: env["m_" + n] for n in names}
    var = {n: env["v_" + n] for n in names}

    bsz, seq, d = x.shape
    t = bsz * seq
    core = lax.axis_index("c").astype(jnp.int32).reshape(1)

    mats = [("gmlp_w_in", 2), ("gmlp_w_out", 1), ("w_dkv", 0), ("w_ukv", 1), ("mla_w_dq", 1), ("mla_w_uq", 2),
            ("mla_w_o", 1)]
    vecs = [("ffn_pre_g", 2), ("ffn_post_g", 2), ("gmlp_ln_g", 1), ("gmlp_ln_b", 1)]
    replicated = ["mix_pre_g", "mix_post_g", "gmlp_w_s", "gmlp_b_s", "kv_norm_g", "kv_a_norm_g", "mla_q_norm_g"]
    n_mats = sum(w[n].size for n, _ in mats)
    n_vecs = sum(w[n].size for n, _ in vecs)
    mat_rows = _round_up(-(-n_mats // PACK_WIDTH), 64)
    vec_rows = _round_up(-(-n_vecs // 128), 32)
    mat_pack = _pack_flat([w[n] for n, _ in mats], mat_rows, PACK_WIDTH, BF16).reshape(2, 2, mat_rows // 4, PACK_WIDTH)
    vec_pack = _pack_flat([w[n] for n, _ in vecs], vec_rows, 128, F32).reshape(2, 2, vec_rows // 4, 128)
    ffn_names = ("ffn_w_gate", "ffn_w_up", "ffn_w_down")
    lj = [(l, j) for l in range(2) for j in range(2)]
    riders = {(0, 0): [vec_pack], (0, 1): [mat_pack], (1, 0): [], (1, 1): []}
    ffn_w = {}
    last = None
    for q, (l, j) in enumerate(lj):
        shards = [w[n][l, j].astype(BF16) for n in ffn_names]
        bufs = [s.reshape(2, 2, s.shape[0] // 4, s.shape[1]) for s in shards] + riders[(l, j)]
        if last is not None:
            bufs = _not_before(bufs, last)
        got = _all_gather(bufs, q + 1, f"gather_weights_{q}")
        last = got[-1]
        ffn_w[(l, j)] = [g.reshape((N_SHARDS,) + s.shape) for g, s in zip(got, shards)]
        if (l, j) == (0, 0):
            vec_all = got[3]
        if (l, j) == (0, 1):
            mat_all = got[3]

    def unpack(packed, entries):
        flat4, off, out = packed.reshape(N_SHARDS, -1), 0, {}
        for n, ax in entries:
            out[n] = _merge_shards(flat4[:, off:off + w[n].size].reshape((N_SHARDS,) + w[n].shape), ax)
            off += w[n].size
        return out

    full = unpack(vec_all, vecs)
    ln_g, ln_b = full["gmlp_ln_g"], full["gmlp_ln_b"]
    pre_g, post_g = full["ffn_pre_g"], full["ffn_post_g"]
    w_s = w["gmlp_w_s"][0]
    bsb = jnp.broadcast_to(w["gmlp_b_s"][0][:, :, None], (GROUPS, CHUNK, CHUNK))
    row = lambda v: v.reshape(1, -1)

    inv_freq = ROPE_THETA ** (-jnp.arange(0, QK_ROPE, 2, dtype=F32) / QK_ROPE)
    ang = positions.astype(F32).reshape(t, 1) * inv_freq
    cos2 = jnp.concatenate([jnp.cos(ang)] * 2, axis=-1)
    sin2 = jnp.concatenate([jnp.sin(ang)] * 2, axis=-1)
    cos_h, sin_h = jnp.tile(cos2, (1, N_HEADS)), jnp.tile(sin2, (1, N_HEADS))

    def rope_epi(n_lin):
        def epi(accs, ex):
            return accs[:n_lin] + [accs[n_lin] * ex[0] + accs[n_lin + 1] * ex[1]]
        return epi

    h0 = x.reshape(t, d)
    saved = {}

    def ffn_fwd(l, j, h, n, next_gs):
        wg, wu, wd = ffn_w[(l, j)]
        g, u, a = _ffn_up(n, wg, wu)
        f, h_new, *n_next = _ffn_down(a, wd, h, row(post_g[l, j]), next_gs)
        saved[("ffn", l, j)] = (h, n, g, u, a, f)
        return h_new, n_next

    n0 = _rms_fwd(h0, row(pre_g[0, 0]))
    h1, (n1,) = ffn_fwd(0, 0, h0, n0, row(w["mix_pre_g"][0]))

    full.update(unpack(_not_before(mat_all, h1), mats))
    w_in, w_out = full["gmlp_w_in"][0], full["gmlp_w_out"][0]
    w_c, w_kr = full["w_dkv"][:, :KV_RANK], full["w_dkv"][:, KV_RANK:]
    w_kr_rot = _rot_cols(w_kr)
    ukv = full["w_ukv"].reshape(KV_RANK, N_HEADS, 2, QK_NOPE)
    w_k, w_v = ukv[:, :, 0].reshape(KV_RANK, -1), ukv[:, :, 1].reshape(KV_RANK, -1)
    w_dq, w_o = full["mla_w_dq"][0], full["mla_w_o"][0]
    q_rank = w_dq.shape[1]
    uq = full["mla_w_uq"][0].reshape(q_rank, N_HEADS, QK_NOPE + QK_ROPE)
    w_qn = uq[:, :, :QK_NOPE].reshape(q_rank, -1)
    w_qr = uq[:, :, QK_NOPE:].reshape(q_rank, -1)
    w_qr_rot = _rot_cols(uq[:, :, QK_NOPE:]).reshape(q_rank, -1)

    zp = _mm2d("gmlp_in", [(n1, w_in, "nn", 0)], [(w_in.shape[1], F32)])[0]
    uv = _sgu_fwd(zp, ln_g, ln_b, w_s, bsb)
    half = uv.shape[1]
    tm = min(t, ROW_TILE)
    m0, h2, n2 = _down("gmlp_out", (uv, (tm, 512), lambda i, _, k: (i, k)), (w_out, (512, d), lambda i, _, k: (k, 0)),
                       half // 512, h1, row(w["mix_post_g"][0]), row(pre_g[0, 1]), 1.0)
    h3, (n3kv, n3) = ffn_fwd(0, 1, h2, n2, jnp.stack([w["kv_norm_g"], pre_g[1, 0]]))

    def kv_epi(accs, ex):
        c_raw = accs[0]
        return [c_raw, _rms(c_raw, ex[2]), accs[1] * ex[0] + accs[2] * ex[1]]

    c_raw, c_n, k_r = _mm2d("kv_down", [(n3kv, w_c, "nn", 0), (n3kv, w_kr, "nn", 1), (n3kv, w_kr_rot, "nn", 2)],
                            [(KV_RANK, F32), (KV_RANK, BF16), (QK_ROPE, BF16)], kv_epi, [cos2, sin2],
                            [row(w["kv_a_norm_g"])])
    k_n, v_h = _mm2d("kv_up", [(c_n, w_k, "nn", 0), (c_n, w_v, "nn", 1)], [(w_k.shape[1], BF16), (w_v.shape[1], BF16)])

    h4, (n4,) = ffn_fwd(1, 0, h3, n3, row(w["mix_pre_g"][1]))
    qd, qn = _mm2d("q_down", [(n4, w_dq, "nn", 0)], [(q_rank, F32), (q_rank, BF16)],
                   lambda accs, ex: [accs[0], _rms(accs[0], ex[0])], [], [row(w["mla_q_norm_g"][0])])
    q_n, q_r = _mm2d("q_up", [(qn, w_qn, "nn", 0), (qn, w_qr, "nn", 1), (qn, w_qr_rot, "nn", 2)],
                     [(w_qn.shape[1], BF16), (w_qr.shape[1], BF16)], rope_epi(1), [cos_h, sin_h])
    q_r = q_r.reshape(t, N_HEADS, QK_ROPE).transpose(1, 0, 2)
    o, lse = _attn_fwd(q_n, q_r, k_n, v_h, k_r, seq)
    m1, h5, n5 = _down("attn_out", (o, (tm, 512), lambda i, _, k: (i, k)), (w_o, (512, d), lambda i, _, k: (k, 0)),
                       o.shape[1] // 512, h4, row(w["mix_post_g"][1]), row(pre_g[1, 1]), 1.0)
    y, _ = ffn_fwd(1, 1, h5, n5, row(pre_g[1, 1]))

    loss_part, dy = _loss_head(y, loss_target.reshape(t, d))
    loss = lax.psum(loss_part, ("x", "y", "c"))

    grads = {}
    ffn_dw = {}
    d_pre, d_post = {}, {}

    def ffn_bwd(l, j, dh_out, extra=()):
        h, n, g, u, a, f = saved[("ffn", l, j)]
        df, d_post[(l, j)] = _norm_out_bwd("ffn_post_bwd", f, dh_out, row(post_g[l, j]), 0.5)
        wg, wu, wd = ffn_w[(l, j)]
        dg, du = _ffn_dact(df, wd, g, u)
        dwd = _ffn_dw_down(a, df)
        dn = _ffn_dn(dg, du, wg, wu)
        dwg, dwu = _ffn_dw_in(n, dg, du)
        ffn_dw[(l, j)] = (dwg, dwu, dwd)
        dh, d_pre[(l, j)], *rest = _norm_in_bwd("ffn_pre_bwd", h, dh_out, [(row(pre_g[l, j]), dn)] + list(extra))
        return dh, rest

    chip = (2 * lax.axis_index("x") + lax.axis_index("y")).astype(jnp.int32).reshape(1)
    rs = {}

    def ffn_parts(l, j):
        return [g.reshape(N_SHARDS, 2, g.shape[1] // 2, g.shape[2]) for g in ffn_dw[(l, j)]]

    def rs_swap(gid, parts):
        rs[gid] = {"parts": parts, "others": _swap_halves(parts, 5 + gid, f"grad_swap_{gid}")}

    def rs_scatter(gid, after):
        r = rs[gid]
        parts, others = _not_before((r["parts"], r["others"]), after)
        r["chip"] = [_add_half(p, o, core) for p, o in zip(parts, others)]
        r["slots"] = _scatter_chips(r["chip"], 9 + gid, f"grad_scatter_{gid}")

    def rs_join(gid, after):
        r = rs[gid]
        slots, mine = _not_before((r["slots"], r["chip"]), after)
        r["own"] = [_sum_slots(s, p, chip) for s, p in zip(slots, mine)]
        r["recv"] = _join_halves(r["own"], 13 + gid, f"grad_join_{gid}")

    dh5, _ = ffn_bwd(1, 1, dy)
    rs_swap(0, ffn_parts(1, 1))
    rs_scatter(0, dh5)

    dm1, g_mix_post1 = _norm_out_bwd("mix_post_bwd", m1, dh5, row(w["mix_post_g"][1]), 1.0)
    do = _mm2d("attn_out_dx", [(dm1, w_o, "nt", 0)], [(w_o.shape[0], BF16)])[0]
    g_w_o = _mm2d("attn_out_dw", [(o, dm1, "tn", 0)], [(d, F32)])[0]
    dq_n, dk_n, dv_h, dq_c, dq_s, dk_r = _attn_bwd(q_n, q_r, k_n, v_h, k_r, do, lse, cos2, sin2, seq)
    dq_c = dq_c.transpose(1, 0, 2).reshape(t, -1)
    dq_s = dq_s.transpose(1, 0, 2).reshape(t, -1)
    dqn = _mm2d("q_up_dx", [(dq_n, w_qn, "nt", 0), (dq_c, w_qr, "nt", 0), (dq_s, w_qr_rot, "nt", 0)], [(q_rank, F32)])[0]
    g_qn, g_qr, g_qr_rot = _mm2d("q_up_dw", [(qn, dq_n, "tn", 0), (qn, dq_c, "tn", 1), (qn, dq_s, "tn", 2)],
                                 [(w_qn.shape[1], F32), (w_qr.shape[1], F32), (w_qr.shape[1], F32)])
    dqd, g_q_norm = _norm_out_bwd("q_norm_bwd", qd, dqn, row(w["mla_q_norm_g"][0]), 1.0)
    dn4 = _mm2d("q_down_dx", [(dqd, w_dq, "nt", 0)], [(d, F32)])[0]
    g_w_dq = _mm2d("q_down_dw", [(n4, dqd, "tn", 0)], [(q_rank, F32)])[0]
    dh4, g_mix_pre1 = _norm_in_bwd("mix_pre_bwd", h4, dh5, [(row(w["mix_pre_g"][1]), dn4)])

    dc_n = _mm2d("kv_up_dx", [(dk_n, w_k, "nt", 0), (dv_h, w_v, "nt", 0)], [(KV_RANK, F32)])[0]
    g_wk, g_wv = _mm2d("kv_up_dw", [(c_n, dk_n, "tn", 0), (c_n, dv_h, "tn", 1)], [(w_k.shape[1], F32), (w_v.shape[1], F32)])
    dc, g_kv_a = _norm_out_bwd("kv_a_norm_bwd", c_raw, dc_n, row(w["kv_a_norm_g"]), 1.0)
    dkr_c, dkr_s = _rope_bwd(dk_r, cos2, sin2)
    dn3kv = _mm2d("kv_down_dx", [(dc, w_c, "nt", 0), (dkr_c, w_kr, "nt", 0), (dkr_s, w_kr_rot, "nt", 0)], [(d, F32)])[0]
    g_wc, g_wkr, g_wkr_rot = _mm2d("kv_down_dw", [(n3kv, dc, "tn", 0), (n3kv, dkr_c, "tn", 1), (n3kv, dkr_s, "tn", 2)],
                                   [(KV_RANK, F32), (QK_ROPE, F32), (QK_ROPE, F32)])

    rs_join(0, dh4)
    dh3, (g_kv_norm,) = ffn_bwd(1, 0, dh4, extra=[(row(w["kv_norm_g"]), dn3kv)])
    rs_swap(1, ffn_parts(1, 0))
    rs_scatter(1, dh3)
    dh2, _ = ffn_bwd(0, 1, dh3)
    rs_join(1, dh2)
    rs_swap(2, ffn_parts(0, 1))
    rs_scatter(2, dh2)

    dm0, g_mix_post0 = _norm_out_bwd("mix_post_bwd", m0, dh2, row(w["mix_post_g"][0]), 1.0)
    d_uv = _mm2d("gmlp_out_dx", [(dm0, w_out, "nt", 0)], [(half, F32)])[0]
    g_w_out = _mm2d("gmlp_out_dw", [(uv, dm0, "tn", 0)], [(d, F32)])[0]
    dzp, g_ln_g, g_ln_b, g_w_s, g_b_s = _sgu_bwd(zp, d_uv, ln_g, ln_b, w_s, bsb)
    dn1 = _mm2d("gmlp_in_dx", [(dzp, w_in, "nt", 0)], [(d, F32)])[0]
    g_w_in = _mm2d("gmlp_in_dw", [(n1, dzp, "tn", 0)], [(w_in.shape[1], F32)])[0]
    dh1, g_mix_pre0 = _norm_in_bwd("mix_pre_bwd", h1, dh2, [(row(w["mix_pre_g"][0]), dn1)])
    rs_join(2, dh1)
    dx, _ = ffn_bwd(0, 0, dh1)

    lj = [(l, j) for l in range(2) for j in range(2)]
    part = {
        "gmlp_w_in": g_w_in[None], "gmlp_w_out": g_w_out[None],
        "w_dkv": jnp.concatenate([g_wc, g_wkr + _unrot_cols(g_wkr_rot)], axis=1),
        "w_ukv": jnp.stack([g_wk.reshape(KV_RANK, N_HEADS, QK_NOPE), g_wv.reshape(KV_RANK, N_HEADS, V_DIM)],
                           axis=2).reshape(KV_RANK, -1),
        "mla_w_dq": g_w_dq[None],
        "mla_w_uq": jnp.concatenate(
            [g_qn.reshape(q_rank, N_HEADS, QK_NOPE),
             g_qr.reshape(q_rank, N_HEADS, QK_ROPE) + _unrot_cols(g_qr_rot.reshape(q_rank, N_HEADS, QK_ROPE))],
            axis=-1).reshape(1, q_rank, -1),
        "mla_w_o": g_w_o[None],
        "ffn_pre_g": jnp.concatenate([d_pre[k] for k in lj]).reshape(2, 2, d),
        "ffn_post_g": jnp.concatenate([d_post[k] for k in lj]).reshape(2, 2, d),
        "gmlp_ln_g": g_ln_g, "gmlp_ln_b": g_ln_b,
        "mix_pre_g": jnp.concatenate([g_mix_pre0, g_mix_pre1]), "mix_post_g": jnp.concatenate([g_mix_post0, g_mix_post1]),
        "gmlp_w_s": g_w_s[None], "gmlp_b_s": g_b_s.reshape(1, GROUPS, CHUNK),
        "kv_norm_g": g_kv_norm.reshape(-1), "kv_a_norm_g": g_kv_a.reshape(-1), "mla_q_norm_g": g_q_norm,
    }

    sharded = mats + vecs
    n_sh = n_mats + n_vecs
    n_rep = sum(w[n].size for n in replicated)
    sh_rows = _round_up(-(-n_sh // PACK_WIDTH), 8)
    rep_rows = _round_up(-(-(n_rep // N_SHARDS) // PACK_WIDTH), 8)
    rows = _round_up(sh_rows + rep_rows, 32)
    sh_flat = jnp.concatenate([_split_shards(part[n], ax) for n, ax in sharded], axis=1)
    rep_flat = jnp.concatenate([part[n].reshape(-1) for n in replicated]).reshape(N_SHARDS, -1)
    small = jnp.concatenate([
        jnp.pad(sh_flat, ((0, 0), (0, sh_rows * PACK_WIDTH - n_sh))),
        jnp.pad(rep_flat, ((0, 0), (0, (rows - sh_rows) * PACK_WIDTH - n_rep // N_SHARDS)))], axis=1)
    small = small.astype(BF16).reshape(N_SHARDS, 2, rows // 2, PACK_WIDTH)

    rs_swap(3, ffn_parts(0, 0) + [small])
    rs_scatter(3, dx)
    rs_join(3, dx)
    own_small, recv_small = rs[3]["own"][3], rs[3]["recv"][3]
    g_small = jnp.where(core[0] == 0, jnp.concatenate([own_small, recv_small]), jnp.concatenate([recv_small, own_small]))
    g_rep = _gather_rows(g_small, sh_rows, rep_rows)

    grads, delta, new_m, new_v = {}, {}, {}, {}
    group_of = {(1, 1): 0, (1, 0): 1, (0, 1): 2, (0, 0): 3}
    for q, n in enumerate(ffn_names):
        own, recv = [jnp.stack([rs[group_of[k]][side][q] for k in lj]) for side in ("own", "recv")]
        shape = (2, 2) + own.shape[1:]
        grads[n], delta[n], new_m[n], new_v[n] = _adamw_halves(w[n], own.reshape(shape), recv.reshape(shape), mom[n],
                                                               var[n], core)
    for (n, _), g in zip(sharded, _split_flat(g_small.reshape(-1), [w[n].shape for n, _ in sharded])):
        grads[n] = g
    rep_vec = g_rep.reshape(N_SHARDS, -1)[:, :n_rep // N_SHARDS].reshape(-1)
    for n, g in zip(replicated, _split_flat(rep_vec, [w[n].shape for n in replicated])):
        grads[n] = g

    for n in names:
        if n not in delta:
            delta[n], new_m[n], new_v[n] = _adamw(w[n], grads[n], mom[n], var[n])
    return (loss, dx.reshape(x.shape), *[grads[n] for n in names], *[delta[n] for n in names],
            *[new_m[n] for n in names], *[new_v[n] for n in names])
```

```python
import math

import jax
import jax.numpy as jnp
from jax import lax
from jax.experimental import pallas as pl
from jax.experimental.pallas import tpu as pltpu
from jax.experimental.pallas import tpu_sc as plsc

F32, BF16 = jnp.float32, jnp.bfloat16

RMS_EPS, LN_EPS, NEG_INF = 1e-6, 1e-5, -1e30
N_HEADS, QK_NOPE, QK_ROPE, V_DIM, KV_RANK = 8, 128, 64, 128, 256
CHUNK, GROUPS = 128, 16
ROPE_THETA = 10000.0
ADAM_LR, ADAM_B1, ADAM_B2, ADAM_EPS, ADAM_WD, ADAM_STEP = 0.001, 0.9, 0.999, 1e-08, 0.01, 10
N_SHARDS = 4

VMEM_LIMIT_BYTES = 48 * 1024 * 1024
ROW_TILE = 512
PACK_WIDTH = 1024

_DN = {"nn": (((1,), (0,)), ((), ())), "nt": (((1,), (1,)), ((), ())), "tn": (((0,), (0,)), ((), ()))}
_MESH = pl.DeviceIdType.MESH
_ANY = pl.BlockSpec(memory_space=pl.ANY)


def _params(sem):
    return pltpu.CompilerParams(dimension_semantics=sem, vmem_limit_bytes=VMEM_LIMIT_BYTES)


def _mm(name, grid, ins, pairs, acc_shapes, outs, epilogue, extras=()):
    n_in, n_ex, n_out = len(ins), len(extras), len(outs)
    gk = grid[2]

    def body(*refs):
        in_refs, ex_refs = refs[:n_in], refs[n_in:n_in + n_ex]
        out_refs = refs[n_in + n_ex:n_in + n_ex + n_out]
        acc_refs = refs[n_in + n_ex + n_out:]
        parts = [None] * len(acc_shapes)
        for a, b, c, dims in pairs:
            p = lax.dot_general(in_refs[a][...], in_refs[b][...], _DN[dims], preferred_element_type=F32)
            parts[c] = p if parts[c] is None else parts[c] + p

        def finish(accs):
            vals = epilogue(accs, [r[...] for r in ex_refs])
            for r, v in zip(out_refs, vals):
                r[...] = v.astype(r.dtype)

        if gk == 1:
            finish(parts)
        else:
            k = pl.program_id(2)

            @pl.when(k == 0)
            def _():
                for r, p in zip(acc_refs, parts):
                    r[...] = p

            @pl.when(k > 0)
            def _():
                for r, p in zip(acc_refs, parts):
                    r[...] += p

            @pl.when(k == gk - 1)
            def _():
                finish([r[...] for r in acc_refs])

    return pl.pallas_call(
        body,
        out_shape=[jax.ShapeDtypeStruct(s, d) for s, d, _, _ in outs],
        grid=grid,
        in_specs=[pl.BlockSpec(bs, im) for _, bs, im in list(ins) + list(extras)],
        out_specs=[pl.BlockSpec(bs, im) for _, _, bs, im in outs],
        scratch_shapes=[pltpu.VMEM(s, F32) for s in acc_shapes] if gk > 1 else [],
        name=name,
        compiler_params=_params(("parallel", "parallel", "arbitrary")),
    )(*[a for a, _, _ in ins], *[a for a, _, _ in extras])


def _mm2d(name, pairs, outs, epilogue=None, row_extras=(), vec_extras=()):
    def mk(a, dims):
        return (a.shape[0], a.shape[1]) if dims[0] == "n" else (a.shape[1], a.shape[0])

    def nk(b, dims):
        return (b.shape[1], b.shape[0]) if dims[1] == "n" else (b.shape[0], b.shape[1])

    m = mk(pairs[0][0], pairs[0][2])[0]
    ks = [mk(a, d)[1] for a, _, d, _ in pairs]
    n_acc = 1 + max(p[3] for p in pairs)
    acc_n = [None] * n_acc
    for a, b, d, c in pairs:
        assert mk(a, d)[0] == m and nk(b, d)[1] == mk(a, d)[1]
        acc_n[c] = nk(b, d)[0]
    tm = min(m, ROW_TILE)
    if len(set(ks)) == 1 and ks[0] > 1024:
        tks, gk = [512] * len(pairs), ks[0] // 512
    else:
        tks, gk = ks, 1
    if len(set(acc_n)) == 1 and acc_n[0] > 1024:
        tns, gj = [1024] * n_acc, acc_n[0] // 1024
    else:
        tns, gj = acc_n, 1

    ins, plist = [], []
    for (a, b, d, c), tk in zip(pairs, tks):
        tn = tns[c]
        a_spec = ((tm, tk), lambda i, j, k: (i, k)) if d[0] == "n" else ((tk, tm), lambda i, j, k: (k, i))
        b_spec = ((tk, tn), lambda i, j, k: (k, j)) if d[1] == "n" else ((tn, tk), lambda i, j, k: (j, k))
        ins += [(a, *a_spec), (b, *b_spec)]
        plist.append((len(ins) - 2, len(ins) - 1, c, d))
    extras = [(r, (tm, r.shape[1]), lambda i, j, k: (i, 0)) for r in row_extras]
    extras += [(v, v.shape, lambda i, j, k: (0, 0)) for v in vec_extras]
    out_specs = []
    for n, dt in outs:
        bn = 1024 if (gj > 1) else n
        out_specs.append(((m, n), dt, (tm, bn), lambda i, j, k: (i, j)))
    if epilogue is None:
        epilogue = lambda accs, ex: accs
    return _mm(name, (m // tm, gj, gk), ins, plist, [(tm, tn) for tn in tns], out_specs, epilogue, extras)


def _rms(x, g):
    return x * lax.rsqrt(jnp.mean(x * x, axis=-1, keepdims=True) + RMS_EPS) * g


def _rms_bwd(x, g, dy):
    r = lax.rsqrt(jnp.mean(x * x, axis=-1, keepdims=True) + RMS_EPS)
    gy = dy * g
    dx = r * gy - x * (r * r * r) * jnp.mean(gy * x, axis=-1, keepdims=True)
    return dx, jnp.sum(dy * x * r, axis=0, keepdims=True)


def _sigmoid(x):
    return 1.0 / (1.0 + jnp.exp(-x))


_GELU_C = math.sqrt(2.0 / math.pi)


def _gelu(x):
    return x * (0.5 * (1.0 + jnp.tanh(_GELU_C * (x + 0.044715 * (x * x * x)))))


def _gelu_grad(x):
    t = jnp.tanh(_GELU_C * (x + 0.044715 * (x * x * x)))
    return 0.5 * (1.0 + t) + 0.5 * x * (1.0 - t * t) * (_GELU_C * (1.0 + 3.0 * 0.044715 * (x * x)))


def _rows(name, row_ins, vec_ins, fn, row_outs, acc_outs=()):
    t = row_ins[0].shape[0]
    tm = min(t, ROW_TILE)
    nr, nv, no = len(row_ins), len(vec_ins), len(row_outs)

    def body(*refs):
        outs, incs = fn([r[...] for r in refs[:nr]], [r[...] for r in refs[nr:nr + nv]])
        for r, v in zip(refs[nr + nv:nr + nv + no], outs):
            r[...] = v.astype(r.dtype)
        i = pl.program_id(0)
        for r, v in zip(refs[nr + nv + no:], incs):
            @pl.when(i == 0)
            def _():
                r[...] = v

            @pl.when(i > 0)
            def _():
                r[...] += v

    in_specs = [pl.BlockSpec((tm, a.shape[1]), lambda i: (i, 0)) for a in row_ins]
    in_specs += [pl.BlockSpec(v.shape, lambda i, nd=v.ndim: (0,) * nd) for v in vec_ins]
    out_shape = [jax.ShapeDtypeStruct((t, c), dt) for c, dt in row_outs]
    out_shape += [jax.ShapeDtypeStruct(s, F32) for s in acc_outs]
    out_specs = [pl.BlockSpec((tm, c), lambda i: (i, 0)) for c, _ in row_outs]
    out_specs += [pl.BlockSpec(s, lambda i, nd=len(s): (0,) * nd) for s in acc_outs]
    return pl.pallas_call(body, out_shape=out_shape, grid=(t // tm,), in_specs=in_specs, out_specs=out_specs,
                          name=name, compiler_params=_params(("arbitrary",)))(*row_ins, *vec_ins)


def _rms_fwd(x, g):
    return _rows("rms_fwd", [x], [g], lambda r, v: ([_rms(r[0], v[0])], []), [(x.shape[1], BF16)])[0]


def _norm_out_bwd(name, f, d_out, g, scale):
    def fn(r, v):
        dx, dg = _rms_bwd(r[0], v[0], r[1] * scale)
        return [dx], [dg]

    c = f.shape[1]
    return _rows(name, [f, d_out], [g], fn, [(c, BF16)], [(1, c)])


def _norm_in_bwd(name, h, d_res, branches):
    nb = len(branches)

    def fn(r, v):
        dh, dgs = r[1], []
        for b in range(nb):
            dx, dg = _rms_bwd(r[0], v[b], r[2 + b])
            dh = dh + dx
            dgs.append(dg)
        return [dh], dgs

    c = h.shape[1]
    return _rows(name, [h, d_res] + [dn for _, dn in branches], [g for g, _ in branches], fn, [(c, F32)],
                 [(1, c)] * nb)


def _loss_head(y, target):
    d = y.shape[1]

    def fn(r, v):
        e = r[0] - r[1]
        s = jnp.sum(jnp.sum(e * e, axis=1, keepdims=True), axis=0, keepdims=True) * (0.5 / d)
        return [e * (1.0 / d)], [jnp.broadcast_to(s, (1, 128))]

    dy, acc = _rows("loss_head", [y, target], [], fn, [(d, F32)], [(1, 128)])
    return acc[0, 0], dy


def _rope_bwd(dk, cos2, sin2):
    c = dk.shape[1]
    return _rows("rope_bwd", [dk, cos2, sin2], [], lambda r, v: ([r[0] * r[1], r[0] * r[2]], []),
                 [(c, BF16), (c, BF16)])


def _ffn_up(n, wg, wu):
    t, d = n.shape
    fs = wg.shape[-1]
    tm = min(t, ROW_TILE)
    w_spec = ((None, d, fs), lambda s, i, k: (s, 0, 0))

    def epi(accs, ex):
        g, u = accs
        return [g, u, g * _sigmoid(g) * u]

    o_spec = ((None, tm, fs), lambda s, i, k: (s, i, 0))
    outs = [((N_SHARDS, t, fs), F32, *o_spec), ((N_SHARDS, t, fs), F32, *o_spec), ((N_SHARDS, t, fs), BF16, *o_spec)]
    return _mm("ffn_up", (N_SHARDS, t // tm, 1),
               [(n, (tm, d), lambda s, i, k: (i, 0)), (wg, *w_spec), (wu, *w_spec)],
               [(0, 1, 0, "nn"), (0, 2, 1, "nn")], [(tm, fs)] * 2, outs, epi)


def _down(name, a_in, w_in, gk, h, post_g, next_gs, scale):
    t, d = h.shape
    tm = min(t, ROW_TILE)
    kn = next_gs.shape[0]

    def epi(accs, ex):
        f, hv, pg, ng = accs[0], ex[0], ex[1], ex[2]
        hn = hv + scale * _rms(f, pg)
        return [f, hn] + [_rms(hn, ng[q:q + 1]) for q in range(kn)]

    row = ((tm, d), lambda i, j, k: (i, 0))
    outs = [((t, d), F32, *row), ((t, d), F32, *row)] + [((t, d), BF16, *row)] * kn
    extras = [(h, *row), (post_g, (1, d), lambda i, j, k: (0, 0)), (next_gs, (kn, d), lambda i, j, k: (0, 0))]
    return _mm(name, (t // tm, 1, gk), [a_in, w_in], [(0, 1, 0, "nn")], [(tm, d)], outs, epi, extras)


def _ffn_down(a, wd, h, post_g, next_gs):
    t, d = h.shape
    fs = a.shape[-1]
    tm = min(t, ROW_TILE)
    return _down("ffn_down", (a, (None, tm, fs), lambda i, _, s: (s, i, 0)),
                 (wd, (None, fs, d), lambda i, _, s: (s, 0, 0)), N_SHARDS, h, post_g, next_gs, 0.5)


def _ffn_dact(df, wd, g, u):
    t, d = df.shape
    fs = g.shape[-1]
    tm = min(t, ROW_TILE)

    def epi(accs, ex):
        da, gv, uv = accs[0], ex[0], ex[1]
        sg = _sigmoid(gv)
        return [da * uv * (sg * (1.0 + gv * (1.0 - sg))), da * (gv * sg)]

    o_spec = ((None, tm, fs), lambda s, i, k: (s, i, 0))
    outs = [((N_SHARDS, t, fs), BF16, *o_spec)] * 2
    return _mm("ffn_dact", (N_SHARDS, t // tm, 1),
               [(df, (tm, d), lambda s, i, k: (i, 0)), (wd, (None, fs, d), lambda s, i, k: (s, 0, 0))],
               [(0, 1, 0, "nt")], [(tm, fs)], outs, epi, [(g, *o_spec), (u, *o_spec)])


def _ffn_dn(dg, du, wg, wu):
    _, t, fs = dg.shape
    d = wg.shape[-2]
    tm = min(t, ROW_TILE)
    a_spec = ((None, tm, fs), lambda i, _, s: (s, i, 0))
    w_spec = ((None, d, fs), lambda i, _, s: (s, 0, 0))
    outs = [((t, d), F32, (tm, d), lambda i, _, s: (i, 0))]
    return _mm("ffn_dn", (t // tm, 1, N_SHARDS), [(dg, *a_spec), (wg, *w_spec), (du, *a_spec), (wu, *w_spec)],
               [(0, 1, 0, "nt"), (2, 3, 0, "nt")], [(tm, d)], outs, lambda accs, ex: accs)[0]


def _ffn_dw_in(n, dg, du):
    _, t, fs = dg.shape
    d = n.shape[1]
    tk = min(t, ROW_TILE)
    b_spec = ((None, tk, fs), lambda s, _, k: (s, k, 0))
    o_spec = ((None, d, fs), lambda s, _, k: (s, 0, 0))
    outs = [((N_SHARDS, d, fs), BF16, *o_spec)] * 2
    return _mm("ffn_dw_in", (N_SHARDS, 1, t // tk), [(n, (tk, d), lambda s, _, k: (k, 0)), (dg, *b_spec), (du, *b_spec)],
               [(0, 1, 0, "tn"), (0, 2, 1, "tn")], [(d, fs)] * 2, outs, lambda accs, ex: accs)


def _ffn_dw_down(a, df):
    _, t, fs = a.shape
    d = df.shape[1]
    tk = min(t, ROW_TILE)
    outs = [((N_SHARDS, fs, d), BF16, (None, fs, d), lambda s, _, k: (s, 0, 0))]
    return _mm("ffn_dw_down", (N_SHARDS, 1, t // tk),
               [(a, (None, tk, fs), lambda s, _, k: (s, k, 0)), (df, (tk, d), lambda s, _, k: (k, 0))],
               [(0, 1, 0, "tn")], [(fs, d)], outs, lambda accs, ex: accs)[0]


def _causal_weight(w):
    row = lax.broadcasted_iota(jnp.int32, (CHUNK, CHUNK), 0)
    col = lax.broadcasted_iota(jnp.int32, (CHUNK, CHUNK), 1)
    return row >= col, jnp.where(row >= col, w, 0.0).astype(BF16)


def _layer_norm(v, g, b):
    xc = v - jnp.mean(v, axis=-1, keepdims=True)
    rstd = lax.rsqrt(jnp.mean(xc * xc, axis=-1, keepdims=True) + LN_EPS)
    xhat = xc * rstd
    return xhat, rstd, xhat * g + b


def _sgu_specs(t, half, tm):
    return [pl.BlockSpec((tm, half), lambda i: (i, 0)), pl.BlockSpec((tm, half), lambda i: (i, 1))]


def _sgu_fwd(zp, ln_g, ln_b, w_s, bsb):
    t, half = zp.shape[0], zp.shape[1] // 2
    tm = min(t, 2 * CHUNK)

    def body(u_ref, v_ref, g_ref, b_ref, w_ref, bs_ref, o_ref):
        u = _gelu(u_ref[...])
        _, _, vn = _layer_norm(_gelu(v_ref[...]), g_ref[...], b_ref[...])
        vb = vn.astype(BF16)
        for g in range(GROUPS):
            _, wm = _causal_weight(w_ref[g])
            cols = slice(g * CHUNK, (g + 1) * CHUNK)
            for c in range(tm // CHUNK):
                rows = slice(c * CHUNK, (c + 1) * CHUNK)
                sv = jnp.dot(wm, vb[rows, cols], preferred_element_type=F32) + bs_ref[g]
                o_ref[rows, cols] = (u[rows, cols] * sv).astype(BF16)

    whole = lambda a: pl.BlockSpec(a.shape, lambda i, nd=a.ndim: (0,) * nd)
    return pl.pallas_call(
        body, out_shape=jax.ShapeDtypeStruct((t, half), BF16), grid=(t // tm,),
        in_specs=_sgu_specs(t, half, tm) + [whole(ln_g), whole(ln_b), whole(w_s), whole(bsb)],
        out_specs=pl.BlockSpec((tm, half), lambda i: (i, 0)), name="sgu_fwd",
        compiler_params=_params(("arbitrary",)))(zp, zp, ln_g, ln_b, w_s, bsb)


def _sgu_bwd(zp, d_uv, ln_g, ln_b, w_s, bsb):
    t, half = zp.shape[0], zp.shape[1] // 2
    tm = min(t, 2 * CHUNK)

    def body(u_ref, v_ref, d_ref, g_ref, b_ref, w_ref, bs_ref, dz_ref, dlg_ref, dlb_ref, dws_ref, dbs_ref, dvn_ref):
        i = pl.program_id(0)

        @pl.when(i == 0)
        def _():
            dlg_ref[...] = jnp.zeros_like(dlg_ref)
            dlb_ref[...] = jnp.zeros_like(dlb_ref)
            dws_ref[...] = jnp.zeros_like(dws_ref)
            dbs_ref[...] = jnp.zeros_like(dbs_ref)

        up, vp = u_ref[...], v_ref[...]
        u, gup = _gelu(up), _gelu_grad(up)
        xhat, rstd, vn = _layer_norm(_gelu(vp), g_ref[...], b_ref[...])
        vb = vn.astype(BF16)
        d = d_ref[...]
        for g in range(GROUPS):
            mask, wm = _causal_weight(w_ref[g])
            cols = slice(g * CHUNK, (g + 1) * CHUNK)
            for c in range(tm // CHUNK):
                rows = slice(c * CHUNK, (c + 1) * CHUNK)
                blk = vb[rows, cols]
                sv = jnp.dot(wm, blk, preferred_element_type=F32) + bs_ref[g]
                dblk = d[rows, cols]
                dz_ref[rows, cols] = (dblk * sv * gup[rows, cols]).astype(BF16)
                dsv = dblk * u[rows, cols]
                dsvb = dsv.astype(BF16)
                dvn_ref[rows, cols] = lax.dot_general(wm, dsvb, _DN["tn"], preferred_element_type=F32)
                dw = lax.dot_general(dsvb, blk, _DN["nt"], preferred_element_type=F32)
                dws_ref[g] += jnp.where(mask, dw, 0.0)
                dbs_ref[g] += jnp.sum(dsv, axis=1, keepdims=True)
        dvn = dvn_ref[...]
        dlg_ref[...] += jnp.sum(dvn * xhat, axis=0, keepdims=True)
        dlb_ref[...] += jnp.sum(dvn, axis=0, keepdims=True)
        dxh = dvn * g_ref[...]
        dv = rstd * (dxh - jnp.mean(dxh, axis=-1, keepdims=True)
                     - xhat * jnp.mean(dxh * xhat, axis=-1, keepdims=True))
        dz_ref[:, half:] = (dv * _gelu_grad(vp)).astype(BF16)

    whole = lambda a: pl.BlockSpec(a.shape, lambda i, nd=a.ndim: (0,) * nd)
    wshape = lambda s: pl.BlockSpec(s, lambda i, nd=len(s): (0,) * nd)
    out_shape = [jax.ShapeDtypeStruct((t, 2 * half), BF16), jax.ShapeDtypeStruct((1, half), F32),
                 jax.ShapeDtypeStruct((1, half), F32), jax.ShapeDtypeStruct(w_s.shape, F32),
                 jax.ShapeDtypeStruct((GROUPS, CHUNK, 1), F32)]
    return pl.pallas_call(
        body, out_shape=out_shape, grid=(t // tm,),
        in_specs=_sgu_specs(t, half, tm) + [pl.BlockSpec((tm, half), lambda i: (i, 0)), whole(ln_g), whole(ln_b),
                                            whole(w_s), whole(bsb)],
        out_specs=[pl.BlockSpec((tm, 2 * half), lambda i: (i, 0)), wshape((1, half)), wshape((1, half)),
                   wshape(w_s.shape), wshape((GROUPS, CHUNK, 1))],
        scratch_shapes=[pltpu.VMEM((tm, half), F32)], name="sgu_bwd",
        compiler_params=_params(("arbitrary",)))(zp, zp, d_uv, ln_g, ln_b, w_s, bsb)


_SCALE = (QK_NOPE + QK_ROPE) ** -0.5


def _attn_scores(qn, qr, kn, kr, i, tq, n):
    s = lax.dot_general(qn, kn, _DN["nt"], preferred_element_type=F32)
    s = (s + lax.dot_general(qr, kr, _DN["nt"], preferred_element_type=F32)) * _SCALE
    row = i * tq + lax.broadcasted_iota(jnp.int32, (tq, n), 0)
    col = lax.broadcasted_iota(jnp.int32, (tq, n), 1)
    return jnp.where(col <= row, s, NEG_INF)


def _attn_specs(seq):
    head = lambda b, h: (b, h)
    return dict(
        qn=pl.BlockSpec((seq, QK_NOPE), head),
        qr=pl.BlockSpec((None, seq, QK_ROPE), lambda b, h: (h, b, 0)),
        kr=pl.BlockSpec((seq, QK_ROPE), lambda b, h: (b, 0)),
        lse=pl.BlockSpec((None, seq, 1), lambda b, h: (h, b, 0)),
    )


def _attn_fwd(qn, qr, kn, v, kr, seq):
    t = qn.shape[0]
    tq = min(seq, 2 * CHUNK)
    sp = _attn_specs(seq)

    def body(qn_ref, qr_ref, kn_ref, v_ref, kr_ref, o_ref, lse_ref):
        for i in range(seq // tq):
            rows, n = slice(i * tq, (i + 1) * tq), (i + 1) * tq
            s = _attn_scores(qn_ref[rows, :], qr_ref[rows, :], kn_ref[0:n, :], kr_ref[0:n, :], i, tq, n)
            m = jnp.max(s, axis=-1, keepdims=True)
            p = jnp.exp(s - m)
            l = jnp.sum(p, axis=-1, keepdims=True)
            o_ref[rows, :] = jnp.dot((p / l).astype(BF16), v_ref[0:n, :], preferred_element_type=F32).astype(BF16)
            lse_ref[rows, :] = m + jnp.log(l)

    return pl.pallas_call(
        body, out_shape=[jax.ShapeDtypeStruct((t, N_HEADS * V_DIM), BF16), jax.ShapeDtypeStruct((N_HEADS, t, 1), F32)],
        grid=(t // seq, N_HEADS), in_specs=[sp["qn"], sp["qr"], sp["qn"], sp["qn"], sp["kr"]],
        out_specs=[sp["qn"], sp["lse"]], name="attn_fwd",
        compiler_params=_params(("parallel", "arbitrary")))(qn, qr, kn, v, kr)


def _attn_bwd(qn, qr, kn, v, kr, do, lse, cos2, sin2, seq):
    t = qn.shape[0]
    tq = min(seq, 2 * CHUNK)
    sp = _attn_specs(seq)

    def body(qn_ref, qr_ref, kn_ref, v_ref, kr_ref, do_ref, lse_ref, cos_ref, sin_ref,
             dqn_ref, dkn_ref, dv_ref, dqc_ref, dqs_ref, dkr_ref, dk_acc, dv_acc, dkr_acc):
        dk_acc[...] = jnp.zeros_like(dk_acc)
        dv_acc[...] = jnp.zeros_like(dv_acc)
        dkr_acc[...] = jnp.zeros_like(dkr_acc)
        for i in range(seq // tq):
            rows, n = slice(i * tq, (i + 1) * tq), (i + 1) * tq
            q_n, q_r, d_o = qn_ref[rows, :], qr_ref[rows, :], do_ref[rows, :]
            k_n, k_r = kn_ref[0:n, :], kr_ref[0:n, :]
            s = _attn_scores(q_n, q_r, k_n, k_r, i, tq, n)
            p = jnp.exp(s - lse_ref[rows, :])
            dp = lax.dot_general(d_o, v_ref[0:n, :], _DN["nt"], preferred_element_type=F32)
            ds = (p * (dp - jnp.sum(p * dp, axis=-1, keepdims=True)) * _SCALE).astype(BF16)
            dqn_ref[rows, :] = jnp.dot(ds, k_n, preferred_element_type=F32).astype(BF16)
            dqr = jnp.dot(ds, k_r, preferred_element_type=F32)
            dqc_ref[rows, :] = (dqr * cos_ref[rows, :]).astype(BF16)
            dqs_ref[rows, :] = (dqr * sin_ref[rows, :]).astype(BF16)
            dk_acc[0:n, :] += lax.dot_general(ds, q_n, _DN["tn"], preferred_element_type=F32)
            dkr_acc[0:n, :] += lax.dot_general(ds, q_r, _DN["tn"], preferred_element_type=F32)
            dv_acc[0:n, :] += lax.dot_general(p.astype(BF16), d_o, _DN["tn"], preferred_element_type=F32)
        dkn_ref[...] = dk_acc[...].astype(BF16)
        dv_ref[...] = dv_acc[...].astype(BF16)
        h = pl.program_id(1)

        @pl.when(h == 0)
        def _():
            dkr_ref[...] = dkr_acc[...]

        @pl.when(h > 0)
        def _():
            dkr_ref[...] += dkr_acc[...]

    wide = jax.ShapeDtypeStruct((t, N_HEADS * V_DIM), BF16)
    rope = jax.ShapeDtypeStruct((N_HEADS, t, QK_ROPE), BF16)
    krf = pl.BlockSpec((seq, QK_ROPE), lambda b, h: (b, 0))
    return pl.pallas_call(
        body, out_shape=[wide, wide, wide, rope, rope, jax.ShapeDtypeStruct((t, QK_ROPE), F32)],
        grid=(t // seq, N_HEADS),
        in_specs=[sp["qn"], sp["qr"], sp["qn"], sp["qn"], sp["kr"], sp["qn"], sp["lse"], krf, krf],
        out_specs=[sp["qn"], sp["qn"], sp["qn"], sp["qr"], sp["qr"], krf],
        scratch_shapes=[pltpu.VMEM((seq, QK_NOPE), F32), pltpu.VMEM((seq, V_DIM), F32), pltpu.VMEM((seq, QK_ROPE), F32)],
        name="attn_bwd", compiler_params=_params(("parallel", "arbitrary")))(qn, qr, kn, v, kr, do, lse, cos2, sin2)


def _row_tile(rows, cols, row_mult=8):
    cap = max(row_mult, (1 << 18) // cols)
    best = rows
    for tr in range(row_mult, min(rows, cap) + 1, row_mult):
        if rows % tr == 0:
            best = tr
    return best if rows > cap else rows


def _adamw_math(w, g, m, v):
    mv = ADAM_B1 * m + (1.0 - ADAM_B1) * g
    vv = ADAM_B2 * v + (1.0 - ADAM_B2) * (g * g)
    m_hat = mv / (1.0 - ADAM_B1 ** ADAM_STEP)
    v_hat = vv / (1.0 - ADAM_B2 ** ADAM_STEP)
    return -ADAM_LR * (m_hat / (jnp.sqrt(v_hat) + ADAM_EPS) + ADAM_WD * w), mv, vv


def _adamw(w, g, m, v):
    shape = w.shape
    c = shape[-1]
    r = w.size // c
    tr = _row_tile(r, c)

    def body(w_ref, g_ref, m_ref, v_ref, d_ref, nm_ref, nv_ref):
        d_ref[...], nm_ref[...], nv_ref[...] = _adamw_math(w_ref[...], g_ref[...], m_ref[...], v_ref[...])

    spec = pl.BlockSpec((tr, c), lambda i: (i, 0))
    outs = pl.pallas_call(body, out_shape=[jax.ShapeDtypeStruct((r, c), F32)] * 3, grid=(r // tr,),
                          in_specs=[spec] * 4, out_specs=[spec] * 3, name="adamw",
                          compiler_params=_params(("parallel",)))(*[a.reshape(r, c) for a in (w, g, m, v)])
    return [o.reshape(shape) for o in outs]


def _adamw_halves(w, m, v, l, j, own, recv, core, prev):
    nl, nj, rows, c = w.shape
    r = rows // 2
    tr = _row_tile(r, c)
    n_prev = 0 if prev is None else 4

    def body(core_ref, w_ref, own_ref, recv_ref, m_ref, v_ref, *rest):
        g_ref, d_ref, nm_ref, nv_ref = rest[n_prev:]
        g = jnp.where(pl.program_id(0) == core_ref[0], own_ref[...], recv_ref[...])
        g_ref[...] = g
        d_ref[...], nm_ref[...], nv_ref[...] = _adamw_math(w_ref[...], g, m_ref[...], v_ref[...])

    slab = pl.BlockSpec((None, None, None, tr, c), lambda h, i, cr: (l, j, h, i, 0))
    half = pl.BlockSpec((tr, c), lambda h, i, cr: (i, 0))
    grid_spec = pltpu.PrefetchScalarGridSpec(num_scalar_prefetch=1, grid=(2, r // tr),
                                             in_specs=[slab, half, half, slab, slab] + [_ANY] * n_prev,
                                             out_specs=[slab] * 4)
    split = lambda a: a.reshape(nl, nj, 2, r, c)
    return pl.pallas_call(body, out_shape=[jax.ShapeDtypeStruct((nl, nj, 2, r, c), F32)] * 4, grid_spec=grid_spec,
                          input_output_aliases={6 + q: q for q in range(n_prev)}, name="adamw_halves",
                          compiler_params=_params(("parallel",) * 2))(
                              core, split(w), own, recv, split(m), split(v), *(prev or ()))


def _place():
    x, y, c = lax.axis_index("x"), lax.axis_index("y"), lax.axis_index("c")
    return x, y, c, [(1 - x, y), (x, 1 - y), (1 - x, 1 - y)]


def _dma_sems(*counts):
    return [pltpu.SemaphoreType.DMA((n,)) for n in counts]


def _all_gather(bufs, collective_id, name):
    n = len(bufs)

    def body(*refs):
        ins, outs = refs[:n], refs[n:2 * n]
        send, recv, fsend, frecv, osend, orecv = refs[2 * n:]
        x, y, c, _ = _place()
        xn, yn, sib = (1 - x, y, c), (x, 1 - y, c), (x, y, 1 - c)
        k, kx, ky, kd = 2 * x + y, 2 * (1 - x) + y, 2 * x + 1 - y, 2 * (1 - x) + 1 - y
        _handshake([xn, yn, sib])

        def copy(src, dst, sems, i, to):
            return pltpu.make_async_remote_copy(src, dst, sems[0].at[i], sems[1].at[i], device_id=to, device_id_type=_MESH)

        ici, d2d, own_s = (send, recv), (fsend, frecv), (osend, orecv)
        started = [copy(ins[b], outs[b].at[k], own_s, b, sib) for b in range(n)]
        for first in (True, False):
            for b in range(n):
                mine = outs[b].at[k, c]
                if first:
                    started += [copy(ins[b].at[c, 0], mine.at[0], ici, 6 * b, xn), copy(ins[b].at[c, 1], mine.at[1], ici, 6 * b + 1, yn)]
                else:
                    started += [copy(ins[b].at[c, 1], mine.at[1], ici, 6 * b + 2, xn), copy(ins[b].at[c, 0], mine.at[0], ici, 6 * b + 3, yn)]
        for cp in started:
            cp.start()
        passed = []
        for b in range(n):
            for i, (src_chip, q, to) in enumerate([(kx, 0, yn), (ky, 1, xn)]):
                piece = outs[b].at[src_chip, c, q]
                copy(piece, piece, ici, 6 * b + i, to).wait_recv()
                cp = copy(piece, piece, ici, 6 * b + 4 + i, to)
                cp.start()
                passed.append(cp)
        for b in range(n):
            for i, (src_chip, q) in enumerate([(kx, 1), (ky, 0)]):
                piece = outs[b].at[src_chip, c, q]
                copy(piece, piece, ici, 6 * b + 2 + i, xn).wait_recv()
                half = outs[b].at[src_chip, c]
                cp = copy(half, half, d2d, 3 * b + i, sib)
                cp.start()
                passed.append(cp)
        for b in range(n):
            for i, q in enumerate([0, 1]):
                piece = outs[b].at[kd, c, q]
                copy(piece, piece, ici, 6 * b + 4 + i, xn).wait_recv()
            half = outs[b].at[kd, c]
            cp = copy(half, half, d2d, 3 * b + 2, sib)
            cp.start()
            passed.append(cp)
        for b in range(n):
            for i, src_chip in enumerate([kx, ky, kd]):
                half = outs[b].at[src_chip, 1 - c]
                copy(half, half, d2d, 3 * b + i, sib).wait_recv()
        for cp in started[n:] + passed:
            cp.wait_send()
        for cp in started[:n]:
            cp.wait()

    return _sequencer(body, [jax.ShapeDtypeStruct((N_SHARDS,) + b.shape, b.dtype) for b in bufs],
                      _dma_sems(6 * n, 6 * n, 3 * n, 3 * n, n, n), collective_id, name, bufs)


def _sequencer(body, out_type, sems, collective_id, name, args):
    return pl.kernel(body, out_type=out_type, mesh=plsc.ScalarSubcoreMesh(axis_name="sequencer", num_cores=1),
                     scratch_types=sems, compiler_params=pltpu.CompilerParams(collective_id=collective_id),
                     name=name)(*args)


def _handshake(peers):
    barrier = pltpu.get_barrier_semaphore()
    for peer in peers:
        pl.semaphore_signal(barrier, inc=1, device_id=peer, device_id_type=_MESH)
    pl.semaphore_wait(barrier, len(peers))


def _swap_halves(parts, collective_id, name):
    n = len(parts)

    def body(*refs):
        ins, outs = refs[:n], refs[n:2 * n]
        send, recv = refs[2 * n:]
        x, y, c, _ = _place()
        _handshake([(x, y, 1 - c)])
        cps = [pltpu.make_async_remote_copy(ins[b].at[:, pl.ds(1 - c, 1)], outs[b], send.at[b], recv.at[b],
                                            device_id=(x, y, 1 - c), device_id_type=_MESH) for b in range(n)]
        for cp in cps:
            cp.start()
        for cp in cps:
            cp.wait()

    return _sequencer(body, [jax.ShapeDtypeStruct((N_SHARDS, 1) + p.shape[2:], p.dtype) for p in parts],
                      _dma_sems(n, n), collective_id, name, parts)


def _add_half(part, other, core):
    _, _, r, c = part.shape
    tr = _row_tile(r, c, 16)

    def body(core_ref, p_ref, o_ref, out_ref):
        out_ref[...] = (p_ref[...].astype(F32) + o_ref[...].astype(F32)).astype(out_ref.dtype)

    grid_spec = pltpu.PrefetchScalarGridSpec(
        num_scalar_prefetch=1, grid=(N_SHARDS, r // tr),
        in_specs=[pl.BlockSpec((None, None, tr, c), lambda k, i, cr: (k, cr[0], i, 0)),
                  pl.BlockSpec((None, None, tr, c), lambda k, i, cr: (k, 0, i, 0))],
        out_specs=pl.BlockSpec((None, tr, c), lambda k, i, cr: (k, i, 0)))
    return pl.pallas_call(body, out_shape=jax.ShapeDtypeStruct((N_SHARDS, r, c), part.dtype), grid_spec=grid_spec,
                          name="grad_add_half", compiler_params=_params(("parallel", "parallel")))(core, part, other)


def _scatter_chips(parts, collective_id, name):
    n = len(parts)

    def body(*refs):
        ins, outs = refs[:n], refs[n:2 * n]
        send, recv = refs[2 * n:]
        x, y, c, chips = _place()
        k = 2 * x + y
        _handshake([(px, py, c) for px, py in chips])
        started = []
        for b in range(n):
            for j, (px, py) in enumerate(chips):
                cp = pltpu.make_async_remote_copy(ins[b].at[2 * px + py], outs[b].at[k], send.at[3 * b + j],
                                                  recv.at[3 * b + j], device_id=(px, py, c), device_id_type=_MESH)
                cp.start()
                started.append(cp)
        for b in range(n):
            for j, (px, py) in enumerate(chips):
                got = outs[b].at[2 * px + py]
                pltpu.make_async_remote_copy(got, got, send.at[3 * b + j], recv.at[3 * b + j],
                                             device_id=(px, py, c), device_id_type=_MESH).wait_recv()
        for cp in started:
            cp.wait_send()

    return _sequencer(body, [jax.ShapeDtypeStruct(p.shape, p.dtype) for p in parts], _dma_sems(3 * n, 3 * n),
                      collective_id, name, parts)


def _sum_slots(slots, mine, chip):
    _, r, c = slots.shape
    tr = _row_tile(r, c, 16)

    def body(chip_ref, s0, s1, s2, s3, own_ref, out_ref):
        own = own_ref[...].astype(F32)
        v = [jnp.where(chip_ref[0] == s, own, ref[...].astype(F32)) for s, ref in enumerate((s0, s1, s2, s3))]
        out_ref[...] = ((v[0] + v[1]) + v[2]) + v[3]

    def slot_spec(s):
        return pl.BlockSpec((None, tr, c), lambda i, kr: (jnp.where(kr[0] == s, (s + 1) % N_SHARDS, s), i, 0))

    grid_spec = pltpu.PrefetchScalarGridSpec(
        num_scalar_prefetch=1, grid=(r // tr,),
        in_specs=[slot_spec(s) for s in range(N_SHARDS)] + [pl.BlockSpec((None, tr, c), lambda i, kr: (kr[0], i, 0))],
        out_specs=pl.BlockSpec((tr, c), lambda i, kr: (i, 0)))
    return pl.pallas_call(body, out_shape=jax.ShapeDtypeStruct((r, c), F32), grid_spec=grid_spec, name="grad_sum_slots",
                          compiler_params=_params(("parallel",)))(chip, slots, slots, slots, slots, mine)


def _join_halves(halves, collective_id, name):
    n = len(halves)

    def body(*refs):
        ins, outs = refs[:n], refs[n:2 * n]
        send, recv = refs[2 * n:]
        x, y, c, _ = _place()
        _handshake([(x, y, 1 - c)])
        cps = [pltpu.make_async_remote_copy(ins[b], outs[b], send.at[b], recv.at[b], device_id=(x, y, 1 - c),
                                            device_id_type=_MESH) for b in range(n)]
        for cp in cps:
            cp.start()
        for cp in cps:
            cp.wait()

    return _sequencer(body, [jax.ShapeDtypeStruct(h.shape, F32) for h in halves], _dma_sems(n, n), collective_id,
                      name, halves)


def _gather_rows(buf, start, rows):
    def body(in_ref, out_ref, send, recv, lsem):
        x, y, c, chips = _place()
        k = 2 * x + y
        src = in_ref.at[pl.ds(start, rows)]
        local = pltpu.make_async_remote_copy(src, out_ref.at[k], lsem.at[0], lsem.at[1], device_id=(x, y, 1 - c),
                                             device_id_type=_MESH)
        local.start()
        cps = [pltpu.make_async_remote_copy(src, out_ref.at[k], send.at[j], recv.at[j], device_id=(px, py, c),
                                            device_id_type=_MESH) for j, (px, py) in enumerate(chips)]
        for cp in cps:
            cp.start()
        for j, (px, py) in enumerate(chips):
            got = out_ref.at[2 * px + py]
            pltpu.make_async_remote_copy(got, got, send.at[j], recv.at[j], device_id=(px, py, c),
                                         device_id_type=_MESH).wait_recv()
        for cp in cps:
            cp.wait_send()
        local.wait()

    return pl.pallas_call(body, out_shape=jax.ShapeDtypeStruct((N_SHARDS, rows, buf.shape[1]), F32),
                          in_specs=[_ANY], out_specs=_ANY, scratch_shapes=_dma_sems(3, 3, 2),
                          name="gather_replicated_grads")(buf)


def _all_sum(vec):
    r, c = vec.shape
    n_dev = 2 * N_SHARDS

    def body(in_ref, out_ref, slots, send, recv):
        x, y, cc, _ = _place()
        flip = lambda v, bit: 1 - v if bit else v
        peers = [(flip(x, (q >> 2) & 1), flip(y, (q >> 1) & 1), flip(cc, q & 1)) for q in range(1, n_dev)]
        index = lambda p: 4 * p[0] + 2 * p[1] + p[2]
        slots[index((x, y, cc))] = in_ref[...]
        cps = [pltpu.make_async_remote_copy(in_ref, slots.at[index((x, y, cc))], send.at[q], recv.at[q], device_id=p,
                                            device_id_type=_MESH) for q, p in enumerate(peers)]
        for cp in cps:
            cp.start()
        for q, p in enumerate(peers):
            got = slots.at[index(p)]
            pltpu.make_async_remote_copy(got, got, send.at[q], recv.at[q], device_id=p, device_id_type=_MESH).wait_recv()
        for cp in cps:
            cp.wait_send()
        acc = slots[0]
        for s in range(1, n_dev):
            acc = acc + slots[s]
        out_ref[...] = acc

    vmem = pl.BlockSpec(memory_space=pltpu.VMEM)
    return pl.pallas_call(body, out_shape=jax.ShapeDtypeStruct((r, c), F32), in_specs=[vmem], out_specs=vmem,
                          scratch_shapes=[pltpu.VMEM((n_dev, r, c), F32)] + _dma_sems(n_dev - 1, n_dev - 1),
                          name="sum_small_grads")(vec)


def _not_before(value, other):
    return lax.optimization_barrier((value, other))[0]


def _round_up(n, m):
    return -(-n // m) * m


def _pack_flat(vecs, rows, width, dtype):
    flat = jnp.concatenate([v.reshape(-1).astype(dtype) for v in vecs])
    return jnp.pad(flat, (0, rows * width - flat.size)).reshape(rows, width)


def _split_flat(flat, shapes):
    out, off = [], 0
    for s in shapes:
        n = math.prod(s)
        out.append(flat[off:off + n].reshape(s))
        off += n
    return out


def _merge_shards(arr4, axis):
    a = jnp.moveaxis(arr4, 0, axis)
    s = list(a.shape)
    return a.reshape(s[:axis] + [s[axis] * s[axis + 1]] + s[axis + 2:])


def _split_shards(full, axis):
    s = list(full.shape)
    a = full.reshape(s[:axis] + [N_SHARDS, s[axis] // N_SHARDS] + s[axis + 1:])
    return jnp.moveaxis(a, axis, 0).reshape(N_SHARDS, -1)


def _rot_cols(w):
    half = w.shape[-1] // 2
    return jnp.concatenate([-w[..., half:], w[..., :half]], axis=-1)


def _unrot_cols(dw):
    half = dw.shape[-1] // 2
    return jnp.concatenate([dw[..., half:], -dw[..., :half]], axis=-1)


def kernel(x, positions, ffn_pre_g, ffn_post_g, ffn_w_gate, ffn_w_up, ffn_w_down, mix_pre_g, mix_post_g, gmlp_w_in, gmlp_ln_g, gmlp_ln_b, gmlp_w_s, gmlp_b_s, gmlp_w_out, kv_norm_g, w_dkv, kv_a_norm_g, w_ukv, mla_w_dq, mla_q_norm_g, mla_w_uq, mla_w_o, loss_target, m_ffn_pre_g, m_ffn_post_g, m_ffn_w_gate, m_ffn_w_up, m_ffn_w_down, m_mix_pre_g, m_mix_post_g, m_gmlp_w_in, m_gmlp_ln_g, m_gmlp_ln_b, m_gmlp_w_s, m_gmlp_b_s, m_gmlp_w_out, m_kv_norm_g, m_w_dkv, m_kv_a_norm_g, m_w_ukv, m_mla_w_dq, m_mla_q_norm_g, m_mla_w_uq, m_mla_w_o, v_ffn_pre_g, v_ffn_post_g, v_ffn_w_gate, v_ffn_w_up, v_ffn_w_down, v_mix_pre_g, v_mix_post_g, v_gmlp_w_in, v_gmlp_ln_g, v_gmlp_ln_b, v_gmlp_w_s, v_gmlp_b_s, v_gmlp_w_out, v_kv_norm_g, v_w_dkv, v_kv_a_norm_g, v_w_ukv, v_mla_w_dq, v_mla_q_norm_g, v_mla_w_uq, v_mla_w_o):
    names = ["ffn_pre_g", "ffn_post_g", "ffn_w_gate", "ffn_w_up", "ffn_w_down", "mix_pre_g", "mix_post_g", "gmlp_w_in",
             "gmlp_ln_g", "gmlp_ln_b", "gmlp_w_s", "gmlp_b_s", "gmlp_w_out", "kv_norm_g", "w_dkv", "kv_a_norm_g", "w_ukv",
             "mla_w_dq", "mla_q_norm_g", "mla_w_uq", "mla_w_o"]
    env = locals()
    w = {n: env[n] for n in names}
    mom = {n: env["m_" + n] for n in names}
    var = {n: env["v_" + n] for n in names}

    bsz, seq, d = x.shape
    t = bsz * seq
    core = lax.axis_index("c").astype(jnp.int32).reshape(1)

    mats = [("gmlp_w_in", 2), ("gmlp_w_out", 1), ("w_dkv", 0), ("w_ukv", 1), ("mla_w_dq", 1), ("mla_w_uq", 2),
            ("mla_w_o", 1)]
    vecs = [("ffn_pre_g", 2), ("ffn_post_g", 2), ("gmlp_ln_g", 1), ("gmlp_ln_b", 1)]
    replicated = ["mix_pre_g", "mix_post_g", "gmlp_w_s", "gmlp_b_s", "kv_norm_g", "kv_a_norm_g", "mla_q_norm_g"]
    n_mats = sum(w[n].size for n, _ in mats)
    n_vecs = sum(w[n].size for n, _ in vecs)
    mat_rows = _round_up(-(-n_mats // PACK_WIDTH), 64)
    vec_rows = _round_up(-(-n_vecs // 128), 32)
    mat_pack = _pack_flat([w[n] for n, _ in mats], mat_rows, PACK_WIDTH, BF16).reshape(2, 2, mat_rows // 4, PACK_WIDTH)
    vec_pack = _pack_flat([w[n] for n, _ in vecs], vec_rows, 128, F32).reshape(2, 2, vec_rows // 4, 128)
    ffn_names = ("ffn_w_gate", "ffn_w_up", "ffn_w_down")
    lj = [(l, j) for l in range(2) for j in range(2)]
    riders = {(0, 0): [vec_pack], (0, 1): [mat_pack], (1, 0): [], (1, 1): []}
    ffn_w = {}
    landed = []
    for q, (l, j) in enumerate(lj):
        shards = [w[n][l, j].astype(BF16) for n in ffn_names]
        bufs = [s.reshape(2, 2, s.shape[0] // 4, s.shape[1]) for s in shards] + riders[(l, j)]
        if q > 0:
            bufs = _not_before(bufs, landed[0 if q < 3 else 1])
        got = _all_gather(bufs, q + 1, f"gather_weights_{q}")
        landed.append(got[-1])
        ffn_w[(l, j)] = [g.reshape((N_SHARDS,) + s.shape) for g, s in zip(got, shards)]
        if (l, j) == (0, 0):
            vec_all = got[3]
        if (l, j) == (0, 1):
            mat_all = got[3]

    def unpack(packed, entries):
        flat4, off, out = packed.reshape(N_SHARDS, -1), 0, {}
        for n, ax in entries:
            out[n] = _merge_shards(flat4[:, off:off + w[n].size].reshape((N_SHARDS,) + w[n].shape), ax)
            off += w[n].size
        return out

    full = unpack(vec_all, vecs)
    ln_g, ln_b = full["gmlp_ln_g"], full["gmlp_ln_b"]
    pre_g, post_g = full["ffn_pre_g"], full["ffn_post_g"]
    w_s = w["gmlp_w_s"][0]
    bsb = jnp.broadcast_to(w["gmlp_b_s"][0][:, :, None], (GROUPS, CHUNK, CHUNK))
    row = lambda v: v.reshape(1, -1)

    inv_freq = ROPE_THETA ** (-jnp.arange(0, QK_ROPE, 2, dtype=F32) / QK_ROPE)
    ang = positions.astype(F32).reshape(t, 1) * inv_freq
    cos2 = jnp.concatenate([jnp.cos(ang)] * 2, axis=-1)
    sin2 = jnp.concatenate([jnp.sin(ang)] * 2, axis=-1)
    cos_h, sin_h = jnp.tile(cos2, (1, N_HEADS)), jnp.tile(sin2, (1, N_HEADS))

    def rope_epi(n_lin):
        def epi(accs, ex):
            return accs[:n_lin] + [accs[n_lin] * ex[0] + accs[n_lin + 1] * ex[1]]
        return epi

    h0 = x.reshape(t, d)
    saved = {}

    def ffn_fwd(l, j, h, n, next_gs):
        wg, wu, wd = ffn_w[(l, j)]
        g, u, a = _ffn_up(n, wg, wu)
        f, h_new, *n_next = _ffn_down(a, wd, h, row(post_g[l, j]), next_gs)
        saved[("ffn", l, j)] = (h, n, g, u, a, f)
        return h_new, n_next

    n0 = _rms_fwd(h0, row(pre_g[0, 0]))
    h1, (n1,) = ffn_fwd(0, 0, h0, n0, row(w["mix_pre_g"][0]))

    full.update(unpack(_not_before(mat_all, h1), mats))
    w_in, w_out = full["gmlp_w_in"][0], full["gmlp_w_out"][0]
    w_c, w_kr = full["w_dkv"][:, :KV_RANK], full["w_dkv"][:, KV_RANK:]
    w_kr_rot = _rot_cols(w_kr)
    ukv = full["w_ukv"].reshape(KV_RANK, N_HEADS, 2, QK_NOPE)
    w_k, w_v = ukv[:, :, 0].reshape(KV_RANK, -1), ukv[:, :, 1].reshape(KV_RANK, -1)
    w_dq, w_o = full["mla_w_dq"][0], full["mla_w_o"][0]
    q_rank = w_dq.shape[1]
    uq = full["mla_w_uq"][0].reshape(q_rank, N_HEADS, QK_NOPE + QK_ROPE)
    w_qn = uq[:, :, :QK_NOPE].reshape(q_rank, -1)
    w_qr = uq[:, :, QK_NOPE:].reshape(q_rank, -1)
    w_qr_rot = _rot_cols(uq[:, :, QK_NOPE:]).reshape(q_rank, -1)

    zp = _mm2d("gmlp_in", [(n1, w_in, "nn", 0)], [(w_in.shape[1], F32)])[0]
    uv = _sgu_fwd(zp, ln_g, ln_b, w_s, bsb)
    half = uv.shape[1]
    tm = min(t, ROW_TILE)
    m0, h2, n2 = _down("gmlp_out", (uv, (tm, 512), lambda i, _, k: (i, k)), (w_out, (512, d), lambda i, _, k: (k, 0)),
                       half // 512, h1, row(w["mix_post_g"][0]), row(pre_g[0, 1]), 1.0)
    h3, (n3kv, n3) = ffn_fwd(0, 1, h2, n2, jnp.stack([w["kv_norm_g"], pre_g[1, 0]]))

    def kv_epi(accs, ex):
        c_raw = accs[0]
        return [c_raw, _rms(c_raw, ex[2]), accs[1] * ex[0] + accs[2] * ex[1]]

    c_raw, c_n, k_r = _mm2d("kv_down", [(n3kv, w_c, "nn", 0), (n3kv, w_kr, "nn", 1), (n3kv, w_kr_rot, "nn", 2)],
                            [(KV_RANK, F32), (KV_RANK, BF16), (QK_ROPE, BF16)], kv_epi, [cos2, sin2],
                            [row(w["kv_a_norm_g"])])
    k_n, v_h = _mm2d("kv_up", [(c_n, w_k, "nn", 0), (c_n, w_v, "nn", 1)], [(w_k.shape[1], BF16), (w_v.shape[1], BF16)])

    h4, (n4,) = ffn_fwd(1, 0, h3, n3, row(w["mix_pre_g"][1]))
    qd, qn = _mm2d("q_down", [(n4, w_dq, "nn", 0)], [(q_rank, F32), (q_rank, BF16)],
                   lambda accs, ex: [accs[0], _rms(accs[0], ex[0])], [], [row(w["mla_q_norm_g"][0])])
    q_n, q_r = _mm2d("q_up", [(qn, w_qn, "nn", 0), (qn, w_qr, "nn", 1), (qn, w_qr_rot, "nn", 2)],
                     [(w_qn.shape[1], BF16), (w_qr.shape[1], BF16)], rope_epi(1), [cos_h, sin_h])
    q_r = q_r.reshape(t, N_HEADS, QK_ROPE).transpose(1, 0, 2)
    o, lse = _attn_fwd(q_n, q_r, k_n, v_h, k_r, seq)
    m1, h5, n5 = _down("attn_out", (o, (tm, 512), lambda i, _, k: (i, k)), (w_o, (512, d), lambda i, _, k: (k, 0)),
                       o.shape[1] // 512, h4, row(w["mix_post_g"][1]), row(pre_g[1, 1]), 1.0)
    y, _ = ffn_fwd(1, 1, h5, n5, row(pre_g[1, 1]))

    loss_part, dy = _loss_head(y, loss_target.reshape(t, d))
    loss = lax.psum(loss_part, ("x", "y", "c"))

    chip = (2 * lax.axis_index("x") + lax.axis_index("y")).astype(jnp.int32).reshape(1)
    rs = {}

    def rs_launch(gid, parts):
        rs[gid] = {"parts": parts, "others": _swap_halves(parts, 5 + gid, f"grad_swap_{gid}")}

    def rs_mid(gid, after):
        r = rs[gid]
        parts, others = _not_before((r["parts"], r["others"]), after)
        r["chip"] = [_add_half(p, o, core) for p, o in zip(parts, others)]
        r["slots"] = _scatter_chips(r["chip"], 10 + gid, f"grad_scatter_{gid}")
        return r["chip"]

    def rs_end(gid, after):
        r = rs[gid]
        slots, mine = _not_before((r["slots"], r["chip"]), after)
        r["own"] = [_sum_slots(s, p, chip) for s, p in zip(slots, mine)]
        r["recv"] = _join_halves(r["own"], 15 + gid, f"grad_join_{gid}")
        return r["own"]

    d_pre, d_post = {}, {}

    def ffn_bwd(l, j, gid, dh_out, extra=(), after_post=None):
        h, n, g, u, a, f = saved[("ffn", l, j)]
        df, d_post[(l, j)] = _norm_out_bwd("ffn_post_bwd", f, dh_out, row(post_g[l, j]), 0.5)
        if after_post is not None:
            df = after_post(df)
        wg, wu, wd = ffn_w[(l, j)]
        dg, du = _ffn_dact(df, wd, g, u)
        dwd = _ffn_dw_down(a, df)
        dwg, dwu = _ffn_dw_in(n, dg, du)
        parts = [p.reshape(N_SHARDS, 2, p.shape[1] // 2, p.shape[2]) for p in (dwg, dwu, dwd)]
        rs_launch(gid, parts)
        dg, du = _not_before((dg, du), parts)
        dn = _ffn_dn(dg, du, wg, wu)
        dh, d_pre[(l, j)], *rest = _norm_in_bwd("ffn_pre_bwd", h, dh_out, [(row(pre_g[l, j]), dn)] + list(extra))
        return dh, rest

    dh5, _ = ffn_bwd(1, 1, 0, dy)
    dh5 = _not_before(dh5, rs_mid(0, dh5))

    dm1, g_mix_post1 = _norm_out_bwd("mix_post_bwd", m1, dh5, row(w["mix_post_g"][1]), 1.0)
    do = _mm2d("attn_out_dx", [(dm1, w_o, "nt", 0)], [(w_o.shape[0], BF16)])[0]
    g_w_o = _mm2d("attn_out_dw", [(o, dm1, "tn", 0)], [(d, F32)])[0]
    dq_n, dk_n, dv_h, dq_c, dq_s, dk_r = _attn_bwd(q_n, q_r, k_n, v_h, k_r, do, lse, cos2, sin2, seq)
    dq_c = dq_c.transpose(1, 0, 2).reshape(t, -1)
    dq_s = dq_s.transpose(1, 0, 2).reshape(t, -1)
    dqn = _mm2d("q_up_dx", [(dq_n, w_qn, "nt", 0), (dq_c, w_qr, "nt", 0), (dq_s, w_qr_rot, "nt", 0)], [(q_rank, F32)])[0]
    g_qn, g_qr, g_qr_rot = _mm2d("q_up_dw", [(qn, dq_n, "tn", 0), (qn, dq_c, "tn", 1), (qn, dq_s, "tn", 2)],
                                 [(w_qn.shape[1], F32), (w_qr.shape[1], F32), (w_qr.shape[1], F32)])
    dqd, g_q_norm = _norm_out_bwd("q_norm_bwd", qd, dqn, row(w["mla_q_norm_g"][0]), 1.0)
    dn4 = _mm2d("q_down_dx", [(dqd, w_dq, "nt", 0)], [(d, F32)])[0]
    g_w_dq = _mm2d("q_down_dw", [(n4, dqd, "tn", 0)], [(q_rank, F32)])[0]
    dh4, g_mix_pre1 = _norm_in_bwd("mix_pre_bwd", h4, dh5, [(row(w["mix_pre_g"][1]), dn4)])

    dc_n = _mm2d("kv_up_dx", [(dk_n, w_k, "nt", 0), (dv_h, w_v, "nt", 0)], [(KV_RANK, F32)])[0]
    g_wk, g_wv = _mm2d("kv_up_dw", [(c_n, dk_n, "tn", 0), (c_n, dv_h, "tn", 1)], [(w_k.shape[1], F32), (w_v.shape[1], F32)])
    dc, g_kv_a = _norm_out_bwd("kv_a_norm_bwd", c_raw, dc_n, row(w["kv_a_norm_g"]), 1.0)
    dkr_c, dkr_s = _rope_bwd(dk_r, cos2, sin2)
    dn3kv = _mm2d("kv_down_dx", [(dc, w_c, "nt", 0), (dkr_c, w_kr, "nt", 0), (dkr_s, w_kr_rot, "nt", 0)], [(d, F32)])[0]
    g_wc, g_wkr, g_wkr_rot = _mm2d("kv_down_dw", [(n3kv, dc, "tn", 0), (n3kv, dkr_c, "tn", 1), (n3kv, dkr_s, "tn", 2)],
                                   [(KV_RANK, F32), (QK_ROPE, F32), (QK_ROPE, F32)])

    dh4 = _not_before(dh4, rs_end(0, dh4))
    dh3, (g_kv_norm,) = ffn_bwd(1, 0, 1, dh4, extra=[(row(w["kv_norm_g"]), dn3kv)])
    dh3 = _not_before(dh3, rs_mid(1, dh3))
    dh2, _ = ffn_bwd(0, 1, 2, dh3)
    dh2 = _not_before(dh2, (rs_end(1, dh2), rs_mid(2, dh2)))

    dm0, g_mix_post0 = _norm_out_bwd("mix_post_bwd", m0, dh2, row(w["mix_post_g"][0]), 1.0)
    d_uv = _mm2d("gmlp_out_dx", [(dm0, w_out, "nt", 0)], [(half, F32)])[0]
    g_w_out = _mm2d("gmlp_out_dw", [(uv, dm0, "tn", 0)], [(d, F32)])[0]
    dzp, g_ln_g, g_ln_b, g_w_s, g_b_s = _sgu_bwd(zp, d_uv, ln_g, ln_b, w_s, bsb)
    dn1 = _mm2d("gmlp_in_dx", [(dzp, w_in, "nt", 0)], [(d, F32)])[0]
    g_w_in = _mm2d("gmlp_in_dw", [(n1, dzp, "tn", 0)], [(w_in.shape[1], F32)])[0]
    dh1, g_mix_pre0 = _norm_in_bwd("mix_pre_bwd", h1, dh2, [(row(w["mix_pre_g"][0]), dn1)])
    part = {
        "gmlp_w_in": g_w_in[None], "gmlp_w_out": g_w_out[None],
        "w_dkv": jnp.concatenate([g_wc, g_wkr + _unrot_cols(g_wkr_rot)], axis=1),
        "w_ukv": jnp.stack([g_wk.reshape(KV_RANK, N_HEADS, QK_NOPE), g_wv.reshape(KV_RANK, N_HEADS, V_DIM)],
                           axis=2).reshape(KV_RANK, -1),
        "mla_w_dq": g_w_dq[None],
        "mla_w_uq": jnp.concatenate(
            [g_qn.reshape(q_rank, N_HEADS, QK_NOPE),
             g_qr.reshape(q_rank, N_HEADS, QK_ROPE) + _unrot_cols(g_qr_rot.reshape(q_rank, N_HEADS, QK_ROPE))],
            axis=-1).reshape(1, q_rank, -1),
        "mla_w_o": g_w_o[None],
        "gmlp_ln_g": g_ln_g, "gmlp_ln_b": g_ln_b,
        "mix_pre_g": jnp.concatenate([g_mix_pre0, g_mix_pre1]), "mix_post_g": jnp.concatenate([g_mix_post0, g_mix_post1]),
        "gmlp_w_s": g_w_s[None], "gmlp_b_s": g_b_s.reshape(1, GROUPS, CHUNK),
        "kv_norm_g": g_kv_norm.reshape(-1), "kv_a_norm_g": g_kv_a.reshape(-1), "mla_q_norm_g": g_q_norm,
    }

    sharded = mats + [e for e in vecs if e[0].startswith("gmlp")]
    n_sh = sum(w[n].size for n, _ in sharded)
    n_rep = sum(w[n].size for n in replicated)
    sh_rows = _round_up(-(-n_sh // PACK_WIDTH), 8)
    rep_rows = _round_up(-(-(n_rep // N_SHARDS) // PACK_WIDTH), 8)
    rows = _round_up(sh_rows + rep_rows, 32)
    sh_flat = jnp.concatenate([_split_shards(part[n], ax) for n, ax in sharded], axis=1)
    rep_flat = jnp.concatenate([part[n].reshape(-1) for n in replicated]).reshape(N_SHARDS, -1)
    small = jnp.concatenate([
        jnp.pad(sh_flat, ((0, 0), (0, sh_rows * PACK_WIDTH - n_sh))),
        jnp.pad(rep_flat, ((0, 0), (0, (rows - sh_rows) * PACK_WIDTH - n_rep // N_SHARDS)))], axis=1)
    small = small.astype(BF16).reshape(N_SHARDS, 2, rows // 2, PACK_WIDTH)

    rs_launch(3, [small])
    dh1 = _not_before(dh1, rs_end(2, dh1))
    dx, _ = ffn_bwd(0, 0, 4, dh1, after_post=lambda df: _not_before(df, rs_mid(3, df)))

    launched = rs_mid(4, dx)
    lj = [(l, j) for l in range(2) for j in range(2)]
    tiny = jnp.concatenate([d_pre[k] for k in lj] + [d_post[k] for k in lj]).reshape(-1, 128)
    tiny = _all_sum(_not_before(tiny, launched)).reshape(2, 2, 2, d)
    shard_cols = d // N_SHARDS
    grads = {"ffn_pre_g": lax.dynamic_slice_in_dim(tiny[0], chip[0] * shard_cols, shard_cols, axis=2),
             "ffn_post_g": lax.dynamic_slice_in_dim(tiny[1], chip[0] * shard_cols, shard_cols, axis=2)}
    own_small, recv_small = rs_end(3, launched)[0], rs[3]["recv"][0]
    g_small = jnp.where(core[0] == 0, jnp.concatenate([own_small, recv_small]), jnp.concatenate([recv_small, own_small]))
    g_rep = _gather_rows(g_small, sh_rows, rep_rows)
    for (n, _), g in zip(sharded, _split_flat(g_small.reshape(-1), [w[n].shape for n, _ in sharded])):
        grads[n] = g
    rep_vec = g_rep.reshape(N_SHARDS, -1)[:, :n_rep // N_SHARDS].reshape(-1)
    for n, g in zip(replicated, _split_flat(rep_vec, [w[n].shape for n in replicated])):
        grads[n] = g

    delta, new_m, new_v = {}, {}, {}
    for n in names:
        if n not in ffn_names:
            delta[n], new_m[n], new_v[n] = _adamw(w[n], grads[n], mom[n], var[n])
    chain = {n: None for n in ffn_names}

    def ffn_update(gid, l, j):
        for q, n in enumerate(ffn_names):
            chain[n] = _adamw_halves(w[n], mom[n], var[n], l, j, rs[gid]["own"][q], rs[gid]["recv"][q], core, chain[n])

    ffn_update(0, 1, 1)
    ffn_update(1, 1, 0)
    ffn_update(2, 0, 1)
    rs_end(4, ([delta[n] for n in delta], [chain[n] for n in ffn_names]))
    ffn_update(4, 0, 0)
    for n in ffn_names:
        grads[n], delta[n], new_m[n], new_v[n] = [o.reshape(w[n].shape) for o in chain[n]]
    return (loss, dx.reshape(x.shape), *[grads[n] for n in names], *[delta[n] for n in names],
            *[new_m[n] for n in names], *[new_v[n] for n in names])
```

```python
import math

import jax
import jax.numpy as jnp
from jax import lax
from jax.experimental import pallas as pl
from jax.experimental.pallas import tpu as pltpu
from jax.experimental.pallas import tpu_sc as plsc

F32, BF16 = jnp.float32, jnp.bfloat16

RMS_EPS, LN_EPS, NEG_INF = 1e-6, 1e-5, -1e30
N_HEADS, QK_NOPE, QK_ROPE, V_DIM, KV_RANK = 8, 128, 64, 128, 256
CHUNK, GROUPS = 128, 16
ROPE_THETA = 10000.0
ADAM_LR, ADAM_B1, ADAM_B2, ADAM_EPS, ADAM_WD, ADAM_STEP = 0.001, 0.9, 0.999, 1e-08, 0.01, 10
N_SHARDS = 4

VMEM_LIMIT_BYTES = 48 * 1024 * 1024
ROW_TILE = 512
K_TILE = 2048
PACK_WIDTH = 1024

_DN = {"nn": (((1,), (0,)), ((), ())), "nt": (((1,), (1,)), ((), ())), "tn": (((0,), (0,)), ((), ()))}
_MESH = pl.DeviceIdType.MESH
_ANY = pl.BlockSpec(memory_space=pl.ANY)


def _params(sem):
    return pltpu.CompilerParams(dimension_semantics=sem, vmem_limit_bytes=VMEM_LIMIT_BYTES)


def _mm(name, grid, ins, pairs, acc_shapes, outs, epilogue, extras=(), inner=0):
    n_in, n_ex, n_out = len(ins), len(extras), len(outs)
    gk = grid[2]

    def body(*refs):
        in_refs, ex_refs = refs[:n_in], refs[n_in:n_in + n_ex]
        out_refs = refs[n_in + n_ex:n_in + n_ex + n_out]
        acc_refs = refs[n_in + n_ex + n_out:]
        parts = [None] * len(acc_shapes)
        for a, b, c, dims in pairs:
            for s in range(max(inner, 1)):
                lhs, rhs = (in_refs[a][s], in_refs[b][s]) if inner else (in_refs[a][...], in_refs[b][...])
                p = lax.dot_general(lhs, rhs, _DN[dims], preferred_element_type=F32)
                parts[c] = p if parts[c] is None else parts[c] + p

        def finish(accs):
            vals = epilogue(accs, [r[...] for r in ex_refs])
            for r, v in zip(out_refs, vals):
                r[...] = v.astype(r.dtype)

        if gk == 1:
            finish(parts)
        else:
            k = pl.program_id(2)

            @pl.when(k == 0)
            def _():
                for r, p in zip(acc_refs, parts):
                    r[...] = p

            @pl.when(k > 0)
            def _():
                for r, p in zip(acc_refs, parts):
                    r[...] += p

            @pl.when(k == gk - 1)
            def _():
                finish([r[...] for r in acc_refs])

    return pl.pallas_call(
        body,
        out_shape=[jax.ShapeDtypeStruct(s, d) for s, d, _, _ in outs],
        grid=grid,
        in_specs=[pl.BlockSpec(bs, im) for _, bs, im in list(ins) + list(extras)],
        out_specs=[pl.BlockSpec(bs, im) for _, _, bs, im in outs],
        scratch_shapes=[pltpu.VMEM(s, F32) for s in acc_shapes] if gk > 1 else [],
        name=name,
        compiler_params=_params(("parallel", "parallel", "arbitrary")),
    )(*[a for a, _, _ in ins], *[a for a, _, _ in extras])


def _mm2d(name, pairs, outs, epilogue=None, row_extras=(), vec_extras=()):
    def mk(a, dims):
        return (a.shape[0], a.shape[1]) if dims[0] == "n" else (a.shape[1], a.shape[0])

    def nk(b, dims):
        return (b.shape[1], b.shape[0]) if dims[1] == "n" else (b.shape[0], b.shape[1])

    m = mk(pairs[0][0], pairs[0][2])[0]
    ks = [mk(a, d)[1] for a, _, d, _ in pairs]
    n_acc = 1 + max(p[3] for p in pairs)
    acc_n = [None] * n_acc
    for a, b, d, c in pairs:
        assert mk(a, d)[0] == m and nk(b, d)[1] == mk(a, d)[1]
        acc_n[c] = nk(b, d)[0]
    tm = min(m, ROW_TILE)
    if len(set(ks)) == 1 and ks[0] > 1024:
        tk = K_TILE if ks[0] % K_TILE == 0 else 512
        tks, gk = [tk] * len(pairs), ks[0] // tk
    else:
        tks, gk = ks, 1
    if len(set(acc_n)) == 1 and acc_n[0] > 1024:
        tns, gj = [1024] * n_acc, acc_n[0] // 1024
    else:
        tns, gj = acc_n, 1

    ins, plist = [], []
    for (a, b, d, c), tk in zip(pairs, tks):
        tn = tns[c]
        a_spec = ((tm, tk), lambda i, j, k: (i, k)) if d[0] == "n" else ((tk, tm), lambda i, j, k: (k, i))
        b_spec = ((tk, tn), lambda i, j, k: (k, j)) if d[1] == "n" else ((tn, tk), lambda i, j, k: (j, k))
        ins += [(a, *a_spec), (b, *b_spec)]
        plist.append((len(ins) - 2, len(ins) - 1, c, d))
    extras = [(r, (tm, r.shape[1]), lambda i, j, k: (i, 0)) for r in row_extras]
    extras += [(v, v.shape, lambda i, j, k: (0, 0)) for v in vec_extras]
    out_specs = []
    for n, dt in outs:
        bn = 1024 if (gj > 1) else n
        out_specs.append(((m, n), dt, (tm, bn), lambda i, j, k: (i, j)))
    if epilogue is None:
        epilogue = lambda accs, ex: accs
    return _mm(name, (m // tm, gj, gk), ins, plist, [(tm, tn) for tn in tns], out_specs, epilogue, extras)


def _rms(x, g):
    return x * lax.rsqrt(jnp.mean(x * x, axis=-1, keepdims=True) + RMS_EPS) * g


def _rms_bwd(x, g, dy):
    r = lax.rsqrt(jnp.mean(x * x, axis=-1, keepdims=True) + RMS_EPS)
    gy = dy * g
    dx = r * gy - x * (r * r * r) * jnp.mean(gy * x, axis=-1, keepdims=True)
    return dx, jnp.sum(dy * x * r, axis=0, keepdims=True)


def _sigmoid(x):
    return 1.0 / (1.0 + jnp.exp(-x))


_GELU_C = math.sqrt(2.0 / math.pi)


def _gelu(x):
    return x * (0.5 * (1.0 + jnp.tanh(_GELU_C * (x + 0.044715 * (x * x * x)))))


def _gelu_grad(x):
    t = jnp.tanh(_GELU_C * (x + 0.044715 * (x * x * x)))
    return 0.5 * (1.0 + t) + 0.5 * x * (1.0 - t * t) * (_GELU_C * (1.0 + 3.0 * 0.044715 * (x * x)))


def _rows(name, row_ins, vec_ins, fn, row_outs, acc_outs=()):
    t = row_ins[0].shape[0]
    tm = min(t, ROW_TILE)
    nr, nv, no = len(row_ins), len(vec_ins), len(row_outs)

    def body(*refs):
        outs, incs = fn([r[...] for r in refs[:nr]], [r[...] for r in refs[nr:nr + nv]])
        for r, v in zip(refs[nr + nv:nr + nv + no], outs):
            r[...] = v.astype(r.dtype)
        i = pl.program_id(0)
        for r, v in zip(refs[nr + nv + no:], incs):
            @pl.when(i == 0)
            def _():
                r[...] = v

            @pl.when(i > 0)
            def _():
                r[...] += v

    in_specs = [pl.BlockSpec((tm, a.shape[1]), lambda i: (i, 0)) for a in row_ins]
    in_specs += [pl.BlockSpec(v.shape, lambda i, nd=v.ndim: (0,) * nd) for v in vec_ins]
    out_shape = [jax.ShapeDtypeStruct((t, c), dt) for c, dt in row_outs]
    out_shape += [jax.ShapeDtypeStruct(s, F32) for s in acc_outs]
    out_specs = [pl.BlockSpec((tm, c), lambda i: (i, 0)) for c, _ in row_outs]
    out_specs += [pl.BlockSpec(s, lambda i, nd=len(s): (0,) * nd) for s in acc_outs]
    return pl.pallas_call(body, out_shape=out_shape, grid=(t // tm,), in_specs=in_specs, out_specs=out_specs,
                          name=name, compiler_params=_params(("arbitrary",)))(*row_ins, *vec_ins)


def _rms_fwd(x, g):
    return _rows("rms_fwd", [x], [g], lambda r, v: ([_rms(r[0], v[0])], []), [(x.shape[1], BF16)])[0]


def _norm_out_bwd(name, f, d_out, g, scale):
    def fn(r, v):
        dx, dg = _rms_bwd(r[0], v[0], r[1] * scale)
        return [dx], [dg]

    c = f.shape[1]
    return _rows(name, [f, d_out], [g], fn, [(c, BF16)], [(1, c)])


def _norm_in_bwd(name, h, d_res, branches):
    nb = len(branches)

    def fn(r, v):
        dh, dgs = r[1], []
        for b in range(nb):
            dx, dg = _rms_bwd(r[0], v[b], r[2 + b])
            dh = dh + dx
            dgs.append(dg)
        return [dh], dgs

    c = h.shape[1]
    return _rows(name, [h, d_res] + [dn for _, dn in branches], [g for g, _ in branches], fn, [(c, F32)],
                 [(1, c)] * nb)


def _loss_head(y, target):
    d = y.shape[1]

    def fn(r, v):
        e = r[0] - r[1]
        s = jnp.sum(jnp.sum(e * e, axis=1, keepdims=True), axis=0, keepdims=True) * (0.5 / d)
        return [e * (1.0 / d)], [jnp.broadcast_to(s, (1, 128))]

    dy, acc = _rows("loss_head", [y, target], [], fn, [(d, F32)], [(1, 128)])
    return acc[0, 0], dy


def _rope_bwd(dk, cos2, sin2):
    c = dk.shape[1]
    return _rows("rope_bwd", [dk, cos2, sin2], [], lambda r, v: ([r[0] * r[1], r[0] * r[2]], []),
                 [(c, BF16), (c, BF16)])


def _ffn_up(n, wg, wu):
    t, d = n.shape
    fs = wg.shape[-1]
    tm = min(t, ROW_TILE)
    w_spec = ((None, d, fs), lambda s, i, k: (s, 0, 0))

    def epi(accs, ex):
        g, u = accs
        return [g, u, g * _sigmoid(g) * u]

    o_spec = ((None, tm, fs), lambda s, i, k: (s, i, 0))
    outs = [((N_SHARDS, t, fs), F32, *o_spec), ((N_SHARDS, t, fs), F32, *o_spec), ((N_SHARDS, t, fs), BF16, *o_spec)]
    return _mm("ffn_up", (N_SHARDS, t // tm, 1),
               [(n, (tm, d), lambda s, i, k: (i, 0)), (wg, *w_spec), (wu, *w_spec)],
               [(0, 1, 0, "nn"), (0, 2, 1, "nn")], [(tm, fs)] * 2, outs, epi)


def _down(name, a_in, w_in, gk, h, post_g, next_gs, scale, inner=0):
    t, d = h.shape
    tm = min(t, ROW_TILE)
    kn = next_gs.shape[0]

    def epi(accs, ex):
        f, hv, pg, ng = accs[0], ex[0], ex[1], ex[2]
        hn = hv + scale * _rms(f, pg)
        return [f, hn] + [_rms(hn, ng[q:q + 1]) for q in range(kn)]

    row = ((tm, d), lambda i, j, k: (i, 0))
    outs = [((t, d), F32, *row), ((t, d), F32, *row)] + [((t, d), BF16, *row)] * kn
    extras = [(h, *row), (post_g, (1, d), lambda i, j, k: (0, 0)), (next_gs, (kn, d), lambda i, j, k: (0, 0))]
    return _mm(name, (t // tm, 1, gk), [a_in, w_in], [(0, 1, 0, "nn")], [(tm, d)], outs, epi, extras, inner)


def _ffn_down(a, wd, h, post_g, next_gs):
    t, d = h.shape
    fs = a.shape[-1]
    tm = min(t, ROW_TILE)
    return _down("ffn_down", (a, (N_SHARDS, tm, fs), lambda i, _, k: (0, i, 0)),
                 (wd, (N_SHARDS, fs, d), lambda i, _, k: (0, 0, 0)), 1, h, post_g, next_gs, 0.5, N_SHARDS)


def _ffn_dact(df, wd, g, u):
    t, d = df.shape
    fs = g.shape[-1]
    tm = min(t, ROW_TILE)

    def epi(accs, ex):
        da, gv, uv = accs[0], ex[0], ex[1]
        sg = _sigmoid(gv)
        return [da * uv * (sg * (1.0 + gv * (1.0 - sg))), da * (gv * sg)]

    o_spec = ((None, tm, fs), lambda s, i, k: (s, i, 0))
    outs = [((N_SHARDS, t, fs), BF16, *o_spec)] * 2
    return _mm("ffn_dact", (N_SHARDS, t // tm, 1),
               [(df, (tm, d), lambda s, i, k: (i, 0)), (wd, (None, fs, d), lambda s, i, k: (s, 0, 0))],
               [(0, 1, 0, "nt")], [(tm, fs)], outs, epi, [(g, *o_spec), (u, *o_spec)])


def _ffn_dn(dg, du, wg, wu):
    _, t, fs = dg.shape
    d = wg.shape[-2]
    tm = min(t, ROW_TILE)
    a_spec = ((N_SHARDS, tm, fs), lambda i, _, k: (0, i, 0))
    w_spec = ((N_SHARDS, d, fs), lambda i, _, k: (0, 0, 0))
    outs = [((t, d), F32, (tm, d), lambda i, _, k: (i, 0))]
    return _mm("ffn_dn", (t // tm, 1, 1), [(dg, *a_spec), (wg, *w_spec), (du, *a_spec), (wu, *w_spec)],
               [(0, 1, 0, "nt"), (2, 3, 0, "nt")], [(tm, d)], outs, lambda accs, ex: accs, inner=N_SHARDS)[0]


def _ffn_dw_in(n, dg, du):
    _, t, fs = dg.shape
    d = n.shape[1]
    tk = min(t, K_TILE)
    b_spec = ((None, tk, fs), lambda s, _, k: (s, k, 0))
    o_spec = ((None, d, fs), lambda s, _, k: (s, 0, 0))
    outs = [((N_SHARDS, d, fs), BF16, *o_spec)] * 2
    return _mm("ffn_dw_in", (N_SHARDS, 1, t // tk), [(n, (tk, d), lambda s, _, k: (k, 0)), (dg, *b_spec), (du, *b_spec)],
               [(0, 1, 0, "tn"), (0, 2, 1, "tn")], [(d, fs)] * 2, outs, lambda accs, ex: accs)


def _ffn_dw_down(a, df):
    _, t, fs = a.shape
    d = df.shape[1]
    tk = min(t, K_TILE)
    outs = [((N_SHARDS, fs, d), BF16, (None, fs, d), lambda s, _, k: (s, 0, 0))]
    return _mm("ffn_dw_down", (N_SHARDS, 1, t // tk),
               [(a, (None, tk, fs), lambda s, _, k: (s, k, 0)), (df, (tk, d), lambda s, _, k: (k, 0))],
               [(0, 1, 0, "tn")], [(fs, d)], outs, lambda accs, ex: accs)[0]


def _causal_weight(w):
    row = lax.broadcasted_iota(jnp.int32, (CHUNK, CHUNK), 0)
    col = lax.broadcasted_iota(jnp.int32, (CHUNK, CHUNK), 1)
    return row >= col, jnp.where(row >= col, w, 0.0).astype(BF16)


def _layer_norm(v, g, b):
    xc = v - jnp.mean(v, axis=-1, keepdims=True)
    rstd = lax.rsqrt(jnp.mean(xc * xc, axis=-1, keepdims=True) + LN_EPS)
    xhat = xc * rstd
    return xhat, rstd, xhat * g + b


def _sgu_specs(t, half, tm):
    return [pl.BlockSpec((tm, half), lambda i: (i, 0)), pl.BlockSpec((tm, half), lambda i: (i, 1))]


def _sgu_fwd(zp, ln_g, ln_b, w_s, bsb):
    t, half = zp.shape[0], zp.shape[1] // 2
    tm = min(t, 2 * CHUNK)

    def body(u_ref, v_ref, g_ref, b_ref, w_ref, bs_ref, o_ref):
        u = _gelu(u_ref[...])
        _, _, vn = _layer_norm(_gelu(v_ref[...]), g_ref[...], b_ref[...])
        vb = vn.astype(BF16)
        for g in range(GROUPS):
            _, wm = _causal_weight(w_ref[g])
            cols = slice(g * CHUNK, (g + 1) * CHUNK)
            for c in range(tm // CHUNK):
                rows = slice(c * CHUNK, (c + 1) * CHUNK)
                sv = jnp.dot(wm, vb[rows, cols], preferred_element_type=F32) + bs_ref[g]
                o_ref[rows, cols] = (u[rows, cols] * sv).astype(BF16)

    whole = lambda a: pl.BlockSpec(a.shape, lambda i, nd=a.ndim: (0,) * nd)
    return pl.pallas_call(
        body, out_shape=jax.ShapeDtypeStruct((t, half), BF16), grid=(t // tm,),
        in_specs=_sgu_specs(t, half, tm) + [whole(ln_g), whole(ln_b), whole(w_s), whole(bsb)],
        out_specs=pl.BlockSpec((tm, half), lambda i: (i, 0)), name="sgu_fwd",
        compiler_params=_params(("arbitrary",)))(zp, zp, ln_g, ln_b, w_s, bsb)


def _sgu_bwd(zp, d_uv, ln_g, ln_b, w_s, bsb):
    t, half = zp.shape[0], zp.shape[1] // 2
    tm = min(t, 2 * CHUNK)

    def body(u_ref, v_ref, d_ref, g_ref, b_ref, w_ref, bs_ref, dz_ref, dlg_ref, dlb_ref, dws_ref, dbs_ref, dvn_ref):
        i = pl.program_id(0)

        @pl.when(i == 0)
        def _():
            dlg_ref[...] = jnp.zeros_like(dlg_ref)
            dlb_ref[...] = jnp.zeros_like(dlb_ref)
            dws_ref[...] = jnp.zeros_like(dws_ref)
            dbs_ref[...] = jnp.zeros_like(dbs_ref)

        up, vp = u_ref[...], v_ref[...]
        u, gup = _gelu(up), _gelu_grad(up)
        xhat, rstd, vn = _layer_norm(_gelu(vp), g_ref[...], b_ref[...])
        vb = vn.astype(BF16)
        d = d_ref[...]
        for g in range(GROUPS):
            mask, wm = _causal_weight(w_ref[g])
            cols = slice(g * CHUNK, (g + 1) * CHUNK)
            for c in range(tm // CHUNK):
                rows = slice(c * CHUNK, (c + 1) * CHUNK)
                blk = vb[rows, cols]
                sv = jnp.dot(wm, blk, preferred_element_type=F32) + bs_ref[g]
                dblk = d[rows, cols]
                dz_ref[rows, cols] = (dblk * sv * gup[rows, cols]).astype(BF16)
                dsv = dblk * u[rows, cols]
                dsvb = dsv.astype(BF16)
                dvn_ref[rows, cols] = lax.dot_general(wm, dsvb, _DN["tn"], preferred_element_type=F32)
                dw = lax.dot_general(dsvb, blk, _DN["nt"], preferred_element_type=F32)
                dws_ref[g] += jnp.where(mask, dw, 0.0)
                dbs_ref[g] += jnp.sum(dsv, axis=1, keepdims=True)
        dvn = dvn_ref[...]
        dlg_ref[...] += jnp.sum(dvn * xhat, axis=0, keepdims=True)
        dlb_ref[...] += jnp.sum(dvn, axis=0, keepdims=True)
        dxh = dvn * g_ref[...]
        dv = rstd * (dxh - jnp.mean(dxh, axis=-1, keepdims=True)
                     - xhat * jnp.mean(dxh * xhat, axis=-1, keepdims=True))
        dz_ref[:, half:] = (dv * _gelu_grad(vp)).astype(BF16)

    whole = lambda a: pl.BlockSpec(a.shape, lambda i, nd=a.ndim: (0,) * nd)
    wshape = lambda s: pl.BlockSpec(s, lambda i, nd=len(s): (0,) * nd)
    out_shape = [jax.ShapeDtypeStruct((t, 2 * half), BF16), jax.ShapeDtypeStruct((1, half), F32),
                 jax.ShapeDtypeStruct((1, half), F32), jax.ShapeDtypeStruct(w_s.shape, F32),
                 jax.ShapeDtypeStruct((GROUPS, CHUNK, 1), F32)]
    return pl.pallas_call(
        body, out_shape=out_shape, grid=(t // tm,),
        in_specs=_sgu_specs(t, half, tm) + [pl.BlockSpec((tm, half), lambda i: (i, 0)), whole(ln_g), whole(ln_b),
                                            whole(w_s), whole(bsb)],
        out_specs=[pl.BlockSpec((tm, 2 * half), lambda i: (i, 0)), wshape((1, half)), wshape((1, half)),
                   wshape(w_s.shape), wshape((GROUPS, CHUNK, 1))],
        scratch_shapes=[pltpu.VMEM((tm, half), F32)], name="sgu_bwd",
        compiler_params=_params(("arbitrary",)))(zp, zp, d_uv, ln_g, ln_b, w_s, bsb)


_SCALE = (QK_NOPE + QK_ROPE) ** -0.5


def _attn_scores(qn, qr, kn, kr, i, tq, n):
    s = lax.dot_general(qn, kn, _DN["nt"], preferred_element_type=F32)
    s = (s + lax.dot_general(qr, kr, _DN["nt"], preferred_element_type=F32)) * _SCALE
    row = i * tq + lax.broadcasted_iota(jnp.int32, (tq, n), 0)
    col = lax.broadcasted_iota(jnp.int32, (tq, n), 1)
    return jnp.where(col <= row, s, NEG_INF)


def _attn_specs(seq):
    head = lambda b, h: (b, h)
    return dict(
        qn=pl.BlockSpec((seq, QK_NOPE), head),
        qr=pl.BlockSpec((None, seq, QK_ROPE), lambda b, h: (h, b, 0)),
        kr=pl.BlockSpec((seq, QK_ROPE), lambda b, h: (b, 0)),
        lse=pl.BlockSpec((None, seq, 1), lambda b, h: (h, b, 0)),
    )


def _attn_fwd(qn, qr, kn, v, kr, seq):
    t = qn.shape[0]
    tq = min(seq, 2 * CHUNK)
    sp = _attn_specs(seq)

    def body(qn_ref, qr_ref, kn_ref, v_ref, kr_ref, o_ref, lse_ref):
        for i in range(seq // tq):
            rows, n = slice(i * tq, (i + 1) * tq), (i + 1) * tq
            s = _attn_scores(qn_ref[rows, :], qr_ref[rows, :], kn_ref[0:n, :], kr_ref[0:n, :], i, tq, n)
            m = jnp.max(s, axis=-1, keepdims=True)
            p = jnp.exp(s - m)
            l = jnp.sum(p, axis=-1, keepdims=True)
            o_ref[rows, :] = jnp.dot((p / l).astype(BF16), v_ref[0:n, :], preferred_element_type=F32).astype(BF16)
            lse_ref[rows, :] = m + jnp.log(l)

    return pl.pallas_call(
        body, out_shape=[jax.ShapeDtypeStruct((t, N_HEADS * V_DIM), BF16), jax.ShapeDtypeStruct((N_HEADS, t, 1), F32)],
        grid=(t // seq, N_HEADS), in_specs=[sp["qn"], sp["qr"], sp["qn"], sp["qn"], sp["kr"]],
        out_specs=[sp["qn"], sp["lse"]], name="attn_fwd",
        compiler_params=_params(("parallel", "arbitrary")))(qn, qr, kn, v, kr)


def _attn_bwd(qn, qr, kn, v, kr, do, lse, cos2, sin2, seq):
    t = qn.shape[0]
    tq = min(seq, 2 * CHUNK)
    sp = _attn_specs(seq)

    def body(qn_ref, qr_ref, kn_ref, v_ref, kr_ref, do_ref, lse_ref, cos_ref, sin_ref,
             dqn_ref, dkn_ref, dv_ref, dqc_ref, dqs_ref, dkr_ref, dk_acc, dv_acc, dkr_acc):
        dk_acc[...] = jnp.zeros_like(dk_acc)
        dv_acc[...] = jnp.zeros_like(dv_acc)
        dkr_acc[...] = jnp.zeros_like(dkr_acc)
        for i in range(seq // tq):
            rows, n = slice(i * tq, (i + 1) * tq), (i + 1) * tq
            q_n, q_r, d_o = qn_ref[rows, :], qr_ref[rows, :], do_ref[rows, :]
            k_n, k_r = kn_ref[0:n, :], kr_ref[0:n, :]
            s = _attn_scores(q_n, q_r, k_n, k_r, i, tq, n)
            p = jnp.exp(s - lse_ref[rows, :])
            dp = lax.dot_general(d_o, v_ref[0:n, :], _DN["nt"], preferred_element_type=F32)
            ds = (p * (dp - jnp.sum(p * dp, axis=-1, keepdims=True)) * _SCALE).astype(BF16)
            dqn_ref[rows, :] = jnp.dot(ds, k_n, preferred_element_type=F32).astype(BF16)
            dqr = jnp.dot(ds, k_r, preferred_element_type=F32)
            dqc_ref[rows, :] = (dqr * cos_ref[rows, :]).astype(BF16)
            dqs_ref[rows, :] = (dqr * sin_ref[rows, :]).astype(BF16)
            dk_acc[0:n, :] += lax.dot_general(ds, q_n, _DN["tn"], preferred_element_type=F32)
            dkr_acc[0:n, :] += lax.dot_general(ds, q_r, _DN["tn"], preferred_element_type=F32)
            dv_acc[0:n, :] += lax.dot_general(p.astype(BF16), d_o, _DN["tn"], preferred_element_type=F32)
        dkn_ref[...] = dk_acc[...].astype(BF16)
        dv_ref[...] = dv_acc[...].astype(BF16)
        h = pl.program_id(1)

        @pl.when(h == 0)
        def _():
            dkr_ref[...] = dkr_acc[...]

        @pl.when(h > 0)
        def _():
            dkr_ref[...] += dkr_acc[...]

    wide = jax.ShapeDtypeStruct((t, N_HEADS * V_DIM), BF16)
    rope = jax.ShapeDtypeStruct((N_HEADS, t, QK_ROPE), BF16)
    krf = pl.BlockSpec((seq, QK_ROPE), lambda b, h: (b, 0))
    return pl.pallas_call(
        body, out_shape=[wide, wide, wide, rope, rope, jax.ShapeDtypeStruct((t, QK_ROPE), F32)],
        grid=(t // seq, N_HEADS),
        in_specs=[sp["qn"], sp["qr"], sp["qn"], sp["qn"], sp["kr"], sp["qn"], sp["lse"], krf, krf],
        out_specs=[sp["qn"], sp["qn"], sp["qn"], sp["qr"], sp["qr"], krf],
        scratch_shapes=[pltpu.VMEM((seq, QK_NOPE), F32), pltpu.VMEM((seq, V_DIM), F32), pltpu.VMEM((seq, QK_ROPE), F32)],
        name="attn_bwd", compiler_params=_params(("parallel", "arbitrary")))(qn, qr, kn, v, kr, do, lse, cos2, sin2)


def _row_tile(rows, cols, row_mult=8):
    cap = max(row_mult, (1 << 18) // cols)
    best = rows
    for tr in range(row_mult, min(rows, cap) + 1, row_mult):
        if rows % tr == 0:
            best = tr
    return best if rows > cap else rows


def _adamw_math(w, g, m, v):
    mv = ADAM_B1 * m + (1.0 - ADAM_B1) * g
    vv = ADAM_B2 * v + (1.0 - ADAM_B2) * (g * g)
    m_hat = mv / (1.0 - ADAM_B1 ** ADAM_STEP)
    v_hat = vv / (1.0 - ADAM_B2 ** ADAM_STEP)
    return -ADAM_LR * (m_hat / (jnp.sqrt(v_hat) + ADAM_EPS) + ADAM_WD * w), mv, vv


def _adamw(w, g, m, v):
    shape = w.shape
    c = shape[-1]
    r = w.size // c
    tr = _row_tile(r, c)

    def body(w_ref, g_ref, m_ref, v_ref, d_ref, nm_ref, nv_ref):
        d_ref[...], nm_ref[...], nv_ref[...] = _adamw_math(w_ref[...], g_ref[...], m_ref[...], v_ref[...])

    spec = pl.BlockSpec((tr, c), lambda i: (i, 0))
    outs = pl.pallas_call(body, out_shape=[jax.ShapeDtypeStruct((r, c), F32)] * 3, grid=(r // tr,),
                          in_specs=[spec] * 4, out_specs=[spec] * 3, name="adamw",
                          compiler_params=_params(("parallel",)))(*[a.reshape(r, c) for a in (w, g, m, v)])
    return [o.reshape(shape) for o in outs]


def _adamw_halves(w, m, v, l, j, own, recv, core, prev):
    nl, nj, rows, c = w.shape
    r = rows // 2
    tr = _row_tile(r, c)
    n_prev = 0 if prev is None else 4

    def body(core_ref, w_ref, own_ref, recv_ref, m_ref, v_ref, *rest):
        g_ref, d_ref, nm_ref, nv_ref = rest[n_prev:]
        g = jnp.where(pl.program_id(0) == core_ref[0], own_ref[...], recv_ref[...])
        g_ref[...] = g
        d_ref[...], nm_ref[...], nv_ref[...] = _adamw_math(w_ref[...], g, m_ref[...], v_ref[...])

    slab = pl.BlockSpec((None, None, None, tr, c), lambda h, i, cr: (l, j, h, i, 0))
    half = pl.BlockSpec((tr, c), lambda h, i, cr: (i, 0))
    grid_spec = pltpu.PrefetchScalarGridSpec(num_scalar_prefetch=1, grid=(2, r // tr),
                                             in_specs=[slab, half, half, slab, slab] + [_ANY] * n_prev,
                                             out_specs=[slab] * 4)
    split = lambda a: a.reshape(nl, nj, 2, r, c)
    return pl.pallas_call(body, out_shape=[jax.ShapeDtypeStruct((nl, nj, 2, r, c), F32)] * 4, grid_spec=grid_spec,
                          input_output_aliases={6 + q: q for q in range(n_prev)}, name="adamw_halves",
                          compiler_params=_params(("parallel",) * 2))(
                              core, split(w), own, recv, split(m), split(v), *(prev or ()))


def _place():
    x, y, c = lax.axis_index("x"), lax.axis_index("y"), lax.axis_index("c")
    return x, y, c, [(1 - x, y), (x, 1 - y), (1 - x, 1 - y)]


def _dma_sems(*counts):
    return [pltpu.SemaphoreType.DMA((n,)) for n in counts]


def _all_gather(bufs, collective_id, name):
    n = len(bufs)

    def body(*refs):
        ins, outs = refs[:n], refs[n:2 * n]
        send, recv, fsend, frecv, osend, orecv = refs[2 * n:]
        x, y, c, _ = _place()
        xn, yn, sib = (1 - x, y, c), (x, 1 - y, c), (x, y, 1 - c)
        k, kx, ky, kd = 2 * x + y, 2 * (1 - x) + y, 2 * x + 1 - y, 2 * (1 - x) + 1 - y
        _handshake([xn, yn, sib])

        def copy(src, dst, sems, i, to):
            return pltpu.make_async_remote_copy(src, dst, sems[0].at[i], sems[1].at[i], device_id=to, device_id_type=_MESH)

        ici, d2d, own_s = (send, recv), (fsend, frecv), (osend, orecv)
        started = [copy(ins[b], outs[b].at[k], own_s, b, sib) for b in range(n)]
        for first in (True, False):
            for b in range(n):
                mine = outs[b].at[k, c]
                if first:
                    started += [copy(ins[b].at[c, 0], mine.at[0], ici, 6 * b, xn), copy(ins[b].at[c, 1], mine.at[1], ici, 6 * b + 1, yn)]
                else:
                    started += [copy(ins[b].at[c, 1], mine.at[1], ici, 6 * b + 2, xn), copy(ins[b].at[c, 0], mine.at[0], ici, 6 * b + 3, yn)]
        for cp in started:
            cp.start()
        passed = []
        for b in range(n):
            for i, (src_chip, q, to) in enumerate([(kx, 0, yn), (ky, 1, xn)]):
                piece = outs[b].at[src_chip, c, q]
                copy(piece, piece, ici, 6 * b + i, to).wait_recv()
                cp = copy(piece, piece, ici, 6 * b + 4 + i, to)
                cp.start()
                passed.append(cp)
        for b in range(n):
            for i, (src_chip, q) in enumerate([(kx, 1), (ky, 0)]):
                piece = outs[b].at[src_chip, c, q]
                copy(piece, piece, ici, 6 * b + 2 + i, xn).wait_recv()
                half = outs[b].at[src_chip, c]
                cp = copy(half, half, d2d, 3 * b + i, sib)
                cp.start()
                passed.append(cp)
        for b in range(n):
            for i, q in enumerate([0, 1]):
                piece = outs[b].at[kd, c, q]
                copy(piece, piece, ici, 6 * b + 4 + i, xn).wait_recv()
            half = outs[b].at[kd, c]
            cp = copy(half, half, d2d, 3 * b + 2, sib)
            cp.start()
            passed.append(cp)
        for b in range(n):
            for i, src_chip in enumerate([kx, ky, kd]):
                half = outs[b].at[src_chip, 1 - c]
                copy(half, half, d2d, 3 * b + i, sib).wait_recv()
        for cp in started[n:] + passed:
            cp.wait_send()
        for cp in started[:n]:
            cp.wait()

    return _sequencer(body, [jax.ShapeDtypeStruct((N_SHARDS,) + b.shape, b.dtype) for b in bufs],
                      _dma_sems(6 * n, 6 * n, 3 * n, 3 * n, n, n), collective_id, name, bufs)


def _sequencer(body, out_type, sems, collective_id, name, args):
    return pl.kernel(body, out_type=out_type, mesh=plsc.ScalarSubcoreMesh(axis_name="sequencer", num_cores=1),
                     scratch_types=sems, compiler_params=pltpu.CompilerParams(collective_id=collective_id),
                     name=name)(*args)


def _handshake(peers):
    barrier = pltpu.get_barrier_semaphore()
    for peer in peers:
        pl.semaphore_signal(barrier, inc=1, device_id=peer, device_id_type=_MESH)
    pl.semaphore_wait(barrier, len(peers))


def _swap_halves(parts, collective_id, name):
    n = len(parts)

    def body(*refs):
        ins, outs = refs[:n], refs[n:2 * n]
        send, recv = refs[2 * n:]
        x, y, c, _ = _place()
        _handshake([(x, y, 1 - c)])
        cps = [pltpu.make_async_remote_copy(ins[b].at[:, pl.ds(1 - c, 1)], outs[b], send.at[b], recv.at[b],
                                            device_id=(x, y, 1 - c), device_id_type=_MESH) for b in range(n)]
        for cp in cps:
            cp.start()
        for cp in cps:
            cp.wait()

    return _sequencer(body, [jax.ShapeDtypeStruct((N_SHARDS, 1) + p.shape[2:], p.dtype) for p in parts],
                      _dma_sems(n, n), collective_id, name, parts)


def _add_half(part, other, core):
    _, _, r, c = part.shape
    tr = _row_tile(r, c, 16)

    def body(core_ref, p_ref, o_ref, out_ref):
        out_ref[...] = (p_ref[...].astype(F32) + o_ref[...].astype(F32)).astype(out_ref.dtype)

    grid_spec = pltpu.PrefetchScalarGridSpec(
        num_scalar_prefetch=1, grid=(N_SHARDS, r // tr),
        in_specs=[pl.BlockSpec((None, None, tr, c), lambda k, i, cr: (k, cr[0], i, 0)),
                  pl.BlockSpec((None, None, tr, c), lambda k, i, cr: (k, 0, i, 0))],
        out_specs=pl.BlockSpec((None, tr, c), lambda k, i, cr: (k, i, 0)))
    return pl.pallas_call(body, out_shape=jax.ShapeDtypeStruct((N_SHARDS, r, c), part.dtype), grid_spec=grid_spec,
                          name="grad_add_half", compiler_params=_params(("parallel", "parallel")))(core, part, other)


def _scatter_chips(parts, collective_id, name):
    n = len(parts)

    def body(*refs):
        ins, outs = refs[:n], refs[n:2 * n]
        send, recv = refs[2 * n:]
        x, y, c, chips = _place()
        k = 2 * x + y
        _handshake([(px, py, c) for px, py in chips])
        started = []
        for b in range(n):
            for j, (px, py) in enumerate(chips):
                cp = pltpu.make_async_remote_copy(ins[b].at[2 * px + py], outs[b].at[k], send.at[3 * b + j],
                                                  recv.at[3 * b + j], device_id=(px, py, c), device_id_type=_MESH)
                cp.start()
                started.append(cp)
        for b in range(n):
            for j, (px, py) in enumerate(chips):
                got = outs[b].at[2 * px + py]
                pltpu.make_async_remote_copy(got, got, send.at[3 * b + j], recv.at[3 * b + j],
                                             device_id=(px, py, c), device_id_type=_MESH).wait_recv()
        for cp in started:
            cp.wait_send()

    return _sequencer(body, [jax.ShapeDtypeStruct(p.shape, p.dtype) for p in parts], _dma_sems(3 * n, 3 * n),
                      collective_id, name, parts)


def _sum_slots(slots, mine, chip):
    _, r, c = slots.shape
    tr = _row_tile(r, c, 16)

    def body(chip_ref, s0, s1, s2, s3, own_ref, out_ref):
        own = own_ref[...].astype(F32)
        v = [jnp.where(chip_ref[0] == s, own, ref[...].astype(F32)) for s, ref in enumerate((s0, s1, s2, s3))]
        out_ref[...] = ((v[0] + v[1]) + v[2]) + v[3]

    def slot_spec(s):
        return pl.BlockSpec((None, tr, c), lambda i, kr: (jnp.where(kr[0] == s, (s + 1) % N_SHARDS, s), i, 0))

    grid_spec = pltpu.PrefetchScalarGridSpec(
        num_scalar_prefetch=1, grid=(r // tr,),
        in_specs=[slot_spec(s) for s in range(N_SHARDS)] + [pl.BlockSpec((None, tr, c), lambda i, kr: (kr[0], i, 0))],
        out_specs=pl.BlockSpec((tr, c), lambda i, kr: (i, 0)))
    return pl.pallas_call(body, out_shape=jax.ShapeDtypeStruct((r, c), F32), grid_spec=grid_spec, name="grad_sum_slots",
                          compiler_params=_params(("parallel",)))(chip, slots, slots, slots, slots, mine)


def _join_halves(halves, collective_id, name):
    n = len(halves)

    def body(*refs):
        ins, outs = refs[:n], refs[n:2 * n]
        send, recv = refs[2 * n:]
        x, y, c, _ = _place()
        _handshake([(x, y, 1 - c)])
        cps = [pltpu.make_async_remote_copy(ins[b], outs[b], send.at[b], recv.at[b], device_id=(x, y, 1 - c),
                                            device_id_type=_MESH) for b in range(n)]
        for cp in cps:
            cp.start()
        for cp in cps:
            cp.wait()

    return _sequencer(body, [jax.ShapeDtypeStruct(h.shape, F32) for h in halves], _dma_sems(n, n), collective_id,
                      name, halves)


def _gather_rows(buf, start, rows):
    def body(in_ref, out_ref, send, recv, lsem):
        x, y, c, chips = _place()
        k = 2 * x + y
        src = in_ref.at[pl.ds(start, rows)]
        local = pltpu.make_async_remote_copy(src, out_ref.at[k], lsem.at[0], lsem.at[1], device_id=(x, y, 1 - c),
                                             device_id_type=_MESH)
        local.start()
        cps = [pltpu.make_async_remote_copy(src, out_ref.at[k], send.at[j], recv.at[j], device_id=(px, py, c),
                                            device_id_type=_MESH) for j, (px, py) in enumerate(chips)]
        for cp in cps:
            cp.start()
        for j, (px, py) in enumerate(chips):
            got = out_ref.at[2 * px + py]
            pltpu.make_async_remote_copy(got, got, send.at[j], recv.at[j], device_id=(px, py, c),
                                         device_id_type=_MESH).wait_recv()
        for cp in cps:
            cp.wait_send()
        local.wait()

    return pl.pallas_call(body, out_shape=jax.ShapeDtypeStruct((N_SHARDS, rows, buf.shape[1]), F32),
                          in_specs=[_ANY], out_specs=_ANY, scratch_shapes=_dma_sems(3, 3, 2),
                          name="gather_replicated_grads")(buf)


def _all_sum(vec):
    r, c = vec.shape
    n_dev = 2 * N_SHARDS

    def body(in_ref, out_ref, slots, send, recv):
        x, y, cc, _ = _place()
        flip = lambda v, bit: 1 - v if bit else v
        peers = [(flip(x, (q >> 2) & 1), flip(y, (q >> 1) & 1), flip(cc, q & 1)) for q in range(1, n_dev)]
        index = lambda p: 4 * p[0] + 2 * p[1] + p[2]
        slots[index((x, y, cc))] = in_ref[...]
        cps = [pltpu.make_async_remote_copy(in_ref, slots.at[index((x, y, cc))], send.at[q], recv.at[q], device_id=p,
                                            device_id_type=_MESH) for q, p in enumerate(peers)]
        for cp in cps:
            cp.start()
        for q, p in enumerate(peers):
            got = slots.at[index(p)]
            pltpu.make_async_remote_copy(got, got, send.at[q], recv.at[q], device_id=p, device_id_type=_MESH).wait_recv()
        for cp in cps:
            cp.wait_send()
        acc = slots[0]
        for s in range(1, n_dev):
            acc = acc + slots[s]
        out_ref[...] = acc

    vmem = pl.BlockSpec(memory_space=pltpu.VMEM)
    return pl.pallas_call(body, out_shape=jax.ShapeDtypeStruct((r, c), F32), in_specs=[vmem], out_specs=vmem,
                          scratch_shapes=[pltpu.VMEM((n_dev, r, c), F32)] + _dma_sems(n_dev - 1, n_dev - 1),
                          name="sum_small_grads")(vec)


def _not_before(value, other):
    return lax.optimization_barrier((value, other))[0]


def _round_up(n, m):
    return -(-n // m) * m


def _pack_flat(vecs, rows, width, dtype):
    flat = jnp.concatenate([v.reshape(-1).astype(dtype) for v in vecs])
    return jnp.pad(flat, (0, rows * width - flat.size)).reshape(rows, width)


def _split_flat(flat, shapes):
    out, off = [], 0
    for s in shapes:
        n = math.prod(s)
        out.append(flat[off:off + n].reshape(s))
        off += n
    return out


def _merge_shards(arr4, axis):
    a = jnp.moveaxis(arr4, 0, axis)
    s = list(a.shape)
    return a.reshape(s[:axis] + [s[axis] * s[axis + 1]] + s[axis + 2:])


def _split_shards(full, axis):
    s = list(full.shape)
    a = full.reshape(s[:axis] + [N_SHARDS, s[axis] // N_SHARDS] + s[axis + 1:])
    return jnp.moveaxis(a, axis, 0).reshape(N_SHARDS, -1)


def _rot_cols(w):
    half = w.shape[-1] // 2
    return jnp.concatenate([-w[..., half:], w[..., :half]], axis=-1)


def _unrot_cols(dw):
    half = dw.shape[-1] // 2
    return jnp.concatenate([dw[..., half:], -dw[..., :half]], axis=-1)


def kernel(x, positions, ffn_pre_g, ffn_post_g, ffn_w_gate, ffn_w_up, ffn_w_down, mix_pre_g, mix_post_g, gmlp_w_in, gmlp_ln_g, gmlp_ln_b, gmlp_w_s, gmlp_b_s, gmlp_w_out, kv_norm_g, w_dkv, kv_a_norm_g, w_ukv, mla_w_dq, mla_q_norm_g, mla_w_uq, mla_w_o, loss_target, m_ffn_pre_g, m_ffn_post_g, m_ffn_w_gate, m_ffn_w_up, m_ffn_w_down, m_mix_pre_g, m_mix_post_g, m_gmlp_w_in, m_gmlp_ln_g, m_gmlp_ln_b, m_gmlp_w_s, m_gmlp_b_s, m_gmlp_w_out, m_kv_norm_g, m_w_dkv, m_kv_a_norm_g, m_w_ukv, m_mla_w_dq, m_mla_q_norm_g, m_mla_w_uq, m_mla_w_o, v_ffn_pre_g, v_ffn_post_g, v_ffn_w_gate, v_ffn_w_up, v_ffn_w_down, v_mix_pre_g, v_mix_post_g, v_gmlp_w_in, v_gmlp_ln_g, v_gmlp_ln_b, v_gmlp_w_s, v_gmlp_b_s, v_gmlp_w_out, v_kv_norm_g, v_w_dkv, v_kv_a_norm_g, v_w_ukv, v_mla_w_dq, v_mla_q_norm_g, v_mla_w_uq, v_mla_w_o):
    names = ["ffn_pre_g", "ffn_post_g", "ffn_w_gate", "ffn_w_up", "ffn_w_down", "mix_pre_g", "mix_post_g", "gmlp_w_in",
             "gmlp_ln_g", "gmlp_ln_b", "gmlp_w_s", "gmlp_b_s", "gmlp_w_out", "kv_norm_g", "w_dkv", "kv_a_norm_g", "w_ukv",
             "mla_w_dq", "mla_q_norm_g", "mla_w_uq", "mla_w_o"]
    env = locals()
    w = {n: env[n] for n in names}
    mom = {n: env["m_" + n] for n in names}
    var = {n: env["v_" + n] for n in names}

    bsz, seq, d = x.shape
    t = bsz * seq
    core = lax.axis_index("c").astype(jnp.int32).reshape(1)

    mats = [("gmlp_w_in", 2), ("gmlp_w_out", 1), ("w_dkv", 0), ("w_ukv", 1), ("mla_w_dq", 1), ("mla_w_uq", 2),
            ("mla_w_o", 1)]
    vecs = [("ffn_pre_g", 2), ("ffn_post_g", 2), ("gmlp_ln_g", 1), ("gmlp_ln_b", 1)]
    replicated = ["mix_pre_g", "mix_post_g", "gmlp_w_s", "gmlp_b_s", "kv_norm_g", "kv_a_norm_g", "mla_q_norm_g"]
    n_mats = sum(w[n].size for n, _ in mats)
    n_vecs = sum(w[n].size for n, _ in vecs)
    mat_rows = _round_up(-(-n_mats // PACK_WIDTH), 64)
    vec_rows = _round_up(-(-n_vecs // 128), 32)
    mat_pack = _pack_flat([w[n] for n, _ in mats], mat_rows, PACK_WIDTH, BF16).reshape(2, 2, mat_rows // 4, PACK_WIDTH)
    vec_pack = _pack_flat([w[n] for n, _ in vecs], vec_rows, 128, F32).reshape(2, 2, vec_rows // 4, 128)
    ffn_names = ("ffn_w_gate", "ffn_w_up", "ffn_w_down")
    lj = [(l, j) for l in range(2) for j in range(2)]
    riders = {(0, 0): [vec_pack], (0, 1): [mat_pack], (1, 0): [], (1, 1): []}
    ffn_w = {}
    landed = []
    for q, (l, j) in enumerate(lj):
        shards = [w[n][l, j].astype(BF16) for n in ffn_names]
        bufs = [s.reshape(2, 2, s.shape[0] // 4, s.shape[1]) for s in shards] + riders[(l, j)]
        if q > 0:
            bufs = _not_before(bufs, landed[0 if q < 3 else 1])
        got = _all_gather(bufs, q + 1, f"gather_weights_{q}")
        landed.append(got[-1])
        ffn_w[(l, j)] = [g.reshape((N_SHARDS,) + s.shape) for g, s in zip(got, shards)]
        if (l, j) == (0, 0):
            vec_all = got[3]
        if (l, j) == (0, 1):
            mat_all = got[3]

    def unpack(packed, entries):
        flat4, off, out = packed.reshape(N_SHARDS, -1), 0, {}
        for n, ax in entries:
            out[n] = _merge_shards(flat4[:, off:off + w[n].size].reshape((N_SHARDS,) + w[n].shape), ax)
            off += w[n].size
        return out

    full = unpack(vec_all, vecs)
    ln_g, ln_b = full["gmlp_ln_g"], full["gmlp_ln_b"]
    pre_g, post_g = full["ffn_pre_g"], full["ffn_post_g"]
    w_s = w["gmlp_w_s"][0]
    bsb = jnp.broadcast_to(w["gmlp_b_s"][0][:, :, None], (GROUPS, CHUNK, CHUNK))
    row = lambda v: v.reshape(1, -1)

    inv_freq = ROPE_THETA ** (-jnp.arange(0, QK_ROPE, 2, dtype=F32) / QK_ROPE)
    ang = positions.astype(F32).reshape(t, 1) * inv_freq
    cos2 = jnp.concatenate([jnp.cos(ang)] * 2, axis=-1)
    sin2 = jnp.concatenate([jnp.sin(ang)] * 2, axis=-1)
    cos_h, sin_h = jnp.tile(cos2, (1, N_HEADS)), jnp.tile(sin2, (1, N_HEADS))

    def rope_epi(n_lin):
        def epi(accs, ex):
            return accs[:n_lin] + [accs[n_lin] * ex[0] + accs[n_lin + 1] * ex[1]]
        return epi

    h0 = x.reshape(t, d)
    saved = {}

    def ffn_fwd(l, j, h, n, next_gs):
        wg, wu, wd = ffn_w[(l, j)]
        g, u, a = _ffn_up(n, wg, wu)
        f, h_new, *n_next = _ffn_down(a, wd, h, row(post_g[l, j]), next_gs)
        saved[("ffn", l, j)] = (h, n, g, u, a, f)
        return h_new, n_next

    n0 = _rms_fwd(h0, row(pre_g[0, 0]))
    h1, (n1,) = ffn_fwd(0, 0, h0, n0, row(w["mix_pre_g"][0]))

    full.update(unpack(_not_before(mat_all, h1), mats))
    w_in, w_out = full["gmlp_w_in"][0], full["gmlp_w_out"][0]
    w_c, w_kr = full["w_dkv"][:, :KV_RANK], full["w_dkv"][:, KV_RANK:]
    w_kr_rot = _rot_cols(w_kr)
    ukv = full["w_ukv"].reshape(KV_RANK, N_HEADS, 2, QK_NOPE)
    w_k, w_v = ukv[:, :, 0].reshape(KV_RANK, -1), ukv[:, :, 1].reshape(KV_RANK, -1)
    w_dq, w_o = full["mla_w_dq"][0], full["mla_w_o"][0]
    q_rank = w_dq.shape[1]
    uq = full["mla_w_uq"][0].reshape(q_rank, N_HEADS, QK_NOPE + QK_ROPE)
    w_qn = uq[:, :, :QK_NOPE].reshape(q_rank, -1)
    w_qr = uq[:, :, QK_NOPE:].reshape(q_rank, -1)
    w_qr_rot = _rot_cols(uq[:, :, QK_NOPE:]).reshape(q_rank, -1)

    zp = _mm2d("gmlp_in", [(n1, w_in, "nn", 0)], [(w_in.shape[1], F32)])[0]
    uv = _sgu_fwd(zp, ln_g, ln_b, w_s, bsb)
    half = uv.shape[1]
    tm = min(t, ROW_TILE)
    m0, h2, n2 = _down("gmlp_out", (uv, (tm, 512), lambda i, _, k: (i, k)), (w_out, (512, d), lambda i, _, k: (k, 0)),
                       half // 512, h1, row(w["mix_post_g"][0]), row(pre_g[0, 1]), 1.0)
    h3, (n3kv, n3) = ffn_fwd(0, 1, h2, n2, jnp.stack([w["kv_norm_g"], pre_g[1, 0]]))

    def kv_epi(accs, ex):
        c_raw = accs[0]
        return [c_raw, _rms(c_raw, ex[2]), accs[1] * ex[0] + accs[2] * ex[1]]

    c_raw, c_n, k_r = _mm2d("kv_down", [(n3kv, w_c, "nn", 0), (n3kv, w_kr, "nn", 1), (n3kv, w_kr_rot, "nn", 2)],
                            [(KV_RANK, F32), (KV_RANK, BF16), (QK_ROPE, BF16)], kv_epi, [cos2, sin2],
                            [row(w["kv_a_norm_g"])])
    k_n, v_h = _mm2d("kv_up", [(c_n, w_k, "nn", 0), (c_n, w_v, "nn", 1)], [(w_k.shape[1], BF16), (w_v.shape[1], BF16)])

    h4, (n4,) = ffn_fwd(1, 0, h3, n3, row(w["mix_pre_g"][1]))
    qd, qn = _mm2d("q_down", [(n4, w_dq, "nn", 0)], [(q_rank, F32), (q_rank, BF16)],
                   lambda accs, ex: [accs[0], _rms(accs[0], ex[0])], [], [row(w["mla_q_norm_g"][0])])
    q_n, q_r = _mm2d("q_up", [(qn, w_qn, "nn", 0), (qn, w_qr, "nn", 1), (qn, w_qr_rot, "nn", 2)],
                     [(w_qn.shape[1], BF16), (w_qr.shape[1], BF16)], rope_epi(1), [cos_h, sin_h])
    q_r = q_r.reshape(t, N_HEADS, QK_ROPE).transpose(1, 0, 2)
    o, lse = _attn_fwd(q_n, q_r, k_n, v_h, k_r, seq)
    m1, h5, n5 = _down("attn_out", (o, (tm, 512), lambda i, _, k: (i, k)), (w_o, (512, d), lambda i, _, k: (k, 0)),
                       o.shape[1] // 512, h4, row(w["mix_post_g"][1]), row(pre_g[1, 1]), 1.0)
    y, _ = ffn_fwd(1, 1, h5, n5, row(pre_g[1, 1]))

    loss_part, dy = _loss_head(y, loss_target.reshape(t, d))
    loss = lax.psum(loss_part, ("x", "y", "c"))

    chip = (2 * lax.axis_index("x") + lax.axis_index("y")).astype(jnp.int32).reshape(1)
    rs = {}

    def rs_launch(gid, parts):
        rs[gid] = {"parts": parts, "others": _swap_halves(parts, 5 + gid, f"grad_swap_{gid}")}

    def rs_mid(gid, after):
        r = rs[gid]
        parts, others = _not_before((r["parts"], r["others"]), after)
        r["chip"] = [_add_half(p, o, core) for p, o in zip(parts, others)]
        r["slots"] = _scatter_chips(r["chip"], 10 + gid, f"grad_scatter_{gid}")
        return r["chip"]

    def rs_end(gid, after):
        r = rs[gid]
        slots, mine = _not_before((r["slots"], r["chip"]), after)
        r["own"] = [_sum_slots(s, p, chip) for s, p in zip(slots, mine)]
        r["recv"] = _join_halves(r["own"], 15 + gid, f"grad_join_{gid}")
        return r["own"]

    d_pre, d_post = {}, {}

    def ffn_bwd(l, j, gid, dh_out, extra=(), after_post=None):
        h, n, g, u, a, f = saved[("ffn", l, j)]
        df, d_post[(l, j)] = _norm_out_bwd("ffn_post_bwd", f, dh_out, row(post_g[l, j]), 0.5)
        if after_post is not None:
            df = after_post(df)
        wg, wu, wd = ffn_w[(l, j)]
        dg, du = _ffn_dact(df, wd, g, u)
        dwd = _ffn_dw_down(a, df)
        dwg, dwu = _ffn_dw_in(n, dg, du)
        parts = [p.reshape(N_SHARDS, 2, p.shape[1] // 2, p.shape[2]) for p in (dwg, dwu, dwd)]
        rs_launch(gid, parts)
        dg, du = _not_before((dg, du), parts)
        dn = _ffn_dn(dg, du, wg, wu)
        dh, d_pre[(l, j)], *rest = _norm_in_bwd("ffn_pre_bwd", h, dh_out, [(row(pre_g[l, j]), dn)] + list(extra))
        return dh, rest

    dh5, _ = ffn_bwd(1, 1, 0, dy)
    dh5 = _not_before(dh5, rs_mid(0, dh5))

    dm1, g_mix_post1 = _norm_out_bwd("mix_post_bwd", m1, dh5, row(w["mix_post_g"][1]), 1.0)
    do = _mm2d("attn_out_dx", [(dm1, w_o, "nt", 0)], [(w_o.shape[0], BF16)])[0]
    g_w_o = _mm2d("attn_out_dw", [(o, dm1, "tn", 0)], [(d, F32)])[0]
    dq_n, dk_n, dv_h, dq_c, dq_s, dk_r = _attn_bwd(q_n, q_r, k_n, v_h, k_r, do, lse, cos2, sin2, seq)
    dq_c = dq_c.transpose(1, 0, 2).reshape(t, -1)
    dq_s = dq_s.transpose(1, 0, 2).reshape(t, -1)
    dqn = _mm2d("q_up_dx", [(dq_n, w_qn, "nt", 0), (dq_c, w_qr, "nt", 0), (dq_s, w_qr_rot, "nt", 0)], [(q_rank, F32)])[0]
    g_qn, g_qr, g_qr_rot = _mm2d("q_up_dw", [(qn, dq_n, "tn", 0), (qn, dq_c, "tn", 1), (qn, dq_s, "tn", 2)],
                                 [(w_qn.shape[1], F32), (w_qr.shape[1], F32), (w_qr.shape[1], F32)])
    dqd, g_q_norm = _norm_out_bwd("q_norm_bwd", qd, dqn, row(w["mla_q_norm_g"][0]), 1.0)
    dn4 = _mm2d("q_down_dx", [(dqd, w_dq, "nt", 0)], [(d, F32)])[0]
    g_w_dq = _mm2d("q_down_dw", [(n4, dqd, "tn", 0)], [(q_rank, F32)])[0]
    dh4, g_mix_pre1 = _norm_in_bwd("mix_pre_bwd", h4, dh5, [(row(w["mix_pre_g"][1]), dn4)])

    dc_n = _mm2d("kv_up_dx", [(dk_n, w_k, "nt", 0), (dv_h, w_v, "nt", 0)], [(KV_RANK, F32)])[0]
    g_wk, g_wv = _mm2d("kv_up_dw", [(c_n, dk_n, "tn", 0), (c_n, dv_h, "tn", 1)], [(w_k.shape[1], F32), (w_v.shape[1], F32)])
    dc, g_kv_a = _norm_out_bwd("kv_a_norm_bwd", c_raw, dc_n, row(w["kv_a_norm_g"]), 1.0)
    dkr_c, dkr_s = _rope_bwd(dk_r, cos2, sin2)
    dn3kv = _mm2d("kv_down_dx", [(dc, w_c, "nt", 0), (dkr_c, w_kr, "nt", 0), (dkr_s, w_kr_rot, "nt", 0)], [(d, F32)])[0]
    g_wc, g_wkr, g_wkr_rot = _mm2d("kv_down_dw", [(n3kv, dc, "tn", 0), (n3kv, dkr_c, "tn", 1), (n3kv, dkr_s, "tn", 2)],
                                   [(KV_RANK, F32), (QK_ROPE, F32), (QK_ROPE, F32)])

    dh4 = _not_before(dh4, rs_end(0, dh4))
    dh3, (g_kv_norm,) = ffn_bwd(1, 0, 1, dh4, extra=[(row(w["kv_norm_g"]), dn3kv)])
    dh3 = _not_before(dh3, rs_mid(1, dh3))
    dh2, _ = ffn_bwd(0, 1, 2, dh3)
    dh2 = _not_before(dh2, (rs_end(1, dh2), rs_mid(2, dh2)))

    dm0, g_mix_post0 = _norm_out_bwd("mix_post_bwd", m0, dh2, row(w["mix_post_g"][0]), 1.0)
    d_uv = _mm2d("gmlp_out_dx", [(dm0, w_out, "nt", 0)], [(half, F32)])[0]
    g_w_out = _mm2d("gmlp_out_dw", [(uv, dm0, "tn", 0)], [(d, F32)])[0]
    dzp, g_ln_g, g_ln_b, g_w_s, g_b_s = _sgu_bwd(zp, d_uv, ln_g, ln_b, w_s, bsb)
    dn1 = _mm2d("gmlp_in_dx", [(dzp, w_in, "nt", 0)], [(d, F32)])[0]
    g_w_in = _mm2d("gmlp_in_dw", [(n1, dzp, "tn", 0)], [(w_in.shape[1], F32)])[0]
    dh1, g_mix_pre0 = _norm_in_bwd("mix_pre_bwd", h1, dh2, [(row(w["mix_pre_g"][0]), dn1)])
    part = {
        "gmlp_w_in": g_w_in[None], "gmlp_w_out": g_w_out[None],
        "w_dkv": jnp.concatenate([g_wc, g_wkr + _unrot_cols(g_wkr_rot)], axis=1),
        "w_ukv": jnp.stack([g_wk.reshape(KV_RANK, N_HEADS, QK_NOPE), g_wv.reshape(KV_RANK, N_HEADS, V_DIM)],
                           axis=2).reshape(KV_RANK, -1),
        "mla_w_dq": g_w_dq[None],
        "mla_w_uq": jnp.concatenate(
            [g_qn.reshape(q_rank, N_HEADS, QK_NOPE),
             g_qr.reshape(q_rank, N_HEADS, QK_ROPE) + _unrot_cols(g_qr_rot.reshape(q_rank, N_HEADS, QK_ROPE))],
            axis=-1).reshape(1, q_rank, -1),
        "mla_w_o": g_w_o[None],
        "gmlp_ln_g": g_ln_g, "gmlp_ln_b": g_ln_b,
        "mix_pre_g": jnp.concatenate([g_mix_pre0, g_mix_pre1]), "mix_post_g": jnp.concatenate([g_mix_post0, g_mix_post1]),
        "gmlp_w_s": g_w_s[None], "gmlp_b_s": g_b_s.reshape(1, GROUPS, CHUNK),
        "kv_norm_g": g_kv_norm.reshape(-1), "kv_a_norm_g": g_kv_a.reshape(-1), "mla_q_norm_g": g_q_norm,
    }

    sharded = mats + [e for e in vecs if e[0].startswith("gmlp")]
    n_sh = sum(w[n].size for n, _ in sharded)
    n_rep = sum(w[n].size for n in replicated)
    sh_rows = _round_up(-(-n_sh // PACK_WIDTH), 8)
    rep_rows = _round_up(-(-(n_rep // N_SHARDS) // PACK_WIDTH), 8)
    rows = _round_up(sh_rows + rep_rows, 32)
    sh_flat = jnp.concatenate([_split_shards(part[n], ax) for n, ax in sharded], axis=1)
    rep_flat = jnp.concatenate([part[n].reshape(-1) for n in replicated]).reshape(N_SHARDS, -1)
    small = jnp.concatenate([
        jnp.pad(sh_flat, ((0, 0), (0, sh_rows * PACK_WIDTH - n_sh))),
        jnp.pad(rep_flat, ((0, 0), (0, (rows - sh_rows) * PACK_WIDTH - n_rep // N_SHARDS)))], axis=1)
    small = small.astype(BF16).reshape(N_SHARDS, 2, rows // 2, PACK_WIDTH)

    rs_launch(3, [small])
    dh1 = _not_before(dh1, rs_end(2, dh1))
    dx, _ = ffn_bwd(0, 0, 4, dh1, after_post=lambda df: _not_before(df, rs_mid(3, df)))

    launched = rs_mid(4, dx)
    lj = [(l, j) for l in range(2) for j in range(2)]
    tiny = jnp.concatenate([d_pre[k] for k in lj] + [d_post[k] for k in lj]).reshape(-1, 128)
    tiny = _all_sum(_not_before(tiny, launched)).reshape(2, 2, 2, d)
    shard_cols = d // N_SHARDS
    grads = {"ffn_pre_g": lax.dynamic_slice_in_dim(tiny[0], chip[0] * shard_cols, shard_cols, axis=2),
             "ffn_post_g": lax.dynamic_slice_in_dim(tiny[1], chip[0] * shard_cols, shard_cols, axis=2)}
    own_small, recv_small = rs_end(3, launched)[0], rs[3]["recv"][0]
    g_small = jnp.where(core[0] == 0, jnp.concatenate([own_small, recv_small]), jnp.concatenate([recv_small, own_small]))
    g_rep = _gather_rows(g_small, sh_rows, rep_rows)
    for (n, _), g in zip(sharded, _split_flat(g_small.reshape(-1), [w[n].shape for n, _ in sharded])):
        grads[n] = g
    rep_vec = g_rep.reshape(N_SHARDS, -1)[:, :n_rep // N_SHARDS].reshape(-1)
    for n, g in zip(replicated, _split_flat(rep_vec, [w[n].shape for n in replicated])):
        grads[n] = g

    delta, new_m, new_v = {}, {}, {}
    for n in names:
        if n not in ffn_names:
            delta[n], new_m[n], new_v[n] = _adamw(w[n], grads[n], mom[n], var[n])
    chain = {n: None for n in ffn_names}

    def ffn_update(gid, l, j):
        for q, n in enumerate(ffn_names):
            chain[n] = _adamw_halves(w[n], mom[n], var[n], l, j, rs[gid]["own"][q], rs[gid]["recv"][q], core, chain[n])

    ffn_update(0, 1, 1)
    ffn_update(1, 1, 0)
    ffn_update(2, 0, 1)
    rs_end(4, ([delta[n] for n in delta], [chain[n] for n in ffn_names]))
    ffn_update(4, 0, 0)
    for n in ffn_names:
        grads[n], delta[n], new_m[n], new_v[n] = [o.reshape(w[n].shape) for o in chain[n]]
    return (loss, dx.reshape(x.shape), *[grads[n] for n in names], *[delta[n] for n in names],
            *[new_m[n] for n in names], *[new_v[n] for n in names])
```

```python
import math

import jax
import jax.numpy as jnp
from jax import lax
from jax.experimental import pallas as pl
from jax.experimental.pallas import tpu as pltpu
from jax.experimental.pallas import tpu_sc as plsc

F32, BF16 = jnp.float32, jnp.bfloat16

RMS_EPS, LN_EPS, NEG_INF = 1e-6, 1e-5, -1e30
N_HEADS, QK_NOPE, QK_ROPE, V_DIM, KV_RANK = 8, 128, 64, 128, 256
CHUNK, GROUPS = 128, 16
ROPE_THETA = 10000.0
ADAM_LR, ADAM_B1, ADAM_B2, ADAM_EPS, ADAM_WD, ADAM_STEP = 0.001, 0.9, 0.999, 1e-08, 0.01, 10
N_SHARDS = 4

VMEM_LIMIT_BYTES = 48 * 1024 * 1024
ROW_TILE = 512
K_TILE = 2048
PACK_WIDTH = 1024

_DN = {"nn": (((1,), (0,)), ((), ())), "nt": (((1,), (1,)), ((), ())), "tn": (((0,), (0,)), ((), ()))}
_MESH = pl.DeviceIdType.MESH
_ANY = pl.BlockSpec(memory_space=pl.ANY)


def _params(sem):
    return pltpu.CompilerParams(dimension_semantics=sem, vmem_limit_bytes=VMEM_LIMIT_BYTES)


def _mm(name, grid, ins, pairs, acc_shapes, outs, epilogue, extras=(), inner=0):
    n_in, n_ex, n_out = len(ins), len(extras), len(outs)
    gk = grid[2]

    def body(*refs):
        in_refs, ex_refs = refs[:n_in], refs[n_in:n_in + n_ex]
        out_refs = refs[n_in + n_ex:n_in + n_ex + n_out]
        acc_refs = refs[n_in + n_ex + n_out:]
        parts = [None] * len(acc_shapes)
        for a, b, c, dims in pairs:
            for s in range(max(inner, 1)):
                lhs, rhs = (in_refs[a][s], in_refs[b][s]) if inner else (in_refs[a][...], in_refs[b][...])
                p = lax.dot_general(lhs, rhs, _DN[dims], preferred_element_type=F32)
                parts[c] = p if parts[c] is None else parts[c] + p

        def finish(accs):
            vals = epilogue(accs, [r[...] for r in ex_refs])
            for r, v in zip(out_refs, vals):
                r[...] = v.astype(r.dtype)

        if gk == 1:
            finish(parts)
        else:
            k = pl.program_id(2)

            @pl.when(k == 0)
            def _():
                for r, p in zip(acc_refs, parts):
                    r[...] = p

            @pl.when(k > 0)
            def _():
                for r, p in zip(acc_refs, parts):
                    r[...] += p

            @pl.when(k == gk - 1)
            def _():
                finish([r[...] for r in acc_refs])

    return pl.pallas_call(
        body,
        out_shape=[jax.ShapeDtypeStruct(s, d) for s, d, _, _ in outs],
        grid=grid,
        in_specs=[pl.BlockSpec(bs, im) for _, bs, im in list(ins) + list(extras)],
        out_specs=[pl.BlockSpec(bs, im) for _, _, bs, im in outs],
        scratch_shapes=[pltpu.VMEM(s, F32) for s in acc_shapes] if gk > 1 else [],
        name=name,
        compiler_params=_params(("parallel", "parallel", "arbitrary")),
    )(*[a for a, _, _ in ins], *[a for a, _, _ in extras])


def _mm2d(name, pairs, outs, epilogue=None, row_extras=(), vec_extras=()):
    def mk(a, dims):
        return (a.shape[0], a.shape[1]) if dims[0] == "n" else (a.shape[1], a.shape[0])

    def nk(b, dims):
        return (b.shape[1], b.shape[0]) if dims[1] == "n" else (b.shape[0], b.shape[1])

    m = mk(pairs[0][0], pairs[0][2])[0]
    ks = [mk(a, d)[1] for a, _, d, _ in pairs]
    n_acc = 1 + max(p[3] for p in pairs)
    acc_n = [None] * n_acc
    for a, b, d, c in pairs:
        assert mk(a, d)[0] == m and nk(b, d)[1] == mk(a, d)[1]
        acc_n[c] = nk(b, d)[0]
    tm = min(m, ROW_TILE)
    if len(set(ks)) == 1 and ks[0] > 1024:
        tk = K_TILE if ks[0] % K_TILE == 0 else 512
        tks, gk = [tk] * len(pairs), ks[0] // tk
    else:
        tks, gk = ks, 1
    if len(set(acc_n)) == 1 and acc_n[0] > 1024:
        tns, gj = [1024] * n_acc, acc_n[0] // 1024
    else:
        tns, gj = acc_n, 1

    ins, plist = [], []
    for (a, b, d, c), tk in zip(pairs, tks):
        tn = tns[c]
        a_spec = ((tm, tk), lambda i, j, k: (i, k)) if d[0] == "n" else ((tk, tm), lambda i, j, k: (k, i))
        b_spec = ((tk, tn), lambda i, j, k: (k, j)) if d[1] == "n" else ((tn, tk), lambda i, j, k: (j, k))
        ins += [(a, *a_spec), (b, *b_spec)]
        plist.append((len(ins) - 2, len(ins) - 1, c, d))
    extras = [(r, (tm, r.shape[1]), lambda i, j, k: (i, 0)) for r in row_extras]
    extras += [(v, v.shape, lambda i, j, k: (0, 0)) for v in vec_extras]
    out_specs = []
    for n, dt in outs:
        bn = 1024 if (gj > 1) else n
        out_specs.append(((m, n), dt, (tm, bn), lambda i, j, k: (i, j)))
    if epilogue is None:
        epilogue = lambda accs, ex: accs
    return _mm(name, (m // tm, gj, gk), ins, plist, [(tm, tn) for tn in tns], out_specs, epilogue, extras)


def _rms(x, g):
    return x * lax.rsqrt(jnp.mean(x * x, axis=-1, keepdims=True) + RMS_EPS) * g


def _rms_bwd(x, g, dy):
    r = lax.rsqrt(jnp.mean(x * x, axis=-1, keepdims=True) + RMS_EPS)
    gy = dy * g
    dx = r * gy - x * (r * r * r) * jnp.mean(gy * x, axis=-1, keepdims=True)
    return dx, jnp.sum(dy * x * r, axis=0, keepdims=True)


def _sigmoid(x):
    return 1.0 / (1.0 + jnp.exp(-x))


_GELU_C = math.sqrt(2.0 / math.pi)


def _gelu(x):
    return x * (0.5 * (1.0 + jnp.tanh(_GELU_C * (x + 0.044715 * (x * x * x)))))


def _gelu_grad(x):
    t = jnp.tanh(_GELU_C * (x + 0.044715 * (x * x * x)))
    return 0.5 * (1.0 + t) + 0.5 * x * (1.0 - t * t) * (_GELU_C * (1.0 + 3.0 * 0.044715 * (x * x)))


def _rows(name, row_ins, vec_ins, fn, row_outs, acc_outs=()):
    t = row_ins[0].shape[0]
    tm = min(t, ROW_TILE)
    nr, nv, no = len(row_ins), len(vec_ins), len(row_outs)

    def body(*refs):
        outs, incs = fn([r[...] for r in refs[:nr]], [r[...] for r in refs[nr:nr + nv]])
        for r, v in zip(refs[nr + nv:nr + nv + no], outs):
            r[...] = v.astype(r.dtype)
        i = pl.program_id(0)
        for r, v in zip(refs[nr + nv + no:], incs):
            @pl.when(i == 0)
            def _():
                r[...] = v

            @pl.when(i > 0)
            def _():
                r[...] += v

    in_specs = [pl.BlockSpec((tm, a.shape[1]), lambda i: (i, 0)) for a in row_ins]
    in_specs += [pl.BlockSpec(v.shape, lambda i, nd=v.ndim: (0,) * nd) for v in vec_ins]
    out_shape = [jax.ShapeDtypeStruct((t, c), dt) for c, dt in row_outs]
    out_shape += [jax.ShapeDtypeStruct(s, F32) for s in acc_outs]
    out_specs = [pl.BlockSpec((tm, c), lambda i: (i, 0)) for c, _ in row_outs]
    out_specs += [pl.BlockSpec(s, lambda i, nd=len(s): (0,) * nd) for s in acc_outs]
    return pl.pallas_call(body, out_shape=out_shape, grid=(t // tm,), in_specs=in_specs, out_specs=out_specs,
                          name=name, compiler_params=_params(("arbitrary",)))(*row_ins, *vec_ins)


def _rms_fwd(x, g):
    return _rows("rms_fwd", [x], [g], lambda r, v: ([_rms(r[0], v[0])], []), [(x.shape[1], BF16)])[0]


def _norm_out_bwd(name, f, d_out, g, scale):
    def fn(r, v):
        dx, dg = _rms_bwd(r[0], v[0], r[1] * scale)
        return [dx], [dg]

    c = f.shape[1]
    return _rows(name, [f, d_out], [g], fn, [(c, BF16)], [(1, c)])


def _norm_in_bwd(name, h, d_res, branches):
    nb = len(branches)

    def fn(r, v):
        dh, dgs = r[1], []
        for b in range(nb):
            dx, dg = _rms_bwd(r[0], v[b], r[2 + b])
            dh = dh + dx
            dgs.append(dg)
        return [dh], dgs

    c = h.shape[1]
    return _rows(name, [h, d_res] + [dn for _, dn in branches], [g for g, _ in branches], fn, [(c, F32)],
                 [(1, c)] * nb)


def _loss_head(y, target):
    d = y.shape[1]

    def fn(r, v):
        e = r[0] - r[1]
        s = jnp.sum(jnp.sum(e * e, axis=1, keepdims=True), axis=0, keepdims=True) * (0.5 / d)
        return [e * (1.0 / d)], [jnp.broadcast_to(s, (1, 128))]

    dy, acc = _rows("loss_head", [y, target], [], fn, [(d, F32)], [(1, 128)])
    return acc[0, 0], dy


def _rope_bwd(dk, cos2, sin2):
    c = dk.shape[1]
    return _rows("rope_bwd", [dk, cos2, sin2], [], lambda r, v: ([r[0] * r[1], r[0] * r[2]], []),
                 [(c, BF16), (c, BF16)])


def _ffn_up(n, wg, wu):
    t, d = n.shape
    fs = wg.shape[-1]
    tm = min(t, ROW_TILE)
    w_spec = ((None, d, fs), lambda s, i, k: (s, 0, 0))

    def epi(accs, ex):
        g, u = accs
        return [g, u, g * _sigmoid(g) * u]

    o_spec = ((None, tm, fs), lambda s, i, k: (s, i, 0))
    outs = [((N_SHARDS, t, fs), BF16, *o_spec)] * 3
    return _mm("ffn_up", (N_SHARDS, t // tm, 1),
               [(n, (tm, d), lambda s, i, k: (i, 0)), (wg, *w_spec), (wu, *w_spec)],
               [(0, 1, 0, "nn"), (0, 2, 1, "nn")], [(tm, fs)] * 2, outs, epi)


def _down(name, a_in, w_in, gk, h, post_g, next_gs, scale, inner=0):
    t, d = h.shape
    tm = min(t, ROW_TILE)
    kn = next_gs.shape[0]

    def epi(accs, ex):
        f, hv, pg, ng = accs[0], ex[0], ex[1], ex[2]
        hn = hv + scale * _rms(f, pg)
        return [f, hn] + [_rms(hn, ng[q:q + 1]) for q in range(kn)]

    row = ((tm, d), lambda i, j, k: (i, 0))
    outs = [((t, d), F32, *row), ((t, d), F32, *row)] + [((t, d), BF16, *row)] * kn
    extras = [(h, *row), (post_g, (1, d), lambda i, j, k: (0, 0)), (next_gs, (kn, d), lambda i, j, k: (0, 0))]
    return _mm(name, (t // tm, 1, gk), [a_in, w_in], [(0, 1, 0, "nn")], [(tm, d)], outs, epi, extras, inner)


def _ffn_down(a, wd, h, post_g, next_gs):
    t, d = h.shape
    fs = a.shape[-1]
    tm = min(t, ROW_TILE)
    return _down("ffn_down", (a, (N_SHARDS, tm, fs), lambda i, _, k: (0, i, 0)),
                 (wd, (N_SHARDS, fs, d), lambda i, _, k: (0, 0, 0)), 1, h, post_g, next_gs, 0.5, N_SHARDS)


def _ffn_dact(df, wd, g, u):
    t, d = df.shape
    fs = g.shape[-1]
    tm = min(t, ROW_TILE)

    def epi(accs, ex):
        da, gv, uv = accs[0], ex[0].astype(F32), ex[1].astype(F32)
        sg = _sigmoid(gv)
        return [da * uv * (sg * (1.0 + gv * (1.0 - sg))), da * (gv * sg)]

    o_spec = ((None, tm, fs), lambda s, i, k: (s, i, 0))
    outs = [((N_SHARDS, t, fs), BF16, *o_spec)] * 2
    return _mm("ffn_dact", (N_SHARDS, t // tm, 1),
               [(df, (tm, d), lambda s, i, k: (i, 0)), (wd, (None, fs, d), lambda s, i, k: (s, 0, 0))],
               [(0, 1, 0, "nt")], [(tm, fs)], outs, epi, [(g, *o_spec), (u, *o_spec)])


def _ffn_dn(dg, du, wg, wu):
    _, t, fs = dg.shape
    d = wg.shape[-2]
    tm = min(t, ROW_TILE)
    a_spec = ((N_SHARDS, tm, fs), lambda i, _, k: (0, i, 0))
    w_spec = ((N_SHARDS, d, fs), lambda i, _, k: (0, 0, 0))
    outs = [((t, d), F32, (tm, d), lambda i, _, k: (i, 0))]
    return _mm("ffn_dn", (t // tm, 1, 1), [(dg, *a_spec), (wg, *w_spec), (du, *a_spec), (wu, *w_spec)],
               [(0, 1, 0, "nt"), (2, 3, 0, "nt")], [(tm, d)], outs, lambda accs, ex: accs, inner=N_SHARDS)[0]


def _ffn_dw_in(n, dg, du):
    _, t, fs = dg.shape
    d = n.shape[1]
    tk = min(t, K_TILE)
    b_spec = ((None, tk, fs), lambda s, _, k: (s, k, 0))
    o_spec = ((None, d, fs), lambda s, _, k: (s, 0, 0))
    outs = [((N_SHARDS, d, fs), BF16, *o_spec)] * 2
    return _mm("ffn_dw_in", (N_SHARDS, 1, t // tk), [(n, (tk, d), lambda s, _, k: (k, 0)), (dg, *b_spec), (du, *b_spec)],
               [(0, 1, 0, "tn"), (0, 2, 1, "tn")], [(d, fs)] * 2, outs, lambda accs, ex: accs)


def _ffn_dw_down(a, df):
    _, t, fs = a.shape
    d = df.shape[1]
    tk = min(t, K_TILE)
    outs = [((N_SHARDS, fs, d), BF16, (None, fs, d), lambda s, _, k: (s, 0, 0))]
    return _mm("ffn_dw_down", (N_SHARDS, 1, t // tk),
               [(a, (None, tk, fs), lambda s, _, k: (s, k, 0)), (df, (tk, d), lambda s, _, k: (k, 0))],
               [(0, 1, 0, "tn")], [(fs, d)], outs, lambda accs, ex: accs)[0]


def _causal_weight(w):
    row = lax.broadcasted_iota(jnp.int32, (CHUNK, CHUNK), 0)
    col = lax.broadcasted_iota(jnp.int32, (CHUNK, CHUNK), 1)
    return row >= col, jnp.where(row >= col, w, 0.0).astype(BF16)


def _layer_norm(v, g, b):
    xc = v - jnp.mean(v, axis=-1, keepdims=True)
    rstd = lax.rsqrt(jnp.mean(xc * xc, axis=-1, keepdims=True) + LN_EPS)
    xhat = xc * rstd
    return xhat, rstd, xhat * g + b


def _sgu_specs(t, half, tm):
    return [pl.BlockSpec((tm, half), lambda i: (i, 0)), pl.BlockSpec((tm, half), lambda i: (i, 1))]


def _sgu_fwd(zp, ln_g, ln_b, w_s, bsb):
    t, half = zp.shape[0], zp.shape[1] // 2
    tm = min(t, 2 * CHUNK)

    def body(u_ref, v_ref, g_ref, b_ref, w_ref, bs_ref, o_ref):
        u = _gelu(u_ref[...])
        _, _, vn = _layer_norm(_gelu(v_ref[...]), g_ref[...], b_ref[...])
        vb = vn.astype(BF16)
        for g in range(GROUPS):
            _, wm = _causal_weight(w_ref[g])
            cols = slice(g * CHUNK, (g + 1) * CHUNK)
            for c in range(tm // CHUNK):
                rows = slice(c * CHUNK, (c + 1) * CHUNK)
                sv = jnp.dot(wm, vb[rows, cols], preferred_element_type=F32) + bs_ref[g]
                o_ref[rows, cols] = (u[rows, cols] * sv).astype(BF16)

    whole = lambda a: pl.BlockSpec(a.shape, lambda i, nd=a.ndim: (0,) * nd)
    return pl.pallas_call(
        body, out_shape=jax.ShapeDtypeStruct((t, half), BF16), grid=(t // tm,),
        in_specs=_sgu_specs(t, half, tm) + [whole(ln_g), whole(ln_b), whole(w_s), whole(bsb)],
        out_specs=pl.BlockSpec((tm, half), lambda i: (i, 0)), name="sgu_fwd",
        compiler_params=_params(("arbitrary",)))(zp, zp, ln_g, ln_b, w_s, bsb)


def _sgu_bwd(zp, d_uv, ln_g, ln_b, w_s, bsb):
    t, half = zp.shape[0], zp.shape[1] // 2
    tm = min(t, 2 * CHUNK)

    def body(u_ref, v_ref, d_ref, g_ref, b_ref, w_ref, bs_ref, dz_ref, dlg_ref, dlb_ref, dws_ref, dbs_ref, dvn_ref):
        i = pl.program_id(0)

        @pl.when(i == 0)
        def _():
            dlg_ref[...] = jnp.zeros_like(dlg_ref)
            dlb_ref[...] = jnp.zeros_like(dlb_ref)
            dws_ref[...] = jnp.zeros_like(dws_ref)
            dbs_ref[...] = jnp.zeros_like(dbs_ref)

        up, vp = u_ref[...], v_ref[...]
        u, gup = _gelu(up), _gelu_grad(up)
        xhat, rstd, vn = _layer_norm(_gelu(vp), g_ref[...], b_ref[...])
        vb = vn.astype(BF16)
        d = d_ref[...]
        for g in range(GROUPS):
            mask, wm = _causal_weight(w_ref[g])
            cols = slice(g * CHUNK, (g + 1) * CHUNK)
            for c in range(tm // CHUNK):
                rows = slice(c * CHUNK, (c + 1) * CHUNK)
                blk = vb[rows, cols]
                sv = jnp.dot(wm, blk, preferred_element_type=F32) + bs_ref[g]
                dblk = d[rows, cols]
                dz_ref[rows, cols] = (dblk * sv * gup[rows, cols]).astype(BF16)
                dsv = dblk * u[rows, cols]
                dsvb = dsv.astype(BF16)
                dvn_ref[rows, cols] = lax.dot_general(wm, dsvb, _DN["tn"], preferred_element_type=F32)
                dw = lax.dot_general(dsvb, blk, _DN["nt"], preferred_element_type=F32)
                dws_ref[g] += jnp.where(mask, dw, 0.0)
                dbs_ref[g] += jnp.sum(dsv, axis=1, keepdims=True)
        dvn = dvn_ref[...]
        dlg_ref[...] += jnp.sum(dvn * xhat, axis=0, keepdims=True)
        dlb_ref[...] += jnp.sum(dvn, axis=0, keepdims=True)
        dxh = dvn * g_ref[...]
        dv = rstd * (dxh - jnp.mean(dxh, axis=-1, keepdims=True)
                     - xhat * jnp.mean(dxh * xhat, axis=-1, keepdims=True))
        dz_ref[:, half:] = (dv * _gelu_grad(vp)).astype(BF16)

    whole = lambda a: pl.BlockSpec(a.shape, lambda i, nd=a.ndim: (0,) * nd)
    wshape = lambda s: pl.BlockSpec(s, lambda i, nd=len(s): (0,) * nd)
    out_shape = [jax.ShapeDtypeStruct((t, 2 * half), BF16), jax.ShapeDtypeStruct((1, half), F32),
                 jax.ShapeDtypeStruct((1, half), F32), jax.ShapeDtypeStruct(w_s.shape, F32),
                 jax.ShapeDtypeStruct((GROUPS, CHUNK, 1), F32)]
    return pl.pallas_call(
        body, out_shape=out_shape, grid=(t // tm,),
        in_specs=_sgu_specs(t, half, tm) + [pl.BlockSpec((tm, half), lambda i: (i, 0)), whole(ln_g), whole(ln_b),
                                            whole(w_s), whole(bsb)],
        out_specs=[pl.BlockSpec((tm, 2 * half), lambda i: (i, 0)), wshape((1, half)), wshape((1, half)),
                   wshape(w_s.shape), wshape((GROUPS, CHUNK, 1))],
        scratch_shapes=[pltpu.VMEM((tm, half), F32)], name="sgu_bwd",
        compiler_params=_params(("arbitrary",)))(zp, zp, d_uv, ln_g, ln_b, w_s, bsb)


_SCALE = (QK_NOPE + QK_ROPE) ** -0.5


def _attn_scores(qn, qr, kn, kr, i, tq, n):
    s = lax.dot_general(qn, kn, _DN["nt"], preferred_element_type=F32)
    s = (s + lax.dot_general(qr, kr, _DN["nt"], preferred_element_type=F32)) * _SCALE
    row = i * tq + lax.broadcasted_iota(jnp.int32, (tq, n), 0)
    col = lax.broadcasted_iota(jnp.int32, (tq, n), 1)
    return jnp.where(col <= row, s, NEG_INF)


def _attn_specs(seq):
    head = lambda b, h: (b, h)
    return dict(
        qn=pl.BlockSpec((seq, QK_NOPE), head),
        qr=pl.BlockSpec((None, seq, QK_ROPE), lambda b, h: (h, b, 0)),
        kr=pl.BlockSpec((seq, QK_ROPE), lambda b, h: (b, 0)),
        lse=pl.BlockSpec((None, seq, 1), lambda b, h: (h, b, 0)),
    )


def _attn_fwd(qn, qr, kn, v, kr, seq):
    t = qn.shape[0]
    tq = min(seq, 2 * CHUNK)
    sp = _attn_specs(seq)

    def body(qn_ref, qr_ref, kn_ref, v_ref, kr_ref, o_ref, lse_ref):
        for i in range(seq // tq):
            rows, n = slice(i * tq, (i + 1) * tq), (i + 1) * tq
            s = _attn_scores(qn_ref[rows, :], qr_ref[rows, :], kn_ref[0:n, :], kr_ref[0:n, :], i, tq, n)
            m = jnp.max(s, axis=-1, keepdims=True)
            p = jnp.exp(s - m)
            l = jnp.sum(p, axis=-1, keepdims=True)
            o_ref[rows, :] = jnp.dot((p / l).astype(BF16), v_ref[0:n, :], preferred_element_type=F32).astype(BF16)
            lse_ref[rows, :] = m + jnp.log(l)

    return pl.pallas_call(
        body, out_shape=[jax.ShapeDtypeStruct((t, N_HEADS * V_DIM), BF16), jax.ShapeDtypeStruct((N_HEADS, t, 1), F32)],
        grid=(t // seq, N_HEADS), in_specs=[sp["qn"], sp["qr"], sp["qn"], sp["qn"], sp["kr"]],
        out_specs=[sp["qn"], sp["lse"]], name="attn_fwd",
        compiler_params=_params(("parallel", "arbitrary")))(qn, qr, kn, v, kr)


def _attn_bwd(qn, qr, kn, v, kr, do, lse, cos2, sin2, seq):
    t = qn.shape[0]
    tq = min(seq, 2 * CHUNK)
    sp = _attn_specs(seq)

    def body(qn_ref, qr_ref, kn_ref, v_ref, kr_ref, do_ref, lse_ref, cos_ref, sin_ref,
             dqn_ref, dkn_ref, dv_ref, dqc_ref, dqs_ref, dkr_ref, dk_acc, dv_acc, dkr_acc):
        dk_acc[...] = jnp.zeros_like(dk_acc)
        dv_acc[...] = jnp.zeros_like(dv_acc)
        dkr_acc[...] = jnp.zeros_like(dkr_acc)
        for i in range(seq // tq):
            rows, n = slice(i * tq, (i + 1) * tq), (i + 1) * tq
            q_n, q_r, d_o = qn_ref[rows, :], qr_ref[rows, :], do_ref[rows, :]
            k_n, k_r = kn_ref[0:n, :], kr_ref[0:n, :]
            s = _attn_scores(q_n, q_r, k_n, k_r, i, tq, n)
            p = jnp.exp(s - lse_ref[rows, :])
            dp = lax.dot_general(d_o, v_ref[0:n, :], _DN["nt"], preferred_element_type=F32)
            ds = (p * (dp - jnp.sum(p * dp, axis=-1, keepdims=True)) * _SCALE).astype(BF16)
            dqn_ref[rows, :] = jnp.dot(ds, k_n, preferred_element_type=F32).astype(BF16)
            dqr = jnp.dot(ds, k_r, preferred_element_type=F32)
            dqc_ref[rows, :] = (dqr * cos_ref[rows, :]).astype(BF16)
            dqs_ref[rows, :] = (dqr * sin_ref[rows, :]).astype(BF16)
            dk_acc[0:n, :] += lax.dot_general(ds, q_n, _DN["tn"], preferred_element_type=F32)
            dkr_acc[0:n, :] += lax.dot_general(ds, q_r, _DN["tn"], preferred_element_type=F32)
            dv_acc[0:n, :] += lax.dot_general(p.astype(BF16), d_o, _DN["tn"], preferred_element_type=F32)
        dkn_ref[...] = dk_acc[...].astype(BF16)
        dv_ref[...] = dv_acc[...].astype(BF16)
        h = pl.program_id(1)

        @pl.when(h == 0)
        def _():
            dkr_ref[...] = dkr_acc[...]

        @pl.when(h > 0)
        def _():
            dkr_ref[...] += dkr_acc[...]

    wide = jax.ShapeDtypeStruct((t, N_HEADS * V_DIM), BF16)
    rope = jax.ShapeDtypeStruct((N_HEADS, t, QK_ROPE), BF16)
    krf = pl.BlockSpec((seq, QK_ROPE), lambda b, h: (b, 0))
    return pl.pallas_call(
        body, out_shape=[wide, wide, wide, rope, rope, jax.ShapeDtypeStruct((t, QK_ROPE), F32)],
        grid=(t // seq, N_HEADS),
        in_specs=[sp["qn"], sp["qr"], sp["qn"], sp["qn"], sp["kr"], sp["qn"], sp["lse"], krf, krf],
        out_specs=[sp["qn"], sp["qn"], sp["qn"], sp["qr"], sp["qr"], krf],
        scratch_shapes=[pltpu.VMEM((seq, QK_NOPE), F32), pltpu.VMEM((seq, V_DIM), F32), pltpu.VMEM((seq, QK_ROPE), F32)],
        name="attn_bwd", compiler_params=_params(("parallel", "arbitrary")))(qn, qr, kn, v, kr, do, lse, cos2, sin2)


def _row_tile(rows, cols, row_mult=8):
    cap = max(row_mult, (1 << 18) // cols)
    best = rows
    for tr in range(row_mult, min(rows, cap) + 1, row_mult):
        if rows % tr == 0:
            best = tr
    return best if rows > cap else rows


def _adamw_math(w, g, m, v):
    mv = ADAM_B1 * m + (1.0 - ADAM_B1) * g
    vv = ADAM_B2 * v + (1.0 - ADAM_B2) * (g * g)
    m_hat = mv / (1.0 - ADAM_B1 ** ADAM_STEP)
    v_hat = vv / (1.0 - ADAM_B2 ** ADAM_STEP)
    return -ADAM_LR * (m_hat / (jnp.sqrt(v_hat) + ADAM_EPS) + ADAM_WD * w), mv, vv


def _adamw(w, g, m, v):
    shape = w.shape
    c = shape[-1]
    r = w.size // c
    tr = _row_tile(r, c)

    def body(w_ref, g_ref, m_ref, v_ref, d_ref, nm_ref, nv_ref):
        d_ref[...], nm_ref[...], nv_ref[...] = _adamw_math(w_ref[...], g_ref[...], m_ref[...], v_ref[...])

    spec = pl.BlockSpec((tr, c), lambda i: (i, 0))
    outs = pl.pallas_call(body, out_shape=[jax.ShapeDtypeStruct((r, c), F32)] * 3, grid=(r // tr,),
                          in_specs=[spec] * 4, out_specs=[spec] * 3, name="adamw",
                          compiler_params=_params(("parallel",)))(*[a.reshape(r, c) for a in (w, g, m, v)])
    return [o.reshape(shape) for o in outs]


def _adamw_halves(w, m, v, l, j, own, recv, core, prev):
    nl, nj, rows, c = w.shape
    r = rows // 2
    tr = _row_tile(r, c)
    n_prev = 0 if prev is None else 4

    def body(core_ref, w_ref, own_ref, recv_ref, m_ref, v_ref, *rest):
        g_ref, d_ref, nm_ref, nv_ref = rest[n_prev:]
        g = jnp.where(pl.program_id(0) == core_ref[0], own_ref[...], recv_ref[...])
        g_ref[...] = g
        d_ref[...], nm_ref[...], nv_ref[...] = _adamw_math(w_ref[...], g, m_ref[...], v_ref[...])

    nb = r // tr
    slab = pl.BlockSpec((None, None, tr, c), lambda h, i, cr: (l, j, h * nb + i, 0))
    half = pl.BlockSpec((tr, c), lambda h, i, cr: (i, 0))
    grid_spec = pltpu.PrefetchScalarGridSpec(num_scalar_prefetch=1, grid=(2, nb),
                                             in_specs=[slab, half, half, slab, slab] + [_ANY] * n_prev,
                                             out_specs=[slab] * 4)
    return pl.pallas_call(body, out_shape=[jax.ShapeDtypeStruct(w.shape, F32)] * 4, grid_spec=grid_spec,
                          input_output_aliases={6 + q: q for q in range(n_prev)}, name="adamw_halves",
                          compiler_params=_params(("parallel",) * 2))(core, w, own, recv, m, v, *(prev or ()))


def _place():
    x, y, c = lax.axis_index("x"), lax.axis_index("y"), lax.axis_index("c")
    return x, y, c, [(1 - x, y), (x, 1 - y), (1 - x, 1 - y)]


def _dma_sems(*counts):
    return [pltpu.SemaphoreType.DMA((n,)) for n in counts]


def _all_gather(bufs, collective_id, name):
    n = len(bufs)

    def body(*refs):
        ins, outs = refs[:n], refs[n:2 * n]
        send, recv, fsend, frecv, osend, orecv = refs[2 * n:]
        x, y, c, _ = _place()
        xn, yn, sib = (1 - x, y, c), (x, 1 - y, c), (x, y, 1 - c)
        k, kx, ky, kd = 2 * x + y, 2 * (1 - x) + y, 2 * x + 1 - y, 2 * (1 - x) + 1 - y
        _handshake([xn, yn, sib])

        def copy(src, dst, sems, i, to):
            return pltpu.make_async_remote_copy(src, dst, sems[0].at[i], sems[1].at[i], device_id=to, device_id_type=_MESH)

        ici, d2d, own_s = (send, recv), (fsend, frecv), (osend, orecv)
        started = [copy(ins[b], outs[b].at[k], own_s, b, sib) for b in range(n)]
        for first in (True, False):
            for b in range(n):
                mine = outs[b].at[k, c]
                if first:
                    started += [copy(ins[b].at[c, 0], mine.at[0], ici, 6 * b, xn), copy(ins[b].at[c, 1], mine.at[1], ici, 6 * b + 1, yn)]
                else:
                    started += [copy(ins[b].at[c, 1], mine.at[1], ici, 6 * b + 2, xn), copy(ins[b].at[c, 0], mine.at[0], ici, 6 * b + 3, yn)]
        for cp in started:
            cp.start()
        passed = []
        for b in range(n):
            for i, (src_chip, q, to) in enumerate([(kx, 0, yn), (ky, 1, xn)]):
                piece = outs[b].at[src_chip, c, q]
                copy(piece, piece, ici, 6 * b + i, to).wait_recv()
                cp = copy(piece, piece, ici, 6 * b + 4 + i, to)
                cp.start()
                passed.append(cp)
        for b in range(n):
            for i, (src_chip, q) in enumerate([(kx, 1), (ky, 0)]):
                piece = outs[b].at[src_chip, c, q]
                copy(piece, piece, ici, 6 * b + 2 + i, xn).wait_recv()
                half = outs[b].at[src_chip, c]
                cp = copy(half, half, d2d, 3 * b + i, sib)
                cp.start()
                passed.append(cp)
        for b in range(n):
            for i, q in enumerate([0, 1]):
                piece = outs[b].at[kd, c, q]
                copy(piece, piece, ici, 6 * b + 4 + i, xn).wait_recv()
            half = outs[b].at[kd, c]
            cp = copy(half, half, d2d, 3 * b + 2, sib)
            cp.start()
            passed.append(cp)
        for b in range(n):
            for i, src_chip in enumerate([kx, ky, kd]):
                half = outs[b].at[src_chip, 1 - c]
                copy(half, half, d2d, 3 * b + i, sib).wait_recv()
        for cp in started[n:] + passed:
            cp.wait_send()
        for cp in started[:n]:
            cp.wait()

    return _sequencer(body, [jax.ShapeDtypeStruct((N_SHARDS,) + b.shape, b.dtype) for b in bufs],
                      _dma_sems(6 * n, 6 * n, 3 * n, 3 * n, n, n), collective_id, name, bufs)


def _sequencer(body, out_type, sems, collective_id, name, args):
    return pl.kernel(body, out_type=out_type, mesh=plsc.ScalarSubcoreMesh(axis_name="sequencer", num_cores=1),
                     scratch_types=sems, compiler_params=pltpu.CompilerParams(collective_id=collective_id),
                     name=name)(*args)


def _handshake(peers):
    barrier = pltpu.get_barrier_semaphore()
    for peer in peers:
        pl.semaphore_signal(barrier, inc=1, device_id=peer, device_id_type=_MESH)
    pl.semaphore_wait(barrier, len(peers))


def _swap_halves(parts, collective_id, name):
    n = len(parts)

    def body(*refs):
        ins, outs = refs[:n], refs[n:2 * n]
        send, recv = refs[2 * n:]
        x, y, c, _ = _place()
        _handshake([(x, y, 1 - c)])
        cps = [pltpu.make_async_remote_copy(ins[b].at[:, pl.ds(1 - c, 1)], outs[b], send.at[b], recv.at[b],
                                            device_id=(x, y, 1 - c), device_id_type=_MESH) for b in range(n)]
        for cp in cps:
            cp.start()
        for cp in cps:
            cp.wait()

    return _sequencer(body, [jax.ShapeDtypeStruct((N_SHARDS, 1) + p.shape[2:], p.dtype) for p in parts],
                      _dma_sems(n, n), collective_id, name, parts)


def _add_half(part, other, core):
    _, _, r, c = part.shape
    tr = _row_tile(r, c, 16)

    def body(core_ref, p_ref, o_ref, out_ref):
        out_ref[...] = (p_ref[...].astype(F32) + o_ref[...].astype(F32)).astype(out_ref.dtype)

    grid_spec = pltpu.PrefetchScalarGridSpec(
        num_scalar_prefetch=1, grid=(N_SHARDS, r // tr),
        in_specs=[pl.BlockSpec((None, None, tr, c), lambda k, i, cr: (k, cr[0], i, 0)),
                  pl.BlockSpec((None, None, tr, c), lambda k, i, cr: (k, 0, i, 0))],
        out_specs=pl.BlockSpec((None, tr, c), lambda k, i, cr: (k, i, 0)))
    return pl.pallas_call(body, out_shape=jax.ShapeDtypeStruct((N_SHARDS, r, c), part.dtype), grid_spec=grid_spec,
                          name="grad_add_half", compiler_params=_params(("parallel", "parallel")))(core, part, other)


def _scatter_chips(parts, collective_id, name):
    n = len(parts)

    def body(*refs):
        ins, outs = refs[:n], refs[n:2 * n]
        send, recv = refs[2 * n:]
        x, y, c, chips = _place()
        k = 2 * x + y
        _handshake([(px, py, c) for px, py in chips])
        started = []
        for b in range(n):
            for j, (px, py) in enumerate(chips):
                cp = pltpu.make_async_remote_copy(ins[b].at[2 * px + py], outs[b].at[k], send.at[3 * b + j],
                                                  recv.at[3 * b + j], device_id=(px, py, c), device_id_type=_MESH)
                cp.start()
                started.append(cp)
        for b in range(n):
            for j, (px, py) in enumerate(chips):
                got = outs[b].at[2 * px + py]
                pltpu.make_async_remote_copy(got, got, send.at[3 * b + j], recv.at[3 * b + j],
                                             device_id=(px, py, c), device_id_type=_MESH).wait_recv()
        for cp in started:
            cp.wait_send()

    return _sequencer(body, [jax.ShapeDtypeStruct(p.shape, p.dtype) for p in parts], _dma_sems(3 * n, 3 * n),
                      collective_id, name, parts)


def _sum_slots(slots, mine, chip):
    _, r, c = slots.shape
    tr = _row_tile(r, c, 16)

    def body(chip_ref, s0, s1, s2, s3, own_ref, out_ref):
        own = own_ref[...].astype(F32)
        v = [jnp.where(chip_ref[0] == s, own, ref[...].astype(F32)) for s, ref in enumerate((s0, s1, s2, s3))]
        out_ref[...] = ((v[0] + v[1]) + v[2]) + v[3]

    def slot_spec(s):
        return pl.BlockSpec((None, tr, c), lambda i, kr: (jnp.where(kr[0] == s, (s + 1) % N_SHARDS, s), i, 0))

    grid_spec = pltpu.PrefetchScalarGridSpec(
        num_scalar_prefetch=1, grid=(r // tr,),
        in_specs=[slot_spec(s) for s in range(N_SHARDS)] + [pl.BlockSpec((None, tr, c), lambda i, kr: (kr[0], i, 0))],
        out_specs=pl.BlockSpec((tr, c), lambda i, kr: (i, 0)))
    return pl.pallas_call(body, out_shape=jax.ShapeDtypeStruct((r, c), F32), grid_spec=grid_spec, name="grad_sum_slots",
                          compiler_params=_params(("parallel",)))(chip, slots, slots, slots, slots, mine)


def _join_halves(halves, collective_id, name):
    n = len(halves)

    def body(*refs):
        ins, outs = refs[:n], refs[n:2 * n]
        send, recv = refs[2 * n:]
        x, y, c, _ = _place()
        _handshake([(x, y, 1 - c)])
        cps = [pltpu.make_async_remote_copy(ins[b], outs[b], send.at[b], recv.at[b], device_id=(x, y, 1 - c),
                                            device_id_type=_MESH) for b in range(n)]
        for cp in cps:
            cp.start()
        for cp in cps:
            cp.wait()

    return _sequencer(body, [jax.ShapeDtypeStruct(h.shape, F32) for h in halves], _dma_sems(n, n), collective_id,
                      name, halves)


def _gather_rows(buf, start, rows):
    def body(in_ref, out_ref, send, recv, lsem):
        x, y, c, chips = _place()
        k = 2 * x + y
        src = in_ref.at[pl.ds(start, rows)]
        local = pltpu.make_async_remote_copy(src, out_ref.at[k], lsem.at[0], lsem.at[1], device_id=(x, y, 1 - c),
                                             device_id_type=_MESH)
        local.start()
        cps = [pltpu.make_async_remote_copy(src, out_ref.at[k], send.at[j], recv.at[j], device_id=(px, py, c),
                                            device_id_type=_MESH) for j, (px, py) in enumerate(chips)]
        for cp in cps:
            cp.start()
        for j, (px, py) in enumerate(chips):
            got = out_ref.at[2 * px + py]
            pltpu.make_async_remote_copy(got, got, send.at[j], recv.at[j], device_id=(px, py, c),
                                         device_id_type=_MESH).wait_recv()
        for cp in cps:
            cp.wait_send()
        local.wait()

    return pl.pallas_call(body, out_shape=jax.ShapeDtypeStruct((N_SHARDS, rows, buf.shape[1]), F32),
                          in_specs=[_ANY], out_specs=_ANY, scratch_shapes=_dma_sems(3, 3, 2),
                          name="gather_replicated_grads")(buf)


def _all_sum(vec):
    r, c = vec.shape
    n_dev = 2 * N_SHARDS

    def body(in_ref, out_ref, slots, send, recv):
        x, y, cc, _ = _place()
        flip = lambda v, bit: 1 - v if bit else v
        peers = [(flip(x, (q >> 2) & 1), flip(y, (q >> 1) & 1), flip(cc, q & 1)) for q in range(1, n_dev)]
        index = lambda p: 4 * p[0] + 2 * p[1] + p[2]
        slots[index((x, y, cc))] = in_ref[...]
        cps = [pltpu.make_async_remote_copy(in_ref, slots.at[index((x, y, cc))], send.at[q], recv.at[q], device_id=p,
                                            device_id_type=_MESH) for q, p in enumerate(peers)]
        for cp in cps:
            cp.start()
        for q, p in enumerate(peers):
            got = slots.at[index(p)]
            pltpu.make_async_remote_copy(got, got, send.at[q], recv.at[q], device_id=p, device_id_type=_MESH).wait_recv()
        for cp in cps:
            cp.wait_send()
        acc = slots[0]
        for s in range(1, n_dev):
            acc = acc + slots[s]
        out_ref[...] = acc

    vmem = pl.BlockSpec(memory_space=pltpu.VMEM)
    return pl.pallas_call(body, out_shape=jax.ShapeDtypeStruct((r, c), F32), in_specs=[vmem], out_specs=vmem,
                          scratch_shapes=[pltpu.VMEM((n_dev, r, c), F32)] + _dma_sems(n_dev - 1, n_dev - 1),
                          name="sum_small_grads")(vec)


def _not_before(value, other):
    return lax.optimization_barrier((value, other))[0]


def _round_up(n, m):
    return -(-n // m) * m


def _pack_flat(vecs, rows, width, dtype):
    flat = jnp.concatenate([v.reshape(-1).astype(dtype) for v in vecs])
    return jnp.pad(flat, (0, rows * width - flat.size)).reshape(rows, width)


def _split_flat(flat, shapes):
    out, off = [], 0
    for s in shapes:
        n = math.prod(s)
        out.append(flat[off:off + n].reshape(s))
        off += n
    return out


def _merge_shards(arr4, axis):
    a = jnp.moveaxis(arr4, 0, axis)
    s = list(a.shape)
    return a.reshape(s[:axis] + [s[axis] * s[axis + 1]] + s[axis + 2:])


def _split_shards(full, axis):
    s = list(full.shape)
    a = full.reshape(s[:axis] + [N_SHARDS, s[axis] // N_SHARDS] + s[axis + 1:])
    return jnp.moveaxis(a, axis, 0).reshape(N_SHARDS, -1)


def _rot_cols(w):
    half = w.shape[-1] // 2
    return jnp.concatenate([-w[..., half:], w[..., :half]], axis=-1)


def _unrot_cols(dw):
    half = dw.shape[-1] // 2
    return jnp.concatenate([dw[..., half:], -dw[..., :half]], axis=-1)


def kernel(x, positions, ffn_pre_g, ffn_post_g, ffn_w_gate, ffn_w_up, ffn_w_down, mix_pre_g, mix_post_g, gmlp_w_in, gmlp_ln_g, gmlp_ln_b, gmlp_w_s, gmlp_b_s, gmlp_w_out, kv_norm_g, w_dkv, kv_a_norm_g, w_ukv, mla_w_dq, mla_q_norm_g, mla_w_uq, mla_w_o, loss_target, m_ffn_pre_g, m_ffn_post_g, m_ffn_w_gate, m_ffn_w_up, m_ffn_w_down, m_mix_pre_g, m_mix_post_g, m_gmlp_w_in, m_gmlp_ln_g, m_gmlp_ln_b, m_gmlp_w_s, m_gmlp_b_s, m_gmlp_w_out, m_kv_norm_g, m_w_dkv, m_kv_a_norm_g, m_w_ukv, m_mla_w_dq, m_mla_q_norm_g, m_mla_w_uq, m_mla_w_o, v_ffn_pre_g, v_ffn_post_g, v_ffn_w_gate, v_ffn_w_up, v_ffn_w_down, v_mix_pre_g, v_mix_post_g, v_gmlp_w_in, v_gmlp_ln_g, v_gmlp_ln_b, v_gmlp_w_s, v_gmlp_b_s, v_gmlp_w_out, v_kv_norm_g, v_w_dkv, v_kv_a_norm_g, v_w_ukv, v_mla_w_dq, v_mla_q_norm_g, v_mla_w_uq, v_mla_w_o):
    names = ["ffn_pre_g", "ffn_post_g", "ffn_w_gate", "ffn_w_up", "ffn_w_down", "mix_pre_g", "mix_post_g", "gmlp_w_in",
             "gmlp_ln_g", "gmlp_ln_b", "gmlp_w_s", "gmlp_b_s", "gmlp_w_out", "kv_norm_g", "w_dkv", "kv_a_norm_g", "w_ukv",
             "mla_w_dq", "mla_q_norm_g", "mla_w_uq", "mla_w_o"]
    env = locals()
    w = {n: env[n] for n in names}
    mom = {n: env["m_" + n] for n in names}
    var = {n: env["v_" + n] for n in names}

    bsz, seq, d = x.shape
    t = bsz * seq
    core = lax.axis_index("c").astype(jnp.int32).reshape(1)

    mats = [("gmlp_w_in", 2), ("gmlp_w_out", 1), ("w_dkv", 0), ("w_ukv", 1), ("mla_w_dq", 1), ("mla_w_uq", 2),
            ("mla_w_o", 1)]
    vecs = [("ffn_pre_g", 2), ("ffn_post_g", 2), ("gmlp_ln_g", 1), ("gmlp_ln_b", 1)]
    replicated = ["mix_pre_g", "mix_post_g", "gmlp_w_s", "gmlp_b_s", "kv_norm_g", "kv_a_norm_g", "mla_q_norm_g"]
    n_mats = sum(w[n].size for n, _ in mats)
    n_vecs = sum(w[n].size for n, _ in vecs)
    mat_rows = _round_up(-(-n_mats // PACK_WIDTH), 64)
    vec_rows = _round_up(-(-n_vecs // 128), 32)
    mat_pack = _pack_flat([w[n] for n, _ in mats], mat_rows, PACK_WIDTH, BF16).reshape(2, 2, mat_rows // 4, PACK_WIDTH)
    vec_pack = _pack_flat([w[n] for n, _ in vecs], vec_rows, 128, F32).reshape(2, 2, vec_rows // 4, 128)
    ffn_names = ("ffn_w_gate", "ffn_w_up", "ffn_w_down")
    lj = [(l, j) for l in range(2) for j in range(2)]
    riders = {(0, 0): [vec_pack], (0, 1): [mat_pack], (1, 0): [], (1, 1): []}
    ffn_w = {}
    landed = []
    for q, (l, j) in enumerate(lj):
        shards = [w[n][l, j].astype(BF16) for n in ffn_names]
        bufs = [s.reshape(2, 2, s.shape[0] // 4, s.shape[1]) for s in shards] + riders[(l, j)]
        if q > 0:
            bufs = _not_before(bufs, landed[0 if q < 3 else 1])
        got = _all_gather(bufs, q + 1, f"gather_weights_{q}")
        landed.append(got[-1])
        ffn_w[(l, j)] = [g.reshape((N_SHARDS,) + s.shape) for g, s in zip(got, shards)]
        if (l, j) == (0, 0):
            vec_all = got[3]
        if (l, j) == (0, 1):
            mat_all = got[3]

    def unpack(packed, entries):
        flat4, off, out = packed.reshape(N_SHARDS, -1), 0, {}
        for n, ax in entries:
            out[n] = _merge_shards(flat4[:, off:off + w[n].size].reshape((N_SHARDS,) + w[n].shape), ax)
            off += w[n].size
        return out

    full = unpack(vec_all, vecs)
    ln_g, ln_b = full["gmlp_ln_g"], full["gmlp_ln_b"]
    pre_g, post_g = full["ffn_pre_g"], full["ffn_post_g"]
    w_s = w["gmlp_w_s"][0]
    bsb = w["gmlp_b_s"][0][:, :, None]
    row = lambda v: v.reshape(1, -1)

    inv_freq = ROPE_THETA ** (-jnp.arange(0, QK_ROPE, 2, dtype=F32) / QK_ROPE)
    ang = positions.astype(F32).reshape(t, 1) * inv_freq
    cos2 = jnp.concatenate([jnp.cos(ang)] * 2, axis=-1)
    sin2 = jnp.concatenate([jnp.sin(ang)] * 2, axis=-1)
    cos_h, sin_h = jnp.tile(cos2, (1, N_HEADS)), jnp.tile(sin2, (1, N_HEADS))

    def rope_epi(n_lin):
        def epi(accs, ex):
            return accs[:n_lin] + [accs[n_lin] * ex[0] + accs[n_lin + 1] * ex[1]]
        return epi

    h0 = x.reshape(t, d)
    saved = {}

    def ffn_fwd(l, j, h, n, next_gs):
        wg, wu, wd = ffn_w[(l, j)]
        g, u, a = _ffn_up(n, wg, wu)
        f, h_new, *n_next = _ffn_down(a, wd, h, row(post_g[l, j]), next_gs)
        saved[("ffn", l, j)] = (h, n, g, u, a, f)
        return h_new, n_next

    n0 = _rms_fwd(h0, row(pre_g[0, 0]))
    h1, (n1,) = ffn_fwd(0, 0, h0, n0, row(w["mix_pre_g"][0]))

    full.update(unpack(_not_before(mat_all, h1), mats))
    w_in, w_out = full["gmlp_w_in"][0], full["gmlp_w_out"][0]
    w_c, w_kr = full["w_dkv"][:, :KV_RANK], full["w_dkv"][:, KV_RANK:]
    w_kr_rot = _rot_cols(w_kr)
    ukv = full["w_ukv"].reshape(KV_RANK, N_HEADS, 2, QK_NOPE)
    w_k, w_v = ukv[:, :, 0].reshape(KV_RANK, -1), ukv[:, :, 1].reshape(KV_RANK, -1)
    w_dq, w_o = full["mla_w_dq"][0], full["mla_w_o"][0]
    q_rank = w_dq.shape[1]
    uq = full["mla_w_uq"][0].reshape(q_rank, N_HEADS, QK_NOPE + QK_ROPE)
    w_qn = uq[:, :, :QK_NOPE].reshape(q_rank, -1)
    w_qr = uq[:, :, QK_NOPE:].reshape(q_rank, -1)
    w_qr_rot = _rot_cols(uq[:, :, QK_NOPE:]).reshape(q_rank, -1)

    zp = _mm2d("gmlp_in", [(n1, w_in, "nn", 0)], [(w_in.shape[1], F32)])[0]
    uv = _sgu_fwd(zp, ln_g, ln_b, w_s, bsb)
    half = uv.shape[1]
    tm = min(t, ROW_TILE)
    m0, h2, n2 = _down("gmlp_out", (uv, (tm, 512), lambda i, _, k: (i, k)), (w_out, (512, d), lambda i, _, k: (k, 0)),
                       half // 512, h1, row(w["mix_post_g"][0]), row(pre_g[0, 1]), 1.0)
    h3, (n3kv, n3) = ffn_fwd(0, 1, h2, n2, jnp.stack([w["kv_norm_g"], pre_g[1, 0]]))

    def kv_epi(accs, ex):
        c_raw = accs[0]
        return [c_raw, _rms(c_raw, ex[2]), accs[1] * ex[0] + accs[2] * ex[1]]

    c_raw, c_n, k_r = _mm2d("kv_down", [(n3kv, w_c, "nn", 0), (n3kv, w_kr, "nn", 1), (n3kv, w_kr_rot, "nn", 2)],
                            [(KV_RANK, F32), (KV_RANK, BF16), (QK_ROPE, BF16)], kv_epi, [cos2, sin2],
                            [row(w["kv_a_norm_g"])])
    k_n, v_h = _mm2d("kv_up", [(c_n, w_k, "nn", 0), (c_n, w_v, "nn", 1)], [(w_k.shape[1], BF16), (w_v.shape[1], BF16)])

    h4, (n4,) = ffn_fwd(1, 0, h3, n3, row(w["mix_pre_g"][1]))
    qd, qn = _mm2d("q_down", [(n4, w_dq, "nn", 0)], [(q_rank, F32), (q_rank, BF16)],
                   lambda accs, ex: [accs[0], _rms(accs[0], ex[0])], [], [row(w["mla_q_norm_g"][0])])
    q_n, q_r = _mm2d("q_up", [(qn, w_qn, "nn", 0), (qn, w_qr, "nn", 1), (qn, w_qr_rot, "nn", 2)],
                     [(w_qn.shape[1], BF16), (w_qr.shape[1], BF16)], rope_epi(1), [cos_h, sin_h])
    q_r = q_r.reshape(t, N_HEADS, QK_ROPE).transpose(1, 0, 2)
    o, lse = _attn_fwd(q_n, q_r, k_n, v_h, k_r, seq)
    m1, h5, n5 = _down("attn_out", (o, (tm, 512), lambda i, _, k: (i, k)), (w_o, (512, d), lambda i, _, k: (k, 0)),
                       o.shape[1] // 512, h4, row(w["mix_post_g"][1]), row(pre_g[1, 1]), 1.0)
    y, _ = ffn_fwd(1, 1, h5, n5, row(pre_g[1, 1]))

    loss_part, dy = _loss_head(y, loss_target.reshape(t, d))
    loss = lax.psum(loss_part, ("x", "y", "c"))

    chip = (2 * lax.axis_index("x") + lax.axis_index("y")).astype(jnp.int32).reshape(1)
    rs = {}

    def rs_launch(gid, parts):
        rs[gid] = {"parts": parts, "others": _swap_halves(parts, 5 + gid, f"grad_swap_{gid}")}

    def rs_mid(gid, after):
        r = rs[gid]
        parts, others = _not_before((r["parts"], r["others"]), after)
        r["chip"] = [_add_half(p, o, core) for p, o in zip(parts, others)]
        r["slots"] = _scatter_chips(r["chip"], 10 + gid, f"grad_scatter_{gid}")
        return r["chip"]

    def rs_end(gid, after):
        r = rs[gid]
        slots, mine = _not_before((r["slots"], r["chip"]), after)
        r["own"] = [_sum_slots(s, p, chip) for s, p in zip(slots, mine)]
        r["recv"] = _join_halves(r["own"], 15 + gid, f"grad_join_{gid}")
        return r["own"]

    d_pre, d_post = {}, {}

    def ffn_bwd(l, j, gid, dh_out, extra=(), after_post=None):
        h, n, g, u, a, f = saved[("ffn", l, j)]
        df, d_post[(l, j)] = _norm_out_bwd("ffn_post_bwd", f, dh_out, row(post_g[l, j]), 0.5)
        if after_post is not None:
            df = after_post(df)
        wg, wu, wd = ffn_w[(l, j)]
        dg, du = _ffn_dact(df, wd, g, u)
        dwd = _ffn_dw_down(a, df)
        dwg, dwu = _ffn_dw_in(n, dg, du)
        parts = [p.reshape(N_SHARDS, 2, p.shape[1] // 2, p.shape[2]) for p in (dwg, dwu, dwd)]
        rs_launch(gid, parts)
        dg, du = _not_before((dg, du), parts)
        dn = _ffn_dn(dg, du, wg, wu)
        dh, d_pre[(l, j)], *rest = _norm_in_bwd("ffn_pre_bwd", h, dh_out, [(row(pre_g[l, j]), dn)] + list(extra))
        return dh, rest

    dh5, _ = ffn_bwd(1, 1, 0, dy)
    dh5 = _not_before(dh5, rs_mid(0, dh5))

    dm1, g_mix_post1 = _norm_out_bwd("mix_post_bwd", m1, dh5, row(w["mix_post_g"][1]), 1.0)
    do = _mm2d("attn_out_dx", [(dm1, w_o, "nt", 0)], [(w_o.shape[0], BF16)])[0]
    g_w_o = _mm2d("attn_out_dw", [(o, dm1, "tn", 0)], [(d, BF16)])[0]
    dq_n, dk_n, dv_h, dq_c, dq_s, dk_r = _attn_bwd(q_n, q_r, k_n, v_h, k_r, do, lse, cos2, sin2, seq)
    dq_c = dq_c.transpose(1, 0, 2).reshape(t, -1)
    dq_s = dq_s.transpose(1, 0, 2).reshape(t, -1)
    dqn = _mm2d("q_up_dx", [(dq_n, w_qn, "nt", 0), (dq_c, w_qr, "nt", 0), (dq_s, w_qr_rot, "nt", 0)], [(q_rank, F32)])[0]
    g_qn, g_qr, g_qr_rot = _mm2d("q_up_dw", [(qn, dq_n, "tn", 0), (qn, dq_c, "tn", 1), (qn, dq_s, "tn", 2)],
                                 [(w_qn.shape[1], F32), (w_qr.shape[1], F32), (w_qr.shape[1], F32)])
    dqd, g_q_norm = _norm_out_bwd("q_norm_bwd", qd, dqn, row(w["mla_q_norm_g"][0]), 1.0)
    dn4 = _mm2d("q_down_dx", [(dqd, w_dq, "nt", 0)], [(d, F32)])[0]
    g_w_dq = _mm2d("q_down_dw", [(n4, dqd, "tn", 0)], [(q_rank, BF16)])[0]
    dh4, g_mix_pre1 = _norm_in_bwd("mix_pre_bwd", h4, dh5, [(row(w["mix_pre_g"][1]), dn4)])

    dc_n = _mm2d("kv_up_dx", [(dk_n, w_k, "nt", 0), (dv_h, w_v, "nt", 0)], [(KV_RANK, F32)])[0]
    g_wk, g_wv = _mm2d("kv_up_dw", [(c_n, dk_n, "tn", 0), (c_n, dv_h, "tn", 1)], [(w_k.shape[1], F32), (w_v.shape[1], F32)])
    dc, g_kv_a = _norm_out_bwd("kv_a_norm_bwd", c_raw, dc_n, row(w["kv_a_norm_g"]), 1.0)
    dkr_c, dkr_s = _rope_bwd(dk_r, cos2, sin2)
    dn3kv = _mm2d("kv_down_dx", [(dc, w_c, "nt", 0), (dkr_c, w_kr, "nt", 0), (dkr_s, w_kr_rot, "nt", 0)], [(d, F32)])[0]
    g_wc, g_wkr, g_wkr_rot = _mm2d("kv_down_dw", [(n3kv, dc, "tn", 0), (n3kv, dkr_c, "tn", 1), (n3kv, dkr_s, "tn", 2)],
                                   [(KV_RANK, F32), (QK_ROPE, F32), (QK_ROPE, F32)])

    dh4 = _not_before(dh4, rs_end(0, dh4))
    dh3, (g_kv_norm,) = ffn_bwd(1, 0, 1, dh4, extra=[(row(w["kv_norm_g"]), dn3kv)])
    dh3 = _not_before(dh3, rs_mid(1, dh3))
    dh2, _ = ffn_bwd(0, 1, 2, dh3)
    dh2 = _not_before(dh2, (rs_end(1, dh2), rs_mid(2, dh2)))

    dm0, g_mix_post0 = _norm_out_bwd("mix_post_bwd", m0, dh2, row(w["mix_post_g"][0]), 1.0)
    d_uv = _mm2d("gmlp_out_dx", [(dm0, w_out, "nt", 0)], [(half, F32)])[0]
    g_w_out = _mm2d("gmlp_out_dw", [(uv, dm0, "tn", 0)], [(d, BF16)])[0]
    dzp, g_ln_g, g_ln_b, g_w_s, g_b_s = _sgu_bwd(zp, d_uv, ln_g, ln_b, w_s, bsb)
    dn1 = _mm2d("gmlp_in_dx", [(dzp, w_in, "nt", 0)], [(d, F32)])[0]
    tk, tmw, w_cols = min(t, K_TILE), min(d, ROW_TILE), w_in.shape[1] // N_SHARDS
    g_w_in = _mm("gmlp_in_dw", (d // tmw, N_SHARDS, t // tk),
                 [(n1, (tk, tmw), lambda i, j, k: (k, i)), (dzp, (tk, w_cols), lambda i, j, k: (k, j))],
                 [(0, 1, 0, "tn")], [(tmw, w_cols)],
                 [((N_SHARDS, d, w_cols), BF16, (None, tmw, w_cols), lambda i, j, k: (j, i, 0))], lambda accs, ex: accs)[0]
    dh1, g_mix_pre0 = _norm_in_bwd("mix_pre_bwd", h1, dh2, [(row(w["mix_pre_g"][0]), dn1)])

    g_w_dkv = jnp.concatenate([g_wc, g_wkr + _unrot_cols(g_wkr_rot)], axis=1).astype(BF16)
    direct = {"gmlp_w_in": g_w_in, "gmlp_w_out": g_w_out, "mla_w_o": g_w_o, "mla_w_dq": g_w_dq, "w_dkv": g_w_dkv}
    direct = {n: g.reshape(N_SHARDS, -1, g.shape[-1]) for n, g in direct.items()}
    part = {
        "w_ukv": jnp.stack([g_wk.reshape(KV_RANK, N_HEADS, QK_NOPE), g_wv.reshape(KV_RANK, N_HEADS, V_DIM)],
                           axis=2).reshape(KV_RANK, -1),
        "mla_w_uq": jnp.concatenate(
            [g_qn.reshape(q_rank, N_HEADS, QK_NOPE),
             g_qr.reshape(q_rank, N_HEADS, QK_ROPE) + _unrot_cols(g_qr_rot.reshape(q_rank, N_HEADS, QK_ROPE))],
            axis=-1).reshape(1, q_rank, -1),
        "gmlp_ln_g": g_ln_g, "gmlp_ln_b": g_ln_b,
        "mix_pre_g": jnp.concatenate([g_mix_pre0, g_mix_pre1]), "mix_post_g": jnp.concatenate([g_mix_post0, g_mix_post1]),
        "gmlp_w_s": g_w_s[None], "gmlp_b_s": g_b_s.reshape(1, GROUPS, CHUNK),
        "kv_norm_g": g_kv_norm.reshape(-1), "kv_a_norm_g": g_kv_a.reshape(-1), "mla_q_norm_g": g_q_norm,
    }

    sharded = [e for e in mats + vecs if e[0] in part]
    n_sh = sum(w[n].size for n, _ in sharded)
    n_rep = sum(w[n].size for n in replicated)
    sh_rows = _round_up(-(-n_sh // PACK_WIDTH), 8)
    rep_rows = _round_up(-(-(n_rep // N_SHARDS) // PACK_WIDTH), 8)
    rows = _round_up(sh_rows + rep_rows, 32)
    sh_flat = jnp.concatenate([_split_shards(part[n], ax) for n, ax in sharded], axis=1)
    rep_flat = jnp.concatenate([part[n].reshape(-1) for n in replicated]).reshape(N_SHARDS, -1)
    small = jnp.concatenate([
        jnp.pad(sh_flat, ((0, 0), (0, sh_rows * PACK_WIDTH - n_sh))),
        jnp.pad(rep_flat, ((0, 0), (0, (rows - sh_rows) * PACK_WIDTH - n_rep // N_SHARDS)))], axis=1)
    small = small.astype(BF16).reshape(N_SHARDS, 2, rows // 2, PACK_WIDTH)

    rs_launch(3, [g.reshape(N_SHARDS, 2, g.shape[1] // 2, g.shape[2]) for g in direct.values()] + [small])
    dh1 = _not_before(dh1, rs_end(2, dh1))
    dx, _ = ffn_bwd(0, 0, 4, dh1, after_post=lambda df: _not_before(df, rs_mid(3, df)))

    launched = rs_mid(4, dx)
    lj = [(l, j) for l in range(2) for j in range(2)]
    tiny = jnp.concatenate([d_pre[k] for k in lj] + [d_post[k] for k in lj]).reshape(-1, 128)
    tiny = _all_sum(_not_before(tiny, launched)).reshape(2, 2, 2, d)
    shard_cols = d // N_SHARDS
    grads = {"ffn_pre_g": lax.dynamic_slice_in_dim(tiny[0], chip[0] * shard_cols, shard_cols, axis=2),
             "ffn_post_g": lax.dynamic_slice_in_dim(tiny[1], chip[0] * shard_cols, shard_cols, axis=2)}
    own_small, recv_small = rs_end(3, launched)[-1], rs[3]["recv"][-1]
    delta, new_m, new_v = {}, {}, {}
    for q, n in enumerate(direct):
        lead = lambda a: a.reshape((1, 1) + a.shape[-2:])
        upd = _adamw_halves(lead(w[n]), lead(mom[n]), lead(var[n]), 0, 0, rs[3]["own"][q], rs[3]["recv"][q], core, None)
        grads[n], delta[n], new_m[n], new_v[n] = [o.reshape(w[n].shape) for o in upd]
    g_small = jnp.where(core[0] == 0, jnp.concatenate([own_small, recv_small]), jnp.concatenate([recv_small, own_small]))
    g_rep = _gather_rows(g_small, sh_rows, rep_rows)
    for (n, _), g in zip(sharded, _split_flat(g_small.reshape(-1), [w[n].shape for n, _ in sharded])):
        grads[n] = g
    rep_vec = g_rep.reshape(N_SHARDS, -1)[:, :n_rep // N_SHARDS].reshape(-1)
    for n, g in zip(replicated, _split_flat(rep_vec, [w[n].shape for n in replicated])):
        grads[n] = g

    for n in names:
        if n not in ffn_names and n not in delta:
            delta[n], new_m[n], new_v[n] = _adamw(w[n], grads[n], mom[n], var[n])
    chain = {n: None for n in ffn_names}

    def ffn_update(gid, l, j):
        for q, n in enumerate(ffn_names):
            chain[n] = _adamw_halves(w[n], mom[n], var[n], l, j, rs[gid]["own"][q], rs[gid]["recv"][q], core, chain[n])

    ffn_update(0, 1, 1)
    ffn_update(1, 1, 0)
    ffn_update(2, 0, 1)
    rs_end(4, ([delta[n] for n in delta], [chain[n] for n in ffn_names]))
    ffn_update(4, 0, 0)
    for n in ffn_names:
        grads[n], delta[n], new_m[n], new_v[n] = chain[n]
    return (loss, dx.reshape(x.shape), *[grads[n] for n in names], *[delta[n] for n in names],
            *[new_m[n] for n in names], *[new_v[n] for n in names])
```

```python
import math

import jax
import jax.numpy as jnp
from jax import lax
from jax.experimental import pallas as pl
from jax.experimental.pallas import tpu as pltpu
from jax.experimental.pallas import tpu_sc as plsc

F32, BF16 = jnp.float32, jnp.bfloat16

RMS_EPS, LN_EPS, NEG_INF = 1e-6, 1e-5, -1e30
N_HEADS, QK_NOPE, QK_ROPE, V_DIM, KV_RANK = 8, 128, 64, 128, 256
CHUNK, GROUPS = 128, 16
ROPE_THETA = 10000.0
ADAM_LR, ADAM_B1, ADAM_B2, ADAM_EPS, ADAM_WD, ADAM_STEP = 0.001, 0.9, 0.999, 1e-08, 0.01, 10
N_SHARDS = 4

VMEM_LIMIT_BYTES = 48 * 1024 * 1024
ROW_TILE = 512
K_TILE = 2048
PACK_WIDTH = 1024

_DN = {"nn": (((1,), (0,)), ((), ())), "nt": (((1,), (1,)), ((), ())), "tn": (((0,), (0,)), ((), ()))}
_MESH = pl.DeviceIdType.MESH
_ANY = pl.BlockSpec(memory_space=pl.ANY)


def _params(sem):
    return pltpu.CompilerParams(dimension_semantics=sem, vmem_limit_bytes=VMEM_LIMIT_BYTES)


def _mm(name, grid, ins, pairs, acc_shapes, outs, epilogue, extras=(), inner=0):
    n_in, n_ex, n_out = len(ins), len(extras), len(outs)
    gk = grid[2]

    def body(*refs):
        in_refs, ex_refs = refs[:n_in], refs[n_in:n_in + n_ex]
        out_refs = refs[n_in + n_ex:n_in + n_ex + n_out]
        acc_refs = refs[n_in + n_ex + n_out:]
        parts = [None] * len(acc_shapes)
        for a, b, c, dims in pairs:
            for s in range(max(inner, 1)):
                lhs, rhs = (in_refs[a][s], in_refs[b][s]) if inner else (in_refs[a][...], in_refs[b][...])
                p = lax.dot_general(lhs, rhs, _DN[dims], preferred_element_type=F32)
                parts[c] = p if parts[c] is None else parts[c] + p

        def finish(accs):
            vals = epilogue(accs, [r[...] for r in ex_refs])
            for r, v in zip(out_refs, vals):
                r[...] = v.astype(r.dtype)

        if gk == 1:
            finish(parts)
        else:
            k = pl.program_id(2)

            @pl.when(k == 0)
            def _():
                for r, p in zip(acc_refs, parts):
                    r[...] = p

            @pl.when(k > 0)
            def _():
                for r, p in zip(acc_refs, parts):
                    r[...] += p

            @pl.when(k == gk - 1)
            def _():
                finish([r[...] for r in acc_refs])

    return pl.pallas_call(
        body,
        out_shape=[jax.ShapeDtypeStruct(s, d) for s, d, _, _ in outs],
        grid=grid,
        in_specs=[pl.BlockSpec(bs, im) for _, bs, im in list(ins) + list(extras)],
        out_specs=[pl.BlockSpec(bs, im) for _, _, bs, im in outs],
        scratch_shapes=[pltpu.VMEM(s, F32) for s in acc_shapes] if gk > 1 else [],
        name=name,
        compiler_params=_params(("parallel", "parallel", "arbitrary")),
    )(*[a for a, _, _ in ins], *[a for a, _, _ in extras])


def _mm2d(name, pairs, outs, epilogue=None, row_extras=(), vec_extras=()):
    def mk(a, dims):
        return (a.shape[0], a.shape[1]) if dims[0] == "n" else (a.shape[1], a.shape[0])

    def nk(b, dims):
        return (b.shape[1], b.shape[0]) if dims[1] == "n" else (b.shape[0], b.shape[1])

    m = mk(pairs[0][0], pairs[0][2])[0]
    ks = [mk(a, d)[1] for a, _, d, _ in pairs]
    n_acc = 1 + max(p[3] for p in pairs)
    acc_n = [None] * n_acc
    for a, b, d, c in pairs:
        assert mk(a, d)[0] == m and nk(b, d)[1] == mk(a, d)[1]
        acc_n[c] = nk(b, d)[0]
    tm = min(m, ROW_TILE)
    if len(set(ks)) == 1 and ks[0] > 1024:
        tk = K_TILE if ks[0] % K_TILE == 0 else 512
        tks, gk = [tk] * len(pairs), ks[0] // tk
    else:
        tks, gk = ks, 1
    if len(set(acc_n)) == 1 and acc_n[0] > 1024:
        tns, gj = [1024] * n_acc, acc_n[0] // 1024
    else:
        tns, gj = acc_n, 1

    ins, plist = [], []
    for (a, b, d, c), tk in zip(pairs, tks):
        tn = tns[c]
        a_spec = ((tm, tk), lambda i, j, k: (i, k)) if d[0] == "n" else ((tk, tm), lambda i, j, k: (k, i))
        b_spec = ((tk, tn), lambda i, j, k: (k, j)) if d[1] == "n" else ((tn, tk), lambda i, j, k: (j, k))
        ins += [(a, *a_spec), (b, *b_spec)]
        plist.append((len(ins) - 2, len(ins) - 1, c, d))
    extras = [(r, (tm, r.shape[1]), lambda i, j, k: (i, 0)) for r in row_extras]
    extras += [(v, v.shape, lambda i, j, k: (0, 0)) for v in vec_extras]
    out_specs = []
    for n, dt in outs:
        bn = 1024 if (gj > 1) else n
        out_specs.append(((m, n), dt, (tm, bn), lambda i, j, k: (i, j)))
    if epilogue is None:
        epilogue = lambda accs, ex: accs
    return _mm(name, (m // tm, gj, gk), ins, plist, [(tm, tn) for tn in tns], out_specs, epilogue, extras)


def _rms(x, g):
    return x * lax.rsqrt(jnp.mean(x * x, axis=-1, keepdims=True) + RMS_EPS) * g


def _rms_bwd(x, g, dy):
    r = lax.rsqrt(jnp.mean(x * x, axis=-1, keepdims=True) + RMS_EPS)
    gy = dy * g
    dx = r * gy - x * (r * r * r) * jnp.mean(gy * x, axis=-1, keepdims=True)
    return dx, jnp.sum(dy * x * r, axis=0, keepdims=True)


def _sigmoid(x):
    return 1.0 / (1.0 + jnp.exp(-x))


_GELU_C = math.sqrt(2.0 / math.pi)


def _gelu(x):
    return x * (0.5 * (1.0 + jnp.tanh(_GELU_C * (x + 0.044715 * (x * x * x)))))


def _gelu_grad(x):
    t = jnp.tanh(_GELU_C * (x + 0.044715 * (x * x * x)))
    return 0.5 * (1.0 + t) + 0.5 * x * (1.0 - t * t) * (_GELU_C * (1.0 + 3.0 * 0.044715 * (x * x)))


def _rows(name, row_ins, vec_ins, fn, row_outs, acc_outs=()):
    t = row_ins[0].shape[0]
    tm = min(t, ROW_TILE)
    nr, nv, no = len(row_ins), len(vec_ins), len(row_outs)

    def body(*refs):
        outs, incs = fn([r[...] for r in refs[:nr]], [r[...] for r in refs[nr:nr + nv]])
        for r, v in zip(refs[nr + nv:nr + nv + no], outs):
            r[...] = v.astype(r.dtype)
        i = pl.program_id(0)
        for r, v in zip(refs[nr + nv + no:], incs):
            @pl.when(i == 0)
            def _():
                r[...] = v

            @pl.when(i > 0)
            def _():
                r[...] += v

    in_specs = [pl.BlockSpec((tm, a.shape[1]), lambda i: (i, 0)) for a in row_ins]
    in_specs += [pl.BlockSpec(v.shape, lambda i, nd=v.ndim: (0,) * nd) for v in vec_ins]
    out_shape = [jax.ShapeDtypeStruct((t, c), dt) for c, dt in row_outs]
    out_shape += [jax.ShapeDtypeStruct(s, F32) for s in acc_outs]
    out_specs = [pl.BlockSpec((tm, c), lambda i: (i, 0)) for c, _ in row_outs]
    out_specs += [pl.BlockSpec(s, lambda i, nd=len(s): (0,) * nd) for s in acc_outs]
    return pl.pallas_call(body, out_shape=out_shape, grid=(t // tm,), in_specs=in_specs, out_specs=out_specs,
                          name=name, compiler_params=_params(("arbitrary",)))(*row_ins, *vec_ins)


def _rms_fwd(x, g):
    return _rows("rms_fwd", [x], [g], lambda r, v: ([_rms(r[0], v[0])], []), [(x.shape[1], BF16)])[0]


def _norm_out_bwd(name, f, d_out, g, scale):
    def fn(r, v):
        dx, dg = _rms_bwd(r[0], v[0], r[1] * scale)
        return [dx], [dg]

    c = f.shape[1]
    return _rows(name, [f, d_out], [g], fn, [(c, BF16)], [(1, c)])


def _norm_in_bwd(name, h, d_res, branches):
    nb = len(branches)

    def fn(r, v):
        dh, dgs = r[1], []
        for b in range(nb):
            dx, dg = _rms_bwd(r[0], v[b], r[2 + b])
            dh = dh + dx
            dgs.append(dg)
        return [dh], dgs

    c = h.shape[1]
    return _rows(name, [h, d_res] + [dn for _, dn in branches], [g for g, _ in branches], fn, [(c, F32)],
                 [(1, c)] * nb)


def _loss_head(y, target):
    d = y.shape[1]

    def fn(r, v):
        e = r[0] - r[1]
        s = jnp.sum(jnp.sum(e * e, axis=1, keepdims=True), axis=0, keepdims=True) * (0.5 / d)
        return [e * (1.0 / d)], [jnp.broadcast_to(s, (1, 128))]

    dy, acc = _rows("loss_head", [y, target], [], fn, [(d, F32)], [(1, 128)])
    return acc[0, 0], dy


def _rope_bwd(dk, cos2, sin2):
    c = dk.shape[1]
    return _rows("rope_bwd", [dk, cos2, sin2], [], lambda r, v: ([r[0] * r[1], r[0] * r[2]], []),
                 [(c, BF16), (c, BF16)])


def _ffn_up(n, wg, wu):
    t, d = n.shape
    fs = wg.shape[-2]
    tm = min(t, ROW_TILE)
    w_spec = ((None, fs, d), lambda s, i, k: (s, 0, 0))

    def epi(accs, ex):
        g, u = accs
        return [g, u, g * _sigmoid(g) * u]

    o_spec = ((None, tm, fs), lambda s, i, k: (s, i, 0))
    outs = [((N_SHARDS, t, fs), BF16, *o_spec)] * 3
    return _mm("ffn_up", (N_SHARDS, t // tm, 1),
               [(n, (tm, d), lambda s, i, k: (i, 0)), (wg, *w_spec), (wu, *w_spec)],
               [(0, 1, 0, "nt"), (0, 2, 1, "nt")], [(tm, fs)] * 2, outs, epi)


def _down(name, a_in, w_in, gk, h, post_g, next_gs, scale, inner=0):
    t, d = h.shape
    tm = min(t, ROW_TILE)
    kn = next_gs.shape[0]

    def epi(accs, ex):
        f, hv, pg, ng = accs[0], ex[0], ex[1], ex[2]
        hn = hv + scale * _rms(f, pg)
        return [f, hn] + [_rms(hn, ng[q:q + 1]) for q in range(kn)]

    row = ((tm, d), lambda i, j, k: (i, 0))
    outs = [((t, d), F32, *row), ((t, d), F32, *row)] + [((t, d), BF16, *row)] * kn
    extras = [(h, *row), (post_g, (1, d), lambda i, j, k: (0, 0)), (next_gs, (kn, d), lambda i, j, k: (0, 0))]
    return _mm(name, (t // tm, 1, gk), [a_in, w_in], [(0, 1, 0, "nn")], [(tm, d)], outs, epi, extras, inner)


def _ffn_down(a, wd, h, post_g, next_gs):
    t, d = h.shape
    fs = a.shape[-1]
    tm = min(t, ROW_TILE)
    return _down("ffn_down", (a, (N_SHARDS, tm, fs), lambda i, _, k: (0, i, 0)),
                 (wd, (N_SHARDS, fs, d), lambda i, _, k: (0, 0, 0)), 1, h, post_g, next_gs, 0.5, N_SHARDS)


def _ffn_dact(df, wd, g, u):
    t, d = df.shape
    fs = g.shape[-1]
    tm = min(t, ROW_TILE)

    def epi(accs, ex):
        da, gv, uv = accs[0], ex[0].astype(F32), ex[1].astype(F32)
        sg = _sigmoid(gv)
        return [da * uv * (sg * (1.0 + gv * (1.0 - sg))), da * (gv * sg)]

    o_spec = ((None, tm, fs), lambda s, i, k: (s, i, 0))
    outs = [((N_SHARDS, t, fs), BF16, *o_spec)] * 2
    return _mm("ffn_dact", (N_SHARDS, t // tm, 1),
               [(df, (tm, d), lambda s, i, k: (i, 0)), (wd, (None, fs, d), lambda s, i, k: (s, 0, 0))],
               [(0, 1, 0, "nt")], [(tm, fs)], outs, epi, [(g, *o_spec), (u, *o_spec)])


def _ffn_dn(dg, du, wg, wu):
    _, t, fs = dg.shape
    d = wg.shape[-1]
    tm = min(t, ROW_TILE)
    a_spec = ((N_SHARDS, tm, fs), lambda i, _, k: (0, i, 0))
    w_spec = ((N_SHARDS, fs, d), lambda i, _, k: (0, 0, 0))
    outs = [((t, d), F32, (tm, d), lambda i, _, k: (i, 0))]
    return _mm("ffn_dn", (t // tm, 1, 1), [(dg, *a_spec), (wg, *w_spec), (du, *a_spec), (wu, *w_spec)],
               [(0, 1, 0, "nn"), (2, 3, 0, "nn")], [(tm, d)], outs, lambda accs, ex: accs, inner=N_SHARDS)[0]


def _ffn_dw_in(n, dg, du):
    _, t, fs = dg.shape
    d = n.shape[1]
    tk = min(t, K_TILE)
    a_spec = ((None, tk, fs), lambda s, _, k: (s, k, 0))
    o_spec = ((None, fs, d), lambda s, _, k: (s, 0, 0))
    outs = [((N_SHARDS, fs, d), BF16, *o_spec)] * 2
    return _mm("ffn_dw_in", (N_SHARDS, 1, t // tk), [(dg, *a_spec), (du, *a_spec), (n, (tk, d), lambda s, _, k: (k, 0))],
               [(0, 2, 0, "tn"), (1, 2, 1, "tn")], [(fs, d)] * 2, outs, lambda accs, ex: accs)


def _ffn_dw_down(a, df):
    _, t, fs = a.shape
    d = df.shape[1]
    tk = min(t, K_TILE)
    outs = [((N_SHARDS, fs, d), BF16, (None, fs, d), lambda s, _, k: (s, 0, 0))]
    return _mm("ffn_dw_down", (N_SHARDS, 1, t // tk),
               [(a, (None, tk, fs), lambda s, _, k: (s, k, 0)), (df, (tk, d), lambda s, _, k: (k, 0))],
               [(0, 1, 0, "tn")], [(fs, d)], outs, lambda accs, ex: accs)[0]


def _causal_weight(w):
    row = lax.broadcasted_iota(jnp.int32, (CHUNK, CHUNK), 0)
    col = lax.broadcasted_iota(jnp.int32, (CHUNK, CHUNK), 1)
    return row >= col, jnp.where(row >= col, w, 0.0).astype(BF16)


def _layer_norm(v, g, b):
    xc = v - jnp.mean(v, axis=-1, keepdims=True)
    rstd = lax.rsqrt(jnp.mean(xc * xc, axis=-1, keepdims=True) + LN_EPS)
    xhat = xc * rstd
    return xhat, rstd, xhat * g + b


def _sgu_specs(t, half, tm):
    return [pl.BlockSpec((tm, half), lambda i: (i, 0)), pl.BlockSpec((tm, half), lambda i: (i, 1))]


def _sgu_fwd(zp, ln_g, ln_b, w_s, bsb):
    t, half = zp.shape[0], zp.shape[1] // 2
    tm = min(t, 2 * CHUNK)

    def body(u_ref, v_ref, g_ref, b_ref, w_ref, bs_ref, o_ref):
        u = _gelu(u_ref[...])
        _, _, vn = _layer_norm(_gelu(v_ref[...]), g_ref[...], b_ref[...])
        vb = vn.astype(BF16)
        for g in range(GROUPS):
            _, wm = _causal_weight(w_ref[g])
            cols = slice(g * CHUNK, (g + 1) * CHUNK)
            for c in range(tm // CHUNK):
                rows = slice(c * CHUNK, (c + 1) * CHUNK)
                sv = jnp.dot(wm, vb[rows, cols], preferred_element_type=F32) + bs_ref[g]
                o_ref[rows, cols] = (u[rows, cols] * sv).astype(BF16)

    whole = lambda a: pl.BlockSpec(a.shape, lambda i, nd=a.ndim: (0,) * nd)
    return pl.pallas_call(
        body, out_shape=jax.ShapeDtypeStruct((t, half), BF16), grid=(t // tm,),
        in_specs=_sgu_specs(t, half, tm) + [whole(ln_g), whole(ln_b), whole(w_s), whole(bsb)],
        out_specs=pl.BlockSpec((tm, half), lambda i: (i, 0)), name="sgu_fwd",
        compiler_params=_params(("arbitrary",)))(zp, zp, ln_g, ln_b, w_s, bsb)


def _sgu_bwd(zp, d_uv, ln_g, ln_b, w_s, bsb):
    t, half = zp.shape[0], zp.shape[1] // 2
    tm = min(t, 2 * CHUNK)

    def body(u_ref, v_ref, d_ref, g_ref, b_ref, w_ref, bs_ref, dz_ref, dlg_ref, dlb_ref, dws_ref, dbs_ref, dvn_ref):
        i = pl.program_id(0)

        @pl.when(i == 0)
        def _():
            dlg_ref[...] = jnp.zeros_like(dlg_ref)
            dlb_ref[...] = jnp.zeros_like(dlb_ref)
            dws_ref[...] = jnp.zeros_like(dws_ref)
            dbs_ref[...] = jnp.zeros_like(dbs_ref)

        up, vp = u_ref[...], v_ref[...]
        u, gup = _gelu(up), _gelu_grad(up)
        xhat, rstd, vn = _layer_norm(_gelu(vp), g_ref[...], b_ref[...])
        vb = vn.astype(BF16)
        d = d_ref[...]
        for g in range(GROUPS):
            mask, wm = _causal_weight(w_ref[g])
            cols = slice(g * CHUNK, (g + 1) * CHUNK)
            for c in range(tm // CHUNK):
                rows = slice(c * CHUNK, (c + 1) * CHUNK)
                blk = vb[rows, cols]
                sv = jnp.dot(wm, blk, preferred_element_type=F32) + bs_ref[g]
                dblk = d[rows, cols]
                dz_ref[rows, cols] = (dblk * sv * gup[rows, cols]).astype(BF16)
                dsv = dblk * u[rows, cols]
                dsvb = dsv.astype(BF16)
                dvn_ref[rows, cols] = lax.dot_general(wm, dsvb, _DN["tn"], preferred_element_type=F32)
                dw = lax.dot_general(dsvb, blk, _DN["nt"], preferred_element_type=F32)
                dws_ref[g] += jnp.where(mask, dw, 0.0)
                dbs_ref[g] += jnp.sum(dsv, axis=1, keepdims=True)
        dvn = dvn_ref[...]
        dlg_ref[...] += jnp.sum(dvn * xhat, axis=0, keepdims=True)
        dlb_ref[...] += jnp.sum(dvn, axis=0, keepdims=True)
        dxh = dvn * g_ref[...]
        dv = rstd * (dxh - jnp.mean(dxh, axis=-1, keepdims=True)
                     - xhat * jnp.mean(dxh * xhat, axis=-1, keepdims=True))
        dz_ref[:, half:] = (dv * _gelu_grad(vp)).astype(BF16)

    whole = lambda a: pl.BlockSpec(a.shape, lambda i, nd=a.ndim: (0,) * nd)
    wshape = lambda s: pl.BlockSpec(s, lambda i, nd=len(s): (0,) * nd)
    out_shape = [jax.ShapeDtypeStruct((t, 2 * half), BF16), jax.ShapeDtypeStruct((1, half), F32),
                 jax.ShapeDtypeStruct((1, half), F32), jax.ShapeDtypeStruct(w_s.shape, F32),
                 jax.ShapeDtypeStruct((GROUPS, CHUNK, 1), F32)]
    return pl.pallas_call(
        body, out_shape=out_shape, grid=(t // tm,),
        in_specs=_sgu_specs(t, half, tm) + [pl.BlockSpec((tm, half), lambda i: (i, 0)), whole(ln_g), whole(ln_b),
                                            whole(w_s), whole(bsb)],
        out_specs=[pl.BlockSpec((tm, 2 * half), lambda i: (i, 0)), wshape((1, half)), wshape((1, half)),
                   wshape(w_s.shape), wshape((GROUPS, CHUNK, 1))],
        scratch_shapes=[pltpu.VMEM((tm, half), F32)], name="sgu_bwd",
        compiler_params=_params(("arbitrary",)))(zp, zp, d_uv, ln_g, ln_b, w_s, bsb)


_SCALE = (QK_NOPE + QK_ROPE) ** -0.5


def _attn_scores(qn, qr, kn, kr, i, tq, n):
    s = lax.dot_general(qn, kn, _DN["nt"], preferred_element_type=F32)
    s = (s + lax.dot_general(qr, kr, _DN["nt"], preferred_element_type=F32)) * _SCALE
    row = i * tq + lax.broadcasted_iota(jnp.int32, (tq, n), 0)
    col = lax.broadcasted_iota(jnp.int32, (tq, n), 1)
    return jnp.where(col <= row, s, NEG_INF)


def _attn_specs(seq):
    head = lambda b, h: (b, h)
    return dict(
        qn=pl.BlockSpec((seq, QK_NOPE), head),
        qr=pl.BlockSpec((None, seq, QK_ROPE), lambda b, h: (h, b, 0)),
        kr=pl.BlockSpec((seq, QK_ROPE), lambda b, h: (b, 0)),
        lse=pl.BlockSpec((None, seq, 1), lambda b, h: (h, b, 0)),
    )


def _attn_fwd(qn, qr, kn, v, kr, seq):
    t = qn.shape[0]
    tq = min(seq, 2 * CHUNK)
    sp = _attn_specs(seq)

    def body(qn_ref, qr_ref, kn_ref, v_ref, kr_ref, o_ref, lse_ref):
        for i in range(seq // tq):
            rows, n = slice(i * tq, (i + 1) * tq), (i + 1) * tq
            s = _attn_scores(qn_ref[rows, :], qr_ref[rows, :], kn_ref[0:n, :], kr_ref[0:n, :], i, tq, n)
            m = jnp.max(s, axis=-1, keepdims=True)
            p = jnp.exp(s - m)
            l = jnp.sum(p, axis=-1, keepdims=True)
            o_ref[rows, :] = jnp.dot((p / l).astype(BF16), v_ref[0:n, :], preferred_element_type=F32).astype(BF16)
            lse_ref[rows, :] = m + jnp.log(l)

    return pl.pallas_call(
        body, out_shape=[jax.ShapeDtypeStruct((t, N_HEADS * V_DIM), BF16), jax.ShapeDtypeStruct((N_HEADS, t, 1), F32)],
        grid=(t // seq, N_HEADS), in_specs=[sp["qn"], sp["qr"], sp["qn"], sp["qn"], sp["kr"]],
        out_specs=[sp["qn"], sp["lse"]], name="attn_fwd",
        compiler_params=_params(("parallel", "arbitrary")))(qn, qr, kn, v, kr)


def _attn_bwd(qn, qr, kn, v, kr, do, lse, cos2, sin2, seq):
    t = qn.shape[0]
    tq = min(seq, 2 * CHUNK)
    sp = _attn_specs(seq)

    def body(qn_ref, qr_ref, kn_ref, v_ref, kr_ref, do_ref, lse_ref, cos_ref, sin_ref,
             dqn_ref, dkn_ref, dv_ref, dqc_ref, dqs_ref, dkr_ref, dk_acc, dv_acc, dkr_acc):
        dk_acc[...] = jnp.zeros_like(dk_acc)
        dv_acc[...] = jnp.zeros_like(dv_acc)
        dkr_acc[...] = jnp.zeros_like(dkr_acc)
        for i in range(seq // tq):
            rows, n = slice(i * tq, (i + 1) * tq), (i + 1) * tq
            q_n, q_r, d_o = qn_ref[rows, :], qr_ref[rows, :], do_ref[rows, :]
            k_n, k_r = kn_ref[0:n, :], kr_ref[0:n, :]
            s = _attn_scores(q_n, q_r, k_n, k_r, i, tq, n)
            p = jnp.exp(s - lse_ref[rows, :])
            dp = lax.dot_general(d_o, v_ref[0:n, :], _DN["nt"], preferred_element_type=F32)
            ds = (p * (dp - jnp.sum(p * dp, axis=-1, keepdims=True)) * _SCALE).astype(BF16)
            dqn_ref[rows, :] = jnp.dot(ds, k_n, preferred_element_type=F32).astype(BF16)
            dqr = jnp.dot(ds, k_r, preferred_element_type=F32)
            dqc_ref[rows, :] = (dqr * cos_ref[rows, :]).astype(BF16)
            dqs_ref[rows, :] = (dqr * sin_ref[rows, :]).astype(BF16)
            dk_acc[0:n, :] += lax.dot_general(ds, q_n, _DN["tn"], preferred_element_type=F32)
            dkr_acc[0:n, :] += lax.dot_general(ds, q_r, _DN["tn"], preferred_element_type=F32)
            dv_acc[0:n, :] += lax.dot_general(p.astype(BF16), d_o, _DN["tn"], preferred_element_type=F32)
        dkn_ref[...] = dk_acc[...].astype(BF16)
        dv_ref[...] = dv_acc[...].astype(BF16)
        h = pl.program_id(1)

        @pl.when(h == 0)
        def _():
            dkr_ref[...] = dkr_acc[...]

        @pl.when(h > 0)
        def _():
            dkr_ref[...] += dkr_acc[...]

    wide = jax.ShapeDtypeStruct((t, N_HEADS * V_DIM), BF16)
    rope = jax.ShapeDtypeStruct((N_HEADS, t, QK_ROPE), BF16)
    krf = pl.BlockSpec((seq, QK_ROPE), lambda b, h: (b, 0))
    return pl.pallas_call(
        body, out_shape=[wide, wide, wide, rope, rope, jax.ShapeDtypeStruct((t, QK_ROPE), F32)],
        grid=(t // seq, N_HEADS),
        in_specs=[sp["qn"], sp["qr"], sp["qn"], sp["qn"], sp["kr"], sp["qn"], sp["lse"], krf, krf],
        out_specs=[sp["qn"], sp["qn"], sp["qn"], sp["qr"], sp["qr"], krf],
        scratch_shapes=[pltpu.VMEM((seq, QK_NOPE), F32), pltpu.VMEM((seq, V_DIM), F32), pltpu.VMEM((seq, QK_ROPE), F32)],
        name="attn_bwd", compiler_params=_params(("parallel", "arbitrary")))(qn, qr, kn, v, kr, do, lse, cos2, sin2)


def _row_tile(rows, cols, row_mult=8):
    cap = max(row_mult, (1 << 18) // cols)
    best = rows
    for tr in range(row_mult, min(rows, cap) + 1, row_mult):
        if rows % tr == 0:
            best = tr
    return best if rows > cap else rows


def _adamw_math(w, g, m, v):
    mv = ADAM_B1 * m + (1.0 - ADAM_B1) * g
    vv = ADAM_B2 * v + (1.0 - ADAM_B2) * (g * g)
    m_hat = mv / (1.0 - ADAM_B1 ** ADAM_STEP)
    v_hat = vv / (1.0 - ADAM_B2 ** ADAM_STEP)
    return -ADAM_LR * (m_hat / (jnp.sqrt(v_hat) + ADAM_EPS) + ADAM_WD * w), mv, vv


def _adamw(w, g, m, v):
    shape = w.shape
    c = shape[-1]
    r = w.size // c
    tr = _row_tile(r, c)

    def body(w_ref, g_ref, m_ref, v_ref, d_ref, nm_ref, nv_ref):
        d_ref[...], nm_ref[...], nv_ref[...] = _adamw_math(w_ref[...], g_ref[...], m_ref[...], v_ref[...])

    spec = pl.BlockSpec((tr, c), lambda i: (i, 0))
    outs = pl.pallas_call(body, out_shape=[jax.ShapeDtypeStruct((r, c), F32)] * 3, grid=(r // tr,),
                          in_specs=[spec] * 4, out_specs=[spec] * 3, name="adamw",
                          compiler_params=_params(("parallel",)))(*[a.reshape(r, c) for a in (w, g, m, v)])
    return [o.reshape(shape) for o in outs]


def _adamw_halves(w, m, v, l, j, own, recv, core, prev):
    nl, nj, rows, c = w.shape
    r = rows // 2
    tr = _row_tile(r, c)
    n_prev = 0 if prev is None else 4

    def body(core_ref, w_ref, own_ref, recv_ref, m_ref, v_ref, *rest):
        g_ref, d_ref, nm_ref, nv_ref = rest[n_prev:]
        g = jnp.where(pl.program_id(0) == core_ref[0], own_ref[...], recv_ref[...])
        g_ref[...] = g
        d_ref[...], nm_ref[...], nv_ref[...] = _adamw_math(w_ref[...], g, m_ref[...], v_ref[...])

    nb = r // tr
    slab = pl.BlockSpec((None, None, tr, c), lambda h, i, cr: (l, j, h * nb + i, 0))
    half = pl.BlockSpec((tr, c), lambda h, i, cr: (i, 0))
    grid_spec = pltpu.PrefetchScalarGridSpec(num_scalar_prefetch=1, grid=(2, nb),
                                             in_specs=[slab, half, half, slab, slab] + [_ANY] * n_prev,
                                             out_specs=[slab] * 4)
    return pl.pallas_call(body, out_shape=[jax.ShapeDtypeStruct(w.shape, F32)] * 4, grid_spec=grid_spec,
                          input_output_aliases={6 + q: q for q in range(n_prev)}, name="adamw_halves",
                          compiler_params=_params(("parallel",) * 2))(core, w, own, recv, m, v, *(prev or ()))


def _place():
    x, y, c = lax.axis_index("x"), lax.axis_index("y"), lax.axis_index("c")
    return x, y, c, [(1 - x, y), (x, 1 - y), (1 - x, 1 - y)]


def _dma_sems(*counts):
    return [pltpu.SemaphoreType.DMA((n,)) for n in counts]


def _all_gather(bufs, collective_id, name):
    n = len(bufs)

    def body(*refs):
        ins, outs = refs[:n], refs[n:2 * n]
        send, recv, fsend, frecv, osend, orecv = refs[2 * n:]
        x, y, c, _ = _place()
        xn, yn, sib = (1 - x, y, c), (x, 1 - y, c), (x, y, 1 - c)
        k, kx, ky, kd = 2 * x + y, 2 * (1 - x) + y, 2 * x + 1 - y, 2 * (1 - x) + 1 - y
        _handshake([xn, yn, sib])

        def copy(src, dst, sems, i, to):
            return pltpu.make_async_remote_copy(src, dst, sems[0].at[i], sems[1].at[i], device_id=to, device_id_type=_MESH)

        ici, d2d, own_s = (send, recv), (fsend, frecv), (osend, orecv)
        started = [copy(ins[b], outs[b].at[k], own_s, b, sib) for b in range(n)]
        for first in (True, False):
            for b in range(n):
                mine = outs[b].at[k, c]
                if first:
                    started += [copy(ins[b].at[c, 0], mine.at[0], ici, 6 * b, xn), copy(ins[b].at[c, 1], mine.at[1], ici, 6 * b + 1, yn)]
                else:
                    started += [copy(ins[b].at[c, 1], mine.at[1], ici, 6 * b + 2, xn), copy(ins[b].at[c, 0], mine.at[0], ici, 6 * b + 3, yn)]
        for cp in started:
            cp.start()
        passed = []
        for b in range(n):
            for i, (src_chip, q, to) in enumerate([(kx, 0, yn), (ky, 1, xn)]):
                piece = outs[b].at[src_chip, c, q]
                copy(piece, piece, ici, 6 * b + i, to).wait_recv()
                cp = copy(piece, piece, ici, 6 * b + 4 + i, to)
                cp.start()
                passed.append(cp)
        for b in range(n):
            for i, (src_chip, q) in enumerate([(kx, 1), (ky, 0)]):
                piece = outs[b].at[src_chip, c, q]
                copy(piece, piece, ici, 6 * b + 2 + i, xn).wait_recv()
                half = outs[b].at[src_chip, c]
                cp = copy(half, half, d2d, 3 * b + i, sib)
                cp.start()
                passed.append(cp)
        for b in range(n):
            for i, q in enumerate([0, 1]):
                piece = outs[b].at[kd, c, q]
                copy(piece, piece, ici, 6 * b + 4 + i, xn).wait_recv()
            half = outs[b].at[kd, c]
            cp = copy(half, half, d2d, 3 * b + 2, sib)
            cp.start()
            passed.append(cp)
        for b in range(n):
            for i, src_chip in enumerate([kx, ky, kd]):
                half = outs[b].at[src_chip, 1 - c]
                copy(half, half, d2d, 3 * b + i, sib).wait_recv()
        for cp in started[n:] + passed:
            cp.wait_send()
        for cp in started[:n]:
            cp.wait()

    return _sequencer(body, [jax.ShapeDtypeStruct((N_SHARDS,) + b.shape, b.dtype) for b in bufs],
                      _dma_sems(6 * n, 6 * n, 3 * n, 3 * n, n, n), collective_id, name, bufs)


def _sequencer(body, out_type, sems, collective_id, name, args):
    return pl.kernel(body, out_type=out_type, mesh=plsc.ScalarSubcoreMesh(axis_name="sequencer", num_cores=1),
                     scratch_types=sems, compiler_params=pltpu.CompilerParams(collective_id=collective_id),
                     name=name)(*args)


def _handshake(peers):
    barrier = pltpu.get_barrier_semaphore()
    for peer in peers:
        pl.semaphore_signal(barrier, inc=1, device_id=peer, device_id_type=_MESH)
    pl.semaphore_wait(barrier, len(peers))


def _swap_halves(parts, collective_id, name):
    n = len(parts)

    def body(*refs):
        ins, outs = refs[:n], refs[n:2 * n]
        send, recv = refs[2 * n:]
        x, y, c, _ = _place()
        _handshake([(x, y, 1 - c)])
        cps = [pltpu.make_async_remote_copy(ins[b].at[:, pl.ds(1 - c, 1)], outs[b], send.at[b], recv.at[b],
                                            device_id=(x, y, 1 - c), device_id_type=_MESH) for b in range(n)]
        for cp in cps:
            cp.start()
        for cp in cps:
            cp.wait()

    return _sequencer(body, [jax.ShapeDtypeStruct((N_SHARDS, 1) + p.shape[2:], p.dtype) for p in parts],
                      _dma_sems(n, n), collective_id, name, parts)


def _add_half(part, other, core):
    _, _, r, c = part.shape
    tr = _row_tile(r, c, 16)

    def body(core_ref, p_ref, o_ref, out_ref):
        out_ref[...] = (p_ref[...].astype(F32) + o_ref[...].astype(F32)).astype(out_ref.dtype)

    grid_spec = pltpu.PrefetchScalarGridSpec(
        num_scalar_prefetch=1, grid=(N_SHARDS, r // tr),
        in_specs=[pl.BlockSpec((None, None, tr, c), lambda k, i, cr: (k, cr[0], i, 0)),
                  pl.BlockSpec((None, None, tr, c), lambda k, i, cr: (k, 0, i, 0))],
        out_specs=pl.BlockSpec((None, tr, c), lambda k, i, cr: (k, i, 0)))
    return pl.pallas_call(body, out_shape=jax.ShapeDtypeStruct((N_SHARDS, r, c), part.dtype), grid_spec=grid_spec,
                          name="grad_add_half", compiler_params=_params(("parallel", "parallel")))(core, part, other)


def _scatter_chips(parts, collective_id, name):
    n = len(parts)

    def body(*refs):
        ins, outs = refs[:n], refs[n:2 * n]
        send, recv = refs[2 * n:]
        x, y, c, chips = _place()
        k = 2 * x + y
        _handshake([(px, py, c) for px, py in chips])
        started = []
        for b in range(n):
            for j, (px, py) in enumerate(chips):
                cp = pltpu.make_async_remote_copy(ins[b].at[2 * px + py], outs[b].at[k], send.at[3 * b + j],
                                                  recv.at[3 * b + j], device_id=(px, py, c), device_id_type=_MESH)
                cp.start()
                started.append(cp)
        for b in range(n):
            for j, (px, py) in enumerate(chips):
                got = outs[b].at[2 * px + py]
                pltpu.make_async_remote_copy(got, got, send.at[3 * b + j], recv.at[3 * b + j],
                                             device_id=(px, py, c), device_id_type=_MESH).wait_recv()
        for cp in started:
            cp.wait_send()

    return _sequencer(body, [jax.ShapeDtypeStruct(p.shape, p.dtype) for p in parts], _dma_sems(3 * n, 3 * n),
                      collective_id, name, parts)


def _sum_slots(slots, mine, chip):
    _, r, c = slots.shape
    tr = _row_tile(r, c, 16)

    def body(chip_ref, s0, s1, s2, s3, own_ref, out_ref):
        own = own_ref[...].astype(F32)
        v = [jnp.where(chip_ref[0] == s, own, ref[...].astype(F32)) for s, ref in enumerate((s0, s1, s2, s3))]
        out_ref[...] = ((v[0] + v[1]) + v[2]) + v[3]

    def slot_spec(s):
        return pl.BlockSpec((None, tr, c), lambda i, kr: (jnp.where(kr[0] == s, (s + 1) % N_SHARDS, s), i, 0))

    grid_spec = pltpu.PrefetchScalarGridSpec(
        num_scalar_prefetch=1, grid=(r // tr,),
        in_specs=[slot_spec(s) for s in range(N_SHARDS)] + [pl.BlockSpec((None, tr, c), lambda i, kr: (kr[0], i, 0))],
        out_specs=pl.BlockSpec((tr, c), lambda i, kr: (i, 0)))
    return pl.pallas_call(body, out_shape=jax.ShapeDtypeStruct((r, c), F32), grid_spec=grid_spec, name="grad_sum_slots",
                          compiler_params=_params(("parallel",)))(chip, slots, slots, slots, slots, mine)


def _join_halves(halves, collective_id, name):
    n = len(halves)

    def body(*refs):
        ins, outs = refs[:n], refs[n:2 * n]
        send, recv = refs[2 * n:]
        x, y, c, _ = _place()
        _handshake([(x, y, 1 - c)])
        cps = [pltpu.make_async_remote_copy(ins[b], outs[b], send.at[b], recv.at[b], device_id=(x, y, 1 - c),
                                            device_id_type=_MESH) for b in range(n)]
        for cp in cps:
            cp.start()
        for cp in cps:
            cp.wait()

    return _sequencer(body, [jax.ShapeDtypeStruct(h.shape, F32) for h in halves], _dma_sems(n, n), collective_id,
                      name, halves)


def _gather_rows(buf, start, rows):
    def body(in_ref, out_ref, send, recv, lsem):
        x, y, c, chips = _place()
        k = 2 * x + y
        src = in_ref.at[pl.ds(start, rows)]
        local = pltpu.make_async_remote_copy(src, out_ref.at[k], lsem.at[0], lsem.at[1], device_id=(x, y, 1 - c),
                                             device_id_type=_MESH)
        local.start()
        cps = [pltpu.make_async_remote_copy(src, out_ref.at[k], send.at[j], recv.at[j], device_id=(px, py, c),
                                            device_id_type=_MESH) for j, (px, py) in enumerate(chips)]
        for cp in cps:
            cp.start()
        for j, (px, py) in enumerate(chips):
            got = out_ref.at[2 * px + py]
            pltpu.make_async_remote_copy(got, got, send.at[j], recv.at[j], device_id=(px, py, c),
                                         device_id_type=_MESH).wait_recv()
        for cp in cps:
            cp.wait_send()
        local.wait()

    return pl.pallas_call(body, out_shape=jax.ShapeDtypeStruct((N_SHARDS, rows, buf.shape[1]), F32),
                          in_specs=[_ANY], out_specs=_ANY, scratch_shapes=_dma_sems(3, 3, 2),
                          name="gather_replicated_grads")(buf)


def _all_sum(vec):
    r, c = vec.shape
    n_dev = 2 * N_SHARDS

    def body(in_ref, out_ref, slots, send, recv):
        x, y, cc, _ = _place()
        flip = lambda v, bit: 1 - v if bit else v
        peers = [(flip(x, (q >> 2) & 1), flip(y, (q >> 1) & 1), flip(cc, q & 1)) for q in range(1, n_dev)]
        index = lambda p: 4 * p[0] + 2 * p[1] + p[2]
        slots[index((x, y, cc))] = in_ref[...]
        cps = [pltpu.make_async_remote_copy(in_ref, slots.at[index((x, y, cc))], send.at[q], recv.at[q], device_id=p,
                                            device_id_type=_MESH) for q, p in enumerate(peers)]
        for cp in cps:
            cp.start()
        for q, p in enumerate(peers):
            got = slots.at[index(p)]
            pltpu.make_async_remote_copy(got, got, send.at[q], recv.at[q], device_id=p, device_id_type=_MESH).wait_recv()
        for cp in cps:
            cp.wait_send()
        acc = slots[0]
        for s in range(1, n_dev):
            acc = acc + slots[s]
        out_ref[...] = acc

    vmem = pl.BlockSpec(memory_space=pltpu.VMEM)
    return pl.pallas_call(body, out_shape=jax.ShapeDtypeStruct((r, c), F32), in_specs=[vmem], out_specs=vmem,
                          scratch_shapes=[pltpu.VMEM((n_dev, r, c), F32)] + _dma_sems(n_dev - 1, n_dev - 1),
                          name="sum_small_grads")(vec)


def _not_before(value, other):
    return lax.optimization_barrier((value, other))[0]


def _round_up(n, m):
    return -(-n // m) * m


def _pack_flat(vecs, rows, width, dtype):
    flat = jnp.concatenate([v.reshape(-1).astype(dtype) for v in vecs])
    return jnp.pad(flat, (0, rows * width - flat.size)).reshape(rows, width)


def _split_flat(flat, shapes):
    out, off = [], 0
    for s in shapes:
        n = math.prod(s)
        out.append(flat[off:off + n].reshape(s))
        off += n
    return out


def _merge_shards(arr4, axis):
    a = jnp.moveaxis(arr4, 0, axis)
    s = list(a.shape)
    return a.reshape(s[:axis] + [s[axis] * s[axis + 1]] + s[axis + 2:])


def _split_shards(full, axis):
    s = list(full.shape)
    a = full.reshape(s[:axis] + [N_SHARDS, s[axis] // N_SHARDS] + s[axis + 1:])
    return jnp.moveaxis(a, axis, 0).reshape(N_SHARDS, -1)


def _rot_cols(w):
    half = w.shape[-1] // 2
    return jnp.concatenate([-w[..., half:], w[..., :half]], axis=-1)


def _unrot_cols(dw):
    half = dw.shape[-1] // 2
    return jnp.concatenate([dw[..., half:], -dw[..., :half]], axis=-1)


def kernel(x, positions, ffn_pre_g, ffn_post_g, ffn_w_gate, ffn_w_up, ffn_w_down, mix_pre_g, mix_post_g, gmlp_w_in, gmlp_ln_g, gmlp_ln_b, gmlp_w_s, gmlp_b_s, gmlp_w_out, kv_norm_g, w_dkv, kv_a_norm_g, w_ukv, mla_w_dq, mla_q_norm_g, mla_w_uq, mla_w_o, loss_target, m_ffn_pre_g, m_ffn_post_g, m_ffn_w_gate, m_ffn_w_up, m_ffn_w_down, m_mix_pre_g, m_mix_post_g, m_gmlp_w_in, m_gmlp_ln_g, m_gmlp_ln_b, m_gmlp_w_s, m_gmlp_b_s, m_gmlp_w_out, m_kv_norm_g, m_w_dkv, m_kv_a_norm_g, m_w_ukv, m_mla_w_dq, m_mla_q_norm_g, m_mla_w_uq, m_mla_w_o, v_ffn_pre_g, v_ffn_post_g, v_ffn_w_gate, v_ffn_w_up, v_ffn_w_down, v_mix_pre_g, v_mix_post_g, v_gmlp_w_in, v_gmlp_ln_g, v_gmlp_ln_b, v_gmlp_w_s, v_gmlp_b_s, v_gmlp_w_out, v_kv_norm_g, v_w_dkv, v_kv_a_norm_g, v_w_ukv, v_mla_w_dq, v_mla_q_norm_g, v_mla_w_uq, v_mla_w_o):
    names = ["ffn_pre_g", "ffn_post_g", "ffn_w_gate", "ffn_w_up", "ffn_w_down", "mix_pre_g", "mix_post_g", "gmlp_w_in",
             "gmlp_ln_g", "gmlp_ln_b", "gmlp_w_s", "gmlp_b_s", "gmlp_w_out", "kv_norm_g", "w_dkv", "kv_a_norm_g", "w_ukv",
             "mla_w_dq", "mla_q_norm_g", "mla_w_uq", "mla_w_o"]
    env = locals()
    w = {n: env[n] for n in names}
    mom = {n: env["m_" + n] for n in names}
    var = {n: env["v_" + n] for n in names}

    bsz, seq, d = x.shape
    t = bsz * seq
    core = lax.axis_index("c").astype(jnp.int32).reshape(1)

    mats = [("gmlp_w_in", 2), ("gmlp_w_out", 1), ("w_dkv", 0), ("w_ukv", 1), ("mla_w_dq", 1), ("mla_w_uq", 2),
            ("mla_w_o", 1)]
    vecs = [("ffn_pre_g", 2), ("ffn_post_g", 2), ("gmlp_ln_g", 1), ("gmlp_ln_b", 1)]
    replicated = ["mix_pre_g", "mix_post_g", "gmlp_w_s", "gmlp_b_s", "kv_norm_g", "kv_a_norm_g", "mla_q_norm_g"]
    n_mats = sum(w[n].size for n, _ in mats)
    n_vecs = sum(w[n].size for n, _ in vecs)
    mat_rows = _round_up(-(-n_mats // PACK_WIDTH), 64)
    vec_rows = _round_up(-(-n_vecs // 128), 32)
    mat_pack = _pack_flat([w[n] for n, _ in mats], mat_rows, PACK_WIDTH, BF16).reshape(2, 2, mat_rows // 4, PACK_WIDTH)
    vec_pack = _pack_flat([w[n] for n, _ in vecs], vec_rows, 128, F32).reshape(2, 2, vec_rows // 4, 128)
    ffn_names = ("ffn_w_gate", "ffn_w_up", "ffn_w_down")

    def oriented(a, name):
        return a if name == "ffn_w_down" else jnp.swapaxes(a, 2, 3)

    lj = [(l, j) for l in range(2) for j in range(2)]
    riders = {(0, 0): [vec_pack], (0, 1): [mat_pack], (1, 0): [], (1, 1): []}
    ffn_w = {}
    landed = []
    for q, (l, j) in enumerate(lj):
        shards = [oriented(w[n], n)[l, j].astype(BF16) for n in ffn_names]
        bufs = [s.reshape(2, 2, s.shape[0] // 4, s.shape[1]) for s in shards] + riders[(l, j)]
        if q > 0:
            bufs = _not_before(bufs, landed[0 if q < 3 else 1])
        got = _all_gather(bufs, q + 1, f"gather_weights_{q}")
        landed.append(got[-1])
        ffn_w[(l, j)] = [g.reshape((N_SHARDS,) + s.shape) for g, s in zip(got, shards)]
        if (l, j) == (0, 0):
            vec_all = got[3]
        if (l, j) == (0, 1):
            mat_all = got[3]

    def unpack(packed, entries):
        flat4, off, out = packed.reshape(N_SHARDS, -1), 0, {}
        for n, ax in entries:
            out[n] = _merge_shards(flat4[:, off:off + w[n].size].reshape((N_SHARDS,) + w[n].shape), ax)
            off += w[n].size
        return out

    full = unpack(vec_all, vecs)
    ln_g, ln_b = full["gmlp_ln_g"], full["gmlp_ln_b"]
    pre_g, post_g = full["ffn_pre_g"], full["ffn_post_g"]
    w_s = w["gmlp_w_s"][0]
    bsb = w["gmlp_b_s"][0][:, :, None]
    row = lambda v: v.reshape(1, -1)

    inv_freq = ROPE_THETA ** (-jnp.arange(0, QK_ROPE, 2, dtype=F32) / QK_ROPE)
    ang = positions.astype(F32).reshape(t, 1) * inv_freq
    cos2 = jnp.concatenate([jnp.cos(ang)] * 2, axis=-1)
    sin2 = jnp.concatenate([jnp.sin(ang)] * 2, axis=-1)
    cos_h, sin_h = jnp.tile(cos2, (1, N_HEADS)), jnp.tile(sin2, (1, N_HEADS))

    def rope_epi(n_lin):
        def epi(accs, ex):
            return accs[:n_lin] + [accs[n_lin] * ex[0] + accs[n_lin + 1] * ex[1]]
        return epi

    h0 = x.reshape(t, d)
    saved = {}

    def ffn_fwd(l, j, h, n, next_gs):
        wg, wu, wd = ffn_w[(l, j)]
        g, u, a = _ffn_up(n, wg, wu)
        f, h_new, *n_next = _ffn_down(a, wd, h, row(post_g[l, j]), next_gs)
        saved[("ffn", l, j)] = (h, n, g, u, a, f)
        return h_new, n_next

    n0 = _rms_fwd(h0, row(pre_g[0, 0]))
    h1, (n1,) = ffn_fwd(0, 0, h0, n0, row(w["mix_pre_g"][0]))

    full.update(unpack(_not_before(mat_all, h1), mats))
    w_in, w_out = full["gmlp_w_in"][0], full["gmlp_w_out"][0]
    w_c, w_kr = full["w_dkv"][:, :KV_RANK], full["w_dkv"][:, KV_RANK:]
    w_kr_rot = _rot_cols(w_kr)
    ukv = full["w_ukv"].reshape(KV_RANK, N_HEADS, 2, QK_NOPE)
    w_k, w_v = ukv[:, :, 0].reshape(KV_RANK, -1), ukv[:, :, 1].reshape(KV_RANK, -1)
    w_dq, w_o = full["mla_w_dq"][0], full["mla_w_o"][0]
    q_rank = w_dq.shape[1]
    uq = full["mla_w_uq"][0].reshape(q_rank, N_HEADS, QK_NOPE + QK_ROPE)
    w_qn = uq[:, :, :QK_NOPE].reshape(q_rank, -1)
    w_qr = uq[:, :, QK_NOPE:].reshape(q_rank, -1)
    w_qr_rot = _rot_cols(uq[:, :, QK_NOPE:]).reshape(q_rank, -1)

    zp = _mm2d("gmlp_in", [(n1, w_in, "nn", 0)], [(w_in.shape[1], F32)])[0]
    uv = _sgu_fwd(zp, ln_g, ln_b, w_s, bsb)
    half = uv.shape[1]
    tm = min(t, ROW_TILE)
    m0, h2, n2 = _down("gmlp_out", (uv, (tm, 512), lambda i, _, k: (i, k)), (w_out, (512, d), lambda i, _, k: (k, 0)),
                       half // 512, h1, row(w["mix_post_g"][0]), row(pre_g[0, 1]), 1.0)
    h3, (n3kv, n3) = ffn_fwd(0, 1, h2, n2, jnp.stack([w["kv_norm_g"], pre_g[1, 0]]))

    def kv_epi(accs, ex):
        c_raw = accs[0]
        return [c_raw, _rms(c_raw, ex[2]), accs[1] * ex[0] + accs[2] * ex[1]]

    c_raw, c_n, k_r = _mm2d("kv_down", [(n3kv, w_c, "nn", 0), (n3kv, w_kr, "nn", 1), (n3kv, w_kr_rot, "nn", 2)],
                            [(KV_RANK, F32), (KV_RANK, BF16), (QK_ROPE, BF16)], kv_epi, [cos2, sin2],
                            [row(w["kv_a_norm_g"])])
    k_n, v_h = _mm2d("kv_up", [(c_n, w_k, "nn", 0), (c_n, w_v, "nn", 1)], [(w_k.shape[1], BF16), (w_v.shape[1], BF16)])

    h4, (n4,) = ffn_fwd(1, 0, h3, n3, row(w["mix_pre_g"][1]))
    qd, qn = _mm2d("q_down", [(n4, w_dq, "nn", 0)], [(q_rank, F32), (q_rank, BF16)],
                   lambda accs, ex: [accs[0], _rms(accs[0], ex[0])], [], [row(w["mla_q_norm_g"][0])])
    q_n, q_r = _mm2d("q_up", [(qn, w_qn, "nn", 0), (qn, w_qr, "nn", 1), (qn, w_qr_rot, "nn", 2)],
                     [(w_qn.shape[1], BF16), (w_qr.shape[1], BF16)], rope_epi(1), [cos_h, sin_h])
    q_r = q_r.reshape(t, N_HEADS, QK_ROPE).transpose(1, 0, 2)
    o, lse = _attn_fwd(q_n, q_r, k_n, v_h, k_r, seq)
    m1, h5, n5 = _down("attn_out", (o, (tm, 512), lambda i, _, k: (i, k)), (w_o, (512, d), lambda i, _, k: (k, 0)),
                       o.shape[1] // 512, h4, row(w["mix_post_g"][1]), row(pre_g[1, 1]), 1.0)
    y, _ = ffn_fwd(1, 1, h5, n5, row(pre_g[1, 1]))

    loss_part, dy = _loss_head(y, loss_target.reshape(t, d))
    loss = lax.psum(loss_part, ("x", "y", "c"))

    chip = (2 * lax.axis_index("x") + lax.axis_index("y")).astype(jnp.int32).reshape(1)
    rs = {}

    def rs_launch(gid, parts):
        rs[gid] = {"parts": parts, "others": _swap_halves(parts, 5 + gid, f"grad_swap_{gid}")}

    def rs_mid(gid, after):
        r = rs[gid]
        parts, others = _not_before((r["parts"], r["others"]), after)
        r["chip"] = [_add_half(p, o, core) for p, o in zip(parts, others)]
        r["slots"] = _scatter_chips(r["chip"], 10 + gid, f"grad_scatter_{gid}")
        return r["chip"]

    def rs_end(gid, after):
        r = rs[gid]
        slots, mine = _not_before((r["slots"], r["chip"]), after)
        r["own"] = [_sum_slots(s, p, chip) for s, p in zip(slots, mine)]
        r["recv"] = _join_halves(r["own"], 15 + gid, f"grad_join_{gid}")
        return r["own"]

    d_pre, d_post = {}, {}

    def ffn_bwd(l, j, gid, dh_out, extra=(), after_post=None):
        h, n, g, u, a, f = saved[("ffn", l, j)]
        df, d_post[(l, j)] = _norm_out_bwd("ffn_post_bwd", f, dh_out, row(post_g[l, j]), 0.5)
        if after_post is not None:
            df = after_post(df)
        wg, wu, wd = ffn_w[(l, j)]
        dg, du = _ffn_dact(df, wd, g, u)
        dwd = _ffn_dw_down(a, df)
        dwg, dwu = _ffn_dw_in(n, dg, du)
        parts = [p.reshape(N_SHARDS, 2, p.shape[1] // 2, p.shape[2]) for p in (dwg, dwu, dwd)]
        rs_launch(gid, parts)
        dg, du = _not_before((dg, du), parts)
        dn = _ffn_dn(dg, du, wg, wu)
        dh, d_pre[(l, j)], *rest = _norm_in_bwd("ffn_pre_bwd", h, dh_out, [(row(pre_g[l, j]), dn)] + list(extra))
        return dh, rest

    dh5, _ = ffn_bwd(1, 1, 0, dy)
    dh5 = _not_before(dh5, rs_mid(0, dh5))

    dm1, g_mix_post1 = _norm_out_bwd("mix_post_bwd", m1, dh5, row(w["mix_post_g"][1]), 1.0)
    do = _mm2d("attn_out_dx", [(dm1, w_o, "nt", 0)], [(w_o.shape[0], BF16)])[0]
    g_w_o = _mm2d("attn_out_dw", [(o, dm1, "tn", 0)], [(d, BF16)])[0]
    dq_n, dk_n, dv_h, dq_c, dq_s, dk_r = _attn_bwd(q_n, q_r, k_n, v_h, k_r, do, lse, cos2, sin2, seq)
    dq_c = dq_c.transpose(1, 0, 2).reshape(t, -1)
    dq_s = dq_s.transpose(1, 0, 2).reshape(t, -1)
    dqn = _mm2d("q_up_dx", [(dq_n, w_qn, "nt", 0), (dq_c, w_qr, "nt", 0), (dq_s, w_qr_rot, "nt", 0)], [(q_rank, F32)])[0]
    g_qn, g_qr, g_qr_rot = _mm2d("q_up_dw", [(qn, dq_n, "tn", 0), (qn, dq_c, "tn", 1), (qn, dq_s, "tn", 2)],
                                 [(w_qn.shape[1], F32), (w_qr.shape[1], F32), (w_qr.shape[1], F32)])
    dqd, g_q_norm = _norm_out_bwd("q_norm_bwd", qd, dqn, row(w["mla_q_norm_g"][0]), 1.0)
    dn4 = _mm2d("q_down_dx", [(dqd, w_dq, "nt", 0)], [(d, F32)])[0]
    g_w_dq = _mm2d("q_down_dw", [(n4, dqd, "tn", 0)], [(q_rank, BF16)])[0]
    dh4, g_mix_pre1 = _norm_in_bwd("mix_pre_bwd", h4, dh5, [(row(w["mix_pre_g"][1]), dn4)])

    dc_n = _mm2d("kv_up_dx", [(dk_n, w_k, "nt", 0), (dv_h, w_v, "nt", 0)], [(KV_RANK, F32)])[0]
    g_wk, g_wv = _mm2d("kv_up_dw", [(c_n, dk_n, "tn", 0), (c_n, dv_h, "tn", 1)], [(w_k.shape[1], F32), (w_v.shape[1], F32)])
    dc, g_kv_a = _norm_out_bwd("kv_a_norm_bwd", c_raw, dc_n, row(w["kv_a_norm_g"]), 1.0)
    dkr_c, dkr_s = _rope_bwd(dk_r, cos2, sin2)
    dn3kv = _mm2d("kv_down_dx", [(dc, w_c, "nt", 0), (dkr_c, w_kr, "nt", 0), (dkr_s, w_kr_rot, "nt", 0)], [(d, F32)])[0]
    g_wc, g_wkr, g_wkr_rot = _mm2d("kv_down_dw", [(n3kv, dc, "tn", 0), (n3kv, dkr_c, "tn", 1), (n3kv, dkr_s, "tn", 2)],
                                   [(KV_RANK, F32), (QK_ROPE, F32), (QK_ROPE, F32)])

    dh4 = _not_before(dh4, rs_end(0, dh4))
    dh3, (g_kv_norm,) = ffn_bwd(1, 0, 1, dh4, extra=[(row(w["kv_norm_g"]), dn3kv)])
    dh3 = _not_before(dh3, rs_mid(1, dh3))
    dh2, _ = ffn_bwd(0, 1, 2, dh3)
    dh2 = _not_before(dh2, (rs_end(1, dh2), rs_mid(2, dh2)))

    dm0, g_mix_post0 = _norm_out_bwd("mix_post_bwd", m0, dh2, row(w["mix_post_g"][0]), 1.0)
    d_uv = _mm2d("gmlp_out_dx", [(dm0, w_out, "nt", 0)], [(half, F32)])[0]
    g_w_out = _mm2d("gmlp_out_dw", [(uv, dm0, "tn", 0)], [(d, BF16)])[0]
    dzp, g_ln_g, g_ln_b, g_w_s, g_b_s = _sgu_bwd(zp, d_uv, ln_g, ln_b, w_s, bsb)
    dn1 = _mm2d("gmlp_in_dx", [(dzp, w_in, "nt", 0)], [(d, F32)])[0]
    tk, tmw, w_cols = min(t, K_TILE), min(d, ROW_TILE), w_in.shape[1] // N_SHARDS
    g_w_in = _mm("gmlp_in_dw", (d // tmw, N_SHARDS, t // tk),
                 [(n1, (tk, tmw), lambda i, j, k: (k, i)), (dzp, (tk, w_cols), lambda i, j, k: (k, j))],
                 [(0, 1, 0, "tn")], [(tmw, w_cols)],
                 [((N_SHARDS, d, w_cols), BF16, (None, tmw, w_cols), lambda i, j, k: (j, i, 0))], lambda accs, ex: accs)[0]
    dh1, g_mix_pre0 = _norm_in_bwd("mix_pre_bwd", h1, dh2, [(row(w["mix_pre_g"][0]), dn1)])

    g_w_dkv = jnp.concatenate([g_wc, g_wkr + _unrot_cols(g_wkr_rot)], axis=1).astype(BF16)
    direct = {"gmlp_w_in": g_w_in, "gmlp_w_out": g_w_out, "mla_w_o": g_w_o, "mla_w_dq": g_w_dq, "w_dkv": g_w_dkv}
    direct = {n: g.reshape(N_SHARDS, -1, g.shape[-1]) for n, g in direct.items()}
    part = {
        "w_ukv": jnp.stack([g_wk.reshape(KV_RANK, N_HEADS, QK_NOPE), g_wv.reshape(KV_RANK, N_HEADS, V_DIM)],
                           axis=2).reshape(KV_RANK, -1),
        "mla_w_uq": jnp.concatenate(
            [g_qn.reshape(q_rank, N_HEADS, QK_NOPE),
             g_qr.reshape(q_rank, N_HEADS, QK_ROPE) + _unrot_cols(g_qr_rot.reshape(q_rank, N_HEADS, QK_ROPE))],
            axis=-1).reshape(1, q_rank, -1),
        "gmlp_ln_g": g_ln_g, "gmlp_ln_b": g_ln_b,
        "mix_pre_g": jnp.concatenate([g_mix_pre0, g_mix_pre1]), "mix_post_g": jnp.concatenate([g_mix_post0, g_mix_post1]),
        "gmlp_w_s": g_w_s[None], "gmlp_b_s": g_b_s.reshape(1, GROUPS, CHUNK),
        "kv_norm_g": g_kv_norm.reshape(-1), "kv_a_norm_g": g_kv_a.reshape(-1), "mla_q_norm_g": g_q_norm,
    }

    sharded = [e for e in mats + vecs if e[0] in part]
    n_sh = sum(w[n].size for n, _ in sharded)
    n_rep = sum(w[n].size for n in replicated)
    sh_rows = _round_up(-(-n_sh // PACK_WIDTH), 8)
    rep_rows = _round_up(-(-(n_rep // N_SHARDS) // PACK_WIDTH), 8)
    rows = _round_up(sh_rows + rep_rows, 32)
    sh_flat = jnp.concatenate([_split_shards(part[n], ax) for n, ax in sharded], axis=1)
    rep_flat = jnp.concatenate([part[n].reshape(-1) for n in replicated]).reshape(N_SHARDS, -1)
    small = jnp.concatenate([
        jnp.pad(sh_flat, ((0, 0), (0, sh_rows * PACK_WIDTH - n_sh))),
        jnp.pad(rep_flat, ((0, 0), (0, (rows - sh_rows) * PACK_WIDTH - n_rep // N_SHARDS)))], axis=1)
    small = small.astype(BF16).reshape(N_SHARDS, 2, rows // 2, PACK_WIDTH)

    rs_launch(3, [g.reshape(N_SHARDS, 2, g.shape[1] // 2, g.shape[2]) for g in direct.values()] + [small])
    dh1 = _not_before(dh1, rs_end(2, dh1))
    dx, _ = ffn_bwd(0, 0, 4, dh1, after_post=lambda df: _not_before(df, rs_mid(3, df)))

    launched = rs_mid(4, dx)
    lj = [(l, j) for l in range(2) for j in range(2)]
    tiny = jnp.concatenate([d_pre[k] for k in lj] + [d_post[k] for k in lj]).reshape(-1, 128)
    tiny = _all_sum(_not_before(tiny, launched)).reshape(2, 2, 2, d)
    shard_cols = d // N_SHARDS
    grads = {"ffn_pre_g": lax.dynamic_slice_in_dim(tiny[0], chip[0] * shard_cols, shard_cols, axis=2),
             "ffn_post_g": lax.dynamic_slice_in_dim(tiny[1], chip[0] * shard_cols, shard_cols, axis=2)}
    own_small, recv_small = rs_end(3, launched)[-1], rs[3]["recv"][-1]
    delta, new_m, new_v = {}, {}, {}
    for q, n in enumerate(direct):
        lead = lambda a: a.reshape((1, 1) + a.shape[-2:])
        upd = _adamw_halves(lead(w[n]), lead(mom[n]), lead(var[n]), 0, 0, rs[3]["own"][q], rs[3]["recv"][q], core, None)
        grads[n], delta[n], new_m[n], new_v[n] = [o.reshape(w[n].shape) for o in upd]
    g_small = jnp.where(core[0] == 0, jnp.concatenate([own_small, recv_small]), jnp.concatenate([recv_small, own_small]))
    g_rep = _gather_rows(g_small, sh_rows, rep_rows)
    for (n, _), g in zip(sharded, _split_flat(g_small.reshape(-1), [w[n].shape for n, _ in sharded])):
        grads[n] = g
    rep_vec = g_rep.reshape(N_SHARDS, -1)[:, :n_rep // N_SHARDS].reshape(-1)
    for n, g in zip(replicated, _split_flat(rep_vec, [w[n].shape for n in replicated])):
        grads[n] = g

    for n in names:
        if n not in ffn_names and n not in delta:
            delta[n], new_m[n], new_v[n] = _adamw(w[n], grads[n], mom[n], var[n])
    chain = {n: None for n in ffn_names}

    def ffn_update(gid, l, j):
        for q, n in enumerate(ffn_names):
            chain[n] = _adamw_halves(oriented(w[n], n), oriented(mom[n], n), oriented(var[n], n), l, j,
                                     rs[gid]["own"][q], rs[gid]["recv"][q], core, chain[n])

    ffn_update(0, 1, 1)
    ffn_update(1, 1, 0)
    ffn_update(2, 0, 1)
    rs_end(4, ([delta[n] for n in delta], [chain[n] for n in ffn_names]))
    ffn_update(4, 0, 0)
    for n in ffn_names:
        grads[n], delta[n], new_m[n], new_v[n] = [oriented(o, n) for o in chain[n]]
    return (loss, dx.reshape(x.shape), *[grads[n] for n in names], *[delta[n] for n in names],
            *[new_m[n] for n in names], *[new_v[n] for n in names])
```

```python
import math

import jax
import jax.numpy as jnp
from jax import lax
from jax.experimental import pallas as pl
from jax.experimental.pallas import tpu as pltpu
from jax.experimental.pallas import tpu_sc as plsc

F32, BF16 = jnp.float32, jnp.bfloat16

RMS_EPS, LN_EPS, NEG_INF = 1e-6, 1e-5, -1e30
N_HEADS, QK_NOPE, QK_ROPE, V_DIM, KV_RANK = 8, 128, 64, 128, 256
CHUNK, GROUPS = 128, 16
ROPE_THETA = 10000.0
ADAM_LR, ADAM_B1, ADAM_B2, ADAM_EPS, ADAM_WD, ADAM_STEP = 0.001, 0.9, 0.999, 1e-08, 0.01, 10
N_SHARDS = 4

VMEM_LIMIT_BYTES = 48 * 1024 * 1024
ROW_TILE = 512
K_TILE = 2048
PACK_WIDTH = 1024

_DN = {"nn": (((1,), (0,)), ((), ())), "nt": (((1,), (1,)), ((), ())), "tn": (((0,), (0,)), ((), ()))}
_MESH = pl.DeviceIdType.MESH
_ANY = pl.BlockSpec(memory_space=pl.ANY)


def _params(sem):
    return pltpu.CompilerParams(dimension_semantics=sem, vmem_limit_bytes=VMEM_LIMIT_BYTES)


def _mm(name, grid, ins, pairs, acc_shapes, outs, epilogue, extras=(), inner=0, sums=()):
    n_in, n_ex, n_out = len(ins), len(extras), len(outs)
    gk = grid[2]

    def body(*refs):
        in_refs, ex_refs = refs[:n_in], refs[n_in:n_in + n_ex]
        out_refs = refs[n_in + n_ex:n_in + n_ex + n_out]
        sum_refs = refs[n_in + n_ex + n_out:n_in + n_ex + n_out + len(sums)]
        acc_refs = refs[n_in + n_ex + n_out + len(sums):]
        parts = [None] * len(acc_shapes)
        for a, b, c, dims in pairs:
            for s in range(max(inner, 1)):
                lhs, rhs = (in_refs[a][s], in_refs[b][s]) if inner else (in_refs[a][...], in_refs[b][...])
                p = lax.dot_general(lhs, rhs, _DN[dims], preferred_element_type=F32)
                parts[c] = p if parts[c] is None else parts[c] + p

        def finish(accs):
            vals = epilogue(accs, [r[...] for r in ex_refs])
            for r, v in zip(out_refs, vals):
                r[...] = v.astype(r.dtype)
            first = (pl.program_id(0) == 0) & (pl.program_id(1) == 0)
            for r, v in zip(sum_refs, vals[n_out:]):
                @pl.when(first)
                def _():
                    r[...] = v

                @pl.when(jnp.logical_not(first))
                def _():
                    r[...] += v

        if gk == 1:
            finish(parts)
        else:
            k = pl.program_id(2)

            @pl.when(k == 0)
            def _():
                for r, p in zip(acc_refs, parts):
                    r[...] = p

            @pl.when(k > 0)
            def _():
                for r, p in zip(acc_refs, parts):
                    r[...] += p

            @pl.when(k == gk - 1)
            def _():
                finish([r[...] for r in acc_refs])

    return pl.pallas_call(
        body,
        out_shape=[jax.ShapeDtypeStruct(s, d) for s, d, _, _ in outs] + [jax.ShapeDtypeStruct(s, F32) for s in sums],
        grid=grid,
        in_specs=[pl.BlockSpec(bs, im) for _, bs, im in list(ins) + list(extras)],
        out_specs=[pl.BlockSpec(bs, im) for _, _, bs, im in outs]
        + [pl.BlockSpec(s, lambda i, j, k, nd=len(s): (0,) * nd) for s in sums],
        scratch_shapes=[pltpu.VMEM(s, F32) for s in acc_shapes] if gk > 1 else [],
        name=name,
        compiler_params=_params(("arbitrary",) * 3 if sums else ("parallel", "parallel", "arbitrary")),
    )(*[a for a, _, _ in ins], *[a for a, _, _ in extras])


def _mm2d(name, pairs, outs, epilogue=None, row_extras=(), vec_extras=()):
    def mk(a, dims):
        return (a.shape[0], a.shape[1]) if dims[0] == "n" else (a.shape[1], a.shape[0])

    def nk(b, dims):
        return (b.shape[1], b.shape[0]) if dims[1] == "n" else (b.shape[0], b.shape[1])

    m = mk(pairs[0][0], pairs[0][2])[0]
    ks = [mk(a, d)[1] for a, _, d, _ in pairs]
    n_acc = 1 + max(p[3] for p in pairs)
    acc_n = [None] * n_acc
    for a, b, d, c in pairs:
        assert mk(a, d)[0] == m and nk(b, d)[1] == mk(a, d)[1]
        acc_n[c] = nk(b, d)[0]
    tm = min(m, ROW_TILE)
    if len(set(ks)) == 1 and ks[0] > 1024:
        tk = K_TILE if ks[0] % K_TILE == 0 else 512
        tks, gk = [tk] * len(pairs), ks[0] // tk
    else:
        tks, gk = ks, 1
    if len(set(acc_n)) == 1 and acc_n[0] > 1024:
        tns, gj = [1024] * n_acc, acc_n[0] // 1024
    else:
        tns, gj = acc_n, 1

    ins, plist = [], []
    for (a, b, d, c), tk in zip(pairs, tks):
        tn = tns[c]
        a_spec = ((tm, tk), lambda i, j, k: (i, k)) if d[0] == "n" else ((tk, tm), lambda i, j, k: (k, i))
        b_spec = ((tk, tn), lambda i, j, k: (k, j)) if d[1] == "n" else ((tn, tk), lambda i, j, k: (j, k))
        ins += [(a, *a_spec), (b, *b_spec)]
        plist.append((len(ins) - 2, len(ins) - 1, c, d))
    extras = [(r, (tm, r.shape[1]), lambda i, j, k: (i, 0)) for r in row_extras]
    extras += [(v, v.shape, lambda i, j, k: (0, 0)) for v in vec_extras]
    out_specs = []
    for n, dt in outs:
        bn = 1024 if (gj > 1) else n
        out_specs.append(((m, n), dt, (tm, bn), lambda i, j, k: (i, j)))
    if epilogue is None:
        epilogue = lambda accs, ex: accs
    return _mm(name, (m // tm, gj, gk), ins, plist, [(tm, tn) for tn in tns], out_specs, epilogue, extras)


def _rms(x, g):
    return x * lax.rsqrt(jnp.mean(x * x, axis=-1, keepdims=True) + RMS_EPS) * g


def _rms_bwd(x, g, dy):
    r = lax.rsqrt(jnp.mean(x * x, axis=-1, keepdims=True) + RMS_EPS)
    gy = dy * g
    dx = r * gy - x * (r * r * r) * jnp.mean(gy * x, axis=-1, keepdims=True)
    return dx, jnp.sum(dy * x * r, axis=0, keepdims=True)


def _sigmoid(x):
    return 1.0 / (1.0 + jnp.exp(-x))


_GELU_C = math.sqrt(2.0 / math.pi)


def _gelu(x):
    return x * (0.5 * (1.0 + jnp.tanh(_GELU_C * (x + 0.044715 * (x * x * x)))))


def _gelu_grad(x):
    t = jnp.tanh(_GELU_C * (x + 0.044715 * (x * x * x)))
    return 0.5 * (1.0 + t) + 0.5 * x * (1.0 - t * t) * (_GELU_C * (1.0 + 3.0 * 0.044715 * (x * x)))


def _rows(name, row_ins, vec_ins, fn, row_outs, acc_outs=()):
    t = row_ins[0].shape[0]
    tm = min(t, ROW_TILE)
    nr, nv, no = len(row_ins), len(vec_ins), len(row_outs)

    def body(*refs):
        outs, incs = fn([r[...] for r in refs[:nr]], [r[...] for r in refs[nr:nr + nv]])
        for r, v in zip(refs[nr + nv:nr + nv + no], outs):
            r[...] = v.astype(r.dtype)
        i = pl.program_id(0)
        for r, v in zip(refs[nr + nv + no:], incs):
            @pl.when(i == 0)
            def _():
                r[...] = v

            @pl.when(i > 0)
            def _():
                r[...] += v

    in_specs = [pl.BlockSpec((tm, a.shape[1]), lambda i: (i, 0)) for a in row_ins]
    in_specs += [pl.BlockSpec(v.shape, lambda i, nd=v.ndim: (0,) * nd) for v in vec_ins]
    out_shape = [jax.ShapeDtypeStruct((t, c), dt) for c, dt in row_outs]
    out_shape += [jax.ShapeDtypeStruct(s, F32) for s in acc_outs]
    out_specs = [pl.BlockSpec((tm, c), lambda i: (i, 0)) for c, _ in row_outs]
    out_specs += [pl.BlockSpec(s, lambda i, nd=len(s): (0,) * nd) for s in acc_outs]
    return pl.pallas_call(body, out_shape=out_shape, grid=(t // tm,), in_specs=in_specs, out_specs=out_specs,
                          name=name, compiler_params=_params(("arbitrary",)))(*row_ins, *vec_ins)


def _rms_fwd(x, g):
    return _rows("rms_fwd", [x], [g], lambda r, v: ([_rms(r[0], v[0])], []), [(x.shape[1], BF16)])[0]


def _norm_out_bwd(name, f, d_out, g, scale):
    def fn(r, v):
        dx, dg = _rms_bwd(r[0], v[0], r[1] * scale)
        return [dx], [dg]

    c = f.shape[1]
    return _rows(name, [f, d_out], [g], fn, [(c, BF16)], [(1, c)])


def _norm_in_bwd(name, h, d_res, branches):
    nb = len(branches)

    def fn(r, v):
        dh, dgs = r[1], []
        for b in range(nb):
            dx, dg = _rms_bwd(r[0], v[b], r[2 + b])
            dh = dh + dx
            dgs.append(dg)
        return [dh], dgs

    c = h.shape[1]
    return _rows(name, [h, d_res] + [dn for _, dn in branches], [g for g, _ in branches], fn, [(c, F32)],
                 [(1, c)] * nb)


def _loss_head(y, target):
    d = y.shape[1]

    def fn(r, v):
        e = r[0] - r[1]
        s = jnp.sum(jnp.sum(e * e, axis=1, keepdims=True), axis=0, keepdims=True) * (0.5 / d)
        return [e * (1.0 / d)], [jnp.broadcast_to(s, (1, 128))]

    dy, acc = _rows("loss_head", [y, target], [], fn, [(d, F32)], [(1, 128)])
    return acc[0, 0], dy


def _rope_bwd(dk, cos2, sin2):
    c = dk.shape[1]
    return _rows("rope_bwd", [dk, cos2, sin2], [], lambda r, v: ([r[0] * r[1], r[0] * r[2]], []),
                 [(c, BF16), (c, BF16)])


def _ffn_up(n, wg, wu):
    t, d = n.shape
    fs = wg.shape[-2]
    tm = min(t, ROW_TILE)
    w_spec = ((None, fs, d), lambda s, i, k: (s, 0, 0))

    def epi(accs, ex):
        g, u = accs
        return [g, u, g * _sigmoid(g) * u]

    o_spec = ((None, tm, fs), lambda s, i, k: (s, i, 0))
    outs = [((N_SHARDS, t, fs), BF16, *o_spec)] * 3
    return _mm("ffn_up", (N_SHARDS, t // tm, 1),
               [(n, (tm, d), lambda s, i, k: (i, 0)), (wg, *w_spec), (wu, *w_spec)],
               [(0, 1, 0, "nt"), (0, 2, 1, "nt")], [(tm, fs)] * 2, outs, epi)


def _down(name, a_in, w_in, gk, h, post_g, next_gs, scale, inner=0):
    t, d = h.shape
    tm = min(t, ROW_TILE)
    kn = next_gs.shape[0]

    def epi(accs, ex):
        f, hv, pg, ng = accs[0], ex[0], ex[1], ex[2]
        hn = hv + scale * _rms(f, pg)
        return [f, hn] + [_rms(hn, ng[q:q + 1]) for q in range(kn)]

    row = ((tm, d), lambda i, j, k: (i, 0))
    outs = [((t, d), F32, *row), ((t, d), F32, *row)] + [((t, d), BF16, *row)] * kn
    extras = [(h, *row), (post_g, (1, d), lambda i, j, k: (0, 0)), (next_gs, (kn, d), lambda i, j, k: (0, 0))]
    return _mm(name, (t // tm, 1, gk), [a_in, w_in], [(0, 1, 0, "nn")], [(tm, d)], outs, epi, extras, inner)


def _ffn_down(a, wd, h, post_g, next_gs):
    t, d = h.shape
    fs = a.shape[-1]
    tm = min(t, ROW_TILE)
    return _down("ffn_down", (a, (N_SHARDS, tm, fs), lambda i, _, k: (0, i, 0)),
                 (wd, (N_SHARDS, fs, d), lambda i, _, k: (0, 0, 0)), 1, h, post_g, next_gs, 0.5, N_SHARDS)


def _ffn_dact(df, wd, g, u):
    t, d = df.shape
    fs = g.shape[-1]
    tm = min(t, ROW_TILE)

    def epi(accs, ex):
        da, gv, uv = accs[0], ex[0].astype(F32), ex[1].astype(F32)
        sg = _sigmoid(gv)
        return [da * uv * (sg * (1.0 + gv * (1.0 - sg))), da * (gv * sg)]

    o_spec = ((None, tm, fs), lambda s, i, k: (s, i, 0))
    outs = [((N_SHARDS, t, fs), BF16, *o_spec)] * 2
    return _mm("ffn_dact", (N_SHARDS, t // tm, 1),
               [(df, (tm, d), lambda s, i, k: (i, 0)), (wd, (None, fs, d), lambda s, i, k: (s, 0, 0))],
               [(0, 1, 0, "nt")], [(tm, fs)], outs, epi, [(g, *o_spec), (u, *o_spec)])


def _ffn_dn(dg, du, wg, wu, h, d_res, branches):
    _, t, fs = dg.shape
    d = wg.shape[-1]
    tm = min(t, ROW_TILE)
    nb = len(branches)
    a_spec = ((N_SHARDS, tm, fs), lambda i, _, k: (0, i, 0))
    w_spec = ((N_SHARDS, fs, d), lambda i, _, k: (0, 0, 0))
    row = ((tm, d), lambda i, _, k: (i, 0))
    vec = ((1, d), lambda i, _, k: (0, 0))

    def epi(accs, ex):
        hv, dh = ex[0], ex[1]
        dns, gs = [accs[0]] + ex[2:1 + nb], ex[1 + nb:]
        dgs = []
        for dn, g in zip(dns, gs):
            dx, dgv = _rms_bwd(hv, g, dn)
            dh = dh + dx
            dgs.append(dgv)
        return [dh] + dgs

    extras = [(h, *row), (d_res, *row)] + [(dn, *row) for _, dn in branches[1:]] + [(g, *vec) for g, _ in branches]
    return _mm("ffn_dn", (t // tm, 1, 1), [(dg, *a_spec), (wg, *w_spec), (du, *a_spec), (wu, *w_spec)],
               [(0, 1, 0, "nn"), (2, 3, 0, "nn")], [(tm, d)], [((t, d), F32, *row)], epi, extras, inner=N_SHARDS,
               sums=[(1, d)] * nb)


def _ffn_dw_in(n, dg, du):
    _, t, fs = dg.shape
    d = n.shape[1]
    tk = min(t, K_TILE)
    a_spec = ((None, tk, fs), lambda s, _, k: (s, k, 0))
    o_spec = ((None, fs, d), lambda s, _, k: (s, 0, 0))
    outs = [((N_SHARDS, fs, d), BF16, *o_spec)] * 2
    return _mm("ffn_dw_in", (N_SHARDS, 1, t // tk), [(dg, *a_spec), (du, *a_spec), (n, (tk, d), lambda s, _, k: (k, 0))],
               [(0, 2, 0, "tn"), (1, 2, 1, "tn")], [(fs, d)] * 2, outs, lambda accs, ex: accs)


def _ffn_dw_down(a, df):
    _, t, fs = a.shape
    d = df.shape[1]
    tk = min(t, K_TILE)
    outs = [((N_SHARDS, fs, d), BF16, (None, fs, d), lambda s, _, k: (s, 0, 0))]
    return _mm("ffn_dw_down", (N_SHARDS, 1, t // tk),
               [(a, (None, tk, fs), lambda s, _, k: (s, k, 0)), (df, (tk, d), lambda s, _, k: (k, 0))],
               [(0, 1, 0, "tn")], [(fs, d)], outs, lambda accs, ex: accs)[0]


def _causal_weight(w):
    row = lax.broadcasted_iota(jnp.int32, (CHUNK, CHUNK), 0)
    col = lax.broadcasted_iota(jnp.int32, (CHUNK, CHUNK), 1)
    return row >= col, jnp.where(row >= col, w, 0.0).astype(BF16)


def _layer_norm(v, g, b):
    xc = v - jnp.mean(v, axis=-1, keepdims=True)
    rstd = lax.rsqrt(jnp.mean(xc * xc, axis=-1, keepdims=True) + LN_EPS)
    xhat = xc * rstd
    return xhat, rstd, xhat * g + b


def _sgu_specs(t, half, tm):
    return [pl.BlockSpec((tm, half), lambda i: (i, 0)), pl.BlockSpec((tm, half), lambda i: (i, 1))]


def _sgu_fwd(zp, ln_g, ln_b, w_s, bsb):
    t, half = zp.shape[0], zp.shape[1] // 2
    tm = min(t, 2 * CHUNK)

    def body(u_ref, v_ref, g_ref, b_ref, w_ref, bs_ref, o_ref):
        u = _gelu(u_ref[...])
        _, _, vn = _layer_norm(_gelu(v_ref[...]), g_ref[...], b_ref[...])
        vb = vn.astype(BF16)
        for g in range(GROUPS):
            _, wm = _causal_weight(w_ref[g])
            cols = slice(g * CHUNK, (g + 1) * CHUNK)
            for c in range(tm // CHUNK):
                rows = slice(c * CHUNK, (c + 1) * CHUNK)
                sv = jnp.dot(wm, vb[rows, cols], preferred_element_type=F32) + bs_ref[g]
                o_ref[rows, cols] = (u[rows, cols] * sv).astype(BF16)

    whole = lambda a: pl.BlockSpec(a.shape, lambda i, nd=a.ndim: (0,) * nd)
    return pl.pallas_call(
        body, out_shape=jax.ShapeDtypeStruct((t, half), BF16), grid=(t // tm,),
        in_specs=_sgu_specs(t, half, tm) + [whole(ln_g), whole(ln_b), whole(w_s), whole(bsb)],
        out_specs=pl.BlockSpec((tm, half), lambda i: (i, 0)), name="sgu_fwd",
        compiler_params=_params(("arbitrary",)))(zp, zp, ln_g, ln_b, w_s, bsb)


def _sgu_bwd(zp, d_uv, ln_g, ln_b, w_s, bsb):
    t, half = zp.shape[0], zp.shape[1] // 2
    tm = min(t, 2 * CHUNK)

    def body(u_ref, v_ref, d_ref, g_ref, b_ref, w_ref, bs_ref, dz_ref, dlg_ref, dlb_ref, dws_ref, dbs_ref, dvn_ref):
        i = pl.program_id(0)

        @pl.when(i == 0)
        def _():
            dlg_ref[...] = jnp.zeros_like(dlg_ref)
            dlb_ref[...] = jnp.zeros_like(dlb_ref)
            dws_ref[...] = jnp.zeros_like(dws_ref)
            dbs_ref[...] = jnp.zeros_like(dbs_ref)

        up, vp = u_ref[...], v_ref[...]
        u, gup = _gelu(up), _gelu_grad(up)
        xhat, rstd, vn = _layer_norm(_gelu(vp), g_ref[...], b_ref[...])
        vb = vn.astype(BF16)
        d = d_ref[...]
        for g in range(GROUPS):
            mask, wm = _causal_weight(w_ref[g])
            cols = slice(g * CHUNK, (g + 1) * CHUNK)
            for c in range(tm // CHUNK):
                rows = slice(c * CHUNK, (c + 1) * CHUNK)
                blk = vb[rows, cols]
                sv = jnp.dot(wm, blk, preferred_element_type=F32) + bs_ref[g]
                dblk = d[rows, cols]
                dz_ref[rows, cols] = (dblk * sv * gup[rows, cols]).astype(BF16)
                dsv = dblk * u[rows, cols]
                dsvb = dsv.astype(BF16)
                dvn_ref[rows, cols] = lax.dot_general(wm, dsvb, _DN["tn"], preferred_element_type=F32)
                dw = lax.dot_general(dsvb, blk, _DN["nt"], preferred_element_type=F32)
                dws_ref[g] += jnp.where(mask, dw, 0.0)
                dbs_ref[g] += jnp.sum(dsv, axis=1, keepdims=True)
        dvn = dvn_ref[...]
        dlg_ref[...] += jnp.sum(dvn * xhat, axis=0, keepdims=True)
        dlb_ref[...] += jnp.sum(dvn, axis=0, keepdims=True)
        dxh = dvn * g_ref[...]
        dv = rstd * (dxh - jnp.mean(dxh, axis=-1, keepdims=True)
                     - xhat * jnp.mean(dxh * xhat, axis=-1, keepdims=True))
        dz_ref[:, half:] = (dv * _gelu_grad(vp)).astype(BF16)

    whole = lambda a: pl.BlockSpec(a.shape, lambda i, nd=a.ndim: (0,) * nd)
    wshape = lambda s: pl.BlockSpec(s, lambda i, nd=len(s): (0,) * nd)
    out_shape = [jax.ShapeDtypeStruct((t, 2 * half), BF16), jax.ShapeDtypeStruct((1, half), F32),
                 jax.ShapeDtypeStruct((1, half), F32), jax.ShapeDtypeStruct(w_s.shape, F32),
                 jax.ShapeDtypeStruct((GROUPS, CHUNK, 1), F32)]
    return pl.pallas_call(
        body, out_shape=out_shape, grid=(t // tm,),
        in_specs=_sgu_specs(t, half, tm) + [pl.BlockSpec((tm, half), lambda i: (i, 0)), whole(ln_g), whole(ln_b),
                                            whole(w_s), whole(bsb)],
        out_specs=[pl.BlockSpec((tm, 2 * half), lambda i: (i, 0)), wshape((1, half)), wshape((1, half)),
                   wshape(w_s.shape), wshape((GROUPS, CHUNK, 1))],
        scratch_shapes=[pltpu.VMEM((tm, half), F32)], name="sgu_bwd",
        compiler_params=_params(("arbitrary",)))(zp, zp, d_uv, ln_g, ln_b, w_s, bsb)


_SCALE = (QK_NOPE + QK_ROPE) ** -0.5


def _attn_scores(qn, qr, kn, kr, i, tq, n):
    s = lax.dot_general(qn, kn, _DN["nt"], preferred_element_type=F32)
    s = (s + lax.dot_general(qr, kr, _DN["nt"], preferred_element_type=F32)) * _SCALE
    row = i * tq + lax.broadcasted_iota(jnp.int32, (tq, n), 0)
    col = lax.broadcasted_iota(jnp.int32, (tq, n), 1)
    return jnp.where(col <= row, s, NEG_INF)


def _attn_specs(seq):
    head = lambda b, h: (b, h)
    return dict(
        qn=pl.BlockSpec((seq, QK_NOPE), head),
        qr=pl.BlockSpec((None, seq, QK_ROPE), lambda b, h: (h, b, 0)),
        kr=pl.BlockSpec((seq, QK_ROPE), lambda b, h: (b, 0)),
        lse=pl.BlockSpec((None, seq, 1), lambda b, h: (h, b, 0)),
    )


def _attn_fwd(qn, qr, kn, v, kr, seq):
    t = qn.shape[0]
    tq = min(seq, 2 * CHUNK)
    sp = _attn_specs(seq)

    def body(qn_ref, qr_ref, kn_ref, v_ref, kr_ref, o_ref, lse_ref):
        for i in range(seq // tq):
            rows, n = slice(i * tq, (i + 1) * tq), (i + 1) * tq
            s = _attn_scores(qn_ref[rows, :], qr_ref[rows, :], kn_ref[0:n, :], kr_ref[0:n, :], i, tq, n)
            m = jnp.max(s, axis=-1, keepdims=True)
            p = jnp.exp(s - m)
            l = jnp.sum(p, axis=-1, keepdims=True)
            o_ref[rows, :] = jnp.dot((p / l).astype(BF16), v_ref[0:n, :], preferred_element_type=F32).astype(BF16)
            lse_ref[rows, :] = m + jnp.log(l)

    return pl.pallas_call(
        body, out_shape=[jax.ShapeDtypeStruct((t, N_HEADS * V_DIM), BF16), jax.ShapeDtypeStruct((N_HEADS, t, 1), F32)],
        grid=(t // seq, N_HEADS), in_specs=[sp["qn"], sp["qr"], sp["qn"], sp["qn"], sp["kr"]],
        out_specs=[sp["qn"], sp["lse"]], name="attn_fwd",
        compiler_params=_params(("parallel", "arbitrary")))(qn, qr, kn, v, kr)


def _attn_bwd(qn, qr, kn, v, kr, do, lse, cos2, sin2, seq):
    t = qn.shape[0]
    tq = min(seq, 2 * CHUNK)
    sp = _attn_specs(seq)

    def body(qn_ref, qr_ref, kn_ref, v_ref, kr_ref, do_ref, lse_ref, cos_ref, sin_ref,
             dqn_ref, dkn_ref, dv_ref, dqc_ref, dqs_ref, dkr_ref, dk_acc, dv_acc, dkr_acc):
        dk_acc[...] = jnp.zeros_like(dk_acc)
        dv_acc[...] = jnp.zeros_like(dv_acc)
        dkr_acc[...] = jnp.zeros_like(dkr_acc)
        for i in range(seq // tq):
            rows, n = slice(i * tq, (i + 1) * tq), (i + 1) * tq
            q_n, q_r, d_o = qn_ref[rows, :], qr_ref[rows, :], do_ref[rows, :]
            k_n, k_r = kn_ref[0:n, :], kr_ref[0:n, :]
            s = _attn_scores(q_n, q_r, k_n, k_r, i, tq, n)
            p = jnp.exp(s - lse_ref[rows, :])
            dp = lax.dot_general(d_o, v_ref[0:n, :], _DN["nt"], preferred_element_type=F32)
            ds = (p * (dp - jnp.sum(p * dp, axis=-1, keepdims=True)) * _SCALE).astype(BF16)
            dqn_ref[rows, :] = jnp.dot(ds, k_n, preferred_element_type=F32).astype(BF16)
            dqr = jnp.dot(ds, k_r, preferred_element_type=F32)
            dqc_ref[rows, :] = (dqr * cos_ref[rows, :]).astype(BF16)
            dqs_ref[rows, :] = (dqr * sin_ref[rows, :]).astype(BF16)
            dk_acc[0:n, :] += lax.dot_general(ds, q_n, _DN["tn"], preferred_element_type=F32)
            dkr_acc[0:n, :] += lax.dot_general(ds, q_r, _DN["tn"], preferred_element_type=F32)
            dv_acc[0:n, :] += lax.dot_general(p.astype(BF16), d_o, _DN["tn"], preferred_element_type=F32)
        dkn_ref[...] = dk_acc[...].astype(BF16)
        dv_ref[...] = dv_acc[...].astype(BF16)
        h = pl.program_id(1)

        @pl.when(h == 0)
        def _():
            dkr_ref[...] = dkr_acc[...]

        @pl.when(h > 0)
        def _():
            dkr_ref[...] += dkr_acc[...]

    wide = jax.ShapeDtypeStruct((t, N_HEADS * V_DIM), BF16)
    rope = jax.ShapeDtypeStruct((N_HEADS, t, QK_ROPE), BF16)
    krf = pl.BlockSpec((seq, QK_ROPE), lambda b, h: (b, 0))
    return pl.pallas_call(
        body, out_shape=[wide, wide, wide, rope, rope, jax.ShapeDtypeStruct((t, QK_ROPE), F32)],
        grid=(t // seq, N_HEADS),
        in_specs=[sp["qn"], sp["qr"], sp["qn"], sp["qn"], sp["kr"], sp["qn"], sp["lse"], krf, krf],
        out_specs=[sp["qn"], sp["qn"], sp["qn"], sp["qr"], sp["qr"], krf],
        scratch_shapes=[pltpu.VMEM((seq, QK_NOPE), F32), pltpu.VMEM((seq, V_DIM), F32), pltpu.VMEM((seq, QK_ROPE), F32)],
        name="attn_bwd", compiler_params=_params(("parallel", "arbitrary")))(qn, qr, kn, v, kr, do, lse, cos2, sin2)


def _row_tile(rows, cols, row_mult=8):
    cap = max(row_mult, (1 << 18) // cols)
    best = rows
    for tr in range(row_mult, min(rows, cap) + 1, row_mult):
        if rows % tr == 0:
            best = tr
    return best if rows > cap else rows


def _adamw_math(w, g, m, v):
    mv = ADAM_B1 * m + (1.0 - ADAM_B1) * g
    vv = ADAM_B2 * v + (1.0 - ADAM_B2) * (g * g)
    m_hat = mv / (1.0 - ADAM_B1 ** ADAM_STEP)
    v_hat = vv / (1.0 - ADAM_B2 ** ADAM_STEP)
    return -ADAM_LR * (m_hat / (jnp.sqrt(v_hat) + ADAM_EPS) + ADAM_WD * w), mv, vv


def _adamw(w, g, m, v):
    shape = w.shape
    c = shape[-1]
    r = w.size // c
    tr = _row_tile(r, c)

    def body(w_ref, g_ref, m_ref, v_ref, d_ref, nm_ref, nv_ref):
        d_ref[...], nm_ref[...], nv_ref[...] = _adamw_math(w_ref[...], g_ref[...], m_ref[...], v_ref[...])

    spec = pl.BlockSpec((tr, c), lambda i: (i, 0))
    outs = pl.pallas_call(body, out_shape=[jax.ShapeDtypeStruct((r, c), F32)] * 3, grid=(r // tr,),
                          in_specs=[spec] * 4, out_specs=[spec] * 3, name="adamw",
                          compiler_params=_params(("parallel",)))(*[a.reshape(r, c) for a in (w, g, m, v)])
    return [o.reshape(shape) for o in outs]


def _adamw_halves(w, m, v, l, j, own, recv, core, prev):
    nl, nj, rows, c = w.shape
    r = rows // 2
    tr = _row_tile(r, c)
    n_prev = 0 if prev is None else 4

    def body(core_ref, w_ref, own_ref, recv_ref, m_ref, v_ref, *rest):
        g_ref, d_ref, nm_ref, nv_ref = rest[n_prev:]
        g = jnp.where(pl.program_id(0) == core_ref[0], own_ref[...], recv_ref[...])
        g_ref[...] = g
        d_ref[...], nm_ref[...], nv_ref[...] = _adamw_math(w_ref[...], g, m_ref[...], v_ref[...])

    nb = r // tr
    slab = pl.BlockSpec((None, None, tr, c), lambda h, i, cr: (l, j, h * nb + i, 0))
    half = pl.BlockSpec((tr, c), lambda h, i, cr: (i, 0))
    grid_spec = pltpu.PrefetchScalarGridSpec(num_scalar_prefetch=1, grid=(2, nb),
                                             in_specs=[slab, half, half, slab, slab] + [_ANY] * n_prev,
                                             out_specs=[slab] * 4)
    return pl.pallas_call(body, out_shape=[jax.ShapeDtypeStruct(w.shape, F32)] * 4, grid_spec=grid_spec,
                          input_output_aliases={6 + q: q for q in range(n_prev)}, name="adamw_halves",
                          compiler_params=_params(("parallel",) * 2))(core, w, own, recv, m, v, *(prev or ()))


def _place():
    x, y, c = lax.axis_index("x"), lax.axis_index("y"), lax.axis_index("c")
    return x, y, c, [(1 - x, y), (x, 1 - y), (1 - x, 1 - y)]


def _dma_sems(*counts):
    return [pltpu.SemaphoreType.DMA((n,)) for n in counts]


def _all_gather(bufs, collective_id, name):
    n = len(bufs)

    def body(*refs):
        ins, outs = refs[:n], refs[n:2 * n]
        send, recv, fsend, frecv, osend, orecv = refs[2 * n:]
        x, y, c, _ = _place()
        xn, yn, sib = (1 - x, y, c), (x, 1 - y, c), (x, y, 1 - c)
        k, kx, ky, kd = 2 * x + y, 2 * (1 - x) + y, 2 * x + 1 - y, 2 * (1 - x) + 1 - y
        _handshake([xn, yn, sib])

        def copy(src, dst, sems, i, to):
            return pltpu.make_async_remote_copy(src, dst, sems[0].at[i], sems[1].at[i], device_id=to, device_id_type=_MESH)

        ici, d2d, own_s = (send, recv), (fsend, frecv), (osend, orecv)
        started = [copy(ins[b], outs[b].at[k], own_s, b, sib) for b in range(n)]
        for first in (True, False):
            for b in range(n):
                mine = outs[b].at[k, c]
                if first:
                    started += [copy(ins[b].at[c, 0], mine.at[0], ici, 6 * b, xn), copy(ins[b].at[c, 1], mine.at[1], ici, 6 * b + 1, yn)]
                else:
                    started += [copy(ins[b].at[c, 1], mine.at[1], ici, 6 * b + 2, xn), copy(ins[b].at[c, 0], mine.at[0], ici, 6 * b + 3, yn)]
        for cp in started:
            cp.start()
        passed = []
        for b in range(n):
            for i, (src_chip, q, to) in enumerate([(kx, 0, yn), (ky, 1, xn)]):
                piece = outs[b].at[src_chip, c, q]
                copy(piece, piece, ici, 6 * b + i, to).wait_recv()
                cp = copy(piece, piece, ici, 6 * b + 4 + i, to)
                cp.start()
                passed.append(cp)
        for b in range(n):
            for i, (src_chip, q) in enumerate([(kx, 1), (ky, 0)]):
                piece = outs[b].at[src_chip, c, q]
                copy(piece, piece, ici, 6 * b + 2 + i, xn).wait_recv()
                half = outs[b].at[src_chip, c]
                cp = copy(half, half, d2d, 3 * b + i, sib)
                cp.start()
                passed.append(cp)
        for b in range(n):
            for i, q in enumerate([0, 1]):
                piece = outs[b].at[kd, c, q]
                copy(piece, piece, ici, 6 * b + 4 + i, xn).wait_recv()
            half = outs[b].at[kd, c]
            cp = copy(half, half, d2d, 3 * b + 2, sib)
            cp.start()
            passed.append(cp)
        for b in range(n):
            for i, src_chip in enumerate([kx, ky, kd]):
                half = outs[b].at[src_chip, 1 - c]
                copy(half, half, d2d, 3 * b + i, sib).wait_recv()
        for cp in started[n:] + passed:
            cp.wait_send()
        for cp in started[:n]:
            cp.wait()

    return _sequencer(body, [jax.ShapeDtypeStruct((N_SHARDS,) + b.shape, b.dtype) for b in bufs],
                      _dma_sems(6 * n, 6 * n, 3 * n, 3 * n, n, n), collective_id, name, bufs)


def _sequencer(body, out_type, sems, collective_id, name, args):
    return pl.kernel(body, out_type=out_type, mesh=plsc.ScalarSubcoreMesh(axis_name="sequencer", num_cores=1),
                     scratch_types=sems, compiler_params=pltpu.CompilerParams(collective_id=collective_id),
                     name=name)(*args)


def _handshake(peers):
    barrier = pltpu.get_barrier_semaphore()
    for peer in peers:
        pl.semaphore_signal(barrier, inc=1, device_id=peer, device_id_type=_MESH)
    pl.semaphore_wait(barrier, len(peers))


def _swap_halves(parts, collective_id, name):
    n = len(parts)

    def body(*refs):
        ins, outs = refs[:n], refs[n:2 * n]
        send, recv = refs[2 * n:]
        x, y, c, _ = _place()
        _handshake([(x, y, 1 - c)])
        cps = [pltpu.make_async_remote_copy(ins[b].at[:, pl.ds(1 - c, 1)], outs[b], send.at[b], recv.at[b],
                                            device_id=(x, y, 1 - c), device_id_type=_MESH) for b in range(n)]
        for cp in cps:
            cp.start()
        for cp in cps:
            cp.wait()

    return _sequencer(body, [jax.ShapeDtypeStruct((N_SHARDS, 1) + p.shape[2:], p.dtype) for p in parts],
                      _dma_sems(n, n), collective_id, name, parts)


def _by_shape(fn, first, second, scalar):
    out, groups = [None] * len(first), {}
    for i, p in enumerate(first):
        groups.setdefault(p.shape, []).append(i)
    for idx in groups.values():
        for i, r in zip(idx, fn([first[i] for i in idx], [second[i] for i in idx], scalar)):
            out[i] = r
    return out


def _add_half(parts, others, core):
    n = len(parts)
    _, _, r, c = parts[0].shape
    tr = _row_tile(r, c, 16)

    def body(core_ref, *refs):
        for q in range(n):
            refs[2 * n + q][...] = (refs[q][...].astype(F32) + refs[n + q][...].astype(F32)).astype(BF16)

    grid_spec = pltpu.PrefetchScalarGridSpec(
        num_scalar_prefetch=1, grid=(N_SHARDS, r // tr),
        in_specs=[pl.BlockSpec((None, None, tr, c), lambda k, i, cr: (k, cr[0], i, 0))] * n
        + [pl.BlockSpec((None, None, tr, c), lambda k, i, cr: (k, 0, i, 0))] * n,
        out_specs=[pl.BlockSpec((None, tr, c), lambda k, i, cr: (k, i, 0))] * n)
    return pl.pallas_call(body, out_shape=[jax.ShapeDtypeStruct((N_SHARDS, r, c), BF16)] * n, grid_spec=grid_spec,
                          name="grad_add_half", compiler_params=_params(("parallel", "parallel")))(core, *parts, *others)


def _scatter_chips(parts, collective_id, name):
    n = len(parts)

    def body(*refs):
        ins, outs = refs[:n], refs[n:2 * n]
        send, recv = refs[2 * n:]
        x, y, c, chips = _place()
        k = 2 * x + y
        _handshake([(px, py, c) for px, py in chips])
        started = []
        for b in range(n):
            for j, (px, py) in enumerate(chips):
                cp = pltpu.make_async_remote_copy(ins[b].at[2 * px + py], outs[b].at[k], send.at[3 * b + j],
                                                  recv.at[3 * b + j], device_id=(px, py, c), device_id_type=_MESH)
                cp.start()
                started.append(cp)
        for b in range(n):
            for j, (px, py) in enumerate(chips):
                got = outs[b].at[2 * px + py]
                pltpu.make_async_remote_copy(got, got, send.at[3 * b + j], recv.at[3 * b + j],
                                             device_id=(px, py, c), device_id_type=_MESH).wait_recv()
        for cp in started:
            cp.wait_send()

    return _sequencer(body, [jax.ShapeDtypeStruct(p.shape, p.dtype) for p in parts], _dma_sems(3 * n, 3 * n),
                      collective_id, name, parts)


def _sum_slots(slots, mine, chip):
    n = len(slots)
    _, r, c = slots[0].shape
    tr = _row_tile(r, c, 16)

    def body(chip_ref, *refs):
        for q in range(n):
            own = refs[5 * q + 4][...].astype(F32)
            v = [jnp.where(chip_ref[0] == s, own, refs[5 * q + s][...].astype(F32)) for s in range(N_SHARDS)]
            refs[5 * n + q][...] = ((v[0] + v[1]) + v[2]) + v[3]

    def slot_spec(s):
        return pl.BlockSpec((None, tr, c), lambda i, kr: (jnp.where(kr[0] == s, (s + 1) % N_SHARDS, s), i, 0))

    per_buffer = [slot_spec(s) for s in range(N_SHARDS)] + [pl.BlockSpec((None, tr, c), lambda i, kr: (kr[0], i, 0))]
    grid_spec = pltpu.PrefetchScalarGridSpec(num_scalar_prefetch=1, grid=(r // tr,), in_specs=per_buffer * n,
                                             out_specs=[pl.BlockSpec((tr, c), lambda i, kr: (i, 0))] * n)
    args = [a for sl, mn in zip(slots, mine) for a in (sl, sl, sl, sl, mn)]
    return pl.pallas_call(body, out_shape=[jax.ShapeDtypeStruct((r, c), F32)] * n, grid_spec=grid_spec,
                          name="grad_sum_slots", compiler_params=_params(("parallel",)))(chip, *args)


def _join_halves(halves, collective_id, name):
    n = len(halves)

    def body(*refs):
        ins, outs = refs[:n], refs[n:2 * n]
        send, recv = refs[2 * n:]
        x, y, c, _ = _place()
        _handshake([(x, y, 1 - c)])
        cps = [pltpu.make_async_remote_copy(ins[b], outs[b], send.at[b], recv.at[b], device_id=(x, y, 1 - c),
                                            device_id_type=_MESH) for b in range(n)]
        for cp in cps:
            cp.start()
        for cp in cps:
            cp.wait()

    return _sequencer(body, [jax.ShapeDtypeStruct(h.shape, F32) for h in halves], _dma_sems(n, n), collective_id,
                      name, halves)


def _gather_rows(buf, start, rows):
    def body(in_ref, out_ref, send, recv, lsem):
        x, y, c, chips = _place()
        k = 2 * x + y
        src = in_ref.at[pl.ds(start, rows)]
        local = pltpu.make_async_remote_copy(src, out_ref.at[k], lsem.at[0], lsem.at[1], device_id=(x, y, 1 - c),
                                             device_id_type=_MESH)
        local.start()
        cps = [pltpu.make_async_remote_copy(src, out_ref.at[k], send.at[j], recv.at[j], device_id=(px, py, c),
                                            device_id_type=_MESH) for j, (px, py) in enumerate(chips)]
        for cp in cps:
            cp.start()
        for j, (px, py) in enumerate(chips):
            got = out_ref.at[2 * px + py]
            pltpu.make_async_remote_copy(got, got, send.at[j], recv.at[j], device_id=(px, py, c),
                                         device_id_type=_MESH).wait_recv()
        for cp in cps:
            cp.wait_send()
        local.wait()

    return pl.pallas_call(body, out_shape=jax.ShapeDtypeStruct((N_SHARDS, rows, buf.shape[1]), F32),
                          in_specs=[_ANY], out_specs=_ANY, scratch_shapes=_dma_sems(3, 3, 2),
                          name="gather_replicated_grads")(buf)


def _all_sum(vec):
    r, c = vec.shape
    n_dev = 2 * N_SHARDS

    def body(in_ref, out_ref, slots, send, recv):
        x, y, cc, _ = _place()
        flip = lambda v, bit: 1 - v if bit else v
        peers = [(flip(x, (q >> 2) & 1), flip(y, (q >> 1) & 1), flip(cc, q & 1)) for q in range(1, n_dev)]
        index = lambda p: 4 * p[0] + 2 * p[1] + p[2]
        slots[index((x, y, cc))] = in_ref[...]
        cps = [pltpu.make_async_remote_copy(in_ref, slots.at[index((x, y, cc))], send.at[q], recv.at[q], device_id=p,
                                            device_id_type=_MESH) for q, p in enumerate(peers)]
        for cp in cps:
            cp.start()
        for q, p in enumerate(peers):
            got = slots.at[index(p)]
            pltpu.make_async_remote_copy(got, got, send.at[q], recv.at[q], device_id=p, device_id_type=_MESH).wait_recv()
        for cp in cps:
            cp.wait_send()
        acc = slots[0]
        for s in range(1, n_dev):
            acc = acc + slots[s]
        out_ref[...] = acc

    vmem = pl.BlockSpec(memory_space=pltpu.VMEM)
    return pl.pallas_call(body, out_shape=jax.ShapeDtypeStruct((r, c), F32), in_specs=[vmem], out_specs=vmem,
                          scratch_shapes=[pltpu.VMEM((n_dev, r, c), F32)] + _dma_sems(n_dev - 1, n_dev - 1),
                          name="sum_small_grads")(vec)


def _not_before(value, other):
    return lax.optimization_barrier((value, other))[0]


def _round_up(n, m):
    return -(-n // m) * m


def _pack_flat(vecs, rows, width, dtype):
    flat = jnp.concatenate([v.reshape(-1).astype(dtype) for v in vecs])
    return jnp.pad(flat, (0, rows * width - flat.size)).reshape(rows, width)


def _split_flat(flat, shapes):
    out, off = [], 0
    for s in shapes:
        n = math.prod(s)
        out.append(flat[off:off + n].reshape(s))
        off += n
    return out


def _merge_shards(arr4, axis):
    a = jnp.moveaxis(arr4, 0, axis)
    s = list(a.shape)
    return a.reshape(s[:axis] + [s[axis] * s[axis + 1]] + s[axis + 2:])


def _split_shards(full, axis):
    s = list(full.shape)
    a = full.reshape(s[:axis] + [N_SHARDS, s[axis] // N_SHARDS] + s[axis + 1:])
    return jnp.moveaxis(a, axis, 0).reshape(N_SHARDS, -1)


def _rot_cols(w):
    half = w.shape[-1] // 2
    return jnp.concatenate([-w[..., half:], w[..., :half]], axis=-1)


def _unrot_cols(dw):
    half = dw.shape[-1] // 2
    return jnp.concatenate([dw[..., half:], -dw[..., :half]], axis=-1)


def kernel(x, positions, ffn_pre_g, ffn_post_g, ffn_w_gate, ffn_w_up, ffn_w_down, mix_pre_g, mix_post_g, gmlp_w_in, gmlp_ln_g, gmlp_ln_b, gmlp_w_s, gmlp_b_s, gmlp_w_out, kv_norm_g, w_dkv, kv_a_norm_g, w_ukv, mla_w_dq, mla_q_norm_g, mla_w_uq, mla_w_o, loss_target, m_ffn_pre_g, m_ffn_post_g, m_ffn_w_gate, m_ffn_w_up, m_ffn_w_down, m_mix_pre_g, m_mix_post_g, m_gmlp_w_in, m_gmlp_ln_g, m_gmlp_ln_b, m_gmlp_w_s, m_gmlp_b_s, m_gmlp_w_out, m_kv_norm_g, m_w_dkv, m_kv_a_norm_g, m_w_ukv, m_mla_w_dq, m_mla_q_norm_g, m_mla_w_uq, m_mla_w_o, v_ffn_pre_g, v_ffn_post_g, v_ffn_w_gate, v_ffn_w_up, v_ffn_w_down, v_mix_pre_g, v_mix_post_g, v_gmlp_w_in, v_gmlp_ln_g, v_gmlp_ln_b, v_gmlp_w_s, v_gmlp_b_s, v_gmlp_w_out, v_kv_norm_g, v_w_dkv, v_kv_a_norm_g, v_w_ukv, v_mla_w_dq, v_mla_q_norm_g, v_mla_w_uq, v_mla_w_o):
    names = ["ffn_pre_g", "ffn_post_g", "ffn_w_gate", "ffn_w_up", "ffn_w_down", "mix_pre_g", "mix_post_g", "gmlp_w_in",
             "gmlp_ln_g", "gmlp_ln_b", "gmlp_w_s", "gmlp_b_s", "gmlp_w_out", "kv_norm_g", "w_dkv", "kv_a_norm_g", "w_ukv",
             "mla_w_dq", "mla_q_norm_g", "mla_w_uq", "mla_w_o"]
    env = locals()
    w = {n: env[n] for n in names}
    mom = {n: env["m_" + n] for n in names}
    var = {n: env["v_" + n] for n in names}

    bsz, seq, d = x.shape
    t = bsz * seq
    core = lax.axis_index("c").astype(jnp.int32).reshape(1)

    mats = [("gmlp_w_in", 2), ("gmlp_w_out", 1), ("w_dkv", 0), ("w_ukv", 1), ("mla_w_dq", 1), ("mla_w_uq", 2),
            ("mla_w_o", 1)]
    vecs = [("ffn_pre_g", 2), ("ffn_post_g", 2), ("gmlp_ln_g", 1), ("gmlp_ln_b", 1)]
    replicated = ["mix_pre_g", "mix_post_g", "gmlp_w_s", "gmlp_b_s", "kv_norm_g", "kv_a_norm_g", "mla_q_norm_g"]
    n_mats = sum(w[n].size for n, _ in mats)
    n_vecs = sum(w[n].size for n, _ in vecs)
    mat_rows = _round_up(-(-n_mats // PACK_WIDTH), 64)
    vec_rows = _round_up(-(-n_vecs // 128), 32)
    mat_pack = _pack_flat([w[n] for n, _ in mats], mat_rows, PACK_WIDTH, BF16).reshape(2, 2, mat_rows // 4, PACK_WIDTH)
    vec_pack = _pack_flat([w[n] for n, _ in vecs], vec_rows, 128, F32).reshape(2, 2, vec_rows // 4, 128)
    ffn_names = ("ffn_w_gate", "ffn_w_up", "ffn_w_down")

    def oriented(a, name):
        return a if name == "ffn_w_down" else jnp.swapaxes(a, 2, 3)

    lj = [(l, j) for l in range(2) for j in range(2)]
    plan = [((0, 0), (0, 1), [vec_pack], None), ((0, 0), (2,), [], 0), ((0, 1), (0, 1, 2), [mat_pack], 0),
            ((1, 0), (0, 1, 2), [], 0), ((1, 1), (0, 1, 2), [], 2)]
    ffn_w = {k: [None] * 3 for k in lj}
    landed = []
    for q, ((l, j), which, riders, after) in enumerate(plan):
        shards = [oriented(w[ffn_names[i]], ffn_names[i])[l, j].astype(BF16) for i in which]
        bufs = [s.reshape(2, 2, s.shape[0] // 4, s.shape[1]) for s in shards] + riders
        if after is not None:
            bufs = _not_before(bufs, landed[after])
        got = _all_gather(bufs, q + 1, f"gather_weights_{q}")
        landed.append(got[-1])
        for i, g, s in zip(which, got, shards):
            ffn_w[(l, j)][i] = g.reshape((N_SHARDS,) + s.shape)
        if riders and q == 0:
            vec_all = got[-1]
        if riders and q == 2:
            mat_all = got[-1]

    def unpack(packed, entries):
        flat4, off, out = packed.reshape(N_SHARDS, -1), 0, {}
        for n, ax in entries:
            out[n] = _merge_shards(flat4[:, off:off + w[n].size].reshape((N_SHARDS,) + w[n].shape), ax)
            off += w[n].size
        return out

    full = unpack(vec_all, vecs)
    ln_g, ln_b = full["gmlp_ln_g"], full["gmlp_ln_b"]
    pre_g, post_g = full["ffn_pre_g"], full["ffn_post_g"]
    w_s = w["gmlp_w_s"][0]
    bsb = w["gmlp_b_s"][0][:, :, None]
    row = lambda v: v.reshape(1, -1)

    inv_freq = ROPE_THETA ** (-jnp.arange(0, QK_ROPE, 2, dtype=F32) / QK_ROPE)
    ang = positions.astype(F32).reshape(t, 1) * inv_freq
    cos2 = jnp.concatenate([jnp.cos(ang)] * 2, axis=-1)
    sin2 = jnp.concatenate([jnp.sin(ang)] * 2, axis=-1)
    cos_h, sin_h = jnp.tile(cos2, (1, N_HEADS)), jnp.tile(sin2, (1, N_HEADS))

    def rope_epi(n_lin):
        def epi(accs, ex):
            return accs[:n_lin] + [accs[n_lin] * ex[0] + accs[n_lin + 1] * ex[1]]
        return epi

    h0 = x.reshape(t, d)
    saved = {}

    def ffn_fwd(l, j, h, n, next_gs):
        wg, wu, wd = ffn_w[(l, j)]
        g, u, a = _ffn_up(n, wg, wu)
        f, h_new, *n_next = _ffn_down(a, wd, h, row(post_g[l, j]), next_gs)
        saved[("ffn", l, j)] = (h, n, g, u, a, f)
        return h_new, n_next

    n0 = _rms_fwd(h0, row(pre_g[0, 0]))
    h1, (n1,) = ffn_fwd(0, 0, h0, n0, row(w["mix_pre_g"][0]))

    full.update(unpack(_not_before(mat_all, h1), mats))
    w_in, w_out = full["gmlp_w_in"][0], full["gmlp_w_out"][0]
    w_c, w_kr = full["w_dkv"][:, :KV_RANK], full["w_dkv"][:, KV_RANK:]
    w_kr_rot = _rot_cols(w_kr)
    ukv = full["w_ukv"].reshape(KV_RANK, N_HEADS, 2, QK_NOPE)
    w_k, w_v = ukv[:, :, 0].reshape(KV_RANK, -1), ukv[:, :, 1].reshape(KV_RANK, -1)
    w_dq, w_o = full["mla_w_dq"][0], full["mla_w_o"][0]
    q_rank = w_dq.shape[1]
    uq = full["mla_w_uq"][0].reshape(q_rank, N_HEADS, QK_NOPE + QK_ROPE)
    w_qn = uq[:, :, :QK_NOPE].reshape(q_rank, -1)
    w_qr = uq[:, :, QK_NOPE:].reshape(q_rank, -1)
    w_qr_rot = _rot_cols(uq[:, :, QK_NOPE:]).reshape(q_rank, -1)

    zp = _mm2d("gmlp_in", [(n1, w_in, "nn", 0)], [(w_in.shape[1], F32)])[0]
    uv = _sgu_fwd(zp, ln_g, ln_b, w_s, bsb)
    half = uv.shape[1]
    tm = min(t, ROW_TILE)
    m0, h2, n2 = _down("gmlp_out", (uv, (tm, 512), lambda i, _, k: (i, k)), (w_out, (512, d), lambda i, _, k: (k, 0)),
                       half // 512, h1, row(w["mix_post_g"][0]), row(pre_g[0, 1]), 1.0)
    h3, (n3kv, n3) = ffn_fwd(0, 1, h2, n2, jnp.stack([w["kv_norm_g"], pre_g[1, 0]]))

    def kv_epi(accs, ex):
        c_raw = accs[0]
        return [c_raw, _rms(c_raw, ex[2]), accs[1] * ex[0] + accs[2] * ex[1]]

    c_raw, c_n, k_r = _mm2d("kv_down", [(n3kv, w_c, "nn", 0), (n3kv, w_kr, "nn", 1), (n3kv, w_kr_rot, "nn", 2)],
                            [(KV_RANK, F32), (KV_RANK, BF16), (QK_ROPE, BF16)], kv_epi, [cos2, sin2],
                            [row(w["kv_a_norm_g"])])
    k_n, v_h = _mm2d("kv_up", [(c_n, w_k, "nn", 0), (c_n, w_v, "nn", 1)], [(w_k.shape[1], BF16), (w_v.shape[1], BF16)])

    h4, (n4,) = ffn_fwd(1, 0, h3, n3, row(w["mix_pre_g"][1]))
    qd, qn = _mm2d("q_down", [(n4, w_dq, "nn", 0)], [(q_rank, F32), (q_rank, BF16)],
                   lambda accs, ex: [accs[0], _rms(accs[0], ex[0])], [], [row(w["mla_q_norm_g"][0])])
    q_n, q_r = _mm2d("q_up", [(qn, w_qn, "nn", 0), (qn, w_qr, "nn", 1), (qn, w_qr_rot, "nn", 2)],
                     [(w_qn.shape[1], BF16), (w_qr.shape[1], BF16)], rope_epi(1), [cos_h, sin_h])
    q_r = q_r.reshape(t, N_HEADS, QK_ROPE).transpose(1, 0, 2)
    o, lse = _attn_fwd(q_n, q_r, k_n, v_h, k_r, seq)
    m1, h5, n5 = _down("attn_out", (o, (tm, 512), lambda i, _, k: (i, k)), (w_o, (512, d), lambda i, _, k: (k, 0)),
                       o.shape[1] // 512, h4, row(w["mix_post_g"][1]), row(pre_g[1, 1]), 1.0)
    y, _ = ffn_fwd(1, 1, h5, n5, row(pre_g[1, 1]))

    loss_part, dy = _loss_head(y, loss_target.reshape(t, d))
    loss = lax.psum(loss_part, ("x", "y", "c"))

    chip = (2 * lax.axis_index("x") + lax.axis_index("y")).astype(jnp.int32).reshape(1)
    rs = {}

    def rs_launch(gid, parts):
        rs[gid] = {"parts": parts, "others": _swap_halves(parts, 6 + gid, f"grad_swap_{gid}")}

    def rs_mid(gid, after):
        r = rs[gid]
        parts, others = _not_before((r["parts"], r["others"]), after)
        r["chip"] = _by_shape(_add_half, parts, others, core)
        r["slots"] = _scatter_chips(r["chip"], 11 + gid, f"grad_scatter_{gid}")
        return r["chip"]

    def rs_end(gid, after):
        r = rs[gid]
        slots, mine = _not_before((r["slots"], r["chip"]), after)
        r["own"] = _by_shape(_sum_slots, slots, mine, chip)
        r["recv"] = _join_halves(r["own"], 16 + gid, f"grad_join_{gid}")
        return r["own"]

    d_pre, d_post = {}, {}

    def ffn_bwd(l, j, gid, dh_out, extra=(), after_post=None):
        h, n, g, u, a, f = saved[("ffn", l, j)]
        df, d_post[(l, j)] = _norm_out_bwd("ffn_post_bwd", f, dh_out, row(post_g[l, j]), 0.5)
        if after_post is not None:
            df = after_post(df)
        wg, wu, wd = ffn_w[(l, j)]
        dg, du = _ffn_dact(df, wd, g, u)
        dwd = _ffn_dw_down(a, df)
        dwg, dwu = _ffn_dw_in(n, dg, du)
        parts = [p.reshape(N_SHARDS, 2, p.shape[1] // 2, p.shape[2]) for p in (dwg, dwu, dwd)]
        rs_launch(gid, parts)
        dg, du = _not_before((dg, du), parts)
        dh, d_pre[(l, j)], *rest = _ffn_dn(dg, du, wg, wu, h, dh_out, [(row(pre_g[l, j]), None)] + list(extra))
        return dh, rest

    dh5, _ = ffn_bwd(1, 1, 0, dy)
    dh5 = _not_before(dh5, rs_mid(0, dh5))

    dm1, g_mix_post1 = _norm_out_bwd("mix_post_bwd", m1, dh5, row(w["mix_post_g"][1]), 1.0)
    do = _mm2d("attn_out_dx", [(dm1, w_o, "nt", 0)], [(w_o.shape[0], BF16)])[0]
    g_w_o = _mm2d("attn_out_dw", [(o, dm1, "tn", 0)], [(d, BF16)])[0]
    dq_n, dk_n, dv_h, dq_c, dq_s, dk_r = _attn_bwd(q_n, q_r, k_n, v_h, k_r, do, lse, cos2, sin2, seq)
    dq_c = dq_c.transpose(1, 0, 2).reshape(t, -1)
    dq_s = dq_s.transpose(1, 0, 2).reshape(t, -1)
    dqn = _mm2d("q_up_dx", [(dq_n, w_qn, "nt", 0), (dq_c, w_qr, "nt", 0), (dq_s, w_qr_rot, "nt", 0)], [(q_rank, F32)])[0]
    g_qn, g_qr, g_qr_rot = _mm2d("q_up_dw", [(qn, dq_n, "tn", 0), (qn, dq_c, "tn", 1), (qn, dq_s, "tn", 2)],
                                 [(w_qn.shape[1], F32), (w_qr.shape[1], F32), (w_qr.shape[1], F32)])
    dqd, g_q_norm = _norm_out_bwd("q_norm_bwd", qd, dqn, row(w["mla_q_norm_g"][0]), 1.0)
    dn4 = _mm2d("q_down_dx", [(dqd, w_dq, "nt", 0)], [(d, F32)])[0]
    g_w_dq = _mm2d("q_down_dw", [(n4, dqd, "tn", 0)], [(q_rank, BF16)])[0]
    dh4, g_mix_pre1 = _norm_in_bwd("mix_pre_bwd", h4, dh5, [(row(w["mix_pre_g"][1]), dn4)])

    dc_n = _mm2d("kv_up_dx", [(dk_n, w_k, "nt", 0), (dv_h, w_v, "nt", 0)], [(KV_RANK, F32)])[0]
    g_wk, g_wv = _mm2d("kv_up_dw", [(c_n, dk_n, "tn", 0), (c_n, dv_h, "tn", 1)], [(w_k.shape[1], F32), (w_v.shape[1], F32)])
    dc, g_kv_a = _norm_out_bwd("kv_a_norm_bwd", c_raw, dc_n, row(w["kv_a_norm_g"]), 1.0)
    dkr_c, dkr_s = _rope_bwd(dk_r, cos2, sin2)
    dn3kv = _mm2d("kv_down_dx", [(dc, w_c, "nt", 0), (dkr_c, w_kr, "nt", 0), (dkr_s, w_kr_rot, "nt", 0)], [(d, F32)])[0]
    g_wc, g_wkr, g_wkr_rot = _mm2d("kv_down_dw", [(n3kv, dc, "tn", 0), (n3kv, dkr_c, "tn", 1), (n3kv, dkr_s, "tn", 2)],
                                   [(KV_RANK, F32), (QK_ROPE, F32), (QK_ROPE, F32)])

    dh4 = _not_before(dh4, rs_end(0, dh4))
    dh3, (g_kv_norm,) = ffn_bwd(1, 0, 1, dh4, extra=[(row(w["kv_norm_g"]), dn3kv)])
    dh3 = _not_before(dh3, rs_mid(1, dh3))
    dh2, _ = ffn_bwd(0, 1, 2, dh3)
    dh2 = _not_before(dh2, (rs_end(1, dh2), rs_mid(2, dh2)))

    dm0, g_mix_post0 = _norm_out_bwd("mix_post_bwd", m0, dh2, row(w["mix_post_g"][0]), 1.0)
    d_uv = _mm2d("gmlp_out_dx", [(dm0, w_out, "nt", 0)], [(half, F32)])[0]
    g_w_out = _mm2d("gmlp_out_dw", [(uv, dm0, "tn", 0)], [(d, BF16)])[0]
    dzp, g_ln_g, g_ln_b, g_w_s, g_b_s = _sgu_bwd(zp, d_uv, ln_g, ln_b, w_s, bsb)
    dn1 = _mm2d("gmlp_in_dx", [(dzp, w_in, "nt", 0)], [(d, F32)])[0]
    tk, tmw, w_cols = min(t, K_TILE), min(d, ROW_TILE), w_in.shape[1] // N_SHARDS
    g_w_in = _mm("gmlp_in_dw", (d // tmw, N_SHARDS, t // tk),
                 [(n1, (tk, tmw), lambda i, j, k: (k, i)), (dzp, (tk, w_cols), lambda i, j, k: (k, j))],
                 [(0, 1, 0, "tn")], [(tmw, w_cols)],
                 [((N_SHARDS, d, w_cols), BF16, (None, tmw, w_cols), lambda i, j, k: (j, i, 0))], lambda accs, ex: accs)[0]
    dh1, g_mix_pre0 = _norm_in_bwd("mix_pre_bwd", h1, dh2, [(row(w["mix_pre_g"][0]), dn1)])

    g_w_dkv = jnp.concatenate([g_wc, g_wkr + _unrot_cols(g_wkr_rot)], axis=1).astype(BF16)
    direct = {"gmlp_w_in": g_w_in, "gmlp_w_out": g_w_out, "mla_w_o": g_w_o, "mla_w_dq": g_w_dq, "w_dkv": g_w_dkv}
    direct = {n: g.reshape(N_SHARDS, -1, g.shape[-1]) for n, g in direct.items()}
    part = {
        "w_ukv": jnp.stack([g_wk.reshape(KV_RANK, N_HEADS, QK_NOPE), g_wv.reshape(KV_RANK, N_HEADS, V_DIM)],
                           axis=2).reshape(KV_RANK, -1),
        "mla_w_uq": jnp.concatenate(
            [g_qn.reshape(q_rank, N_HEADS, QK_NOPE),
             g_qr.reshape(q_rank, N_HEADS, QK_ROPE) + _unrot_cols(g_qr_rot.reshape(q_rank, N_HEADS, QK_ROPE))],
            axis=-1).reshape(1, q_rank, -1),
        "gmlp_ln_g": g_ln_g, "gmlp_ln_b": g_ln_b,
        "mix_pre_g": jnp.concatenate([g_mix_pre0, g_mix_pre1]), "mix_post_g": jnp.concatenate([g_mix_post0, g_mix_post1]),
        "gmlp_w_s": g_w_s[None], "gmlp_b_s": g_b_s.reshape(1, GROUPS, CHUNK),
        "kv_norm_g": g_kv_norm.reshape(-1), "kv_a_norm_g": g_kv_a.reshape(-1), "mla_q_norm_g": g_q_norm,
    }

    sharded = [e for e in mats + vecs if e[0] in part]
    n_sh = sum(w[n].size for n, _ in sharded)
    n_rep = sum(w[n].size for n in replicated)
    sh_rows = _round_up(-(-n_sh // PACK_WIDTH), 8)
    rep_rows = _round_up(-(-(n_rep // N_SHARDS) // PACK_WIDTH), 8)
    rows = _round_up(sh_rows + rep_rows, 32)
    sh_flat = jnp.concatenate([_split_shards(part[n], ax) for n, ax in sharded], axis=1)
    rep_flat = jnp.concatenate([part[n].reshape(-1) for n in replicated]).reshape(N_SHARDS, -1)
    small = jnp.concatenate([
        jnp.pad(sh_flat, ((0, 0), (0, sh_rows * PACK_WIDTH - n_sh))),
        jnp.pad(rep_flat, ((0, 0), (0, (rows - sh_rows) * PACK_WIDTH - n_rep // N_SHARDS)))], axis=1)
    small = small.astype(BF16).reshape(N_SHARDS, 2, rows // 2, PACK_WIDTH)

    rs_launch(3, [g.reshape(N_SHARDS, 2, g.shape[1] // 2, g.shape[2]) for g in direct.values()] + [small])
    dh1 = _not_before(dh1, rs_end(2, dh1))
    dx, _ = ffn_bwd(0, 0, 4, dh1, after_post=lambda df: _not_before(df, rs_mid(3, df)))

    launched = rs_mid(4, dx)
    lj = [(l, j) for l in range(2) for j in range(2)]
    tiny = jnp.concatenate([d_pre[k] for k in lj] + [d_post[k] for k in lj]).reshape(-1, 128)
    tiny = _all_sum(_not_before(tiny, launched)).reshape(2, 2, 2, d)
    shard_cols = d // N_SHARDS
    grads = {"ffn_pre_g": lax.dynamic_slice_in_dim(tiny[0], chip[0] * shard_cols, shard_cols, axis=2),
             "ffn_post_g": lax.dynamic_slice_in_dim(tiny[1], chip[0] * shard_cols, shard_cols, axis=2)}
    own_small, recv_small = rs_end(3, launched)[-1], rs[3]["recv"][-1]
    delta, new_m, new_v = {}, {}, {}
    for q, n in enumerate(direct):
        lead = lambda a: a.reshape((1, 1) + a.shape[-2:])
        upd = _adamw_halves(lead(w[n]), lead(mom[n]), lead(var[n]), 0, 0, rs[3]["own"][q], rs[3]["recv"][q], core, None)
        grads[n], delta[n], new_m[n], new_v[n] = [o.reshape(w[n].shape) for o in upd]
    g_small = jnp.where(core[0] == 0, jnp.concatenate([own_small, recv_small]), jnp.concatenate([recv_small, own_small]))
    g_rep = _gather_rows(g_small, sh_rows, rep_rows)
    for (n, _), g in zip(sharded, _split_flat(g_small.reshape(-1), [w[n].shape for n, _ in sharded])):
        grads[n] = g
    rep_vec = g_rep.reshape(N_SHARDS, -1)[:, :n_rep // N_SHARDS].reshape(-1)
    for n, g in zip(replicated, _split_flat(rep_vec, [w[n].shape for n in replicated])):
        grads[n] = g

    for n in names:
        if n not in ffn_names and n not in delta:
            delta[n], new_m[n], new_v[n] = _adamw(w[n], grads[n], mom[n], var[n])
    chain = {n: None for n in ffn_names}

    def ffn_update(gid, l, j):
        for q, n in enumerate(ffn_names):
            chain[n] = _adamw_halves(oriented(w[n], n), oriented(mom[n], n), oriented(var[n], n), l, j,
                                     rs[gid]["own"][q], rs[gid]["recv"][q], core, chain[n])

    ffn_update(0, 1, 1)
    ffn_update(1, 1, 0)
    ffn_update(2, 0, 1)
    rs_end(4, ([delta[n] for n in delta], [chain[n] for n in ffn_names]))
    ffn_update(4, 0, 0)
    for n in ffn_names:
        grads[n], delta[n], new_m[n], new_v[n] = [oriented(o, n) for o in chain[n]]
    return (loss, dx.reshape(x.shape), *[grads[n] for n in names], *[delta[n] for n in names],
            *[new_m[n] for n in names], *[new_v[n] for n in names])
```

```python
import math

import jax
import jax.numpy as jnp
from jax import lax
from jax.experimental import pallas as pl
from jax.experimental.pallas import tpu as pltpu
from jax.experimental.pallas import tpu_sc as plsc

F32, BF16 = jnp.float32, jnp.bfloat16

RMS_EPS, LN_EPS, NEG_INF = 1e-6, 1e-5, -1e30
N_HEADS, QK_NOPE, QK_ROPE, V_DIM, KV_RANK = 8, 128, 64, 128, 256
CHUNK, GROUPS = 128, 16
ROPE_THETA = 10000.0
ADAM_LR, ADAM_B1, ADAM_B2, ADAM_EPS, ADAM_WD, ADAM_STEP = 0.001, 0.9, 0.999, 1e-08, 0.01, 10
N_SHARDS = 4

VMEM_LIMIT_BYTES = 48 * 1024 * 1024
ROW_TILE = 512
K_TILE = 2048
PACK_WIDTH = 1024

_DN = {"nn": (((1,), (0,)), ((), ())), "nt": (((1,), (1,)), ((), ())), "tn": (((0,), (0,)), ((), ()))}
_MESH = pl.DeviceIdType.MESH
_ANY = pl.BlockSpec(memory_space=pl.ANY)


def _params(sem):
    return pltpu.CompilerParams(dimension_semantics=sem, vmem_limit_bytes=VMEM_LIMIT_BYTES)


def _mm(name, grid, ins, pairs, acc_shapes, outs, epilogue, extras=(), inner=0, sums=()):
    n_in, n_ex, n_out = len(ins), len(extras), len(outs)
    gk = grid[2]

    def body(*refs):
        in_refs, ex_refs = refs[:n_in], refs[n_in:n_in + n_ex]
        out_refs = refs[n_in + n_ex:n_in + n_ex + n_out]
        sum_refs = refs[n_in + n_ex + n_out:n_in + n_ex + n_out + len(sums)]
        acc_refs = refs[n_in + n_ex + n_out + len(sums):]
        parts = [None] * len(acc_shapes)
        for a, b, c, dims in pairs:
            for s in range(max(inner, 1)):
                lhs, rhs = (in_refs[a][s], in_refs[b][s]) if inner else (in_refs[a][...], in_refs[b][...])
                p = lax.dot_general(lhs, rhs, _DN[dims], preferred_element_type=F32)
                parts[c] = p if parts[c] is None else parts[c] + p

        def finish(accs):
            vals = epilogue(accs, [r[...] for r in ex_refs])
            for r, v in zip(out_refs, vals):
                r[...] = v.astype(r.dtype)
            first = (pl.program_id(0) == 0) & (pl.program_id(1) == 0)
            for r, v in zip(sum_refs, vals[n_out:]):
                @pl.when(first)
                def _():
                    r[...] = v

                @pl.when(jnp.logical_not(first))
                def _():
                    r[...] += v

        if gk == 1:
            finish(parts)
        else:
            k = pl.program_id(2)

            @pl.when(k == 0)
            def _():
                for r, p in zip(acc_refs, parts):
                    r[...] = p

            @pl.when(k > 0)
            def _():
                for r, p in zip(acc_refs, parts):
                    r[...] += p

            @pl.when(k == gk - 1)
            def _():
                finish([r[...] for r in acc_refs])

    return pl.pallas_call(
        body,
        out_shape=[jax.ShapeDtypeStruct(s, d) for s, d, _, _ in outs] + [jax.ShapeDtypeStruct(s, F32) for s in sums],
        grid=grid,
        in_specs=[pl.BlockSpec(bs, im) for _, bs, im in list(ins) + list(extras)],
        out_specs=[pl.BlockSpec(bs, im) for _, _, bs, im in outs]
        + [pl.BlockSpec(s, lambda i, j, k, nd=len(s): (0,) * nd) for s in sums],
        scratch_shapes=[pltpu.VMEM(s, F32) for s in acc_shapes] if gk > 1 else [],
        name=name,
        compiler_params=_params(("arbitrary",) * 3 if sums else ("parallel", "parallel", "arbitrary")),
    )(*[a for a, _, _ in ins], *[a for a, _, _ in extras])


def _mm2d(name, pairs, outs, epilogue=None, row_extras=(), vec_extras=()):
    def mk(a, dims):
        return (a.shape[0], a.shape[1]) if dims[0] == "n" else (a.shape[1], a.shape[0])

    def nk(b, dims):
        return (b.shape[1], b.shape[0]) if dims[1] == "n" else (b.shape[0], b.shape[1])

    m = mk(pairs[0][0], pairs[0][2])[0]
    ks = [mk(a, d)[1] for a, _, d, _ in pairs]
    n_acc = 1 + max(p[3] for p in pairs)
    acc_n = [None] * n_acc
    for a, b, d, c in pairs:
        assert mk(a, d)[0] == m and nk(b, d)[1] == mk(a, d)[1]
        acc_n[c] = nk(b, d)[0]
    tm = min(m, ROW_TILE)
    if len(set(ks)) == 1 and ks[0] > 1024:
        tk = K_TILE if ks[0] % K_TILE == 0 else 512
        tks, gk = [tk] * len(pairs), ks[0] // tk
    else:
        tks, gk = ks, 1
    if len(set(acc_n)) == 1 and acc_n[0] > 1024:
        tns, gj = [1024] * n_acc, acc_n[0] // 1024
    else:
        tns, gj = acc_n, 1

    ins, plist = [], []
    for (a, b, d, c), tk in zip(pairs, tks):
        tn = tns[c]
        a_spec = ((tm, tk), lambda i, j, k: (i, k)) if d[0] == "n" else ((tk, tm), lambda i, j, k: (k, i))
        b_spec = ((tk, tn), lambda i, j, k: (k, j)) if d[1] == "n" else ((tn, tk), lambda i, j, k: (j, k))
        ins += [(a, *a_spec), (b, *b_spec)]
        plist.append((len(ins) - 2, len(ins) - 1, c, d))
    extras = [(r, (tm, r.shape[1]), lambda i, j, k: (i, 0)) for r in row_extras]
    extras += [(v, v.shape, lambda i, j, k: (0, 0)) for v in vec_extras]
    out_specs = []
    for n, dt in outs:
        bn = 1024 if (gj > 1) else n
        out_specs.append(((m, n), dt, (tm, bn), lambda i, j, k: (i, j)))
    if epilogue is None:
        epilogue = lambda accs, ex: accs
    return _mm(name, (m // tm, gj, gk), ins, plist, [(tm, tn) for tn in tns], out_specs, epilogue, extras)


def _rms(x, g):
    return x * lax.rsqrt(jnp.mean(x * x, axis=-1, keepdims=True) + RMS_EPS) * g


def _rms_bwd(x, g, dy):
    r = lax.rsqrt(jnp.mean(x * x, axis=-1, keepdims=True) + RMS_EPS)
    gy = dy * g
    dx = r * gy - x * (r * r * r) * jnp.mean(gy * x, axis=-1, keepdims=True)
    return dx, jnp.sum(dy * x * r, axis=0, keepdims=True)


def _sigmoid(x):
    return 1.0 / (1.0 + jnp.exp(-x))


_GELU_C = math.sqrt(2.0 / math.pi)


def _gelu(x):
    return x * (0.5 * (1.0 + jnp.tanh(_GELU_C * (x + 0.044715 * (x * x * x)))))


def _gelu_grad(x):
    t = jnp.tanh(_GELU_C * (x + 0.044715 * (x * x * x)))
    return 0.5 * (1.0 + t) + 0.5 * x * (1.0 - t * t) * (_GELU_C * (1.0 + 3.0 * 0.044715 * (x * x)))


def _rows(name, row_ins, vec_ins, fn, row_outs, acc_outs=()):
    t = row_ins[0].shape[0]
    tm = min(t, ROW_TILE)
    nr, nv, no = len(row_ins), len(vec_ins), len(row_outs)

    def body(*refs):
        outs, incs = fn([r[...] for r in refs[:nr]], [r[...] for r in refs[nr:nr + nv]])
        for r, v in zip(refs[nr + nv:nr + nv + no], outs):
            r[...] = v.astype(r.dtype)
        i = pl.program_id(0)
        for r, v in zip(refs[nr + nv + no:], incs):
            @pl.when(i == 0)
            def _():
                r[...] = v

            @pl.when(i > 0)
            def _():
                r[...] += v

    in_specs = [pl.BlockSpec((tm, a.shape[1]), lambda i: (i, 0)) for a in row_ins]
    in_specs += [pl.BlockSpec(v.shape, lambda i, nd=v.ndim: (0,) * nd) for v in vec_ins]
    out_shape = [jax.ShapeDtypeStruct((t, c), dt) for c, dt in row_outs]
    out_shape += [jax.ShapeDtypeStruct(s, F32) for s in acc_outs]
    out_specs = [pl.BlockSpec((tm, c), lambda i: (i, 0)) for c, _ in row_outs]
    out_specs += [pl.BlockSpec(s, lambda i, nd=len(s): (0,) * nd) for s in acc_outs]
    return pl.pallas_call(body, out_shape=out_shape, grid=(t // tm,), in_specs=in_specs, out_specs=out_specs,
                          name=name, compiler_params=_params(("arbitrary",)))(*row_ins, *vec_ins)


def _rms_fwd(x, g):
    return _rows("rms_fwd", [x], [g], lambda r, v: ([_rms(r[0], v[0])], []), [(x.shape[1], BF16)])[0]


def _norm_out_bwd(name, f, d_out, g, scale):
    def fn(r, v):
        dx, dg = _rms_bwd(r[0], v[0], r[1] * scale)
        return [dx], [dg]

    c = f.shape[1]
    return _rows(name, [f, d_out], [g], fn, [(c, BF16)], [(1, c)])


def _norm_in_bwd(name, h, d_res, branches):
    nb = len(branches)

    def fn(r, v):
        dh, dgs = r[1], []
        for b in range(nb):
            dx, dg = _rms_bwd(r[0], v[b], r[2 + b])
            dh = dh + dx
            dgs.append(dg)
        return [dh], dgs

    c = h.shape[1]
    return _rows(name, [h, d_res] + [dn for _, dn in branches], [g for g, _ in branches], fn, [(c, F32)],
                 [(1, c)] * nb)


def _loss_head(y, target):
    d = y.shape[1]

    def fn(r, v):
        e = r[0] - r[1]
        s = jnp.sum(jnp.sum(e * e, axis=1, keepdims=True), axis=0, keepdims=True) * (0.5 / d)
        return [e * (1.0 / d)], [jnp.broadcast_to(s, (1, 128))]

    dy, acc = _rows("loss_head", [y, target], [], fn, [(d, F32)], [(1, 128)])
    return acc[0, 0], dy


def _rope_bwd(dk, cos2, sin2):
    c = dk.shape[1]
    return _rows("rope_bwd", [dk, cos2, sin2], [], lambda r, v: ([r[0] * r[1], r[0] * r[2]], []),
                 [(c, BF16), (c, BF16)])


def _ffn_up(n, wg, wu):
    t, d = n.shape
    fs = wg.shape[-2]
    tm = min(t, ROW_TILE)
    w_spec = ((None, fs, d), lambda s, i, k: (s, 0, 0))

    def epi(accs, ex):
        g, u = accs
        return [g, u, g * _sigmoid(g) * u]

    o_spec = ((None, tm, fs), lambda s, i, k: (s, i, 0))
    outs = [((N_SHARDS, t, fs), BF16, *o_spec)] * 3
    return _mm("ffn_up", (N_SHARDS, t // tm, 1),
               [(n, (tm, d), lambda s, i, k: (i, 0)), (wg, *w_spec), (wu, *w_spec)],
               [(0, 1, 0, "nt"), (0, 2, 1, "nt")], [(tm, fs)] * 2, outs, epi)


def _down(name, a_in, w_in, gk, h, post_g, next_gs, scale, inner=0):
    t, d = h.shape
    tm = min(t, ROW_TILE)
    kn = next_gs.shape[0]

    def epi(accs, ex):
        f, hv, pg, ng = accs[0], ex[0], ex[1], ex[2]
        hn = hv + scale * _rms(f, pg)
        return [f, hn] + [_rms(hn, ng[q:q + 1]) for q in range(kn)]

    row = ((tm, d), lambda i, j, k: (i, 0))
    outs = [((t, d), F32, *row), ((t, d), F32, *row)] + [((t, d), BF16, *row)] * kn
    extras = [(h, *row), (post_g, (1, d), lambda i, j, k: (0, 0)), (next_gs, (kn, d), lambda i, j, k: (0, 0))]
    return _mm(name, (t // tm, 1, gk), [a_in, w_in], [(0, 1, 0, "nn")], [(tm, d)], outs, epi, extras, inner)


def _ffn_down(a, wd, h, post_g, next_gs):
    t, d = h.shape
    fs = a.shape[-1]
    tm = min(t, ROW_TILE)
    return _down("ffn_down", (a, (N_SHARDS, tm, fs), lambda i, _, k: (0, i, 0)),
                 (wd, (N_SHARDS, fs, d), lambda i, _, k: (0, 0, 0)), 1, h, post_g, next_gs, 0.5, N_SHARDS)


def _ffn_dact(f, d_out, post_g, wd, g, u):
    t, d = f.shape
    fs = g.shape[-1]
    tm = min(t, ROW_TILE // 2)

    def body(f_ref, do_ref, pg_ref, wd_ref, g_ref, u_ref, df_ref, dg_ref, du_ref, dpg_ref):
        dfv, dpg = _rms_bwd(f_ref[...], pg_ref[...], do_ref[...] * 0.5)
        dfb = dfv.astype(BF16)
        df_ref[...] = dfb
        i = pl.program_id(0)

        @pl.when(i == 0)
        def _():
            dpg_ref[...] = dpg

        @pl.when(i > 0)
        def _():
            dpg_ref[...] += dpg

        for s in range(N_SHARDS):
            da = lax.dot_general(dfb, wd_ref[s], _DN["nt"], preferred_element_type=F32)
            gv, uv = g_ref[s].astype(F32), u_ref[s].astype(F32)
            sg = _sigmoid(gv)
            dg_ref[s] = (da * uv * (sg * (1.0 + gv * (1.0 - sg)))).astype(BF16)
            du_ref[s] = (da * (gv * sg)).astype(BF16)

    row = pl.BlockSpec((tm, d), lambda i: (i, 0))
    hid = pl.BlockSpec((N_SHARDS, tm, fs), lambda i: (0, i, 0))
    vec = pl.BlockSpec((1, d), lambda i: (0, 0))
    hid_shape = jax.ShapeDtypeStruct((N_SHARDS, t, fs), BF16)
    return pl.pallas_call(
        body, out_shape=[jax.ShapeDtypeStruct((t, d), BF16), hid_shape, hid_shape, jax.ShapeDtypeStruct((1, d), F32)],
        grid=(t // tm,), in_specs=[row, row, vec, pl.BlockSpec(wd.shape, lambda i: (0, 0, 0)), hid, hid],
        out_specs=[row, hid, hid, vec], name="ffn_dact", compiler_params=_params(("arbitrary",)))(
            f, d_out, post_g, wd, g, u)


def _ffn_dn(dg, du, wg, wu, h, d_res, branches):
    _, t, fs = dg.shape
    d = wg.shape[-1]
    tm = min(t, ROW_TILE)
    nb = len(branches)
    a_spec = ((N_SHARDS, tm, fs), lambda i, _, k: (0, i, 0))
    w_spec = ((N_SHARDS, fs, d), lambda i, _, k: (0, 0, 0))
    row = ((tm, d), lambda i, _, k: (i, 0))
    vec = ((1, d), lambda i, _, k: (0, 0))

    def epi(accs, ex):
        hv, dh = ex[0], ex[1]
        dns, gs = [accs[0]] + ex[2:1 + nb], ex[1 + nb:]
        dgs = []
        for dn, g in zip(dns, gs):
            dx, dgv = _rms_bwd(hv, g, dn)
            dh = dh + dx
            dgs.append(dgv)
        return [dh] + dgs

    extras = [(h, *row), (d_res, *row)] + [(dn, *row) for _, dn in branches[1:]] + [(g, *vec) for g, _ in branches]
    return _mm("ffn_dn", (t // tm, 1, 1), [(dg, *a_spec), (wg, *w_spec), (du, *a_spec), (wu, *w_spec)],
               [(0, 1, 0, "nn"), (2, 3, 0, "nn")], [(tm, d)], [((t, d), F32, *row)], epi, extras, inner=N_SHARDS,
               sums=[(1, d)] * nb)


def _ffn_dw_in(n, dg, du):
    _, t, fs = dg.shape
    d = n.shape[1]
    tk = min(t, K_TILE)
    a_spec = ((None, tk, fs), lambda s, _, k: (s, k, 0))
    o_spec = ((None, fs, d), lambda s, _, k: (s, 0, 0))
    outs = [((N_SHARDS, fs, d), BF16, *o_spec)] * 2
    return _mm("ffn_dw_in", (N_SHARDS, 1, t // tk), [(dg, *a_spec), (du, *a_spec), (n, (tk, d), lambda s, _, k: (k, 0))],
               [(0, 2, 0, "tn"), (1, 2, 1, "tn")], [(fs, d)] * 2, outs, lambda accs, ex: accs)


def _ffn_dw_down(a, df):
    _, t, fs = a.shape
    d = df.shape[1]
    tk = min(t, K_TILE)
    outs = [((N_SHARDS, fs, d), BF16, (None, fs, d), lambda s, _, k: (s, 0, 0))]
    return _mm("ffn_dw_down", (N_SHARDS, 1, t // tk),
               [(a, (None, tk, fs), lambda s, _, k: (s, k, 0)), (df, (tk, d), lambda s, _, k: (k, 0))],
               [(0, 1, 0, "tn")], [(fs, d)], outs, lambda accs, ex: accs)[0]


def _causal_weight(w):
    row = lax.broadcasted_iota(jnp.int32, (CHUNK, CHUNK), 0)
    col = lax.broadcasted_iota(jnp.int32, (CHUNK, CHUNK), 1)
    return row >= col, jnp.where(row >= col, w, 0.0).astype(BF16)


def _layer_norm(v, g, b):
    xc = v - jnp.mean(v, axis=-1, keepdims=True)
    rstd = lax.rsqrt(jnp.mean(xc * xc, axis=-1, keepdims=True) + LN_EPS)
    xhat = xc * rstd
    return xhat, rstd, xhat * g + b


def _sgu_specs(t, half, tm):
    return [pl.BlockSpec((tm, half), lambda i: (i, 0)), pl.BlockSpec((tm, half), lambda i: (i, 1))]


def _sgu_fwd(zp, ln_g, ln_b, w_s, bsb):
    t, half = zp.shape[0], zp.shape[1] // 2
    tm = min(t, 2 * CHUNK)

    def body(u_ref, v_ref, g_ref, b_ref, w_ref, bs_ref, o_ref):
        u = _gelu(u_ref[...])
        _, _, vn = _layer_norm(_gelu(v_ref[...]), g_ref[...], b_ref[...])
        vb = vn.astype(BF16)
        for g in range(GROUPS):
            _, wm = _causal_weight(w_ref[g])
            cols = slice(g * CHUNK, (g + 1) * CHUNK)
            for c in range(tm // CHUNK):
                rows = slice(c * CHUNK, (c + 1) * CHUNK)
                sv = jnp.dot(wm, vb[rows, cols], preferred_element_type=F32) + bs_ref[g]
                o_ref[rows, cols] = (u[rows, cols] * sv).astype(BF16)

    whole = lambda a: pl.BlockSpec(a.shape, lambda i, nd=a.ndim: (0,) * nd)
    return pl.pallas_call(
        body, out_shape=jax.ShapeDtypeStruct((t, half), BF16), grid=(t // tm,),
        in_specs=_sgu_specs(t, half, tm) + [whole(ln_g), whole(ln_b), whole(w_s), whole(bsb)],
        out_specs=pl.BlockSpec((tm, half), lambda i: (i, 0)), name="sgu_fwd",
        compiler_params=_params(("arbitrary",)))(zp, zp, ln_g, ln_b, w_s, bsb)


def _sgu_bwd(zp, d_uv, ln_g, ln_b, w_s, bsb):
    t, half = zp.shape[0], zp.shape[1] // 2
    tm = min(t, 2 * CHUNK)

    def body(u_ref, v_ref, d_ref, g_ref, b_ref, w_ref, bs_ref, dz_ref, dlg_ref, dlb_ref, dws_ref, dbs_ref, dvn_ref):
        i = pl.program_id(0)

        @pl.when(i == 0)
        def _():
            dlg_ref[...] = jnp.zeros_like(dlg_ref)
            dlb_ref[...] = jnp.zeros_like(dlb_ref)
            dws_ref[...] = jnp.zeros_like(dws_ref)
            dbs_ref[...] = jnp.zeros_like(dbs_ref)

        up, vp = u_ref[...], v_ref[...]
        u, gup = _gelu(up), _gelu_grad(up)
        xhat, rstd, vn = _layer_norm(_gelu(vp), g_ref[...], b_ref[...])
        vb = vn.astype(BF16)
        d = d_ref[...]
        for g in range(GROUPS):
            mask, wm = _causal_weight(w_ref[g])
            cols = slice(g * CHUNK, (g + 1) * CHUNK)
            for c in range(tm // CHUNK):
                rows = slice(c * CHUNK, (c + 1) * CHUNK)
                blk = vb[rows, cols]
                sv = jnp.dot(wm, blk, preferred_element_type=F32) + bs_ref[g]
                dblk = d[rows, cols]
                dz_ref[rows, cols] = (dblk * sv * gup[rows, cols]).astype(BF16)
                dsv = dblk * u[rows, cols]
                dsvb = dsv.astype(BF16)
                dvn_ref[rows, cols] = lax.dot_general(wm, dsvb, _DN["tn"], preferred_element_type=F32)
                dw = lax.dot_general(dsvb, blk, _DN["nt"], preferred_element_type=F32)
                dws_ref[g] += jnp.where(mask, dw, 0.0)
                dbs_ref[g] += jnp.sum(dsv, axis=1, keepdims=True)
        dvn = dvn_ref[...]
        dlg_ref[...] += jnp.sum(dvn * xhat, axis=0, keepdims=True)
        dlb_ref[...] += jnp.sum(dvn, axis=0, keepdims=True)
        dxh = dvn * g_ref[...]
        dv = rstd * (dxh - jnp.mean(dxh, axis=-1, keepdims=True)
                     - xhat * jnp.mean(dxh * xhat, axis=-1, keepdims=True))
        dz_ref[:, half:] = (dv * _gelu_grad(vp)).astype(BF16)

    whole = lambda a: pl.BlockSpec(a.shape, lambda i, nd=a.ndim: (0,) * nd)
    wshape = lambda s: pl.BlockSpec(s, lambda i, nd=len(s): (0,) * nd)
    out_shape = [jax.ShapeDtypeStruct((t, 2 * half), BF16), jax.ShapeDtypeStruct((1, half), F32),
                 jax.ShapeDtypeStruct((1, half), F32), jax.ShapeDtypeStruct(w_s.shape, F32),
                 jax.ShapeDtypeStruct((GROUPS, CHUNK, 1), F32)]
    return pl.pallas_call(
        body, out_shape=out_shape, grid=(t // tm,),
        in_specs=_sgu_specs(t, half, tm) + [pl.BlockSpec((tm, half), lambda i: (i, 0)), whole(ln_g), whole(ln_b),
                                            whole(w_s), whole(bsb)],
        out_specs=[pl.BlockSpec((tm, 2 * half), lambda i: (i, 0)), wshape((1, half)), wshape((1, half)),
                   wshape(w_s.shape), wshape((GROUPS, CHUNK, 1))],
        scratch_shapes=[pltpu.VMEM((tm, half), F32)], name="sgu_bwd",
        compiler_params=_params(("arbitrary",)))(zp, zp, d_uv, ln_g, ln_b, w_s, bsb)


_SCALE = (QK_NOPE + QK_ROPE) ** -0.5


def _attn_scores(qn, qr, kn, kr, i, tq, n):
    s = lax.dot_general(qn, kn, _DN["nt"], preferred_element_type=F32)
    s = (s + lax.dot_general(qr, kr, _DN["nt"], preferred_element_type=F32)) * _SCALE
    row = i * tq + lax.broadcasted_iota(jnp.int32, (tq, n), 0)
    col = lax.broadcasted_iota(jnp.int32, (tq, n), 1)
    return jnp.where(col <= row, s, NEG_INF)


def _attn_specs(seq):
    head = lambda b, h: (b, h)
    return dict(
        qn=pl.BlockSpec((seq, QK_NOPE), head),
        qr=pl.BlockSpec((None, seq, QK_ROPE), lambda b, h: (h, b, 0)),
        kr=pl.BlockSpec((seq, QK_ROPE), lambda b, h: (b, 0)),
        lse=pl.BlockSpec((None, seq, 1), lambda b, h: (h, b, 0)),
    )


def _attn_fwd(qn, qr, kn, v, kr, seq):
    t = qn.shape[0]
    tq = min(seq, 2 * CHUNK)
    sp = _attn_specs(seq)

    def body(qn_ref, qr_ref, kn_ref, v_ref, kr_ref, o_ref, lse_ref):
        for i in range(seq // tq):
            rows, n = slice(i * tq, (i + 1) * tq), (i + 1) * tq
            s = _attn_scores(qn_ref[rows, :], qr_ref[rows, :], kn_ref[0:n, :], kr_ref[0:n, :], i, tq, n)
            m = jnp.max(s, axis=-1, keepdims=True)
            p = jnp.exp(s - m)
            l = jnp.sum(p, axis=-1, keepdims=True)
            o_ref[rows, :] = jnp.dot((p / l).astype(BF16), v_ref[0:n, :], preferred_element_type=F32).astype(BF16)
            lse_ref[rows, :] = m + jnp.log(l)

    return pl.pallas_call(
        body, out_shape=[jax.ShapeDtypeStruct((t, N_HEADS * V_DIM), BF16), jax.ShapeDtypeStruct((N_HEADS, t, 1), F32)],
        grid=(t // seq, N_HEADS), in_specs=[sp["qn"], sp["qr"], sp["qn"], sp["qn"], sp["kr"]],
        out_specs=[sp["qn"], sp["lse"]], name="attn_fwd",
        compiler_params=_params(("parallel", "arbitrary")))(qn, qr, kn, v, kr)


def _attn_bwd(qn, qr, kn, v, kr, do, lse, cos2, sin2, seq):
    t = qn.shape[0]
    tq = min(seq, 2 * CHUNK)
    sp = _attn_specs(seq)

    def body(qn_ref, qr_ref, kn_ref, v_ref, kr_ref, do_ref, lse_ref, cos_ref, sin_ref,
             dqn_ref, dkn_ref, dv_ref, dqc_ref, dqs_ref, dkr_ref, dk_acc, dv_acc, dkr_acc):
        dk_acc[...] = jnp.zeros_like(dk_acc)
        dv_acc[...] = jnp.zeros_like(dv_acc)
        dkr_acc[...] = jnp.zeros_like(dkr_acc)
        for i in range(seq // tq):
            rows, n = slice(i * tq, (i + 1) * tq), (i + 1) * tq
            q_n, q_r, d_o = qn_ref[rows, :], qr_ref[rows, :], do_ref[rows, :]
            k_n, k_r = kn_ref[0:n, :], kr_ref[0:n, :]
            s = _attn_scores(q_n, q_r, k_n, k_r, i, tq, n)
            p = jnp.exp(s - lse_ref[rows, :])
            dp = lax.dot_general(d_o, v_ref[0:n, :], _DN["nt"], preferred_element_type=F32)
            ds = (p * (dp - jnp.sum(p * dp, axis=-1, keepdims=True)) * _SCALE).astype(BF16)
            dqn_ref[rows, :] = jnp.dot(ds, k_n, preferred_element_type=F32).astype(BF16)
            dqr = jnp.dot(ds, k_r, preferred_element_type=F32)
            dqc_ref[rows, :] = (dqr * cos_ref[rows, :]).astype(BF16)
            dqs_ref[rows, :] = (dqr * sin_ref[rows, :]).astype(BF16)
            dk_acc[0:n, :] += lax.dot_general(ds, q_n, _DN["tn"], preferred_element_type=F32)
            dkr_acc[0:n, :] += lax.dot_general(ds, q_r, _DN["tn"], preferred_element_type=F32)
            dv_acc[0:n, :] += lax.dot_general(p.astype(BF16), d_o, _DN["tn"], preferred_element_type=F32)
        dkn_ref[...] = dk_acc[...].astype(BF16)
        dv_ref[...] = dv_acc[...].astype(BF16)
        h = pl.program_id(1)

        @pl.when(h == 0)
        def _():
            dkr_ref[...] = dkr_acc[...]

        @pl.when(h > 0)
        def _():
            dkr_ref[...] += dkr_acc[...]

    wide = jax.ShapeDtypeStruct((t, N_HEADS * V_DIM), BF16)
    rope = jax.ShapeDtypeStruct((N_HEADS, t, QK_ROPE), BF16)
    krf = pl.BlockSpec((seq, QK_ROPE), lambda b, h: (b, 0))
    return pl.pallas_call(
        body, out_shape=[wide, wide, wide, rope, rope, jax.ShapeDtypeStruct((t, QK_ROPE), F32)],
        grid=(t // seq, N_HEADS),
        in_specs=[sp["qn"], sp["qr"], sp["qn"], sp["qn"], sp["kr"], sp["qn"], sp["lse"], krf, krf],
        out_specs=[sp["qn"], sp["qn"], sp["qn"], sp["qr"], sp["qr"], krf],
        scratch_shapes=[pltpu.VMEM((seq, QK_NOPE), F32), pltpu.VMEM((seq, V_DIM), F32), pltpu.VMEM((seq, QK_ROPE), F32)],
        name="attn_bwd", compiler_params=_params(("parallel", "arbitrary")))(qn, qr, kn, v, kr, do, lse, cos2, sin2)


def _row_tile(rows, cols, row_mult=8):
    cap = max(row_mult, (1 << 18) // cols)
    best = rows
    for tr in range(row_mult, min(rows, cap) + 1, row_mult):
        if rows % tr == 0:
            best = tr
    return best if rows > cap else rows


def _adamw_math(w, g, m, v):
    mv = ADAM_B1 * m + (1.0 - ADAM_B1) * g
    vv = ADAM_B2 * v + (1.0 - ADAM_B2) * (g * g)
    m_hat = mv / (1.0 - ADAM_B1 ** ADAM_STEP)
    v_hat = vv / (1.0 - ADAM_B2 ** ADAM_STEP)
    return -ADAM_LR * (m_hat / (jnp.sqrt(v_hat) + ADAM_EPS) + ADAM_WD * w), mv, vv


def _adamw(w, g, m, v):
    shape = w.shape
    c = shape[-1]
    r = w.size // c
    tr = _row_tile(r, c)

    def body(w_ref, g_ref, m_ref, v_ref, d_ref, nm_ref, nv_ref):
        d_ref[...], nm_ref[...], nv_ref[...] = _adamw_math(w_ref[...], g_ref[...], m_ref[...], v_ref[...])

    spec = pl.BlockSpec((tr, c), lambda i: (i, 0))
    outs = pl.pallas_call(body, out_shape=[jax.ShapeDtypeStruct((r, c), F32)] * 3, grid=(r // tr,),
                          in_specs=[spec] * 4, out_specs=[spec] * 3, name="adamw",
                          compiler_params=_params(("parallel",)))(*[a.reshape(r, c) for a in (w, g, m, v)])
    return [o.reshape(shape) for o in outs]


def _adamw_halves(w, m, v, l, j, own, recv, core, prev):
    nl, nj, rows, c = w.shape
    r = rows // 2
    tr = _row_tile(r, c)
    n_prev = 0 if prev is None else 4

    def body(core_ref, w_ref, own_ref, recv_ref, m_ref, v_ref, *rest):
        g_ref, d_ref, nm_ref, nv_ref = rest[n_prev:]
        g = jnp.where(pl.program_id(0) == core_ref[0], own_ref[...], recv_ref[...])
        g_ref[...] = g
        d_ref[...], nm_ref[...], nv_ref[...] = _adamw_math(w_ref[...], g, m_ref[...], v_ref[...])

    nb = r // tr
    slab = pl.BlockSpec((None, None, tr, c), lambda h, i, cr: (l, j, h * nb + i, 0))
    half = pl.BlockSpec((tr, c), lambda h, i, cr: (i, 0))
    grid_spec = pltpu.PrefetchScalarGridSpec(num_scalar_prefetch=1, grid=(2, nb),
                                             in_specs=[slab, half, half, slab, slab] + [_ANY] * n_prev,
                                             out_specs=[slab] * 4)
    return pl.pallas_call(body, out_shape=[jax.ShapeDtypeStruct(w.shape, F32)] * 4, grid_spec=grid_spec,
                          input_output_aliases={6 + q: q for q in range(n_prev)}, name="adamw_halves",
                          compiler_params=_params(("parallel",) * 2))(core, w, own, recv, m, v, *(prev or ()))


def _place():
    x, y, c = lax.axis_index("x"), lax.axis_index("y"), lax.axis_index("c")
    return x, y, c, [(1 - x, y), (x, 1 - y), (1 - x, 1 - y)]


def _dma_sems(*counts):
    return [pltpu.SemaphoreType.DMA((n,)) for n in counts]


def _all_gather(bufs, collective_id, name):
    n = len(bufs)

    def body(*refs):
        ins, outs = refs[:n], refs[n:2 * n]
        send, recv, fsend, frecv, osend, orecv = refs[2 * n:]
        x, y, c, _ = _place()
        xn, yn, sib = (1 - x, y, c), (x, 1 - y, c), (x, y, 1 - c)
        k, kx, ky, kd = 2 * x + y, 2 * (1 - x) + y, 2 * x + 1 - y, 2 * (1 - x) + 1 - y
        _handshake([xn, yn, sib])

        def copy(src, dst, sems, i, to):
            return pltpu.make_async_remote_copy(src, dst, sems[0].at[i], sems[1].at[i], device_id=to, device_id_type=_MESH)

        ici, d2d, own_s = (send, recv), (fsend, frecv), (osend, orecv)
        started = [copy(ins[b], outs[b].at[k], own_s, b, sib) for b in range(n)]
        for first in (True, False):
            for b in range(n):
                mine = outs[b].at[k, c]
                if first:
                    started += [copy(ins[b].at[c, 0], mine.at[0], ici, 6 * b, xn), copy(ins[b].at[c, 1], mine.at[1], ici, 6 * b + 1, yn)]
                else:
                    started += [copy(ins[b].at[c, 1], mine.at[1], ici, 6 * b + 2, xn), copy(ins[b].at[c, 0], mine.at[0], ici, 6 * b + 3, yn)]
        for cp in started:
            cp.start()
        passed = []
        for b in range(n):
            for i, (src_chip, q, to) in enumerate([(kx, 0, yn), (ky, 1, xn)]):
                piece = outs[b].at[src_chip, c, q]
                copy(piece, piece, ici, 6 * b + i, to).wait_recv()
                cp = copy(piece, piece, ici, 6 * b + 4 + i, to)
                cp.start()
                passed.append(cp)
        for b in range(n):
            for i, (src_chip, q) in enumerate([(kx, 1), (ky, 0)]):
                piece = outs[b].at[src_chip, c, q]
                copy(piece, piece, ici, 6 * b + 2 + i, xn).wait_recv()
                half = outs[b].at[src_chip, c]
                cp = copy(half, half, d2d, 3 * b + i, sib)
                cp.start()
                passed.append(cp)
        for b in range(n):
            for i, q in enumerate([0, 1]):
                piece = outs[b].at[kd, c, q]
                copy(piece, piece, ici, 6 * b + 4 + i, xn).wait_recv()
            half = outs[b].at[kd, c]
            cp = copy(half, half, d2d, 3 * b + 2, sib)
            cp.start()
            passed.append(cp)
        for b in range(n):
            for i, src_chip in enumerate([kx, ky, kd]):
                half = outs[b].at[src_chip, 1 - c]
                copy(half, half, d2d, 3 * b + i, sib).wait_recv()
        for cp in started[n:] + passed:
            cp.wait_send()
        for cp in started[:n]:
            cp.wait()

    return _sequencer(body, [jax.ShapeDtypeStruct((N_SHARDS,) + b.shape, b.dtype) for b in bufs],
                      _dma_sems(6 * n, 6 * n, 3 * n, 3 * n, n, n), collective_id, name, bufs)


def _sequencer(body, out_type, sems, collective_id, name, args):
    return pl.kernel(body, out_type=out_type, mesh=plsc.ScalarSubcoreMesh(axis_name="sequencer", num_cores=1),
                     scratch_types=sems, compiler_params=pltpu.CompilerParams(collective_id=collective_id),
                     name=name)(*args)


def _handshake(peers):
    barrier = pltpu.get_barrier_semaphore()
    for peer in peers:
        pl.semaphore_signal(barrier, inc=1, device_id=peer, device_id_type=_MESH)
    pl.semaphore_wait(barrier, len(peers))


def _swap_halves(parts, collective_id, name):
    n = len(parts)

    def body(*refs):
        ins, outs = refs[:n], refs[n:2 * n]
        send, recv = refs[2 * n:]
        x, y, c, _ = _place()
        _handshake([(x, y, 1 - c)])
        cps = [pltpu.make_async_remote_copy(ins[b].at[:, pl.ds(1 - c, 1)], outs[b], send.at[b], recv.at[b],
                                            device_id=(x, y, 1 - c), device_id_type=_MESH) for b in range(n)]
        for cp in cps:
            cp.start()
        for cp in cps:
            cp.wait()

    return _sequencer(body, [jax.ShapeDtypeStruct((N_SHARDS, 1) + p.shape[2:], p.dtype) for p in parts],
                      _dma_sems(n, n), collective_id, name, parts)


def _by_shape(fn, first, second, scalar):
    out, groups = [None] * len(first), {}
    for i, p in enumerate(first):
        groups.setdefault(p.shape, []).append(i)
    for idx in groups.values():
        for i, r in zip(idx, fn([first[i] for i in idx], [second[i] for i in idx], scalar)):
            out[i] = r
    return out


def _add_half(parts, others, core):
    n = len(parts)
    _, _, r, c = parts[0].shape
    tr = _row_tile(r, c, 16)

    def body(core_ref, *refs):
        for q in range(n):
            refs[2 * n + q][...] = (refs[q][...].astype(F32) + refs[n + q][...].astype(F32)).astype(BF16)

    grid_spec = pltpu.PrefetchScalarGridSpec(
        num_scalar_prefetch=1, grid=(N_SHARDS, r // tr),
        in_specs=[pl.BlockSpec((None, None, tr, c), lambda k, i, cr: (k, cr[0], i, 0))] * n
        + [pl.BlockSpec((None, None, tr, c), lambda k, i, cr: (k, 0, i, 0))] * n,
        out_specs=[pl.BlockSpec((None, tr, c), lambda k, i, cr: (k, i, 0))] * n)
    return pl.pallas_call(body, out_shape=[jax.ShapeDtypeStruct((N_SHARDS, r, c), BF16)] * n, grid_spec=grid_spec,
                          name="grad_add_half", compiler_params=_params(("parallel", "parallel")))(core, *parts, *others)


def _scatter_chips(parts, collective_id, name):
    n = len(parts)

    def body(*refs):
        ins, outs = refs[:n], refs[n:2 * n]
        send, recv = refs[2 * n:]
        x, y, c, chips = _place()
        k = 2 * x + y
        _handshake([(px, py, c) for px, py in chips])
        started = []
        for b in range(n):
            for j, (px, py) in enumerate(chips):
                cp = pltpu.make_async_remote_copy(ins[b].at[2 * px + py], outs[b].at[k], send.at[3 * b + j],
                                                  recv.at[3 * b + j], device_id=(px, py, c), device_id_type=_MESH)
                cp.start()
                started.append(cp)
        for b in range(n):
            for j, (px, py) in enumerate(chips):
                got = outs[b].at[2 * px + py]
                pltpu.make_async_remote_copy(got, got, send.at[3 * b + j], recv.at[3 * b + j],
                                             device_id=(px, py, c), device_id_type=_MESH).wait_recv()
        for cp in started:
            cp.wait_send()

    return _sequencer(body, [jax.ShapeDtypeStruct(p.shape, p.dtype) for p in parts], _dma_sems(3 * n, 3 * n),
                      collective_id, name, parts)


def _sum_slots(slots, mine, chip):
    n = len(slots)
    _, r, c = slots[0].shape
    tr = _row_tile(r, c, 16)

    def body(chip_ref, *refs):
        for q in range(n):
            own = refs[5 * q + 4][...].astype(F32)
            v = [jnp.where(chip_ref[0] == s, own, refs[5 * q + s][...].astype(F32)) for s in range(N_SHARDS)]
            refs[5 * n + q][...] = ((v[0] + v[1]) + v[2]) + v[3]

    def slot_spec(s):
        return pl.BlockSpec((None, tr, c), lambda i, kr: (jnp.where(kr[0] == s, (s + 1) % N_SHARDS, s), i, 0))

    per_buffer = [slot_spec(s) for s in range(N_SHARDS)] + [pl.BlockSpec((None, tr, c), lambda i, kr: (kr[0], i, 0))]
    grid_spec = pltpu.PrefetchScalarGridSpec(num_scalar_prefetch=1, grid=(r // tr,), in_specs=per_buffer * n,
                                             out_specs=[pl.BlockSpec((tr, c), lambda i, kr: (i, 0))] * n)
    args = [a for sl, mn in zip(slots, mine) for a in (sl, sl, sl, sl, mn)]
    return pl.pallas_call(body, out_shape=[jax.ShapeDtypeStruct((r, c), F32)] * n, grid_spec=grid_spec,
                          name="grad_sum_slots", compiler_params=_params(("parallel",)))(chip, *args)


def _join_halves(halves, collective_id, name):
    n = len(halves)

    def body(*refs):
        ins, outs = refs[:n], refs[n:2 * n]
        send, recv = refs[2 * n:]
        x, y, c, _ = _place()
        _handshake([(x, y, 1 - c)])
        cps = [pltpu.make_async_remote_copy(ins[b], outs[b], send.at[b], recv.at[b], device_id=(x, y, 1 - c),
                                            device_id_type=_MESH) for b in range(n)]
        for cp in cps:
            cp.start()
        for cp in cps:
            cp.wait()

    return _sequencer(body, [jax.ShapeDtypeStruct(h.shape, F32) for h in halves], _dma_sems(n, n), collective_id,
                      name, halves)


def _gather_rows(buf, start, rows):
    def body(in_ref, out_ref, send, recv, lsem):
        x, y, c, chips = _place()
        k = 2 * x + y
        src = in_ref.at[pl.ds(start, rows)]
        local = pltpu.make_async_remote_copy(src, out_ref.at[k], lsem.at[0], lsem.at[1], device_id=(x, y, 1 - c),
                                             device_id_type=_MESH)
        local.start()
        cps = [pltpu.make_async_remote_copy(src, out_ref.at[k], send.at[j], recv.at[j], device_id=(px, py, c),
                                            device_id_type=_MESH) for j, (px, py) in enumerate(chips)]
        for cp in cps:
            cp.start()
        for j, (px, py) in enumerate(chips):
            got = out_ref.at[2 * px + py]
            pltpu.make_async_remote_copy(got, got, send.at[j], recv.at[j], device_id=(px, py, c),
                                         device_id_type=_MESH).wait_recv()
        for cp in cps:
            cp.wait_send()
        local.wait()

    return pl.pallas_call(body, out_shape=jax.ShapeDtypeStruct((N_SHARDS, rows, buf.shape[1]), F32),
                          in_specs=[_ANY], out_specs=_ANY, scratch_shapes=_dma_sems(3, 3, 2),
                          name="gather_replicated_grads")(buf)


def _all_sum(vec):
    r, c = vec.shape
    n_dev = 2 * N_SHARDS

    def body(in_ref, out_ref, slots, send, recv):
        x, y, cc, _ = _place()
        flip = lambda v, bit: 1 - v if bit else v
        peers = [(flip(x, (q >> 2) & 1), flip(y, (q >> 1) & 1), flip(cc, q & 1)) for q in range(1, n_dev)]
        index = lambda p: 4 * p[0] + 2 * p[1] + p[2]
        slots[index((x, y, cc))] = in_ref[...]
        cps = [pltpu.make_async_remote_copy(in_ref, slots.at[index((x, y, cc))], send.at[q], recv.at[q], device_id=p,
                                            device_id_type=_MESH) for q, p in enumerate(peers)]
        for cp in cps:
            cp.start()
        for q, p in enumerate(peers):
            got = slots.at[index(p)]
            pltpu.make_async_remote_copy(got, got, send.at[q], recv.at[q], device_id=p, device_id_type=_MESH).wait_recv()
        for cp in cps:
            cp.wait_send()
        acc = slots[0]
        for s in range(1, n_dev):
            acc = acc + slots[s]
        out_ref[...] = acc

    vmem = pl.BlockSpec(memory_space=pltpu.VMEM)
    return pl.pallas_call(body, out_shape=jax.ShapeDtypeStruct((r, c), F32), in_specs=[vmem], out_specs=vmem,
                          scratch_shapes=[pltpu.VMEM((n_dev, r, c), F32)] + _dma_sems(n_dev - 1, n_dev - 1),
                          name="sum_small_grads")(vec)


def _not_before(value, other):
    return lax.optimization_barrier((value, other))[0]


def _round_up(n, m):
    return -(-n // m) * m


def _pack_flat(vecs, rows, width, dtype):
    flat = jnp.concatenate([v.reshape(-1).astype(dtype) for v in vecs])
    return jnp.pad(flat, (0, rows * width - flat.size)).reshape(rows, width)


def _split_flat(flat, shapes):
    out, off = [], 0
    for s in shapes:
        n = math.prod(s)
        out.append(flat[off:off + n].reshape(s))
        off += n
    return out


def _merge_shards(arr4, axis):
    a = jnp.moveaxis(arr4, 0, axis)
    s = list(a.shape)
    return a.reshape(s[:axis] + [s[axis] * s[axis + 1]] + s[axis + 2:])


def _split_shards(full, axis):
    s = list(full.shape)
    a = full.reshape(s[:axis] + [N_SHARDS, s[axis] // N_SHARDS] + s[axis + 1:])
    return jnp.moveaxis(a, axis, 0).reshape(N_SHARDS, -1)


def _rot_cols(w):
    half = w.shape[-1] // 2
    return jnp.concatenate([-w[..., half:], w[..., :half]], axis=-1)


def _unrot_cols(dw):
    half = dw.shape[-1] // 2
    return jnp.concatenate([dw[..., half:], -dw[..., :half]], axis=-1)


def kernel(x, positions, ffn_pre_g, ffn_post_g, ffn_w_gate, ffn_w_up, ffn_w_down, mix_pre_g, mix_post_g, gmlp_w_in, gmlp_ln_g, gmlp_ln_b, gmlp_w_s, gmlp_b_s, gmlp_w_out, kv_norm_g, w_dkv, kv_a_norm_g, w_ukv, mla_w_dq, mla_q_norm_g, mla_w_uq, mla_w_o, loss_target, m_ffn_pre_g, m_ffn_post_g, m_ffn_w_gate, m_ffn_w_up, m_ffn_w_down, m_mix_pre_g, m_mix_post_g, m_gmlp_w_in, m_gmlp_ln_g, m_gmlp_ln_b, m_gmlp_w_s, m_gmlp_b_s, m_gmlp_w_out, m_kv_norm_g, m_w_dkv, m_kv_a_norm_g, m_w_ukv, m_mla_w_dq, m_mla_q_norm_g, m_mla_w_uq, m_mla_w_o, v_ffn_pre_g, v_ffn_post_g, v_ffn_w_gate, v_ffn_w_up, v_ffn_w_down, v_mix_pre_g, v_mix_post_g, v_gmlp_w_in, v_gmlp_ln_g, v_gmlp_ln_b, v_gmlp_w_s, v_gmlp_b_s, v_gmlp_w_out, v_kv_norm_g, v_w_dkv, v_kv_a_norm_g, v_w_ukv, v_mla_w_dq, v_mla_q_norm_g, v_mla_w_uq, v_mla_w_o):
    names = ["ffn_pre_g", "ffn_post_g", "ffn_w_gate", "ffn_w_up", "ffn_w_down", "mix_pre_g", "mix_post_g", "gmlp_w_in",
             "gmlp_ln_g", "gmlp_ln_b", "gmlp_w_s", "gmlp_b_s", "gmlp_w_out", "kv_norm_g", "w_dkv", "kv_a_norm_g", "w_ukv",
             "mla_w_dq", "mla_q_norm_g", "mla_w_uq", "mla_w_o"]
    env = locals()
    w = {n: env[n] for n in names}
    mom = {n: env["m_" + n] for n in names}
    var = {n: env["v_" + n] for n in names}

    bsz, seq, d = x.shape
    t = bsz * seq
    core = lax.axis_index("c").astype(jnp.int32).reshape(1)

    mats = [("gmlp_w_in", 2), ("gmlp_w_out", 1), ("w_dkv", 0), ("w_ukv", 1), ("mla_w_dq", 1), ("mla_w_uq", 2),
            ("mla_w_o", 1)]
    vecs = [("ffn_pre_g", 2), ("ffn_post_g", 2), ("gmlp_ln_g", 1), ("gmlp_ln_b", 1)]
    replicated = ["mix_pre_g", "mix_post_g", "gmlp_w_s", "gmlp_b_s", "kv_norm_g", "kv_a_norm_g", "mla_q_norm_g"]
    n_mats = sum(w[n].size for n, _ in mats)
    n_vecs = sum(w[n].size for n, _ in vecs)
    mat_rows = _round_up(-(-n_mats // PACK_WIDTH), 64)
    vec_rows = _round_up(-(-n_vecs // 128), 32)
    mat_pack = _pack_flat([w[n] for n, _ in mats], mat_rows, PACK_WIDTH, BF16).reshape(2, 2, mat_rows // 4, PACK_WIDTH)
    vec_pack = _pack_flat([w[n] for n, _ in vecs], vec_rows, 128, F32).reshape(2, 2, vec_rows // 4, 128)
    ffn_names = ("ffn_w_gate", "ffn_w_up", "ffn_w_down")

    def oriented(a, name):
        return a if name == "ffn_w_down" else jnp.swapaxes(a, 2, 3)

    lj = [(l, j) for l in range(2) for j in range(2)]
    plan = [((0, 0), (0, 1), [vec_pack], None), ((0, 0), (2,), [mat_pack], 0), ((0, 1), (0, 1, 2), [], 0),
            ((1, 0), (0, 1, 2), [], 0), ((1, 1), (0, 1, 2), [], 0)]
    ffn_w = {k: [None] * 3 for k in lj}
    landed = []
    for q, ((l, j), which, riders, after) in enumerate(plan):
        shards = [oriented(w[ffn_names[i]], ffn_names[i])[l, j].astype(BF16) for i in which]
        bufs = [s.reshape(2, 2, s.shape[0] // 4, s.shape[1]) for s in shards] + riders
        if after is not None:
            bufs = _not_before(bufs, landed[after])
        got = _all_gather(bufs, q + 1, f"gather_weights_{q}")
        landed.append(got[-1])
        for i, g, s in zip(which, got, shards):
            ffn_w[(l, j)][i] = g.reshape((N_SHARDS,) + s.shape)
        if riders and q == 0:
            vec_all = got[-1]
        if riders and q == 1:
            mat_all = got[-1]

    def unpack(packed, entries):
        flat4, off, out = packed.reshape(N_SHARDS, -1), 0, {}
        for n, ax in entries:
            out[n] = _merge_shards(flat4[:, off:off + w[n].size].reshape((N_SHARDS,) + w[n].shape), ax)
            off += w[n].size
        return out

    full = unpack(vec_all, vecs)
    ln_g, ln_b = full["gmlp_ln_g"], full["gmlp_ln_b"]
    pre_g, post_g = full["ffn_pre_g"], full["ffn_post_g"]
    w_s = w["gmlp_w_s"][0]
    bsb = w["gmlp_b_s"][0][:, :, None]
    row = lambda v: v.reshape(1, -1)

    inv_freq = ROPE_THETA ** (-jnp.arange(0, QK_ROPE, 2, dtype=F32) / QK_ROPE)
    ang = positions.astype(F32).reshape(t, 1) * inv_freq
    cos2 = jnp.concatenate([jnp.cos(ang)] * 2, axis=-1)
    sin2 = jnp.concatenate([jnp.sin(ang)] * 2, axis=-1)
    cos_h, sin_h = jnp.tile(cos2, (1, N_HEADS)), jnp.tile(sin2, (1, N_HEADS))

    def rope_epi(n_lin):
        def epi(accs, ex):
            return accs[:n_lin] + [accs[n_lin] * ex[0] + accs[n_lin + 1] * ex[1]]
        return epi

    h0 = x.reshape(t, d)
    saved = {}

    def ffn_fwd(l, j, h, n, next_gs):
        wg, wu, wd = ffn_w[(l, j)]
        g, u, a = _ffn_up(n, wg, wu)
        f, h_new, *n_next = _ffn_down(a, wd, h, row(post_g[l, j]), next_gs)
        saved[("ffn", l, j)] = (h, n, g, u, a, f)
        return h_new, n_next

    n0 = _rms_fwd(h0, row(pre_g[0, 0]))
    h1, (n1,) = ffn_fwd(0, 0, h0, n0, row(w["mix_pre_g"][0]))

    full.update(unpack(_not_before(mat_all, h1), mats))
    w_in, w_out = full["gmlp_w_in"][0], full["gmlp_w_out"][0]
    w_c, w_kr = full["w_dkv"][:, :KV_RANK], full["w_dkv"][:, KV_RANK:]
    w_kr_rot = _rot_cols(w_kr)
    ukv = full["w_ukv"].reshape(KV_RANK, N_HEADS, 2, QK_NOPE)
    w_k, w_v = ukv[:, :, 0].reshape(KV_RANK, -1), ukv[:, :, 1].reshape(KV_RANK, -1)
    w_dq, w_o = full["mla_w_dq"][0], full["mla_w_o"][0]
    q_rank = w_dq.shape[1]
    uq = full["mla_w_uq"][0].reshape(q_rank, N_HEADS, QK_NOPE + QK_ROPE)
    w_qn = uq[:, :, :QK_NOPE].reshape(q_rank, -1)
    w_qr = uq[:, :, QK_NOPE:].reshape(q_rank, -1)
    w_qr_rot = _rot_cols(uq[:, :, QK_NOPE:]).reshape(q_rank, -1)

    zp = _mm2d("gmlp_in", [(n1, w_in, "nn", 0)], [(w_in.shape[1], F32)])[0]
    uv = _sgu_fwd(zp, ln_g, ln_b, w_s, bsb)
    half = uv.shape[1]
    tm = min(t, ROW_TILE)
    m0, h2, n2 = _down("gmlp_out", (uv, (tm, 512), lambda i, _, k: (i, k)), (w_out, (512, d), lambda i, _, k: (k, 0)),
                       half // 512, h1, row(w["mix_post_g"][0]), row(pre_g[0, 1]), 1.0)
    h3, (n3kv, n3) = ffn_fwd(0, 1, h2, n2, jnp.stack([w["kv_norm_g"], pre_g[1, 0]]))

    def kv_epi(accs, ex):
        c_raw = accs[0]
        return [c_raw, _rms(c_raw, ex[2]), accs[1] * ex[0] + accs[2] * ex[1]]

    c_raw, c_n, k_r = _mm2d("kv_down", [(n3kv, w_c, "nn", 0), (n3kv, w_kr, "nn", 1), (n3kv, w_kr_rot, "nn", 2)],
                            [(KV_RANK, F32), (KV_RANK, BF16), (QK_ROPE, BF16)], kv_epi, [cos2, sin2],
                            [row(w["kv_a_norm_g"])])
    k_n, v_h = _mm2d("kv_up", [(c_n, w_k, "nn", 0), (c_n, w_v, "nn", 1)], [(w_k.shape[1], BF16), (w_v.shape[1], BF16)])

    h4, (n4,) = ffn_fwd(1, 0, h3, n3, row(w["mix_pre_g"][1]))
    qd, qn = _mm2d("q_down", [(n4, w_dq, "nn", 0)], [(q_rank, F32), (q_rank, BF16)],
                   lambda accs, ex: [accs[0], _rms(accs[0], ex[0])], [], [row(w["mla_q_norm_g"][0])])
    q_n, q_r = _mm2d("q_up", [(qn, w_qn, "nn", 0), (qn, w_qr, "nn", 1), (qn, w_qr_rot, "nn", 2)],
                     [(w_qn.shape[1], BF16), (w_qr.shape[1], BF16)], rope_epi(1), [cos_h, sin_h])
    q_r = q_r.reshape(t, N_HEADS, QK_ROPE).transpose(1, 0, 2)
    o, lse = _attn_fwd(q_n, q_r, k_n, v_h, k_r, seq)
    m1, h5, n5 = _down("attn_out", (o, (tm, 512), lambda i, _, k: (i, k)), (w_o, (512, d), lambda i, _, k: (k, 0)),
                       o.shape[1] // 512, h4, row(w["mix_post_g"][1]), row(pre_g[1, 1]), 1.0)
    y, _ = ffn_fwd(1, 1, h5, n5, row(pre_g[1, 1]))

    loss_part, dy = _loss_head(y, loss_target.reshape(t, d))
    loss = lax.psum(loss_part, ("x", "y", "c"))

    chip = (2 * lax.axis_index("x") + lax.axis_index("y")).astype(jnp.int32).reshape(1)
    rs = {}

    def rs_launch(gid, parts):
        rs[gid] = {"parts": parts, "others": _swap_halves(parts, 7 + gid, f"grad_swap_{gid}")}

    def rs_mid(gid, after):
        r = rs[gid]
        parts, others = _not_before((r["parts"], r["others"]), after)
        r["chip"] = _by_shape(_add_half, parts, others, core)
        r["slots"] = _scatter_chips(r["chip"], 12 + gid, f"grad_scatter_{gid}")
        return r["chip"]

    def rs_end(gid, after):
        r = rs[gid]
        slots, mine = _not_before((r["slots"], r["chip"]), after)
        r["own"] = _by_shape(_sum_slots, slots, mine, chip)
        r["recv"] = _join_halves(r["own"], 17 + gid, f"grad_join_{gid}")
        return r["own"]

    d_pre, d_post = {}, {}

    def ffn_bwd(l, j, gid, dh_out, extra=()):
        h, n, g, u, a, f = saved[("ffn", l, j)]
        wg, wu, wd = ffn_w[(l, j)]
        df, dg, du, d_post[(l, j)] = _ffn_dact(f, dh_out, row(post_g[l, j]), wd, g, u)
        dwd = _ffn_dw_down(a, df)
        dwg, dwu = _ffn_dw_in(n, dg, du)
        parts = [p.reshape(N_SHARDS, 2, p.shape[1] // 2, p.shape[2]) for p in (dwg, dwu, dwd)]
        rs_launch(gid, parts)
        dg, du = _not_before((dg, du), parts)
        dh, d_pre[(l, j)], *rest = _ffn_dn(dg, du, wg, wu, h, dh_out, [(row(pre_g[l, j]), None)] + list(extra))
        return dh, rest

    dh5, _ = ffn_bwd(1, 1, 0, dy)
    dh5 = _not_before(dh5, rs_mid(0, dh5))

    dm1, g_mix_post1 = _norm_out_bwd("mix_post_bwd", m1, dh5, row(w["mix_post_g"][1]), 1.0)
    do = _mm2d("attn_out_dx", [(dm1, w_o, "nt", 0)], [(w_o.shape[0], BF16)])[0]
    g_w_o = _mm2d("attn_out_dw", [(o, dm1, "tn", 0)], [(d, BF16)])[0]
    dq_n, dk_n, dv_h, dq_c, dq_s, dk_r = _attn_bwd(q_n, q_r, k_n, v_h, k_r, do, lse, cos2, sin2, seq)
    dq_c = dq_c.transpose(1, 0, 2).reshape(t, -1)
    dq_s = dq_s.transpose(1, 0, 2).reshape(t, -1)
    dqn = _mm2d("q_up_dx", [(dq_n, w_qn, "nt", 0), (dq_c, w_qr, "nt", 0), (dq_s, w_qr_rot, "nt", 0)], [(q_rank, F32)])[0]
    g_qn, g_qr, g_qr_rot = _mm2d("q_up_dw", [(qn, dq_n, "tn", 0), (qn, dq_c, "tn", 1), (qn, dq_s, "tn", 2)],
                                 [(w_qn.shape[1], F32), (w_qr.shape[1], F32), (w_qr.shape[1], F32)])
    dqd, g_q_norm = _norm_out_bwd("q_norm_bwd", qd, dqn, row(w["mla_q_norm_g"][0]), 1.0)
    dn4 = _mm2d("q_down_dx", [(dqd, w_dq, "nt", 0)], [(d, F32)])[0]
    g_w_dq = _mm2d("q_down_dw", [(n4, dqd, "tn", 0)], [(q_rank, BF16)])[0]
    dh4, g_mix_pre1 = _norm_in_bwd("mix_pre_bwd", h4, dh5, [(row(w["mix_pre_g"][1]), dn4)])

    dc_n = _mm2d("kv_up_dx", [(dk_n, w_k, "nt", 0), (dv_h, w_v, "nt", 0)], [(KV_RANK, F32)])[0]
    g_wk, g_wv = _mm2d("kv_up_dw", [(c_n, dk_n, "tn", 0), (c_n, dv_h, "tn", 1)], [(w_k.shape[1], F32), (w_v.shape[1], F32)])
    dc, g_kv_a = _norm_out_bwd("kv_a_norm_bwd", c_raw, dc_n, row(w["kv_a_norm_g"]), 1.0)
    dkr_c, dkr_s = _rope_bwd(dk_r, cos2, sin2)
    dn3kv = _mm2d("kv_down_dx", [(dc, w_c, "nt", 0), (dkr_c, w_kr, "nt", 0), (dkr_s, w_kr_rot, "nt", 0)], [(d, F32)])[0]
    g_wc, g_wkr, g_wkr_rot = _mm2d("kv_down_dw", [(n3kv, dc, "tn", 0), (n3kv, dkr_c, "tn", 1), (n3kv, dkr_s, "tn", 2)],
                                   [(KV_RANK, F32), (QK_ROPE, F32), (QK_ROPE, F32)])

    dh4 = _not_before(dh4, rs_end(0, dh4))
    dh3, (g_kv_norm,) = ffn_bwd(1, 0, 1, dh4, extra=[(row(w["kv_norm_g"]), dn3kv)])
    dh3 = _not_before(dh3, rs_mid(1, dh3))
    dh2, _ = ffn_bwd(0, 1, 2, dh3)
    dh2 = _not_before(dh2, (rs_end(1, dh2), rs_mid(2, dh2)))

    dm0, g_mix_post0 = _norm_out_bwd("mix_post_bwd", m0, dh2, row(w["mix_post_g"][0]), 1.0)
    d_uv = _mm2d("gmlp_out_dx", [(dm0, w_out, "nt", 0)], [(half, F32)])[0]
    g_w_out = _mm2d("gmlp_out_dw", [(uv, dm0, "tn", 0)], [(d, BF16)])[0]
    dzp, g_ln_g, g_ln_b, g_w_s, g_b_s = _sgu_bwd(zp, d_uv, ln_g, ln_b, w_s, bsb)
    dn1 = _mm2d("gmlp_in_dx", [(dzp, w_in, "nt", 0)], [(d, F32)])[0]
    tk, tmw, w_cols = min(t, K_TILE), min(d, ROW_TILE), w_in.shape[1] // N_SHARDS
    g_w_in = _mm("gmlp_in_dw", (d // tmw, N_SHARDS, t // tk),
                 [(n1, (tk, tmw), lambda i, j, k: (k, i)), (dzp, (tk, w_cols), lambda i, j, k: (k, j))],
                 [(0, 1, 0, "tn")], [(tmw, w_cols)],
                 [((N_SHARDS, d, w_cols), BF16, (None, tmw, w_cols), lambda i, j, k: (j, i, 0))], lambda accs, ex: accs)[0]
    dh1, g_mix_pre0 = _norm_in_bwd("mix_pre_bwd", h1, dh2, [(row(w["mix_pre_g"][0]), dn1)])

    g_w_dkv = jnp.concatenate([g_wc, g_wkr + _unrot_cols(g_wkr_rot)], axis=1).astype(BF16)
    direct = {"gmlp_w_in": g_w_in, "gmlp_w_out": g_w_out, "mla_w_o": g_w_o, "mla_w_dq": g_w_dq, "w_dkv": g_w_dkv}
    direct = {n: g.reshape(N_SHARDS, -1, g.shape[-1]) for n, g in direct.items()}
    part = {
        "w_ukv": jnp.stack([g_wk.reshape(KV_RANK, N_HEADS, QK_NOPE), g_wv.reshape(KV_RANK, N_HEADS, V_DIM)],
                           axis=2).reshape(KV_RANK, -1),
        "mla_w_uq": jnp.concatenate(
            [g_qn.reshape(q_rank, N_HEADS, QK_NOPE),
             g_qr.reshape(q_rank, N_HEADS, QK_ROPE) + _unrot_cols(g_qr_rot.reshape(q_rank, N_HEADS, QK_ROPE))],
            axis=-1).reshape(1, q_rank, -1),
        "gmlp_ln_g": g_ln_g, "gmlp_ln_b": g_ln_b,
        "mix_pre_g": jnp.concatenate([g_mix_pre0, g_mix_pre1]), "mix_post_g": jnp.concatenate([g_mix_post0, g_mix_post1]),
        "gmlp_w_s": g_w_s[None], "gmlp_b_s": g_b_s.reshape(1, GROUPS, CHUNK),
        "kv_norm_g": g_kv_norm.reshape(-1), "kv_a_norm_g": g_kv_a.reshape(-1), "mla_q_norm_g": g_q_norm,
    }

    sharded = [e for e in mats + vecs if e[0] in part]
    n_sh = sum(w[n].size for n, _ in sharded)
    n_rep = sum(w[n].size for n in replicated)
    sh_rows = _round_up(-(-n_sh // PACK_WIDTH), 8)
    rep_rows = _round_up(-(-(n_rep // N_SHARDS) // PACK_WIDTH), 8)
    rows = _round_up(sh_rows + rep_rows, 32)
    sh_flat = jnp.concatenate([_split_shards(part[n], ax) for n, ax in sharded], axis=1)
    rep_flat = jnp.concatenate([part[n].reshape(-1) for n in replicated]).reshape(N_SHARDS, -1)
    small = jnp.concatenate([
        jnp.pad(sh_flat, ((0, 0), (0, sh_rows * PACK_WIDTH - n_sh))),
        jnp.pad(rep_flat, ((0, 0), (0, (rows - sh_rows) * PACK_WIDTH - n_rep // N_SHARDS)))], axis=1)
    small = small.astype(BF16).reshape(N_SHARDS, 2, rows // 2, PACK_WIDTH)

    rs_launch(3, [g.reshape(N_SHARDS, 2, g.shape[1] // 2, g.shape[2]) for g in direct.values()] + [small])
    dh1 = _not_before(dh1, rs_end(2, dh1))
    dh1 = _not_before(dh1, rs_mid(3, dh1))
    dx, _ = ffn_bwd(0, 0, 4, dh1)

    own3 = rs_end(3, dx)
    launched = rs_mid(4, (dx, own3))
    lj = [(l, j) for l in range(2) for j in range(2)]
    tiny = jnp.concatenate([d_pre[k] for k in lj] + [d_post[k] for k in lj]).reshape(-1, 128)
    tiny = _all_sum(_not_before(tiny, launched)).reshape(2, 2, 2, d)
    shard_cols = d // N_SHARDS
    grads = {"ffn_pre_g": lax.dynamic_slice_in_dim(tiny[0], chip[0] * shard_cols, shard_cols, axis=2),
             "ffn_post_g": lax.dynamic_slice_in_dim(tiny[1], chip[0] * shard_cols, shard_cols, axis=2)}
    own_small, recv_small = own3[-1], rs[3]["recv"][-1]
    delta, new_m, new_v = {}, {}, {}
    for q, n in enumerate(direct):
        lead = lambda a: a.reshape((1, 1) + a.shape[-2:])
        upd = _adamw_halves(lead(w[n]), lead(mom[n]), lead(var[n]), 0, 0, rs[3]["own"][q], rs[3]["recv"][q], core, None)
        grads[n], delta[n], new_m[n], new_v[n] = [o.reshape(w[n].shape) for o in upd]
    g_small = jnp.where(core[0] == 0, jnp.concatenate([own_small, recv_small]), jnp.concatenate([recv_small, own_small]))
    g_rep = _gather_rows(g_small, sh_rows, rep_rows)
    for (n, _), g in zip(sharded, _split_flat(g_small.reshape(-1), [w[n].shape for n, _ in sharded])):
        grads[n] = g
    rep_vec = g_rep.reshape(N_SHARDS, -1)[:, :n_rep // N_SHARDS].reshape(-1)
    for n, g in zip(replicated, _split_flat(rep_vec, [w[n].shape for n in replicated])):
        grads[n] = g

    for n in names:
        if n not in ffn_names and n not in delta:
            delta[n], new_m[n], new_v[n] = _adamw(w[n], grads[n], mom[n], var[n])
    chain = {n: None for n in ffn_names}

    def ffn_update(gid, l, j):
        for q, n in enumerate(ffn_names):
            chain[n] = _adamw_halves(oriented(w[n], n), oriented(mom[n], n), oriented(var[n], n), l, j,
                                     rs[gid]["own"][q], rs[gid]["recv"][q], core, chain[n])

    ffn_update(0, 1, 1)
    ffn_update(1, 1, 0)
    ffn_update(2, 0, 1)
    rs_end(4, ([delta[n] for n in delta], [chain[n] for n in ffn_names]))
    ffn_update(4, 0, 0)
    for n in ffn_names:
        grads[n], delta[n], new_m[n], new_v[n] = [oriented(o, n) for o in chain[n]]
    return (loss, dx.reshape(x.shape), *[grads[n] for n in names], *[delta[n] for n in names],
            *[new_m[n] for n in names], *[new_v[n] for n in names])
```

```python
import math

import jax
import jax.numpy as jnp
from jax import lax
from jax.experimental import pallas as pl
from jax.experimental.pallas import tpu as pltpu
from jax.experimental.pallas import tpu_sc as plsc

F32, BF16 = jnp.float32, jnp.bfloat16

RMS_EPS, LN_EPS, NEG_INF = 1e-6, 1e-5, -1e30
N_HEADS, QK_NOPE, QK_ROPE, V_DIM, KV_RANK = 8, 128, 64, 128, 256
CHUNK, GROUPS = 128, 16
ROPE_THETA = 10000.0
ADAM_LR, ADAM_B1, ADAM_B2, ADAM_EPS, ADAM_WD, ADAM_STEP = 0.001, 0.9, 0.999, 1e-08, 0.01, 10
N_SHARDS = 4

VMEM_LIMIT_BYTES = 48 * 1024 * 1024
ROW_TILE = 512
K_TILE = 2048
PACK_WIDTH = 1024

_DN = {"nn": (((1,), (0,)), ((), ())), "nt": (((1,), (1,)), ((), ())), "tn": (((0,), (0,)), ((), ()))}
_MESH = pl.DeviceIdType.MESH
_ANY = pl.BlockSpec(memory_space=pl.ANY)


def _params(sem):
    return pltpu.CompilerParams(dimension_semantics=sem, vmem_limit_bytes=VMEM_LIMIT_BYTES)


def _mm(name, grid, ins, pairs, acc_shapes, outs, epilogue, extras=(), inner=0, sums=()):
    n_in, n_ex, n_out = len(ins), len(extras), len(outs)
    gk = grid[2]

    def body(*refs):
        in_refs, ex_refs = refs[:n_in], refs[n_in:n_in + n_ex]
        out_refs = refs[n_in + n_ex:n_in + n_ex + n_out]
        sum_refs = refs[n_in + n_ex + n_out:n_in + n_ex + n_out + len(sums)]
        acc_refs = refs[n_in + n_ex + n_out + len(sums):]
        parts = [None] * len(acc_shapes)
        for a, b, c, dims in pairs:
            for s in range(max(inner, 1)):
                lhs, rhs = (in_refs[a][s], in_refs[b][s]) if inner else (in_refs[a][...], in_refs[b][...])
                p = lax.dot_general(lhs, rhs, _DN[dims], preferred_element_type=F32)
                parts[c] = p if parts[c] is None else parts[c] + p

        def finish(accs):
            vals = epilogue(accs, [r[...] for r in ex_refs])
            for r, v in zip(out_refs, vals):
                r[...] = v.astype(r.dtype)
            first = (pl.program_id(0) == 0) & (pl.program_id(1) == 0)
            for r, v in zip(sum_refs, vals[n_out:]):
                @pl.when(first)
                def _():
                    r[...] = v

                @pl.when(jnp.logical_not(first))
                def _():
                    r[...] += v

        if gk == 1:
            finish(parts)
        else:
            k = pl.program_id(2)

            @pl.when(k == 0)
            def _():
                for r, p in zip(acc_refs, parts):
                    r[...] = p

            @pl.when(k > 0)
            def _():
                for r, p in zip(acc_refs, parts):
                    r[...] += p

            @pl.when(k == gk - 1)
            def _():
                finish([r[...] for r in acc_refs])

    return pl.pallas_call(
        body,
        out_shape=[jax.ShapeDtypeStruct(s, d) for s, d, _, _ in outs] + [jax.ShapeDtypeStruct(s, F32) for s in sums],
        grid=grid,
        in_specs=[pl.BlockSpec(bs, im) for _, bs, im in list(ins) + list(extras)],
        out_specs=[pl.BlockSpec(bs, im) for _, _, bs, im in outs]
        + [pl.BlockSpec(s, lambda i, j, k, nd=len(s): (0,) * nd) for s in sums],
        scratch_shapes=[pltpu.VMEM(s, F32) for s in acc_shapes] if gk > 1 else [],
        name=name,
        compiler_params=_params(("arbitrary",) * 3 if sums else ("parallel", "parallel", "arbitrary")),
    )(*[a for a, _, _ in ins], *[a for a, _, _ in extras])


def _mm2d(name, pairs, outs, epilogue=None, row_extras=(), vec_extras=()):
    def mk(a, dims):
        return (a.shape[0], a.shape[1]) if dims[0] == "n" else (a.shape[1], a.shape[0])

    def nk(b, dims):
        return (b.shape[1], b.shape[0]) if dims[1] == "n" else (b.shape[0], b.shape[1])

    m = mk(pairs[0][0], pairs[0][2])[0]
    ks = [mk(a, d)[1] for a, _, d, _ in pairs]
    n_acc = 1 + max(p[3] for p in pairs)
    acc_n = [None] * n_acc
    for a, b, d, c in pairs:
        assert mk(a, d)[0] == m and nk(b, d)[1] == mk(a, d)[1]
        acc_n[c] = nk(b, d)[0]
    tm = min(m, ROW_TILE)
    if len(set(ks)) == 1 and ks[0] > 1024:
        tk = K_TILE if ks[0] % K_TILE == 0 else 512
        tks, gk = [tk] * len(pairs), ks[0] // tk
    else:
        tks, gk = ks, 1
    if len(set(acc_n)) == 1 and acc_n[0] > 1024:
        tns, gj = [1024] * n_acc, acc_n[0] // 1024
    else:
        tns, gj = acc_n, 1

    ins, plist = [], []
    for (a, b, d, c), tk in zip(pairs, tks):
        tn = tns[c]
        a_spec = ((tm, tk), lambda i, j, k: (i, k)) if d[0] == "n" else ((tk, tm), lambda i, j, k: (k, i))
        b_spec = ((tk, tn), lambda i, j, k: (k, j)) if d[1] == "n" else ((tn, tk), lambda i, j, k: (j, k))
        ins += [(a, *a_spec), (b, *b_spec)]
        plist.append((len(ins) - 2, len(ins) - 1, c, d))
    extras = [(r, (tm, r.shape[1]), lambda i, j, k: (i, 0)) for r in row_extras]
    extras += [(v, v.shape, lambda i, j, k: (0, 0)) for v in vec_extras]
    out_specs = []
    for n, dt in outs:
        bn = 1024 if (gj > 1) else n
        out_specs.append(((m, n), dt, (tm, bn), lambda i, j, k: (i, j)))
    if epilogue is None:
        epilogue = lambda accs, ex: accs
    return _mm(name, (m // tm, gj, gk), ins, plist, [(tm, tn) for tn in tns], out_specs, epilogue, extras)


def _rms(x, g):
    return x * lax.rsqrt(jnp.mean(x * x, axis=-1, keepdims=True) + RMS_EPS) * g


def _rms_bwd(x, g, dy):
    r = lax.rsqrt(jnp.mean(x * x, axis=-1, keepdims=True) + RMS_EPS)
    gy = dy * g
    dx = r * gy - x * (r * r * r) * jnp.mean(gy * x, axis=-1, keepdims=True)
    return dx, jnp.sum(dy * x * r, axis=0, keepdims=True)


def _sigmoid(x):
    return 1.0 / (1.0 + jnp.exp(-x))


_GELU_C = math.sqrt(2.0 / math.pi)


def _gelu(x):
    return x * (0.5 * (1.0 + jnp.tanh(_GELU_C * (x + 0.044715 * (x * x * x)))))


def _gelu_grad(x):
    t = jnp.tanh(_GELU_C * (x + 0.044715 * (x * x * x)))
    return 0.5 * (1.0 + t) + 0.5 * x * (1.0 - t * t) * (_GELU_C * (1.0 + 3.0 * 0.044715 * (x * x)))


def _rows(name, row_ins, vec_ins, fn, row_outs, acc_outs=()):
    t = row_ins[0].shape[0]
    tm = min(t, ROW_TILE)
    nr, nv, no = len(row_ins), len(vec_ins), len(row_outs)

    def body(*refs):
        outs, incs = fn([r[...] for r in refs[:nr]], [r[...] for r in refs[nr:nr + nv]])
        for r, v in zip(refs[nr + nv:nr + nv + no], outs):
            r[...] = v.astype(r.dtype)
        i = pl.program_id(0)
        for r, v in zip(refs[nr + nv + no:], incs):
            @pl.when(i == 0)
            def _():
                r[...] = v

            @pl.when(i > 0)
            def _():
                r[...] += v

    in_specs = [pl.BlockSpec((tm, a.shape[1]), lambda i: (i, 0)) for a in row_ins]
    in_specs += [pl.BlockSpec(v.shape, lambda i, nd=v.ndim: (0,) * nd) for v in vec_ins]
    out_shape = [jax.ShapeDtypeStruct((t, c), dt) for c, dt in row_outs]
    out_shape += [jax.ShapeDtypeStruct(s, F32) for s in acc_outs]
    out_specs = [pl.BlockSpec((tm, c), lambda i: (i, 0)) for c, _ in row_outs]
    out_specs += [pl.BlockSpec(s, lambda i, nd=len(s): (0,) * nd) for s in acc_outs]
    return pl.pallas_call(body, out_shape=out_shape, grid=(t // tm,), in_specs=in_specs, out_specs=out_specs,
                          name=name, compiler_params=_params(("arbitrary",)))(*row_ins, *vec_ins)


def _rms_fwd(x, g):
    return _rows("rms_fwd", [x], [g], lambda r, v: ([_rms(r[0], v[0])], []), [(x.shape[1], BF16)])[0]


def _norm_out_bwd(name, f, d_out, g, scale):
    def fn(r, v):
        dx, dg = _rms_bwd(r[0], v[0], r[1] * scale)
        return [dx], [dg]

    c = f.shape[1]
    return _rows(name, [f, d_out], [g], fn, [(c, BF16)], [(1, c)])


def _norm_in_bwd(name, h, d_res, branches):
    nb = len(branches)

    def fn(r, v):
        dh, dgs = r[1], []
        for b in range(nb):
            dx, dg = _rms_bwd(r[0], v[b], r[2 + b])
            dh = dh + dx
            dgs.append(dg)
        return [dh], dgs

    c = h.shape[1]
    return _rows(name, [h, d_res] + [dn for _, dn in branches], [g for g, _ in branches], fn, [(c, F32)],
                 [(1, c)] * nb)


def _loss_head(y, target):
    d = y.shape[1]

    def fn(r, v):
        e = r[0] - r[1]
        s = jnp.sum(jnp.sum(e * e, axis=1, keepdims=True), axis=0, keepdims=True) * (0.5 / d)
        return [e * (1.0 / d)], [jnp.broadcast_to(s, (1, 128))]

    dy, acc = _rows("loss_head", [y, target], [], fn, [(d, F32)], [(1, 128)])
    return acc[0, 0], dy


def _rope_bwd(dk, cos2, sin2):
    c = dk.shape[1]
    return _rows("rope_bwd", [dk, cos2, sin2], [], lambda r, v: ([r[0] * r[1], r[0] * r[2]], []),
                 [(c, BF16), (c, BF16)])


def _ffn_up(n, wg, wu):
    t, d = n.shape
    fs = wg.shape[-2]
    tm = min(t, ROW_TILE // 2)

    def body(n_ref, wg_ref, wu_ref, g_ref, u_ref, a_ref):
        x = n_ref[...]
        for s in range(N_SHARDS):
            g = lax.dot_general(x, wg_ref[s], _DN["nt"], preferred_element_type=F32)
            u = lax.dot_general(x, wu_ref[s], _DN["nt"], preferred_element_type=F32)
            g_ref[s] = g.astype(BF16)
            u_ref[s] = u.astype(BF16)
            a_ref[s] = (g * _sigmoid(g) * u).astype(BF16)

    hid = pl.BlockSpec((N_SHARDS, tm, fs), lambda i: (0, i, 0))
    whole = pl.BlockSpec(wg.shape, lambda i: (0, 0, 0))
    return pl.pallas_call(body, out_shape=[jax.ShapeDtypeStruct((N_SHARDS, t, fs), BF16)] * 3, grid=(t // tm,),
                          in_specs=[pl.BlockSpec((tm, d), lambda i: (i, 0)), whole, whole], out_specs=[hid] * 3,
                          name="ffn_up", compiler_params=_params(("parallel",)))(n, wg, wu)


def _down(name, a_in, w_in, gk, h, post_g, next_gs, scale, inner=0):
    t, d = h.shape
    tm = min(t, ROW_TILE)
    kn = next_gs.shape[0]

    def epi(accs, ex):
        f, hv, pg, ng = accs[0], ex[0], ex[1], ex[2]
        hn = hv + scale * _rms(f, pg)
        return [f, hn] + [_rms(hn, ng[q:q + 1]) for q in range(kn)]

    row = ((tm, d), lambda i, j, k: (i, 0))
    outs = [((t, d), F32, *row), ((t, d), F32, *row)] + [((t, d), BF16, *row)] * kn
    extras = [(h, *row), (post_g, (1, d), lambda i, j, k: (0, 0)), (next_gs, (kn, d), lambda i, j, k: (0, 0))]
    return _mm(name, (t // tm, 1, gk), [a_in, w_in], [(0, 1, 0, "nn")], [(tm, d)], outs, epi, extras, inner)


def _ffn_down(a, wd, h, post_g, next_gs):
    t, d = h.shape
    fs = a.shape[-1]
    tm = min(t, ROW_TILE)
    return _down("ffn_down", (a, (N_SHARDS, tm, fs), lambda i, _, k: (0, i, 0)),
                 (wd, (N_SHARDS, fs, d), lambda i, _, k: (0, 0, 0)), 1, h, post_g, next_gs, 0.5, N_SHARDS)


def _ffn_dact(f, d_out, post_g, wd, g, u):
    t, d = f.shape
    fs = g.shape[-1]
    tm = min(t, ROW_TILE // 2)

    def body(f_ref, do_ref, pg_ref, wd_ref, g_ref, u_ref, df_ref, dg_ref, du_ref, dpg_ref):
        dfv, dpg = _rms_bwd(f_ref[...], pg_ref[...], do_ref[...] * 0.5)
        dfb = dfv.astype(BF16)
        df_ref[...] = dfb
        i = pl.program_id(0)

        @pl.when(i == 0)
        def _():
            dpg_ref[...] = dpg

        @pl.when(i > 0)
        def _():
            dpg_ref[...] += dpg

        for s in range(N_SHARDS):
            da = lax.dot_general(dfb, wd_ref[s], _DN["nt"], preferred_element_type=F32)
            gv, uv = g_ref[s].astype(F32), u_ref[s].astype(F32)
            sg = _sigmoid(gv)
            dg_ref[s] = (da * uv * (sg * (1.0 + gv * (1.0 - sg)))).astype(BF16)
            du_ref[s] = (da * (gv * sg)).astype(BF16)

    row = pl.BlockSpec((tm, d), lambda i: (i, 0))
    hid = pl.BlockSpec((N_SHARDS, tm, fs), lambda i: (0, i, 0))
    vec = pl.BlockSpec((1, d), lambda i: (0, 0))
    hid_shape = jax.ShapeDtypeStruct((N_SHARDS, t, fs), BF16)
    return pl.pallas_call(
        body, out_shape=[jax.ShapeDtypeStruct((t, d), BF16), hid_shape, hid_shape, jax.ShapeDtypeStruct((1, d), F32)],
        grid=(t // tm,), in_specs=[row, row, vec, pl.BlockSpec(wd.shape, lambda i: (0, 0, 0)), hid, hid],
        out_specs=[row, hid, hid, vec], name="ffn_dact", compiler_params=_params(("arbitrary",)))(
            f, d_out, post_g, wd, g, u)


def _ffn_dn(dg, du, wg, wu, h, d_res, branches):
    _, t, fs = dg.shape
    d = wg.shape[-1]
    tm = min(t, ROW_TILE)
    nb = len(branches)
    a_spec = ((N_SHARDS, tm, fs), lambda i, _, k: (0, i, 0))
    w_spec = ((N_SHARDS, fs, d), lambda i, _, k: (0, 0, 0))
    row = ((tm, d), lambda i, _, k: (i, 0))
    vec = ((1, d), lambda i, _, k: (0, 0))

    def epi(accs, ex):
        hv, dh = ex[0], ex[1]
        dns, gs = [accs[0]] + ex[2:1 + nb], ex[1 + nb:]
        dgs = []
        for dn, g in zip(dns, gs):
            dx, dgv = _rms_bwd(hv, g, dn)
            dh = dh + dx
            dgs.append(dgv)
        return [dh] + dgs

    extras = [(h, *row), (d_res, *row)] + [(dn, *row) for _, dn in branches[1:]] + [(g, *vec) for g, _ in branches]
    return _mm("ffn_dn", (t // tm, 1, 1), [(dg, *a_spec), (wg, *w_spec), (du, *a_spec), (wu, *w_spec)],
               [(0, 1, 0, "nn"), (2, 3, 0, "nn")], [(tm, d)], [((t, d), F32, *row)], epi, extras, inner=N_SHARDS,
               sums=[(1, d)] * nb)


def _ffn_dw_in(n, dg, du):
    _, t, fs = dg.shape
    d = n.shape[1]
    tk = min(t, K_TILE)
    a_spec = ((None, tk, fs), lambda s, _, k: (s, k, 0))
    o_spec = ((None, fs, d), lambda s, _, k: (s, 0, 0))
    outs = [((N_SHARDS, fs, d), BF16, *o_spec)] * 2
    return _mm("ffn_dw_in", (N_SHARDS, 1, t // tk), [(dg, *a_spec), (du, *a_spec), (n, (tk, d), lambda s, _, k: (k, 0))],
               [(0, 2, 0, "tn"), (1, 2, 1, "tn")], [(fs, d)] * 2, outs, lambda accs, ex: accs)


def _ffn_dw_down(a, df):
    _, t, fs = a.shape
    d = df.shape[1]
    tk = min(t, K_TILE)
    outs = [((N_SHARDS, fs, d), BF16, (None, fs, d), lambda s, _, k: (s, 0, 0))]
    return _mm("ffn_dw_down", (N_SHARDS, 1, t // tk),
               [(a, (None, tk, fs), lambda s, _, k: (s, k, 0)), (df, (tk, d), lambda s, _, k: (k, 0))],
               [(0, 1, 0, "tn")], [(fs, d)], outs, lambda accs, ex: accs)[0]


def _causal_weight(w):
    row = lax.broadcasted_iota(jnp.int32, (CHUNK, CHUNK), 0)
    col = lax.broadcasted_iota(jnp.int32, (CHUNK, CHUNK), 1)
    return row >= col, jnp.where(row >= col, w, 0.0).astype(BF16)


def _layer_norm(v, g, b):
    xc = v - jnp.mean(v, axis=-1, keepdims=True)
    rstd = lax.rsqrt(jnp.mean(xc * xc, axis=-1, keepdims=True) + LN_EPS)
    xhat = xc * rstd
    return xhat, rstd, xhat * g + b


def _sgu_specs(t, half, tm):
    return [pl.BlockSpec((tm, half), lambda i: (i, 0)), pl.BlockSpec((tm, half), lambda i: (i, 1))]


def _sgu_fwd(zp, ln_g, ln_b, w_s, bsb):
    t, half = zp.shape[0], zp.shape[1] // 2
    tm = min(t, 2 * CHUNK)

    def body(u_ref, v_ref, g_ref, b_ref, w_ref, bs_ref, o_ref):
        u = _gelu(u_ref[...])
        _, _, vn = _layer_norm(_gelu(v_ref[...]), g_ref[...], b_ref[...])
        vb = vn.astype(BF16)
        for g in range(GROUPS):
            _, wm = _causal_weight(w_ref[g])
            cols = slice(g * CHUNK, (g + 1) * CHUNK)
            for c in range(tm // CHUNK):
                rows = slice(c * CHUNK, (c + 1) * CHUNK)
                sv = jnp.dot(wm, vb[rows, cols], preferred_element_type=F32) + bs_ref[g]
                o_ref[rows, cols] = (u[rows, cols] * sv).astype(BF16)

    whole = lambda a: pl.BlockSpec(a.shape, lambda i, nd=a.ndim: (0,) * nd)
    return pl.pallas_call(
        body, out_shape=jax.ShapeDtypeStruct((t, half), BF16), grid=(t // tm,),
        in_specs=_sgu_specs(t, half, tm) + [whole(ln_g), whole(ln_b), whole(w_s), whole(bsb)],
        out_specs=pl.BlockSpec((tm, half), lambda i: (i, 0)), name="sgu_fwd",
        compiler_params=_params(("arbitrary",)))(zp, zp, ln_g, ln_b, w_s, bsb)


def _sgu_bwd(zp, d_uv, ln_g, ln_b, w_s, bsb):
    t, half = zp.shape[0], zp.shape[1] // 2
    tm = min(t, 2 * CHUNK)

    def body(u_ref, v_ref, d_ref, g_ref, b_ref, w_ref, bs_ref, dz_ref, dlg_ref, dlb_ref, dws_ref, dbs_ref, dvn_ref):
        i = pl.program_id(0)

        @pl.when(i == 0)
        def _():
            dlg_ref[...] = jnp.zeros_like(dlg_ref)
            dlb_ref[...] = jnp.zeros_like(dlb_ref)
            dws_ref[...] = jnp.zeros_like(dws_ref)
            dbs_ref[...] = jnp.zeros_like(dbs_ref)

        up, vp = u_ref[...], v_ref[...]
        u, gup = _gelu(up), _gelu_grad(up)
        xhat, rstd, vn = _layer_norm(_gelu(vp), g_ref[...], b_ref[...])
        vb = vn.astype(BF16)
        d = d_ref[...]
        for g in range(GROUPS):
            mask, wm = _causal_weight(w_ref[g])
            cols = slice(g * CHUNK, (g + 1) * CHUNK)
            for c in range(tm // CHUNK):
                rows = slice(c * CHUNK, (c + 1) * CHUNK)
                blk = vb[rows, cols]
                sv = jnp.dot(wm, blk, preferred_element_type=F32) + bs_ref[g]
                dblk = d[rows, cols]
                dz_ref[rows, cols] = (dblk * sv * gup[rows, cols]).astype(BF16)
                dsv = dblk * u[rows, cols]
                dsvb = dsv.astype(BF16)
                dvn_ref[rows, cols] = lax.dot_general(wm, dsvb, _DN["tn"], preferred_element_type=F32)
                dw = lax.dot_general(dsvb, blk, _DN["nt"], preferred_element_type=F32)
                dws_ref[g] += jnp.where(mask, dw, 0.0)
                dbs_ref[g] += jnp.sum(dsv, axis=1, keepdims=True)
        dvn = dvn_ref[...]
        dlg_ref[...] += jnp.sum(dvn * xhat, axis=0, keepdims=True)
        dlb_ref[...] += jnp.sum(dvn, axis=0, keepdims=True)
        dxh = dvn * g_ref[...]
        dv = rstd * (dxh - jnp.mean(dxh, axis=-1, keepdims=True)
                     - xhat * jnp.mean(dxh * xhat, axis=-1, keepdims=True))
        dz_ref[:, half:] = (dv * _gelu_grad(vp)).astype(BF16)

    whole = lambda a: pl.BlockSpec(a.shape, lambda i, nd=a.ndim: (0,) * nd)
    wshape = lambda s: pl.BlockSpec(s, lambda i, nd=len(s): (0,) * nd)
    out_shape = [jax.ShapeDtypeStruct((t, 2 * half), BF16), jax.ShapeDtypeStruct((1, half), F32),
                 jax.ShapeDtypeStruct((1, half), F32), jax.ShapeDtypeStruct(w_s.shape, F32),
                 jax.ShapeDtypeStruct((GROUPS, CHUNK, 1), F32)]
    return pl.pallas_call(
        body, out_shape=out_shape, grid=(t // tm,),
        in_specs=_sgu_specs(t, half, tm) + [pl.BlockSpec((tm, half), lambda i: (i, 0)), whole(ln_g), whole(ln_b),
                                            whole(w_s), whole(bsb)],
        out_specs=[pl.BlockSpec((tm, 2 * half), lambda i: (i, 0)), wshape((1, half)), wshape((1, half)),
                   wshape(w_s.shape), wshape((GROUPS, CHUNK, 1))],
        scratch_shapes=[pltpu.VMEM((tm, half), F32)], name="sgu_bwd",
        compiler_params=_params(("arbitrary",)))(zp, zp, d_uv, ln_g, ln_b, w_s, bsb)


_SCALE = (QK_NOPE + QK_ROPE) ** -0.5


def _attn_scores(qn, qr, kn, kr, i, tq, n):
    s = lax.dot_general(qn, kn, _DN["nt"], preferred_element_type=F32)
    s = (s + lax.dot_general(qr, kr, _DN["nt"], preferred_element_type=F32)) * _SCALE
    row = i * tq + lax.broadcasted_iota(jnp.int32, (tq, n), 0)
    col = lax.broadcasted_iota(jnp.int32, (tq, n), 1)
    return jnp.where(col <= row, s, NEG_INF)


def _attn_specs(seq):
    head = lambda b, h: (b, h)
    return dict(
        qn=pl.BlockSpec((seq, QK_NOPE), head),
        qr=pl.BlockSpec((None, seq, QK_ROPE), lambda b, h: (h, b, 0)),
        kr=pl.BlockSpec((seq, QK_ROPE), lambda b, h: (b, 0)),
        lse=pl.BlockSpec((None, seq, 1), lambda b, h: (h, b, 0)),
    )


def _attn_fwd(qn, qr, kn, v, kr, seq):
    t = qn.shape[0]
    tq = min(seq, 2 * CHUNK)
    sp = _attn_specs(seq)

    def body(qn_ref, qr_ref, kn_ref, v_ref, kr_ref, o_ref, lse_ref):
        for i in range(seq // tq):
            rows, n = slice(i * tq, (i + 1) * tq), (i + 1) * tq
            s = _attn_scores(qn_ref[rows, :], qr_ref[rows, :], kn_ref[0:n, :], kr_ref[0:n, :], i, tq, n)
            m = jnp.max(s, axis=-1, keepdims=True)
            p = jnp.exp(s - m)
            l = jnp.sum(p, axis=-1, keepdims=True)
            o_ref[rows, :] = jnp.dot((p / l).astype(BF16), v_ref[0:n, :], preferred_element_type=F32).astype(BF16)
            lse_ref[rows, :] = m + jnp.log(l)

    return pl.pallas_call(
        body, out_shape=[jax.ShapeDtypeStruct((t, N_HEADS * V_DIM), BF16), jax.ShapeDtypeStruct((N_HEADS, t, 1), F32)],
        grid=(t // seq, N_HEADS), in_specs=[sp["qn"], sp["qr"], sp["qn"], sp["qn"], sp["kr"]],
        out_specs=[sp["qn"], sp["lse"]], name="attn_fwd",
        compiler_params=_params(("parallel", "arbitrary")))(qn, qr, kn, v, kr)


def _attn_bwd(qn, qr, kn, v, kr, do, lse, cos2, sin2, seq):
    t = qn.shape[0]
    tq = min(seq, 2 * CHUNK)
    sp = _attn_specs(seq)

    def body(qn_ref, qr_ref, kn_ref, v_ref, kr_ref, do_ref, lse_ref, cos_ref, sin_ref,
             dqn_ref, dkn_ref, dv_ref, dqc_ref, dqs_ref, dkr_ref, dk_acc, dv_acc, dkr_acc):
        dk_acc[...] = jnp.zeros_like(dk_acc)
        dv_acc[...] = jnp.zeros_like(dv_acc)
        dkr_acc[...] = jnp.zeros_like(dkr_acc)
        for i in range(seq // tq):
            rows, n = slice(i * tq, (i + 1) * tq), (i + 1) * tq
            q_n, q_r, d_o = qn_ref[rows, :], qr_ref[rows, :], do_ref[rows, :]
            k_n, k_r = kn_ref[0:n, :], kr_ref[0:n, :]
            s = _attn_scores(q_n, q_r, k_n, k_r, i, tq, n)
            p = jnp.exp(s - lse_ref[rows, :])
            dp = lax.dot_general(d_o, v_ref[0:n, :], _DN["nt"], preferred_element_type=F32)
            ds = (p * (dp - jnp.sum(p * dp, axis=-1, keepdims=True)) * _SCALE).astype(BF16)
            dqn_ref[rows, :] = jnp.dot(ds, k_n, preferred_element_type=F32).astype(BF16)
            dqr = jnp.dot(ds, k_r, preferred_element_type=F32)
            dqc_ref[rows, :] = (dqr * cos_ref[rows, :]).astype(BF16)
            dqs_ref[rows, :] = (dqr * sin_ref[rows, :]).astype(BF16)
            dk_acc[0:n, :] += lax.dot_general(ds, q_n, _DN["tn"], preferred_element_type=F32)
            dkr_acc[0:n, :] += lax.dot_general(ds, q_r, _DN["tn"], preferred_element_type=F32)
            dv_acc[0:n, :] += lax.dot_general(p.astype(BF16), d_o, _DN["tn"], preferred_element_type=F32)
        dkn_ref[...] = dk_acc[...].astype(BF16)
        dv_ref[...] = dv_acc[...].astype(BF16)
        h = pl.program_id(1)

        @pl.when(h == 0)
        def _():
            dkr_ref[...] = dkr_acc[...]

        @pl.when(h > 0)
        def _():
            dkr_ref[...] += dkr_acc[...]

    wide = jax.ShapeDtypeStruct((t, N_HEADS * V_DIM), BF16)
    rope = jax.ShapeDtypeStruct((N_HEADS, t, QK_ROPE), BF16)
    krf = pl.BlockSpec((seq, QK_ROPE), lambda b, h: (b, 0))
    return pl.pallas_call(
        body, out_shape=[wide, wide, wide, rope, rope, jax.ShapeDtypeStruct((t, QK_ROPE), F32)],
        grid=(t // seq, N_HEADS),
        in_specs=[sp["qn"], sp["qr"], sp["qn"], sp["qn"], sp["kr"], sp["qn"], sp["lse"], krf, krf],
        out_specs=[sp["qn"], sp["qn"], sp["qn"], sp["qr"], sp["qr"], krf],
        scratch_shapes=[pltpu.VMEM((seq, QK_NOPE), F32), pltpu.VMEM((seq, V_DIM), F32), pltpu.VMEM((seq, QK_ROPE), F32)],
        name="attn_bwd", compiler_params=_params(("parallel", "arbitrary")))(qn, qr, kn, v, kr, do, lse, cos2, sin2)


def _q_rope(qn, w, w_rot, cos2, sin2):
    t, r = qn.shape
    nh, _, e = w.shape
    tm = min(t, ROW_TILE)

    def body(x_ref, w_ref, wr_ref, c_ref, s_ref, o_ref):
        x = x_ref[...]
        for h in range(nh):
            raw = jnp.dot(x, w_ref[h], preferred_element_type=F32)
            rot = jnp.dot(x, wr_ref[h], preferred_element_type=F32)
            o_ref[h] = (raw * c_ref[...] + rot * s_ref[...]).astype(BF16)

    whole = pl.BlockSpec(w.shape, lambda i: (0, 0, 0))
    rows = pl.BlockSpec((tm, e), lambda i: (i, 0))
    return pl.pallas_call(body, out_shape=jax.ShapeDtypeStruct((nh, t, e), BF16), grid=(t // tm,),
                          in_specs=[pl.BlockSpec((tm, r), lambda i: (i, 0)), whole, whole, rows, rows],
                          out_specs=pl.BlockSpec((nh, tm, e), lambda i: (0, i, 0)), name="q_rope",
                          compiler_params=_params(("parallel",)))(qn, w, w_rot, cos2, sin2)


def _q_rope_dw(qn, dq_c, dq_s):
    t, r = qn.shape
    nh, _, e = dq_c.shape
    tk = min(t, K_TILE)

    def body(x_ref, c_ref, s_ref, gc_ref, gs_ref):
        k = pl.program_id(0)
        x = x_ref[...]
        for h in range(nh):
            pc = lax.dot_general(c_ref[h], x, _DN["tn"], preferred_element_type=F32)
            ps = lax.dot_general(s_ref[h], x, _DN["tn"], preferred_element_type=F32)

            @pl.when(k == 0)
            def _():
                gc_ref[h] = pc
                gs_ref[h] = ps

            @pl.when(k > 0)
            def _():
                gc_ref[h] += pc
                gs_ref[h] += ps

    heads = pl.BlockSpec((nh, tk, e), lambda k: (0, k, 0))
    out = pl.BlockSpec((nh, e, r), lambda k: (0, 0, 0))
    return pl.pallas_call(body, out_shape=[jax.ShapeDtypeStruct((nh, e, r), F32)] * 2, grid=(t // tk,),
                          in_specs=[pl.BlockSpec((tk, r), lambda k: (k, 0)), heads, heads], out_specs=[out, out],
                          name="q_rope_dw", compiler_params=_params(("arbitrary",)))(qn, dq_c, dq_s)


def _row_tile(rows, cols, row_mult=8):
    cap = max(row_mult, (1 << 18) // cols)
    best = rows
    for tr in range(row_mult, min(rows, cap) + 1, row_mult):
        if rows % tr == 0:
            best = tr
    return best if rows > cap else rows


def _adamw_math(w, g, m, v):
    mv = ADAM_B1 * m + (1.0 - ADAM_B1) * g
    vv = ADAM_B2 * v + (1.0 - ADAM_B2) * (g * g)
    m_hat = mv / (1.0 - ADAM_B1 ** ADAM_STEP)
    v_hat = vv / (1.0 - ADAM_B2 ** ADAM_STEP)
    return -ADAM_LR * (m_hat / (jnp.sqrt(v_hat) + ADAM_EPS) + ADAM_WD * w), mv, vv


def _adamw(w, g, m, v):
    shape = w.shape
    c = shape[-1]
    r = w.size // c
    tr = _row_tile(r, c)

    def body(w_ref, g_ref, m_ref, v_ref, d_ref, nm_ref, nv_ref):
        d_ref[...], nm_ref[...], nv_ref[...] = _adamw_math(w_ref[...], g_ref[...], m_ref[...], v_ref[...])

    spec = pl.BlockSpec((tr, c), lambda i: (i, 0))
    outs = pl.pallas_call(body, out_shape=[jax.ShapeDtypeStruct((r, c), F32)] * 3, grid=(r // tr,),
                          in_specs=[spec] * 4, out_specs=[spec] * 3, name="adamw",
                          compiler_params=_params(("parallel",)))(*[a.reshape(r, c) for a in (w, g, m, v)])
    return [o.reshape(shape) for o in outs]


def _adamw_halves(w, m, v, l, j, own, recv, core, prev):
    nl, nj, rows, c = w.shape
    r = rows // 2
    tr = _row_tile(r, c)
    n_prev = 0 if prev is None else 4

    def body(core_ref, w_ref, own_ref, recv_ref, m_ref, v_ref, *rest):
        g_ref, d_ref, nm_ref, nv_ref = rest[n_prev:]
        g = jnp.where(pl.program_id(0) == core_ref[0], own_ref[...], recv_ref[...])
        g_ref[...] = g
        d_ref[...], nm_ref[...], nv_ref[...] = _adamw_math(w_ref[...], g, m_ref[...], v_ref[...])

    nb = r // tr
    slab = pl.BlockSpec((None, None, tr, c), lambda h, i, cr: (l, j, h * nb + i, 0))
    half = pl.BlockSpec((tr, c), lambda h, i, cr: (i, 0))
    grid_spec = pltpu.PrefetchScalarGridSpec(num_scalar_prefetch=1, grid=(2, nb),
                                             in_specs=[slab, half, half, slab, slab] + [_ANY] * n_prev,
                                             out_specs=[slab] * 4)
    return pl.pallas_call(body, out_shape=[jax.ShapeDtypeStruct(w.shape, F32)] * 4, grid_spec=grid_spec,
                          input_output_aliases={6 + q: q for q in range(n_prev)}, name="adamw_halves",
                          compiler_params=_params(("parallel",) * 2))(core, w, own, recv, m, v, *(prev or ()))


def _place():
    x, y, c = lax.axis_index("x"), lax.axis_index("y"), lax.axis_index("c")
    return x, y, c, [(1 - x, y), (x, 1 - y), (1 - x, 1 - y)]


def _dma_sems(*counts):
    return [pltpu.SemaphoreType.DMA((n,)) for n in counts]


def _all_gather(bufs, collective_id, name):
    n = len(bufs)

    def body(*refs):
        ins, outs = refs[:n], refs[n:2 * n]
        send, recv, fsend, frecv, osend, orecv = refs[2 * n:]
        x, y, c, _ = _place()
        xn, yn, sib = (1 - x, y, c), (x, 1 - y, c), (x, y, 1 - c)
        k, kx, ky, kd = 2 * x + y, 2 * (1 - x) + y, 2 * x + 1 - y, 2 * (1 - x) + 1 - y
        _handshake([xn, yn, sib])

        def copy(src, dst, sems, i, to):
            return pltpu.make_async_remote_copy(src, dst, sems[0].at[i], sems[1].at[i], device_id=to, device_id_type=_MESH)

        ici, d2d, own_s = (send, recv), (fsend, frecv), (osend, orecv)
        started = [copy(ins[b], outs[b].at[k], own_s, b, sib) for b in range(n)]
        for first in (True, False):
            for b in range(n):
                mine = outs[b].at[k, c]
                if first:
                    started += [copy(ins[b].at[c, 0], mine.at[0], ici, 6 * b, xn), copy(ins[b].at[c, 1], mine.at[1], ici, 6 * b + 1, yn)]
                else:
                    started += [copy(ins[b].at[c, 1], mine.at[1], ici, 6 * b + 2, xn), copy(ins[b].at[c, 0], mine.at[0], ici, 6 * b + 3, yn)]
        for cp in started:
            cp.start()
        passed = []
        for b in range(n):
            for i, (src_chip, q, to) in enumerate([(kx, 0, yn), (ky, 1, xn)]):
                piece = outs[b].at[src_chip, c, q]
                copy(piece, piece, ici, 6 * b + i, to).wait_recv()
                cp = copy(piece, piece, ici, 6 * b + 4 + i, to)
                cp.start()
                passed.append(cp)
        for b in range(n):
            for i, (src_chip, q) in enumerate([(kx, 1), (ky, 0)]):
                piece = outs[b].at[src_chip, c, q]
                copy(piece, piece, ici, 6 * b + 2 + i, xn).wait_recv()
                half = outs[b].at[src_chip, c]
                cp = copy(half, half, d2d, 3 * b + i, sib)
                cp.start()
                passed.append(cp)
        for b in range(n):
            for i, q in enumerate([0, 1]):
                piece = outs[b].at[kd, c, q]
                copy(piece, piece, ici, 6 * b + 4 + i, xn).wait_recv()
            half = outs[b].at[kd, c]
            cp = copy(half, half, d2d, 3 * b + 2, sib)
            cp.start()
            passed.append(cp)
        for b in range(n):
            for i, src_chip in enumerate([kx, ky, kd]):
                half = outs[b].at[src_chip, 1 - c]
                copy(half, half, d2d, 3 * b + i, sib).wait_recv()
        for cp in started[n:] + passed:
            cp.wait_send()
        for cp in started[:n]:
            cp.wait()

    return _sequencer(body, [jax.ShapeDtypeStruct((N_SHARDS,) + b.shape, b.dtype) for b in bufs],
                      _dma_sems(6 * n, 6 * n, 3 * n, 3 * n, n, n), collective_id, name, bufs)


def _sequencer(body, out_type, sems, collective_id, name, args):
    return pl.kernel(body, out_type=out_type, mesh=plsc.ScalarSubcoreMesh(axis_name="sequencer", num_cores=1),
                     scratch_types=sems, compiler_params=pltpu.CompilerParams(collective_id=collective_id),
                     name=name)(*args)


def _handshake(peers):
    barrier = pltpu.get_barrier_semaphore()
    for peer in peers:
        pl.semaphore_signal(barrier, inc=1, device_id=peer, device_id_type=_MESH)
    pl.semaphore_wait(barrier, len(peers))


def _swap_halves(parts, collective_id, name):
    n = len(parts)

    def body(*refs):
        ins, outs = refs[:n], refs[n:2 * n]
        send, recv = refs[2 * n:]
        x, y, c, _ = _place()
        _handshake([(x, y, 1 - c)])
        cps = [pltpu.make_async_remote_copy(ins[b].at[:, pl.ds(1 - c, 1)], outs[b], send.at[b], recv.at[b],
                                            device_id=(x, y, 1 - c), device_id_type=_MESH) for b in range(n)]
        for cp in cps:
            cp.start()
        for cp in cps:
            cp.wait()

    return _sequencer(body, [jax.ShapeDtypeStruct((N_SHARDS, 1) + p.shape[2:], p.dtype) for p in parts],
                      _dma_sems(n, n), collective_id, name, parts)


def _by_shape(fn, first, second, scalar):
    out, groups = [None] * len(first), {}
    for i, p in enumerate(first):
        groups.setdefault(p.shape, []).append(i)
    for idx in groups.values():
        for i, r in zip(idx, fn([first[i] for i in idx], [second[i] for i in idx], scalar)):
            out[i] = r
    return out


def _add_half(parts, others, core):
    n = len(parts)
    _, _, r, c = parts[0].shape
    tr = _row_tile(r, c, 16)

    def body(core_ref, *refs):
        for q in range(n):
            refs[2 * n + q][...] = (refs[q][...].astype(F32) + refs[n + q][...].astype(F32)).astype(BF16)

    grid_spec = pltpu.PrefetchScalarGridSpec(
        num_scalar_prefetch=1, grid=(N_SHARDS, r // tr),
        in_specs=[pl.BlockSpec((None, None, tr, c), lambda k, i, cr: (k, cr[0], i, 0))] * n
        + [pl.BlockSpec((None, None, tr, c), lambda k, i, cr: (k, 0, i, 0))] * n,
        out_specs=[pl.BlockSpec((None, tr, c), lambda k, i, cr: (k, i, 0))] * n)
    return pl.pallas_call(body, out_shape=[jax.ShapeDtypeStruct((N_SHARDS, r, c), BF16)] * n, grid_spec=grid_spec,
                          name="grad_add_half", compiler_params=_params(("parallel", "parallel")))(core, *parts, *others)


def _scatter_chips(parts, collective_id, name):
    n = len(parts)

    def body(*refs):
        ins, outs = refs[:n], refs[n:2 * n]
        send, recv = refs[2 * n:]
        x, y, c, chips = _place()
        k = 2 * x + y
        _handshake([(px, py, c) for px, py in chips])
        started = []
        for b in range(n):
            for j, (px, py) in enumerate(chips):
                cp = pltpu.make_async_remote_copy(ins[b].at[2 * px + py], outs[b].at[k], send.at[3 * b + j],
                                                  recv.at[3 * b + j], device_id=(px, py, c), device_id_type=_MESH)
                cp.start()
                started.append(cp)
        for b in range(n):
            for j, (px, py) in enumerate(chips):
                got = outs[b].at[2 * px + py]
                pltpu.make_async_remote_copy(got, got, send.at[3 * b + j], recv.at[3 * b + j],
                                             device_id=(px, py, c), device_id_type=_MESH).wait_recv()
        for cp in started:
            cp.wait_send()

    return _sequencer(body, [jax.ShapeDtypeStruct(p.shape, p.dtype) for p in parts], _dma_sems(3 * n, 3 * n),
                      collective_id, name, parts)


def _sum_slots(slots, mine, chip):
    n = len(slots)
    _, r, c = slots[0].shape
    tr = _row_tile(r, c, 16)

    def body(chip_ref, *refs):
        for q in range(n):
            own = refs[5 * q + 4][...].astype(F32)
            v = [jnp.where(chip_ref[0] == s, own, refs[5 * q + s][...].astype(F32)) for s in range(N_SHARDS)]
            refs[5 * n + q][...] = ((v[0] + v[1]) + v[2]) + v[3]

    def slot_spec(s):
        return pl.BlockSpec((None, tr, c), lambda i, kr: (jnp.where(kr[0] == s, (s + 1) % N_SHARDS, s), i, 0))

    per_buffer = [slot_spec(s) for s in range(N_SHARDS)] + [pl.BlockSpec((None, tr, c), lambda i, kr: (kr[0], i, 0))]
    grid_spec = pltpu.PrefetchScalarGridSpec(num_scalar_prefetch=1, grid=(r // tr,), in_specs=per_buffer * n,
                                             out_specs=[pl.BlockSpec((tr, c), lambda i, kr: (i, 0))] * n)
    args = [a for sl, mn in zip(slots, mine) for a in (sl, sl, sl, sl, mn)]
    return pl.pallas_call(body, out_shape=[jax.ShapeDtypeStruct((r, c), F32)] * n, grid_spec=grid_spec,
                          name="grad_sum_slots", compiler_params=_params(("parallel",)))(chip, *args)


def _join_halves(halves, collective_id, name):
    n = len(halves)

    def body(*refs):
        ins, outs = refs[:n], refs[n:2 * n]
        send, recv = refs[2 * n:]
        x, y, c, _ = _place()
        _handshake([(x, y, 1 - c)])
        cps = [pltpu.make_async_remote_copy(ins[b], outs[b], send.at[b], recv.at[b], device_id=(x, y, 1 - c),
                                            device_id_type=_MESH) for b in range(n)]
        for cp in cps:
            cp.start()
        for cp in cps:
            cp.wait()

    return _sequencer(body, [jax.ShapeDtypeStruct(h.shape, F32) for h in halves], _dma_sems(n, n), collective_id,
                      name, halves)


def _gather_rows(buf, start, rows):
    def body(in_ref, out_ref, send, recv, lsem):
        x, y, c, chips = _place()
        k = 2 * x + y
        src = in_ref.at[pl.ds(start, rows)]
        local = pltpu.make_async_remote_copy(src, out_ref.at[k], lsem.at[0], lsem.at[1], device_id=(x, y, 1 - c),
                                             device_id_type=_MESH)
        local.start()
        cps = [pltpu.make_async_remote_copy(src, out_ref.at[k], send.at[j], recv.at[j], device_id=(px, py, c),
                                            device_id_type=_MESH) for j, (px, py) in enumerate(chips)]
        for cp in cps:
            cp.start()
        for j, (px, py) in enumerate(chips):
            got = out_ref.at[2 * px + py]
            pltpu.make_async_remote_copy(got, got, send.at[j], recv.at[j], device_id=(px, py, c),
                                         device_id_type=_MESH).wait_recv()
        for cp in cps:
            cp.wait_send()
        local.wait()

    return pl.pallas_call(body, out_shape=jax.ShapeDtypeStruct((N_SHARDS, rows, buf.shape[1]), F32),
                          in_specs=[_ANY], out_specs=_ANY, scratch_shapes=_dma_sems(3, 3, 2),
                          name="gather_replicated_grads")(buf)


def _all_sum(vec):
    r, c = vec.shape
    n_dev = 2 * N_SHARDS

    def body(in_ref, out_ref, slots, send, recv):
        x, y, cc, _ = _place()
        flip = lambda v, bit: 1 - v if bit else v
        peers = [(flip(x, (q >> 2) & 1), flip(y, (q >> 1) & 1), flip(cc, q & 1)) for q in range(1, n_dev)]
        index = lambda p: 4 * p[0] + 2 * p[1] + p[2]
        slots[index((x, y, cc))] = in_ref[...]
        cps = [pltpu.make_async_remote_copy(in_ref, slots.at[index((x, y, cc))], send.at[q], recv.at[q], device_id=p,
                                            device_id_type=_MESH) for q, p in enumerate(peers)]
        for cp in cps:
            cp.start()
        for q, p in enumerate(peers):
            got = slots.at[index(p)]
            pltpu.make_async_remote_copy(got, got, send.at[q], recv.at[q], device_id=p, device_id_type=_MESH).wait_recv()
        for cp in cps:
            cp.wait_send()
        acc = slots[0]
        for s in range(1, n_dev):
            acc = acc + slots[s]
        out_ref[...] = acc

    vmem = pl.BlockSpec(memory_space=pltpu.VMEM)
    return pl.pallas_call(body, out_shape=jax.ShapeDtypeStruct((r, c), F32), in_specs=[vmem], out_specs=vmem,
                          scratch_shapes=[pltpu.VMEM((n_dev, r, c), F32)] + _dma_sems(n_dev - 1, n_dev - 1),
                          name="sum_small_grads")(vec)


def _not_before(value, other):
    return lax.optimization_barrier((value, other))[0]


def _round_up(n, m):
    return -(-n // m) * m


def _pack_flat(vecs, rows, width, dtype):
    flat = jnp.concatenate([v.reshape(-1).astype(dtype) for v in vecs])
    return jnp.pad(flat, (0, rows * width - flat.size)).reshape(rows, width)


def _split_flat(flat, shapes):
    out, off = [], 0
    for s in shapes:
        n = math.prod(s)
        out.append(flat[off:off + n].reshape(s))
        off += n
    return out


def _merge_shards(arr4, axis):
    a = jnp.moveaxis(arr4, 0, axis)
    s = list(a.shape)
    return a.reshape(s[:axis] + [s[axis] * s[axis + 1]] + s[axis + 2:])


def _split_shards(full, axis):
    s = list(full.shape)
    a = full.reshape(s[:axis] + [N_SHARDS, s[axis] // N_SHARDS] + s[axis + 1:])
    return jnp.moveaxis(a, axis, 0).reshape(N_SHARDS, -1)


def _rot_cols(w):
    half = w.shape[-1] // 2
    return jnp.concatenate([-w[..., half:], w[..., :half]], axis=-1)


def _unrot_cols(dw):
    half = dw.shape[-1] // 2
    return jnp.concatenate([dw[..., half:], -dw[..., :half]], axis=-1)


def kernel(x, positions, ffn_pre_g, ffn_post_g, ffn_w_gate, ffn_w_up, ffn_w_down, mix_pre_g, mix_post_g, gmlp_w_in, gmlp_ln_g, gmlp_ln_b, gmlp_w_s, gmlp_b_s, gmlp_w_out, kv_norm_g, w_dkv, kv_a_norm_g, w_ukv, mla_w_dq, mla_q_norm_g, mla_w_uq, mla_w_o, loss_target, m_ffn_pre_g, m_ffn_post_g, m_ffn_w_gate, m_ffn_w_up, m_ffn_w_down, m_mix_pre_g, m_mix_post_g, m_gmlp_w_in, m_gmlp_ln_g, m_gmlp_ln_b, m_gmlp_w_s, m_gmlp_b_s, m_gmlp_w_out, m_kv_norm_g, m_w_dkv, m_kv_a_norm_g, m_w_ukv, m_mla_w_dq, m_mla_q_norm_g, m_mla_w_uq, m_mla_w_o, v_ffn_pre_g, v_ffn_post_g, v_ffn_w_gate, v_ffn_w_up, v_ffn_w_down, v_mix_pre_g, v_mix_post_g, v_gmlp_w_in, v_gmlp_ln_g, v_gmlp_ln_b, v_gmlp_w_s, v_gmlp_b_s, v_gmlp_w_out, v_kv_norm_g, v_w_dkv, v_kv_a_norm_g, v_w_ukv, v_mla_w_dq, v_mla_q_norm_g, v_mla_w_uq, v_mla_w_o):
    names = ["ffn_pre_g", "ffn_post_g", "ffn_w_gate", "ffn_w_up", "ffn_w_down", "mix_pre_g", "mix_post_g", "gmlp_w_in",
             "gmlp_ln_g", "gmlp_ln_b", "gmlp_w_s", "gmlp_b_s", "gmlp_w_out", "kv_norm_g", "w_dkv", "kv_a_norm_g", "w_ukv",
             "mla_w_dq", "mla_q_norm_g", "mla_w_uq", "mla_w_o"]
    env = locals()
    w = {n: env[n] for n in names}
    mom = {n: env["m_" + n] for n in names}
    var = {n: env["v_" + n] for n in names}

    bsz, seq, d = x.shape
    t = bsz * seq
    core = lax.axis_index("c").astype(jnp.int32).reshape(1)

    mats = [("gmlp_w_in", 2), ("gmlp_w_out", 1), ("w_dkv", 0), ("w_ukv", 1), ("mla_w_dq", 1), ("mla_w_uq", 2),
            ("mla_w_o", 1)]
    vecs = [("ffn_pre_g", 2), ("ffn_post_g", 2), ("gmlp_ln_g", 1), ("gmlp_ln_b", 1)]
    replicated = ["mix_pre_g", "mix_post_g", "gmlp_w_s", "gmlp_b_s", "kv_norm_g", "kv_a_norm_g", "mla_q_norm_g"]
    n_mats = sum(w[n].size for n, _ in mats)
    n_vecs = sum(w[n].size for n, _ in vecs)
    mat_rows = _round_up(-(-n_mats // PACK_WIDTH), 64)
    vec_rows = _round_up(-(-n_vecs // 128), 32)
    mat_pack = _pack_flat([w[n] for n, _ in mats], mat_rows, PACK_WIDTH, BF16).reshape(2, 2, mat_rows // 4, PACK_WIDTH)
    vec_pack = _pack_flat([w[n] for n, _ in vecs], vec_rows, 128, F32).reshape(2, 2, vec_rows // 4, 128)
    ffn_names = ("ffn_w_gate", "ffn_w_up", "ffn_w_down")

    def oriented(a, name):
        return a if name == "ffn_w_down" else jnp.swapaxes(a, 2, 3)

    lj = [(l, j) for l in range(2) for j in range(2)]
    plan = [((0, 0), (0, 1), [vec_pack], None), ((0, 0), (2,), [mat_pack], 0), ((0, 1), (0, 1, 2), [], 0),
            ((1, 0), (0, 1, 2), [], 0), ((1, 1), (0, 1, 2), [], 0)]
    ffn_w = {k: [None] * 3 for k in lj}
    landed = []
    for q, ((l, j), which, riders, after) in enumerate(plan):
        shards = [oriented(w[ffn_names[i]], ffn_names[i])[l, j].astype(BF16) for i in which]
        bufs = [s.reshape(2, 2, s.shape[0] // 4, s.shape[1]) for s in shards] + riders
        if after is not None:
            bufs = _not_before(bufs, landed[after])
        got = _all_gather(bufs, q + 1, f"gather_weights_{q}")
        landed.append(got[-1])
        for i, g, s in zip(which, got, shards):
            ffn_w[(l, j)][i] = g.reshape((N_SHARDS,) + s.shape)
        if riders and q == 0:
            vec_all = got[-1]
        if riders and q == 1:
            mat_all = got[-1]

    def unpack(packed, entries):
        flat4, off, out = packed.reshape(N_SHARDS, -1), 0, {}
        for n, ax in entries:
            out[n] = _merge_shards(flat4[:, off:off + w[n].size].reshape((N_SHARDS,) + w[n].shape), ax)
            off += w[n].size
        return out

    full = unpack(vec_all, vecs)
    ln_g, ln_b = full["gmlp_ln_g"], full["gmlp_ln_b"]
    pre_g, post_g = full["ffn_pre_g"], full["ffn_post_g"]
    w_s = w["gmlp_w_s"][0]
    bsb = w["gmlp_b_s"][0][:, :, None]
    row = lambda v: v.reshape(1, -1)

    inv_freq = ROPE_THETA ** (-jnp.arange(0, QK_ROPE, 2, dtype=F32) / QK_ROPE)
    ang = positions.astype(F32).reshape(t, 1) * inv_freq
    cos2 = jnp.concatenate([jnp.cos(ang)] * 2, axis=-1)
    sin2 = jnp.concatenate([jnp.sin(ang)] * 2, axis=-1)

    h0 = x.reshape(t, d)
    saved = {}

    def ffn_fwd(l, j, h, n, next_gs):
        wg, wu, wd = ffn_w[(l, j)]
        g, u, a = _ffn_up(n, wg, wu)
        f, h_new, *n_next = _ffn_down(a, wd, h, row(post_g[l, j]), next_gs)
        saved[("ffn", l, j)] = (h, n, g, u, a, f)
        return h_new, n_next

    n0 = _rms_fwd(h0, row(pre_g[0, 0]))
    h1, (n1,) = ffn_fwd(0, 0, h0, n0, row(w["mix_pre_g"][0]))

    full.update(unpack(_not_before(mat_all, h1), mats))
    w_in, w_out = full["gmlp_w_in"][0], full["gmlp_w_out"][0]
    w_c, w_kr = full["w_dkv"][:, :KV_RANK], full["w_dkv"][:, KV_RANK:]
    w_kr_rot = _rot_cols(w_kr)
    ukv = full["w_ukv"].reshape(KV_RANK, N_HEADS, 2, QK_NOPE)
    w_k, w_v = ukv[:, :, 0].reshape(KV_RANK, -1), ukv[:, :, 1].reshape(KV_RANK, -1)
    w_dq, w_o = full["mla_w_dq"][0], full["mla_w_o"][0]
    q_rank = w_dq.shape[1]
    uq = full["mla_w_uq"][0].reshape(q_rank, N_HEADS, QK_NOPE + QK_ROPE)
    w_qn = uq[:, :, :QK_NOPE].reshape(q_rank, -1)
    w_qr = uq[:, :, QK_NOPE:].transpose(1, 0, 2)
    w_qr_rot = _rot_cols(w_qr)

    zp = _mm2d("gmlp_in", [(n1, w_in, "nn", 0)], [(w_in.shape[1], F32)])[0]
    uv = _sgu_fwd(zp, ln_g, ln_b, w_s, bsb)
    half = uv.shape[1]
    tm = min(t, ROW_TILE)
    m0, h2, n2 = _down("gmlp_out", (uv, (tm, 512), lambda i, _, k: (i, k)), (w_out, (512, d), lambda i, _, k: (k, 0)),
                       half // 512, h1, row(w["mix_post_g"][0]), row(pre_g[0, 1]), 1.0)
    h3, (n3kv, n3) = ffn_fwd(0, 1, h2, n2, jnp.stack([w["kv_norm_g"], pre_g[1, 0]]))

    def kv_epi(accs, ex):
        c_raw = accs[0]
        return [c_raw, _rms(c_raw, ex[2]), accs[1] * ex[0] + accs[2] * ex[1]]

    c_raw, c_n, k_r = _mm2d("kv_down", [(n3kv, w_c, "nn", 0), (n3kv, w_kr, "nn", 1), (n3kv, w_kr_rot, "nn", 2)],
                            [(KV_RANK, F32), (KV_RANK, BF16), (QK_ROPE, BF16)], kv_epi, [cos2, sin2],
                            [row(w["kv_a_norm_g"])])
    k_n, v_h = _mm2d("kv_up", [(c_n, w_k, "nn", 0), (c_n, w_v, "nn", 1)], [(w_k.shape[1], BF16), (w_v.shape[1], BF16)])

    h4, (n4,) = ffn_fwd(1, 0, h3, n3, row(w["mix_pre_g"][1]))
    qd, qn = _mm2d("q_down", [(n4, w_dq, "nn", 0)], [(q_rank, F32), (q_rank, BF16)],
                   lambda accs, ex: [accs[0], _rms(accs[0], ex[0])], [], [row(w["mla_q_norm_g"][0])])
    q_n = _mm2d("q_up", [(qn, w_qn, "nn", 0)], [(w_qn.shape[1], BF16)])[0]
    q_r = _q_rope(qn, w_qr, w_qr_rot, cos2, sin2)
    o, lse = _attn_fwd(q_n, q_r, k_n, v_h, k_r, seq)
    m1, h5, n5 = _down("attn_out", (o, (tm, 512), lambda i, _, k: (i, k)), (w_o, (512, d), lambda i, _, k: (k, 0)),
                       o.shape[1] // 512, h4, row(w["mix_post_g"][1]), row(pre_g[1, 1]), 1.0)
    y, _ = ffn_fwd(1, 1, h5, n5, row(pre_g[1, 1]))

    loss_part, dy = _loss_head(y, loss_target.reshape(t, d))
    loss = lax.psum(loss_part, ("x", "y", "c"))

    chip = (2 * lax.axis_index("x") + lax.axis_index("y")).astype(jnp.int32).reshape(1)
    rs = {}

    def rs_launch(gid, parts):
        rs[gid] = {"parts": parts, "others": _swap_halves(parts, 7 + gid, f"grad_swap_{gid}")}

    def rs_mid(gid, after):
        r = rs[gid]
        parts, others = _not_before((r["parts"], r["others"]), after)
        r["chip"] = _by_shape(_add_half, parts, others, core)
        r["slots"] = _scatter_chips(r["chip"], 12 + gid, f"grad_scatter_{gid}")
        return r["chip"]

    def rs_end(gid, after):
        r = rs[gid]
        slots, mine = _not_before((r["slots"], r["chip"]), after)
        r["own"] = _by_shape(_sum_slots, slots, mine, chip)
        r["recv"] = _join_halves(r["own"], 17 + gid, f"grad_join_{gid}")
        return r["own"]

    d_pre, d_post = {}, {}

    def ffn_bwd(l, j, gid, dh_out, extra=()):
        h, n, g, u, a, f = saved[("ffn", l, j)]
        wg, wu, wd = ffn_w[(l, j)]
        df, dg, du, d_post[(l, j)] = _ffn_dact(f, dh_out, row(post_g[l, j]), wd, g, u)
        dwd = _ffn_dw_down(a, df)
        dwg, dwu = _ffn_dw_in(n, dg, du)
        parts = [p.reshape(N_SHARDS, 2, p.shape[1] // 2, p.shape[2]) for p in (dwg, dwu, dwd)]
        rs_launch(gid, parts)
        dg, du = _not_before((dg, du), parts)
        dh, d_pre[(l, j)], *rest = _ffn_dn(dg, du, wg, wu, h, dh_out, [(row(pre_g[l, j]), None)] + list(extra))
        return dh, rest

    dh5, _ = ffn_bwd(1, 1, 0, dy)
    dh5 = _not_before(dh5, rs_mid(0, dh5))

    dm1, g_mix_post1 = _norm_out_bwd("mix_post_bwd", m1, dh5, row(w["mix_post_g"][1]), 1.0)
    do = _mm2d("attn_out_dx", [(dm1, w_o, "nt", 0)], [(w_o.shape[0], BF16)])[0]
    g_w_o = _mm2d("attn_out_dw", [(o, dm1, "tn", 0)], [(d, BF16)])[0]
    dq_n, dk_n, dv_h, dq_c, dq_s, dk_r = _attn_bwd(q_n, q_r, k_n, v_h, k_r, do, lse, cos2, sin2, seq)
    dqn = _mm2d("q_up_dx", [(dq_n, w_qn, "nt", 0)], [(q_rank, F32)])[0]
    heads_x = ((N_HEADS, tm, QK_ROPE), lambda i, j, k: (0, i, 0))
    heads_w = ((N_HEADS, q_rank, QK_ROPE), lambda i, j, k: (0, 0, 0))
    q_row = ((tm, q_rank), lambda i, j, k: (i, 0))
    dqn = _mm("q_rope_dx", (t // tm, 1, 1), [(dq_c, *heads_x), (w_qr, *heads_w), (dq_s, *heads_x), (w_qr_rot, *heads_w)],
              [(0, 1, 0, "nt"), (2, 3, 0, "nt")], [(tm, q_rank)], [((t, q_rank), F32, *q_row)],
              lambda accs, ex: [accs[0] + ex[0]], [(dqn, *q_row)], inner=N_HEADS)[0]
    g_qn = _mm2d("q_up_dw", [(qn, dq_n, "tn", 0)], [(w_qn.shape[1], F32)])[0]
    g_qr, g_qr_rot = [g.transpose(0, 2, 1) for g in _q_rope_dw(qn, dq_c, dq_s)]
    dqd, g_q_norm = _norm_out_bwd("q_norm_bwd", qd, dqn, row(w["mla_q_norm_g"][0]), 1.0)
    dn4 = _mm2d("q_down_dx", [(dqd, w_dq, "nt", 0)], [(d, F32)])[0]
    g_w_dq = _mm2d("q_down_dw", [(n4, dqd, "tn", 0)], [(q_rank, BF16)])[0]
    dh4, g_mix_pre1 = _norm_in_bwd("mix_pre_bwd", h4, dh5, [(row(w["mix_pre_g"][1]), dn4)])

    dc_n = _mm2d("kv_up_dx", [(dk_n, w_k, "nt", 0), (dv_h, w_v, "nt", 0)], [(KV_RANK, F32)])[0]
    g_wk, g_wv = _mm2d("kv_up_dw", [(c_n, dk_n, "tn", 0), (c_n, dv_h, "tn", 1)], [(w_k.shape[1], F32), (w_v.shape[1], F32)])
    dc, g_kv_a = _norm_out_bwd("kv_a_norm_bwd", c_raw, dc_n, row(w["kv_a_norm_g"]), 1.0)
    dkr_c, dkr_s = _rope_bwd(dk_r, cos2, sin2)
    dn3kv = _mm2d("kv_down_dx", [(dc, w_c, "nt", 0), (dkr_c, w_kr, "nt", 0), (dkr_s, w_kr_rot, "nt", 0)], [(d, F32)])[0]
    g_wc, g_wkr, g_wkr_rot = _mm2d("kv_down_dw", [(n3kv, dc, "tn", 0), (n3kv, dkr_c, "tn", 1), (n3kv, dkr_s, "tn", 2)],
                                   [(KV_RANK, F32), (QK_ROPE, F32), (QK_ROPE, F32)])

    dh4 = _not_before(dh4, rs_end(0, dh4))
    dh3, (g_kv_norm,) = ffn_bwd(1, 0, 1, dh4, extra=[(row(w["kv_norm_g"]), dn3kv)])
    dh3 = _not_before(dh3, rs_mid(1, dh3))
    dh2, _ = ffn_bwd(0, 1, 2, dh3)
    dh2 = _not_before(dh2, (rs_end(1, dh2), rs_mid(2, dh2)))

    dm0, g_mix_post0 = _norm_out_bwd("mix_post_bwd", m0, dh2, row(w["mix_post_g"][0]), 1.0)
    d_uv = _mm2d("gmlp_out_dx", [(dm0, w_out, "nt", 0)], [(half, F32)])[0]
    g_w_out = _mm2d("gmlp_out_dw", [(uv, dm0, "tn", 0)], [(d, BF16)])[0]
    dzp, g_ln_g, g_ln_b, g_w_s, g_b_s = _sgu_bwd(zp, d_uv, ln_g, ln_b, w_s, bsb)
    dn1 = _mm2d("gmlp_in_dx", [(dzp, w_in, "nt", 0)], [(d, F32)])[0]
    tk, tmw, w_cols = min(t, K_TILE), min(d, ROW_TILE), w_in.shape[1] // N_SHARDS
    g_w_in = _mm("gmlp_in_dw", (d // tmw, N_SHARDS, t // tk),
                 [(n1, (tk, tmw), lambda i, j, k: (k, i)), (dzp, (tk, w_cols), lambda i, j, k: (k, j))],
                 [(0, 1, 0, "tn")], [(tmw, w_cols)],
                 [((N_SHARDS, d, w_cols), BF16, (None, tmw, w_cols), lambda i, j, k: (j, i, 0))], lambda accs, ex: accs)[0]
    dh1, g_mix_pre0 = _norm_in_bwd("mix_pre_bwd", h1, dh2, [(row(w["mix_pre_g"][0]), dn1)])

    g_w_dkv = jnp.concatenate([g_wc, g_wkr + _unrot_cols(g_wkr_rot)], axis=1).astype(BF16)
    direct = {"gmlp_w_in": g_w_in, "gmlp_w_out": g_w_out, "mla_w_o": g_w_o, "mla_w_dq": g_w_dq, "w_dkv": g_w_dkv}
    direct = {n: g.reshape(N_SHARDS, -1, g.shape[-1]) for n, g in direct.items()}
    part = {
        "w_ukv": jnp.stack([g_wk.reshape(KV_RANK, N_HEADS, QK_NOPE), g_wv.reshape(KV_RANK, N_HEADS, V_DIM)],
                           axis=2).reshape(KV_RANK, -1),
        "mla_w_uq": jnp.concatenate(
            [g_qn.reshape(q_rank, N_HEADS, QK_NOPE),
             (g_qr + _unrot_cols(g_qr_rot)).transpose(1, 0, 2)],
            axis=-1).reshape(1, q_rank, -1),
        "gmlp_ln_g": g_ln_g, "gmlp_ln_b": g_ln_b,
        "mix_pre_g": jnp.concatenate([g_mix_pre0, g_mix_pre1]), "mix_post_g": jnp.concatenate([g_mix_post0, g_mix_post1]),
        "gmlp_w_s": g_w_s[None], "gmlp_b_s": g_b_s.reshape(1, GROUPS, CHUNK),
        "kv_norm_g": g_kv_norm.reshape(-1), "kv_a_norm_g": g_kv_a.reshape(-1), "mla_q_norm_g": g_q_norm,
    }

    sharded = [e for e in mats + vecs if e[0] in part]
    n_sh = sum(w[n].size for n, _ in sharded)
    n_rep = sum(w[n].size for n in replicated)
    sh_rows = _round_up(-(-n_sh // PACK_WIDTH), 8)
    rep_rows = _round_up(-(-(n_rep // N_SHARDS) // PACK_WIDTH), 8)
    rows = _round_up(sh_rows + rep_rows, 32)
    sh_flat = jnp.concatenate([_split_shards(part[n], ax) for n, ax in sharded], axis=1)
    rep_flat = jnp.concatenate([part[n].reshape(-1) for n in replicated]).reshape(N_SHARDS, -1)
    small = jnp.concatenate([
        jnp.pad(sh_flat, ((0, 0), (0, sh_rows * PACK_WIDTH - n_sh))),
        jnp.pad(rep_flat, ((0, 0), (0, (rows - sh_rows) * PACK_WIDTH - n_rep // N_SHARDS)))], axis=1)
    small = small.astype(BF16).reshape(N_SHARDS, 2, rows // 2, PACK_WIDTH)

    rs_launch(3, [g.reshape(N_SHARDS, 2, g.shape[1] // 2, g.shape[2]) for g in direct.values()] + [small])
    dh1 = _not_before(dh1, rs_end(2, dh1))
    dh1 = _not_before(dh1, rs_mid(3, dh1))
    dx, _ = ffn_bwd(0, 0, 4, dh1)

    own3 = rs_end(3, dx)
    launched = rs_mid(4, (dx, own3))
    lj = [(l, j) for l in range(2) for j in range(2)]
    tiny = jnp.concatenate([d_pre[k] for k in lj] + [d_post[k] for k in lj]).reshape(-1, 128)
    tiny = _all_sum(_not_before(tiny, launched)).reshape(2, 2, 2, d)
    shard_cols = d // N_SHARDS
    grads = {"ffn_pre_g": lax.dynamic_slice_in_dim(tiny[0], chip[0] * shard_cols, shard_cols, axis=2),
             "ffn_post_g": lax.dynamic_slice_in_dim(tiny[1], chip[0] * shard_cols, shard_cols, axis=2)}
    own_small, recv_small = own3[-1], rs[3]["recv"][-1]
    delta, new_m, new_v = {}, {}, {}
    for q, n in enumerate(direct):
        lead = lambda a: a.reshape((1, 1) + a.shape[-2:])
        upd = _adamw_halves(lead(w[n]), lead(mom[n]), lead(var[n]), 0, 0, rs[3]["own"][q], rs[3]["recv"][q], core, None)
        grads[n], delta[n], new_m[n], new_v[n] = [o.reshape(w[n].shape) for o in upd]
    g_small = jnp.where(core[0] == 0, jnp.concatenate([own_small, recv_small]), jnp.concatenate([recv_small, own_small]))
    g_rep = _gather_rows(g_small, sh_rows, rep_rows)
    for (n, _), g in zip(sharded, _split_flat(g_small.reshape(-1), [w[n].shape for n, _ in sharded])):
        grads[n] = g
    rep_vec = g_rep.reshape(N_SHARDS, -1)[:, :n_rep // N_SHARDS].reshape(-1)
    for n, g in zip(replicated, _split_flat(rep_vec, [w[n].shape for n in replicated])):
        grads[n] = g

    for n in names:
        if n not in ffn_names and n not in delta:
            delta[n], new_m[n], new_v[n] = _adamw(w[n], grads[n], mom[n], var[n])
    chain = {n: None for n in ffn_names}

    def ffn_update(gid, l, j):
        for q, n in enumerate(ffn_names):
            chain[n] = _adamw_halves(oriented(w[n], n), oriented(mom[n], n), oriented(var[n], n), l, j,
                                     rs[gid]["own"][q], rs[gid]["recv"][q], core, chain[n])

    ffn_update(0, 1, 1)
    ffn_update(1, 1, 0)
    ffn_update(2, 0, 1)
    rs_end(4, ([delta[n] for n in delta], [chain[n] for n in ffn_names]))
    ffn_update(4, 0, 0)
    for n in ffn_names:
        grads[n], delta[n], new_m[n], new_v[n] = [oriented(o, n) for o in chain[n]]
    return (loss, dx.reshape(x.shape), *[grads[n] for n in names], *[delta[n] for n in names],
            *[new_m[n] for n in names], *[new_v[n] for n in names])
```

```python
import math

import jax
import jax.numpy as jnp
from jax import lax
from jax.experimental import pallas as pl
from jax.experimental.pallas import tpu as pltpu
from jax.experimental.pallas import tpu_sc as plsc

F32, BF16 = jnp.float32, jnp.bfloat16

RMS_EPS, LN_EPS, NEG_INF = 1e-6, 1e-5, -1e30
N_HEADS, QK_NOPE, QK_ROPE, V_DIM, KV_RANK = 8, 128, 64, 128, 256
CHUNK, GROUPS = 128, 16
ROPE_THETA = 10000.0
ADAM_LR, ADAM_B1, ADAM_B2, ADAM_EPS, ADAM_WD, ADAM_STEP = 0.001, 0.9, 0.999, 1e-08, 0.01, 10
N_SHARDS = 4

VMEM_LIMIT_BYTES = 48 * 1024 * 1024
ROW_TILE = 512
K_TILE = 2048
PACK_WIDTH = 1024

_DN = {"nn": (((1,), (0,)), ((), ())), "nt": (((1,), (1,)), ((), ())), "tn": (((0,), (0,)), ((), ()))}
_MESH = pl.DeviceIdType.MESH
_ANY = pl.BlockSpec(memory_space=pl.ANY)


def _params(sem):
    return pltpu.CompilerParams(dimension_semantics=sem, vmem_limit_bytes=VMEM_LIMIT_BYTES)


def _mm(name, grid, ins, pairs, acc_shapes, outs, epilogue, extras=(), inner=0, sums=()):
    n_in, n_ex, n_out = len(ins), len(extras), len(outs)
    gk = grid[2]

    def body(*refs):
        in_refs, ex_refs = refs[:n_in], refs[n_in:n_in + n_ex]
        out_refs = refs[n_in + n_ex:n_in + n_ex + n_out]
        sum_refs = refs[n_in + n_ex + n_out:n_in + n_ex + n_out + len(sums)]
        acc_refs = refs[n_in + n_ex + n_out + len(sums):]
        parts = [None] * len(acc_shapes)
        for a, b, c, dims in pairs:
            for s in range(max(inner, 1)):
                lhs, rhs = (in_refs[a][s], in_refs[b][s]) if inner else (in_refs[a][...], in_refs[b][...])
                p = lax.dot_general(lhs, rhs, _DN[dims], preferred_element_type=F32)
                parts[c] = p if parts[c] is None else parts[c] + p

        def finish(accs):
            vals = epilogue(accs, [r[...] for r in ex_refs])
            for r, v in zip(out_refs, vals):
                r[...] = v.astype(r.dtype)
            first = (pl.program_id(0) == 0) & (pl.program_id(1) == 0)
            for r, v in zip(sum_refs, vals[n_out:]):
                @pl.when(first)
                def _():
                    r[...] = v

                @pl.when(jnp.logical_not(first))
                def _():
                    r[...] += v

        if gk == 1:
            finish(parts)
        else:
            k = pl.program_id(2)

            @pl.when(k == 0)
            def _():
                for r, p in zip(acc_refs, parts):
                    r[...] = p

            @pl.when(k > 0)
            def _():
                for r, p in zip(acc_refs, parts):
                    r[...] += p

            @pl.when(k == gk - 1)
            def _():
                finish([r[...] for r in acc_refs])

    return pl.pallas_call(
        body,
        out_shape=[jax.ShapeDtypeStruct(s, d) for s, d, _, _ in outs] + [jax.ShapeDtypeStruct(s, F32) for s in sums],
        grid=grid,
        in_specs=[pl.BlockSpec(bs, im) for _, bs, im in list(ins) + list(extras)],
        out_specs=[pl.BlockSpec(bs, im) for _, _, bs, im in outs]
        + [pl.BlockSpec(s, lambda i, j, k, nd=len(s): (0,) * nd) for s in sums],
        scratch_shapes=[pltpu.VMEM(s, F32) for s in acc_shapes] if gk > 1 else [],
        name=name,
        compiler_params=_params(("arbitrary",) * 3 if sums else ("parallel", "parallel", "arbitrary")),
    )(*[a for a, _, _ in ins], *[a for a, _, _ in extras])


def _mm2d(name, pairs, outs, epilogue=None, row_extras=(), vec_extras=(), sums=()):
    def mk(a, dims):
        return (a.shape[0], a.shape[1]) if dims[0] == "n" else (a.shape[1], a.shape[0])

    def nk(b, dims):
        return (b.shape[1], b.shape[0]) if dims[1] == "n" else (b.shape[0], b.shape[1])

    m = mk(pairs[0][0], pairs[0][2])[0]
    ks = [mk(a, d)[1] for a, _, d, _ in pairs]
    n_acc = 1 + max(p[3] for p in pairs)
    acc_n = [None] * n_acc
    for a, b, d, c in pairs:
        assert mk(a, d)[0] == m and nk(b, d)[1] == mk(a, d)[1]
        acc_n[c] = nk(b, d)[0]
    tm = min(m, ROW_TILE)
    if len(set(ks)) == 1 and ks[0] > 1024:
        tk = K_TILE if ks[0] % K_TILE == 0 else 512
        tks, gk = [tk] * len(pairs), ks[0] // tk
    else:
        tks, gk = ks, 1
    if len(set(acc_n)) == 1 and acc_n[0] > 1024:
        tns, gj = [1024] * n_acc, acc_n[0] // 1024
    else:
        tns, gj = acc_n, 1

    ins, plist = [], []
    for (a, b, d, c), tk in zip(pairs, tks):
        tn = tns[c]
        a_spec = ((tm, tk), lambda i, j, k: (i, k)) if d[0] == "n" else ((tk, tm), lambda i, j, k: (k, i))
        b_spec = ((tk, tn), lambda i, j, k: (k, j)) if d[1] == "n" else ((tn, tk), lambda i, j, k: (j, k))
        ins += [(a, *a_spec), (b, *b_spec)]
        plist.append((len(ins) - 2, len(ins) - 1, c, d))
    extras = [(r, (tm, r.shape[1]), lambda i, j, k: (i, 0)) for r in row_extras]
    extras += [(v, v.shape, lambda i, j, k: (0, 0)) for v in vec_extras]
    out_specs = []
    for n, dt in outs:
        bn = 1024 if (gj > 1) else n
        out_specs.append(((m, n), dt, (tm, bn), lambda i, j, k: (i, j)))
    if epilogue is None:
        epilogue = lambda accs, ex: accs
    return _mm(name, (m // tm, gj, gk), ins, plist, [(tm, tn) for tn in tns], out_specs, epilogue, extras, sums=sums)


def _rms(x, g):
    return x * lax.rsqrt(jnp.mean(x * x, axis=-1, keepdims=True) + RMS_EPS) * g


def _rms_bwd(x, g, dy):
    r = lax.rsqrt(jnp.mean(x * x, axis=-1, keepdims=True) + RMS_EPS)
    gy = dy * g
    dx = r * gy - x * (r * r * r) * jnp.mean(gy * x, axis=-1, keepdims=True)
    return dx, jnp.sum(dy * x * r, axis=0, keepdims=True)


def _sigmoid(x):
    return 0.5 * (1.0 + jnp.tanh(0.5 * x))


_GELU_C = math.sqrt(2.0 / math.pi)


def _gelu(x):
    return x * (0.5 * (1.0 + jnp.tanh(_GELU_C * (x + 0.044715 * (x * x * x)))))


def _gelu_grad(x):
    t = jnp.tanh(_GELU_C * (x + 0.044715 * (x * x * x)))
    return 0.5 * (1.0 + t) + 0.5 * x * (1.0 - t * t) * (_GELU_C * (1.0 + 3.0 * 0.044715 * (x * x)))


def _rows(name, row_ins, vec_ins, fn, row_outs, acc_outs=()):
    t = row_ins[0].shape[0]
    tm = min(t, ROW_TILE)
    nr, nv, no = len(row_ins), len(vec_ins), len(row_outs)

    def body(*refs):
        outs, incs = fn([r[...] for r in refs[:nr]], [r[...] for r in refs[nr:nr + nv]])
        for r, v in zip(refs[nr + nv:nr + nv + no], outs):
            r[...] = v.astype(r.dtype)
        i = pl.program_id(0)
        for r, v in zip(refs[nr + nv + no:], incs):
            @pl.when(i == 0)
            def _():
                r[...] = v

            @pl.when(i > 0)
            def _():
                r[...] += v

    in_specs = [pl.BlockSpec((tm, a.shape[1]), lambda i: (i, 0)) for a in row_ins]
    in_specs += [pl.BlockSpec(v.shape, lambda i, nd=v.ndim: (0,) * nd) for v in vec_ins]
    out_shape = [jax.ShapeDtypeStruct((t, c), dt) for c, dt in row_outs]
    out_shape += [jax.ShapeDtypeStruct(s, F32) for s in acc_outs]
    out_specs = [pl.BlockSpec((tm, c), lambda i: (i, 0)) for c, _ in row_outs]
    out_specs += [pl.BlockSpec(s, lambda i, nd=len(s): (0,) * nd) for s in acc_outs]
    return pl.pallas_call(body, out_shape=out_shape, grid=(t // tm,), in_specs=in_specs, out_specs=out_specs,
                          name=name, compiler_params=_params(("arbitrary",)))(*row_ins, *vec_ins)


def _rms_fwd(x, g):
    return _rows("rms_fwd", [x], [g], lambda r, v: ([_rms(r[0], v[0])], []), [(x.shape[1], BF16)])[0]


def _norm_out_bwd(name, f, d_out, g, scale):
    def fn(r, v):
        dx, dg = _rms_bwd(r[0], v[0], r[1] * scale)
        return [dx], [dg]

    c = f.shape[1]
    return _rows(name, [f, d_out], [g], fn, [(c, BF16)], [(1, c)])


def _norm_in_bwd(name, h, d_res, branches):
    nb = len(branches)

    def fn(r, v):
        dh, dgs = r[1], []
        for b in range(nb):
            dx, dg = _rms_bwd(r[0], v[b], r[2 + b])
            dh = dh + dx
            dgs.append(dg)
        return [dh], dgs

    c = h.shape[1]
    return _rows(name, [h, d_res] + [dn for _, dn in branches], [g for g, _ in branches], fn, [(c, F32)],
                 [(1, c)] * nb)


def _loss_head(y, target):
    d = y.shape[1]

    def fn(r, v):
        e = r[0] - r[1]
        s = jnp.sum(jnp.sum(e * e, axis=1, keepdims=True), axis=0, keepdims=True) * (0.5 / d)
        return [e * (1.0 / d)], [jnp.broadcast_to(s, (1, 128))]

    dy, acc = _rows("loss_head", [y, target], [], fn, [(d, F32)], [(1, 128)])
    return acc[0, 0], dy


def _rope_bwd(dk, cos2, sin2):
    c = dk.shape[1]
    return _rows("rope_bwd", [dk, cos2, sin2], [], lambda r, v: ([r[0] * r[1], r[0] * r[2]], []),
                 [(c, BF16), (c, BF16)])


def _ffn_up(n, wg, wu):
    t, d = n.shape
    fs = wg.shape[-2]
    tm = min(t, ROW_TILE // 2)

    def body(n_ref, wg_ref, wu_ref, g_ref, u_ref, a_ref):
        x = n_ref[...]
        for s in range(N_SHARDS):
            g = lax.dot_general(x, wg_ref[s], _DN["nt"], preferred_element_type=F32)
            u = lax.dot_general(x, wu_ref[s], _DN["nt"], preferred_element_type=F32)
            g_ref[s] = g.astype(BF16)
            u_ref[s] = u.astype(BF16)
            a_ref[s] = (g * _sigmoid(g) * u).astype(BF16)

    hid = pl.BlockSpec((N_SHARDS, tm, fs), lambda i: (0, i, 0))
    whole = pl.BlockSpec(wg.shape, lambda i: (0, 0, 0))
    return pl.pallas_call(body, out_shape=[jax.ShapeDtypeStruct((N_SHARDS, t, fs), BF16)] * 3, grid=(t // tm,),
                          in_specs=[pl.BlockSpec((tm, d), lambda i: (i, 0)), whole, whole], out_specs=[hid] * 3,
                          name="ffn_up", compiler_params=_params(("parallel",)))(n, wg, wu)


def _down(name, a_in, w_in, gk, h, post_g, next_gs, scale, inner=0):
    t, d = h.shape
    tm = min(t, ROW_TILE)
    kn = next_gs.shape[0]

    def epi(accs, ex):
        f, hv, pg, ng = accs[0], ex[0], ex[1], ex[2]
        hn = hv + scale * _rms(f, pg)
        return [f, hn] + [_rms(hn, ng[q:q + 1]) for q in range(kn)]

    row = ((tm, d), lambda i, j, k: (i, 0))
    outs = [((t, d), F32, *row), ((t, d), F32, *row)] + [((t, d), BF16, *row)] * kn
    extras = [(h, *row), (post_g, (1, d), lambda i, j, k: (0, 0)), (next_gs, (kn, d), lambda i, j, k: (0, 0))]
    return _mm(name, (t // tm, 1, gk), [a_in, w_in], [(0, 1, 0, "nn")], [(tm, d)], outs, epi, extras, inner)


def _ffn_down(a, wd, h, post_g, next_gs):
    t, d = h.shape
    fs = a.shape[-1]
    tm = min(t, ROW_TILE)
    return _down("ffn_down", (a, (N_SHARDS, tm, fs), lambda i, _, k: (0, i, 0)),
                 (wd, (N_SHARDS, fs, d), lambda i, _, k: (0, 0, 0)), 1, h, post_g, next_gs, 0.5, N_SHARDS)


def _ffn_dact(f, d_out, post_g, wd, g, u):
    t, d = f.shape
    fs = g.shape[-1]
    tm = min(t, ROW_TILE // 2)

    def body(f_ref, do_ref, pg_ref, wd_ref, g_ref, u_ref, df_ref, dg_ref, du_ref, dpg_ref):
        dfv, dpg = _rms_bwd(f_ref[...], pg_ref[...], do_ref[...] * 0.5)
        dfb = dfv.astype(BF16)
        df_ref[...] = dfb
        i = pl.program_id(0)

        @pl.when(i == 0)
        def _():
            dpg_ref[...] = dpg

        @pl.when(i > 0)
        def _():
            dpg_ref[...] += dpg

        for s in range(N_SHARDS):
            da = lax.dot_general(dfb, wd_ref[s], _DN["nt"], preferred_element_type=F32)
            gv, uv = g_ref[s].astype(F32), u_ref[s].astype(F32)
            sg = _sigmoid(gv)
            dg_ref[s] = (da * uv * (sg * (1.0 + gv * (1.0 - sg)))).astype(BF16)
            du_ref[s] = (da * (gv * sg)).astype(BF16)

    row = pl.BlockSpec((tm, d), lambda i: (i, 0))
    hid = pl.BlockSpec((N_SHARDS, tm, fs), lambda i: (0, i, 0))
    vec = pl.BlockSpec((1, d), lambda i: (0, 0))
    hid_shape = jax.ShapeDtypeStruct((N_SHARDS, t, fs), BF16)
    return pl.pallas_call(
        body, out_shape=[jax.ShapeDtypeStruct((t, d), BF16), hid_shape, hid_shape, jax.ShapeDtypeStruct((1, d), F32)],
        grid=(t // tm,), in_specs=[row, row, vec, pl.BlockSpec(wd.shape, lambda i: (0, 0, 0)), hid, hid],
        out_specs=[row, hid, hid, vec], name="ffn_dact", compiler_params=_params(("arbitrary",)))(
            f, d_out, post_g, wd, g, u)


def _ffn_dn(dg, du, wg, wu, h, d_res, branches, then=None):
    _, t, fs = dg.shape
    d = wg.shape[-1]
    tm = min(t, ROW_TILE)
    nb = len(branches)
    a_spec = ((N_SHARDS, tm, fs), lambda i, _, k: (0, i, 0))
    w_spec = ((N_SHARDS, fs, d), lambda i, _, k: (0, 0, 0))
    row = ((tm, d), lambda i, _, k: (i, 0))
    vec = ((1, d), lambda i, _, k: (0, 0))

    def epi(accs, ex):
        hv, dh = ex[0], ex[1]
        dns, gs = [accs[0]] + ex[2:1 + nb], ex[1 + nb:]
        dgs = []
        for dn, g in zip(dns, gs[:nb]):
            dx, dgv = _rms_bwd(hv, g, dn)
            dh = dh + dx
            dgs.append(dgv)
        if then is None:
            return [dh] + dgs
        dm, dgm = _rms_bwd(ex[-1], gs[nb], dh * then[2])
        return [dh, dm] + dgs + [dgm]

    extras = [(h, *row), (d_res, *row)] + [(dn, *row) for _, dn in branches[1:]] + [(g, *vec) for g, _ in branches]
    outs = [((t, d), F32, *row)]
    if then is not None:
        extras += [(then[1], *vec), (then[0], *row)]
        outs.append(((t, d), BF16, *row))
    return _mm("ffn_dn", (t // tm, 1, 1), [(dg, *a_spec), (wg, *w_spec), (du, *a_spec), (wu, *w_spec)],
               [(0, 1, 0, "nn"), (2, 3, 0, "nn")], [(tm, d)], outs, epi, extras, inner=N_SHARDS,
               sums=[(1, d)] * (nb + (then is not None)))


def _ffn_dw_in(n, dg, du):
    _, t, fs = dg.shape
    d = n.shape[1]
    tk = min(t, K_TILE)
    a_spec = ((None, tk, fs), lambda s, _, k: (s, k, 0))
    o_spec = ((None, fs, d), lambda s, _, k: (s, 0, 0))
    outs = [((N_SHARDS, fs, d), BF16, *o_spec)] * 2
    return _mm("ffn_dw_in", (N_SHARDS, 1, t // tk), [(dg, *a_spec), (du, *a_spec), (n, (tk, d), lambda s, _, k: (k, 0))],
               [(0, 2, 0, "tn"), (1, 2, 1, "tn")], [(fs, d)] * 2, outs, lambda accs, ex: accs)


def _ffn_dw_down(a, df):
    _, t, fs = a.shape
    d = df.shape[1]
    tk = min(t, K_TILE)
    outs = [((N_SHARDS, fs, d), BF16, (None, fs, d), lambda s, _, k: (s, 0, 0))]
    return _mm("ffn_dw_down", (N_SHARDS, 1, t // tk),
               [(a, (None, tk, fs), lambda s, _, k: (s, k, 0)), (df, (tk, d), lambda s, _, k: (k, 0))],
               [(0, 1, 0, "tn")], [(fs, d)], outs, lambda accs, ex: accs)[0]


def _causal_weight(w):
    row = lax.broadcasted_iota(jnp.int32, (CHUNK, CHUNK), 0)
    col = lax.broadcasted_iota(jnp.int32, (CHUNK, CHUNK), 1)
    return row >= col, jnp.where(row >= col, w, 0.0).astype(BF16)


def _layer_norm(v, g, b):
    xc = v - jnp.mean(v, axis=-1, keepdims=True)
    rstd = lax.rsqrt(jnp.mean(xc * xc, axis=-1, keepdims=True) + LN_EPS)
    xhat = xc * rstd
    return xhat, rstd, xhat * g + b


def _sgu_specs(t, half, tm):
    return [pl.BlockSpec((tm, half), lambda i: (i, 0)), pl.BlockSpec((tm, half), lambda i: (i, 1))]


def _sgu_fwd(zp, ln_g, ln_b, w_s, bsb):
    t, half = zp.shape[0], zp.shape[1] // 2
    tm = min(t, 2 * CHUNK)

    def body(u_ref, v_ref, g_ref, b_ref, w_ref, bs_ref, o_ref):
        u = _gelu(u_ref[...])
        _, _, vn = _layer_norm(_gelu(v_ref[...]), g_ref[...], b_ref[...])
        vb = vn.astype(BF16)
        for g in range(GROUPS):
            _, wm = _causal_weight(w_ref[g])
            cols = slice(g * CHUNK, (g + 1) * CHUNK)
            for c in range(tm // CHUNK):
                rows = slice(c * CHUNK, (c + 1) * CHUNK)
                sv = jnp.dot(wm, vb[rows, cols], preferred_element_type=F32) + bs_ref[g]
                o_ref[rows, cols] = (u[rows, cols] * sv).astype(BF16)

    whole = lambda a: pl.BlockSpec(a.shape, lambda i, nd=a.ndim: (0,) * nd)
    return pl.pallas_call(
        body, out_shape=jax.ShapeDtypeStruct((t, half), BF16), grid=(t // tm,),
        in_specs=_sgu_specs(t, half, tm) + [whole(ln_g), whole(ln_b), whole(w_s), whole(bsb)],
        out_specs=pl.BlockSpec((tm, half), lambda i: (i, 0)), name="sgu_fwd",
        compiler_params=_params(("arbitrary",)))(zp, zp, ln_g, ln_b, w_s, bsb)


def _sgu_bwd(zp, d_uv, ln_g, ln_b, w_s, bsb):
    t, half = zp.shape[0], zp.shape[1] // 2
    tm = min(t, 2 * CHUNK)

    def body(u_ref, v_ref, d_ref, g_ref, b_ref, w_ref, bs_ref, dz_ref, dlg_ref, dlb_ref, dws_ref, dbs_ref, dvn_ref):
        i = pl.program_id(0)

        @pl.when(i == 0)
        def _():
            dlg_ref[...] = jnp.zeros_like(dlg_ref)
            dlb_ref[...] = jnp.zeros_like(dlb_ref)
            dws_ref[...] = jnp.zeros_like(dws_ref)
            dbs_ref[...] = jnp.zeros_like(dbs_ref)

        up, vp = u_ref[...], v_ref[...]
        u, gup = _gelu(up), _gelu_grad(up)
        xhat, rstd, vn = _layer_norm(_gelu(vp), g_ref[...], b_ref[...])
        vb = vn.astype(BF16)
        d = d_ref[...]
        for g in range(GROUPS):
            mask, wm = _causal_weight(w_ref[g])
            cols = slice(g * CHUNK, (g + 1) * CHUNK)
            for c in range(tm // CHUNK):
                rows = slice(c * CHUNK, (c + 1) * CHUNK)
                blk = vb[rows, cols]
                sv = jnp.dot(wm, blk, preferred_element_type=F32) + bs_ref[g]
                dblk = d[rows, cols]
                dz_ref[rows, cols] = (dblk * sv * gup[rows, cols]).astype(BF16)
                dsv = dblk * u[rows, cols]
                dsvb = dsv.astype(BF16)
                dvn_ref[rows, cols] = lax.dot_general(wm, dsvb, _DN["tn"], preferred_element_type=F32)
                dw = lax.dot_general(dsvb, blk, _DN["nt"], preferred_element_type=F32)
                dws_ref[g] += jnp.where(mask, dw, 0.0)
                dbs_ref[g] += jnp.sum(dsv, axis=1, keepdims=True)
        dvn = dvn_ref[...]
        dlg_ref[...] += jnp.sum(dvn * xhat, axis=0, keepdims=True)
        dlb_ref[...] += jnp.sum(dvn, axis=0, keepdims=True)
        dxh = dvn * g_ref[...]
        dv = rstd * (dxh - jnp.mean(dxh, axis=-1, keepdims=True)
                     - xhat * jnp.mean(dxh * xhat, axis=-1, keepdims=True))
        dz_ref[:, half:] = (dv * _gelu_grad(vp)).astype(BF16)

    whole = lambda a: pl.BlockSpec(a.shape, lambda i, nd=a.ndim: (0,) * nd)
    wshape = lambda s: pl.BlockSpec(s, lambda i, nd=len(s): (0,) * nd)
    out_shape = [jax.ShapeDtypeStruct((t, 2 * half), BF16), jax.ShapeDtypeStruct((1, half), F32),
                 jax.ShapeDtypeStruct((1, half), F32), jax.ShapeDtypeStruct(w_s.shape, F32),
                 jax.ShapeDtypeStruct((GROUPS, CHUNK, 1), F32)]
    return pl.pallas_call(
        body, out_shape=out_shape, grid=(t // tm,),
        in_specs=_sgu_specs(t, half, tm) + [pl.BlockSpec((tm, half), lambda i: (i, 0)), whole(ln_g), whole(ln_b),
                                            whole(w_s), whole(bsb)],
        out_specs=[pl.BlockSpec((tm, 2 * half), lambda i: (i, 0)), wshape((1, half)), wshape((1, half)),
                   wshape(w_s.shape), wshape((GROUPS, CHUNK, 1))],
        scratch_shapes=[pltpu.VMEM((tm, half), F32)], name="sgu_bwd",
        compiler_params=_params(("arbitrary",)))(zp, zp, d_uv, ln_g, ln_b, w_s, bsb)


_SCALE = (QK_NOPE + QK_ROPE) ** -0.5


def _attn_scores(qn, qr, kn, kr, i, tq, n):
    s = lax.dot_general(qn, kn, _DN["nt"], preferred_element_type=F32)
    s = (s + lax.dot_general(qr, kr, _DN["nt"], preferred_element_type=F32)) * _SCALE
    row = i * tq + lax.broadcasted_iota(jnp.int32, (tq, n), 0)
    col = lax.broadcasted_iota(jnp.int32, (tq, n), 1)
    return jnp.where(col <= row, s, NEG_INF)


def _attn_specs(seq):
    head = lambda b, h: (b, h)
    return dict(
        qn=pl.BlockSpec((seq, QK_NOPE), head),
        qr=pl.BlockSpec((None, seq, QK_ROPE), lambda b, h: (h, b, 0)),
        kr=pl.BlockSpec((seq, QK_ROPE), lambda b, h: (b, 0)),
        lse=pl.BlockSpec((None, seq, 1), lambda b, h: (h, b, 0)),
    )


def _attn_fwd(qn, qr, kn, v, kr, seq):
    t = qn.shape[0]
    tq = min(seq, 2 * CHUNK)
    sp = _attn_specs(seq)

    def body(qn_ref, qr_ref, kn_ref, v_ref, kr_ref, o_ref, lse_ref):
        for i in range(seq // tq):
            rows, n = slice(i * tq, (i + 1) * tq), (i + 1) * tq
            s = _attn_scores(qn_ref[rows, :], qr_ref[rows, :], kn_ref[0:n, :], kr_ref[0:n, :], i, tq, n)
            m = jnp.max(s, axis=-1, keepdims=True)
            p = jnp.exp(s - m)
            l = jnp.sum(p, axis=-1, keepdims=True)
            o_ref[rows, :] = jnp.dot((p / l).astype(BF16), v_ref[0:n, :], preferred_element_type=F32).astype(BF16)
            lse_ref[rows, :] = m + jnp.log(l)

    return pl.pallas_call(
        body, out_shape=[jax.ShapeDtypeStruct((t, N_HEADS * V_DIM), BF16), jax.ShapeDtypeStruct((N_HEADS, t, 1), F32)],
        grid=(t // seq, N_HEADS), in_specs=[sp["qn"], sp["qr"], sp["qn"], sp["qn"], sp["kr"]],
        out_specs=[sp["qn"], sp["lse"]], name="attn_fwd",
        compiler_params=_params(("parallel", "arbitrary")))(qn, qr, kn, v, kr)


def _attn_bwd(qn, qr, kn, v, kr, do, lse, cos2, sin2, seq):
    t = qn.shape[0]
    tq = min(seq, 2 * CHUNK)
    sp = _attn_specs(seq)

    def body(qn_ref, qr_ref, kn_ref, v_ref, kr_ref, do_ref, lse_ref, cos_ref, sin_ref,
             dqn_ref, dkn_ref, dv_ref, dqc_ref, dqs_ref, dkr_ref, dk_acc, dv_acc, dkr_acc):
        dk_acc[...] = jnp.zeros_like(dk_acc)
        dv_acc[...] = jnp.zeros_like(dv_acc)
        dkr_acc[...] = jnp.zeros_like(dkr_acc)
        for i in range(seq // tq):
            rows, n = slice(i * tq, (i + 1) * tq), (i + 1) * tq
            q_n, q_r, d_o = qn_ref[rows, :], qr_ref[rows, :], do_ref[rows, :]
            k_n, k_r = kn_ref[0:n, :], kr_ref[0:n, :]
            s = _attn_scores(q_n, q_r, k_n, k_r, i, tq, n)
            p = jnp.exp(s - lse_ref[rows, :])
            dp = lax.dot_general(d_o, v_ref[0:n, :], _DN["nt"], preferred_element_type=F32)
            ds = (p * (dp - jnp.sum(p * dp, axis=-1, keepdims=True)) * _SCALE).astype(BF16)
            dqn_ref[rows, :] = jnp.dot(ds, k_n, preferred_element_type=F32).astype(BF16)
            dqr = jnp.dot(ds, k_r, preferred_element_type=F32)
            dqc_ref[rows, :] = (dqr * cos_ref[rows, :]).astype(BF16)
            dqs_ref[rows, :] = (dqr * sin_ref[rows, :]).astype(BF16)
            dk_acc[0:n, :] += lax.dot_general(ds, q_n, _DN["tn"], preferred_element_type=F32)
            dkr_acc[0:n, :] += lax.dot_general(ds, q_r, _DN["tn"], preferred_element_type=F32)
            dv_acc[0:n, :] += lax.dot_general(p.astype(BF16), d_o, _DN["tn"], preferred_element_type=F32)
        dkn_ref[...] = dk_acc[...].astype(BF16)
        dv_ref[...] = dv_acc[...].astype(BF16)
        h = pl.program_id(1)

        @pl.when(h == 0)
        def _():
            dkr_ref[...] = dkr_acc[...]

        @pl.when(h > 0)
        def _():
            dkr_ref[...] += dkr_acc[...]

    wide = jax.ShapeDtypeStruct((t, N_HEADS * V_DIM), BF16)
    rope = jax.ShapeDtypeStruct((N_HEADS, t, QK_ROPE), BF16)
    krf = pl.BlockSpec((seq, QK_ROPE), lambda b, h: (b, 0))
    return pl.pallas_call(
        body, out_shape=[wide, wide, wide, rope, rope, jax.ShapeDtypeStruct((t, QK_ROPE), F32)],
        grid=(t // seq, N_HEADS),
        in_specs=[sp["qn"], sp["qr"], sp["qn"], sp["qn"], sp["kr"], sp["qn"], sp["lse"], krf, krf],
        out_specs=[sp["qn"], sp["qn"], sp["qn"], sp["qr"], sp["qr"], krf],
        scratch_shapes=[pltpu.VMEM((seq, QK_NOPE), F32), pltpu.VMEM((seq, V_DIM), F32), pltpu.VMEM((seq, QK_ROPE), F32)],
        name="attn_bwd", compiler_params=_params(("parallel", "arbitrary")))(qn, qr, kn, v, kr, do, lse, cos2, sin2)


def _q_rope(qn, w, w_rot, cos2, sin2):
    t, r = qn.shape
    nh, _, e = w.shape
    tm = min(t, ROW_TILE)

    def body(x_ref, w_ref, wr_ref, c_ref, s_ref, o_ref):
        x = x_ref[...]
        for h in range(nh):
            raw = jnp.dot(x, w_ref[h], preferred_element_type=F32)
            rot = jnp.dot(x, wr_ref[h], preferred_element_type=F32)
            o_ref[h] = (raw * c_ref[...] + rot * s_ref[...]).astype(BF16)

    whole = pl.BlockSpec(w.shape, lambda i: (0, 0, 0))
    rows = pl.BlockSpec((tm, e), lambda i: (i, 0))
    return pl.pallas_call(body, out_shape=jax.ShapeDtypeStruct((nh, t, e), BF16), grid=(t // tm,),
                          in_specs=[pl.BlockSpec((tm, r), lambda i: (i, 0)), whole, whole, rows, rows],
                          out_specs=pl.BlockSpec((nh, tm, e), lambda i: (0, i, 0)), name="q_rope",
                          compiler_params=_params(("parallel",)))(qn, w, w_rot, cos2, sin2)


def _q_rope_dw(qn, dq_c, dq_s):
    t, r = qn.shape
    nh, _, e = dq_c.shape
    tk = min(t, K_TILE)

    def body(x_ref, c_ref, s_ref, gc_ref, gs_ref):
        k = pl.program_id(0)
        x = x_ref[...]
        for h in range(nh):
            pc = lax.dot_general(c_ref[h], x, _DN["tn"], preferred_element_type=F32)
            ps = lax.dot_general(s_ref[h], x, _DN["tn"], preferred_element_type=F32)

            @pl.when(k == 0)
            def _():
                gc_ref[h] = pc
                gs_ref[h] = ps

            @pl.when(k > 0)
            def _():
                gc_ref[h] += pc
                gs_ref[h] += ps

    heads = pl.BlockSpec((nh, tk, e), lambda k: (0, k, 0))
    out = pl.BlockSpec((nh, e, r), lambda k: (0, 0, 0))
    return pl.pallas_call(body, out_shape=[jax.ShapeDtypeStruct((nh, e, r), F32)] * 2, grid=(t // tk,),
                          in_specs=[pl.BlockSpec((tk, r), lambda k: (k, 0)), heads, heads], out_specs=[out, out],
                          name="q_rope_dw", compiler_params=_params(("arbitrary",)))(qn, dq_c, dq_s)


def _row_tile(rows, cols, row_mult=8):
    cap = max(row_mult, (1 << 18) // cols)
    best = rows
    for tr in range(row_mult, min(rows, cap) + 1, row_mult):
        if rows % tr == 0:
            best = tr
    return best if rows > cap else rows


def _adamw_math(w, g, m, v):
    mv = ADAM_B1 * m + (1.0 - ADAM_B1) * g
    vv = ADAM_B2 * v + (1.0 - ADAM_B2) * (g * g)
    m_hat = mv / (1.0 - ADAM_B1 ** ADAM_STEP)
    v_hat = vv / (1.0 - ADAM_B2 ** ADAM_STEP)
    return -ADAM_LR * (m_hat / (jnp.sqrt(v_hat) + ADAM_EPS) + ADAM_WD * w), mv, vv


def _adamw(w, g, m, v):
    shape = w.shape
    c = shape[-1]
    r = w.size // c
    tr = _row_tile(r, c)

    def body(w_ref, g_ref, m_ref, v_ref, d_ref, nm_ref, nv_ref):
        d_ref[...], nm_ref[...], nv_ref[...] = _adamw_math(w_ref[...], g_ref[...], m_ref[...], v_ref[...])

    spec = pl.BlockSpec((tr, c), lambda i: (i, 0))
    outs = pl.pallas_call(body, out_shape=[jax.ShapeDtypeStruct((r, c), F32)] * 3, grid=(r // tr,),
                          in_specs=[spec] * 4, out_specs=[spec] * 3, name="adamw",
                          compiler_params=_params(("parallel",)))(*[a.reshape(r, c) for a in (w, g, m, v)])
    return [o.reshape(shape) for o in outs]


def _adamw_halves(w, m, v, l, j, own, recv, core, prev):
    nl, nj, rows, c = w.shape
    r = rows // 2
    tr = _row_tile(r, c)
    n_prev = 0 if prev is None else 4

    def body(core_ref, w_ref, own_ref, recv_ref, m_ref, v_ref, *rest):
        g_ref, d_ref, nm_ref, nv_ref = rest[n_prev:]
        g = jnp.where(pl.program_id(0) == core_ref[0], own_ref[...], recv_ref[...])
        g_ref[...] = g
        d_ref[...], nm_ref[...], nv_ref[...] = _adamw_math(w_ref[...], g, m_ref[...], v_ref[...])

    nb = r // tr
    slab = pl.BlockSpec((None, None, tr, c), lambda h, i, cr: (l, j, h * nb + i, 0))
    half = pl.BlockSpec((tr, c), lambda h, i, cr: (i, 0))
    grid_spec = pltpu.PrefetchScalarGridSpec(num_scalar_prefetch=1, grid=(2, nb),
                                             in_specs=[slab, half, half, slab, slab] + [_ANY] * n_prev,
                                             out_specs=[slab] * 4)
    return pl.pallas_call(body, out_shape=[jax.ShapeDtypeStruct(w.shape, F32)] * 4, grid_spec=grid_spec,
                          input_output_aliases={6 + q: q for q in range(n_prev)}, name="adamw_halves",
                          compiler_params=_params(("parallel",) * 2))(core, w, own, recv, m, v, *(prev or ()))


def _place():
    x, y, c = lax.axis_index("x"), lax.axis_index("y"), lax.axis_index("c")
    return x, y, c, [(1 - x, y), (x, 1 - y), (1 - x, 1 - y)]


def _dma_sems(*counts):
    return [pltpu.SemaphoreType.DMA((n,)) for n in counts]


def _all_gather(bufs, collective_id, name):
    n = len(bufs)

    def body(*refs):
        ins, outs = refs[:n], refs[n:2 * n]
        send, recv, fsend, frecv, osend, orecv = refs[2 * n:]
        x, y, c, _ = _place()
        xn, yn, sib = (1 - x, y, c), (x, 1 - y, c), (x, y, 1 - c)
        k, kx, ky, kd = 2 * x + y, 2 * (1 - x) + y, 2 * x + 1 - y, 2 * (1 - x) + 1 - y
        _handshake([xn, yn, sib])

        def copy(src, dst, sems, i, to):
            return pltpu.make_async_remote_copy(src, dst, sems[0].at[i], sems[1].at[i], device_id=to, device_id_type=_MESH)

        ici, d2d, own_s = (send, recv), (fsend, frecv), (osend, orecv)
        started = [copy(ins[b], outs[b].at[k], own_s, b, sib) for b in range(n)]
        for first in (True, False):
            for b in range(n):
                mine = outs[b].at[k, c]
                if first:
                    started += [copy(ins[b].at[c, 0], mine.at[0], ici, 6 * b, xn), copy(ins[b].at[c, 1], mine.at[1], ici, 6 * b + 1, yn)]
                else:
                    started += [copy(ins[b].at[c, 1], mine.at[1], ici, 6 * b + 2, xn), copy(ins[b].at[c, 0], mine.at[0], ici, 6 * b + 3, yn)]
        for cp in started:
            cp.start()
        passed = []
        for b in range(n):
            for i, (src_chip, q, to) in enumerate([(kx, 0, yn), (ky, 1, xn)]):
                piece = outs[b].at[src_chip, c, q]
                copy(piece, piece, ici, 6 * b + i, to).wait_recv()
                cp = copy(piece, piece, ici, 6 * b + 4 + i, to)
                cp.start()
                passed.append(cp)
        for b in range(n):
            for i, (src_chip, q) in enumerate([(kx, 1), (ky, 0)]):
                piece = outs[b].at[src_chip, c, q]
                copy(piece, piece, ici, 6 * b + 2 + i, xn).wait_recv()
                half = outs[b].at[src_chip, c]
                cp = copy(half, half, d2d, 3 * b + i, sib)
                cp.start()
                passed.append(cp)
        for b in range(n):
            for i, q in enumerate([0, 1]):
                piece = outs[b].at[kd, c, q]
                copy(piece, piece, ici, 6 * b + 4 + i, xn).wait_recv()
            half = outs[b].at[kd, c]
            cp = copy(half, half, d2d, 3 * b + 2, sib)
            cp.start()
            passed.append(cp)
        for b in range(n):
            for i, src_chip in enumerate([kx, ky, kd]):
                half = outs[b].at[src_chip, 1 - c]
                copy(half, half, d2d, 3 * b + i, sib).wait_recv()
        for cp in started[n:] + passed:
            cp.wait_send()
        for cp in started[:n]:
            cp.wait()

    return _sequencer(body, [jax.ShapeDtypeStruct((N_SHARDS,) + b.shape, b.dtype) for b in bufs],
                      _dma_sems(6 * n, 6 * n, 3 * n, 3 * n, n, n), collective_id, name, bufs)


def _sequencer(body, out_type, sems, collective_id, name, args):
    return pl.kernel(body, out_type=out_type, mesh=plsc.ScalarSubcoreMesh(axis_name="sequencer", num_cores=1),
                     scratch_types=sems, compiler_params=pltpu.CompilerParams(collective_id=collective_id),
                     name=name)(*args)


def _handshake(peers):
    barrier = pltpu.get_barrier_semaphore()
    for peer in peers:
        pl.semaphore_signal(barrier, inc=1, device_id=peer, device_id_type=_MESH)
    pl.semaphore_wait(barrier, len(peers))


def _swap_halves(parts, collective_id, name):
    n = len(parts)

    def body(*refs):
        ins, outs = refs[:n], refs[n:2 * n]
        send, recv = refs[2 * n:]
        x, y, c, _ = _place()
        _handshake([(x, y, 1 - c)])
        cps = [pltpu.make_async_remote_copy(ins[b].at[:, pl.ds(1 - c, 1)], outs[b], send.at[b], recv.at[b],
                                            device_id=(x, y, 1 - c), device_id_type=_MESH) for b in range(n)]
        for cp in cps:
            cp.start()
        for cp in cps:
            cp.wait()

    return _sequencer(body, [jax.ShapeDtypeStruct((N_SHARDS, 1) + p.shape[2:], p.dtype) for p in parts],
                      _dma_sems(n, n), collective_id, name, parts)


def _by_shape(fn, first, second, scalar):
    out, groups = [None] * len(first), {}
    for i, p in enumerate(first):
        groups.setdefault(p.shape, []).append(i)
    for idx in groups.values():
        for i, r in zip(idx, fn([first[i] for i in idx], [second[i] for i in idx], scalar)):
            out[i] = r
    return out


def _add_half(parts, others, core):
    n = len(parts)
    _, _, r, c = parts[0].shape
    tr = _row_tile(r, c, 16)

    def body(core_ref, *refs):
        for q in range(n):
            refs[2 * n + q][...] = (refs[q][...].astype(F32) + refs[n + q][...].astype(F32)).astype(BF16)

    grid_spec = pltpu.PrefetchScalarGridSpec(
        num_scalar_prefetch=1, grid=(N_SHARDS, r // tr),
        in_specs=[pl.BlockSpec((None, None, tr, c), lambda k, i, cr: (k, cr[0], i, 0))] * n
        + [pl.BlockSpec((None, None, tr, c), lambda k, i, cr: (k, 0, i, 0))] * n,
        out_specs=[pl.BlockSpec((None, tr, c), lambda k, i, cr: (k, i, 0))] * n)
    return pl.pallas_call(body, out_shape=[jax.ShapeDtypeStruct((N_SHARDS, r, c), BF16)] * n, grid_spec=grid_spec,
                          name="grad_add_half", compiler_params=_params(("parallel", "parallel")))(core, *parts, *others)


def _scatter_chips(parts, collective_id, name):
    n = len(parts)

    def body(*refs):
        ins, outs = refs[:n], refs[n:2 * n]
        send, recv = refs[2 * n:]
        x, y, c, chips = _place()
        k = 2 * x + y
        _handshake([(px, py, c) for px, py in chips])
        started = []
        for b in range(n):
            for j, (px, py) in enumerate(chips):
                cp = pltpu.make_async_remote_copy(ins[b].at[2 * px + py], outs[b].at[k], send.at[3 * b + j],
                                                  recv.at[3 * b + j], device_id=(px, py, c), device_id_type=_MESH)
                cp.start()
                started.append(cp)
        for b in range(n):
            for j, (px, py) in enumerate(chips):
                got = outs[b].at[2 * px + py]
                pltpu.make_async_remote_copy(got, got, send.at[3 * b + j], recv.at[3 * b + j],
                                             device_id=(px, py, c), device_id_type=_MESH).wait_recv()
        for cp in started:
            cp.wait_send()

    return _sequencer(body, [jax.ShapeDtypeStruct(p.shape, p.dtype) for p in parts], _dma_sems(3 * n, 3 * n),
                      collective_id, name, parts)


def _sum_slots(slots, mine, chip):
    n = len(slots)
    _, r, c = slots[0].shape
    tr = _row_tile(r, c, 16)

    def body(chip_ref, *refs):
        for q in range(n):
            own = refs[5 * q + 4][...].astype(F32)
            v = [jnp.where(chip_ref[0] == s, own, refs[5 * q + s][...].astype(F32)) for s in range(N_SHARDS)]
            refs[5 * n + q][...] = ((v[0] + v[1]) + v[2]) + v[3]

    def slot_spec(s):
        return pl.BlockSpec((None, tr, c), lambda i, kr: (jnp.where(kr[0] == s, (s + 1) % N_SHARDS, s), i, 0))

    per_buffer = [slot_spec(s) for s in range(N_SHARDS)] + [pl.BlockSpec((None, tr, c), lambda i, kr: (kr[0], i, 0))]
    grid_spec = pltpu.PrefetchScalarGridSpec(num_scalar_prefetch=1, grid=(r // tr,), in_specs=per_buffer * n,
                                             out_specs=[pl.BlockSpec((tr, c), lambda i, kr: (i, 0))] * n)
    args = [a for sl, mn in zip(slots, mine) for a in (sl, sl, sl, sl, mn)]
    return pl.pallas_call(body, out_shape=[jax.ShapeDtypeStruct((r, c), F32)] * n, grid_spec=grid_spec,
                          name="grad_sum_slots", compiler_params=_params(("parallel",)))(chip, *args)


def _join_halves(halves, collective_id, name):
    n = len(halves)

    def body(*refs):
        ins, outs = refs[:n], refs[n:2 * n]
        send, recv = refs[2 * n:]
        x, y, c, _ = _place()
        _handshake([(x, y, 1 - c)])
        cps = [pltpu.make_async_remote_copy(ins[b], outs[b], send.at[b], recv.at[b], device_id=(x, y, 1 - c),
                                            device_id_type=_MESH) for b in range(n)]
        for cp in cps:
            cp.start()
        for cp in cps:
            cp.wait()

    return _sequencer(body, [jax.ShapeDtypeStruct(h.shape, F32) for h in halves], _dma_sems(n, n), collective_id,
                      name, halves)


def _gather_rows(buf, start, rows):
    def body(in_ref, out_ref, send, recv, lsem):
        x, y, c, chips = _place()
        k = 2 * x + y
        src = in_ref.at[pl.ds(start, rows)]
        local = pltpu.make_async_remote_copy(src, out_ref.at[k], lsem.at[0], lsem.at[1], device_id=(x, y, 1 - c),
                                             device_id_type=_MESH)
        local.start()
        cps = [pltpu.make_async_remote_copy(src, out_ref.at[k], send.at[j], recv.at[j], device_id=(px, py, c),
                                            device_id_type=_MESH) for j, (px, py) in enumerate(chips)]
        for cp in cps:
            cp.start()
        for j, (px, py) in enumerate(chips):
            got = out_ref.at[2 * px + py]
            pltpu.make_async_remote_copy(got, got, send.at[j], recv.at[j], device_id=(px, py, c),
                                         device_id_type=_MESH).wait_recv()
        for cp in cps:
            cp.wait_send()
        local.wait()

    return pl.pallas_call(body, out_shape=jax.ShapeDtypeStruct((N_SHARDS, rows, buf.shape[1]), F32),
                          in_specs=[_ANY], out_specs=_ANY, scratch_shapes=_dma_sems(3, 3, 2),
                          name="gather_replicated_grads")(buf)


def _all_sum(vec):
    r, c = vec.shape
    n_dev = 2 * N_SHARDS

    def body(in_ref, out_ref, slots, send, recv):
        x, y, cc, _ = _place()
        flip = lambda v, bit: 1 - v if bit else v
        peers = [(flip(x, (q >> 2) & 1), flip(y, (q >> 1) & 1), flip(cc, q & 1)) for q in range(1, n_dev)]
        index = lambda p: 4 * p[0] + 2 * p[1] + p[2]
        slots[index((x, y, cc))] = in_ref[...]
        cps = [pltpu.make_async_remote_copy(in_ref, slots.at[index((x, y, cc))], send.at[q], recv.at[q], device_id=p,
                                            device_id_type=_MESH) for q, p in enumerate(peers)]
        for cp in cps:
            cp.start()
        for q, p in enumerate(peers):
            got = slots.at[index(p)]
            pltpu.make_async_remote_copy(got, got, send.at[q], recv.at[q], device_id=p, device_id_type=_MESH).wait_recv()
        for cp in cps:
            cp.wait_send()
        acc = slots[0]
        for s in range(1, n_dev):
            acc = acc + slots[s]
        out_ref[...] = acc

    vmem = pl.BlockSpec(memory_space=pltpu.VMEM)
    return pl.pallas_call(body, out_shape=jax.ShapeDtypeStruct((r, c), F32), in_specs=[vmem], out_specs=vmem,
                          scratch_shapes=[pltpu.VMEM((n_dev, r, c), F32)] + _dma_sems(n_dev - 1, n_dev - 1),
                          name="sum_small_grads")(vec)


def _not_before(value, other):
    return lax.optimization_barrier((value, other))[0]


def _round_up(n, m):
    return -(-n // m) * m


def _pack_flat(vecs, rows, width, dtype):
    flat = jnp.concatenate([v.reshape(-1).astype(dtype) for v in vecs])
    return jnp.pad(flat, (0, rows * width - flat.size)).reshape(rows, width)


def _split_flat(flat, shapes):
    out, off = [], 0
    for s in shapes:
        n = math.prod(s)
        out.append(flat[off:off + n].reshape(s))
        off += n
    return out


def _merge_shards(arr4, axis):
    a = jnp.moveaxis(arr4, 0, axis)
    s = list(a.shape)
    return a.reshape(s[:axis] + [s[axis] * s[axis + 1]] + s[axis + 2:])


def _split_shards(full, axis):
    s = list(full.shape)
    a = full.reshape(s[:axis] + [N_SHARDS, s[axis] // N_SHARDS] + s[axis + 1:])
    return jnp.moveaxis(a, axis, 0).reshape(N_SHARDS, -1)


def _rot_cols(w):
    half = w.shape[-1] // 2
    return jnp.concatenate([-w[..., half:], w[..., :half]], axis=-1)


def _unrot_cols(dw):
    half = dw.shape[-1] // 2
    return jnp.concatenate([dw[..., half:], -dw[..., :half]], axis=-1)


def kernel(x, positions, ffn_pre_g, ffn_post_g, ffn_w_gate, ffn_w_up, ffn_w_down, mix_pre_g, mix_post_g, gmlp_w_in, gmlp_ln_g, gmlp_ln_b, gmlp_w_s, gmlp_b_s, gmlp_w_out, kv_norm_g, w_dkv, kv_a_norm_g, w_ukv, mla_w_dq, mla_q_norm_g, mla_w_uq, mla_w_o, loss_target, m_ffn_pre_g, m_ffn_post_g, m_ffn_w_gate, m_ffn_w_up, m_ffn_w_down, m_mix_pre_g, m_mix_post_g, m_gmlp_w_in, m_gmlp_ln_g, m_gmlp_ln_b, m_gmlp_w_s, m_gmlp_b_s, m_gmlp_w_out, m_kv_norm_g, m_w_dkv, m_kv_a_norm_g, m_w_ukv, m_mla_w_dq, m_mla_q_norm_g, m_mla_w_uq, m_mla_w_o, v_ffn_pre_g, v_ffn_post_g, v_ffn_w_gate, v_ffn_w_up, v_ffn_w_down, v_mix_pre_g, v_mix_post_g, v_gmlp_w_in, v_gmlp_ln_g, v_gmlp_ln_b, v_gmlp_w_s, v_gmlp_b_s, v_gmlp_w_out, v_kv_norm_g, v_w_dkv, v_kv_a_norm_g, v_w_ukv, v_mla_w_dq, v_mla_q_norm_g, v_mla_w_uq, v_mla_w_o):
    names = ["ffn_pre_g", "ffn_post_g", "ffn_w_gate", "ffn_w_up", "ffn_w_down", "mix_pre_g", "mix_post_g", "gmlp_w_in",
             "gmlp_ln_g", "gmlp_ln_b", "gmlp_w_s", "gmlp_b_s", "gmlp_w_out", "kv_norm_g", "w_dkv", "kv_a_norm_g", "w_ukv",
             "mla_w_dq", "mla_q_norm_g", "mla_w_uq", "mla_w_o"]
    env = locals()
    w = {n: env[n] for n in names}
    mom = {n: env["m_" + n] for n in names}
    var = {n: env["v_" + n] for n in names}

    bsz, seq, d = x.shape
    t = bsz * seq
    core = lax.axis_index("c").astype(jnp.int32).reshape(1)

    mats = [("gmlp_w_in", 2), ("gmlp_w_out", 1), ("w_dkv", 0), ("w_ukv", 1), ("mla_w_dq", 1), ("mla_w_uq", 2),
            ("mla_w_o", 1)]
    vecs = [("ffn_pre_g", 2), ("ffn_post_g", 2), ("gmlp_ln_g", 1), ("gmlp_ln_b", 1)]
    replicated = ["mix_pre_g", "mix_post_g", "gmlp_w_s", "gmlp_b_s", "kv_norm_g", "kv_a_norm_g", "mla_q_norm_g"]
    n_mats = sum(w[n].size for n, _ in mats)
    n_vecs = sum(w[n].size for n, _ in vecs)
    mat_rows = _round_up(-(-n_mats // PACK_WIDTH), 64)
    vec_rows = _round_up(-(-n_vecs // 128), 32)
    mat_pack = _pack_flat([w[n] for n, _ in mats], mat_rows, PACK_WIDTH, BF16).reshape(2, 2, mat_rows // 4, PACK_WIDTH)
    vec_pack = _pack_flat([w[n] for n, _ in vecs], vec_rows, 128, F32).reshape(2, 2, vec_rows // 4, 128)
    ffn_names = ("ffn_w_gate", "ffn_w_up", "ffn_w_down")

    def oriented(a, name):
        return a if name == "ffn_w_down" else jnp.swapaxes(a, 2, 3)

    lj = [(l, j) for l in range(2) for j in range(2)]
    plan = [((0, 0), (0, 1), [vec_pack], None), ((0, 0), (2,), [mat_pack], 0), ((0, 1), (0, 1, 2), [], 0),
            ((1, 0), (0, 1, 2), [], 0), ((1, 1), (0, 1, 2), [], 0)]
    ffn_w = {k: [None] * 3 for k in lj}
    landed = []
    for q, ((l, j), which, riders, after) in enumerate(plan):
        shards = [oriented(w[ffn_names[i]], ffn_names[i])[l, j].astype(BF16) for i in which]
        bufs = [s.reshape(2, 2, s.shape[0] // 4, s.shape[1]) for s in shards] + riders
        if after is not None:
            bufs = _not_before(bufs, landed[after])
        got = _all_gather(bufs, q + 1, f"gather_weights_{q}")
        landed.append(got[-1])
        for i, g, s in zip(which, got, shards):
            ffn_w[(l, j)][i] = g.reshape((N_SHARDS,) + s.shape)
        if riders and q == 0:
            vec_all = got[-1]
        if riders and q == 1:
            mat_all = got[-1]

    def unpack(packed, entries):
        flat4, off, out = packed.reshape(N_SHARDS, -1), 0, {}
        for n, ax in entries:
            out[n] = _merge_shards(flat4[:, off:off + w[n].size].reshape((N_SHARDS,) + w[n].shape), ax)
            off += w[n].size
        return out

    full = unpack(vec_all, vecs)
    ln_g, ln_b = full["gmlp_ln_g"], full["gmlp_ln_b"]
    pre_g, post_g = full["ffn_pre_g"], full["ffn_post_g"]
    w_s = w["gmlp_w_s"][0]
    bsb = w["gmlp_b_s"][0][:, :, None]
    row = lambda v: v.reshape(1, -1)

    inv_freq = ROPE_THETA ** (-jnp.arange(0, QK_ROPE, 2, dtype=F32) / QK_ROPE)
    ang = positions.astype(F32).reshape(t, 1) * inv_freq
    cos2 = jnp.concatenate([jnp.cos(ang)] * 2, axis=-1)
    sin2 = jnp.concatenate([jnp.sin(ang)] * 2, axis=-1)

    h0 = x.reshape(t, d)
    saved = {}

    def ffn_fwd(l, j, h, n, next_gs):
        wg, wu, wd = ffn_w[(l, j)]
        g, u, a = _ffn_up(n, wg, wu)
        f, h_new, *n_next = _ffn_down(a, wd, h, row(post_g[l, j]), next_gs)
        saved[("ffn", l, j)] = (h, n, g, u, a, f)
        return h_new, n_next

    n0 = _rms_fwd(h0, row(pre_g[0, 0]))
    h1, (n1,) = ffn_fwd(0, 0, h0, n0, row(w["mix_pre_g"][0]))

    full.update(unpack(_not_before(mat_all, h1), mats))
    w_in, w_out = full["gmlp_w_in"][0], full["gmlp_w_out"][0]
    w_c, w_kr = full["w_dkv"][:, :KV_RANK], full["w_dkv"][:, KV_RANK:]
    w_kr_rot = _rot_cols(w_kr)
    ukv = full["w_ukv"].reshape(KV_RANK, N_HEADS, 2, QK_NOPE)
    w_k, w_v = ukv[:, :, 0].reshape(KV_RANK, -1), ukv[:, :, 1].reshape(KV_RANK, -1)
    w_dq, w_o = full["mla_w_dq"][0], full["mla_w_o"][0]
    q_rank = w_dq.shape[1]
    uq = full["mla_w_uq"][0].reshape(q_rank, N_HEADS, QK_NOPE + QK_ROPE)
    w_qn = uq[:, :, :QK_NOPE].reshape(q_rank, -1)
    w_qr = uq[:, :, QK_NOPE:].transpose(1, 0, 2)
    w_qr_rot = _rot_cols(w_qr)

    zp = _mm2d("gmlp_in", [(n1, w_in, "nn", 0)], [(w_in.shape[1], F32)])[0]
    uv = _sgu_fwd(zp, ln_g, ln_b, w_s, bsb)
    half = uv.shape[1]
    tm = min(t, ROW_TILE)
    m0, h2, n2 = _down("gmlp_out", (uv, (tm, 512), lambda i, _, k: (i, k)), (w_out, (512, d), lambda i, _, k: (k, 0)),
                       half // 512, h1, row(w["mix_post_g"][0]), row(pre_g[0, 1]), 1.0)
    h3, (n3kv, n3) = ffn_fwd(0, 1, h2, n2, jnp.stack([w["kv_norm_g"], pre_g[1, 0]]))

    def kv_epi(accs, ex):
        c_raw = accs[0]
        return [c_raw, _rms(c_raw, ex[2]), accs[1] * ex[0] + accs[2] * ex[1]]

    c_raw, c_n, k_r = _mm2d("kv_down", [(n3kv, w_c, "nn", 0), (n3kv, w_kr, "nn", 1), (n3kv, w_kr_rot, "nn", 2)],
                            [(KV_RANK, F32), (KV_RANK, BF16), (QK_ROPE, BF16)], kv_epi, [cos2, sin2],
                            [row(w["kv_a_norm_g"])])
    k_n, v_h = _mm2d("kv_up", [(c_n, w_k, "nn", 0), (c_n, w_v, "nn", 1)], [(w_k.shape[1], BF16), (w_v.shape[1], BF16)])

    h4, (n4,) = ffn_fwd(1, 0, h3, n3, row(w["mix_pre_g"][1]))
    qd, qn = _mm2d("q_down", [(n4, w_dq, "nn", 0)], [(q_rank, F32), (q_rank, BF16)],
                   lambda accs, ex: [accs[0], _rms(accs[0], ex[0])], [], [row(w["mla_q_norm_g"][0])])
    q_n = _mm2d("q_up", [(qn, w_qn, "nn", 0)], [(w_qn.shape[1], BF16)])[0]
    q_r = _q_rope(qn, w_qr, w_qr_rot, cos2, sin2)
    o, lse = _attn_fwd(q_n, q_r, k_n, v_h, k_r, seq)
    m1, h5, n5 = _down("attn_out", (o, (tm, 512), lambda i, _, k: (i, k)), (w_o, (512, d), lambda i, _, k: (k, 0)),
                       o.shape[1] // 512, h4, row(w["mix_post_g"][1]), row(pre_g[1, 1]), 1.0)
    y, _ = ffn_fwd(1, 1, h5, n5, row(pre_g[1, 1]))

    loss_part, dy = _loss_head(y, loss_target.reshape(t, d))
    loss = lax.psum(loss_part, ("x", "y", "c"))

    chip = (2 * lax.axis_index("x") + lax.axis_index("y")).astype(jnp.int32).reshape(1)
    rs = {}

    def rs_launch(gid, parts):
        rs[gid] = {"parts": parts, "others": _swap_halves(parts, 7 + gid, f"grad_swap_{gid}")}

    def rs_mid(gid, after):
        r = rs[gid]
        parts, others = _not_before((r["parts"], r["others"]), after)
        r["chip"] = _by_shape(_add_half, parts, others, core)
        r["slots"] = _scatter_chips(r["chip"], 12 + gid, f"grad_scatter_{gid}")
        return r["chip"]

    def rs_end(gid, after):
        r = rs[gid]
        slots, mine = _not_before((r["slots"], r["chip"]), after)
        r["own"] = _by_shape(_sum_slots, slots, mine, chip)
        r["recv"] = _join_halves(r["own"], 17 + gid, f"grad_join_{gid}")
        return r["own"]

    d_pre, d_post = {}, {}

    def ffn_bwd(l, j, gid, dh_out, extra=(), then=None):
        h, n, g, u, a, f = saved[("ffn", l, j)]
        wg, wu, wd = ffn_w[(l, j)]
        df, dg, du, d_post[(l, j)] = _ffn_dact(f, dh_out, row(post_g[l, j]), wd, g, u)
        dwd = _ffn_dw_down(a, df)
        dwg, dwu = _ffn_dw_in(n, dg, du)
        parts = [p.reshape(N_SHARDS, 2, p.shape[1] // 2, p.shape[2]) for p in (dwg, dwu, dwd)]
        rs_launch(gid, parts)
        dg, du = _not_before((dg, du), parts)
        res = _ffn_dn(dg, du, wg, wu, h, dh_out, [(row(pre_g[l, j]), None)] + list(extra),
                      None if then is None else (then[0], then[1], 1.0))
        if then is None:
            dh, d_pre[(l, j)], *rest = res
            return dh, rest
        dh, dm, d_pre[(l, j)], *rest = res
        return dh, [dm, rest[-1]] + rest[:-1]

    dh5, (dm1, g_mix_post1) = ffn_bwd(1, 1, 0, dy, then=(m1, row(w["mix_post_g"][1])))
    dh5, dm1 = _not_before((dh5, dm1), rs_mid(0, dh5))

    do = _mm2d("attn_out_dx", [(dm1, w_o, "nt", 0)], [(w_o.shape[0], BF16)])[0]
    g_w_o = _mm2d("attn_out_dw", [(o, dm1, "tn", 0)], [(d, BF16)])[0]
    dq_n, dk_n, dv_h, dq_c, dq_s, dk_r = _attn_bwd(q_n, q_r, k_n, v_h, k_r, do, lse, cos2, sin2, seq)
    dqn = _mm2d("q_up_dx", [(dq_n, w_qn, "nt", 0)], [(q_rank, F32)])[0]
    heads_x = ((N_HEADS, tm, QK_ROPE), lambda i, j, k: (0, i, 0))
    heads_w = ((N_HEADS, q_rank, QK_ROPE), lambda i, j, k: (0, 0, 0))
    q_row = ((tm, q_rank), lambda i, j, k: (i, 0))
    dqn = _mm("q_rope_dx", (t // tm, 1, 1), [(dq_c, *heads_x), (w_qr, *heads_w), (dq_s, *heads_x), (w_qr_rot, *heads_w)],
              [(0, 1, 0, "nt"), (2, 3, 0, "nt")], [(tm, q_rank)], [((t, q_rank), F32, *q_row)],
              lambda accs, ex: [accs[0] + ex[0]], [(dqn, *q_row)], inner=N_HEADS)[0]
    g_qn = _mm2d("q_up_dw", [(qn, dq_n, "tn", 0)], [(w_qn.shape[1], F32)])[0]
    g_qr, g_qr_rot = [g.transpose(0, 2, 1) for g in _q_rope_dw(qn, dq_c, dq_s)]
    dqd, g_q_norm = _norm_out_bwd("q_norm_bwd", qd, dqn, row(w["mla_q_norm_g"][0]), 1.0)

    def pre_norm_bwd(accs, ex):
        dx, dgain = _rms_bwd(ex[0], ex[2], accs[0])
        return [ex[1] + dx, dgain]

    dh4, g_mix_pre1 = _mm2d("q_down_dx", [(dqd, w_dq, "nt", 0)], [(d, F32)], pre_norm_bwd, [h4, dh5],
                            [row(w["mix_pre_g"][1])], sums=[(1, d)])
    g_w_dq = _mm2d("q_down_dw", [(n4, dqd, "tn", 0)], [(q_rank, BF16)])[0]

    dc_n = _mm2d("kv_up_dx", [(dk_n, w_k, "nt", 0), (dv_h, w_v, "nt", 0)], [(KV_RANK, F32)])[0]
    g_wk, g_wv = _mm2d("kv_up_dw", [(c_n, dk_n, "tn", 0), (c_n, dv_h, "tn", 1)], [(w_k.shape[1], F32), (w_v.shape[1], F32)])
    dc, g_kv_a = _norm_out_bwd("kv_a_norm_bwd", c_raw, dc_n, row(w["kv_a_norm_g"]), 1.0)
    dkr_c, dkr_s = _rope_bwd(dk_r, cos2, sin2)
    dn3kv = _mm2d("kv_down_dx", [(dc, w_c, "nt", 0), (dkr_c, w_kr, "nt", 0), (dkr_s, w_kr_rot, "nt", 0)], [(d, F32)])[0]
    g_wc, g_wkr, g_wkr_rot = _mm2d("kv_down_dw", [(n3kv, dc, "tn", 0), (n3kv, dkr_c, "tn", 1), (n3kv, dkr_s, "tn", 2)],
                                   [(KV_RANK, F32), (QK_ROPE, F32), (QK_ROPE, F32)])

    dh4 = _not_before(dh4, rs_end(0, dh4))
    dh3, (g_kv_norm,) = ffn_bwd(1, 0, 1, dh4, extra=[(row(w["kv_norm_g"]), dn3kv)])
    dh3 = _not_before(dh3, rs_mid(1, dh3))
    dh2, (dm0, g_mix_post0) = ffn_bwd(0, 1, 2, dh3, then=(m0, row(w["mix_post_g"][0])))
    dh2, dm0 = _not_before((dh2, dm0), (rs_end(1, dh2), rs_mid(2, dh2)))

    d_uv = _mm2d("gmlp_out_dx", [(dm0, w_out, "nt", 0)], [(half, F32)])[0]
    g_w_out = _mm2d("gmlp_out_dw", [(uv, dm0, "tn", 0)], [(d, BF16)])[0]
    dzp, g_ln_g, g_ln_b, g_w_s, g_b_s = _sgu_bwd(zp, d_uv, ln_g, ln_b, w_s, bsb)
    dh1, g_mix_pre0 = _mm2d("gmlp_in_dx", [(dzp, w_in, "nt", 0)], [(d, F32)], pre_norm_bwd, [h1, dh2],
                            [row(w["mix_pre_g"][0])], sums=[(1, d)])
    tk, tmw, w_cols = min(t, K_TILE), min(d, ROW_TILE), w_in.shape[1] // N_SHARDS
    g_w_in = _mm("gmlp_in_dw", (d // tmw, N_SHARDS, t // tk),
                 [(n1, (tk, tmw), lambda i, j, k: (k, i)), (dzp, (tk, w_cols), lambda i, j, k: (k, j))],
                 [(0, 1, 0, "tn")], [(tmw, w_cols)],
                 [((N_SHARDS, d, w_cols), BF16, (None, tmw, w_cols), lambda i, j, k: (j, i, 0))], lambda accs, ex: accs)[0]

    g_w_dkv = jnp.concatenate([g_wc, g_wkr + _unrot_cols(g_wkr_rot)], axis=1).astype(BF16)
    direct = {"gmlp_w_in": g_w_in, "gmlp_w_out": g_w_out, "mla_w_o": g_w_o, "mla_w_dq": g_w_dq, "w_dkv": g_w_dkv}
    direct = {n: g.reshape(N_SHARDS, -1, g.shape[-1]) for n, g in direct.items()}
    part = {
        "w_ukv": jnp.stack([g_wk.reshape(KV_RANK, N_HEADS, QK_NOPE), g_wv.reshape(KV_RANK, N_HEADS, V_DIM)],
                           axis=2).reshape(KV_RANK, -1),
        "mla_w_uq": jnp.concatenate(
            [g_qn.reshape(q_rank, N_HEADS, QK_NOPE),
             (g_qr + _unrot_cols(g_qr_rot)).transpose(1, 0, 2)],
            axis=-1).reshape(1, q_rank, -1),
        "gmlp_ln_g": g_ln_g, "gmlp_ln_b": g_ln_b,
        "mix_pre_g": jnp.concatenate([g_mix_pre0, g_mix_pre1]), "mix_post_g": jnp.concatenate([g_mix_post0, g_mix_post1]),
        "gmlp_w_s": g_w_s[None], "gmlp_b_s": g_b_s.reshape(1, GROUPS, CHUNK),
        "kv_norm_g": g_kv_norm.reshape(-1), "kv_a_norm_g": g_kv_a.reshape(-1), "mla_q_norm_g": g_q_norm,
    }

    sharded = [e for e in mats + vecs if e[0] in part]
    n_sh = sum(w[n].size for n, _ in sharded)
    n_rep = sum(w[n].size for n in replicated)
    sh_rows = _round_up(-(-n_sh // PACK_WIDTH), 8)
    rep_rows = _round_up(-(-(n_rep // N_SHARDS) // PACK_WIDTH), 8)
    rows = _round_up(sh_rows + rep_rows, 32)
    sh_flat = jnp.concatenate([_split_shards(part[n], ax) for n, ax in sharded], axis=1)
    rep_flat = jnp.concatenate([part[n].reshape(-1) for n in replicated]).reshape(N_SHARDS, -1)
    small = jnp.concatenate([
        jnp.pad(sh_flat, ((0, 0), (0, sh_rows * PACK_WIDTH - n_sh))),
        jnp.pad(rep_flat, ((0, 0), (0, (rows - sh_rows) * PACK_WIDTH - n_rep // N_SHARDS)))], axis=1)
    small = small.astype(BF16).reshape(N_SHARDS, 2, rows // 2, PACK_WIDTH)

    rs_launch(3, [g.reshape(N_SHARDS, 2, g.shape[1] // 2, g.shape[2]) for g in direct.values()] + [small])
    dh1 = _not_before(dh1, rs_end(2, dh1))
    dh1 = _not_before(dh1, rs_mid(3, dh1))
    dx, _ = ffn_bwd(0, 0, 4, dh1)

    own3 = rs_end(3, dx)
    launched = rs_mid(4, (dx, own3))
    lj = [(l, j) for l in range(2) for j in range(2)]
    tiny = jnp.concatenate([d_pre[k] for k in lj] + [d_post[k] for k in lj]).reshape(-1, 128)
    tiny = _all_sum(_not_before(tiny, launched)).reshape(2, 2, 2, d)
    shard_cols = d // N_SHARDS
    grads = {"ffn_pre_g": lax.dynamic_slice_in_dim(tiny[0], chip[0] * shard_cols, shard_cols, axis=2),
             "ffn_post_g": lax.dynamic_slice_in_dim(tiny[1], chip[0] * shard_cols, shard_cols, axis=2)}
    own_small, recv_small = own3[-1], rs[3]["recv"][-1]
    delta, new_m, new_v = {}, {}, {}
    for q, n in enumerate(direct):
        lead = lambda a: a.reshape((1, 1) + a.shape[-2:])
        upd = _adamw_halves(lead(w[n]), lead(mom[n]), lead(var[n]), 0, 0, rs[3]["own"][q], rs[3]["recv"][q], core, None)
        grads[n], delta[n], new_m[n], new_v[n] = [o.reshape(w[n].shape) for o in upd]
    g_small = jnp.where(core[0] == 0, jnp.concatenate([own_small, recv_small]), jnp.concatenate([recv_small, own_small]))
    g_rep = _gather_rows(g_small, sh_rows, rep_rows)
    for (n, _), g in zip(sharded, _split_flat(g_small.reshape(-1), [w[n].shape for n, _ in sharded])):
        grads[n] = g
    rep_vec = g_rep.reshape(N_SHARDS, -1)[:, :n_rep // N_SHARDS].reshape(-1)
    for n, g in zip(replicated, _split_flat(rep_vec, [w[n].shape for n in replicated])):
        grads[n] = g

    for n in names:
        if n not in ffn_names and n not in delta:
            delta[n], new_m[n], new_v[n] = _adamw(w[n], grads[n], mom[n], var[n])
    chain = {n: None for n in ffn_names}

    def ffn_update(gid, l, j):
        for q, n in enumerate(ffn_names):
            chain[n] = _adamw_halves(oriented(w[n], n), oriented(mom[n], n), oriented(var[n], n), l, j,
                                     rs[gid]["own"][q], rs[gid]["recv"][q], core, chain[n])

    ffn_update(0, 1, 1)
    ffn_update(1, 1, 0)
    ffn_update(2, 0, 1)
    rs_end(4, ([delta[n] for n in delta], [chain[n] for n in ffn_names]))
    ffn_update(4, 0, 0)
    for n in ffn_names:
        grads[n], delta[n], new_m[n], new_v[n] = [oriented(o, n) for o in chain[n]]
    return (loss, dx.reshape(x.shape), *[grads[n] for n in names], *[delta[n] for n in names],
            *[new_m[n] for n in names], *[new_v[n] for n in names])
```

```python
import math

import jax
import jax.numpy as jnp
from jax import lax
from jax.experimental import pallas as pl
from jax.experimental.pallas import tpu as pltpu
from jax.experimental.pallas import tpu_sc as plsc

F32, BF16 = jnp.float32, jnp.bfloat16

RMS_EPS, LN_EPS, NEG_INF = 1e-6, 1e-5, -1e30
N_HEADS, QK_NOPE, QK_ROPE, V_DIM, KV_RANK = 8, 128, 64, 128, 256
CHUNK, GROUPS = 128, 16
ROPE_THETA = 10000.0
ADAM_LR, ADAM_B1, ADAM_B2, ADAM_EPS, ADAM_WD, ADAM_STEP = 0.001, 0.9, 0.999, 1e-08, 0.01, 10
N_SHARDS = 4

VMEM_LIMIT_BYTES = 48 * 1024 * 1024
ROW_TILE = 512
K_TILE = 2048
PACK_WIDTH = 1024

_DN = {"nn": (((1,), (0,)), ((), ())), "nt": (((1,), (1,)), ((), ())), "tn": (((0,), (0,)), ((), ()))}
_MESH = pl.DeviceIdType.MESH
_ANY = pl.BlockSpec(memory_space=pl.ANY)


def _params(sem):
    return pltpu.CompilerParams(dimension_semantics=sem, vmem_limit_bytes=VMEM_LIMIT_BYTES)


def _mm(name, grid, ins, pairs, acc_shapes, outs, epilogue, extras=(), inner=0, sums=()):
    n_in, n_ex, n_out = len(ins), len(extras), len(outs)
    gk = grid[2]

    def body(*refs):
        in_refs, ex_refs = refs[:n_in], refs[n_in:n_in + n_ex]
        out_refs = refs[n_in + n_ex:n_in + n_ex + n_out]
        sum_refs = refs[n_in + n_ex + n_out:n_in + n_ex + n_out + len(sums)]
        acc_refs = refs[n_in + n_ex + n_out + len(sums):]
        parts = [None] * len(acc_shapes)
        for a, b, c, dims in pairs:
            for s in range(max(inner, 1)):
                lhs, rhs = (in_refs[a][s], in_refs[b][s]) if inner else (in_refs[a][...], in_refs[b][...])
                p = lax.dot_general(lhs, rhs, _DN[dims], preferred_element_type=F32)
                parts[c] = p if parts[c] is None else parts[c] + p

        def finish(accs):
            vals = epilogue(accs, [r[...] for r in ex_refs])
            for r, v in zip(out_refs, vals):
                r[...] = v.astype(r.dtype)
            first = (pl.program_id(0) == 0) & (pl.program_id(1) == 0)
            for r, v in zip(sum_refs, vals[n_out:]):
                @pl.when(first)
                def _():
                    r[...] = v

                @pl.when(jnp.logical_not(first))
                def _():
                    r[...] += v

        if gk == 1:
            finish(parts)
        else:
            k = pl.program_id(2)

            @pl.when(k == 0)
            def _():
                for r, p in zip(acc_refs, parts):
                    r[...] = p

            @pl.when(k > 0)
            def _():
                for r, p in zip(acc_refs, parts):
                    r[...] += p

            @pl.when(k == gk - 1)
            def _():
                finish([r[...] for r in acc_refs])

    return pl.pallas_call(
        body,
        out_shape=[jax.ShapeDtypeStruct(s, d) for s, d, _, _ in outs] + [jax.ShapeDtypeStruct(s, F32) for s in sums],
        grid=grid,
        in_specs=[pl.BlockSpec(bs, im) for _, bs, im in list(ins) + list(extras)],
        out_specs=[pl.BlockSpec(bs, im) for _, _, bs, im in outs]
        + [pl.BlockSpec(s, lambda i, j, k, nd=len(s): (0,) * nd) for s in sums],
        scratch_shapes=[pltpu.VMEM(s, F32) for s in acc_shapes] if gk > 1 else [],
        name=name,
        compiler_params=_params(("arbitrary",) * 3 if sums else ("parallel", "parallel", "arbitrary")),
    )(*[a for a, _, _ in ins], *[a for a, _, _ in extras])


def _mm2d(name, pairs, outs, epilogue=None, row_extras=(), vec_extras=(), sums=(), n_outer=False):
    def mk(a, dims):
        return (a.shape[0], a.shape[1]) if dims[0] == "n" else (a.shape[1], a.shape[0])

    def nk(b, dims):
        return (b.shape[1], b.shape[0]) if dims[1] == "n" else (b.shape[0], b.shape[1])

    m = mk(pairs[0][0], pairs[0][2])[0]
    ks = [mk(a, d)[1] for a, _, d, _ in pairs]
    n_acc = 1 + max(p[3] for p in pairs)
    acc_n = [None] * n_acc
    for a, b, d, c in pairs:
        assert mk(a, d)[0] == m and nk(b, d)[1] == mk(a, d)[1]
        acc_n[c] = nk(b, d)[0]
    tm = min(m, ROW_TILE)
    if len(set(ks)) == 1 and ks[0] > 1024:
        tk = K_TILE if ks[0] % K_TILE == 0 else 512
        tks, gk = [tk] * len(pairs), ks[0] // tk
    else:
        tks, gk = ks, 1
    if len(set(acc_n)) == 1 and acc_n[0] > 1024:
        tns, gj = [1024] * n_acc, acc_n[0] // 1024
    else:
        tns, gj = acc_n, 1

    ins, plist = [], []
    for (a, b, d, c), tk in zip(pairs, tks):
        tn = tns[c]
        a_spec = ((tm, tk), lambda i, j, k: (i, k)) if d[0] == "n" else ((tk, tm), lambda i, j, k: (k, i))
        b_spec = ((tk, tn), lambda i, j, k: (k, j)) if d[1] == "n" else ((tn, tk), lambda i, j, k: (j, k))
        ins += [(a, *a_spec), (b, *b_spec)]
        plist.append((len(ins) - 2, len(ins) - 1, c, d))
    extras = [(r, (tm, r.shape[1]), lambda i, j, k: (i, 0)) for r in row_extras]
    extras += [(v, v.shape, lambda i, j, k: (0, 0)) for v in vec_extras]
    out_specs = []
    for n, dt in outs:
        bn = 1024 if (gj > 1) else n
        out_specs.append(((m, n), dt, (tm, bn), lambda i, j, k: (i, j)))
    if epilogue is None:
        epilogue = lambda accs, ex: accs
    grid = (m // tm, gj, gk)
    if n_outer:
        swap = lambda spec: spec[:-1] + ((lambda f: lambda j, i, k: f(i, j, k))(spec[-1]),)
        ins, extras, out_specs, grid = [swap(x) for x in ins], [swap(x) for x in extras], [swap(x) for x in out_specs], (gj, m // tm, gk)
    return _mm(name, grid, ins, plist, [(tm, tn) for tn in tns], out_specs, epilogue, extras, sums=sums)


def _rms(x, g):
    return x * lax.rsqrt(jnp.mean(x * x, axis=-1, keepdims=True) + RMS_EPS) * g


def _rms_bwd(x, g, dy):
    r = lax.rsqrt(jnp.mean(x * x, axis=-1, keepdims=True) + RMS_EPS)
    gy = dy * g
    dx = r * gy - x * (r * r * r) * jnp.mean(gy * x, axis=-1, keepdims=True)
    return dx, jnp.sum(dy * x * r, axis=0, keepdims=True)


def _sigmoid(x):
    return 0.5 * (1.0 + jnp.tanh(0.5 * x))


_GELU_C = math.sqrt(2.0 / math.pi)


def _gelu(x):
    return x * (0.5 * (1.0 + jnp.tanh(_GELU_C * (x + 0.044715 * (x * x * x)))))


def _gelu_grad(x):
    t = jnp.tanh(_GELU_C * (x + 0.044715 * (x * x * x)))
    return 0.5 * (1.0 + t) + 0.5 * x * (1.0 - t * t) * (_GELU_C * (1.0 + 3.0 * 0.044715 * (x * x)))


def _rows(name, row_ins, vec_ins, fn, row_outs, acc_outs=()):
    t = row_ins[0].shape[0]
    tm = min(t, ROW_TILE)
    nr, nv, no = len(row_ins), len(vec_ins), len(row_outs)

    def body(*refs):
        outs, incs = fn([r[...] for r in refs[:nr]], [r[...] for r in refs[nr:nr + nv]])
        for r, v in zip(refs[nr + nv:nr + nv + no], outs):
            r[...] = v.astype(r.dtype)
        i = pl.program_id(0)
        for r, v in zip(refs[nr + nv + no:], incs):
            @pl.when(i == 0)
            def _():
                r[...] = v

            @pl.when(i > 0)
            def _():
                r[...] += v

    in_specs = [pl.BlockSpec((tm, a.shape[1]), lambda i: (i, 0)) for a in row_ins]
    in_specs += [pl.BlockSpec(v.shape, lambda i, nd=v.ndim: (0,) * nd) for v in vec_ins]
    out_shape = [jax.ShapeDtypeStruct((t, c), dt) for c, dt in row_outs]
    out_shape += [jax.ShapeDtypeStruct(s, F32) for s in acc_outs]
    out_specs = [pl.BlockSpec((tm, c), lambda i: (i, 0)) for c, _ in row_outs]
    out_specs += [pl.BlockSpec(s, lambda i, nd=len(s): (0,) * nd) for s in acc_outs]
    return pl.pallas_call(body, out_shape=out_shape, grid=(t // tm,), in_specs=in_specs, out_specs=out_specs,
                          name=name, compiler_params=_params(("arbitrary",)))(*row_ins, *vec_ins)


def _rms_fwd(x, g):
    return _rows("rms_fwd", [x], [g], lambda r, v: ([_rms(r[0], v[0])], []), [(x.shape[1], BF16)])[0]


def _norm_out_bwd(name, f, d_out, g, scale):
    def fn(r, v):
        dx, dg = _rms_bwd(r[0], v[0], r[1] * scale)
        return [dx], [dg]

    c = f.shape[1]
    return _rows(name, [f, d_out], [g], fn, [(c, BF16)], [(1, c)])


def _loss_head(y, target):
    d = y.shape[1]

    def fn(r, v):
        e = r[0] - r[1]
        s = jnp.sum(jnp.sum(e * e, axis=1, keepdims=True), axis=0, keepdims=True) * (0.5 / d)
        return [e * (1.0 / d)], [jnp.broadcast_to(s, (1, 128))]

    dy, acc = _rows("loss_head", [y, target], [], fn, [(d, F32)], [(1, 128)])
    return acc[0, 0], dy


def _rope_bwd(dk, cos2, sin2):
    c = dk.shape[1]
    return _rows("rope_bwd", [dk, cos2, sin2], [], lambda r, v: ([r[0] * r[1], r[0] * r[2]], []),
                 [(c, BF16), (c, BF16)])


def _ffn_up(n, wg, wu):
    t, d = n.shape
    fs = wg.shape[-2]
    tm = min(t, ROW_TILE // 2)

    def body(n_ref, wg_ref, wu_ref, g_ref, u_ref, a_ref):
        x = n_ref[...]
        for s in range(N_SHARDS):
            g = lax.dot_general(x, wg_ref[s], _DN["nt"], preferred_element_type=F32)
            u = lax.dot_general(x, wu_ref[s], _DN["nt"], preferred_element_type=F32)
            g_ref[s] = g.astype(BF16)
            u_ref[s] = u.astype(BF16)
            a_ref[s] = (g * _sigmoid(g) * u).astype(BF16)

    hid = pl.BlockSpec((N_SHARDS, tm, fs), lambda i: (0, i, 0))
    whole = pl.BlockSpec(wg.shape, lambda i: (0, 0, 0))
    return pl.pallas_call(body, out_shape=[jax.ShapeDtypeStruct((N_SHARDS, t, fs), BF16)] * 3, grid=(t // tm,),
                          in_specs=[pl.BlockSpec((tm, d), lambda i: (i, 0)), whole, whole], out_specs=[hid] * 3,
                          name="ffn_up", compiler_params=_params(("parallel",)))(n, wg, wu)


def _down(name, a_in, w_in, gk, h, post_g, next_gs, scale, inner=0):
    t, d = h.shape
    tm = min(t, ROW_TILE)
    kn = next_gs.shape[0]

    def epi(accs, ex):
        f, hv, pg, ng = accs[0], ex[0], ex[1], ex[2]
        hn = hv + scale * _rms(f, pg)
        return [f, hn] + [_rms(hn, ng[q:q + 1]) for q in range(kn)]

    row = ((tm, d), lambda i, j, k: (i, 0))
    outs = [((t, d), F32, *row), ((t, d), F32, *row)] + [((t, d), BF16, *row)] * kn
    extras = [(h, *row), (post_g, (1, d), lambda i, j, k: (0, 0)), (next_gs, (kn, d), lambda i, j, k: (0, 0))]
    return _mm(name, (t // tm, 1, gk), [a_in, w_in], [(0, 1, 0, "nn")], [(tm, d)], outs, epi, extras, inner)


def _ffn_down(a, wd, h, post_g, next_gs):
    t, d = h.shape
    fs = a.shape[-1]
    tm = min(t, ROW_TILE)
    return _down("ffn_down", (a, (N_SHARDS, tm, fs), lambda i, _, k: (0, i, 0)),
                 (wd, (N_SHARDS, fs, d), lambda i, _, k: (0, 0, 0)), 1, h, post_g, next_gs, 0.5, N_SHARDS)


def _ffn_dact(f, d_out, post_g, wd, g, u):
    t, d = f.shape
    fs = g.shape[-1]
    tm = min(t, ROW_TILE // 2)

    def body(f_ref, do_ref, pg_ref, wd_ref, g_ref, u_ref, df_ref, dg_ref, du_ref, dpg_ref):
        dfv, dpg = _rms_bwd(f_ref[...], pg_ref[...], do_ref[...] * 0.5)
        dfb = dfv.astype(BF16)
        df_ref[...] = dfb
        i = pl.program_id(0)

        @pl.when(i == 0)
        def _():
            dpg_ref[...] = dpg

        @pl.when(i > 0)
        def _():
            dpg_ref[...] += dpg

        for s in range(N_SHARDS):
            da = lax.dot_general(dfb, wd_ref[s], _DN["nt"], preferred_element_type=F32)
            gv, uv = g_ref[s].astype(F32), u_ref[s].astype(F32)
            sg = _sigmoid(gv)
            dg_ref[s] = (da * uv * (sg * (1.0 + gv * (1.0 - sg)))).astype(BF16)
            du_ref[s] = (da * (gv * sg)).astype(BF16)

    row = pl.BlockSpec((tm, d), lambda i: (i, 0))
    hid = pl.BlockSpec((N_SHARDS, tm, fs), lambda i: (0, i, 0))
    vec = pl.BlockSpec((1, d), lambda i: (0, 0))
    hid_shape = jax.ShapeDtypeStruct((N_SHARDS, t, fs), BF16)
    return pl.pallas_call(
        body, out_shape=[jax.ShapeDtypeStruct((t, d), BF16), hid_shape, hid_shape, jax.ShapeDtypeStruct((1, d), F32)],
        grid=(t // tm,), in_specs=[row, row, vec, pl.BlockSpec(wd.shape, lambda i: (0, 0, 0)), hid, hid],
        out_specs=[row, hid, hid, vec], name="ffn_dact", compiler_params=_params(("arbitrary",)))(
            f, d_out, post_g, wd, g, u)


def _ffn_dn(dg, du, wg, wu, h, d_res, branches, then=None):
    _, t, fs = dg.shape
    d = wg.shape[-1]
    tm = min(t, ROW_TILE)
    nb = len(branches)
    a_spec = ((N_SHARDS, tm, fs), lambda i, _, k: (0, i, 0))
    w_spec = ((N_SHARDS, fs, d), lambda i, _, k: (0, 0, 0))
    row = ((tm, d), lambda i, _, k: (i, 0))
    vec = ((1, d), lambda i, _, k: (0, 0))

    def epi(accs, ex):
        hv, dh = ex[0], ex[1]
        dns, gs = [accs[0]] + ex[2:1 + nb], ex[1 + nb:]
        dgs = []
        for dn, g in zip(dns, gs[:nb]):
            dx, dgv = _rms_bwd(hv, g, dn)
            dh = dh + dx
            dgs.append(dgv)
        if then is None:
            return [dh] + dgs
        dm, dgm = _rms_bwd(ex[-1], gs[nb], dh * then[2])
        return [dh, dm] + dgs + [dgm]

    extras = [(h, *row), (d_res, *row)] + [(dn, *row) for _, dn in branches[1:]] + [(g, *vec) for g, _ in branches]
    outs = [((t, d), F32, *row)]
    if then is not None:
        extras += [(then[1], *vec), (then[0], *row)]
        outs.append(((t, d), BF16, *row))
    return _mm("ffn_dn", (t // tm, 1, 1), [(dg, *a_spec), (wg, *w_spec), (du, *a_spec), (wu, *w_spec)],
               [(0, 1, 0, "nn"), (2, 3, 0, "nn")], [(tm, d)], outs, epi, extras, inner=N_SHARDS,
               sums=[(1, d)] * (nb + (then is not None)))


def _ffn_dw_in(n, dg, du):
    _, t, fs = dg.shape
    d = n.shape[1]
    tk = min(t, K_TILE)
    a_spec = ((None, tk, fs), lambda s, _, k: (s, k, 0))
    o_spec = ((None, fs, d), lambda s, _, k: (s, 0, 0))
    outs = [((N_SHARDS, fs, d), BF16, *o_spec)] * 2
    return _mm("ffn_dw_in", (N_SHARDS, 1, t // tk), [(dg, *a_spec), (du, *a_spec), (n, (tk, d), lambda s, _, k: (k, 0))],
               [(0, 2, 0, "tn"), (1, 2, 1, "tn")], [(fs, d)] * 2, outs, lambda accs, ex: accs)


def _ffn_dw_down(a, df):
    _, t, fs = a.shape
    d = df.shape[1]
    tk = min(t, K_TILE)
    outs = [((N_SHARDS, fs, d), BF16, (None, fs, d), lambda s, _, k: (s, 0, 0))]
    return _mm("ffn_dw_down", (N_SHARDS, 1, t // tk),
               [(a, (None, tk, fs), lambda s, _, k: (s, k, 0)), (df, (tk, d), lambda s, _, k: (k, 0))],
               [(0, 1, 0, "tn")], [(fs, d)], outs, lambda accs, ex: accs)[0]


def _causal_weight(w):
    row = lax.broadcasted_iota(jnp.int32, (CHUNK, CHUNK), 0)
    col = lax.broadcasted_iota(jnp.int32, (CHUNK, CHUNK), 1)
    return row >= col, jnp.where(row >= col, w, 0.0).astype(BF16)


def _layer_norm(v, g, b):
    xc = v - jnp.mean(v, axis=-1, keepdims=True)
    rstd = lax.rsqrt(jnp.mean(xc * xc, axis=-1, keepdims=True) + LN_EPS)
    xhat = xc * rstd
    return xhat, rstd, xhat * g + b


def _sgu_specs(t, half, tm):
    return [pl.BlockSpec((tm, half), lambda i: (i, 0)), pl.BlockSpec((tm, half), lambda i: (i, 1))]


def _sgu_fwd(zp, ln_g, ln_b, w_s, bsb):
    t, half = zp.shape[0], zp.shape[1] // 2
    tm = min(t, 2 * CHUNK)

    def body(u_ref, v_ref, g_ref, b_ref, w_ref, bs_ref, o_ref):
        u = _gelu(u_ref[...])
        _, _, vn = _layer_norm(_gelu(v_ref[...]), g_ref[...], b_ref[...])
        vb = vn.astype(BF16)
        for g in range(GROUPS):
            _, wm = _causal_weight(w_ref[g])
            cols = slice(g * CHUNK, (g + 1) * CHUNK)
            for c in range(tm // CHUNK):
                rows = slice(c * CHUNK, (c + 1) * CHUNK)
                sv = jnp.dot(wm, vb[rows, cols], preferred_element_type=F32) + bs_ref[g]
                o_ref[rows, cols] = (u[rows, cols] * sv).astype(BF16)

    whole = lambda a: pl.BlockSpec(a.shape, lambda i, nd=a.ndim: (0,) * nd)
    return pl.pallas_call(
        body, out_shape=jax.ShapeDtypeStruct((t, half), BF16), grid=(t // tm,),
        in_specs=_sgu_specs(t, half, tm) + [whole(ln_g), whole(ln_b), whole(w_s), whole(bsb)],
        out_specs=pl.BlockSpec((tm, half), lambda i: (i, 0)), name="sgu_fwd",
        compiler_params=_params(("arbitrary",)))(zp, zp, ln_g, ln_b, w_s, bsb)


def _sgu_bwd(zp, d_uv, ln_g, ln_b, w_s, bsb):
    t, half = zp.shape[0], zp.shape[1] // 2
    tm = min(t, 2 * CHUNK)

    def body(u_ref, v_ref, d_ref, g_ref, b_ref, w_ref, bs_ref, dz_ref, dlg_ref, dlb_ref, dws_ref, dbs_ref, dvn_ref):
        i = pl.program_id(0)

        @pl.when(i == 0)
        def _():
            dlg_ref[...] = jnp.zeros_like(dlg_ref)
            dlb_ref[...] = jnp.zeros_like(dlb_ref)
            dws_ref[...] = jnp.zeros_like(dws_ref)
            dbs_ref[...] = jnp.zeros_like(dbs_ref)

        up, vp = u_ref[...], v_ref[...]
        u, gup = _gelu(up), _gelu_grad(up)
        xhat, rstd, vn = _layer_norm(_gelu(vp), g_ref[...], b_ref[...])
        vb = vn.astype(BF16)
        d = d_ref[...]
        for g in range(GROUPS):
            mask, wm = _causal_weight(w_ref[g])
            cols = slice(g * CHUNK, (g + 1) * CHUNK)
            for c in range(tm // CHUNK):
                rows = slice(c * CHUNK, (c + 1) * CHUNK)
                blk = vb[rows, cols]
                sv = jnp.dot(wm, blk, preferred_element_type=F32) + bs_ref[g]
                dblk = d[rows, cols]
                dz_ref[rows, cols] = (dblk * sv * gup[rows, cols]).astype(BF16)
                dsv = dblk * u[rows, cols]
                dsvb = dsv.astype(BF16)
                dvn_ref[rows, cols] = lax.dot_general(wm, dsvb, _DN["tn"], preferred_element_type=F32)
                dw = lax.dot_general(dsvb, blk, _DN["nt"], preferred_element_type=F32)
                dws_ref[g] += jnp.where(mask, dw, 0.0)
                dbs_ref[g] += jnp.sum(dsv, axis=1, keepdims=True)
        dvn = dvn_ref[...]
        dlg_ref[...] += jnp.sum(dvn * xhat, axis=0, keepdims=True)
        dlb_ref[...] += jnp.sum(dvn, axis=0, keepdims=True)
        dxh = dvn * g_ref[...]
        dv = rstd * (dxh - jnp.mean(dxh, axis=-1, keepdims=True)
                     - xhat * jnp.mean(dxh * xhat, axis=-1, keepdims=True))
        dz_ref[:, half:] = (dv * _gelu_grad(vp)).astype(BF16)

    whole = lambda a: pl.BlockSpec(a.shape, lambda i, nd=a.ndim: (0,) * nd)
    wshape = lambda s: pl.BlockSpec(s, lambda i, nd=len(s): (0,) * nd)
    out_shape = [jax.ShapeDtypeStruct((t, 2 * half), BF16), jax.ShapeDtypeStruct((1, half), F32),
                 jax.ShapeDtypeStruct((1, half), F32), jax.ShapeDtypeStruct(w_s.shape, F32),
                 jax.ShapeDtypeStruct((GROUPS, CHUNK, 1), F32)]
    return pl.pallas_call(
        body, out_shape=out_shape, grid=(t // tm,),
        in_specs=_sgu_specs(t, half, tm) + [pl.BlockSpec((tm, half), lambda i: (i, 0)), whole(ln_g), whole(ln_b),
                                            whole(w_s), whole(bsb)],
        out_specs=[pl.BlockSpec((tm, 2 * half), lambda i: (i, 0)), wshape((1, half)), wshape((1, half)),
                   wshape(w_s.shape), wshape((GROUPS, CHUNK, 1))],
        scratch_shapes=[pltpu.VMEM((tm, half), F32)], name="sgu_bwd",
        compiler_params=_params(("arbitrary",)))(zp, zp, d_uv, ln_g, ln_b, w_s, bsb)


_SCALE = (QK_NOPE + QK_ROPE) ** -0.5


def _attn_scores(qn, qr, kn, kr, i, tq, n):
    s = lax.dot_general(qn, kn, _DN["nt"], preferred_element_type=F32)
    s = (s + lax.dot_general(qr, kr, _DN["nt"], preferred_element_type=F32)) * _SCALE
    row = i * tq + lax.broadcasted_iota(jnp.int32, (tq, n), 0)
    col = lax.broadcasted_iota(jnp.int32, (tq, n), 1)
    return jnp.where(col <= row, s, NEG_INF)


def _attn_specs(seq):
    head = lambda b, h: (b, h)
    return dict(
        qn=pl.BlockSpec((seq, QK_NOPE), head),
        qr=pl.BlockSpec((None, seq, QK_ROPE), lambda b, h: (h, b, 0)),
        kr=pl.BlockSpec((seq, QK_ROPE), lambda b, h: (b, 0)),
        lse=pl.BlockSpec((None, seq, 1), lambda b, h: (h, b, 0)),
    )


def _attn_fwd(qn, qr, kn, v, kr, seq):
    t = qn.shape[0]
    tq = min(seq, 2 * CHUNK)
    sp = _attn_specs(seq)

    def body(qn_ref, qr_ref, kn_ref, v_ref, kr_ref, o_ref, lse_ref):
        for i in range(seq // tq):
            rows, n = slice(i * tq, (i + 1) * tq), (i + 1) * tq
            s = _attn_scores(qn_ref[rows, :], qr_ref[rows, :], kn_ref[0:n, :], kr_ref[0:n, :], i, tq, n)
            m = jnp.max(s, axis=-1, keepdims=True)
            p = jnp.exp(s - m)
            l = jnp.sum(p, axis=-1, keepdims=True)
            o_ref[rows, :] = jnp.dot((p / l).astype(BF16), v_ref[0:n, :], preferred_element_type=F32).astype(BF16)
            lse_ref[rows, :] = m + jnp.log(l)

    return pl.pallas_call(
        body, out_shape=[jax.ShapeDtypeStruct((t, N_HEADS * V_DIM), BF16), jax.ShapeDtypeStruct((N_HEADS, t, 1), F32)],
        grid=(t // seq, N_HEADS), in_specs=[sp["qn"], sp["qr"], sp["qn"], sp["qn"], sp["kr"]],
        out_specs=[sp["qn"], sp["lse"]], name="attn_fwd",
        compiler_params=_params(("parallel", "arbitrary")))(qn, qr, kn, v, kr)


def _attn_bwd(qn, qr, kn, v, kr, do, lse, cos2, sin2, seq):
    t = qn.shape[0]
    tq = min(seq, 2 * CHUNK)
    sp = _attn_specs(seq)

    def body(qn_ref, qr_ref, kn_ref, v_ref, kr_ref, do_ref, lse_ref, cos_ref, sin_ref,
             dqn_ref, dkn_ref, dv_ref, dqc_ref, dqs_ref, dkr_ref, dk_acc, dv_acc, dkr_acc):
        dk_acc[...] = jnp.zeros_like(dk_acc)
        dv_acc[...] = jnp.zeros_like(dv_acc)
        dkr_acc[...] = jnp.zeros_like(dkr_acc)
        for i in range(seq // tq):
            rows, n = slice(i * tq, (i + 1) * tq), (i + 1) * tq
            q_n, q_r, d_o = qn_ref[rows, :], qr_ref[rows, :], do_ref[rows, :]
            k_n, k_r = kn_ref[0:n, :], kr_ref[0:n, :]
            s = _attn_scores(q_n, q_r, k_n, k_r, i, tq, n)
            p = jnp.exp(s - lse_ref[rows, :])
            dp = lax.dot_general(d_o, v_ref[0:n, :], _DN["nt"], preferred_element_type=F32)
            ds = (p * (dp - jnp.sum(p * dp, axis=-1, keepdims=True)) * _SCALE).astype(BF16)
            dqn_ref[rows, :] = jnp.dot(ds, k_n, preferred_element_type=F32).astype(BF16)
            dqr = jnp.dot(ds, k_r, preferred_element_type=F32)
            dqc_ref[rows, :] = (dqr * cos_ref[rows, :]).astype(BF16)
            dqs_ref[rows, :] = (dqr * sin_ref[rows, :]).astype(BF16)
            dk_acc[0:n, :] += lax.dot_general(ds, q_n, _DN["tn"], preferred_element_type=F32)
            dkr_acc[0:n, :] += lax.dot_general(ds, q_r, _DN["tn"], preferred_element_type=F32)
            dv_acc[0:n, :] += lax.dot_general(p.astype(BF16), d_o, _DN["tn"], preferred_element_type=F32)
        dkn_ref[...] = dk_acc[...].astype(BF16)
        dv_ref[...] = dv_acc[...].astype(BF16)
        h = pl.program_id(1)

        @pl.when(h == 0)
        def _():
            dkr_ref[...] = dkr_acc[...]

        @pl.when(h > 0)
        def _():
            dkr_ref[...] += dkr_acc[...]

    wide = jax.ShapeDtypeStruct((t, N_HEADS * V_DIM), BF16)
    rope = jax.ShapeDtypeStruct((N_HEADS, t, QK_ROPE), BF16)
    krf = pl.BlockSpec((seq, QK_ROPE), lambda b, h: (b, 0))
    return pl.pallas_call(
        body, out_shape=[wide, wide, wide, rope, rope, jax.ShapeDtypeStruct((t, QK_ROPE), F32)],
        grid=(t // seq, N_HEADS),
        in_specs=[sp["qn"], sp["qr"], sp["qn"], sp["qn"], sp["kr"], sp["qn"], sp["lse"], krf, krf],
        out_specs=[sp["qn"], sp["qn"], sp["qn"], sp["qr"], sp["qr"], krf],
        scratch_shapes=[pltpu.VMEM((seq, QK_NOPE), F32), pltpu.VMEM((seq, V_DIM), F32), pltpu.VMEM((seq, QK_ROPE), F32)],
        name="attn_bwd", compiler_params=_params(("parallel", "arbitrary")))(qn, qr, kn, v, kr, do, lse, cos2, sin2)


def _q_rope(qn, w, w_rot, cos2, sin2):
    t, r = qn.shape
    nh, _, e = w.shape
    tm = min(t, ROW_TILE)

    def body(x_ref, w_ref, wr_ref, c_ref, s_ref, o_ref):
        x = x_ref[...]
        for h in range(nh):
            raw = jnp.dot(x, w_ref[h], preferred_element_type=F32)
            rot = jnp.dot(x, wr_ref[h], preferred_element_type=F32)
            o_ref[h] = (raw * c_ref[...] + rot * s_ref[...]).astype(BF16)

    whole = pl.BlockSpec(w.shape, lambda i: (0, 0, 0))
    rows = pl.BlockSpec((tm, e), lambda i: (i, 0))
    return pl.pallas_call(body, out_shape=jax.ShapeDtypeStruct((nh, t, e), BF16), grid=(t // tm,),
                          in_specs=[pl.BlockSpec((tm, r), lambda i: (i, 0)), whole, whole, rows, rows],
                          out_specs=pl.BlockSpec((nh, tm, e), lambda i: (0, i, 0)), name="q_rope",
                          compiler_params=_params(("parallel",)))(qn, w, w_rot, cos2, sin2)


def _q_rope_dw(qn, dq_c, dq_s):
    t, r = qn.shape
    nh, _, e = dq_c.shape
    tk = min(t, K_TILE)

    def body(x_ref, c_ref, s_ref, gc_ref, gs_ref):
        k = pl.program_id(0)
        x = x_ref[...]
        for h in range(nh):
            pc = lax.dot_general(c_ref[h], x, _DN["tn"], preferred_element_type=F32)
            ps = lax.dot_general(s_ref[h], x, _DN["tn"], preferred_element_type=F32)

            @pl.when(k == 0)
            def _():
                gc_ref[h] = pc
                gs_ref[h] = ps

            @pl.when(k > 0)
            def _():
                gc_ref[h] += pc
                gs_ref[h] += ps

    heads = pl.BlockSpec((nh, tk, e), lambda k: (0, k, 0))
    out = pl.BlockSpec((nh, e, r), lambda k: (0, 0, 0))
    return pl.pallas_call(body, out_shape=[jax.ShapeDtypeStruct((nh, e, r), F32)] * 2, grid=(t // tk,),
                          in_specs=[pl.BlockSpec((tk, r), lambda k: (k, 0)), heads, heads], out_specs=[out, out],
                          name="q_rope_dw", compiler_params=_params(("arbitrary",)))(qn, dq_c, dq_s)


def _row_tile(rows, cols, row_mult=8):
    cap = max(row_mult, (1 << 18) // cols)
    best = rows
    for tr in range(row_mult, min(rows, cap) + 1, row_mult):
        if rows % tr == 0:
            best = tr
    return best if rows > cap else rows


def _adamw_math(w, g, m, v):
    mv = ADAM_B1 * m + (1.0 - ADAM_B1) * g
    vv = ADAM_B2 * v + (1.0 - ADAM_B2) * (g * g)
    m_hat = mv / (1.0 - ADAM_B1 ** ADAM_STEP)
    v_hat = vv / (1.0 - ADAM_B2 ** ADAM_STEP)
    return -ADAM_LR * (m_hat / (jnp.sqrt(v_hat) + ADAM_EPS) + ADAM_WD * w), mv, vv


def _adamw(w, g, m, v):
    shape = w.shape
    c = shape[-1]
    r = w.size // c
    tr = _row_tile(r, c)

    def body(w_ref, g_ref, m_ref, v_ref, d_ref, nm_ref, nv_ref):
        d_ref[...], nm_ref[...], nv_ref[...] = _adamw_math(w_ref[...], g_ref[...], m_ref[...], v_ref[...])

    spec = pl.BlockSpec((tr, c), lambda i: (i, 0))
    outs = pl.pallas_call(body, out_shape=[jax.ShapeDtypeStruct((r, c), F32)] * 3, grid=(r // tr,),
                          in_specs=[spec] * 4, out_specs=[spec] * 3, name="adamw",
                          compiler_params=_params(("parallel",)))(*[a.reshape(r, c) for a in (w, g, m, v)])
    return [o.reshape(shape) for o in outs]


def _adamw_halves(w, m, v, l, j, own, recv, core, prev):
    nl, nj, rows, c = w.shape
    r = rows // 2
    tr = _row_tile(r, c)
    n_prev = 0 if prev is None else 4

    def body(core_ref, w_ref, own_ref, recv_ref, m_ref, v_ref, *rest):
        g_ref, d_ref, nm_ref, nv_ref = rest[n_prev:]
        g = jnp.where(pl.program_id(0) == core_ref[0], own_ref[...], recv_ref[...])
        g_ref[...] = g
        d_ref[...], nm_ref[...], nv_ref[...] = _adamw_math(w_ref[...], g, m_ref[...], v_ref[...])

    nb = r // tr
    slab = pl.BlockSpec((None, None, tr, c), lambda h, i, cr: (l, j, h * nb + i, 0))
    half = pl.BlockSpec((tr, c), lambda h, i, cr: (i, 0))
    grid_spec = pltpu.PrefetchScalarGridSpec(num_scalar_prefetch=1, grid=(2, nb),
                                             in_specs=[slab, half, half, slab, slab] + [_ANY] * n_prev,
                                             out_specs=[slab] * 4)
    return pl.pallas_call(body, out_shape=[jax.ShapeDtypeStruct(w.shape, F32)] * 4, grid_spec=grid_spec,
                          input_output_aliases={6 + q: q for q in range(n_prev)}, name="adamw_halves",
                          compiler_params=_params(("parallel",) * 2))(core, w, own, recv, m, v, *(prev or ()))


def _place():
    x, y, c = lax.axis_index("x"), lax.axis_index("y"), lax.axis_index("c")
    return x, y, c, [(1 - x, y), (x, 1 - y), (1 - x, 1 - y)]


def _dma_sems(*counts):
    return [pltpu.SemaphoreType.DMA((n,)) for n in counts]


def _all_gather(bufs, collective_id, name):
    n = len(bufs)

    def body(*refs):
        ins, outs = refs[:n], refs[n:2 * n]
        send, recv, fsend, frecv, osend, orecv = refs[2 * n:]
        x, y, c, _ = _place()
        xn, yn, sib = (1 - x, y, c), (x, 1 - y, c), (x, y, 1 - c)
        k, kx, ky, kd = 2 * x + y, 2 * (1 - x) + y, 2 * x + 1 - y, 2 * (1 - x) + 1 - y
        _handshake([xn, yn, sib])

        def copy(src, dst, sems, i, to):
            return pltpu.make_async_remote_copy(src, dst, sems[0].at[i], sems[1].at[i], device_id=to, device_id_type=_MESH)

        ici, d2d, own_s = (send, recv), (fsend, frecv), (osend, orecv)
        started = [copy(ins[b], outs[b].at[k], own_s, b, sib) for b in range(n)]
        for first in (True, False):
            for b in range(n):
                mine = outs[b].at[k, c]
                if first:
                    started += [copy(ins[b].at[c, 0], mine.at[0], ici, 6 * b, xn), copy(ins[b].at[c, 1], mine.at[1], ici, 6 * b + 1, yn)]
                else:
                    started += [copy(ins[b].at[c, 1], mine.at[1], ici, 6 * b + 2, xn), copy(ins[b].at[c, 0], mine.at[0], ici, 6 * b + 3, yn)]
        for cp in started:
            cp.start()
        passed = []
        for b in range(n):
            for i, (src_chip, q, to) in enumerate([(kx, 0, yn), (ky, 1, xn)]):
                piece = outs[b].at[src_chip, c, q]
                copy(piece, piece, ici, 6 * b + i, to).wait_recv()
                cp = copy(piece, piece, ici, 6 * b + 4 + i, to)
                cp.start()
                passed.append(cp)
        for b in range(n):
            for i, (src_chip, q) in enumerate([(kx, 1), (ky, 0)]):
                piece = outs[b].at[src_chip, c, q]
                copy(piece, piece, ici, 6 * b + 2 + i, xn).wait_recv()
                half = outs[b].at[src_chip, c]
                cp = copy(half, half, d2d, 3 * b + i, sib)
                cp.start()
                passed.append(cp)
        for b in range(n):
            for i, q in enumerate([0, 1]):
                piece = outs[b].at[kd, c, q]
                copy(piece, piece, ici, 6 * b + 4 + i, xn).wait_recv()
            half = outs[b].at[kd, c]
            cp = copy(half, half, d2d, 3 * b + 2, sib)
            cp.start()
            passed.append(cp)
        for b in range(n):
            for i, src_chip in enumerate([kx, ky, kd]):
                half = outs[b].at[src_chip, 1 - c]
                copy(half, half, d2d, 3 * b + i, sib).wait_recv()
        for cp in started[n:] + passed:
            cp.wait_send()
        for cp in started[:n]:
            cp.wait()

    return _sequencer(body, [jax.ShapeDtypeStruct((N_SHARDS,) + b.shape, b.dtype) for b in bufs],
                      _dma_sems(6 * n, 6 * n, 3 * n, 3 * n, n, n), collective_id, name, bufs)


def _sequencer(body, out_type, sems, collective_id, name, args):
    return pl.kernel(body, out_type=out_type, mesh=plsc.ScalarSubcoreMesh(axis_name="sequencer", num_cores=1),
                     scratch_types=sems, compiler_params=pltpu.CompilerParams(collective_id=collective_id),
                     name=name)(*args)


def _handshake(peers):
    barrier = pltpu.get_barrier_semaphore()
    for peer in peers:
        pl.semaphore_signal(barrier, inc=1, device_id=peer, device_id_type=_MESH)
    pl.semaphore_wait(barrier, len(peers))


def _swap_halves(parts, collective_id, name):
    n = len(parts)

    def body(*refs):
        ins, outs = refs[:n], refs[n:2 * n]
        send, recv = refs[2 * n:]
        x, y, c, _ = _place()
        _handshake([(x, y, 1 - c)])
        cps = [pltpu.make_async_remote_copy(ins[b].at[:, pl.ds(1 - c, 1)], outs[b], send.at[b], recv.at[b],
                                            device_id=(x, y, 1 - c), device_id_type=_MESH) for b in range(n)]
        for cp in cps:
            cp.start()
        for cp in cps:
            cp.wait()

    return _sequencer(body, [jax.ShapeDtypeStruct((N_SHARDS, 1) + p.shape[2:], p.dtype) for p in parts],
                      _dma_sems(n, n), collective_id, name, parts)


def _by_shape(fn, first, second, scalar):
    out, groups = [None] * len(first), {}
    for i, p in enumerate(first):
        groups.setdefault(p.shape, []).append(i)
    for idx in groups.values():
        for i, r in zip(idx, fn([first[i] for i in idx], [second[i] for i in idx], scalar)):
            out[i] = r
    return out


def _add_half(parts, others, core):
    n = len(parts)
    _, _, r, c = parts[0].shape
    tr = _row_tile(r, c, 16)

    def body(core_ref, *refs):
        for q in range(n):
            refs[2 * n + q][...] = (refs[q][...].astype(F32) + refs[n + q][...].astype(F32)).astype(BF16)

    grid_spec = pltpu.PrefetchScalarGridSpec(
        num_scalar_prefetch=1, grid=(N_SHARDS, r // tr),
        in_specs=[pl.BlockSpec((None, None, tr, c), lambda k, i, cr: (k, cr[0], i, 0))] * n
        + [pl.BlockSpec((None, None, tr, c), lambda k, i, cr: (k, 0, i, 0))] * n,
        out_specs=[pl.BlockSpec((None, tr, c), lambda k, i, cr: (k, i, 0))] * n)
    return pl.pallas_call(body, out_shape=[jax.ShapeDtypeStruct((N_SHARDS, r, c), BF16)] * n, grid_spec=grid_spec,
                          name="grad_add_half", compiler_params=_params(("parallel", "parallel")))(core, *parts, *others)


def _scatter_chips(parts, collective_id, name):
    n = len(parts)

    def body(*refs):
        ins, outs = refs[:n], refs[n:2 * n]
        send, recv = refs[2 * n:]
        x, y, c, chips = _place()
        k = 2 * x + y
        _handshake([(px, py, c) for px, py in chips])
        started = []
        for b in range(n):
            for j, (px, py) in enumerate(chips):
                cp = pltpu.make_async_remote_copy(ins[b].at[2 * px + py], outs[b].at[k], send.at[3 * b + j],
                                                  recv.at[3 * b + j], device_id=(px, py, c), device_id_type=_MESH)
                cp.start()
                started.append(cp)
        for b in range(n):
            for j, (px, py) in enumerate(chips):
                got = outs[b].at[2 * px + py]
                pltpu.make_async_remote_copy(got, got, send.at[3 * b + j], recv.at[3 * b + j],
                                             device_id=(px, py, c), device_id_type=_MESH).wait_recv()
        for cp in started:
            cp.wait_send()

    return _sequencer(body, [jax.ShapeDtypeStruct(p.shape, p.dtype) for p in parts], _dma_sems(3 * n, 3 * n),
                      collective_id, name, parts)


def _sum_slots(slots, mine, chip):
    n = len(slots)
    _, r, c = slots[0].shape
    tr = _row_tile(r, c, 16)

    def body(chip_ref, *refs):
        for q in range(n):
            own = refs[5 * q + 4][...].astype(F32)
            v = [jnp.where(chip_ref[0] == s, own, refs[5 * q + s][...].astype(F32)) for s in range(N_SHARDS)]
            refs[5 * n + q][...] = ((v[0] + v[1]) + v[2]) + v[3]

    def slot_spec(s):
        return pl.BlockSpec((None, tr, c), lambda i, kr: (jnp.where(kr[0] == s, (s + 1) % N_SHARDS, s), i, 0))

    per_buffer = [slot_spec(s) for s in range(N_SHARDS)] + [pl.BlockSpec((None, tr, c), lambda i, kr: (kr[0], i, 0))]
    grid_spec = pltpu.PrefetchScalarGridSpec(num_scalar_prefetch=1, grid=(r // tr,), in_specs=per_buffer * n,
                                             out_specs=[pl.BlockSpec((tr, c), lambda i, kr: (i, 0))] * n)
    args = [a for sl, mn in zip(slots, mine) for a in (sl, sl, sl, sl, mn)]
    return pl.pallas_call(body, out_shape=[jax.ShapeDtypeStruct((r, c), F32)] * n, grid_spec=grid_spec,
                          name="grad_sum_slots", compiler_params=_params(("parallel",)))(chip, *args)


def _join_halves(halves, collective_id, name):
    n = len(halves)

    def body(*refs):
        ins, outs = refs[:n], refs[n:2 * n]
        send, recv = refs[2 * n:]
        x, y, c, _ = _place()
        _handshake([(x, y, 1 - c)])
        cps = [pltpu.make_async_remote_copy(ins[b], outs[b], send.at[b], recv.at[b], device_id=(x, y, 1 - c),
                                            device_id_type=_MESH) for b in range(n)]
        for cp in cps:
            cp.start()
        for cp in cps:
            cp.wait()

    return _sequencer(body, [jax.ShapeDtypeStruct(h.shape, F32) for h in halves], _dma_sems(n, n), collective_id,
                      name, halves)


def _gather_rows(buf, start, rows):
    def body(in_ref, out_ref, send, recv, lsem):
        x, y, c, chips = _place()
        k = 2 * x + y
        src = in_ref.at[pl.ds(start, rows)]
        local = pltpu.make_async_remote_copy(src, out_ref.at[k], lsem.at[0], lsem.at[1], device_id=(x, y, 1 - c),
                                             device_id_type=_MESH)
        local.start()
        cps = [pltpu.make_async_remote_copy(src, out_ref.at[k], send.at[j], recv.at[j], device_id=(px, py, c),
                                            device_id_type=_MESH) for j, (px, py) in enumerate(chips)]
        for cp in cps:
            cp.start()
        for j, (px, py) in enumerate(chips):
            got = out_ref.at[2 * px + py]
            pltpu.make_async_remote_copy(got, got, send.at[j], recv.at[j], device_id=(px, py, c),
                                         device_id_type=_MESH).wait_recv()
        for cp in cps:
            cp.wait_send()
        local.wait()

    return pl.pallas_call(body, out_shape=jax.ShapeDtypeStruct((N_SHARDS, rows, buf.shape[1]), F32),
                          in_specs=[_ANY], out_specs=_ANY, scratch_shapes=_dma_sems(3, 3, 2),
                          name="gather_replicated_grads")(buf)


def _all_sum(vec):
    r, c = vec.shape
    n_dev = 2 * N_SHARDS

    def body(in_ref, out_ref, slots, send, recv):
        x, y, cc, _ = _place()
        flip = lambda v, bit: 1 - v if bit else v
        peers = [(flip(x, (q >> 2) & 1), flip(y, (q >> 1) & 1), flip(cc, q & 1)) for q in range(1, n_dev)]
        index = lambda p: 4 * p[0] + 2 * p[1] + p[2]
        slots[index((x, y, cc))] = in_ref[...]
        cps = [pltpu.make_async_remote_copy(in_ref, slots.at[index((x, y, cc))], send.at[q], recv.at[q], device_id=p,
                                            device_id_type=_MESH) for q, p in enumerate(peers)]
        for cp in cps:
            cp.start()
        for q, p in enumerate(peers):
            got = slots.at[index(p)]
            pltpu.make_async_remote_copy(got, got, send.at[q], recv.at[q], device_id=p, device_id_type=_MESH).wait_recv()
        for cp in cps:
            cp.wait_send()
        acc = slots[0]
        for s in range(1, n_dev):
            acc = acc + slots[s]
        out_ref[...] = acc

    vmem = pl.BlockSpec(memory_space=pltpu.VMEM)
    return pl.pallas_call(body, out_shape=jax.ShapeDtypeStruct((r, c), F32), in_specs=[vmem], out_specs=vmem,
                          scratch_shapes=[pltpu.VMEM((n_dev, r, c), F32)] + _dma_sems(n_dev - 1, n_dev - 1),
                          name="sum_small_grads")(vec)


def _not_before(value, other):
    return lax.optimization_barrier((value, other))[0]


def _round_up(n, m):
    return -(-n // m) * m


def _pack_flat(vecs, rows, width, dtype):
    flat = jnp.concatenate([v.reshape(-1).astype(dtype) for v in vecs])
    return jnp.pad(flat, (0, rows * width - flat.size)).reshape(rows, width)


def _split_flat(flat, shapes):
    out, off = [], 0
    for s in shapes:
        n = math.prod(s)
        out.append(flat[off:off + n].reshape(s))
        off += n
    return out


def _merge_shards(arr4, axis):
    a = jnp.moveaxis(arr4, 0, axis)
    s = list(a.shape)
    return a.reshape(s[:axis] + [s[axis] * s[axis + 1]] + s[axis + 2:])


def _split_shards(full, axis):
    s = list(full.shape)
    a = full.reshape(s[:axis] + [N_SHARDS, s[axis] // N_SHARDS] + s[axis + 1:])
    return jnp.moveaxis(a, axis, 0).reshape(N_SHARDS, -1)


def _rot_cols(w):
    half = w.shape[-1] // 2
    return jnp.concatenate([-w[..., half:], w[..., :half]], axis=-1)


def _unrot_cols(dw):
    half = dw.shape[-1] // 2
    return jnp.concatenate([dw[..., half:], -dw[..., :half]], axis=-1)


def kernel(x, positions, ffn_pre_g, ffn_post_g, ffn_w_gate, ffn_w_up, ffn_w_down, mix_pre_g, mix_post_g, gmlp_w_in, gmlp_ln_g, gmlp_ln_b, gmlp_w_s, gmlp_b_s, gmlp_w_out, kv_norm_g, w_dkv, kv_a_norm_g, w_ukv, mla_w_dq, mla_q_norm_g, mla_w_uq, mla_w_o, loss_target, m_ffn_pre_g, m_ffn_post_g, m_ffn_w_gate, m_ffn_w_up, m_ffn_w_down, m_mix_pre_g, m_mix_post_g, m_gmlp_w_in, m_gmlp_ln_g, m_gmlp_ln_b, m_gmlp_w_s, m_gmlp_b_s, m_gmlp_w_out, m_kv_norm_g, m_w_dkv, m_kv_a_norm_g, m_w_ukv, m_mla_w_dq, m_mla_q_norm_g, m_mla_w_uq, m_mla_w_o, v_ffn_pre_g, v_ffn_post_g, v_ffn_w_gate, v_ffn_w_up, v_ffn_w_down, v_mix_pre_g, v_mix_post_g, v_gmlp_w_in, v_gmlp_ln_g, v_gmlp_ln_b, v_gmlp_w_s, v_gmlp_b_s, v_gmlp_w_out, v_kv_norm_g, v_w_dkv, v_kv_a_norm_g, v_w_ukv, v_mla_w_dq, v_mla_q_norm_g, v_mla_w_uq, v_mla_w_o):
    names = ["ffn_pre_g", "ffn_post_g", "ffn_w_gate", "ffn_w_up", "ffn_w_down", "mix_pre_g", "mix_post_g", "gmlp_w_in",
             "gmlp_ln_g", "gmlp_ln_b", "gmlp_w_s", "gmlp_b_s", "gmlp_w_out", "kv_norm_g", "w_dkv", "kv_a_norm_g", "w_ukv",
             "mla_w_dq", "mla_q_norm_g", "mla_w_uq", "mla_w_o"]
    env = locals()
    w = {n: env[n] for n in names}
    mom = {n: env["m_" + n] for n in names}
    var = {n: env["v_" + n] for n in names}

    bsz, seq, d = x.shape
    t = bsz * seq
    core = lax.axis_index("c").astype(jnp.int32).reshape(1)

    mats = [("gmlp_w_in", 2), ("gmlp_w_out", 1), ("w_dkv", 0), ("w_ukv", 1), ("mla_w_dq", 1), ("mla_w_uq", 2),
            ("mla_w_o", 1)]
    vecs = [("ffn_pre_g", 2), ("ffn_post_g", 2), ("gmlp_ln_g", 1), ("gmlp_ln_b", 1)]
    replicated = ["mix_pre_g", "mix_post_g", "gmlp_w_s", "gmlp_b_s", "kv_norm_g", "kv_a_norm_g", "mla_q_norm_g"]
    n_mats = sum(w[n].size for n, _ in mats)
    n_vecs = sum(w[n].size for n, _ in vecs)
    mat_rows = _round_up(-(-n_mats // PACK_WIDTH), 64)
    vec_rows = _round_up(-(-n_vecs // 128), 32)
    mat_pack = _pack_flat([w[n] for n, _ in mats], mat_rows, PACK_WIDTH, BF16).reshape(2, 2, mat_rows // 4, PACK_WIDTH)
    vec_pack = _pack_flat([w[n] for n, _ in vecs], vec_rows, 128, F32).reshape(2, 2, vec_rows // 4, 128)
    ffn_names = ("ffn_w_gate", "ffn_w_up", "ffn_w_down")

    def oriented(a, name):
        return a if name == "ffn_w_down" else jnp.swapaxes(a, 2, 3)

    lj = [(l, j) for l in range(2) for j in range(2)]
    plan = [((0, 0), (0, 1), [vec_pack], None), ((0, 0), (2,), [mat_pack], 0), ((0, 1), (0, 1, 2), [], 0),
            ((1, 0), (0, 1, 2), [], 0), ((1, 1), (0, 1, 2), [], 0)]
    ffn_w = {k: [None] * 3 for k in lj}
    landed = []
    for q, ((l, j), which, riders, after) in enumerate(plan):
        shards = [oriented(w[ffn_names[i]], ffn_names[i])[l, j].astype(BF16) for i in which]
        bufs = [s.reshape(2, 2, s.shape[0] // 4, s.shape[1]) for s in shards] + riders
        if after is not None:
            bufs = _not_before(bufs, landed[after])
        got = _all_gather(bufs, q + 1, f"gather_weights_{q}")
        landed.append(got[-1])
        for i, g, s in zip(which, got, shards):
            ffn_w[(l, j)][i] = g.reshape((N_SHARDS,) + s.shape)
        if riders and q == 0:
            vec_all = got[-1]
        if riders and q == 1:
            mat_all = got[-1]

    def unpack(packed, entries):
        flat4, off, out = packed.reshape(N_SHARDS, -1), 0, {}
        for n, ax in entries:
            out[n] = _merge_shards(flat4[:, off:off + w[n].size].reshape((N_SHARDS,) + w[n].shape), ax)
            off += w[n].size
        return out

    full = unpack(vec_all, vecs)
    ln_g, ln_b = full["gmlp_ln_g"], full["gmlp_ln_b"]
    pre_g, post_g = full["ffn_pre_g"], full["ffn_post_g"]
    w_s = w["gmlp_w_s"][0]
    bsb = w["gmlp_b_s"][0][:, :, None]
    row = lambda v: v.reshape(1, -1)

    inv_freq = ROPE_THETA ** (-jnp.arange(0, QK_ROPE, 2, dtype=F32) / QK_ROPE)
    ang = positions.astype(F32).reshape(t, 1) * inv_freq
    cos2 = jnp.concatenate([jnp.cos(ang)] * 2, axis=-1)
    sin2 = jnp.concatenate([jnp.sin(ang)] * 2, axis=-1)

    h0 = x.reshape(t, d)
    saved = {}

    def ffn_fwd(l, j, h, n, next_gs):
        wg, wu, wd = ffn_w[(l, j)]
        g, u, a = _ffn_up(n, wg, wu)
        f, h_new, *n_next = _ffn_down(a, wd, h, row(post_g[l, j]), next_gs)
        saved[("ffn", l, j)] = (h, n, g, u, a, f)
        return h_new, n_next

    n0 = _rms_fwd(h0, row(pre_g[0, 0]))
    h1, (n1,) = ffn_fwd(0, 0, h0, n0, row(w["mix_pre_g"][0]))

    full.update(unpack(_not_before(mat_all, h1), mats))
    w_in, w_out = full["gmlp_w_in"][0], full["gmlp_w_out"][0]
    w_c, w_kr = full["w_dkv"][:, :KV_RANK], full["w_dkv"][:, KV_RANK:]
    w_kr_rot = _rot_cols(w_kr)
    ukv = full["w_ukv"].reshape(KV_RANK, N_HEADS, 2, QK_NOPE)
    w_k, w_v = ukv[:, :, 0].reshape(KV_RANK, -1), ukv[:, :, 1].reshape(KV_RANK, -1)
    w_dq, w_o = full["mla_w_dq"][0], full["mla_w_o"][0]
    q_rank = w_dq.shape[1]
    uq = full["mla_w_uq"][0].reshape(q_rank, N_HEADS, QK_NOPE + QK_ROPE)
    w_qn = uq[:, :, :QK_NOPE].reshape(q_rank, -1)
    w_qr = uq[:, :, QK_NOPE:].transpose(1, 0, 2)
    w_qr_rot = _rot_cols(w_qr)

    zp = _mm2d("gmlp_in", [(n1, w_in, "nn", 0)], [(w_in.shape[1], F32)], n_outer=True)[0]
    uv = _sgu_fwd(zp, ln_g, ln_b, w_s, bsb)
    half = uv.shape[1]
    tm = min(t, ROW_TILE)
    m0, h2, n2 = _down("gmlp_out", (uv, (tm, 512), lambda i, _, k: (i, k)), (w_out, (512, d), lambda i, _, k: (k, 0)),
                       half // 512, h1, row(w["mix_post_g"][0]), row(pre_g[0, 1]), 1.0)
    h3, (n3kv, n3) = ffn_fwd(0, 1, h2, n2, jnp.stack([w["kv_norm_g"], pre_g[1, 0]]))

    def kv_epi(accs, ex):
        c_raw = accs[0]
        return [c_raw, _rms(c_raw, ex[2]), accs[1] * ex[0] + accs[2] * ex[1]]

    c_raw, c_n, k_r = _mm2d("kv_down", [(n3kv, w_c, "nn", 0), (n3kv, w_kr, "nn", 1), (n3kv, w_kr_rot, "nn", 2)],
                            [(KV_RANK, F32), (KV_RANK, BF16), (QK_ROPE, BF16)], kv_epi, [cos2, sin2],
                            [row(w["kv_a_norm_g"])])
    k_n, v_h = _mm2d("kv_up", [(c_n, w_k, "nn", 0), (c_n, w_v, "nn", 1)], [(w_k.shape[1], BF16), (w_v.shape[1], BF16)])

    h4, (n4,) = ffn_fwd(1, 0, h3, n3, row(w["mix_pre_g"][1]))
    qd, qn = _mm2d("q_down", [(n4, w_dq, "nn", 0)], [(q_rank, F32), (q_rank, BF16)],
                   lambda accs, ex: [accs[0], _rms(accs[0], ex[0])], [], [row(w["mla_q_norm_g"][0])])
    q_n = _mm2d("q_up", [(qn, w_qn, "nn", 0)], [(w_qn.shape[1], BF16)])[0]
    q_r = _q_rope(qn, w_qr, w_qr_rot, cos2, sin2)
    o, lse = _attn_fwd(q_n, q_r, k_n, v_h, k_r, seq)
    m1, h5, n5 = _down("attn_out", (o, (tm, 512), lambda i, _, k: (i, k)), (w_o, (512, d), lambda i, _, k: (k, 0)),
                       o.shape[1] // 512, h4, row(w["mix_post_g"][1]), row(pre_g[1, 1]), 1.0)
    y, _ = ffn_fwd(1, 1, h5, n5, row(pre_g[1, 1]))

    loss_part, dy = _loss_head(y, loss_target.reshape(t, d))
    loss = lax.psum(loss_part, ("x", "y", "c"))

    chip = (2 * lax.axis_index("x") + lax.axis_index("y")).astype(jnp.int32).reshape(1)
    rs = {}

    def rs_launch(gid, parts):
        rs[gid] = {"parts": parts, "others": _swap_halves(parts, 7 + gid, f"grad_swap_{gid}")}

    def rs_mid(gid, after):
        r = rs[gid]
        parts, others = _not_before((r["parts"], r["others"]), after)
        r["chip"] = _by_shape(_add_half, parts, others, core)
        r["slots"] = _scatter_chips(r["chip"], 12 + gid, f"grad_scatter_{gid}")
        return r["chip"]

    def rs_end(gid, after):
        r = rs[gid]
        slots, mine = _not_before((r["slots"], r["chip"]), after)
        r["own"] = _by_shape(_sum_slots, slots, mine, chip)
        r["recv"] = _join_halves(r["own"], 17 + gid, f"grad_join_{gid}")
        return r["own"]

    d_pre, d_post = {}, {}

    def ffn_bwd(l, j, gid, dh_out, extra=(), then=None):
        h, n, g, u, a, f = saved[("ffn", l, j)]
        wg, wu, wd = ffn_w[(l, j)]
        df, dg, du, d_post[(l, j)] = _ffn_dact(f, dh_out, row(post_g[l, j]), wd, g, u)
        dwd = _ffn_dw_down(a, df)
        dwg, dwu = _ffn_dw_in(n, dg, du)
        parts = [p.reshape(N_SHARDS, 2, p.shape[1] // 2, p.shape[2]) for p in (dwg, dwu, dwd)]
        rs_launch(gid, parts)
        dg, du = _not_before((dg, du), parts)
        res = _ffn_dn(dg, du, wg, wu, h, dh_out, [(row(pre_g[l, j]), None)] + list(extra),
                      None if then is None else (then[0], then[1], 1.0))
        if then is None:
            dh, d_pre[(l, j)], *rest = res
            return dh, rest
        dh, dm, d_pre[(l, j)], *rest = res
        return dh, [dm, rest[-1]] + rest[:-1]

    dh5, (dm1, g_mix_post1) = ffn_bwd(1, 1, 0, dy, then=(m1, row(w["mix_post_g"][1])))
    dh5, dm1 = _not_before((dh5, dm1), rs_mid(0, dh5))

    do = _mm2d("attn_out_dx", [(dm1, w_o, "nt", 0)], [(w_o.shape[0], BF16)])[0]
    g_w_o = _mm2d("attn_out_dw", [(o, dm1, "tn", 0)], [(d, BF16)])[0]
    dq_n, dk_n, dv_h, dq_c, dq_s, dk_r = _attn_bwd(q_n, q_r, k_n, v_h, k_r, do, lse, cos2, sin2, seq)
    dqn = _mm2d("q_up_dx", [(dq_n, w_qn, "nt", 0)], [(q_rank, F32)])[0]
    heads_x = ((N_HEADS, tm, QK_ROPE), lambda i, j, k: (0, i, 0))
    heads_w = ((N_HEADS, q_rank, QK_ROPE), lambda i, j, k: (0, 0, 0))
    q_row = ((tm, q_rank), lambda i, j, k: (i, 0))
    dqn = _mm("q_rope_dx", (t // tm, 1, 1), [(dq_c, *heads_x), (w_qr, *heads_w), (dq_s, *heads_x), (w_qr_rot, *heads_w)],
              [(0, 1, 0, "nt"), (2, 3, 0, "nt")], [(tm, q_rank)], [((t, q_rank), F32, *q_row)],
              lambda accs, ex: [accs[0] + ex[0]], [(dqn, *q_row)], inner=N_HEADS)[0]
    g_qn = _mm2d("q_up_dw", [(qn, dq_n, "tn", 0)], [(w_qn.shape[1], F32)])[0]
    g_qr, g_qr_rot = [g.transpose(0, 2, 1) for g in _q_rope_dw(qn, dq_c, dq_s)]
    dqd, g_q_norm = _norm_out_bwd("q_norm_bwd", qd, dqn, row(w["mla_q_norm_g"][0]), 1.0)

    def pre_norm_bwd(accs, ex):
        dx, dgain = _rms_bwd(ex[0], ex[2], accs[0])
        return [ex[1] + dx, dgain]

    dh4, g_mix_pre1 = _mm2d("q_down_dx", [(dqd, w_dq, "nt", 0)], [(d, F32)], pre_norm_bwd, [h4, dh5],
                            [row(w["mix_pre_g"][1])], sums=[(1, d)])
    g_w_dq = _mm2d("q_down_dw", [(n4, dqd, "tn", 0)], [(q_rank, BF16)])[0]

    dc_n = _mm2d("kv_up_dx", [(dk_n, w_k, "nt", 0), (dv_h, w_v, "nt", 0)], [(KV_RANK, F32)])[0]
    g_wk, g_wv = _mm2d("kv_up_dw", [(c_n, dk_n, "tn", 0), (c_n, dv_h, "tn", 1)], [(w_k.shape[1], F32), (w_v.shape[1], F32)])
    dc, g_kv_a = _norm_out_bwd("kv_a_norm_bwd", c_raw, dc_n, row(w["kv_a_norm_g"]), 1.0)
    dkr_c, dkr_s = _rope_bwd(dk_r, cos2, sin2)
    dn3kv = _mm2d("kv_down_dx", [(dc, w_c, "nt", 0), (dkr_c, w_kr, "nt", 0), (dkr_s, w_kr_rot, "nt", 0)], [(d, F32)])[0]
    g_wc, g_wkr, g_wkr_rot = _mm2d("kv_down_dw", [(n3kv, dc, "tn", 0), (n3kv, dkr_c, "tn", 1), (n3kv, dkr_s, "tn", 2)],
                                   [(KV_RANK, F32), (QK_ROPE, F32), (QK_ROPE, F32)])

    dh4 = _not_before(dh4, rs_end(0, dh4))
    dh3, (g_kv_norm,) = ffn_bwd(1, 0, 1, dh4, extra=[(row(w["kv_norm_g"]), dn3kv)])
    dh3 = _not_before(dh3, rs_mid(1, dh3))
    dh2, (dm0, g_mix_post0) = ffn_bwd(0, 1, 2, dh3, then=(m0, row(w["mix_post_g"][0])))
    dh2, dm0 = _not_before((dh2, dm0), (rs_end(1, dh2), rs_mid(2, dh2)))

    d_uv = _mm2d("gmlp_out_dx", [(dm0, w_out, "nt", 0)], [(half, F32)])[0]
    g_w_out = _mm2d("gmlp_out_dw", [(uv, dm0, "tn", 0)], [(d, BF16)])[0]
    dzp, g_ln_g, g_ln_b, g_w_s, g_b_s = _sgu_bwd(zp, d_uv, ln_g, ln_b, w_s, bsb)
    dh1, g_mix_pre0 = _mm2d("gmlp_in_dx", [(dzp, w_in, "nt", 0)], [(d, F32)], pre_norm_bwd, [h1, dh2],
                            [row(w["mix_pre_g"][0])], sums=[(1, d)])
    tk, tmw, w_cols = min(t, K_TILE), min(d, ROW_TILE), w_in.shape[1] // N_SHARDS
    g_w_in = _mm("gmlp_in_dw", (d // tmw, N_SHARDS, t // tk),
                 [(n1, (tk, tmw), lambda i, j, k: (k, i)), (dzp, (tk, w_cols), lambda i, j, k: (k, j))],
                 [(0, 1, 0, "tn")], [(tmw, w_cols)],
                 [((N_SHARDS, d, w_cols), BF16, (None, tmw, w_cols), lambda i, j, k: (j, i, 0))], lambda accs, ex: accs)[0]

    g_w_dkv = jnp.concatenate([g_wc, g_wkr + _unrot_cols(g_wkr_rot)], axis=1).astype(BF16)
    direct = {"gmlp_w_in": g_w_in, "gmlp_w_out": g_w_out, "mla_w_o": g_w_o, "mla_w_dq": g_w_dq, "w_dkv": g_w_dkv}
    direct = {n: g.reshape(N_SHARDS, -1, g.shape[-1]) for n, g in direct.items()}
    part = {
        "w_ukv": jnp.stack([g_wk.reshape(KV_RANK, N_HEADS, QK_NOPE), g_wv.reshape(KV_RANK, N_HEADS, V_DIM)],
                           axis=2).reshape(KV_RANK, -1),
        "mla_w_uq": jnp.concatenate(
            [g_qn.reshape(q_rank, N_HEADS, QK_NOPE),
             (g_qr + _unrot_cols(g_qr_rot)).transpose(1, 0, 2)],
            axis=-1).reshape(1, q_rank, -1),
        "gmlp_ln_g": g_ln_g, "gmlp_ln_b": g_ln_b,
        "mix_pre_g": jnp.concatenate([g_mix_pre0, g_mix_pre1]), "mix_post_g": jnp.concatenate([g_mix_post0, g_mix_post1]),
        "gmlp_w_s": g_w_s[None], "gmlp_b_s": g_b_s.reshape(1, GROUPS, CHUNK),
        "kv_norm_g": g_kv_norm.reshape(-1), "kv_a_norm_g": g_kv_a.reshape(-1), "mla_q_norm_g": g_q_norm,
    }

    sharded = [e for e in mats + vecs if e[0] in part]
    n_rep = sum(w[n].size for n in replicated)
    pieces = [_split_shards(part[n], ax) for n, ax in sharded]
    pieces.append(jnp.concatenate([part[n].reshape(-1) for n in replicated]).reshape(N_SHARDS, -1))
    piece_rows = [_round_up(-(-p.shape[1] // PACK_WIDTH), 16) for p in pieces]
    piece_start = [sum(piece_rows[:q]) for q in range(len(pieces))]
    rows = _round_up(sum(piece_rows), 32)
    piece_rows[-1] += rows - sum(piece_rows)
    small = jnp.concatenate(
        [jnp.pad(p, ((0, 0), (0, r * PACK_WIDTH - p.shape[1]))).astype(BF16).reshape(N_SHARDS, r, PACK_WIDTH)
         for p, r in zip(pieces, piece_rows)], axis=1).reshape(N_SHARDS, 2, rows // 2, PACK_WIDTH)

    rs_launch(3, [g.reshape(N_SHARDS, 2, g.shape[1] // 2, g.shape[2]) for g in direct.values()] + [small])
    dh1 = _not_before(dh1, rs_end(2, dh1))
    dh1 = _not_before(dh1, rs_mid(3, dh1))
    dx, _ = ffn_bwd(0, 0, 4, dh1)

    own3 = rs_end(3, dx)
    launched = rs_mid(4, (dx, own3))
    lj = [(l, j) for l in range(2) for j in range(2)]
    tiny = jnp.concatenate([d_pre[k] for k in lj] + [d_post[k] for k in lj]).reshape(-1, 128)
    tiny = _all_sum(_not_before(tiny, launched)).reshape(2, 2, 2, d)
    shard_cols = d // N_SHARDS
    grads = {"ffn_pre_g": lax.dynamic_slice_in_dim(tiny[0], chip[0] * shard_cols, shard_cols, axis=2),
             "ffn_post_g": lax.dynamic_slice_in_dim(tiny[1], chip[0] * shard_cols, shard_cols, axis=2)}
    own_small, recv_small = own3[-1], rs[3]["recv"][-1]
    delta, new_m, new_v = {}, {}, {}
    for q, n in enumerate(direct):
        lead = lambda a: a.reshape((1, 1) + a.shape[-2:])
        upd = _adamw_halves(lead(w[n]), lead(mom[n]), lead(var[n]), 0, 0, rs[3]["own"][q], rs[3]["recv"][q], core, None)
        grads[n], delta[n], new_m[n], new_v[n] = [o.reshape(w[n].shape) for o in upd]
    g_small = jnp.where(core[0] == 0, jnp.concatenate([own_small, recv_small]), jnp.concatenate([recv_small, own_small]))
    g_rep = _gather_rows(g_small, piece_start[-1], piece_rows[-1])
    for (n, _), start in zip(sharded, piece_start):
        grads[n] = g_small[start:start + -(-w[n].size // PACK_WIDTH)].reshape(-1)[:w[n].size].reshape(w[n].shape)
    rep_vec = g_rep.reshape(N_SHARDS, -1)[:, :n_rep // N_SHARDS].reshape(-1)
    for n, g in zip(replicated, _split_flat(rep_vec, [w[n].shape for n in replicated])):
        grads[n] = g

    for n in names:
        if n not in ffn_names and n not in delta:
            delta[n], new_m[n], new_v[n] = _adamw(w[n], grads[n], mom[n], var[n])
    chain = {n: None for n in ffn_names}

    def ffn_update(gid, l, j):
        for q, n in enumerate(ffn_names):
            chain[n] = _adamw_halves(oriented(w[n], n), oriented(mom[n], n), oriented(var[n], n), l, j,
                                     rs[gid]["own"][q], rs[gid]["recv"][q], core, chain[n])

    ffn_update(0, 1, 1)
    ffn_update(1, 1, 0)
    ffn_update(2, 0, 1)
    rs_end(4, ([delta[n] for n in delta], [chain[n] for n in ffn_names]))
    ffn_update(4, 0, 0)
    for n in ffn_names:
        grads[n], delta[n], new_m[n], new_v[n] = [oriented(o, n) for o in chain[n]]
    return (loss, dx.reshape(x.shape), *[grads[n] for n in names], *[delta[n] for n in names],
            *[new_m[n] for n in names], *[new_v[n] for n in names])
```

```python
import math

import jax
import jax.numpy as jnp
from jax import lax
from jax.experimental import pallas as pl
from jax.experimental.pallas import tpu as pltpu
from jax.experimental.pallas import tpu_sc as plsc

F32, BF16 = jnp.float32, jnp.bfloat16

RMS_EPS, LN_EPS, NEG_INF = 1e-6, 1e-5, -1e30
N_HEADS, QK_NOPE, QK_ROPE, V_DIM, KV_RANK = 8, 128, 64, 128, 256
CHUNK, GROUPS = 128, 16
ROPE_THETA = 10000.0
ADAM_LR, ADAM_B1, ADAM_B2, ADAM_EPS, ADAM_WD, ADAM_STEP = 0.001, 0.9, 0.999, 1e-08, 0.01, 10
N_SHARDS = 4

VMEM_LIMIT_BYTES = 48 * 1024 * 1024
ROW_TILE = 512
K_TILE = 2048
PACK_WIDTH = 1024

_DN = {"nn": (((1,), (0,)), ((), ())), "nt": (((1,), (1,)), ((), ())), "tn": (((0,), (0,)), ((), ()))}
_MESH = pl.DeviceIdType.MESH
_ANY = pl.BlockSpec(memory_space=pl.ANY)


def _params(sem):
    return pltpu.CompilerParams(dimension_semantics=sem, vmem_limit_bytes=VMEM_LIMIT_BYTES)


def _mm(name, grid, ins, pairs, acc_shapes, outs, epilogue, extras=(), inner=0, sums=()):
    n_in, n_ex, n_out = len(ins), len(extras), len(outs)
    gk = grid[2]

    def body(*refs):
        in_refs, ex_refs = refs[:n_in], refs[n_in:n_in + n_ex]
        out_refs = refs[n_in + n_ex:n_in + n_ex + n_out]
        sum_refs = refs[n_in + n_ex + n_out:n_in + n_ex + n_out + len(sums)]
        acc_refs = refs[n_in + n_ex + n_out + len(sums):]
        parts = [None] * len(acc_shapes)
        for a, b, c, dims in pairs:
            for s in range(max(inner, 1)):
                lhs, rhs = (in_refs[a][s], in_refs[b][s]) if inner else (in_refs[a][...], in_refs[b][...])
                p = lax.dot_general(lhs, rhs, _DN[dims], preferred_element_type=F32)
                parts[c] = p if parts[c] is None else parts[c] + p

        def finish(accs):
            vals = epilogue(accs, [r[...] for r in ex_refs])
            for r, v in zip(out_refs, vals):
                r[...] = v.astype(r.dtype)
            first = (pl.program_id(0) == 0) & (pl.program_id(1) == 0)
            for r, v in zip(sum_refs, vals[n_out:]):
                @pl.when(first)
                def _():
                    r[...] = v

                @pl.when(jnp.logical_not(first))
                def _():
                    r[...] += v

        if gk == 1:
            finish(parts)
        else:
            k = pl.program_id(2)

            @pl.when(k == 0)
            def _():
                for r, p in zip(acc_refs, parts):
                    r[...] = p

            @pl.when(k > 0)
            def _():
                for r, p in zip(acc_refs, parts):
                    r[...] += p

            @pl.when(k == gk - 1)
            def _():
                finish([r[...] for r in acc_refs])

    return pl.pallas_call(
        body,
        out_shape=[jax.ShapeDtypeStruct(s, d) for s, d, _, _ in outs] + [jax.ShapeDtypeStruct(s, F32) for s in sums],
        grid=grid,
        in_specs=[pl.BlockSpec(bs, im) for _, bs, im in list(ins) + list(extras)],
        out_specs=[pl.BlockSpec(bs, im) for _, _, bs, im in outs]
        + [pl.BlockSpec(s, lambda i, j, k, nd=len(s): (0,) * nd) for s in sums],
        scratch_shapes=[pltpu.VMEM(s, F32) for s in acc_shapes] if gk > 1 else [],
        name=name,
        compiler_params=_params(("arbitrary",) * 3 if sums else ("parallel", "parallel", "arbitrary")),
    )(*[a for a, _, _ in ins], *[a for a, _, _ in extras])


def _mm2d(name, pairs, outs, epilogue=None, row_extras=(), vec_extras=(), sums=(), n_outer=False):
    def mk(a, dims):
        return (a.shape[0], a.shape[1]) if dims[0] == "n" else (a.shape[1], a.shape[0])

    def nk(b, dims):
        return (b.shape[1], b.shape[0]) if dims[1] == "n" else (b.shape[0], b.shape[1])

    m = mk(pairs[0][0], pairs[0][2])[0]
    ks = [mk(a, d)[1] for a, _, d, _ in pairs]
    n_acc = 1 + max(p[3] for p in pairs)
    acc_n = [None] * n_acc
    for a, b, d, c in pairs:
        assert mk(a, d)[0] == m and nk(b, d)[1] == mk(a, d)[1]
        acc_n[c] = nk(b, d)[0]
    tm = min(m, ROW_TILE)
    if len(set(ks)) == 1 and ks[0] > 1024:
        tk = K_TILE if ks[0] % K_TILE == 0 else 512
        tks, gk = [tk] * len(pairs), ks[0] // tk
    else:
        tks, gk = ks, 1
    if len(set(acc_n)) == 1 and acc_n[0] > 1024:
        tns, gj = [1024] * n_acc, acc_n[0] // 1024
    else:
        tns, gj = acc_n, 1

    ins, plist = [], []
    for (a, b, d, c), tk in zip(pairs, tks):
        tn = tns[c]
        a_spec = ((tm, tk), lambda i, j, k: (i, k)) if d[0] == "n" else ((tk, tm), lambda i, j, k: (k, i))
        b_spec = ((tk, tn), lambda i, j, k: (k, j)) if d[1] == "n" else ((tn, tk), lambda i, j, k: (j, k))
        ins += [(a, *a_spec), (b, *b_spec)]
        plist.append((len(ins) - 2, len(ins) - 1, c, d))
    extras = [(r, (tm, r.shape[1]), lambda i, j, k: (i, 0)) for r in row_extras]
    extras += [(v, v.shape, lambda i, j, k: (0, 0)) for v in vec_extras]
    out_specs = []
    for n, dt in outs:
        bn = 1024 if (gj > 1) else n
        out_specs.append(((m, n), dt, (tm, bn), lambda i, j, k: (i, j)))
    if epilogue is None:
        epilogue = lambda accs, ex: accs
    grid = (m // tm, gj, gk)
    if n_outer:
        swap = lambda spec: spec[:-1] + ((lambda f: lambda j, i, k: f(i, j, k))(spec[-1]),)
        ins, extras, out_specs, grid = [swap(x) for x in ins], [swap(x) for x in extras], [swap(x) for x in out_specs], (gj, m // tm, gk)
    return _mm(name, grid, ins, plist, [(tm, tn) for tn in tns], out_specs, epilogue, extras, sums=sums)


def _rms(x, g):
    return x * lax.rsqrt(jnp.mean(x * x, axis=-1, keepdims=True) + RMS_EPS) * g


def _rms_bwd(x, g, dy):
    r = lax.rsqrt(jnp.mean(x * x, axis=-1, keepdims=True) + RMS_EPS)
    gy = dy * g
    dx = r * gy - x * (r * r * r) * jnp.mean(gy * x, axis=-1, keepdims=True)
    return dx, jnp.sum(dy * x * r, axis=0, keepdims=True)


def _sigmoid(x):
    return 0.5 * (1.0 + jnp.tanh(0.5 * x))


_GELU_C = math.sqrt(2.0 / math.pi)


def _gelu(x):
    return x * (0.5 * (1.0 + jnp.tanh(_GELU_C * (x + 0.044715 * (x * x * x)))))


def _gelu_grad(x):
    t = jnp.tanh(_GELU_C * (x + 0.044715 * (x * x * x)))
    return 0.5 * (1.0 + t) + 0.5 * x * (1.0 - t * t) * (_GELU_C * (1.0 + 3.0 * 0.044715 * (x * x)))


def _rows(name, row_ins, vec_ins, fn, row_outs, acc_outs=()):
    t = row_ins[0].shape[0]
    tm = min(t, ROW_TILE)
    nr, nv, no = len(row_ins), len(vec_ins), len(row_outs)

    def body(*refs):
        outs, incs = fn([r[...] for r in refs[:nr]], [r[...] for r in refs[nr:nr + nv]])
        for r, v in zip(refs[nr + nv:nr + nv + no], outs):
            r[...] = v.astype(r.dtype)
        i = pl.program_id(0)
        for r, v in zip(refs[nr + nv + no:], incs):
            @pl.when(i == 0)
            def _():
                r[...] = v

            @pl.when(i > 0)
            def _():
                r[...] += v

    in_specs = [pl.BlockSpec((tm, a.shape[1]), lambda i: (i, 0)) for a in row_ins]
    in_specs += [pl.BlockSpec(v.shape, lambda i, nd=v.ndim: (0,) * nd) for v in vec_ins]
    out_shape = [jax.ShapeDtypeStruct((t, c), dt) for c, dt in row_outs]
    out_shape += [jax.ShapeDtypeStruct(s, F32) for s in acc_outs]
    out_specs = [pl.BlockSpec((tm, c), lambda i: (i, 0)) for c, _ in row_outs]
    out_specs += [pl.BlockSpec(s, lambda i, nd=len(s): (0,) * nd) for s in acc_outs]
    return pl.pallas_call(body, out_shape=out_shape, grid=(t // tm,), in_specs=in_specs, out_specs=out_specs,
                          name=name, compiler_params=_params(("arbitrary",)))(*row_ins, *vec_ins)


def _rms_fwd(x, g):
    return _rows("rms_fwd", [x], [g], lambda r, v: ([_rms(r[0], v[0])], []), [(x.shape[1], BF16)])[0]


def _norm_out_bwd(name, f, d_out, g, scale):
    def fn(r, v):
        dx, dg = _rms_bwd(r[0], v[0], r[1] * scale)
        return [dx], [dg]

    c = f.shape[1]
    return _rows(name, [f, d_out], [g], fn, [(c, BF16)], [(1, c)])


def _rope_bwd(dk, cos2, sin2):
    c = dk.shape[1]
    return _rows("rope_bwd", [dk, cos2, sin2], [], lambda r, v: ([r[0] * r[1], r[0] * r[2]], []),
                 [(c, BF16), (c, BF16)])


def _ffn_up(n, wg, wu):
    t, d = n.shape
    fs = wg.shape[-2]
    tm = min(t, ROW_TILE // 2)

    def body(n_ref, wg_ref, wu_ref, g_ref, u_ref, a_ref):
        x = n_ref[...]
        for s in range(N_SHARDS):
            g = lax.dot_general(x, wg_ref[s], _DN["nt"], preferred_element_type=F32)
            u = lax.dot_general(x, wu_ref[s], _DN["nt"], preferred_element_type=F32)
            g_ref[s] = g.astype(BF16)
            u_ref[s] = u.astype(BF16)
            a_ref[s] = (g * _sigmoid(g) * u).astype(BF16)

    hid = pl.BlockSpec((N_SHARDS, tm, fs), lambda i: (0, i, 0))
    whole = pl.BlockSpec(wg.shape, lambda i: (0, 0, 0))
    return pl.pallas_call(body, out_shape=[jax.ShapeDtypeStruct((N_SHARDS, t, fs), BF16)] * 3, grid=(t // tm,),
                          in_specs=[pl.BlockSpec((tm, d), lambda i: (i, 0)), whole, whole], out_specs=[hid] * 3,
                          name="ffn_up", compiler_params=_params(("parallel",)))(n, wg, wu)


def _down(name, a_in, w_in, gk, h, post_g, next_gs, scale, inner=0, target=None):
    t, d = h.shape
    tm = min(t, ROW_TILE)
    kn = 0 if target is not None else next_gs.shape[0]

    def epi(accs, ex):
        f, hv, pg, last = accs[0], ex[0], ex[1], ex[2]
        hn = hv + scale * _rms(f, pg)
        if target is None:
            return [f, hn] + [_rms(hn, last[q:q + 1]) for q in range(kn)]
        e = hn - last
        s = jnp.sum(jnp.sum(e * e, axis=1, keepdims=True), axis=0, keepdims=True) * (0.5 / d)
        return [f, hn, e * (1.0 / d), jnp.broadcast_to(s, (1, 128))]

    row = ((tm, d), lambda i, j, k: (i, 0))
    extras = [(h, *row), (post_g, (1, d), lambda i, j, k: (0, 0))]
    if target is None:
        outs = [((t, d), F32, *row), ((t, d), F32, *row)] + [((t, d), BF16, *row)] * kn
        extras.append((next_gs, (kn, d), lambda i, j, k: (0, 0)))
    else:
        outs = [((t, d), F32, *row)] * 3
        extras.append((target, *row))
    return _mm(name, (t // tm, 1, gk), [a_in, w_in], [(0, 1, 0, "nn")], [(tm, d)], outs, epi, extras, inner,
               sums=[(1, 128)] if target is not None else ())


def _ffn_down(a, wd, h, post_g, next_gs, target=None):
    t, d = h.shape
    fs = a.shape[-1]
    tm = min(t, ROW_TILE)
    return _down("ffn_down", (a, (N_SHARDS, tm, fs), lambda i, _, k: (0, i, 0)),
                 (wd, (N_SHARDS, fs, d), lambda i, _, k: (0, 0, 0)), 1, h, post_g, next_gs, 0.5, N_SHARDS, target)


def _ffn_dact(f, d_out, post_g, wd, g, u):
    t, d = f.shape
    fs = g.shape[-1]
    tm = min(t, ROW_TILE // 2)

    def body(f_ref, do_ref, pg_ref, wd_ref, g_ref, u_ref, df_ref, dg_ref, du_ref, dpg_ref):
        dfv, dpg = _rms_bwd(f_ref[...], pg_ref[...], do_ref[...] * 0.5)
        dfb = dfv.astype(BF16)
        df_ref[...] = dfb
        i = pl.program_id(0)

        @pl.when(i == 0)
        def _():
            dpg_ref[...] = dpg

        @pl.when(i > 0)
        def _():
            dpg_ref[...] += dpg

        for s in range(N_SHARDS):
            da = lax.dot_general(dfb, wd_ref[s], _DN["nt"], preferred_element_type=F32)
            gv, uv = g_ref[s].astype(F32), u_ref[s].astype(F32)
            sg = _sigmoid(gv)
            dg_ref[s] = (da * uv * (sg * (1.0 + gv * (1.0 - sg)))).astype(BF16)
            du_ref[s] = (da * (gv * sg)).astype(BF16)

    row = pl.BlockSpec((tm, d), lambda i: (i, 0))
    hid = pl.BlockSpec((N_SHARDS, tm, fs), lambda i: (0, i, 0))
    vec = pl.BlockSpec((1, d), lambda i: (0, 0))
    hid_shape = jax.ShapeDtypeStruct((N_SHARDS, t, fs), BF16)
    return pl.pallas_call(
        body, out_shape=[jax.ShapeDtypeStruct((t, d), BF16), hid_shape, hid_shape, jax.ShapeDtypeStruct((1, d), F32)],
        grid=(t // tm,), in_specs=[row, row, vec, pl.BlockSpec(wd.shape, lambda i: (0, 0, 0)), hid, hid],
        out_specs=[row, hid, hid, vec], name="ffn_dact", compiler_params=_params(("arbitrary",)))(
            f, d_out, post_g, wd, g, u)


def _ffn_dn(dg, du, wg, wu, h, d_res, branches, then=None):
    _, t, fs = dg.shape
    d = wg.shape[-1]
    tm = min(t, ROW_TILE)
    nb = len(branches)
    a_spec = ((N_SHARDS, tm, fs), lambda i, _, k: (0, i, 0))
    w_spec = ((N_SHARDS, fs, d), lambda i, _, k: (0, 0, 0))
    row = ((tm, d), lambda i, _, k: (i, 0))
    vec = ((1, d), lambda i, _, k: (0, 0))

    def epi(accs, ex):
        hv, dh = ex[0], ex[1]
        dns, gs = [accs[0]] + ex[2:1 + nb], ex[1 + nb:]
        dgs = []
        for dn, g in zip(dns, gs[:nb]):
            dx, dgv = _rms_bwd(hv, g, dn)
            dh = dh + dx
            dgs.append(dgv)
        if then is None:
            return [dh] + dgs
        dm, dgm = _rms_bwd(ex[-1], gs[nb], dh * then[2])
        return [dh, dm] + dgs + [dgm]

    extras = [(h, *row), (d_res, *row)] + [(dn, *row) for _, dn in branches[1:]] + [(g, *vec) for g, _ in branches]
    outs = [((t, d), F32, *row)]
    if then is not None:
        extras += [(then[1], *vec), (then[0], *row)]
        outs.append(((t, d), BF16, *row))
    return _mm("ffn_dn", (t // tm, 1, 1), [(dg, *a_spec), (wg, *w_spec), (du, *a_spec), (wu, *w_spec)],
               [(0, 1, 0, "nn"), (2, 3, 0, "nn")], [(tm, d)], outs, epi, extras, inner=N_SHARDS,
               sums=[(1, d)] * (nb + (then is not None)))


def _ffn_dw_in(n, dg, du):
    _, t, fs = dg.shape
    d = n.shape[1]
    tk = min(t, K_TILE)
    a_spec = ((None, tk, fs), lambda s, _, k: (s, k, 0))
    o_spec = ((None, fs, d), lambda s, _, k: (s, 0, 0))
    outs = [((N_SHARDS, fs, d), BF16, *o_spec)] * 2
    return _mm("ffn_dw_in", (N_SHARDS, 1, t // tk), [(dg, *a_spec), (du, *a_spec), (n, (tk, d), lambda s, _, k: (k, 0))],
               [(0, 2, 0, "tn"), (1, 2, 1, "tn")], [(fs, d)] * 2, outs, lambda accs, ex: accs)


def _ffn_dw_down(a, df):
    _, t, fs = a.shape
    d = df.shape[1]
    tk = min(t, K_TILE)
    outs = [((N_SHARDS, fs, d), BF16, (None, fs, d), lambda s, _, k: (s, 0, 0))]
    return _mm("ffn_dw_down", (N_SHARDS, 1, t // tk),
               [(a, (None, tk, fs), lambda s, _, k: (s, k, 0)), (df, (tk, d), lambda s, _, k: (k, 0))],
               [(0, 1, 0, "tn")], [(fs, d)], outs, lambda accs, ex: accs)[0]


def _causal_weight(w):
    row = lax.broadcasted_iota(jnp.int32, (CHUNK, CHUNK), 0)
    col = lax.broadcasted_iota(jnp.int32, (CHUNK, CHUNK), 1)
    return row >= col, jnp.where(row >= col, w, 0.0).astype(BF16)


def _layer_norm(v, g, b):
    xc = v - jnp.mean(v, axis=-1, keepdims=True)
    rstd = lax.rsqrt(jnp.mean(xc * xc, axis=-1, keepdims=True) + LN_EPS)
    xhat = xc * rstd
    return xhat, rstd, xhat * g + b


def _sgu_specs(t, half, tm):
    return [pl.BlockSpec((tm, half), lambda i: (i, 0)), pl.BlockSpec((tm, half), lambda i: (i, 1))]


def _sgu_fwd(zp, ln_g, ln_b, w_s, bsb):
    t, half = zp.shape[0], zp.shape[1] // 2
    tm = min(t, 2 * CHUNK)

    def body(u_ref, v_ref, g_ref, b_ref, w_ref, bs_ref, o_ref):
        u = _gelu(u_ref[...])
        _, _, vn = _layer_norm(_gelu(v_ref[...]), g_ref[...], b_ref[...])
        vb = vn.astype(BF16)
        for g in range(GROUPS):
            _, wm = _causal_weight(w_ref[g])
            cols = slice(g * CHUNK, (g + 1) * CHUNK)
            for c in range(tm // CHUNK):
                rows = slice(c * CHUNK, (c + 1) * CHUNK)
                sv = jnp.dot(wm, vb[rows, cols], preferred_element_type=F32) + bs_ref[g]
                o_ref[rows, cols] = (u[rows, cols] * sv).astype(BF16)

    whole = lambda a: pl.BlockSpec(a.shape, lambda i, nd=a.ndim: (0,) * nd)
    return pl.pallas_call(
        body, out_shape=jax.ShapeDtypeStruct((t, half), BF16), grid=(t // tm,),
        in_specs=_sgu_specs(t, half, tm) + [whole(ln_g), whole(ln_b), whole(w_s), whole(bsb)],
        out_specs=pl.BlockSpec((tm, half), lambda i: (i, 0)), name="sgu_fwd",
        compiler_params=_params(("arbitrary",)))(zp, zp, ln_g, ln_b, w_s, bsb)


def _sgu_bwd(zp, d_uv, ln_g, ln_b, w_s, bsb):
    t, half = zp.shape[0], zp.shape[1] // 2
    tm = min(t, 2 * CHUNK)

    def body(u_ref, v_ref, d_ref, g_ref, b_ref, w_ref, bs_ref, dz_ref, dlg_ref, dlb_ref, dws_ref, dbs_ref, dvn_ref):
        i = pl.program_id(0)

        @pl.when(i == 0)
        def _():
            dlg_ref[...] = jnp.zeros_like(dlg_ref)
            dlb_ref[...] = jnp.zeros_like(dlb_ref)
            dws_ref[...] = jnp.zeros_like(dws_ref)
            dbs_ref[...] = jnp.zeros_like(dbs_ref)

        up, vp = u_ref[...], v_ref[...]
        u, gup = _gelu(up), _gelu_grad(up)
        xhat, rstd, vn = _layer_norm(_gelu(vp), g_ref[...], b_ref[...])
        vb = vn.astype(BF16)
        d = d_ref[...]
        for g in range(GROUPS):
            mask, wm = _causal_weight(w_ref[g])
            cols = slice(g * CHUNK, (g + 1) * CHUNK)
            for c in range(tm // CHUNK):
                rows = slice(c * CHUNK, (c + 1) * CHUNK)
                blk = vb[rows, cols]
                sv = jnp.dot(wm, blk, preferred_element_type=F32) + bs_ref[g]
                dblk = d[rows, cols]
                dz_ref[rows, cols] = (dblk * sv * gup[rows, cols]).astype(BF16)
                dsv = dblk * u[rows, cols]
                dsvb = dsv.astype(BF16)
                dvn_ref[rows, cols] = lax.dot_general(wm, dsvb, _DN["tn"], preferred_element_type=F32)
                dw = lax.dot_general(dsvb, blk, _DN["nt"], preferred_element_type=F32)
                dws_ref[g] += jnp.where(mask, dw, 0.0)
                dbs_ref[g] += jnp.sum(dsv, axis=1, keepdims=True)
        dvn = dvn_ref[...]
        dlg_ref[...] += jnp.sum(dvn * xhat, axis=0, keepdims=True)
        dlb_ref[...] += jnp.sum(dvn, axis=0, keepdims=True)
        dxh = dvn * g_ref[...]
        dv = rstd * (dxh - jnp.mean(dxh, axis=-1, keepdims=True)
                     - xhat * jnp.mean(dxh * xhat, axis=-1, keepdims=True))
        dz_ref[:, half:] = (dv * _gelu_grad(vp)).astype(BF16)

    whole = lambda a: pl.BlockSpec(a.shape, lambda i, nd=a.ndim: (0,) * nd)
    wshape = lambda s: pl.BlockSpec(s, lambda i, nd=len(s): (0,) * nd)
    out_shape = [jax.ShapeDtypeStruct((t, 2 * half), BF16), jax.ShapeDtypeStruct((1, half), F32),
                 jax.ShapeDtypeStruct((1, half), F32), jax.ShapeDtypeStruct(w_s.shape, F32),
                 jax.ShapeDtypeStruct((GROUPS, CHUNK, 1), F32)]
    return pl.pallas_call(
        body, out_shape=out_shape, grid=(t // tm,),
        in_specs=_sgu_specs(t, half, tm) + [pl.BlockSpec((tm, half), lambda i: (i, 0)), whole(ln_g), whole(ln_b),
                                            whole(w_s), whole(bsb)],
        out_specs=[pl.BlockSpec((tm, 2 * half), lambda i: (i, 0)), wshape((1, half)), wshape((1, half)),
                   wshape(w_s.shape), wshape((GROUPS, CHUNK, 1))],
        scratch_shapes=[pltpu.VMEM((tm, half), F32)], name="sgu_bwd",
        compiler_params=_params(("arbitrary",)))(zp, zp, d_uv, ln_g, ln_b, w_s, bsb)


_SCALE = (QK_NOPE + QK_ROPE) ** -0.5


def _attn_scores(qn, qr, kn, kr, i, tq, n):
    s = lax.dot_general(qn, kn, _DN["nt"], preferred_element_type=F32)
    s = (s + lax.dot_general(qr, kr, _DN["nt"], preferred_element_type=F32)) * _SCALE
    row = i * tq + lax.broadcasted_iota(jnp.int32, (tq, n), 0)
    col = lax.broadcasted_iota(jnp.int32, (tq, n), 1)
    return jnp.where(col <= row, s, NEG_INF)


def _attn_specs(seq):
    head = lambda b, h: (b, h)
    return dict(
        qn=pl.BlockSpec((seq, QK_NOPE), head),
        qr=pl.BlockSpec((None, seq, QK_ROPE), lambda b, h: (h, b, 0)),
        kr=pl.BlockSpec((seq, QK_ROPE), lambda b, h: (b, 0)),
        lse=pl.BlockSpec((None, seq, 1), lambda b, h: (h, b, 0)),
    )


def _attn_fwd(qn, qr, kn, v, kr, seq):
    t = qn.shape[0]
    tq = min(seq, 2 * CHUNK)
    sp = _attn_specs(seq)

    def body(qn_ref, qr_ref, kn_ref, v_ref, kr_ref, o_ref, lse_ref):
        for i in range(seq // tq):
            rows, n = slice(i * tq, (i + 1) * tq), (i + 1) * tq
            s = _attn_scores(qn_ref[rows, :], qr_ref[rows, :], kn_ref[0:n, :], kr_ref[0:n, :], i, tq, n)
            m = jnp.max(s, axis=-1, keepdims=True)
            p = jnp.exp(s - m)
            l = jnp.sum(p, axis=-1, keepdims=True)
            o_ref[rows, :] = jnp.dot((p / l).astype(BF16), v_ref[0:n, :], preferred_element_type=F32).astype(BF16)
            lse_ref[rows, :] = m + jnp.log(l)

    return pl.pallas_call(
        body, out_shape=[jax.ShapeDtypeStruct((t, N_HEADS * V_DIM), BF16), jax.ShapeDtypeStruct((N_HEADS, t, 1), F32)],
        grid=(t // seq, N_HEADS), in_specs=[sp["qn"], sp["qr"], sp["qn"], sp["qn"], sp["kr"]],
        out_specs=[sp["qn"], sp["lse"]], name="attn_fwd",
        compiler_params=_params(("parallel", "arbitrary")))(qn, qr, kn, v, kr)


def _attn_bwd(qn, qr, kn, v, kr, do, lse, cos2, sin2, seq):
    t = qn.shape[0]
    tq = min(seq, 2 * CHUNK)
    sp = _attn_specs(seq)

    def body(qn_ref, qr_ref, kn_ref, v_ref, kr_ref, do_ref, lse_ref, cos_ref, sin_ref,
             dqn_ref, dkn_ref, dv_ref, dqc_ref, dqs_ref, dkr_ref, dk_acc, dv_acc, dkr_acc):
        dk_acc[...] = jnp.zeros_like(dk_acc)
        dv_acc[...] = jnp.zeros_like(dv_acc)
        dkr_acc[...] = jnp.zeros_like(dkr_acc)
        for i in range(seq // tq):
            rows, n = slice(i * tq, (i + 1) * tq), (i + 1) * tq
            q_n, q_r, d_o = qn_ref[rows, :], qr_ref[rows, :], do_ref[rows, :]
            k_n, k_r = kn_ref[0:n, :], kr_ref[0:n, :]
            s = _attn_scores(q_n, q_r, k_n, k_r, i, tq, n)
            p = jnp.exp(s - lse_ref[rows, :])
            dp = lax.dot_general(d_o, v_ref[0:n, :], _DN["nt"], preferred_element_type=F32)
            ds = (p * (dp - jnp.sum(p * dp, axis=-1, keepdims=True)) * _SCALE).astype(BF16)
            dqn_ref[rows, :] = jnp.dot(ds, k_n, preferred_element_type=F32).astype(BF16)
            dqr = jnp.dot(ds, k_r, preferred_element_type=F32)
            dqc_ref[rows, :] = (dqr * cos_ref[rows, :]).astype(BF16)
            dqs_ref[rows, :] = (dqr * sin_ref[rows, :]).astype(BF16)
            dk_acc[0:n, :] += lax.dot_general(ds, q_n, _DN["tn"], preferred_element_type=F32)
            dkr_acc[0:n, :] += lax.dot_general(ds, q_r, _DN["tn"], preferred_element_type=F32)
            dv_acc[0:n, :] += lax.dot_general(p.astype(BF16), d_o, _DN["tn"], preferred_element_type=F32)
        dkn_ref[...] = dk_acc[...].astype(BF16)
        dv_ref[...] = dv_acc[...].astype(BF16)
        h = pl.program_id(1)

        @pl.when(h == 0)
        def _():
            dkr_ref[...] = dkr_acc[...]

        @pl.when(h > 0)
        def _():
            dkr_ref[...] += dkr_acc[...]

    wide = jax.ShapeDtypeStruct((t, N_HEADS * V_DIM), BF16)
    rope = jax.ShapeDtypeStruct((N_HEADS, t, QK_ROPE), BF16)
    krf = pl.BlockSpec((seq, QK_ROPE), lambda b, h: (b, 0))
    return pl.pallas_call(
        body, out_shape=[wide, wide, wide, rope, rope, jax.ShapeDtypeStruct((t, QK_ROPE), F32)],
        grid=(t // seq, N_HEADS),
        in_specs=[sp["qn"], sp["qr"], sp["qn"], sp["qn"], sp["kr"], sp["qn"], sp["lse"], krf, krf],
        out_specs=[sp["qn"], sp["qn"], sp["qn"], sp["qr"], sp["qr"], krf],
        scratch_shapes=[pltpu.VMEM((seq, QK_NOPE), F32), pltpu.VMEM((seq, V_DIM), F32), pltpu.VMEM((seq, QK_ROPE), F32)],
        name="attn_bwd", compiler_params=_params(("parallel", "arbitrary")))(qn, qr, kn, v, kr, do, lse, cos2, sin2)


def _q_rope(qn, w, w_rot, cos2, sin2):
    t, r = qn.shape
    nh, _, e = w.shape
    tm = min(t, ROW_TILE)

    def body(x_ref, w_ref, wr_ref, c_ref, s_ref, o_ref):
        x = x_ref[...]
        for h in range(nh):
            raw = jnp.dot(x, w_ref[h], preferred_element_type=F32)
            rot = jnp.dot(x, wr_ref[h], preferred_element_type=F32)
            o_ref[h] = (raw * c_ref[...] + rot * s_ref[...]).astype(BF16)

    whole = pl.BlockSpec(w.shape, lambda i: (0, 0, 0))
    rows = pl.BlockSpec((tm, e), lambda i: (i, 0))
    return pl.pallas_call(body, out_shape=jax.ShapeDtypeStruct((nh, t, e), BF16), grid=(t // tm,),
                          in_specs=[pl.BlockSpec((tm, r), lambda i: (i, 0)), whole, whole, rows, rows],
                          out_specs=pl.BlockSpec((nh, tm, e), lambda i: (0, i, 0)), name="q_rope",
                          compiler_params=_params(("parallel",)))(qn, w, w_rot, cos2, sin2)


def _q_rope_dw(qn, dq_c, dq_s):
    t, r = qn.shape
    nh, _, e = dq_c.shape
    tk = min(t, K_TILE)

    def body(x_ref, c_ref, s_ref, gc_ref, gs_ref):
        k = pl.program_id(0)
        x = x_ref[...]
        for h in range(nh):
            pc = lax.dot_general(c_ref[h], x, _DN["tn"], preferred_element_type=F32)
            ps = lax.dot_general(s_ref[h], x, _DN["tn"], preferred_element_type=F32)

            @pl.when(k == 0)
            def _():
                gc_ref[h] = pc
                gs_ref[h] = ps

            @pl.when(k > 0)
            def _():
                gc_ref[h] += pc
                gs_ref[h] += ps

    heads = pl.BlockSpec((nh, tk, e), lambda k: (0, k, 0))
    out = pl.BlockSpec((nh, e, r), lambda k: (0, 0, 0))
    return pl.pallas_call(body, out_shape=[jax.ShapeDtypeStruct((nh, e, r), F32)] * 2, grid=(t // tk,),
                          in_specs=[pl.BlockSpec((tk, r), lambda k: (k, 0)), heads, heads], out_specs=[out, out],
                          name="q_rope_dw", compiler_params=_params(("arbitrary",)))(qn, dq_c, dq_s)


def _row_tile(rows, cols, row_mult=8):
    cap = max(row_mult, (1 << 18) // cols)
    best = rows
    for tr in range(row_mult, min(rows, cap) + 1, row_mult):
        if rows % tr == 0:
            best = tr
    return best if rows > cap else rows


def _adamw_math(w, g, m, v):
    mv = ADAM_B1 * m + (1.0 - ADAM_B1) * g
    vv = ADAM_B2 * v + (1.0 - ADAM_B2) * (g * g)
    m_hat = mv / (1.0 - ADAM_B1 ** ADAM_STEP)
    v_hat = vv / (1.0 - ADAM_B2 ** ADAM_STEP)
    return -ADAM_LR * (m_hat / (jnp.sqrt(v_hat) + ADAM_EPS) + ADAM_WD * w), mv, vv


def _adamw(w, g, m, v):
    shape = w.shape
    c = shape[-1]
    r = w.size // c
    tr = _row_tile(r, c)

    def body(w_ref, g_ref, m_ref, v_ref, d_ref, nm_ref, nv_ref):
        d_ref[...], nm_ref[...], nv_ref[...] = _adamw_math(w_ref[...], g_ref[...], m_ref[...], v_ref[...])

    spec = pl.BlockSpec((tr, c), lambda i: (i, 0))
    outs = pl.pallas_call(body, out_shape=[jax.ShapeDtypeStruct((r, c), F32)] * 3, grid=(r // tr,),
                          in_specs=[spec] * 4, out_specs=[spec] * 3, name="adamw",
                          compiler_params=_params(("parallel",)))(*[a.reshape(r, c) for a in (w, g, m, v)])
    return [o.reshape(shape) for o in outs]


def _adamw_halves(w, m, v, l, j, own, recv, core, prev):
    nl, nj, rows, c = w.shape
    r = rows // 2
    tr = _row_tile(r, c)
    n_prev = 0 if prev is None else 4

    def body(core_ref, w_ref, own_ref, recv_ref, m_ref, v_ref, *rest):
        g_ref, d_ref, nm_ref, nv_ref = rest[n_prev:]
        g = jnp.where(pl.program_id(0) == core_ref[0], own_ref[...], recv_ref[...])
        g_ref[...] = g
        d_ref[...], nm_ref[...], nv_ref[...] = _adamw_math(w_ref[...], g, m_ref[...], v_ref[...])

    nb = r // tr
    slab = pl.BlockSpec((None, None, tr, c), lambda h, i, cr: (l, j, h * nb + i, 0))
    half = pl.BlockSpec((tr, c), lambda h, i, cr: (i, 0))
    grid_spec = pltpu.PrefetchScalarGridSpec(num_scalar_prefetch=1, grid=(2, nb),
                                             in_specs=[slab, half, half, slab, slab] + [_ANY] * n_prev,
                                             out_specs=[slab] * 4)
    return pl.pallas_call(body, out_shape=[jax.ShapeDtypeStruct(w.shape, F32)] * 4, grid_spec=grid_spec,
                          input_output_aliases={6 + q: q for q in range(n_prev)}, name="adamw_halves",
                          compiler_params=_params(("parallel",) * 2))(core, w, own, recv, m, v, *(prev or ()))


def _place():
    x, y, c = lax.axis_index("x"), lax.axis_index("y"), lax.axis_index("c")
    return x, y, c, [(1 - x, y), (x, 1 - y), (1 - x, 1 - y)]


def _dma_sems(*counts):
    return [pltpu.SemaphoreType.DMA((n,)) for n in counts]


def _all_gather(bufs, collective_id, name):
    n = len(bufs)

    def body(*refs):
        ins, outs = refs[:n], refs[n:2 * n]
        send, recv, fsend, frecv, osend, orecv = refs[2 * n:]
        x, y, c, _ = _place()
        xn, yn, sib = (1 - x, y, c), (x, 1 - y, c), (x, y, 1 - c)
        k, kx, ky, kd = 2 * x + y, 2 * (1 - x) + y, 2 * x + 1 - y, 2 * (1 - x) + 1 - y
        _handshake([xn, yn, sib])

        def copy(src, dst, sems, i, to):
            return pltpu.make_async_remote_copy(src, dst, sems[0].at[i], sems[1].at[i], device_id=to, device_id_type=_MESH)

        ici, d2d, own_s = (send, recv), (fsend, frecv), (osend, orecv)
        started = [copy(ins[b], outs[b].at[k], own_s, b, sib) for b in range(n)]
        for first in (True, False):
            for b in range(n):
                mine = outs[b].at[k, c]
                if first:
                    started += [copy(ins[b].at[c, 0], mine.at[0], ici, 6 * b, xn), copy(ins[b].at[c, 1], mine.at[1], ici, 6 * b + 1, yn)]
                else:
                    started += [copy(ins[b].at[c, 1], mine.at[1], ici, 6 * b + 2, xn), copy(ins[b].at[c, 0], mine.at[0], ici, 6 * b + 3, yn)]
        for cp in started:
            cp.start()
        passed = []
        for b in range(n):
            for i, (src_chip, q, to) in enumerate([(kx, 0, yn), (ky, 1, xn)]):
                piece = outs[b].at[src_chip, c, q]
                copy(piece, piece, ici, 6 * b + i, to).wait_recv()
                cp = copy(piece, piece, ici, 6 * b + 4 + i, to)
                cp.start()
                passed.append(cp)
        for b in range(n):
            for i, (src_chip, q) in enumerate([(kx, 1), (ky, 0)]):
                piece = outs[b].at[src_chip, c, q]
                copy(piece, piece, ici, 6 * b + 2 + i, xn).wait_recv()
                half = outs[b].at[src_chip, c]
                cp = copy(half, half, d2d, 3 * b + i, sib)
                cp.start()
                passed.append(cp)
        for b in range(n):
            for i, q in enumerate([0, 1]):
                piece = outs[b].at[kd, c, q]
                copy(piece, piece, ici, 6 * b + 4 + i, xn).wait_recv()
            half = outs[b].at[kd, c]
            cp = copy(half, half, d2d, 3 * b + 2, sib)
            cp.start()
            passed.append(cp)
        for b in range(n):
            for i, src_chip in enumerate([kx, ky, kd]):
                half = outs[b].at[src_chip, 1 - c]
                copy(half, half, d2d, 3 * b + i, sib).wait_recv()
        for cp in started[n:] + passed:
            cp.wait_send()
        for cp in started[:n]:
            cp.wait()

    return _sequencer(body, [jax.ShapeDtypeStruct((N_SHARDS,) + b.shape, b.dtype) for b in bufs],
                      _dma_sems(6 * n, 6 * n, 3 * n, 3 * n, n, n), collective_id, name, bufs)


def _sequencer(body, out_type, sems, collective_id, name, args):
    return pl.kernel(body, out_type=out_type, mesh=plsc.ScalarSubcoreMesh(axis_name="sequencer", num_cores=1),
                     scratch_types=sems, compiler_params=pltpu.CompilerParams(collective_id=collective_id),
                     name=name)(*args)


def _handshake(peers):
    barrier = pltpu.get_barrier_semaphore()
    for peer in peers:
        pl.semaphore_signal(barrier, inc=1, device_id=peer, device_id_type=_MESH)
    pl.semaphore_wait(barrier, len(peers))


def _swap_halves(parts, collective_id, name):
    n = len(parts)

    def body(*refs):
        ins, outs = refs[:n], refs[n:2 * n]
        send, recv = refs[2 * n:]
        x, y, c, _ = _place()
        _handshake([(x, y, 1 - c)])
        cps = [pltpu.make_async_remote_copy(ins[b].at[:, pl.ds(1 - c, 1)], outs[b], send.at[b], recv.at[b],
                                            device_id=(x, y, 1 - c), device_id_type=_MESH) for b in range(n)]
        for cp in cps:
            cp.start()
        for cp in cps:
            cp.wait()

    return _sequencer(body, [jax.ShapeDtypeStruct((N_SHARDS, 1) + p.shape[2:], p.dtype) for p in parts],
                      _dma_sems(n, n), collective_id, name, parts)


def _by_shape(fn, first, second, scalar):
    out, groups = [None] * len(first), {}
    for i, p in enumerate(first):
        groups.setdefault(p.shape, []).append(i)
    for idx in groups.values():
        for i, r in zip(idx, fn([first[i] for i in idx], [second[i] for i in idx], scalar)):
            out[i] = r
    return out


def _add_half(parts, others, core):
    n = len(parts)
    _, _, r, c = parts[0].shape
    tr = _row_tile(r, c, 16)

    def body(core_ref, *refs):
        for q in range(n):
            refs[2 * n + q][...] = (refs[q][...].astype(F32) + refs[n + q][...].astype(F32)).astype(BF16)

    grid_spec = pltpu.PrefetchScalarGridSpec(
        num_scalar_prefetch=1, grid=(N_SHARDS, r // tr),
        in_specs=[pl.BlockSpec((None, None, tr, c), lambda k, i, cr: (k, cr[0], i, 0))] * n
        + [pl.BlockSpec((None, None, tr, c), lambda k, i, cr: (k, 0, i, 0))] * n,
        out_specs=[pl.BlockSpec((None, tr, c), lambda k, i, cr: (k, i, 0))] * n)
    return pl.pallas_call(body, out_shape=[jax.ShapeDtypeStruct((N_SHARDS, r, c), BF16)] * n, grid_spec=grid_spec,
                          name="grad_add_half", compiler_params=_params(("parallel", "parallel")))(core, *parts, *others)


def _scatter_chips(parts, collective_id, name):
    n = len(parts)

    def body(*refs):
        ins, outs = refs[:n], refs[n:2 * n]
        send, recv = refs[2 * n:]
        x, y, c, chips = _place()
        k = 2 * x + y
        _handshake([(px, py, c) for px, py in chips])
        started = []
        for b in range(n):
            for j, (px, py) in enumerate(chips):
                cp = pltpu.make_async_remote_copy(ins[b].at[2 * px + py], outs[b].at[k], send.at[3 * b + j],
                                                  recv.at[3 * b + j], device_id=(px, py, c), device_id_type=_MESH)
                cp.start()
                started.append(cp)
        for b in range(n):
            for j, (px, py) in enumerate(chips):
                got = outs[b].at[2 * px + py]
                pltpu.make_async_remote_copy(got, got, send.at[3 * b + j], recv.at[3 * b + j],
                                             device_id=(px, py, c), device_id_type=_MESH).wait_recv()
        for cp in started:
            cp.wait_send()

    return _sequencer(body, [jax.ShapeDtypeStruct(p.shape, p.dtype) for p in parts], _dma_sems(3 * n, 3 * n),
                      collective_id, name, parts)


def _sum_slots(slots, mine, chip):
    n = len(slots)
    _, r, c = slots[0].shape
    tr = _row_tile(r, c, 16)

    def body(chip_ref, *refs):
        for q in range(n):
            own = refs[5 * q + 4][...].astype(F32)
            v = [jnp.where(chip_ref[0] == s, own, refs[5 * q + s][...].astype(F32)) for s in range(N_SHARDS)]
            refs[5 * n + q][...] = ((v[0] + v[1]) + v[2]) + v[3]

    def slot_spec(s):
        return pl.BlockSpec((None, tr, c), lambda i, kr: (jnp.where(kr[0] == s, (s + 1) % N_SHARDS, s), i, 0))

    per_buffer = [slot_spec(s) for s in range(N_SHARDS)] + [pl.BlockSpec((None, tr, c), lambda i, kr: (kr[0], i, 0))]
    grid_spec = pltpu.PrefetchScalarGridSpec(num_scalar_prefetch=1, grid=(r // tr,), in_specs=per_buffer * n,
                                             out_specs=[pl.BlockSpec((tr, c), lambda i, kr: (i, 0))] * n)
    args = [a for sl, mn in zip(slots, mine) for a in (sl, sl, sl, sl, mn)]
    return pl.pallas_call(body, out_shape=[jax.ShapeDtypeStruct((r, c), F32)] * n, grid_spec=grid_spec,
                          name="grad_sum_slots", compiler_params=_params(("parallel",)))(chip, *args)


def _join_halves(halves, collective_id, name):
    n = len(halves)

    def body(*refs):
        ins, outs = refs[:n], refs[n:2 * n]
        send, recv = refs[2 * n:]
        x, y, c, _ = _place()
        _handshake([(x, y, 1 - c)])
        cps = [pltpu.make_async_remote_copy(ins[b], outs[b], send.at[b], recv.at[b], device_id=(x, y, 1 - c),
                                            device_id_type=_MESH) for b in range(n)]
        for cp in cps:
            cp.start()
        for cp in cps:
            cp.wait()

    return _sequencer(body, [jax.ShapeDtypeStruct(h.shape, F32) for h in halves], _dma_sems(n, n), collective_id,
                      name, halves)


def _gather_rows(buf, start, rows, collective_id):
    def body(in_ref, out_ref, send, recv, lsem):
        x, y, c, chips = _place()
        k = 2 * x + y
        _handshake([(px, py, c) for px, py in chips] + [(x, y, 1 - c)])
        src = in_ref.at[pl.ds(start, rows)]
        local = pltpu.make_async_remote_copy(src, out_ref.at[k], lsem.at[0], lsem.at[1], device_id=(x, y, 1 - c),
                                             device_id_type=_MESH)
        local.start()
        cps = [pltpu.make_async_remote_copy(src, out_ref.at[k], send.at[j], recv.at[j], device_id=(px, py, c),
                                            device_id_type=_MESH) for j, (px, py) in enumerate(chips)]
        for cp in cps:
            cp.start()
        for j, (px, py) in enumerate(chips):
            got = out_ref.at[2 * px + py]
            pltpu.make_async_remote_copy(got, got, send.at[j], recv.at[j], device_id=(px, py, c),
                                         device_id_type=_MESH).wait_recv()
        for cp in cps:
            cp.wait_send()
        local.wait()

    return _sequencer(body, jax.ShapeDtypeStruct((N_SHARDS, rows, buf.shape[1]), F32), _dma_sems(3, 3, 2), collective_id,
                      "gather_replicated_grads", [buf])


def _all_sum(vec):
    r, c = vec.shape
    n_dev = 2 * N_SHARDS

    def body(in_ref, out_ref, slots, send, recv):
        x, y, cc, _ = _place()
        flip = lambda v, bit: 1 - v if bit else v
        peers = [(flip(x, (q >> 2) & 1), flip(y, (q >> 1) & 1), flip(cc, q & 1)) for q in range(1, n_dev)]
        index = lambda p: 4 * p[0] + 2 * p[1] + p[2]
        slots[index((x, y, cc))] = in_ref[...]
        cps = [pltpu.make_async_remote_copy(in_ref, slots.at[index((x, y, cc))], send.at[q], recv.at[q], device_id=p,
                                            device_id_type=_MESH) for q, p in enumerate(peers)]
        for cp in cps:
            cp.start()
        for q, p in enumerate(peers):
            got = slots.at[index(p)]
            pltpu.make_async_remote_copy(got, got, send.at[q], recv.at[q], device_id=p, device_id_type=_MESH).wait_recv()
        for cp in cps:
            cp.wait_send()
        acc = slots[0]
        for s in range(1, n_dev):
            acc = acc + slots[s]
        out_ref[...] = acc

    vmem = pl.BlockSpec(memory_space=pltpu.VMEM)
    return pl.pallas_call(body, out_shape=jax.ShapeDtypeStruct((r, c), F32), in_specs=[vmem], out_specs=vmem,
                          scratch_shapes=[pltpu.VMEM((n_dev, r, c), F32)] + _dma_sems(n_dev - 1, n_dev - 1),
                          name="sum_small_grads")(vec)


def _not_before(value, other):
    return lax.optimization_barrier((value, other))[0]


def _round_up(n, m):
    return -(-n // m) * m


def _pack_flat(vecs, rows, width, dtype):
    flat = jnp.concatenate([v.reshape(-1).astype(dtype) for v in vecs])
    return jnp.pad(flat, (0, rows * width - flat.size)).reshape(rows, width)


def _split_flat(flat, shapes):
    out, off = [], 0
    for s in shapes:
        n = math.prod(s)
        out.append(flat[off:off + n].reshape(s))
        off += n
    return out


def _merge_shards(arr4, axis):
    a = jnp.moveaxis(arr4, 0, axis)
    s = list(a.shape)
    return a.reshape(s[:axis] + [s[axis] * s[axis + 1]] + s[axis + 2:])


def _split_shards(full, axis):
    s = list(full.shape)
    a = full.reshape(s[:axis] + [N_SHARDS, s[axis] // N_SHARDS] + s[axis + 1:])
    return jnp.moveaxis(a, axis, 0).reshape(N_SHARDS, -1)


def _rot_cols(w):
    half = w.shape[-1] // 2
    return jnp.concatenate([-w[..., half:], w[..., :half]], axis=-1)


def _unrot_cols(dw):
    half = dw.shape[-1] // 2
    return jnp.concatenate([dw[..., half:], -dw[..., :half]], axis=-1)


def kernel(x, positions, ffn_pre_g, ffn_post_g, ffn_w_gate, ffn_w_up, ffn_w_down, mix_pre_g, mix_post_g, gmlp_w_in, gmlp_ln_g, gmlp_ln_b, gmlp_w_s, gmlp_b_s, gmlp_w_out, kv_norm_g, w_dkv, kv_a_norm_g, w_ukv, mla_w_dq, mla_q_norm_g, mla_w_uq, mla_w_o, loss_target, m_ffn_pre_g, m_ffn_post_g, m_ffn_w_gate, m_ffn_w_up, m_ffn_w_down, m_mix_pre_g, m_mix_post_g, m_gmlp_w_in, m_gmlp_ln_g, m_gmlp_ln_b, m_gmlp_w_s, m_gmlp_b_s, m_gmlp_w_out, m_kv_norm_g, m_w_dkv, m_kv_a_norm_g, m_w_ukv, m_mla_w_dq, m_mla_q_norm_g, m_mla_w_uq, m_mla_w_o, v_ffn_pre_g, v_ffn_post_g, v_ffn_w_gate, v_ffn_w_up, v_ffn_w_down, v_mix_pre_g, v_mix_post_g, v_gmlp_w_in, v_gmlp_ln_g, v_gmlp_ln_b, v_gmlp_w_s, v_gmlp_b_s, v_gmlp_w_out, v_kv_norm_g, v_w_dkv, v_kv_a_norm_g, v_w_ukv, v_mla_w_dq, v_mla_q_norm_g, v_mla_w_uq, v_mla_w_o):
    names = ["ffn_pre_g", "ffn_post_g", "ffn_w_gate", "ffn_w_up", "ffn_w_down", "mix_pre_g", "mix_post_g", "gmlp_w_in",
             "gmlp_ln_g", "gmlp_ln_b", "gmlp_w_s", "gmlp_b_s", "gmlp_w_out", "kv_norm_g", "w_dkv", "kv_a_norm_g", "w_ukv",
             "mla_w_dq", "mla_q_norm_g", "mla_w_uq", "mla_w_o"]
    env = locals()
    w = {n: env[n] for n in names}
    mom = {n: env["m_" + n] for n in names}
    var = {n: env["v_" + n] for n in names}

    bsz, seq, d = x.shape
    t = bsz * seq
    core = lax.axis_index("c").astype(jnp.int32).reshape(1)

    mats = [("gmlp_w_in", 2), ("gmlp_w_out", 1), ("w_dkv", 0), ("w_ukv", 1), ("mla_w_dq", 1), ("mla_w_uq", 2),
            ("mla_w_o", 1)]
    vecs = [("ffn_pre_g", 2), ("ffn_post_g", 2), ("gmlp_ln_g", 1), ("gmlp_ln_b", 1)]
    replicated = ["mix_pre_g", "mix_post_g", "gmlp_w_s", "gmlp_b_s", "kv_norm_g", "kv_a_norm_g", "mla_q_norm_g"]
    n_mats = sum(w[n].size for n, _ in mats)
    n_vecs = sum(w[n].size for n, _ in vecs)
    mat_rows = _round_up(-(-n_mats // PACK_WIDTH), 64)
    vec_rows = _round_up(-(-n_vecs // 128), 32)
    mat_pack = _pack_flat([w[n] for n, _ in mats], mat_rows, PACK_WIDTH, BF16).reshape(2, 2, mat_rows // 4, PACK_WIDTH)
    vec_pack = _pack_flat([w[n] for n, _ in vecs], vec_rows, 128, F32).reshape(2, 2, vec_rows // 4, 128)
    ffn_names = ("ffn_w_gate", "ffn_w_up", "ffn_w_down")

    def oriented(a, name):
        return a if name == "ffn_w_down" else jnp.swapaxes(a, 2, 3)

    lj = [(l, j) for l in range(2) for j in range(2)]
    plan = [((0, 0), (0, 1), [vec_pack], None), ((0, 0), (2,), [mat_pack], 0), ((0, 1), (0, 1, 2), [], 0),
            ((1, 0), (0, 1, 2), [], 0), ((1, 1), (0, 1, 2), [], 0)]
    ffn_w = {k: [None] * 3 for k in lj}
    landed = []
    for q, ((l, j), which, riders, after) in enumerate(plan):
        shards = [oriented(w[ffn_names[i]], ffn_names[i])[l, j].astype(BF16) for i in which]
        bufs = [s.reshape(2, 2, s.shape[0] // 4, s.shape[1]) for s in shards] + riders
        if after is not None:
            bufs = _not_before(bufs, landed[after])
        got = _all_gather(bufs, q + 1, f"gather_weights_{q}")
        landed.append(got[-1])
        for i, g, s in zip(which, got, shards):
            ffn_w[(l, j)][i] = g.reshape((N_SHARDS,) + s.shape)
        if riders and q == 0:
            vec_all = got[-1]
        if riders and q == 1:
            mat_all = got[-1]

    def unpack(packed, entries):
        flat4, off, out = packed.reshape(N_SHARDS, -1), 0, {}
        for n, ax in entries:
            out[n] = _merge_shards(flat4[:, off:off + w[n].size].reshape((N_SHARDS,) + w[n].shape), ax)
            off += w[n].size
        return out

    full = unpack(vec_all, vecs)
    ln_g, ln_b = full["gmlp_ln_g"], full["gmlp_ln_b"]
    pre_g, post_g = full["ffn_pre_g"], full["ffn_post_g"]
    w_s = w["gmlp_w_s"][0]
    bsb = w["gmlp_b_s"][0][:, :, None]
    row = lambda v: v.reshape(1, -1)

    inv_freq = ROPE_THETA ** (-jnp.arange(0, QK_ROPE, 2, dtype=F32) / QK_ROPE)
    ang = positions.astype(F32).reshape(t, 1) * inv_freq
    cos2 = jnp.concatenate([jnp.cos(ang)] * 2, axis=-1)
    sin2 = jnp.concatenate([jnp.sin(ang)] * 2, axis=-1)

    h0 = x.reshape(t, d)
    saved = {}

    def ffn_fwd(l, j, h, n, next_gs, target=None):
        wg, wu, wd = ffn_w[(l, j)]
        g, u, a = _ffn_up(n, wg, wu)
        f, h_new, *n_next = _ffn_down(a, wd, h, row(post_g[l, j]), next_gs, target)
        saved[("ffn", l, j)] = (h, n, g, u, a, f)
        return h_new, n_next

    n0 = _rms_fwd(h0, row(pre_g[0, 0]))
    h1, (n1,) = ffn_fwd(0, 0, h0, n0, row(w["mix_pre_g"][0]))

    full.update(unpack(_not_before(mat_all, h1), mats))
    w_in, w_out = full["gmlp_w_in"][0], full["gmlp_w_out"][0]
    w_c, w_kr = full["w_dkv"][:, :KV_RANK], full["w_dkv"][:, KV_RANK:]
    w_kr_rot = _rot_cols(w_kr)
    ukv = full["w_ukv"].reshape(KV_RANK, N_HEADS, 2, QK_NOPE)
    w_k, w_v = ukv[:, :, 0].reshape(KV_RANK, -1), ukv[:, :, 1].reshape(KV_RANK, -1)
    w_dq, w_o = full["mla_w_dq"][0], full["mla_w_o"][0]
    q_rank = w_dq.shape[1]
    uq = full["mla_w_uq"][0].reshape(q_rank, N_HEADS, QK_NOPE + QK_ROPE)
    w_qn = uq[:, :, :QK_NOPE].reshape(q_rank, -1)
    w_qr = uq[:, :, QK_NOPE:].transpose(1, 0, 2)
    w_qr_rot = _rot_cols(w_qr)

    zp = _mm2d("gmlp_in", [(n1, w_in, "nn", 0)], [(w_in.shape[1], F32)], n_outer=True)[0]
    uv = _sgu_fwd(zp, ln_g, ln_b, w_s, bsb)
    half = uv.shape[1]
    tm = min(t, ROW_TILE)
    m0, h2, n2 = _down("gmlp_out", (uv, (tm, 512), lambda i, _, k: (i, k)), (w_out, (512, d), lambda i, _, k: (k, 0)),
                       half // 512, h1, row(w["mix_post_g"][0]), row(pre_g[0, 1]), 1.0)
    h3, (n3kv, n3) = ffn_fwd(0, 1, h2, n2, jnp.stack([w["kv_norm_g"], pre_g[1, 0]]))

    def kv_epi(accs, ex):
        c_raw = accs[0]
        return [c_raw, _rms(c_raw, ex[2]), accs[1] * ex[0] + accs[2] * ex[1]]

    c_raw, c_n, k_r = _mm2d("kv_down", [(n3kv, w_c, "nn", 0), (n3kv, w_kr, "nn", 1), (n3kv, w_kr_rot, "nn", 2)],
                            [(KV_RANK, F32), (KV_RANK, BF16), (QK_ROPE, BF16)], kv_epi, [cos2, sin2],
                            [row(w["kv_a_norm_g"])])
    k_n, v_h = _mm2d("kv_up", [(c_n, w_k, "nn", 0), (c_n, w_v, "nn", 1)], [(w_k.shape[1], BF16), (w_v.shape[1], BF16)])

    h4, (n4,) = ffn_fwd(1, 0, h3, n3, row(w["mix_pre_g"][1]))
    qd, qn = _mm2d("q_down", [(n4, w_dq, "nn", 0)], [(q_rank, F32), (q_rank, BF16)],
                   lambda accs, ex: [accs[0], _rms(accs[0], ex[0])], [], [row(w["mla_q_norm_g"][0])])
    q_n = _mm2d("q_up", [(qn, w_qn, "nn", 0)], [(w_qn.shape[1], BF16)])[0]
    q_r = _q_rope(qn, w_qr, w_qr_rot, cos2, sin2)
    o, lse = _attn_fwd(q_n, q_r, k_n, v_h, k_r, seq)
    m1, h5, n5 = _down("attn_out", (o, (tm, 512), lambda i, _, k: (i, k)), (w_o, (512, d), lambda i, _, k: (k, 0)),
                       o.shape[1] // 512, h4, row(w["mix_post_g"][1]), row(pre_g[1, 1]), 1.0)
    _, (dy, loss_sum) = ffn_fwd(1, 1, h5, n5, None, loss_target.reshape(t, d))

    chip = (2 * lax.axis_index("x") + lax.axis_index("y")).astype(jnp.int32).reshape(1)
    rs = {}

    def rs_launch(gid, parts):
        rs[gid] = {"parts": parts, "others": _swap_halves(parts, 7 + gid, f"grad_swap_{gid}")}

    def rs_mid(gid, after):
        r = rs[gid]
        parts, others = _not_before((r["parts"], r["others"]), after)
        r["chip"] = _by_shape(_add_half, parts, others, core)
        r["slots"] = _scatter_chips(r["chip"], 12 + gid, f"grad_scatter_{gid}")
        return r["chip"]

    def rs_end(gid, after):
        r = rs[gid]
        slots, mine = _not_before((r["slots"], r["chip"]), after)
        r["own"] = _by_shape(_sum_slots, slots, mine, chip)
        r["recv"] = _join_halves(r["own"], 17 + gid, f"grad_join_{gid}")
        return r["own"]

    d_pre, d_post = {}, {}

    def ffn_bwd(l, j, gid, dh_out, extra=(), then=None):
        h, n, g, u, a, f = saved[("ffn", l, j)]
        wg, wu, wd = ffn_w[(l, j)]
        df, dg, du, d_post[(l, j)] = _ffn_dact(f, dh_out, row(post_g[l, j]), wd, g, u)
        dwd = _ffn_dw_down(a, df)
        dwg, dwu = _ffn_dw_in(n, dg, du)
        parts = [p.reshape(N_SHARDS, 2, p.shape[1] // 2, p.shape[2]) for p in (dwg, dwu, dwd)]
        rs_launch(gid, parts)
        dg, du = _not_before((dg, du), parts)
        res = _ffn_dn(dg, du, wg, wu, h, dh_out, [(row(pre_g[l, j]), None)] + list(extra),
                      None if then is None else (then[0], then[1], 1.0))
        if then is None:
            dh, d_pre[(l, j)], *rest = res
            return dh, rest
        dh, dm, d_pre[(l, j)], *rest = res
        return dh, [dm, rest[-1]] + rest[:-1]

    dh5, (dm1, g_mix_post1) = ffn_bwd(1, 1, 0, dy, then=(m1, row(w["mix_post_g"][1])))
    dh5, dm1 = _not_before((dh5, dm1), rs_mid(0, dh5))

    do = _mm2d("attn_out_dx", [(dm1, w_o, "nt", 0)], [(w_o.shape[0], BF16)])[0]
    g_w_o = _mm2d("attn_out_dw", [(o, dm1, "tn", 0)], [(d, BF16)])[0]
    dq_n, dk_n, dv_h, dq_c, dq_s, dk_r = _attn_bwd(q_n, q_r, k_n, v_h, k_r, do, lse, cos2, sin2, seq)
    dqn = _mm2d("q_up_dx", [(dq_n, w_qn, "nt", 0)], [(q_rank, F32)])[0]
    heads_x = ((N_HEADS, tm, QK_ROPE), lambda i, j, k: (0, i, 0))
    heads_w = ((N_HEADS, q_rank, QK_ROPE), lambda i, j, k: (0, 0, 0))
    q_row = ((tm, q_rank), lambda i, j, k: (i, 0))
    dqn = _mm("q_rope_dx", (t // tm, 1, 1), [(dq_c, *heads_x), (w_qr, *heads_w), (dq_s, *heads_x), (w_qr_rot, *heads_w)],
              [(0, 1, 0, "nt"), (2, 3, 0, "nt")], [(tm, q_rank)], [((t, q_rank), F32, *q_row)],
              lambda accs, ex: [accs[0] + ex[0]], [(dqn, *q_row)], inner=N_HEADS)[0]
    g_qn = _mm2d("q_up_dw", [(qn, dq_n, "tn", 0)], [(w_qn.shape[1], F32)])[0]
    g_qr, g_qr_rot = [g.transpose(0, 2, 1) for g in _q_rope_dw(qn, dq_c, dq_s)]
    dqd, g_q_norm = _norm_out_bwd("q_norm_bwd", qd, dqn, row(w["mla_q_norm_g"][0]), 1.0)

    def pre_norm_bwd(accs, ex):
        dx, dgain = _rms_bwd(ex[0], ex[2], accs[0])
        return [ex[1] + dx, dgain]

    dh4, g_mix_pre1 = _mm2d("q_down_dx", [(dqd, w_dq, "nt", 0)], [(d, F32)], pre_norm_bwd, [h4, dh5],
                            [row(w["mix_pre_g"][1])], sums=[(1, d)])
    g_w_dq = _mm2d("q_down_dw", [(n4, dqd, "tn", 0)], [(q_rank, BF16)])[0]

    dc_n = _mm2d("kv_up_dx", [(dk_n, w_k, "nt", 0), (dv_h, w_v, "nt", 0)], [(KV_RANK, F32)])[0]
    g_wk, g_wv = _mm2d("kv_up_dw", [(c_n, dk_n, "tn", 0), (c_n, dv_h, "tn", 1)], [(w_k.shape[1], F32), (w_v.shape[1], F32)])
    dc, g_kv_a = _norm_out_bwd("kv_a_norm_bwd", c_raw, dc_n, row(w["kv_a_norm_g"]), 1.0)
    dkr_c, dkr_s = _rope_bwd(dk_r, cos2, sin2)
    dn3kv = _mm2d("kv_down_dx", [(dc, w_c, "nt", 0), (dkr_c, w_kr, "nt", 0), (dkr_s, w_kr_rot, "nt", 0)], [(d, F32)])[0]
    g_wc, g_wkr, g_wkr_rot = _mm2d("kv_down_dw", [(n3kv, dc, "tn", 0), (n3kv, dkr_c, "tn", 1), (n3kv, dkr_s, "tn", 2)],
                                   [(KV_RANK, F32), (QK_ROPE, F32), (QK_ROPE, F32)])

    dh4 = _not_before(dh4, rs_end(0, dh4))
    dh3, (g_kv_norm,) = ffn_bwd(1, 0, 1, dh4, extra=[(row(w["kv_norm_g"]), dn3kv)])
    dh3 = _not_before(dh3, rs_mid(1, dh3))
    dh2, (dm0, g_mix_post0) = ffn_bwd(0, 1, 2, dh3, then=(m0, row(w["mix_post_g"][0])))
    dh2, dm0 = _not_before((dh2, dm0), (rs_end(1, dh2), rs_mid(2, dh2)))

    d_uv = _mm2d("gmlp_out_dx", [(dm0, w_out, "nt", 0)], [(half, F32)])[0]
    g_w_out = _mm2d("gmlp_out_dw", [(uv, dm0, "tn", 0)], [(d, BF16)])[0]
    dzp, g_ln_g, g_ln_b, g_w_s, g_b_s = _sgu_bwd(zp, d_uv, ln_g, ln_b, w_s, bsb)
    dh1, g_mix_pre0 = _mm2d("gmlp_in_dx", [(dzp, w_in, "nt", 0)], [(d, F32)], pre_norm_bwd, [h1, dh2],
                            [row(w["mix_pre_g"][0])], sums=[(1, d)])
    tk, tmw, w_cols = min(t, K_TILE), min(d, ROW_TILE), w_in.shape[1] // N_SHARDS
    g_w_in = _mm("gmlp_in_dw", (d // tmw, N_SHARDS, t // tk),
                 [(n1, (tk, tmw), lambda i, j, k: (k, i)), (dzp, (tk, w_cols), lambda i, j, k: (k, j))],
                 [(0, 1, 0, "tn")], [(tmw, w_cols)],
                 [((N_SHARDS, d, w_cols), BF16, (None, tmw, w_cols), lambda i, j, k: (j, i, 0))], lambda accs, ex: accs)[0]

    g_w_dkv = jnp.concatenate([g_wc, g_wkr + _unrot_cols(g_wkr_rot)], axis=1).astype(BF16)
    direct = {"gmlp_w_in": g_w_in, "gmlp_w_out": g_w_out, "mla_w_o": g_w_o, "mla_w_dq": g_w_dq, "w_dkv": g_w_dkv}
    direct = {n: g.reshape(N_SHARDS, -1, g.shape[-1]) for n, g in direct.items()}
    part = {
        "w_ukv": jnp.stack([g_wk.reshape(KV_RANK, N_HEADS, QK_NOPE), g_wv.reshape(KV_RANK, N_HEADS, V_DIM)],
                           axis=2).reshape(KV_RANK, -1),
        "mla_w_uq": jnp.concatenate(
            [g_qn.reshape(q_rank, N_HEADS, QK_NOPE),
             (g_qr + _unrot_cols(g_qr_rot)).transpose(1, 0, 2)],
            axis=-1).reshape(1, q_rank, -1),
        "gmlp_ln_g": g_ln_g, "gmlp_ln_b": g_ln_b,
        "mix_pre_g": jnp.concatenate([g_mix_pre0, g_mix_pre1]), "mix_post_g": jnp.concatenate([g_mix_post0, g_mix_post1]),
        "gmlp_w_s": g_w_s[None], "gmlp_b_s": g_b_s.reshape(1, GROUPS, CHUNK),
        "kv_norm_g": g_kv_norm.reshape(-1), "kv_a_norm_g": g_kv_a.reshape(-1), "mla_q_norm_g": g_q_norm,
    }

    sharded = [e for e in mats + vecs if e[0] in part]
    n_rep = sum(w[n].size for n in replicated)
    pieces = [_split_shards(part[n], ax) for n, ax in sharded]
    pieces.append(jnp.concatenate([part[n].reshape(-1) for n in replicated]).reshape(N_SHARDS, -1))
    piece_rows = [_round_up(-(-p.shape[1] // PACK_WIDTH), 16) for p in pieces]
    piece_start = [sum(piece_rows[:q]) for q in range(len(pieces))]
    rows = _round_up(sum(piece_rows), 32)
    piece_rows[-1] += rows - sum(piece_rows)
    small = jnp.concatenate(
        [jnp.pad(p, ((0, 0), (0, r * PACK_WIDTH - p.shape[1]))).astype(BF16).reshape(N_SHARDS, r, PACK_WIDTH)
         for p, r in zip(pieces, piece_rows)], axis=1).reshape(N_SHARDS, 2, rows // 2, PACK_WIDTH)

    rs_launch(3, [g.reshape(N_SHARDS, 2, g.shape[1] // 2, g.shape[2]) for g in direct.values()] + [small])
    dh1 = _not_before(dh1, rs_end(2, dh1))
    dh1 = _not_before(dh1, rs_mid(3, dh1))
    dx, _ = ffn_bwd(0, 0, 4, dh1)

    own3 = rs_end(3, dx)
    launched = rs_mid(4, (dx, own3))
    lj = [(l, j) for l in range(2) for j in range(2)]
    tiny = jnp.concatenate([d_pre[k] for k in lj] + [d_post[k] for k in lj] + [jnp.tile(loss_sum, (1, d // 128))])
    tiny = _all_sum(_not_before(tiny.reshape(-1, 128), launched)).reshape(-1, d)
    loss, tiny = tiny[-1, 0], tiny[:-1].reshape(2, 2, 2, d)
    shard_cols = d // N_SHARDS
    grads = {"ffn_pre_g": lax.dynamic_slice_in_dim(tiny[0], chip[0] * shard_cols, shard_cols, axis=2),
             "ffn_post_g": lax.dynamic_slice_in_dim(tiny[1], chip[0] * shard_cols, shard_cols, axis=2)}
    own_small, recv_small = own3[-1], rs[3]["recv"][-1]
    delta, new_m, new_v = {}, {}, {}
    for q, n in enumerate(direct):
        lead = lambda a: a.reshape((1, 1) + a.shape[-2:])
        upd = _adamw_halves(lead(w[n]), lead(mom[n]), lead(var[n]), 0, 0, rs[3]["own"][q], rs[3]["recv"][q], core, None)
        grads[n], delta[n], new_m[n], new_v[n] = [o.reshape(w[n].shape) for o in upd]
    g_small = jnp.where(core[0] == 0, jnp.concatenate([own_small, recv_small]), jnp.concatenate([recv_small, own_small]))
    g_rep = _gather_rows(g_small, piece_start[-1], piece_rows[-1], 22)
    for (n, _), start in zip(sharded, piece_start):
        grads[n] = g_small[start:start + -(-w[n].size // PACK_WIDTH)].reshape(-1)[:w[n].size].reshape(w[n].shape)
    rep_vec = g_rep.reshape(N_SHARDS, -1)[:, :n_rep // N_SHARDS].reshape(-1)
    for n, g in zip(replicated, _split_flat(rep_vec, [w[n].shape for n in replicated])):
        grads[n] = g

    for n in names:
        if n not in ffn_names and n not in delta:
            delta[n], new_m[n], new_v[n] = _adamw(w[n], grads[n], mom[n], var[n])
    chain = {n: None for n in ffn_names}

    def ffn_update(gid, l, j):
        for q, n in enumerate(ffn_names):
            chain[n] = _adamw_halves(oriented(w[n], n), oriented(mom[n], n), oriented(var[n], n), l, j,
                                     rs[gid]["own"][q], rs[gid]["recv"][q], core, chain[n])

    ffn_update(0, 1, 1)
    ffn_update(1, 1, 0)
    ffn_update(2, 0, 1)
    rs_end(4, ([delta[n] for n in delta], [chain[n] for n in ffn_names]))
    ffn_update(4, 0, 0)
    for n in ffn_names:
        grads[n], delta[n], new_m[n], new_v[n] = [oriented(o, n) for o in chain[n]]
    return (loss, dx.reshape(x.shape), *[grads[n] for n in names], *[delta[n] for n in names],
            *[new_m[n] for n in names], *[new_v[n] for n in names])
```

```python
import math

import jax
import jax.numpy as jnp
from jax import lax
from jax.experimental import pallas as pl
from jax.experimental.pallas import tpu as pltpu
from jax.experimental.pallas import tpu_sc as plsc

F32, BF16 = jnp.float32, jnp.bfloat16

RMS_EPS, LN_EPS, NEG_INF = 1e-6, 1e-5, -1e30
N_HEADS, QK_NOPE, QK_ROPE, V_DIM, KV_RANK = 8, 128, 64, 128, 256
CHUNK, GROUPS = 128, 16
ROPE_THETA = 10000.0
ADAM_LR, ADAM_B1, ADAM_B2, ADAM_EPS, ADAM_WD, ADAM_STEP = 0.001, 0.9, 0.999, 1e-08, 0.01, 10
N_SHARDS = 4

VMEM_LIMIT_BYTES = 48 * 1024 * 1024
ROW_TILE = 512
K_TILE = 2048
PACK_WIDTH = 1024

_DN = {"nn": (((1,), (0,)), ((), ())), "nt": (((1,), (1,)), ((), ())), "tn": (((0,), (0,)), ((), ()))}
_MESH = pl.DeviceIdType.MESH
_ANY = pl.BlockSpec(memory_space=pl.ANY)


def _params(sem):
    return pltpu.CompilerParams(dimension_semantics=sem, vmem_limit_bytes=VMEM_LIMIT_BYTES)


def _mm(name, grid, ins, pairs, acc_shapes, outs, epilogue, extras=(), inner=0, sums=()):
    n_in, n_ex, n_out = len(ins), len(extras), len(outs)
    gk = grid[2]

    def body(*refs):
        in_refs, ex_refs = refs[:n_in], refs[n_in:n_in + n_ex]
        out_refs = refs[n_in + n_ex:n_in + n_ex + n_out]
        sum_refs = refs[n_in + n_ex + n_out:n_in + n_ex + n_out + len(sums)]
        acc_refs = refs[n_in + n_ex + n_out + len(sums):]
        parts = [None] * len(acc_shapes)
        for a, b, c, dims in pairs:
            for s in range(max(inner, 1)):
                lhs, rhs = (in_refs[a][s], in_refs[b][s]) if inner else (in_refs[a][...], in_refs[b][...])
                p = lax.dot_general(lhs, rhs, _DN[dims], preferred_element_type=F32)
                parts[c] = p if parts[c] is None else parts[c] + p

        def finish(accs):
            vals = epilogue(accs, [r[...] for r in ex_refs])
            for r, v in zip(out_refs, vals):
                r[...] = v.astype(r.dtype)
            first = (pl.program_id(0) == 0) & (pl.program_id(1) == 0)
            for r, v in zip(sum_refs, vals[n_out:]):
                @pl.when(first)
                def _():
                    r[...] = v

                @pl.when(jnp.logical_not(first))
                def _():
                    r[...] += v

        if gk == 1:
            finish(parts)
        else:
            k = pl.program_id(2)

            @pl.when(k == 0)
            def _():
                for r, p in zip(acc_refs, parts):
                    r[...] = p

            @pl.when(k > 0)
            def _():
                for r, p in zip(acc_refs, parts):
                    r[...] += p

            @pl.when(k == gk - 1)
            def _():
                finish([r[...] for r in acc_refs])

    return pl.pallas_call(
        body,
        out_shape=[jax.ShapeDtypeStruct(s, d) for s, d, _, _ in outs] + [jax.ShapeDtypeStruct(s, F32) for s in sums],
        grid=grid,
        in_specs=[pl.BlockSpec(bs, im) for _, bs, im in list(ins) + list(extras)],
        out_specs=[pl.BlockSpec(bs, im) for _, _, bs, im in outs]
        + [pl.BlockSpec(s, lambda i, j, k, nd=len(s): (0,) * nd) for s in sums],
        scratch_shapes=[pltpu.VMEM(s, F32) for s in acc_shapes] if gk > 1 else [],
        name=name,
        compiler_params=_params(("arbitrary",) * 3 if sums else ("parallel", "parallel", "arbitrary")),
    )(*[a for a, _, _ in ins], *[a for a, _, _ in extras])


def _mm2d(name, pairs, outs, epilogue=None, row_extras=(), vec_extras=(), sums=(), n_outer=False):
    def mk(a, dims):
        return (a.shape[0], a.shape[1]) if dims[0] == "n" else (a.shape[1], a.shape[0])

    def nk(b, dims):
        return (b.shape[1], b.shape[0]) if dims[1] == "n" else (b.shape[0], b.shape[1])

    m = mk(pairs[0][0], pairs[0][2])[0]
    ks = [mk(a, d)[1] for a, _, d, _ in pairs]
    n_acc = 1 + max(p[3] for p in pairs)
    acc_n = [None] * n_acc
    for a, b, d, c in pairs:
        assert mk(a, d)[0] == m and nk(b, d)[1] == mk(a, d)[1]
        acc_n[c] = nk(b, d)[0]
    tm = min(m, ROW_TILE)
    if len(set(ks)) == 1 and ks[0] > 1024:
        tk = K_TILE if ks[0] % K_TILE == 0 else 512
        tks, gk = [tk] * len(pairs), ks[0] // tk
    else:
        tks, gk = ks, 1
    if len(set(acc_n)) == 1 and acc_n[0] > 1024:
        tns, gj = [1024] * n_acc, acc_n[0] // 1024
    else:
        tns, gj = acc_n, 1

    ins, plist = [], []
    for (a, b, d, c), tk in zip(pairs, tks):
        tn = tns[c]
        a_spec = ((tm, tk), lambda i, j, k: (i, k)) if d[0] == "n" else ((tk, tm), lambda i, j, k: (k, i))
        b_spec = ((tk, tn), lambda i, j, k: (k, j)) if d[1] == "n" else ((tn, tk), lambda i, j, k: (j, k))
        ins += [(a, *a_spec), (b, *b_spec)]
        plist.append((len(ins) - 2, len(ins) - 1, c, d))
    extras = [(r, (tm, r.shape[1]), lambda i, j, k: (i, 0)) for r in row_extras]
    extras += [(v, v.shape, lambda i, j, k: (0, 0)) for v in vec_extras]
    out_specs = []
    for n, dt in outs:
        bn = 1024 if (gj > 1) else n
        out_specs.append(((m, n), dt, (tm, bn), lambda i, j, k: (i, j)))
    if epilogue is None:
        epilogue = lambda accs, ex: accs
    grid = (m // tm, gj, gk)
    if n_outer:
        swap = lambda spec: spec[:-1] + ((lambda f: lambda j, i, k: f(i, j, k))(spec[-1]),)
        ins, extras, out_specs, grid = [swap(x) for x in ins], [swap(x) for x in extras], [swap(x) for x in out_specs], (gj, m // tm, gk)
    return _mm(name, grid, ins, plist, [(tm, tn) for tn in tns], out_specs, epilogue, extras, sums=sums)


def _rms(x, g):
    return x * lax.rsqrt(jnp.mean(x * x, axis=-1, keepdims=True) + RMS_EPS) * g


def _rms_bwd(x, g, dy):
    r = lax.rsqrt(jnp.mean(x * x, axis=-1, keepdims=True) + RMS_EPS)
    gy = dy * g
    dx = r * gy - x * (r * r * r) * jnp.mean(gy * x, axis=-1, keepdims=True)
    return dx, jnp.sum(dy * x * r, axis=0, keepdims=True)


def _sigmoid(x):
    return 0.5 * (1.0 + jnp.tanh(0.5 * x))


_GELU_C = math.sqrt(2.0 / math.pi)


def _gelu(x):
    return x * (0.5 * (1.0 + jnp.tanh(_GELU_C * (x + 0.044715 * (x * x * x)))))


def _gelu_grad(x):
    t = jnp.tanh(_GELU_C * (x + 0.044715 * (x * x * x)))
    return 0.5 * (1.0 + t) + 0.5 * x * (1.0 - t * t) * (_GELU_C * (1.0 + 3.0 * 0.044715 * (x * x)))


def _rows(name, row_ins, vec_ins, fn, row_outs, acc_outs=()):
    t = row_ins[0].shape[0]
    tm = min(t, ROW_TILE)
    nr, nv, no = len(row_ins), len(vec_ins), len(row_outs)

    def body(*refs):
        outs, incs = fn([r[...] for r in refs[:nr]], [r[...] for r in refs[nr:nr + nv]])
        for r, v in zip(refs[nr + nv:nr + nv + no], outs):
            r[...] = v.astype(r.dtype)
        i = pl.program_id(0)
        for r, v in zip(refs[nr + nv + no:], incs):
            @pl.when(i == 0)
            def _():
                r[...] = v

            @pl.when(i > 0)
            def _():
                r[...] += v

    in_specs = [pl.BlockSpec((tm, a.shape[1]), lambda i: (i, 0)) for a in row_ins]
    in_specs += [pl.BlockSpec(v.shape, lambda i, nd=v.ndim: (0,) * nd) for v in vec_ins]
    out_shape = [jax.ShapeDtypeStruct((t, c), dt) for c, dt in row_outs]
    out_shape += [jax.ShapeDtypeStruct(s, F32) for s in acc_outs]
    out_specs = [pl.BlockSpec((tm, c), lambda i: (i, 0)) for c, _ in row_outs]
    out_specs += [pl.BlockSpec(s, lambda i, nd=len(s): (0,) * nd) for s in acc_outs]
    return pl.pallas_call(body, out_shape=out_shape, grid=(t // tm,), in_specs=in_specs, out_specs=out_specs,
                          name=name, compiler_params=_params(("arbitrary",)))(*row_ins, *vec_ins)


def _rms_fwd(x, g):
    return _rows("rms_fwd", [x], [g], lambda r, v: ([_rms(r[0], v[0])], []), [(x.shape[1], BF16)])[0]


def _norm_out_bwd(name, f, d_out, g, scale):
    def fn(r, v):
        dx, dg = _rms_bwd(r[0], v[0], r[1] * scale)
        return [dx], [dg]

    c = f.shape[1]
    return _rows(name, [f, d_out], [g], fn, [(c, BF16)], [(1, c)])


def _rope_bwd(dk, cos2, sin2):
    c = dk.shape[1]
    return _rows("rope_bwd", [dk, cos2, sin2], [], lambda r, v: ([r[0] * r[1], r[0] * r[2]], []),
                 [(c, BF16), (c, BF16)])


def _ffn_up(n, wg, wu):
    t, d = n.shape
    fs = wg.shape[-2]
    tm = min(t, ROW_TILE // 2)

    def body(n_ref, wg_ref, wu_ref, g_ref, u_ref, a_ref):
        x = n_ref[...]
        for s in range(N_SHARDS):
            g = lax.dot_general(x, wg_ref[s], _DN["nt"], preferred_element_type=F32)
            u = lax.dot_general(x, wu_ref[s], _DN["nt"], preferred_element_type=F32)
            g_ref[s] = g.astype(BF16)
            u_ref[s] = u.astype(BF16)
            a_ref[s] = (g * _sigmoid(g) * u).astype(BF16)

    hid = pl.BlockSpec((N_SHARDS, tm, fs), lambda i: (0, i, 0))
    whole = pl.BlockSpec(wg.shape, lambda i: (0, 0, 0))
    return pl.pallas_call(body, out_shape=[jax.ShapeDtypeStruct((N_SHARDS, t, fs), BF16)] * 3, grid=(t // tm,),
                          in_specs=[pl.BlockSpec((tm, d), lambda i: (i, 0)), whole, whole], out_specs=[hid] * 3,
                          name="ffn_up", compiler_params=_params(("parallel",)))(n, wg, wu)


def _down(name, a_in, w_in, gk, h, post_g, next_gs, scale, inner=0, target=None):
    t, d = h.shape
    tm = min(t, ROW_TILE)
    kn = 0 if target is not None else next_gs.shape[0]

    def epi(accs, ex):
        f, hv, pg, last = accs[0], ex[0], ex[1], ex[2]
        hn = hv + scale * _rms(f, pg)
        if target is None:
            return [f, hn] + [_rms(hn, last[q:q + 1]) for q in range(kn)]
        e = hn - last
        s = jnp.sum(jnp.sum(e * e, axis=1, keepdims=True), axis=0, keepdims=True) * (0.5 / d)
        return [f, hn, e * (1.0 / d), jnp.broadcast_to(s, (1, 128))]

    row = ((tm, d), lambda i, j, k: (i, 0))
    extras = [(h, *row), (post_g, (1, d), lambda i, j, k: (0, 0))]
    if target is None:
        outs = [((t, d), F32, *row), ((t, d), F32, *row)] + [((t, d), BF16, *row)] * kn
        extras.append((next_gs, (kn, d), lambda i, j, k: (0, 0)))
    else:
        outs = [((t, d), F32, *row)] * 3
        extras.append((target, *row))
    return _mm(name, (t // tm, 1, gk), [a_in, w_in], [(0, 1, 0, "nn")], [(tm, d)], outs, epi, extras, inner,
               sums=[(1, 128)] if target is not None else ())


def _ffn_down(a, wd, h, post_g, next_gs, target=None):
    t, d = h.shape
    fs = a.shape[-1]
    tm = min(t, ROW_TILE)
    return _down("ffn_down", (a, (N_SHARDS, tm, fs), lambda i, _, k: (0, i, 0)),
                 (wd, (N_SHARDS, fs, d), lambda i, _, k: (0, 0, 0)), 1, h, post_g, next_gs, 0.5, N_SHARDS, target)


def _ffn_dact(f, d_out, post_g, wd, g, u):
    t, d = f.shape
    fs = g.shape[-1]
    tm = min(t, ROW_TILE // 2)

    def body(f_ref, do_ref, pg_ref, wd_ref, g_ref, u_ref, df_ref, dg_ref, du_ref, dpg_ref):
        dfv, dpg = _rms_bwd(f_ref[...], pg_ref[...], do_ref[...] * 0.5)
        dfb = dfv.astype(BF16)
        df_ref[...] = dfb
        i = pl.program_id(0)

        @pl.when(i == 0)
        def _():
            dpg_ref[...] = dpg

        @pl.when(i > 0)
        def _():
            dpg_ref[...] += dpg

        for s in range(N_SHARDS):
            da = lax.dot_general(dfb, wd_ref[s], _DN["nt"], preferred_element_type=F32)
            gv, uv = g_ref[s].astype(F32), u_ref[s].astype(F32)
            sg = _sigmoid(gv)
            dg_ref[s] = (da * uv * (sg * (1.0 + gv * (1.0 - sg)))).astype(BF16)
            du_ref[s] = (da * (gv * sg)).astype(BF16)

    row = pl.BlockSpec((tm, d), lambda i: (i, 0))
    hid = pl.BlockSpec((N_SHARDS, tm, fs), lambda i: (0, i, 0))
    vec = pl.BlockSpec((1, d), lambda i: (0, 0))
    hid_shape = jax.ShapeDtypeStruct((N_SHARDS, t, fs), BF16)
    return pl.pallas_call(
        body, out_shape=[jax.ShapeDtypeStruct((t, d), BF16), hid_shape, hid_shape, jax.ShapeDtypeStruct((1, d), F32)],
        grid=(t // tm,), in_specs=[row, row, vec, pl.BlockSpec(wd.shape, lambda i: (0, 0, 0)), hid, hid],
        out_specs=[row, hid, hid, vec], name="ffn_dact", compiler_params=_params(("arbitrary",)))(
            f, d_out, post_g, wd, g, u)


def _ffn_dn(dg, du, wg, wu, h, d_res, branches, then=None):
    _, t, fs = dg.shape
    d = wg.shape[-1]
    tm = min(t, ROW_TILE)
    nb = len(branches)
    a_spec = ((N_SHARDS, tm, fs), lambda i, _, k: (0, i, 0))
    w_spec = ((N_SHARDS, fs, d), lambda i, _, k: (0, 0, 0))
    row = ((tm, d), lambda i, _, k: (i, 0))
    vec = ((1, d), lambda i, _, k: (0, 0))

    def epi(accs, ex):
        hv, dh = ex[0], ex[1]
        dns, gs = [accs[0]] + ex[2:1 + nb], ex[1 + nb:]
        dgs = []
        for dn, g in zip(dns, gs[:nb]):
            dx, dgv = _rms_bwd(hv, g, dn)
            dh = dh + dx
            dgs.append(dgv)
        if then is None:
            return [dh] + dgs
        dm, dgm = _rms_bwd(ex[-1], gs[nb], dh * then[2])
        return [dh, dm] + dgs + [dgm]

    extras = [(h, *row), (d_res, *row)] + [(dn, *row) for _, dn in branches[1:]] + [(g, *vec) for g, _ in branches]
    outs = [((t, d), F32, *row)]
    if then is not None:
        extras += [(then[1], *vec), (then[0], *row)]
        outs.append(((t, d), BF16, *row))
    return _mm("ffn_dn", (t // tm, 1, 1), [(dg, *a_spec), (wg, *w_spec), (du, *a_spec), (wu, *w_spec)],
               [(0, 1, 0, "nn"), (2, 3, 0, "nn")], [(tm, d)], outs, epi, extras, inner=N_SHARDS,
               sums=[(1, d)] * (nb + (then is not None)))


def _ffn_dw_in(n, dg, du):
    _, t, fs = dg.shape
    d = n.shape[1]
    tk = min(t, K_TILE)
    a_spec = ((None, tk, fs), lambda s, _, k: (s, k, 0))
    o_spec = ((None, fs, d), lambda s, _, k: (s, 0, 0))
    outs = [((N_SHARDS, fs, d), BF16, *o_spec)] * 2
    return _mm("ffn_dw_in", (N_SHARDS, 1, t // tk), [(dg, *a_spec), (du, *a_spec), (n, (tk, d), lambda s, _, k: (k, 0))],
               [(0, 2, 0, "tn"), (1, 2, 1, "tn")], [(fs, d)] * 2, outs, lambda accs, ex: accs)


def _ffn_dw_down(a, df):
    _, t, fs = a.shape
    d = df.shape[1]
    tk = min(t, K_TILE)
    outs = [((N_SHARDS, fs, d), BF16, (None, fs, d), lambda s, _, k: (s, 0, 0))]
    return _mm("ffn_dw_down", (N_SHARDS, 1, t // tk),
               [(a, (None, tk, fs), lambda s, _, k: (s, k, 0)), (df, (tk, d), lambda s, _, k: (k, 0))],
               [(0, 1, 0, "tn")], [(fs, d)], outs, lambda accs, ex: accs)[0]


def _causal_weight(w):
    row = lax.broadcasted_iota(jnp.int32, (CHUNK, CHUNK), 0)
    col = lax.broadcasted_iota(jnp.int32, (CHUNK, CHUNK), 1)
    return row >= col, jnp.where(row >= col, w, 0.0).astype(BF16)


def _layer_norm(v, g, b):
    xc = v - jnp.mean(v, axis=-1, keepdims=True)
    rstd = lax.rsqrt(jnp.mean(xc * xc, axis=-1, keepdims=True) + LN_EPS)
    xhat = xc * rstd
    return xhat, rstd, xhat * g + b


def _sgu_specs(t, half, tm):
    return [pl.BlockSpec((tm, half), lambda i: (i, 0)), pl.BlockSpec((tm, half), lambda i: (i, 1))]


def _sgu_fwd(zp, ln_g, ln_b, w_s, bsb):
    t, half = zp.shape[0], zp.shape[1] // 2
    tm = min(t, 2 * CHUNK)

    def body(u_ref, v_ref, g_ref, b_ref, w_ref, bs_ref, o_ref):
        u = _gelu(u_ref[...])
        _, _, vn = _layer_norm(_gelu(v_ref[...]), g_ref[...], b_ref[...])
        vb = vn.astype(BF16)
        for g in range(GROUPS):
            _, wm = _causal_weight(w_ref[g])
            cols = slice(g * CHUNK, (g + 1) * CHUNK)
            for c in range(tm // CHUNK):
                rows = slice(c * CHUNK, (c + 1) * CHUNK)
                sv = jnp.dot(wm, vb[rows, cols], preferred_element_type=F32) + bs_ref[g]
                o_ref[rows, cols] = (u[rows, cols] * sv).astype(BF16)

    whole = lambda a: pl.BlockSpec(a.shape, lambda i, nd=a.ndim: (0,) * nd)
    return pl.pallas_call(
        body, out_shape=jax.ShapeDtypeStruct((t, half), BF16), grid=(t // tm,),
        in_specs=_sgu_specs(t, half, tm) + [whole(ln_g), whole(ln_b), whole(w_s), whole(bsb)],
        out_specs=pl.BlockSpec((tm, half), lambda i: (i, 0)), name="sgu_fwd",
        compiler_params=_params(("arbitrary",)))(zp, zp, ln_g, ln_b, w_s, bsb)


def _sgu_bwd(zp, d_uv, ln_g, ln_b, w_s, bsb):
    t, half = zp.shape[0], zp.shape[1] // 2
    tm = min(t, 2 * CHUNK)

    def body(u_ref, v_ref, d_ref, g_ref, b_ref, w_ref, bs_ref, dz_ref, dlg_ref, dlb_ref, dws_ref, dbs_ref, dvn_ref):
        i = pl.program_id(0)

        @pl.when(i == 0)
        def _():
            dlg_ref[...] = jnp.zeros_like(dlg_ref)
            dlb_ref[...] = jnp.zeros_like(dlb_ref)
            dws_ref[...] = jnp.zeros_like(dws_ref)
            dbs_ref[...] = jnp.zeros_like(dbs_ref)

        up, vp = u_ref[...], v_ref[...]
        u, gup = _gelu(up), _gelu_grad(up)
        xhat, rstd, vn = _layer_norm(_gelu(vp), g_ref[...], b_ref[...])
        vb = vn.astype(BF16)
        d = d_ref[...]
        for g in range(GROUPS):
            mask, wm = _causal_weight(w_ref[g])
            cols = slice(g * CHUNK, (g + 1) * CHUNK)
            for c in range(tm // CHUNK):
                rows = slice(c * CHUNK, (c + 1) * CHUNK)
                blk = vb[rows, cols]
                sv = jnp.dot(wm, blk, preferred_element_type=F32) + bs_ref[g]
                dblk = d[rows, cols]
                dz_ref[rows, cols] = (dblk * sv * gup[rows, cols]).astype(BF16)
                dsv = dblk * u[rows, cols]
                dsvb = dsv.astype(BF16)
                dvn_ref[rows, cols] = lax.dot_general(wm, dsvb, _DN["tn"], preferred_element_type=F32)
                dw = lax.dot_general(dsvb, blk, _DN["nt"], preferred_element_type=F32)
                dws_ref[g] += jnp.where(mask, dw, 0.0)
                dbs_ref[g] += jnp.sum(dsv, axis=1, keepdims=True)
        dvn = dvn_ref[...]
        dlg_ref[...] += jnp.sum(dvn * xhat, axis=0, keepdims=True)
        dlb_ref[...] += jnp.sum(dvn, axis=0, keepdims=True)
        dxh = dvn * g_ref[...]
        dv = rstd * (dxh - jnp.mean(dxh, axis=-1, keepdims=True)
                     - xhat * jnp.mean(dxh * xhat, axis=-1, keepdims=True))
        dz_ref[:, half:] = (dv * _gelu_grad(vp)).astype(BF16)

    whole = lambda a: pl.BlockSpec(a.shape, lambda i, nd=a.ndim: (0,) * nd)
    wshape = lambda s: pl.BlockSpec(s, lambda i, nd=len(s): (0,) * nd)
    out_shape = [jax.ShapeDtypeStruct((t, 2 * half), BF16), jax.ShapeDtypeStruct((1, half), F32),
                 jax.ShapeDtypeStruct((1, half), F32), jax.ShapeDtypeStruct(w_s.shape, F32),
                 jax.ShapeDtypeStruct((GROUPS, CHUNK, 1), F32)]
    return pl.pallas_call(
        body, out_shape=out_shape, grid=(t // tm,),
        in_specs=_sgu_specs(t, half, tm) + [pl.BlockSpec((tm, half), lambda i: (i, 0)), whole(ln_g), whole(ln_b),
                                            whole(w_s), whole(bsb)],
        out_specs=[pl.BlockSpec((tm, 2 * half), lambda i: (i, 0)), wshape((1, half)), wshape((1, half)),
                   wshape(w_s.shape), wshape((GROUPS, CHUNK, 1))],
        scratch_shapes=[pltpu.VMEM((tm, half), F32)], name="sgu_bwd",
        compiler_params=_params(("arbitrary",)))(zp, zp, d_uv, ln_g, ln_b, w_s, bsb)


_SCALE = (QK_NOPE + QK_ROPE) ** -0.5


def _attn_scores(qn, qr, kn, kr, i, tq, n):
    s = lax.dot_general(qn, kn, _DN["nt"], preferred_element_type=F32)
    s = (s + lax.dot_general(qr, kr, _DN["nt"], preferred_element_type=F32)) * _SCALE
    row = i * tq + lax.broadcasted_iota(jnp.int32, (tq, n), 0)
    col = lax.broadcasted_iota(jnp.int32, (tq, n), 1)
    return jnp.where(col <= row, s, NEG_INF)


def _attn_specs(seq):
    head = lambda b, h: (b, h)
    return dict(
        qn=pl.BlockSpec((seq, QK_NOPE), head),
        qr=pl.BlockSpec((None, seq, QK_ROPE), lambda b, h: (h, b, 0)),
        kr=pl.BlockSpec((seq, QK_ROPE), lambda b, h: (b, 0)),
        lse=pl.BlockSpec((None, seq, 1), lambda b, h: (h, b, 0)),
    )


def _attn_fwd(qn, qr, kn, v, kr, seq):
    t = qn.shape[0]
    tq = min(seq, 4 * CHUNK)
    sp = _attn_specs(seq)

    def body(qn_ref, qr_ref, kn_ref, v_ref, kr_ref, o_ref, lse_ref):
        for i in range(seq // tq):
            rows, n = slice(i * tq, (i + 1) * tq), (i + 1) * tq
            s = _attn_scores(qn_ref[rows, :], qr_ref[rows, :], kn_ref[0:n, :], kr_ref[0:n, :], i, tq, n)
            m = jnp.max(s, axis=-1, keepdims=True)
            p = jnp.exp(s - m)
            l = jnp.sum(p, axis=-1, keepdims=True)
            o_ref[rows, :] = jnp.dot((p / l).astype(BF16), v_ref[0:n, :], preferred_element_type=F32).astype(BF16)
            lse_ref[rows, :] = m + jnp.log(l)

    return pl.pallas_call(
        body, out_shape=[jax.ShapeDtypeStruct((t, N_HEADS * V_DIM), BF16), jax.ShapeDtypeStruct((N_HEADS, t, 1), F32)],
        grid=(t // seq, N_HEADS), in_specs=[sp["qn"], sp["qr"], sp["qn"], sp["qn"], sp["kr"]],
        out_specs=[sp["qn"], sp["lse"]], name="attn_fwd",
        compiler_params=_params(("parallel", "arbitrary")))(qn, qr, kn, v, kr)


def _attn_bwd(qn, qr, kn, v, kr, do, lse, cos2, sin2, seq):
    t = qn.shape[0]
    tq = min(seq, 4 * CHUNK)
    sp = _attn_specs(seq)

    def body(qn_ref, qr_ref, kn_ref, v_ref, kr_ref, do_ref, lse_ref, cos_ref, sin_ref,
             dqn_ref, dkn_ref, dv_ref, dqc_ref, dqs_ref, dkr_ref, dk_acc, dv_acc, dkr_acc):
        dk_acc[...] = jnp.zeros_like(dk_acc)
        dv_acc[...] = jnp.zeros_like(dv_acc)
        dkr_acc[...] = jnp.zeros_like(dkr_acc)
        for i in range(seq // tq):
            rows, n = slice(i * tq, (i + 1) * tq), (i + 1) * tq
            q_n, q_r, d_o = qn_ref[rows, :], qr_ref[rows, :], do_ref[rows, :]
            k_n, k_r = kn_ref[0:n, :], kr_ref[0:n, :]
            s = _attn_scores(q_n, q_r, k_n, k_r, i, tq, n)
            p = jnp.exp(s - lse_ref[rows, :])
            dp = lax.dot_general(d_o, v_ref[0:n, :], _DN["nt"], preferred_element_type=F32)
            ds = (p * (dp - jnp.sum(p * dp, axis=-1, keepdims=True)) * _SCALE).astype(BF16)
            dqn_ref[rows, :] = jnp.dot(ds, k_n, preferred_element_type=F32).astype(BF16)
            dqr = jnp.dot(ds, k_r, preferred_element_type=F32)
            dqc_ref[rows, :] = (dqr * cos_ref[rows, :]).astype(BF16)
            dqs_ref[rows, :] = (dqr * sin_ref[rows, :]).astype(BF16)
            dk_acc[0:n, :] += lax.dot_general(ds, q_n, _DN["tn"], preferred_element_type=F32)
            dkr_acc[0:n, :] += lax.dot_general(ds, q_r, _DN["tn"], preferred_element_type=F32)
            dv_acc[0:n, :] += lax.dot_general(p.astype(BF16), d_o, _DN["tn"], preferred_element_type=F32)
        dkn_ref[...] = dk_acc[...].astype(BF16)
        dv_ref[...] = dv_acc[...].astype(BF16)
        h = pl.program_id(1)

        @pl.when(h == 0)
        def _():
            dkr_ref[...] = dkr_acc[...]

        @pl.when(h > 0)
        def _():
            dkr_ref[...] += dkr_acc[...]

    wide = jax.ShapeDtypeStruct((t, N_HEADS * V_DIM), BF16)
    rope = jax.ShapeDtypeStruct((N_HEADS, t, QK_ROPE), BF16)
    krf = pl.BlockSpec((seq, QK_ROPE), lambda b, h: (b, 0))
    return pl.pallas_call(
        body, out_shape=[wide, wide, wide, rope, rope, jax.ShapeDtypeStruct((t, QK_ROPE), F32)],
        grid=(t // seq, N_HEADS),
        in_specs=[sp["qn"], sp["qr"], sp["qn"], sp["qn"], sp["kr"], sp["qn"], sp["lse"], krf, krf],
        out_specs=[sp["qn"], sp["qn"], sp["qn"], sp["qr"], sp["qr"], krf],
        scratch_shapes=[pltpu.VMEM((seq, QK_NOPE), F32), pltpu.VMEM((seq, V_DIM), F32), pltpu.VMEM((seq, QK_ROPE), F32)],
        name="attn_bwd", compiler_params=_params(("parallel", "arbitrary")))(qn, qr, kn, v, kr, do, lse, cos2, sin2)


def _q_rope(qn, w, w_rot, cos2, sin2):
    t, r = qn.shape
    nh, _, e = w.shape
    tm = min(t, ROW_TILE)

    def body(x_ref, w_ref, wr_ref, c_ref, s_ref, o_ref):
        x = x_ref[...]
        for h in range(nh):
            raw = jnp.dot(x, w_ref[h], preferred_element_type=F32)
            rot = jnp.dot(x, wr_ref[h], preferred_element_type=F32)
            o_ref[h] = (raw * c_ref[...] + rot * s_ref[...]).astype(BF16)

    whole = pl.BlockSpec(w.shape, lambda i: (0, 0, 0))
    rows = pl.BlockSpec((tm, e), lambda i: (i, 0))
    return pl.pallas_call(body, out_shape=jax.ShapeDtypeStruct((nh, t, e), BF16), grid=(t // tm,),
                          in_specs=[pl.BlockSpec((tm, r), lambda i: (i, 0)), whole, whole, rows, rows],
                          out_specs=pl.BlockSpec((nh, tm, e), lambda i: (0, i, 0)), name="q_rope",
                          compiler_params=_params(("parallel",)))(qn, w, w_rot, cos2, sin2)


def _q_rope_dw(qn, dq_c, dq_s):
    t, r = qn.shape
    nh, _, e = dq_c.shape
    tk = min(t, K_TILE)

    def body(x_ref, c_ref, s_ref, gc_ref, gs_ref):
        k = pl.program_id(0)
        x = x_ref[...]
        for h in range(nh):
            pc = lax.dot_general(c_ref[h], x, _DN["tn"], preferred_element_type=F32)
            ps = lax.dot_general(s_ref[h], x, _DN["tn"], preferred_element_type=F32)

            @pl.when(k == 0)
            def _():
                gc_ref[h] = pc
                gs_ref[h] = ps

            @pl.when(k > 0)
            def _():
                gc_ref[h] += pc
                gs_ref[h] += ps

    heads = pl.BlockSpec((nh, tk, e), lambda k: (0, k, 0))
    out = pl.BlockSpec((nh, e, r), lambda k: (0, 0, 0))
    return pl.pallas_call(body, out_shape=[jax.ShapeDtypeStruct((nh, e, r), F32)] * 2, grid=(t // tk,),
                          in_specs=[pl.BlockSpec((tk, r), lambda k: (k, 0)), heads, heads], out_specs=[out, out],
                          name="q_rope_dw", compiler_params=_params(("arbitrary",)))(qn, dq_c, dq_s)


def _row_tile(rows, cols, row_mult=8):
    cap = max(row_mult, (1 << 18) // cols)
    best = rows
    for tr in range(row_mult, min(rows, cap) + 1, row_mult):
        if rows % tr == 0:
            best = tr
    return best if rows > cap else rows


def _adamw_math(w, g, m, v):
    mv = ADAM_B1 * m + (1.0 - ADAM_B1) * g
    vv = ADAM_B2 * v + (1.0 - ADAM_B2) * (g * g)
    m_hat = mv / (1.0 - ADAM_B1 ** ADAM_STEP)
    v_hat = vv / (1.0 - ADAM_B2 ** ADAM_STEP)
    return -ADAM_LR * (m_hat / (jnp.sqrt(v_hat) + ADAM_EPS) + ADAM_WD * w), mv, vv


def _adamw(w, g, m, v):
    shape = w.shape
    c = shape[-1]
    r = w.size // c
    tr = _row_tile(r, c)

    def body(w_ref, g_ref, m_ref, v_ref, d_ref, nm_ref, nv_ref):
        d_ref[...], nm_ref[...], nv_ref[...] = _adamw_math(w_ref[...], g_ref[...], m_ref[...], v_ref[...])

    spec = pl.BlockSpec((tr, c), lambda i: (i, 0))
    outs = pl.pallas_call(body, out_shape=[jax.ShapeDtypeStruct((r, c), F32)] * 3, grid=(r // tr,),
                          in_specs=[spec] * 4, out_specs=[spec] * 3, name="adamw",
                          compiler_params=_params(("parallel",)))(*[a.reshape(r, c) for a in (w, g, m, v)])
    return [o.reshape(shape) for o in outs]


def _adamw_halves(w, m, v, l, j, own, recv, core, prev):
    nl, nj, rows, c = w.shape
    r = rows // 2
    tr = _row_tile(r, c)
    n_prev = 0 if prev is None else 4

    def body(core_ref, w_ref, own_ref, recv_ref, m_ref, v_ref, *rest):
        g_ref, d_ref, nm_ref, nv_ref = rest[n_prev:]
        g = jnp.where(pl.program_id(0) == core_ref[0], own_ref[...], recv_ref[...])
        g_ref[...] = g
        d_ref[...], nm_ref[...], nv_ref[...] = _adamw_math(w_ref[...], g, m_ref[...], v_ref[...])

    nb = r // tr
    slab = pl.BlockSpec((None, None, tr, c), lambda h, i, cr: (l, j, h * nb + i, 0))
    half = pl.BlockSpec((tr, c), lambda h, i, cr: (i, 0))
    grid_spec = pltpu.PrefetchScalarGridSpec(num_scalar_prefetch=1, grid=(2, nb),
                                             in_specs=[slab, half, half, slab, slab] + [_ANY] * n_prev,
                                             out_specs=[slab] * 4)
    return pl.pallas_call(body, out_shape=[jax.ShapeDtypeStruct(w.shape, F32)] * 4, grid_spec=grid_spec,
                          input_output_aliases={6 + q: q for q in range(n_prev)}, name="adamw_halves",
                          compiler_params=_params(("parallel",) * 2))(core, w, own, recv, m, v, *(prev or ()))


def _place():
    x, y, c = lax.axis_index("x"), lax.axis_index("y"), lax.axis_index("c")
    return x, y, c, [(1 - x, y), (x, 1 - y), (1 - x, 1 - y)]


def _dma_sems(*counts):
    return [pltpu.SemaphoreType.DMA((n,)) for n in counts]


def _all_gather(bufs, collective_id, name):
    n = len(bufs)

    def body(*refs):
        ins, outs = refs[:n], refs[n:2 * n]
        send, recv, fsend, frecv, osend, orecv = refs[2 * n:]
        x, y, c, _ = _place()
        xn, yn, sib = (1 - x, y, c), (x, 1 - y, c), (x, y, 1 - c)
        k, kx, ky, kd = 2 * x + y, 2 * (1 - x) + y, 2 * x + 1 - y, 2 * (1 - x) + 1 - y
        _handshake([xn, yn, sib])

        def copy(src, dst, sems, i, to):
            return pltpu.make_async_remote_copy(src, dst, sems[0].at[i], sems[1].at[i], device_id=to, device_id_type=_MESH)

        ici, d2d, own_s = (send, recv), (fsend, frecv), (osend, orecv)
        started = [copy(ins[b], outs[b].at[k], own_s, b, sib) for b in range(n)]
        for first in (True, False):
            for b in range(n):
                mine = outs[b].at[k, c]
                if first:
                    started += [copy(ins[b].at[c, 0], mine.at[0], ici, 6 * b, xn), copy(ins[b].at[c, 1], mine.at[1], ici, 6 * b + 1, yn)]
                else:
                    started += [copy(ins[b].at[c, 1], mine.at[1], ici, 6 * b + 2, xn), copy(ins[b].at[c, 0], mine.at[0], ici, 6 * b + 3, yn)]
        for cp in started:
            cp.start()
        passed = []
        for b in range(n):
            for i, (src_chip, q, to) in enumerate([(kx, 0, yn), (ky, 1, xn)]):
                piece = outs[b].at[src_chip, c, q]
                copy(piece, piece, ici, 6 * b + i, to).wait_recv()
                cp = copy(piece, piece, ici, 6 * b + 4 + i, to)
                cp.start()
                passed.append(cp)
        for b in range(n):
            for i, (src_chip, q) in enumerate([(kx, 1), (ky, 0)]):
                piece = outs[b].at[src_chip, c, q]
                copy(piece, piece, ici, 6 * b + 2 + i, xn).wait_recv()
                half = outs[b].at[src_chip, c]
                cp = copy(half, half, d2d, 3 * b + i, sib)
                cp.start()
                passed.append(cp)
        for b in range(n):
            for i, q in enumerate([0, 1]):
                piece = outs[b].at[kd, c, q]
                copy(piece, piece, ici, 6 * b + 4 + i, xn).wait_recv()
            half = outs[b].at[kd, c]
            cp = copy(half, half, d2d, 3 * b + 2, sib)
            cp.start()
            passed.append(cp)
        for b in range(n):
            for i, src_chip in enumerate([kx, ky, kd]):
                half = outs[b].at[src_chip, 1 - c]
                copy(half, half, d2d, 3 * b + i, sib).wait_recv()
        for cp in started[n:] + passed:
            cp.wait_send()
        for cp in started[:n]:
            cp.wait()

    return _sequencer(body, [jax.ShapeDtypeStruct((N_SHARDS,) + b.shape, b.dtype) for b in bufs],
                      _dma_sems(6 * n, 6 * n, 3 * n, 3 * n, n, n), collective_id, name, bufs)


def _sequencer(body, out_type, sems, collective_id, name, args):
    return pl.kernel(body, out_type=out_type, mesh=plsc.ScalarSubcoreMesh(axis_name="sequencer", num_cores=1),
                     scratch_types=sems, compiler_params=pltpu.CompilerParams(collective_id=collective_id),
                     name=name)(*args)


def _handshake(peers):
    barrier = pltpu.get_barrier_semaphore()
    for peer in peers:
        pl.semaphore_signal(barrier, inc=1, device_id=peer, device_id_type=_MESH)
    pl.semaphore_wait(barrier, len(peers))


def _swap_halves(parts, collective_id, name):
    n = len(parts)

    def body(*refs):
        ins, outs = refs[:n], refs[n:2 * n]
        send, recv = refs[2 * n:]
        x, y, c, _ = _place()
        _handshake([(x, y, 1 - c)])
        cps = [pltpu.make_async_remote_copy(ins[b].at[:, pl.ds(1 - c, 1)], outs[b], send.at[b], recv.at[b],
                                            device_id=(x, y, 1 - c), device_id_type=_MESH) for b in range(n)]
        for cp in cps:
            cp.start()
        for cp in cps:
            cp.wait()

    return _sequencer(body, [jax.ShapeDtypeStruct((N_SHARDS, 1) + p.shape[2:], p.dtype) for p in parts],
                      _dma_sems(n, n), collective_id, name, parts)


def _by_shape(fn, first, second, scalar):
    out, groups = [None] * len(first), {}
    for i, p in enumerate(first):
        groups.setdefault(p.shape, []).append(i)
    for idx in groups.values():
        for i, r in zip(idx, fn([first[i] for i in idx], [second[i] for i in idx], scalar)):
            out[i] = r
    return out


def _add_half(parts, others, core):
    n = len(parts)
    _, _, r, c = parts[0].shape
    tr = _row_tile(r, c, 16)

    def body(core_ref, *refs):
        for q in range(n):
            refs[2 * n + q][...] = (refs[q][...].astype(F32) + refs[n + q][...].astype(F32)).astype(BF16)

    grid_spec = pltpu.PrefetchScalarGridSpec(
        num_scalar_prefetch=1, grid=(N_SHARDS, r // tr),
        in_specs=[pl.BlockSpec((None, None, tr, c), lambda k, i, cr: (k, cr[0], i, 0))] * n
        + [pl.BlockSpec((None, None, tr, c), lambda k, i, cr: (k, 0, i, 0))] * n,
        out_specs=[pl.BlockSpec((None, tr, c), lambda k, i, cr: (k, i, 0))] * n)
    return pl.pallas_call(body, out_shape=[jax.ShapeDtypeStruct((N_SHARDS, r, c), BF16)] * n, grid_spec=grid_spec,
                          name="grad_add_half", compiler_params=_params(("parallel", "parallel")))(core, *parts, *others)


def _scatter_chips(parts, collective_id, name):
    n = len(parts)

    def body(*refs):
        ins, outs = refs[:n], refs[n:2 * n]
        send, recv = refs[2 * n:]
        x, y, c, chips = _place()
        k = 2 * x + y
        _handshake([(px, py, c) for px, py in chips])
        started = []
        for b in range(n):
            for j, (px, py) in enumerate(chips):
                cp = pltpu.make_async_remote_copy(ins[b].at[2 * px + py], outs[b].at[k], send.at[3 * b + j],
                                                  recv.at[3 * b + j], device_id=(px, py, c), device_id_type=_MESH)
                cp.start()
                started.append(cp)
        for b in range(n):
            for j, (px, py) in enumerate(chips):
                got = outs[b].at[2 * px + py]
                pltpu.make_async_remote_copy(got, got, send.at[3 * b + j], recv.at[3 * b + j],
                                             device_id=(px, py, c), device_id_type=_MESH).wait_recv()
        for cp in started:
            cp.wait_send()

    return _sequencer(body, [jax.ShapeDtypeStruct(p.shape, p.dtype) for p in parts], _dma_sems(3 * n, 3 * n),
                      collective_id, name, parts)


def _sum_slots(slots, mine, chip):
    n = len(slots)
    _, r, c = slots[0].shape
    tr = _row_tile(r, c, 16)

    def body(chip_ref, *refs):
        for q in range(n):
            own = refs[5 * q + 4][...].astype(F32)
            v = [jnp.where(chip_ref[0] == s, own, refs[5 * q + s][...].astype(F32)) for s in range(N_SHARDS)]
            refs[5 * n + q][...] = ((v[0] + v[1]) + v[2]) + v[3]

    def slot_spec(s):
        return pl.BlockSpec((None, tr, c), lambda i, kr: (jnp.where(kr[0] == s, (s + 1) % N_SHARDS, s), i, 0))

    per_buffer = [slot_spec(s) for s in range(N_SHARDS)] + [pl.BlockSpec((None, tr, c), lambda i, kr: (kr[0], i, 0))]
    grid_spec = pltpu.PrefetchScalarGridSpec(num_scalar_prefetch=1, grid=(r // tr,), in_specs=per_buffer * n,
                                             out_specs=[pl.BlockSpec((tr, c), lambda i, kr: (i, 0))] * n)
    args = [a for sl, mn in zip(slots, mine) for a in (sl, sl, sl, sl, mn)]
    return pl.pallas_call(body, out_shape=[jax.ShapeDtypeStruct((r, c), F32)] * n, grid_spec=grid_spec,
                          name="grad_sum_slots", compiler_params=_params(("parallel",)))(chip, *args)


def _join_halves(halves, collective_id, name):
    n = len(halves)

    def body(*refs):
        ins, outs = refs[:n], refs[n:2 * n]
        send, recv = refs[2 * n:]
        x, y, c, _ = _place()
        _handshake([(x, y, 1 - c)])
        cps = [pltpu.make_async_remote_copy(ins[b], outs[b], send.at[b], recv.at[b], device_id=(x, y, 1 - c),
                                            device_id_type=_MESH) for b in range(n)]
        for cp in cps:
            cp.start()
        for cp in cps:
            cp.wait()

    return _sequencer(body, [jax.ShapeDtypeStruct(h.shape, F32) for h in halves], _dma_sems(n, n), collective_id,
                      name, halves)


def _gather_rows(buf, start, rows, collective_id):
    def body(in_ref, out_ref, send, recv, lsem):
        x, y, c, chips = _place()
        k = 2 * x + y
        _handshake([(px, py, c) for px, py in chips] + [(x, y, 1 - c)])
        src = in_ref.at[pl.ds(start, rows)]
        local = pltpu.make_async_remote_copy(src, out_ref.at[k], lsem.at[0], lsem.at[1], device_id=(x, y, 1 - c),
                                             device_id_type=_MESH)
        local.start()
        cps = [pltpu.make_async_remote_copy(src, out_ref.at[k], send.at[j], recv.at[j], device_id=(px, py, c),
                                            device_id_type=_MESH) for j, (px, py) in enumerate(chips)]
        for cp in cps:
            cp.start()
        for j, (px, py) in enumerate(chips):
            got = out_ref.at[2 * px + py]
            pltpu.make_async_remote_copy(got, got, send.at[j], recv.at[j], device_id=(px, py, c),
                                         device_id_type=_MESH).wait_recv()
        for cp in cps:
            cp.wait_send()
        local.wait()

    return _sequencer(body, jax.ShapeDtypeStruct((N_SHARDS, rows, buf.shape[1]), F32), _dma_sems(3, 3, 2), collective_id,
                      "gather_replicated_grads", [buf])


def _all_sum(vec):
    r, c = vec.shape
    n_dev = 2 * N_SHARDS

    def body(in_ref, out_ref, slots, send, recv):
        x, y, cc, _ = _place()
        flip = lambda v, bit: 1 - v if bit else v
        peers = [(flip(x, (q >> 2) & 1), flip(y, (q >> 1) & 1), flip(cc, q & 1)) for q in range(1, n_dev)]
        index = lambda p: 4 * p[0] + 2 * p[1] + p[2]
        slots[index((x, y, cc))] = in_ref[...]
        cps = [pltpu.make_async_remote_copy(in_ref, slots.at[index((x, y, cc))], send.at[q], recv.at[q], device_id=p,
                                            device_id_type=_MESH) for q, p in enumerate(peers)]
        for cp in cps:
            cp.start()
        for q, p in enumerate(peers):
            got = slots.at[index(p)]
            pltpu.make_async_remote_copy(got, got, send.at[q], recv.at[q], device_id=p, device_id_type=_MESH).wait_recv()
        for cp in cps:
            cp.wait_send()
        acc = slots[0]
        for s in range(1, n_dev):
            acc = acc + slots[s]
        out_ref[...] = acc

    vmem = pl.BlockSpec(memory_space=pltpu.VMEM)
    return pl.pallas_call(body, out_shape=jax.ShapeDtypeStruct((r, c), F32), in_specs=[vmem], out_specs=vmem,
                          scratch_shapes=[pltpu.VMEM((n_dev, r, c), F32)] + _dma_sems(n_dev - 1, n_dev - 1),
                          name="sum_small_grads")(vec)


def _not_before(value, other):
    return lax.optimization_barrier((value, other))[0]


def _round_up(n, m):
    return -(-n // m) * m


def _pack_flat(vecs, rows, width, dtype):
    flat = jnp.concatenate([v.reshape(-1).astype(dtype) for v in vecs])
    return jnp.pad(flat, (0, rows * width - flat.size)).reshape(rows, width)


def _split_flat(flat, shapes):
    out, off = [], 0
    for s in shapes:
        n = math.prod(s)
        out.append(flat[off:off + n].reshape(s))
        off += n
    return out


def _merge_shards(arr4, axis):
    a = jnp.moveaxis(arr4, 0, axis)
    s = list(a.shape)
    return a.reshape(s[:axis] + [s[axis] * s[axis + 1]] + s[axis + 2:])


def _split_shards(full, axis):
    s = list(full.shape)
    a = full.reshape(s[:axis] + [N_SHARDS, s[axis] // N_SHARDS] + s[axis + 1:])
    return jnp.moveaxis(a, axis, 0).reshape(N_SHARDS, -1)


def _rot_cols(w):
    half = w.shape[-1] // 2
    return jnp.concatenate([-w[..., half:], w[..., :half]], axis=-1)


def _unrot_cols(dw):
    half = dw.shape[-1] // 2
    return jnp.concatenate([dw[..., half:], -dw[..., :half]], axis=-1)


def kernel(x, positions, ffn_pre_g, ffn_post_g, ffn_w_gate, ffn_w_up, ffn_w_down, mix_pre_g, mix_post_g, gmlp_w_in, gmlp_ln_g, gmlp_ln_b, gmlp_w_s, gmlp_b_s, gmlp_w_out, kv_norm_g, w_dkv, kv_a_norm_g, w_ukv, mla_w_dq, mla_q_norm_g, mla_w_uq, mla_w_o, loss_target, m_ffn_pre_g, m_ffn_post_g, m_ffn_w_gate, m_ffn_w_up, m_ffn_w_down, m_mix_pre_g, m_mix_post_g, m_gmlp_w_in, m_gmlp_ln_g, m_gmlp_ln_b, m_gmlp_w_s, m_gmlp_b_s, m_gmlp_w_out, m_kv_norm_g, m_w_dkv, m_kv_a_norm_g, m_w_ukv, m_mla_w_dq, m_mla_q_norm_g, m_mla_w_uq, m_mla_w_o, v_ffn_pre_g, v_ffn_post_g, v_ffn_w_gate, v_ffn_w_up, v_ffn_w_down, v_mix_pre_g, v_mix_post_g, v_gmlp_w_in, v_gmlp_ln_g, v_gmlp_ln_b, v_gmlp_w_s, v_gmlp_b_s, v_gmlp_w_out, v_kv_norm_g, v_w_dkv, v_kv_a_norm_g, v_w_ukv, v_mla_w_dq, v_mla_q_norm_g, v_mla_w_uq, v_mla_w_o):
    names = ["ffn_pre_g", "ffn_post_g", "ffn_w_gate", "ffn_w_up", "ffn_w_down", "mix_pre_g", "mix_post_g", "gmlp_w_in",
             "gmlp_ln_g", "gmlp_ln_b", "gmlp_w_s", "gmlp_b_s", "gmlp_w_out", "kv_norm_g", "w_dkv", "kv_a_norm_g", "w_ukv",
             "mla_w_dq", "mla_q_norm_g", "mla_w_uq", "mla_w_o"]
    env = locals()
    w = {n: env[n] for n in names}
    mom = {n: env["m_" + n] for n in names}
    var = {n: env["v_" + n] for n in names}

    bsz, seq, d = x.shape
    t = bsz * seq
    core = lax.axis_index("c").astype(jnp.int32).reshape(1)

    mats = [("gmlp_w_in", 2), ("gmlp_w_out", 1), ("w_dkv", 0), ("w_ukv", 1), ("mla_w_dq", 1), ("mla_w_uq", 2),
            ("mla_w_o", 1)]
    vecs = [("ffn_pre_g", 2), ("ffn_post_g", 2), ("gmlp_ln_g", 1), ("gmlp_ln_b", 1)]
    replicated = ["mix_pre_g", "mix_post_g", "gmlp_w_s", "gmlp_b_s", "kv_norm_g", "kv_a_norm_g", "mla_q_norm_g"]
    n_mats = sum(w[n].size for n, _ in mats)
    n_vecs = sum(w[n].size for n, _ in vecs)
    mat_rows = _round_up(-(-n_mats // PACK_WIDTH), 64)
    vec_rows = _round_up(-(-n_vecs // 128), 32)
    mat_pack = _pack_flat([w[n] for n, _ in mats], mat_rows, PACK_WIDTH, BF16).reshape(2, 2, mat_rows // 4, PACK_WIDTH)
    vec_pack = _pack_flat([w[n] for n, _ in vecs], vec_rows, 128, F32).reshape(2, 2, vec_rows // 4, 128)
    ffn_names = ("ffn_w_gate", "ffn_w_up", "ffn_w_down")

    def oriented(a, name):
        return a if name == "ffn_w_down" else jnp.swapaxes(a, 2, 3)

    lj = [(l, j) for l in range(2) for j in range(2)]
    plan = [((0, 0), (0, 1), [vec_pack], None), ((0, 0), (2,), [mat_pack], 0), ((0, 1), (0, 1, 2), [], 0),
            ((1, 0), (0, 1, 2), [], 0), ((1, 1), (0, 1, 2), [], 0)]
    ffn_w = {k: [None] * 3 for k in lj}
    landed = []
    for q, ((l, j), which, riders, after) in enumerate(plan):
        shards = [oriented(w[ffn_names[i]], ffn_names[i])[l, j].astype(BF16) for i in which]
        bufs = [s.reshape(2, 2, s.shape[0] // 4, s.shape[1]) for s in shards] + riders
        if after is not None:
            bufs = _not_before(bufs, landed[after])
        got = _all_gather(bufs, q + 1, f"gather_weights_{q}")
        landed.append(got[-1])
        for i, g, s in zip(which, got, shards):
            ffn_w[(l, j)][i] = g.reshape((N_SHARDS,) + s.shape)
        if riders and q == 0:
            vec_all = got[-1]
        if riders and q == 1:
            mat_all = got[-1]

    def unpack(packed, entries):
        flat4, off, out = packed.reshape(N_SHARDS, -1), 0, {}
        for n, ax in entries:
            out[n] = _merge_shards(flat4[:, off:off + w[n].size].reshape((N_SHARDS,) + w[n].shape), ax)
            off += w[n].size
        return out

    full = unpack(vec_all, vecs)
    ln_g, ln_b = full["gmlp_ln_g"], full["gmlp_ln_b"]
    pre_g, post_g = full["ffn_pre_g"], full["ffn_post_g"]
    w_s = w["gmlp_w_s"][0]
    bsb = w["gmlp_b_s"][0][:, :, None]
    row = lambda v: v.reshape(1, -1)

    inv_freq = ROPE_THETA ** (-jnp.arange(0, QK_ROPE, 2, dtype=F32) / QK_ROPE)
    ang = positions.astype(F32).reshape(t, 1) * inv_freq
    cos2 = jnp.concatenate([jnp.cos(ang)] * 2, axis=-1)
    sin2 = jnp.concatenate([jnp.sin(ang)] * 2, axis=-1)

    h0 = x.reshape(t, d)
    saved = {}

    def ffn_fwd(l, j, h, n, next_gs, target=None):
        wg, wu, wd = ffn_w[(l, j)]
        g, u, a = _ffn_up(n, wg, wu)
        f, h_new, *n_next = _ffn_down(a, wd, h, row(post_g[l, j]), next_gs, target)
        saved[("ffn", l, j)] = (h, n, g, u, a, f)
        return h_new, n_next

    n0 = _rms_fwd(h0, row(pre_g[0, 0]))
    h1, (n1,) = ffn_fwd(0, 0, h0, n0, row(w["mix_pre_g"][0]))

    full.update(unpack(_not_before(mat_all, h1), mats))
    w_in, w_out = full["gmlp_w_in"][0], full["gmlp_w_out"][0]
    w_c, w_kr = full["w_dkv"][:, :KV_RANK], full["w_dkv"][:, KV_RANK:]
    w_kr_rot = _rot_cols(w_kr)
    ukv = full["w_ukv"].reshape(KV_RANK, N_HEADS, 2, QK_NOPE)
    w_k, w_v = ukv[:, :, 0].reshape(KV_RANK, -1), ukv[:, :, 1].reshape(KV_RANK, -1)
    w_dq, w_o = full["mla_w_dq"][0], full["mla_w_o"][0]
    q_rank = w_dq.shape[1]
    uq = full["mla_w_uq"][0].reshape(q_rank, N_HEADS, QK_NOPE + QK_ROPE)
    w_qn = uq[:, :, :QK_NOPE].reshape(q_rank, -1)
    w_qr = uq[:, :, QK_NOPE:].transpose(1, 0, 2)
    w_qr_rot = _rot_cols(w_qr)

    zp = _mm2d("gmlp_in", [(n1, w_in, "nn", 0)], [(w_in.shape[1], F32)], n_outer=True)[0]
    uv = _sgu_fwd(zp, ln_g, ln_b, w_s, bsb)
    half = uv.shape[1]
    tm = min(t, ROW_TILE)
    m0, h2, n2 = _down("gmlp_out", (uv, (tm, 512), lambda i, _, k: (i, k)), (w_out, (512, d), lambda i, _, k: (k, 0)),
                       half // 512, h1, row(w["mix_post_g"][0]), row(pre_g[0, 1]), 1.0)
    h3, (n3kv, n3) = ffn_fwd(0, 1, h2, n2, jnp.stack([w["kv_norm_g"], pre_g[1, 0]]))

    def kv_epi(accs, ex):
        c_raw = accs[0]
        return [c_raw, _rms(c_raw, ex[2]), accs[1] * ex[0] + accs[2] * ex[1]]

    c_raw, c_n, k_r = _mm2d("kv_down", [(n3kv, w_c, "nn", 0), (n3kv, w_kr, "nn", 1), (n3kv, w_kr_rot, "nn", 2)],
                            [(KV_RANK, F32), (KV_RANK, BF16), (QK_ROPE, BF16)], kv_epi, [cos2, sin2],
                            [row(w["kv_a_norm_g"])])
    k_n, v_h = _mm2d("kv_up", [(c_n, w_k, "nn", 0), (c_n, w_v, "nn", 1)], [(w_k.shape[1], BF16), (w_v.shape[1], BF16)])

    h4, (n4,) = ffn_fwd(1, 0, h3, n3, row(w["mix_pre_g"][1]))
    qd, qn = _mm2d("q_down", [(n4, w_dq, "nn", 0)], [(q_rank, F32), (q_rank, BF16)],
                   lambda accs, ex: [accs[0], _rms(accs[0], ex[0])], [], [row(w["mla_q_norm_g"][0])])
    q_n = _mm2d("q_up", [(qn, w_qn, "nn", 0)], [(w_qn.shape[1], BF16)])[0]
    q_r = _q_rope(qn, w_qr, w_qr_rot, cos2, sin2)
    o, lse = _attn_fwd(q_n, q_r, k_n, v_h, k_r, seq)
    m1, h5, n5 = _down("attn_out", (o, (tm, 512), lambda i, _, k: (i, k)), (w_o, (512, d), lambda i, _, k: (k, 0)),
                       o.shape[1] // 512, h4, row(w["mix_post_g"][1]), row(pre_g[1, 1]), 1.0)
    _, (dy, loss_sum) = ffn_fwd(1, 1, h5, n5, None, loss_target.reshape(t, d))

    chip = (2 * lax.axis_index("x") + lax.axis_index("y")).astype(jnp.int32).reshape(1)
    rs = {}

    def rs_launch(gid, parts):
        rs[gid] = {"parts": parts, "others": _swap_halves(parts, 7 + gid, f"grad_swap_{gid}")}

    def rs_mid(gid, after):
        r = rs[gid]
        parts, others = _not_before((r["parts"], r["others"]), after)
        r["chip"] = _by_shape(_add_half, parts, others, core)
        r["slots"] = _scatter_chips(r["chip"], 12 + gid, f"grad_scatter_{gid}")
        return r["chip"]

    def rs_end(gid, after):
        r = rs[gid]
        slots, mine = _not_before((r["slots"], r["chip"]), after)
        r["own"] = _by_shape(_sum_slots, slots, mine, chip)
        r["recv"] = _join_halves(r["own"], 17 + gid, f"grad_join_{gid}")
        return r["own"]

    d_pre, d_post = {}, {}

    def ffn_bwd(l, j, gid, dh_out, extra=(), then=None):
        h, n, g, u, a, f = saved[("ffn", l, j)]
        wg, wu, wd = ffn_w[(l, j)]
        df, dg, du, d_post[(l, j)] = _ffn_dact(f, dh_out, row(post_g[l, j]), wd, g, u)
        dwd = _ffn_dw_down(a, df)
        dwg, dwu = _ffn_dw_in(n, dg, du)
        parts = [p.reshape(N_SHARDS, 2, p.shape[1] // 2, p.shape[2]) for p in (dwg, dwu, dwd)]
        rs_launch(gid, parts)
        dg, du = _not_before((dg, du), parts)
        res = _ffn_dn(dg, du, wg, wu, h, dh_out, [(row(pre_g[l, j]), None)] + list(extra),
                      None if then is None else (then[0], then[1], 1.0))
        if then is None:
            dh, d_pre[(l, j)], *rest = res
            return dh, rest
        dh, dm, d_pre[(l, j)], *rest = res
        return dh, [dm, rest[-1]] + rest[:-1]

    dh5, (dm1, g_mix_post1) = ffn_bwd(1, 1, 0, dy, then=(m1, row(w["mix_post_g"][1])))
    dh5, dm1 = _not_before((dh5, dm1), rs_mid(0, dh5))

    do = _mm2d("attn_out_dx", [(dm1, w_o, "nt", 0)], [(w_o.shape[0], BF16)])[0]
    g_w_o = _mm2d("attn_out_dw", [(o, dm1, "tn", 0)], [(d, BF16)])[0]
    dq_n, dk_n, dv_h, dq_c, dq_s, dk_r = _attn_bwd(q_n, q_r, k_n, v_h, k_r, do, lse, cos2, sin2, seq)
    dqn = _mm2d("q_up_dx", [(dq_n, w_qn, "nt", 0)], [(q_rank, F32)])[0]
    heads_x = ((N_HEADS, tm, QK_ROPE), lambda i, j, k: (0, i, 0))
    heads_w = ((N_HEADS, q_rank, QK_ROPE), lambda i, j, k: (0, 0, 0))
    q_row = ((tm, q_rank), lambda i, j, k: (i, 0))
    dqn = _mm("q_rope_dx", (t // tm, 1, 1), [(dq_c, *heads_x), (w_qr, *heads_w), (dq_s, *heads_x), (w_qr_rot, *heads_w)],
              [(0, 1, 0, "nt"), (2, 3, 0, "nt")], [(tm, q_rank)], [((t, q_rank), F32, *q_row)],
              lambda accs, ex: [accs[0] + ex[0]], [(dqn, *q_row)], inner=N_HEADS)[0]
    g_qn = _mm2d("q_up_dw", [(qn, dq_n, "tn", 0)], [(w_qn.shape[1], F32)])[0]
    g_qr, g_qr_rot = [g.transpose(0, 2, 1) for g in _q_rope_dw(qn, dq_c, dq_s)]
    dqd, g_q_norm = _norm_out_bwd("q_norm_bwd", qd, dqn, row(w["mla_q_norm_g"][0]), 1.0)

    def pre_norm_bwd(accs, ex):
        dx, dgain = _rms_bwd(ex[0], ex[2], accs[0])
        return [ex[1] + dx, dgain]

    dh4, g_mix_pre1 = _mm2d("q_down_dx", [(dqd, w_dq, "nt", 0)], [(d, F32)], pre_norm_bwd, [h4, dh5],
                            [row(w["mix_pre_g"][1])], sums=[(1, d)])
    g_w_dq = _mm2d("q_down_dw", [(n4, dqd, "tn", 0)], [(q_rank, BF16)])[0]

    dc_n = _mm2d("kv_up_dx", [(dk_n, w_k, "nt", 0), (dv_h, w_v, "nt", 0)], [(KV_RANK, F32)])[0]
    g_wk, g_wv = _mm2d("kv_up_dw", [(c_n, dk_n, "tn", 0), (c_n, dv_h, "tn", 1)], [(w_k.shape[1], F32), (w_v.shape[1], F32)])
    dc, g_kv_a = _norm_out_bwd("kv_a_norm_bwd", c_raw, dc_n, row(w["kv_a_norm_g"]), 1.0)
    dkr_c, dkr_s = _rope_bwd(dk_r, cos2, sin2)
    dn3kv = _mm2d("kv_down_dx", [(dc, w_c, "nt", 0), (dkr_c, w_kr, "nt", 0), (dkr_s, w_kr_rot, "nt", 0)], [(d, F32)])[0]
    g_wc, g_wkr, g_wkr_rot = _mm2d("kv_down_dw", [(n3kv, dc, "tn", 0), (n3kv, dkr_c, "tn", 1), (n3kv, dkr_s, "tn", 2)],
                                   [(KV_RANK, F32), (QK_ROPE, F32), (QK_ROPE, F32)])

    dh4 = _not_before(dh4, rs_end(0, dh4))
    dh3, (g_kv_norm,) = ffn_bwd(1, 0, 1, dh4, extra=[(row(w["kv_norm_g"]), dn3kv)])
    dh3 = _not_before(dh3, rs_mid(1, dh3))
    dh2, (dm0, g_mix_post0) = ffn_bwd(0, 1, 2, dh3, then=(m0, row(w["mix_post_g"][0])))
    dh2, dm0 = _not_before((dh2, dm0), (rs_end(1, dh2), rs_mid(2, dh2)))

    d_uv = _mm2d("gmlp_out_dx", [(dm0, w_out, "nt", 0)], [(half, F32)])[0]
    g_w_out = _mm2d("gmlp_out_dw", [(uv, dm0, "tn", 0)], [(d, BF16)])[0]
    dzp, g_ln_g, g_ln_b, g_w_s, g_b_s = _sgu_bwd(zp, d_uv, ln_g, ln_b, w_s, bsb)
    dh1, g_mix_pre0 = _mm2d("gmlp_in_dx", [(dzp, w_in, "nt", 0)], [(d, F32)], pre_norm_bwd, [h1, dh2],
                            [row(w["mix_pre_g"][0])], sums=[(1, d)])
    tk, tmw, w_cols = min(t, K_TILE), min(d, ROW_TILE), w_in.shape[1] // N_SHARDS
    g_w_in = _mm("gmlp_in_dw", (d // tmw, N_SHARDS, t // tk),
                 [(n1, (tk, tmw), lambda i, j, k: (k, i)), (dzp, (tk, w_cols), lambda i, j, k: (k, j))],
                 [(0, 1, 0, "tn")], [(tmw, w_cols)],
                 [((N_SHARDS, d, w_cols), BF16, (None, tmw, w_cols), lambda i, j, k: (j, i, 0))], lambda accs, ex: accs)[0]

    g_w_dkv = jnp.concatenate([g_wc, g_wkr + _unrot_cols(g_wkr_rot)], axis=1).astype(BF16)
    direct = {"gmlp_w_in": g_w_in, "gmlp_w_out": g_w_out, "mla_w_o": g_w_o, "mla_w_dq": g_w_dq, "w_dkv": g_w_dkv}
    direct = {n: g.reshape(N_SHARDS, -1, g.shape[-1]) for n, g in direct.items()}
    part = {
        "w_ukv": jnp.stack([g_wk.reshape(KV_RANK, N_HEADS, QK_NOPE), g_wv.reshape(KV_RANK, N_HEADS, V_DIM)],
                           axis=2).reshape(KV_RANK, -1),
        "mla_w_uq": jnp.concatenate(
            [g_qn.reshape(q_rank, N_HEADS, QK_NOPE),
             (g_qr + _unrot_cols(g_qr_rot)).transpose(1, 0, 2)],
            axis=-1).reshape(1, q_rank, -1),
        "gmlp_ln_g": g_ln_g, "gmlp_ln_b": g_ln_b,
        "mix_pre_g": jnp.concatenate([g_mix_pre0, g_mix_pre1]), "mix_post_g": jnp.concatenate([g_mix_post0, g_mix_post1]),
        "gmlp_w_s": g_w_s[None], "gmlp_b_s": g_b_s.reshape(1, GROUPS, CHUNK),
        "kv_norm_g": g_kv_norm.reshape(-1), "kv_a_norm_g": g_kv_a.reshape(-1), "mla_q_norm_g": g_q_norm,
    }

    sharded = [e for e in mats + vecs if e[0] in part]
    n_rep = sum(w[n].size for n in replicated)
    pieces = [_split_shards(part[n], ax) for n, ax in sharded]
    pieces.append(jnp.concatenate([part[n].reshape(-1) for n in replicated]).reshape(N_SHARDS, -1))
    piece_rows = [_round_up(-(-p.shape[1] // PACK_WIDTH), 16) for p in pieces]
    piece_start = [sum(piece_rows[:q]) for q in range(len(pieces))]
    rows = _round_up(sum(piece_rows), 32)
    piece_rows[-1] += rows - sum(piece_rows)
    small = jnp.concatenate(
        [jnp.pad(p, ((0, 0), (0, r * PACK_WIDTH - p.shape[1]))).astype(BF16).reshape(N_SHARDS, r, PACK_WIDTH)
         for p, r in zip(pieces, piece_rows)], axis=1).reshape(N_SHARDS, 2, rows // 2, PACK_WIDTH)

    rs_launch(3, [g.reshape(N_SHARDS, 2, g.shape[1] // 2, g.shape[2]) for g in direct.values()] + [small])
    dh1 = _not_before(dh1, rs_end(2, dh1))
    dh1 = _not_before(dh1, rs_mid(3, dh1))
    dx, _ = ffn_bwd(0, 0, 4, dh1)

    own3 = rs_end(3, dx)
    launched = rs_mid(4, (dx, own3))
    lj = [(l, j) for l in range(2) for j in range(2)]
    tiny = jnp.concatenate([d_pre[k] for k in lj] + [d_post[k] for k in lj] + [jnp.tile(loss_sum, (1, d // 128))])
    tiny = _all_sum(_not_before(tiny.reshape(-1, 128), launched)).reshape(-1, d)
    loss, tiny = tiny[-1, 0], tiny[:-1].reshape(2, 2, 2, d)
    shard_cols = d // N_SHARDS
    grads = {"ffn_pre_g": lax.dynamic_slice_in_dim(tiny[0], chip[0] * shard_cols, shard_cols, axis=2),
             "ffn_post_g": lax.dynamic_slice_in_dim(tiny[1], chip[0] * shard_cols, shard_cols, axis=2)}
    own_small, recv_small = own3[-1], rs[3]["recv"][-1]
    delta, new_m, new_v = {}, {}, {}
    for q, n in enumerate(direct):
        lead = lambda a: a.reshape((1, 1) + a.shape[-2:])
        upd = _adamw_halves(lead(w[n]), lead(mom[n]), lead(var[n]), 0, 0, rs[3]["own"][q], rs[3]["recv"][q], core, None)
        grads[n], delta[n], new_m[n], new_v[n] = [o.reshape(w[n].shape) for o in upd]
    g_small = jnp.where(core[0] == 0, jnp.concatenate([own_small, recv_small]), jnp.concatenate([recv_small, own_small]))
    g_rep = _gather_rows(g_small, piece_start[-1], piece_rows[-1], 22)
    for (n, _), start in zip(sharded, piece_start):
        grads[n] = g_small[start:start + -(-w[n].size // PACK_WIDTH)].reshape(-1)[:w[n].size].reshape(w[n].shape)
    rep_vec = g_rep.reshape(N_SHARDS, -1)[:, :n_rep // N_SHARDS].reshape(-1)
    for n, g in zip(replicated, _split_flat(rep_vec, [w[n].shape for n in replicated])):
        grads[n] = g

    for n in names:
        if n not in ffn_names and n not in delta:
            delta[n], new_m[n], new_v[n] = _adamw(w[n], grads[n], mom[n], var[n])
    chain = {n: None for n in ffn_names}

    def ffn_update(gid, l, j):
        for q, n in enumerate(ffn_names):
            chain[n] = _adamw_halves(oriented(w[n], n), oriented(mom[n], n), oriented(var[n], n), l, j,
                                     rs[gid]["own"][q], rs[gid]["recv"][q], core, chain[n])

    ffn_update(0, 1, 1)
    ffn_update(1, 1, 0)
    ffn_update(2, 0, 1)
    rs_end(4, ([delta[n] for n in delta], [chain[n] for n in ffn_names]))
    ffn_update(4, 0, 0)
    for n in ffn_names:
        grads[n], delta[n], new_m[n], new_v[n] = [oriented(o, n) for o in chain[n]]
    return (loss, dx.reshape(x.shape), *[grads[n] for n in names], *[delta[n] for n in names],
            *[new_m[n] for n in names], *[new_v[n] for n in names])
```

```python
import math

import jax
import jax.numpy as jnp
from jax import lax
from jax.experimental import pallas as pl
from jax.experimental.pallas import tpu as pltpu
from jax.experimental.pallas import tpu_sc as plsc

F32, BF16 = jnp.float32, jnp.bfloat16

RMS_EPS, LN_EPS, NEG_INF = 1e-6, 1e-5, -1e30
N_HEADS, QK_NOPE, QK_ROPE, V_DIM, KV_RANK = 8, 128, 64, 128, 256
CHUNK, GROUPS = 128, 16
ROPE_THETA = 10000.0
ADAM_LR, ADAM_B1, ADAM_B2, ADAM_EPS, ADAM_WD, ADAM_STEP = 0.001, 0.9, 0.999, 1e-08, 0.01, 10
N_SHARDS = 4

VMEM_LIMIT_BYTES = 48 * 1024 * 1024
ROW_TILE = 512
K_TILE = 2048
PACK_WIDTH = 1024

_DN = {"nn": (((1,), (0,)), ((), ())), "nt": (((1,), (1,)), ((), ())), "tn": (((0,), (0,)), ((), ()))}
_MESH = pl.DeviceIdType.MESH
_ANY = pl.BlockSpec(memory_space=pl.ANY)


def _params(sem):
    return pltpu.CompilerParams(dimension_semantics=sem, vmem_limit_bytes=VMEM_LIMIT_BYTES)


def _mm(name, grid, ins, pairs, acc_shapes, outs, epilogue, extras=(), inner=0, sums=()):
    n_in, n_ex, n_out = len(ins), len(extras), len(outs)
    gk = grid[2]

    def body(*refs):
        in_refs, ex_refs = refs[:n_in], refs[n_in:n_in + n_ex]
        out_refs = refs[n_in + n_ex:n_in + n_ex + n_out]
        sum_refs = refs[n_in + n_ex + n_out:n_in + n_ex + n_out + len(sums)]
        acc_refs = refs[n_in + n_ex + n_out + len(sums):]
        parts = [None] * len(acc_shapes)
        for a, b, c, dims in pairs:
            for s in range(max(inner, 1)):
                lhs, rhs = (in_refs[a][s], in_refs[b][s]) if inner else (in_refs[a][...], in_refs[b][...])
                p = lax.dot_general(lhs, rhs, _DN[dims], preferred_element_type=F32)
                parts[c] = p if parts[c] is None else parts[c] + p

        def finish(accs):
            vals = epilogue(accs, [r[...] for r in ex_refs])
            for r, v in zip(out_refs, vals):
                r[...] = v.astype(r.dtype)
            first = (pl.program_id(0) == 0) & (pl.program_id(1) == 0)
            for r, v in zip(sum_refs, vals[n_out:]):
                @pl.when(first)
                def _():
                    r[...] = v

                @pl.when(jnp.logical_not(first))
                def _():
                    r[...] += v

        if gk == 1:
            finish(parts)
        else:
            k = pl.program_id(2)

            @pl.when(k == 0)
            def _():
                for r, p in zip(acc_refs, parts):
                    r[...] = p

            @pl.when(k > 0)
            def _():
                for r, p in zip(acc_refs, parts):
                    r[...] += p

            @pl.when(k == gk - 1)
            def _():
                finish([r[...] for r in acc_refs])

    return pl.pallas_call(
        body,
        out_shape=[jax.ShapeDtypeStruct(s, d) for s, d, _, _ in outs] + [jax.ShapeDtypeStruct(s, F32) for s in sums],
        grid=grid,
        in_specs=[pl.BlockSpec(bs, im) for _, bs, im in list(ins) + list(extras)],
        out_specs=[pl.BlockSpec(bs, im) for _, _, bs, im in outs]
        + [pl.BlockSpec(s, lambda i, j, k, nd=len(s): (0,) * nd) for s in sums],
        scratch_shapes=[pltpu.VMEM(s, F32) for s in acc_shapes] if gk > 1 else [],
        name=name,
        compiler_params=_params(("arbitrary",) * 3 if sums else ("parallel", "parallel", "arbitrary")),
    )(*[a for a, _, _ in ins], *[a for a, _, _ in extras])


def _mm2d(name, pairs, outs, epilogue=None, row_extras=(), vec_extras=(), sums=(), n_outer=False):
    def mk(a, dims):
        return (a.shape[0], a.shape[1]) if dims[0] == "n" else (a.shape[1], a.shape[0])

    def nk(b, dims):
        return (b.shape[1], b.shape[0]) if dims[1] == "n" else (b.shape[0], b.shape[1])

    m = mk(pairs[0][0], pairs[0][2])[0]
    ks = [mk(a, d)[1] for a, _, d, _ in pairs]
    n_acc = 1 + max(p[3] for p in pairs)
    acc_n = [None] * n_acc
    for a, b, d, c in pairs:
        assert mk(a, d)[0] == m and nk(b, d)[1] == mk(a, d)[1]
        acc_n[c] = nk(b, d)[0]
    tm = min(m, ROW_TILE)
    if len(set(ks)) == 1 and ks[0] > 1024:
        tk = K_TILE if ks[0] % K_TILE == 0 else 512
        tks, gk = [tk] * len(pairs), ks[0] // tk
    else:
        tks, gk = ks, 1
    if len(set(acc_n)) == 1 and acc_n[0] > 1024:
        tns, gj = [1024] * n_acc, acc_n[0] // 1024
    else:
        tns, gj = acc_n, 1

    ins, plist = [], []
    for (a, b, d, c), tk in zip(pairs, tks):
        tn = tns[c]
        a_spec = ((tm, tk), lambda i, j, k: (i, k)) if d[0] == "n" else ((tk, tm), lambda i, j, k: (k, i))
        b_spec = ((tk, tn), lambda i, j, k: (k, j)) if d[1] == "n" else ((tn, tk), lambda i, j, k: (j, k))
        ins += [(a, *a_spec), (b, *b_spec)]
        plist.append((len(ins) - 2, len(ins) - 1, c, d))
    extras = [(r, (tm, r.shape[1]), lambda i, j, k: (i, 0)) for r in row_extras]
    extras += [(v, v.shape, lambda i, j, k: (0, 0)) for v in vec_extras]
    out_specs = []
    for n, dt in outs:
        bn = 1024 if (gj > 1) else n
        out_specs.append(((m, n), dt, (tm, bn), lambda i, j, k: (i, j)))
    if epilogue is None:
        epilogue = lambda accs, ex: accs
    grid = (m // tm, gj, gk)
    if n_outer:
        swap = lambda spec: spec[:-1] + ((lambda f: lambda j, i, k: f(i, j, k))(spec[-1]),)
        ins, extras, out_specs, grid = [swap(x) for x in ins], [swap(x) for x in extras], [swap(x) for x in out_specs], (gj, m // tm, gk)
    return _mm(name, grid, ins, plist, [(tm, tn) for tn in tns], out_specs, epilogue, extras, sums=sums)


def _rms(x, g):
    return x * lax.rsqrt(jnp.mean(x * x, axis=-1, keepdims=True) + RMS_EPS) * g


def _rms_bwd(x, g, dy):
    r = lax.rsqrt(jnp.mean(x * x, axis=-1, keepdims=True) + RMS_EPS)
    gy = dy * g
    dx = r * gy - x * (r * r * r) * jnp.mean(gy * x, axis=-1, keepdims=True)
    return dx, jnp.sum(dy * x * r, axis=0, keepdims=True)


def _sigmoid(x):
    return 0.5 * (1.0 + jnp.tanh(0.5 * x))


_GELU_C = math.sqrt(2.0 / math.pi)


def _gelu(x):
    return x * (0.5 * (1.0 + jnp.tanh(_GELU_C * (x + 0.044715 * (x * x * x)))))


def _gelu_grad(x):
    t = jnp.tanh(_GELU_C * (x + 0.044715 * (x * x * x)))
    return 0.5 * (1.0 + t) + 0.5 * x * (1.0 - t * t) * (_GELU_C * (1.0 + 3.0 * 0.044715 * (x * x)))


def _rows(name, row_ins, vec_ins, fn, row_outs, acc_outs=()):
    t = row_ins[0].shape[0]
    tm = min(t, ROW_TILE)
    nr, nv, no = len(row_ins), len(vec_ins), len(row_outs)

    def body(*refs):
        outs, incs = fn([r[...] for r in refs[:nr]], [r[...] for r in refs[nr:nr + nv]])
        for r, v in zip(refs[nr + nv:nr + nv + no], outs):
            r[...] = v.astype(r.dtype)
        i = pl.program_id(0)
        for r, v in zip(refs[nr + nv + no:], incs):
            @pl.when(i == 0)
            def _():
                r[...] = v

            @pl.when(i > 0)
            def _():
                r[...] += v

    in_specs = [pl.BlockSpec((tm, a.shape[1]), lambda i: (i, 0)) for a in row_ins]
    in_specs += [pl.BlockSpec(v.shape, lambda i, nd=v.ndim: (0,) * nd) for v in vec_ins]
    out_shape = [jax.ShapeDtypeStruct((t, c), dt) for c, dt in row_outs]
    out_shape += [jax.ShapeDtypeStruct(s, F32) for s in acc_outs]
    out_specs = [pl.BlockSpec((tm, c), lambda i: (i, 0)) for c, _ in row_outs]
    out_specs += [pl.BlockSpec(s, lambda i, nd=len(s): (0,) * nd) for s in acc_outs]
    return pl.pallas_call(body, out_shape=out_shape, grid=(t // tm,), in_specs=in_specs, out_specs=out_specs,
                          name=name, compiler_params=_params(("arbitrary",)))(*row_ins, *vec_ins)


def _rms_fwd(x, g):
    return _rows("rms_fwd", [x], [g], lambda r, v: ([_rms(r[0], v[0])], []), [(x.shape[1], BF16)])[0]


def _norm_out_bwd(name, f, d_out, g, scale):
    def fn(r, v):
        dx, dg = _rms_bwd(r[0], v[0], r[1] * scale)
        return [dx], [dg]

    c = f.shape[1]
    return _rows(name, [f, d_out], [g], fn, [(c, BF16)], [(1, c)])


def _rope_bwd(dk, cos2, sin2):
    c = dk.shape[1]
    return _rows("rope_bwd", [dk, cos2, sin2], [], lambda r, v: ([r[0] * r[1], r[0] * r[2]], []),
                 [(c, BF16), (c, BF16)])


def _ffn_up(n, wg, wu):
    t, d = n.shape
    fs = wg.shape[-2]
    tm = min(t, ROW_TILE // 2)

    def body(n_ref, wg_ref, wu_ref, g_ref, u_ref, a_ref):
        x = n_ref[...]
        for s in range(N_SHARDS):
            g = lax.dot_general(x, wg_ref[s], _DN["nt"], preferred_element_type=F32)
            u = lax.dot_general(x, wu_ref[s], _DN["nt"], preferred_element_type=F32)
            g_ref[s] = g.astype(BF16)
            u_ref[s] = u.astype(BF16)
            a_ref[s] = (g * _sigmoid(g) * u).astype(BF16)

    hid = pl.BlockSpec((N_SHARDS, tm, fs), lambda i: (0, i, 0))
    whole = pl.BlockSpec(wg.shape, lambda i: (0, 0, 0))
    return pl.pallas_call(body, out_shape=[jax.ShapeDtypeStruct((N_SHARDS, t, fs), BF16)] * 3, grid=(t // tm,),
                          in_specs=[pl.BlockSpec((tm, d), lambda i: (i, 0)), whole, whole], out_specs=[hid] * 3,
                          name="ffn_up", compiler_params=_params(("parallel",)))(n, wg, wu)


def _down(name, a_in, w_in, gk, h, post_g, next_gs, scale, inner=0, target=None):
    t, d = h.shape
    tm = min(t, ROW_TILE)
    kn = 0 if target is not None else next_gs.shape[0]

    def epi(accs, ex):
        f, hv, pg, last = accs[0], ex[0], ex[1], ex[2]
        hn = hv + scale * _rms(f, pg)
        if target is None:
            return [f, hn] + [_rms(hn, last[q:q + 1]) for q in range(kn)]
        e = hn - last
        s = jnp.sum(jnp.sum(e * e, axis=1, keepdims=True), axis=0, keepdims=True) * (0.5 / d)
        return [f, hn, e * (1.0 / d), jnp.broadcast_to(s, (1, 128))]

    row = ((tm, d), lambda i, j, k: (i, 0))
    extras = [(h, *row), (post_g, (1, d), lambda i, j, k: (0, 0))]
    if target is None:
        outs = [((t, d), F32, *row), ((t, d), F32, *row)] + [((t, d), BF16, *row)] * kn
        extras.append((next_gs, (kn, d), lambda i, j, k: (0, 0)))
    else:
        outs = [((t, d), F32, *row)] * 3
        extras.append((target, *row))
    return _mm(name, (t // tm, 1, gk), [a_in, w_in], [(0, 1, 0, "nn")], [(tm, d)], outs, epi, extras, inner,
               sums=[(1, 128)] if target is not None else ())


def _ffn_down(a, wd, h, post_g, next_gs, target=None):
    t, d = h.shape
    fs = a.shape[-1]
    tm = min(t, ROW_TILE)
    return _down("ffn_down", (a, (N_SHARDS, tm, fs), lambda i, _, k: (0, i, 0)),
                 (wd, (N_SHARDS, fs, d), lambda i, _, k: (0, 0, 0)), 1, h, post_g, next_gs, 0.5, N_SHARDS, target)


def _ffn_dact(f, d_out, post_g, wd, g, u):
    t, d = f.shape
    fs = g.shape[-1]
    tm = min(t, ROW_TILE // 2)

    def body(f_ref, do_ref, pg_ref, wd_ref, g_ref, u_ref, df_ref, dg_ref, du_ref, dpg_ref):
        dfv, dpg = _rms_bwd(f_ref[...], pg_ref[...], do_ref[...] * 0.5)
        dfb = dfv.astype(BF16)
        df_ref[...] = dfb
        i = pl.program_id(0)

        @pl.when(i == 0)
        def _():
            dpg_ref[...] = dpg

        @pl.when(i > 0)
        def _():
            dpg_ref[...] += dpg

        for s in range(N_SHARDS):
            da = lax.dot_general(dfb, wd_ref[s], _DN["nt"], preferred_element_type=F32)
            gv, uv = g_ref[s].astype(F32), u_ref[s].astype(F32)
            sg = _sigmoid(gv)
            dg_ref[s] = (da * uv * (sg * (1.0 + gv * (1.0 - sg)))).astype(BF16)
            du_ref[s] = (da * (gv * sg)).astype(BF16)

    row = pl.BlockSpec((tm, d), lambda i: (i, 0))
    hid = pl.BlockSpec((N_SHARDS, tm, fs), lambda i: (0, i, 0))
    vec = pl.BlockSpec((1, d), lambda i: (0, 0))
    hid_shape = jax.ShapeDtypeStruct((N_SHARDS, t, fs), BF16)
    return pl.pallas_call(
        body, out_shape=[jax.ShapeDtypeStruct((t, d), BF16), hid_shape, hid_shape, jax.ShapeDtypeStruct((1, d), F32)],
        grid=(t // tm,), in_specs=[row, row, vec, pl.BlockSpec(wd.shape, lambda i: (0, 0, 0)), hid, hid],
        out_specs=[row, hid, hid, vec], name="ffn_dact", compiler_params=_params(("arbitrary",)))(
            f, d_out, post_g, wd, g, u)


def _ffn_dn(dg, du, wg, wu, h, d_res, branches, then=None):
    _, t, fs = dg.shape
    d = wg.shape[-1]
    tm = min(t, ROW_TILE)
    nb = len(branches)
    a_spec = ((N_SHARDS, tm, fs), lambda i, _, k: (0, i, 0))
    w_spec = ((N_SHARDS, fs, d), lambda i, _, k: (0, 0, 0))
    row = ((tm, d), lambda i, _, k: (i, 0))
    vec = ((1, d), lambda i, _, k: (0, 0))

    def epi(accs, ex):
        hv, dh = ex[0], ex[1]
        dns, gs = [accs[0]] + ex[2:1 + nb], ex[1 + nb:]
        dgs = []
        for dn, g in zip(dns, gs[:nb]):
            dx, dgv = _rms_bwd(hv, g, dn)
            dh = dh + dx
            dgs.append(dgv)
        if then is None:
            return [dh] + dgs
        dm, dgm = _rms_bwd(ex[-1], gs[nb], dh * then[2])
        return [dh, dm] + dgs + [dgm]

    extras = [(h, *row), (d_res, *row)] + [(dn, *row) for _, dn in branches[1:]] + [(g, *vec) for g, _ in branches]
    outs = [((t, d), F32, *row)]
    if then is not None:
        extras += [(then[1], *vec), (then[0], *row)]
        outs.append(((t, d), BF16, *row))
    return _mm("ffn_dn", (t // tm, 1, 1), [(dg, *a_spec), (wg, *w_spec), (du, *a_spec), (wu, *w_spec)],
               [(0, 1, 0, "nn"), (2, 3, 0, "nn")], [(tm, d)], outs, epi, extras, inner=N_SHARDS,
               sums=[(1, d)] * (nb + (then is not None)))


def _ffn_dw_in(n, dg, du):
    _, t, fs = dg.shape
    d = n.shape[1]
    tk = min(t, K_TILE)
    a_spec = ((None, tk, fs), lambda s, _, k: (s, k, 0))
    o_spec = ((None, fs, d), lambda s, _, k: (s, 0, 0))
    outs = [((N_SHARDS, fs, d), BF16, *o_spec)] * 2
    return _mm("ffn_dw_in", (N_SHARDS, 1, t // tk), [(dg, *a_spec), (du, *a_spec), (n, (tk, d), lambda s, _, k: (k, 0))],
               [(0, 2, 0, "tn"), (1, 2, 1, "tn")], [(fs, d)] * 2, outs, lambda accs, ex: accs)


def _ffn_dw_down(a, df):
    _, t, fs = a.shape
    d = df.shape[1]
    tk = min(t, K_TILE)
    outs = [((N_SHARDS, fs, d), BF16, (None, fs, d), lambda s, _, k: (s, 0, 0))]
    return _mm("ffn_dw_down", (N_SHARDS, 1, t // tk),
               [(a, (None, tk, fs), lambda s, _, k: (s, k, 0)), (df, (tk, d), lambda s, _, k: (k, 0))],
               [(0, 1, 0, "tn")], [(fs, d)], outs, lambda accs, ex: accs)[0]


def _causal_weight(w):
    row = lax.broadcasted_iota(jnp.int32, (CHUNK, CHUNK), 0)
    col = lax.broadcasted_iota(jnp.int32, (CHUNK, CHUNK), 1)
    return row >= col, jnp.where(row >= col, w, 0.0).astype(BF16)


def _layer_norm(v, g, b):
    xc = v - jnp.mean(v, axis=-1, keepdims=True)
    rstd = lax.rsqrt(jnp.mean(xc * xc, axis=-1, keepdims=True) + LN_EPS)
    xhat = xc * rstd
    return xhat, rstd, xhat * g + b


def _sgu_specs(t, half, tm):
    return [pl.BlockSpec((tm, half), lambda i: (i, 0)), pl.BlockSpec((tm, half), lambda i: (i, 1))]


def _sgu_fwd(zp, ln_g, ln_b, w_s, bsb):
    t, half = zp.shape[0], zp.shape[1] // 2
    tm = min(t, 2 * CHUNK)

    def body(u_ref, v_ref, g_ref, b_ref, w_ref, bs_ref, o_ref):
        u = _gelu(u_ref[...])
        _, _, vn = _layer_norm(_gelu(v_ref[...]), g_ref[...], b_ref[...])
        vb = vn.astype(BF16)
        for g in range(GROUPS):
            _, wm = _causal_weight(w_ref[g])
            cols = slice(g * CHUNK, (g + 1) * CHUNK)
            for c in range(tm // CHUNK):
                rows = slice(c * CHUNK, (c + 1) * CHUNK)
                sv = jnp.dot(wm, vb[rows, cols], preferred_element_type=F32) + bs_ref[g]
                o_ref[rows, cols] = (u[rows, cols] * sv).astype(BF16)

    whole = lambda a: pl.BlockSpec(a.shape, lambda i, nd=a.ndim: (0,) * nd)
    return pl.pallas_call(
        body, out_shape=jax.ShapeDtypeStruct((t, half), BF16), grid=(t // tm,),
        in_specs=_sgu_specs(t, half, tm) + [whole(ln_g), whole(ln_b), whole(w_s), whole(bsb)],
        out_specs=pl.BlockSpec((tm, half), lambda i: (i, 0)), name="sgu_fwd",
        compiler_params=_params(("arbitrary",)))(zp, zp, ln_g, ln_b, w_s, bsb)


def _gmlp_in_sgu(n, w_in, ln_g, ln_b, w_s, bsb):
    t, d = n.shape
    half = w_in.shape[1] // 2
    tm = min(t, 2 * CHUNK)

    def body(n_ref, wi_ref, g_ref, b_ref, w_ref, bs_ref, z_ref, o_ref):
        z_ref[...] = jnp.dot(n_ref[...].astype(BF16), wi_ref[...].astype(BF16), preferred_element_type=F32)
        u = _gelu(z_ref[:, :half])
        _, _, vn = _layer_norm(_gelu(z_ref[:, half:]), g_ref[...], b_ref[...])
        vb = vn.astype(BF16)
        for g in range(GROUPS):
            _, wm = _causal_weight(w_ref[g])
            cols = slice(g * CHUNK, (g + 1) * CHUNK)
            for c in range(tm // CHUNK):
                rows = slice(c * CHUNK, (c + 1) * CHUNK)
                sv = jnp.dot(wm, vb[rows, cols], preferred_element_type=F32) + bs_ref[g]
                o_ref[rows, cols] = (u[rows, cols] * sv).astype(BF16)

    whole = lambda a: pl.BlockSpec(a.shape, lambda i, nd=a.ndim: (0,) * nd)
    return pl.pallas_call(
        body, out_shape=[jax.ShapeDtypeStruct((t, 2 * half), F32), jax.ShapeDtypeStruct((t, half), BF16)],
        grid=(t // tm,),
        in_specs=[pl.BlockSpec((tm, d), lambda i: (i, 0)), whole(w_in), whole(ln_g), whole(ln_b), whole(w_s),
                  whole(bsb)],
        out_specs=[pl.BlockSpec((tm, 2 * half), lambda i: (i, 0)), pl.BlockSpec((tm, half), lambda i: (i, 0))],
        name="gmlp_in_sgu", compiler_params=_params(("arbitrary",)))(n, w_in, ln_g, ln_b, w_s, bsb)


def _sgu_bwd(zp, d_uv, ln_g, ln_b, w_s, bsb):
    t, half = zp.shape[0], zp.shape[1] // 2
    tm = min(t, 2 * CHUNK)

    def body(u_ref, v_ref, d_ref, g_ref, b_ref, w_ref, bs_ref, dz_ref, dlg_ref, dlb_ref, dws_ref, dbs_ref, dvn_ref):
        i = pl.program_id(0)

        @pl.when(i == 0)
        def _():
            dlg_ref[...] = jnp.zeros_like(dlg_ref)
            dlb_ref[...] = jnp.zeros_like(dlb_ref)
            dws_ref[...] = jnp.zeros_like(dws_ref)
            dbs_ref[...] = jnp.zeros_like(dbs_ref)

        up, vp = u_ref[...], v_ref[...]
        u, gup = _gelu(up), _gelu_grad(up)
        xhat, rstd, vn = _layer_norm(_gelu(vp), g_ref[...], b_ref[...])
        vb = vn.astype(BF16)
        d = d_ref[...]
        for g in range(GROUPS):
            mask, wm = _causal_weight(w_ref[g])
            cols = slice(g * CHUNK, (g + 1) * CHUNK)
            for c in range(tm // CHUNK):
                rows = slice(c * CHUNK, (c + 1) * CHUNK)
                blk = vb[rows, cols]
                sv = jnp.dot(wm, blk, preferred_element_type=F32) + bs_ref[g]
                dblk = d[rows, cols]
                dz_ref[rows, cols] = (dblk * sv * gup[rows, cols]).astype(BF16)
                dsv = dblk * u[rows, cols]
                dsvb = dsv.astype(BF16)
                dvn_ref[rows, cols] = lax.dot_general(wm, dsvb, _DN["tn"], preferred_element_type=F32)
                dw = lax.dot_general(dsvb, blk, _DN["nt"], preferred_element_type=F32)
                dws_ref[g] += jnp.where(mask, dw, 0.0)
                dbs_ref[g] += jnp.sum(dsv, axis=1, keepdims=True)
        dvn = dvn_ref[...]
        dlg_ref[...] += jnp.sum(dvn * xhat, axis=0, keepdims=True)
        dlb_ref[...] += jnp.sum(dvn, axis=0, keepdims=True)
        dxh = dvn * g_ref[...]
        dv = rstd * (dxh - jnp.mean(dxh, axis=-1, keepdims=True)
                     - xhat * jnp.mean(dxh * xhat, axis=-1, keepdims=True))
        dz_ref[:, half:] = (dv * _gelu_grad(vp)).astype(BF16)

    whole = lambda a: pl.BlockSpec(a.shape, lambda i, nd=a.ndim: (0,) * nd)
    wshape = lambda s: pl.BlockSpec(s, lambda i, nd=len(s): (0,) * nd)
    out_shape = [jax.ShapeDtypeStruct((t, 2 * half), BF16), jax.ShapeDtypeStruct((1, half), F32),
                 jax.ShapeDtypeStruct((1, half), F32), jax.ShapeDtypeStruct(w_s.shape, F32),
                 jax.ShapeDtypeStruct((GROUPS, CHUNK, 1), F32)]
    return pl.pallas_call(
        body, out_shape=out_shape, grid=(t // tm,),
        in_specs=_sgu_specs(t, half, tm) + [pl.BlockSpec((tm, half), lambda i: (i, 0)), whole(ln_g), whole(ln_b),
                                            whole(w_s), whole(bsb)],
        out_specs=[pl.BlockSpec((tm, 2 * half), lambda i: (i, 0)), wshape((1, half)), wshape((1, half)),
                   wshape(w_s.shape), wshape((GROUPS, CHUNK, 1))],
        scratch_shapes=[pltpu.VMEM((tm, half), F32)], name="sgu_bwd",
        compiler_params=_params(("arbitrary",)))(zp, zp, d_uv, ln_g, ln_b, w_s, bsb)


_SCALE = (QK_NOPE + QK_ROPE) ** -0.5


def _attn_scores(qn, qr, kn, kr, i, tq, n):
    s = lax.dot_general(qn, kn, _DN["nt"], preferred_element_type=F32)
    s = (s + lax.dot_general(qr, kr, _DN["nt"], preferred_element_type=F32)) * _SCALE
    row = i * tq + lax.broadcasted_iota(jnp.int32, (tq, n), 0)
    col = lax.broadcasted_iota(jnp.int32, (tq, n), 1)
    return jnp.where(col <= row, s, NEG_INF)


def _attn_specs(seq):
    head = lambda b, h: (b, h)
    return dict(
        qn=pl.BlockSpec((seq, QK_NOPE), head),
        qr=pl.BlockSpec((None, seq, QK_ROPE), lambda b, h: (h, b, 0)),
        kr=pl.BlockSpec((seq, QK_ROPE), lambda b, h: (b, 0)),
        lse=pl.BlockSpec((None, seq, 1), lambda b, h: (h, b, 0)),
    )


def _attn_fwd(qn, qr, kn, v, kr, seq):
    t = qn.shape[0]
    tq = min(seq, 4 * CHUNK)
    sp = _attn_specs(seq)

    def body(qn_ref, qr_ref, kn_ref, v_ref, kr_ref, o_ref, lse_ref):
        for i in range(seq // tq):
            rows, n = slice(i * tq, (i + 1) * tq), (i + 1) * tq
            s = _attn_scores(qn_ref[rows, :], qr_ref[rows, :], kn_ref[0:n, :], kr_ref[0:n, :], i, tq, n)
            m = jnp.max(s, axis=-1, keepdims=True)
            p = jnp.exp(s - m)
            l = jnp.sum(p, axis=-1, keepdims=True)
            o_ref[rows, :] = jnp.dot((p / l).astype(BF16), v_ref[0:n, :], preferred_element_type=F32).astype(BF16)
            lse_ref[rows, :] = m + jnp.log(l)

    return pl.pallas_call(
        body, out_shape=[jax.ShapeDtypeStruct((t, N_HEADS * V_DIM), BF16), jax.ShapeDtypeStruct((N_HEADS, t, 1), F32)],
        grid=(t // seq, N_HEADS), in_specs=[sp["qn"], sp["qr"], sp["qn"], sp["qn"], sp["kr"]],
        out_specs=[sp["qn"], sp["lse"]], name="attn_fwd",
        compiler_params=_params(("parallel", "arbitrary")))(qn, qr, kn, v, kr)


def _attn_bwd(qn, qr, kn, v, kr, do, lse, cos2, sin2, seq):
    t = qn.shape[0]
    tq = min(seq, 4 * CHUNK)
    sp = _attn_specs(seq)

    def body(qn_ref, qr_ref, kn_ref, v_ref, kr_ref, do_ref, lse_ref, cos_ref, sin_ref,
             dqn_ref, dkn_ref, dv_ref, dqc_ref, dqs_ref, dkr_ref, dk_acc, dv_acc, dkr_acc):
        dk_acc[...] = jnp.zeros_like(dk_acc)
        dv_acc[...] = jnp.zeros_like(dv_acc)
        dkr_acc[...] = jnp.zeros_like(dkr_acc)
        for i in range(seq // tq):
            rows, n = slice(i * tq, (i + 1) * tq), (i + 1) * tq
            q_n, q_r, d_o = qn_ref[rows, :], qr_ref[rows, :], do_ref[rows, :]
            k_n, k_r = kn_ref[0:n, :], kr_ref[0:n, :]
            s = _attn_scores(q_n, q_r, k_n, k_r, i, tq, n)
            p = jnp.exp(s - lse_ref[rows, :])
            dp = lax.dot_general(d_o, v_ref[0:n, :], _DN["nt"], preferred_element_type=F32)
            ds = (p * (dp - jnp.sum(p * dp, axis=-1, keepdims=True)) * _SCALE).astype(BF16)
            dqn_ref[rows, :] = jnp.dot(ds, k_n, preferred_element_type=F32).astype(BF16)
            dqr = jnp.dot(ds, k_r, preferred_element_type=F32)
            dqc_ref[rows, :] = (dqr * cos_ref[rows, :]).astype(BF16)
            dqs_ref[rows, :] = (dqr * sin_ref[rows, :]).astype(BF16)
            dk_acc[0:n, :] += lax.dot_general(ds, q_n, _DN["tn"], preferred_element_type=F32)
            dkr_acc[0:n, :] += lax.dot_general(ds, q_r, _DN["tn"], preferred_element_type=F32)
            dv_acc[0:n, :] += lax.dot_general(p.astype(BF16), d_o, _DN["tn"], preferred_element_type=F32)
        dkn_ref[...] = dk_acc[...].astype(BF16)
        dv_ref[...] = dv_acc[...].astype(BF16)
        h = pl.program_id(1)

        @pl.when(h == 0)
        def _():
            dkr_ref[...] = dkr_acc[...]

        @pl.when(h > 0)
        def _():
            dkr_ref[...] += dkr_acc[...]

    wide = jax.ShapeDtypeStruct((t, N_HEADS * V_DIM), BF16)
    rope = jax.ShapeDtypeStruct((N_HEADS, t, QK_ROPE), BF16)
    krf = pl.BlockSpec((seq, QK_ROPE), lambda b, h: (b, 0))
    return pl.pallas_call(
        body, out_shape=[wide, wide, wide, rope, rope, jax.ShapeDtypeStruct((t, QK_ROPE), F32)],
        grid=(t // seq, N_HEADS),
        in_specs=[sp["qn"], sp["qr"], sp["qn"], sp["qn"], sp["kr"], sp["qn"], sp["lse"], krf, krf],
        out_specs=[sp["qn"], sp["qn"], sp["qn"], sp["qr"], sp["qr"], krf],
        scratch_shapes=[pltpu.VMEM((seq, QK_NOPE), F32), pltpu.VMEM((seq, V_DIM), F32), pltpu.VMEM((seq, QK_ROPE), F32)],
        name="attn_bwd", compiler_params=_params(("parallel", "arbitrary")))(qn, qr, kn, v, kr, do, lse, cos2, sin2)


def _q_rope(qn, w, w_rot, cos2, sin2):
    t, r = qn.shape
    nh, _, e = w.shape
    tm = min(t, ROW_TILE)

    def body(x_ref, w_ref, wr_ref, c_ref, s_ref, o_ref):
        x = x_ref[...]
        for h in range(nh):
            raw = jnp.dot(x, w_ref[h], preferred_element_type=F32)
            rot = jnp.dot(x, wr_ref[h], preferred_element_type=F32)
            o_ref[h] = (raw * c_ref[...] + rot * s_ref[...]).astype(BF16)

    whole = pl.BlockSpec(w.shape, lambda i: (0, 0, 0))
    rows = pl.BlockSpec((tm, e), lambda i: (i, 0))
    return pl.pallas_call(body, out_shape=jax.ShapeDtypeStruct((nh, t, e), BF16), grid=(t // tm,),
                          in_specs=[pl.BlockSpec((tm, r), lambda i: (i, 0)), whole, whole, rows, rows],
                          out_specs=pl.BlockSpec((nh, tm, e), lambda i: (0, i, 0)), name="q_rope",
                          compiler_params=_params(("parallel",)))(qn, w, w_rot, cos2, sin2)


def _q_rope_dw(qn, dq_c, dq_s):
    t, r = qn.shape
    nh, _, e = dq_c.shape
    tk = min(t, K_TILE)

    def body(x_ref, c_ref, s_ref, gc_ref, gs_ref):
        k = pl.program_id(0)
        x = x_ref[...]
        for h in range(nh):
            pc = lax.dot_general(c_ref[h], x, _DN["tn"], preferred_element_type=F32)
            ps = lax.dot_general(s_ref[h], x, _DN["tn"], preferred_element_type=F32)

            @pl.when(k == 0)
            def _():
                gc_ref[h] = pc
                gs_ref[h] = ps

            @pl.when(k > 0)
            def _():
                gc_ref[h] += pc
                gs_ref[h] += ps

    heads = pl.BlockSpec((nh, tk, e), lambda k: (0, k, 0))
    out = pl.BlockSpec((nh, e, r), lambda k: (0, 0, 0))
    return pl.pallas_call(body, out_shape=[jax.ShapeDtypeStruct((nh, e, r), F32)] * 2, grid=(t // tk,),
                          in_specs=[pl.BlockSpec((tk, r), lambda k: (k, 0)), heads, heads], out_specs=[out, out],
                          name="q_rope_dw", compiler_params=_params(("arbitrary",)))(qn, dq_c, dq_s)


def _row_tile(rows, cols, row_mult=8):
    cap = max(row_mult, (1 << 18) // cols)
    best = rows
    for tr in range(row_mult, min(rows, cap) + 1, row_mult):
        if rows % tr == 0:
            best = tr
    return best if rows > cap else rows


def _adamw_math(w, g, m, v):
    mv = ADAM_B1 * m + (1.0 - ADAM_B1) * g
    vv = ADAM_B2 * v + (1.0 - ADAM_B2) * (g * g)
    m_hat = mv / (1.0 - ADAM_B1 ** ADAM_STEP)
    v_hat = vv / (1.0 - ADAM_B2 ** ADAM_STEP)
    return -ADAM_LR * (m_hat / (jnp.sqrt(v_hat) + ADAM_EPS) + ADAM_WD * w), mv, vv


def _adamw(w, g, m, v):
    shape = w.shape
    c = shape[-1]
    r = w.size // c
    tr = _row_tile(r, c)

    def body(w_ref, g_ref, m_ref, v_ref, d_ref, nm_ref, nv_ref):
        d_ref[...], nm_ref[...], nv_ref[...] = _adamw_math(w_ref[...], g_ref[...], m_ref[...], v_ref[...])

    spec = pl.BlockSpec((tr, c), lambda i: (i, 0))
    outs = pl.pallas_call(body, out_shape=[jax.ShapeDtypeStruct((r, c), F32)] * 3, grid=(r // tr,),
                          in_specs=[spec] * 4, out_specs=[spec] * 3, name="adamw",
                          compiler_params=_params(("parallel",)))(*[a.reshape(r, c) for a in (w, g, m, v)])
    return [o.reshape(shape) for o in outs]


def _adamw_halves(w, m, v, l, j, own, recv, core, prev):
    nl, nj, rows, c = w.shape
    r = rows // 2
    tr = _row_tile(r, c)
    n_prev = 0 if prev is None else 4

    def body(core_ref, w_ref, own_ref, recv_ref, m_ref, v_ref, *rest):
        g_ref, d_ref, nm_ref, nv_ref = rest[n_prev:]
        g = jnp.where(pl.program_id(0) == core_ref[0], own_ref[...], recv_ref[...])
        g_ref[...] = g
        d_ref[...], nm_ref[...], nv_ref[...] = _adamw_math(w_ref[...], g, m_ref[...], v_ref[...])

    nb = r // tr
    slab = pl.BlockSpec((None, None, tr, c), lambda h, i, cr: (l, j, h * nb + i, 0))
    half = pl.BlockSpec((tr, c), lambda h, i, cr: (i, 0))
    grid_spec = pltpu.PrefetchScalarGridSpec(num_scalar_prefetch=1, grid=(2, nb),
                                             in_specs=[slab, half, half, slab, slab] + [_ANY] * n_prev,
                                             out_specs=[slab] * 4)
    return pl.pallas_call(body, out_shape=[jax.ShapeDtypeStruct(w.shape, F32)] * 4, grid_spec=grid_spec,
                          input_output_aliases={6 + q: q for q in range(n_prev)}, name="adamw_halves",
                          compiler_params=_params(("parallel",) * 2))(core, w, own, recv, m, v, *(prev or ()))


def _place():
    x, y, c = lax.axis_index("x"), lax.axis_index("y"), lax.axis_index("c")
    return x, y, c, [(1 - x, y), (x, 1 - y), (1 - x, 1 - y)]


def _dma_sems(*counts):
    return [pltpu.SemaphoreType.DMA((n,)) for n in counts]


def _all_gather(bufs, collective_id, name):
    n = len(bufs)

    def body(*refs):
        ins, outs = refs[:n], refs[n:2 * n]
        send, recv, fsend, frecv, osend, orecv = refs[2 * n:]
        x, y, c, _ = _place()
        xn, yn, sib = (1 - x, y, c), (x, 1 - y, c), (x, y, 1 - c)
        k, kx, ky, kd = 2 * x + y, 2 * (1 - x) + y, 2 * x + 1 - y, 2 * (1 - x) + 1 - y
        _handshake([xn, yn, sib])

        def copy(src, dst, sems, i, to):
            return pltpu.make_async_remote_copy(src, dst, sems[0].at[i], sems[1].at[i], device_id=to, device_id_type=_MESH)

        ici, d2d, own_s = (send, recv), (fsend, frecv), (osend, orecv)
        started = [copy(ins[b], outs[b].at[k], own_s, b, sib) for b in range(n)]
        for first in (True, False):
            for b in range(n):
                mine = outs[b].at[k, c]
                if first:
                    started += [copy(ins[b].at[c, 0], mine.at[0], ici, 6 * b, xn), copy(ins[b].at[c, 1], mine.at[1], ici, 6 * b + 1, yn)]
                else:
                    started += [copy(ins[b].at[c, 1], mine.at[1], ici, 6 * b + 2, xn), copy(ins[b].at[c, 0], mine.at[0], ici, 6 * b + 3, yn)]
        for cp in started:
            cp.start()
        passed = []
        for b in range(n):
            for i, (src_chip, q, to) in enumerate([(kx, 0, yn), (ky, 1, xn)]):
                piece = outs[b].at[src_chip, c, q]
                copy(piece, piece, ici, 6 * b + i, to).wait_recv()
                cp = copy(piece, piece, ici, 6 * b + 4 + i, to)
                cp.start()
                passed.append(cp)
        for b in range(n):
            for i, (src_chip, q) in enumerate([(kx, 1), (ky, 0)]):
                piece = outs[b].at[src_chip, c, q]
                copy(piece, piece, ici, 6 * b + 2 + i, xn).wait_recv()
                half = outs[b].at[src_chip, c]
                cp = copy(half, half, d2d, 3 * b + i, sib)
                cp.start()
                passed.append(cp)
        for b in range(n):
            for i, q in enumerate([0, 1]):
                piece = outs[b].at[kd, c, q]
                copy(piece, piece, ici, 6 * b + 4 + i, xn).wait_recv()
            half = outs[b].at[kd, c]
            cp = copy(half, half, d2d, 3 * b + 2, sib)
            cp.start()
            passed.append(cp)
        for b in range(n):
            for i, src_chip in enumerate([kx, ky, kd]):
                half = outs[b].at[src_chip, 1 - c]
                copy(half, half, d2d, 3 * b + i, sib).wait_recv()
        for cp in started[n:] + passed:
            cp.wait_send()
        for cp in started[:n]:
            cp.wait()

    return _sequencer(body, [jax.ShapeDtypeStruct((N_SHARDS,) + b.shape, b.dtype) for b in bufs],
                      _dma_sems(6 * n, 6 * n, 3 * n, 3 * n, n, n), collective_id, name, bufs)


def _sequencer(body, out_type, sems, collective_id, name, args):
    return pl.kernel(body, out_type=out_type, mesh=plsc.ScalarSubcoreMesh(axis_name="sequencer", num_cores=1),
                     scratch_types=sems, compiler_params=pltpu.CompilerParams(collective_id=collective_id),
                     name=name)(*args)


def _handshake(peers):
    barrier = pltpu.get_barrier_semaphore()
    for peer in peers:
        pl.semaphore_signal(barrier, inc=1, device_id=peer, device_id_type=_MESH)
    pl.semaphore_wait(barrier, len(peers))


def _swap_halves(parts, collective_id, name):
    n = len(parts)

    def body(*refs):
        ins, outs = refs[:n], refs[n:2 * n]
        send, recv = refs[2 * n:]
        x, y, c, _ = _place()
        _handshake([(x, y, 1 - c)])
        cps = [pltpu.make_async_remote_copy(ins[b].at[:, pl.ds(1 - c, 1)], outs[b], send.at[b], recv.at[b],
                                            device_id=(x, y, 1 - c), device_id_type=_MESH) for b in range(n)]
        for cp in cps:
            cp.start()
        for cp in cps:
            cp.wait()

    return _sequencer(body, [jax.ShapeDtypeStruct((N_SHARDS, 1) + p.shape[2:], p.dtype) for p in parts],
                      _dma_sems(n, n), collective_id, name, parts)


def _by_shape(fn, first, second, scalar):
    out, groups = [None] * len(first), {}
    for i, p in enumerate(first):
        groups.setdefault(p.shape, []).append(i)
    for idx in groups.values():
        for i, r in zip(idx, fn([first[i] for i in idx], [second[i] for i in idx], scalar)):
            out[i] = r
    return out


def _add_half(parts, others, core):
    n = len(parts)
    _, _, r, c = parts[0].shape
    tr = _row_tile(r, c, 16)

    def body(core_ref, *refs):
        for q in range(n):
            refs[2 * n + q][...] = (refs[q][...].astype(F32) + refs[n + q][...].astype(F32)).astype(BF16)

    grid_spec = pltpu.PrefetchScalarGridSpec(
        num_scalar_prefetch=1, grid=(N_SHARDS, r // tr),
        in_specs=[pl.BlockSpec((None, None, tr, c), lambda k, i, cr: (k, cr[0], i, 0))] * n
        + [pl.BlockSpec((None, None, tr, c), lambda k, i, cr: (k, 0, i, 0))] * n,
        out_specs=[pl.BlockSpec((None, tr, c), lambda k, i, cr: (k, i, 0))] * n)
    return pl.pallas_call(body, out_shape=[jax.ShapeDtypeStruct((N_SHARDS, r, c), BF16)] * n, grid_spec=grid_spec,
                          name="grad_add_half", compiler_params=_params(("parallel", "parallel")))(core, *parts, *others)


def _scatter_chips(parts, collective_id, name):
    n = len(parts)

    def body(*refs):
        ins, outs = refs[:n], refs[n:2 * n]
        send, recv = refs[2 * n:]
        x, y, c, chips = _place()
        k = 2 * x + y
        _handshake([(px, py, c) for px, py in chips])
        started = []
        for b in range(n):
            for j, (px, py) in enumerate(chips):
                cp = pltpu.make_async_remote_copy(ins[b].at[2 * px + py], outs[b].at[k], send.at[3 * b + j],
                                                  recv.at[3 * b + j], device_id=(px, py, c), device_id_type=_MESH)
                cp.start()
                started.append(cp)
        for b in range(n):
            for j, (px, py) in enumerate(chips):
                got = outs[b].at[2 * px + py]
                pltpu.make_async_remote_copy(got, got, send.at[3 * b + j], recv.at[3 * b + j],
                                             device_id=(px, py, c), device_id_type=_MESH).wait_recv()
        for cp in started:
            cp.wait_send()

    return _sequencer(body, [jax.ShapeDtypeStruct(p.shape, p.dtype) for p in parts], _dma_sems(3 * n, 3 * n),
                      collective_id, name, parts)


def _sum_slots(slots, mine, chip):
    n = len(slots)
    _, r, c = slots[0].shape
    tr = _row_tile(r, c, 16)

    def body(chip_ref, *refs):
        for q in range(n):
            own = refs[5 * q + 4][...].astype(F32)
            v = [jnp.where(chip_ref[0] == s, own, refs[5 * q + s][...].astype(F32)) for s in range(N_SHARDS)]
            refs[5 * n + q][...] = ((v[0] + v[1]) + v[2]) + v[3]

    def slot_spec(s):
        return pl.BlockSpec((None, tr, c), lambda i, kr: (jnp.where(kr[0] == s, (s + 1) % N_SHARDS, s), i, 0))

    per_buffer = [slot_spec(s) for s in range(N_SHARDS)] + [pl.BlockSpec((None, tr, c), lambda i, kr: (kr[0], i, 0))]
    grid_spec = pltpu.PrefetchScalarGridSpec(num_scalar_prefetch=1, grid=(r // tr,), in_specs=per_buffer * n,
                                             out_specs=[pl.BlockSpec((tr, c), lambda i, kr: (i, 0))] * n)
    args = [a for sl, mn in zip(slots, mine) for a in (sl, sl, sl, sl, mn)]
    return pl.pallas_call(body, out_shape=[jax.ShapeDtypeStruct((r, c), F32)] * n, grid_spec=grid_spec,
                          name="grad_sum_slots", compiler_params=_params(("parallel",)))(chip, *args)


def _join_halves(halves, collective_id, name):
    n = len(halves)

    def body(*refs):
        ins, outs = refs[:n], refs[n:2 * n]
        send, recv = refs[2 * n:]
        x, y, c, _ = _place()
        _handshake([(x, y, 1 - c)])
        cps = [pltpu.make_async_remote_copy(ins[b], outs[b], send.at[b], recv.at[b], device_id=(x, y, 1 - c),
                                            device_id_type=_MESH) for b in range(n)]
        for cp in cps:
            cp.start()
        for cp in cps:
            cp.wait()

    return _sequencer(body, [jax.ShapeDtypeStruct(h.shape, F32) for h in halves], _dma_sems(n, n), collective_id,
                      name, halves)


def _gather_rows(buf, start, rows, collective_id):
    def body(in_ref, out_ref, send, recv, lsem):
        x, y, c, chips = _place()
        k = 2 * x + y
        _handshake([(px, py, c) for px, py in chips] + [(x, y, 1 - c)])
        src = in_ref.at[pl.ds(start, rows)]
        local = pltpu.make_async_remote_copy(src, out_ref.at[k], lsem.at[0], lsem.at[1], device_id=(x, y, 1 - c),
                                             device_id_type=_MESH)
        local.start()
        cps = [pltpu.make_async_remote_copy(src, out_ref.at[k], send.at[j], recv.at[j], device_id=(px, py, c),
                                            device_id_type=_MESH) for j, (px, py) in enumerate(chips)]
        for cp in cps:
            cp.start()
        for j, (px, py) in enumerate(chips):
            got = out_ref.at[2 * px + py]
            pltpu.make_async_remote_copy(got, got, send.at[j], recv.at[j], device_id=(px, py, c),
                                         device_id_type=_MESH).wait_recv()
        for cp in cps:
            cp.wait_send()
        local.wait()

    return _sequencer(body, jax.ShapeDtypeStruct((N_SHARDS, rows, buf.shape[1]), F32), _dma_sems(3, 3, 2), collective_id,
                      "gather_replicated_grads", [buf])


def _all_sum(vec):
    r, c = vec.shape
    n_dev = 2 * N_SHARDS

    def body(in_ref, out_ref, slots, send, recv):
        x, y, cc, _ = _place()
        flip = lambda v, bit: 1 - v if bit else v
        peers = [(flip(x, (q >> 2) & 1), flip(y, (q >> 1) & 1), flip(cc, q & 1)) for q in range(1, n_dev)]
        index = lambda p: 4 * p[0] + 2 * p[1] + p[2]
        slots[index((x, y, cc))] = in_ref[...]
        cps = [pltpu.make_async_remote_copy(in_ref, slots.at[index((x, y, cc))], send.at[q], recv.at[q], device_id=p,
                                            device_id_type=_MESH) for q, p in enumerate(peers)]
        for cp in cps:
            cp.start()
        for q, p in enumerate(peers):
            got = slots.at[index(p)]
            pltpu.make_async_remote_copy(got, got, send.at[q], recv.at[q], device_id=p, device_id_type=_MESH).wait_recv()
        for cp in cps:
            cp.wait_send()
        acc = slots[0]
        for s in range(1, n_dev):
            acc = acc + slots[s]
        out_ref[...] = acc

    vmem = pl.BlockSpec(memory_space=pltpu.VMEM)
    return pl.pallas_call(body, out_shape=jax.ShapeDtypeStruct((r, c), F32), in_specs=[vmem], out_specs=vmem,
                          scratch_shapes=[pltpu.VMEM((n_dev, r, c), F32)] + _dma_sems(n_dev - 1, n_dev - 1),
                          name="sum_small_grads")(vec)


def _not_before(value, other):
    return lax.optimization_barrier((value, other))[0]


def _round_up(n, m):
    return -(-n // m) * m


def _pack_flat(vecs, rows, width, dtype):
    flat = jnp.concatenate([v.reshape(-1).astype(dtype) for v in vecs])
    return jnp.pad(flat, (0, rows * width - flat.size)).reshape(rows, width)


def _split_flat(flat, shapes):
    out, off = [], 0
    for s in shapes:
        n = math.prod(s)
        out.append(flat[off:off + n].reshape(s))
        off += n
    return out


def _merge_shards(arr4, axis):
    a = jnp.moveaxis(arr4, 0, axis)
    s = list(a.shape)
    return a.reshape(s[:axis] + [s[axis] * s[axis + 1]] + s[axis + 2:])


def _split_shards(full, axis):
    s = list(full.shape)
    a = full.reshape(s[:axis] + [N_SHARDS, s[axis] // N_SHARDS] + s[axis + 1:])
    return jnp.moveaxis(a, axis, 0).reshape(N_SHARDS, -1)


def _rot_cols(w):
    half = w.shape[-1] // 2
    return jnp.concatenate([-w[..., half:], w[..., :half]], axis=-1)


def _unrot_cols(dw):
    half = dw.shape[-1] // 2
    return jnp.concatenate([dw[..., half:], -dw[..., :half]], axis=-1)


def kernel(x, positions, ffn_pre_g, ffn_post_g, ffn_w_gate, ffn_w_up, ffn_w_down, mix_pre_g, mix_post_g, gmlp_w_in, gmlp_ln_g, gmlp_ln_b, gmlp_w_s, gmlp_b_s, gmlp_w_out, kv_norm_g, w_dkv, kv_a_norm_g, w_ukv, mla_w_dq, mla_q_norm_g, mla_w_uq, mla_w_o, loss_target, m_ffn_pre_g, m_ffn_post_g, m_ffn_w_gate, m_ffn_w_up, m_ffn_w_down, m_mix_pre_g, m_mix_post_g, m_gmlp_w_in, m_gmlp_ln_g, m_gmlp_ln_b, m_gmlp_w_s, m_gmlp_b_s, m_gmlp_w_out, m_kv_norm_g, m_w_dkv, m_kv_a_norm_g, m_w_ukv, m_mla_w_dq, m_mla_q_norm_g, m_mla_w_uq, m_mla_w_o, v_ffn_pre_g, v_ffn_post_g, v_ffn_w_gate, v_ffn_w_up, v_ffn_w_down, v_mix_pre_g, v_mix_post_g, v_gmlp_w_in, v_gmlp_ln_g, v_gmlp_ln_b, v_gmlp_w_s, v_gmlp_b_s, v_gmlp_w_out, v_kv_norm_g, v_w_dkv, v_kv_a_norm_g, v_w_ukv, v_mla_w_dq, v_mla_q_norm_g, v_mla_w_uq, v_mla_w_o):
    names = ["ffn_pre_g", "ffn_post_g", "ffn_w_gate", "ffn_w_up", "ffn_w_down", "mix_pre_g", "mix_post_g", "gmlp_w_in",
             "gmlp_ln_g", "gmlp_ln_b", "gmlp_w_s", "gmlp_b_s", "gmlp_w_out", "kv_norm_g", "w_dkv", "kv_a_norm_g", "w_ukv",
             "mla_w_dq", "mla_q_norm_g", "mla_w_uq", "mla_w_o"]
    env = locals()
    w = {n: env[n] for n in names}
    mom = {n: env["m_" + n] for n in names}
    var = {n: env["v_" + n] for n in names}

    bsz, seq, d = x.shape
    t = bsz * seq
    core = lax.axis_index("c").astype(jnp.int32).reshape(1)

    mats = [("gmlp_w_in", 2), ("gmlp_w_out", 1), ("w_dkv", 0), ("w_ukv", 1), ("mla_w_dq", 1), ("mla_w_uq", 2),
            ("mla_w_o", 1)]
    vecs = [("ffn_pre_g", 2), ("ffn_post_g", 2), ("gmlp_ln_g", 1), ("gmlp_ln_b", 1)]
    replicated = ["mix_pre_g", "mix_post_g", "gmlp_w_s", "gmlp_b_s", "kv_norm_g", "kv_a_norm_g", "mla_q_norm_g"]
    n_mats = sum(w[n].size for n, _ in mats)
    n_vecs = sum(w[n].size for n, _ in vecs)
    mat_rows = _round_up(-(-n_mats // PACK_WIDTH), 64)
    vec_rows = _round_up(-(-n_vecs // 128), 32)
    mat_pack = _pack_flat([w[n] for n, _ in mats], mat_rows, PACK_WIDTH, BF16).reshape(2, 2, mat_rows // 4, PACK_WIDTH)
    vec_pack = _pack_flat([w[n] for n, _ in vecs], vec_rows, 128, F32).reshape(2, 2, vec_rows // 4, 128)
    ffn_names = ("ffn_w_gate", "ffn_w_up", "ffn_w_down")

    def oriented(a, name):
        return a if name == "ffn_w_down" else jnp.swapaxes(a, 2, 3)

    lj = [(l, j) for l in range(2) for j in range(2)]
    plan = [((0, 0), (0, 1), [vec_pack], None), ((0, 0), (2,), [mat_pack], 0), ((0, 1), (0, 1, 2), [], 0),
            ((1, 0), (0, 1, 2), [], 0), ((1, 1), (0, 1, 2), [], 0)]
    ffn_w = {k: [None] * 3 for k in lj}
    landed = []
    for q, ((l, j), which, riders, after) in enumerate(plan):
        shards = [oriented(w[ffn_names[i]], ffn_names[i])[l, j].astype(BF16) for i in which]
        bufs = [s.reshape(2, 2, s.shape[0] // 4, s.shape[1]) for s in shards] + riders
        if after is not None:
            bufs = _not_before(bufs, landed[after])
        got = _all_gather(bufs, q + 1, f"gather_weights_{q}")
        landed.append(got[-1])
        for i, g, s in zip(which, got, shards):
            ffn_w[(l, j)][i] = g.reshape((N_SHARDS,) + s.shape)
        if riders and q == 0:
            vec_all = got[-1]
        if riders and q == 1:
            mat_all = got[-1]

    def unpack(packed, entries):
        flat4, off, out = packed.reshape(N_SHARDS, -1), 0, {}
        for n, ax in entries:
            out[n] = _merge_shards(flat4[:, off:off + w[n].size].reshape((N_SHARDS,) + w[n].shape), ax)
            off += w[n].size
        return out

    full = unpack(vec_all, vecs)
    ln_g, ln_b = full["gmlp_ln_g"], full["gmlp_ln_b"]
    pre_g, post_g = full["ffn_pre_g"], full["ffn_post_g"]
    w_s = w["gmlp_w_s"][0]
    bsb = w["gmlp_b_s"][0][:, :, None]
    row = lambda v: v.reshape(1, -1)

    inv_freq = ROPE_THETA ** (-jnp.arange(0, QK_ROPE, 2, dtype=F32) / QK_ROPE)
    ang = positions.astype(F32).reshape(t, 1) * inv_freq
    cos2 = jnp.concatenate([jnp.cos(ang)] * 2, axis=-1)
    sin2 = jnp.concatenate([jnp.sin(ang)] * 2, axis=-1)

    h0 = x.reshape(t, d)
    saved = {}

    def ffn_fwd(l, j, h, n, next_gs, target=None):
        wg, wu, wd = ffn_w[(l, j)]
        g, u, a = _ffn_up(n, wg, wu)
        f, h_new, *n_next = _ffn_down(a, wd, h, row(post_g[l, j]), next_gs, target)
        saved[("ffn", l, j)] = (h, n, g, u, a, f)
        return h_new, n_next

    n0 = _rms_fwd(h0, row(pre_g[0, 0]))
    h1, (n1,) = ffn_fwd(0, 0, h0, n0, row(w["mix_pre_g"][0]))

    full.update(unpack(_not_before(mat_all, h1), mats))
    w_in, w_out = full["gmlp_w_in"][0], full["gmlp_w_out"][0]
    w_c, w_kr = full["w_dkv"][:, :KV_RANK], full["w_dkv"][:, KV_RANK:]
    w_kr_rot = _rot_cols(w_kr)
    ukv = full["w_ukv"].reshape(KV_RANK, N_HEADS, 2, QK_NOPE)
    w_k, w_v = ukv[:, :, 0].reshape(KV_RANK, -1), ukv[:, :, 1].reshape(KV_RANK, -1)
    w_dq, w_o = full["mla_w_dq"][0], full["mla_w_o"][0]
    q_rank = w_dq.shape[1]
    uq = full["mla_w_uq"][0].reshape(q_rank, N_HEADS, QK_NOPE + QK_ROPE)
    w_qn = uq[:, :, :QK_NOPE].reshape(q_rank, -1)
    w_qr = uq[:, :, QK_NOPE:].transpose(1, 0, 2)
    w_qr_rot = _rot_cols(w_qr)

    zp, uv = _gmlp_in_sgu(n1, w_in, ln_g, ln_b, w_s, bsb)
    half = uv.shape[1]
    tm = min(t, ROW_TILE)
    m0, h2, n2 = _down("gmlp_out", (uv, (tm, 512), lambda i, _, k: (i, k)), (w_out, (512, d), lambda i, _, k: (k, 0)),
                       half // 512, h1, row(w["mix_post_g"][0]), row(pre_g[0, 1]), 1.0)
    h3, (n3kv, n3) = ffn_fwd(0, 1, h2, n2, jnp.stack([w["kv_norm_g"], pre_g[1, 0]]))

    def kv_epi(accs, ex):
        c_raw = accs[0]
        return [c_raw, _rms(c_raw, ex[2]), accs[1] * ex[0] + accs[2] * ex[1]]

    c_raw, c_n, k_r = _mm2d("kv_down", [(n3kv, w_c, "nn", 0), (n3kv, w_kr, "nn", 1), (n3kv, w_kr_rot, "nn", 2)],
                            [(KV_RANK, F32), (KV_RANK, BF16), (QK_ROPE, BF16)], kv_epi, [cos2, sin2],
                            [row(w["kv_a_norm_g"])])
    k_n, v_h = _mm2d("kv_up", [(c_n, w_k, "nn", 0), (c_n, w_v, "nn", 1)], [(w_k.shape[1], BF16), (w_v.shape[1], BF16)])

    h4, (n4,) = ffn_fwd(1, 0, h3, n3, row(w["mix_pre_g"][1]))
    qd, qn = _mm2d("q_down", [(n4, w_dq, "nn", 0)], [(q_rank, F32), (q_rank, BF16)],
                   lambda accs, ex: [accs[0], _rms(accs[0], ex[0])], [], [row(w["mla_q_norm_g"][0])])
    q_n = _mm2d("q_up", [(qn, w_qn, "nn", 0)], [(w_qn.shape[1], BF16)])[0]
    q_r = _q_rope(qn, w_qr, w_qr_rot, cos2, sin2)
    o, lse = _attn_fwd(q_n, q_r, k_n, v_h, k_r, seq)
    m1, h5, n5 = _down("attn_out", (o, (tm, 512), lambda i, _, k: (i, k)), (w_o, (512, d), lambda i, _, k: (k, 0)),
                       o.shape[1] // 512, h4, row(w["mix_post_g"][1]), row(pre_g[1, 1]), 1.0)
    _, (dy, loss_sum) = ffn_fwd(1, 1, h5, n5, None, loss_target.reshape(t, d))

    chip = (2 * lax.axis_index("x") + lax.axis_index("y")).astype(jnp.int32).reshape(1)
    rs = {}

    def rs_launch(gid, parts):
        rs[gid] = {"parts": parts, "others": _swap_halves(parts, 7 + gid, f"grad_swap_{gid}")}

    def rs_mid(gid, after):
        r = rs[gid]
        parts, others = _not_before((r["parts"], r["others"]), after)
        r["chip"] = _by_shape(_add_half, parts, others, core)
        r["slots"] = _scatter_chips(r["chip"], 12 + gid, f"grad_scatter_{gid}")
        return r["chip"]

    def rs_end(gid, after):
        r = rs[gid]
        slots, mine = _not_before((r["slots"], r["chip"]), after)
        r["own"] = _by_shape(_sum_slots, slots, mine, chip)
        r["recv"] = _join_halves(r["own"], 17 + gid, f"grad_join_{gid}")
        return r["own"]

    d_pre, d_post = {}, {}

    def ffn_bwd(l, j, gid, dh_out, extra=(), then=None):
        h, n, g, u, a, f = saved[("ffn", l, j)]
        wg, wu, wd = ffn_w[(l, j)]
        df, dg, du, d_post[(l, j)] = _ffn_dact(f, dh_out, row(post_g[l, j]), wd, g, u)
        dwd = _ffn_dw_down(a, df)
        dwg, dwu = _ffn_dw_in(n, dg, du)
        parts = [p.reshape(N_SHARDS, 2, p.shape[1] // 2, p.shape[2]) for p in (dwg, dwu, dwd)]
        rs_launch(gid, parts)
        dg, du = _not_before((dg, du), parts)
        res = _ffn_dn(dg, du, wg, wu, h, dh_out, [(row(pre_g[l, j]), None)] + list(extra),
                      None if then is None else (then[0], then[1], 1.0))
        if then is None:
            dh, d_pre[(l, j)], *rest = res
            return dh, rest
        dh, dm, d_pre[(l, j)], *rest = res
        return dh, [dm, rest[-1]] + rest[:-1]

    dh5, (dm1, g_mix_post1) = ffn_bwd(1, 1, 0, dy, then=(m1, row(w["mix_post_g"][1])))
    dh5, dm1 = _not_before((dh5, dm1), rs_mid(0, dh5))

    do = _mm2d("attn_out_dx", [(dm1, w_o, "nt", 0)], [(w_o.shape[0], BF16)])[0]
    g_w_o = _mm2d("attn_out_dw", [(o, dm1, "tn", 0)], [(d, BF16)])[0]
    dq_n, dk_n, dv_h, dq_c, dq_s, dk_r = _attn_bwd(q_n, q_r, k_n, v_h, k_r, do, lse, cos2, sin2, seq)
    dqn = _mm2d("q_up_dx", [(dq_n, w_qn, "nt", 0)], [(q_rank, F32)])[0]
    heads_x = ((N_HEADS, tm, QK_ROPE), lambda i, j, k: (0, i, 0))
    heads_w = ((N_HEADS, q_rank, QK_ROPE), lambda i, j, k: (0, 0, 0))
    q_row = ((tm, q_rank), lambda i, j, k: (i, 0))
    dqn = _mm("q_rope_dx", (t // tm, 1, 1), [(dq_c, *heads_x), (w_qr, *heads_w), (dq_s, *heads_x), (w_qr_rot, *heads_w)],
              [(0, 1, 0, "nt"), (2, 3, 0, "nt")], [(tm, q_rank)], [((t, q_rank), F32, *q_row)],
              lambda accs, ex: [accs[0] + ex[0]], [(dqn, *q_row)], inner=N_HEADS)[0]
    g_qn = _mm2d("q_up_dw", [(qn, dq_n, "tn", 0)], [(w_qn.shape[1], F32)])[0]
    g_qr, g_qr_rot = [g.transpose(0, 2, 1) for g in _q_rope_dw(qn, dq_c, dq_s)]
    dqd, g_q_norm = _norm_out_bwd("q_norm_bwd", qd, dqn, row(w["mla_q_norm_g"][0]), 1.0)

    def pre_norm_bwd(accs, ex):
        dx, dgain = _rms_bwd(ex[0], ex[2], accs[0])
        return [ex[1] + dx, dgain]

    dh4, g_mix_pre1 = _mm2d("q_down_dx", [(dqd, w_dq, "nt", 0)], [(d, F32)], pre_norm_bwd, [h4, dh5],
                            [row(w["mix_pre_g"][1])], sums=[(1, d)])
    g_w_dq = _mm2d("q_down_dw", [(n4, dqd, "tn", 0)], [(q_rank, BF16)])[0]

    dc_n = _mm2d("kv_up_dx", [(dk_n, w_k, "nt", 0), (dv_h, w_v, "nt", 0)], [(KV_RANK, F32)])[0]
    g_wk, g_wv = _mm2d("kv_up_dw", [(c_n, dk_n, "tn", 0), (c_n, dv_h, "tn", 1)], [(w_k.shape[1], F32), (w_v.shape[1], F32)])
    dc, g_kv_a = _norm_out_bwd("kv_a_norm_bwd", c_raw, dc_n, row(w["kv_a_norm_g"]), 1.0)
    dkr_c, dkr_s = _rope_bwd(dk_r, cos2, sin2)
    dn3kv = _mm2d("kv_down_dx", [(dc, w_c, "nt", 0), (dkr_c, w_kr, "nt", 0), (dkr_s, w_kr_rot, "nt", 0)], [(d, F32)])[0]
    g_wc, g_wkr, g_wkr_rot = _mm2d("kv_down_dw", [(n3kv, dc, "tn", 0), (n3kv, dkr_c, "tn", 1), (n3kv, dkr_s, "tn", 2)],
                                   [(KV_RANK, F32), (QK_ROPE, F32), (QK_ROPE, F32)])

    dh4 = _not_before(dh4, rs_end(0, dh4))
    dh3, (g_kv_norm,) = ffn_bwd(1, 0, 1, dh4, extra=[(row(w["kv_norm_g"]), dn3kv)])
    dh3 = _not_before(dh3, rs_mid(1, dh3))
    dh2, (dm0, g_mix_post0) = ffn_bwd(0, 1, 2, dh3, then=(m0, row(w["mix_post_g"][0])))
    dh2, dm0 = _not_before((dh2, dm0), (rs_end(1, dh2), rs_mid(2, dh2)))

    d_uv = _mm2d("gmlp_out_dx", [(dm0, w_out, "nt", 0)], [(half, F32)])[0]
    g_w_out = _mm2d("gmlp_out_dw", [(uv, dm0, "tn", 0)], [(d, BF16)])[0]
    dzp, g_ln_g, g_ln_b, g_w_s, g_b_s = _sgu_bwd(zp, d_uv, ln_g, ln_b, w_s, bsb)
    dh1, g_mix_pre0 = _mm2d("gmlp_in_dx", [(dzp, w_in, "nt", 0)], [(d, F32)], pre_norm_bwd, [h1, dh2],
                            [row(w["mix_pre_g"][0])], sums=[(1, d)])
    tk, tmw, w_cols = min(t, K_TILE), min(d, ROW_TILE), w_in.shape[1] // N_SHARDS
    g_w_in = _mm("gmlp_in_dw", (d // tmw, N_SHARDS, t // tk),
                 [(n1, (tk, tmw), lambda i, j, k: (k, i)), (dzp, (tk, w_cols), lambda i, j, k: (k, j))],
                 [(0, 1, 0, "tn")], [(tmw, w_cols)],
                 [((N_SHARDS, d, w_cols), BF16, (None, tmw, w_cols), lambda i, j, k: (j, i, 0))], lambda accs, ex: accs)[0]

    g_w_dkv = jnp.concatenate([g_wc, g_wkr + _unrot_cols(g_wkr_rot)], axis=1).astype(BF16)
    direct = {"gmlp_w_in": g_w_in, "gmlp_w_out": g_w_out, "mla_w_o": g_w_o, "mla_w_dq": g_w_dq, "w_dkv": g_w_dkv}
    direct = {n: g.reshape(N_SHARDS, -1, g.shape[-1]) for n, g in direct.items()}
    part = {
        "w_ukv": jnp.stack([g_wk.reshape(KV_RANK, N_HEADS, QK_NOPE), g_wv.reshape(KV_RANK, N_HEADS, V_DIM)],
                           axis=2).reshape(KV_RANK, -1),
        "mla_w_uq": jnp.concatenate(
            [g_qn.reshape(q_rank, N_HEADS, QK_NOPE),
             (g_qr + _unrot_cols(g_qr_rot)).transpose(1, 0, 2)],
            axis=-1).reshape(1, q_rank, -1),
        "gmlp_ln_g": g_ln_g, "gmlp_ln_b": g_ln_b,
        "mix_pre_g": jnp.concatenate([g_mix_pre0, g_mix_pre1]), "mix_post_g": jnp.concatenate([g_mix_post0, g_mix_post1]),
        "gmlp_w_s": g_w_s[None], "gmlp_b_s": g_b_s.reshape(1, GROUPS, CHUNK),
        "kv_norm_g": g_kv_norm.reshape(-1), "kv_a_norm_g": g_kv_a.reshape(-1), "mla_q_norm_g": g_q_norm,
    }

    sharded = [e for e in mats + vecs if e[0] in part]
    n_rep = sum(w[n].size for n in replicated)
    pieces = [_split_shards(part[n], ax) for n, ax in sharded]
    pieces.append(jnp.concatenate([part[n].reshape(-1) for n in replicated]).reshape(N_SHARDS, -1))
    piece_rows = [_round_up(-(-p.shape[1] // PACK_WIDTH), 16) for p in pieces]
    piece_start = [sum(piece_rows[:q]) for q in range(len(pieces))]
    rows = _round_up(sum(piece_rows), 32)
    piece_rows[-1] += rows - sum(piece_rows)
    small = jnp.concatenate(
        [jnp.pad(p, ((0, 0), (0, r * PACK_WIDTH - p.shape[1]))).astype(BF16).reshape(N_SHARDS, r, PACK_WIDTH)
         for p, r in zip(pieces, piece_rows)], axis=1).reshape(N_SHARDS, 2, rows // 2, PACK_WIDTH)

    rs_launch(3, [g.reshape(N_SHARDS, 2, g.shape[1] // 2, g.shape[2]) for g in direct.values()] + [small])
    dh1 = _not_before(dh1, rs_end(2, dh1))
    dh1 = _not_before(dh1, rs_mid(3, dh1))
    dx, _ = ffn_bwd(0, 0, 4, dh1)

    own3 = rs_end(3, dx)
    launched = rs_mid(4, (dx, own3))
    lj = [(l, j) for l in range(2) for j in range(2)]
    tiny = jnp.concatenate([d_pre[k] for k in lj] + [d_post[k] for k in lj] + [jnp.tile(loss_sum, (1, d // 128))])
    tiny = _all_sum(_not_before(tiny.reshape(-1, 128), launched)).reshape(-1, d)
    loss, tiny = tiny[-1, 0], tiny[:-1].reshape(2, 2, 2, d)
    shard_cols = d // N_SHARDS
    grads = {"ffn_pre_g": lax.dynamic_slice_in_dim(tiny[0], chip[0] * shard_cols, shard_cols, axis=2),
             "ffn_post_g": lax.dynamic_slice_in_dim(tiny[1], chip[0] * shard_cols, shard_cols, axis=2)}
    own_small, recv_small = own3[-1], rs[3]["recv"][-1]
    delta, new_m, new_v = {}, {}, {}
    for q, n in enumerate(direct):
        lead = lambda a: a.reshape((1, 1) + a.shape[-2:])
        upd = _adamw_halves(lead(w[n]), lead(mom[n]), lead(var[n]), 0, 0, rs[3]["own"][q], rs[3]["recv"][q], core, None)
        grads[n], delta[n], new_m[n], new_v[n] = [o.reshape(w[n].shape) for o in upd]
    g_small = jnp.where(core[0] == 0, jnp.concatenate([own_small, recv_small]), jnp.concatenate([recv_small, own_small]))
    g_rep = _gather_rows(g_small, piece_start[-1], piece_rows[-1], 22)
    for (n, _), start in zip(sharded, piece_start):
        grads[n] = g_small[start:start + -(-w[n].size // PACK_WIDTH)].reshape(-1)[:w[n].size].reshape(w[n].shape)
    rep_vec = g_rep.reshape(N_SHARDS, -1)[:, :n_rep // N_SHARDS].reshape(-1)
    for n, g in zip(replicated, _split_flat(rep_vec, [w[n].shape for n in replicated])):
        grads[n] = g

    for n in names:
        if n not in ffn_names and n not in delta:
            delta[n], new_m[n], new_v[n] = _adamw(w[n], grads[n], mom[n], var[n])
    chain = {n: None for n in ffn_names}

    def ffn_update(gid, l, j):
        for q, n in enumerate(ffn_names):
            chain[n] = _adamw_halves(oriented(w[n], n), oriented(mom[n], n), oriented(var[n], n), l, j,
                                     rs[gid]["own"][q], rs[gid]["recv"][q], core, chain[n])

    ffn_update(0, 1, 1)
    ffn_update(1, 1, 0)
    ffn_update(2, 0, 1)
    rs_end(4, ([delta[n] for n in delta], [chain[n] for n in ffn_names]))
    ffn_update(4, 0, 0)
    for n in ffn_names:
        grads[n], delta[n], new_m[n], new_v[n] = [oriented(o, n) for o in chain[n]]
    return (loss, dx.reshape(x.shape), *[grads[n] for n in names], *[delta[n] for n in names],
            *[new_m[n] for n in names], *[new_v[n] for n in names])
```
